```python
import jax, jax.numpy as jnp
from jax import lax
import numpy as np

D_MODEL = 2048
BATCH = 8
SEQ = 4096
DEPTH = 2

CHUNK = 64
N_META = 16
D_LRU = D_MODEL // 2
D_CONV = D_MODEL // 2
D_MIX = D_LRU + D_CONV
LRU_HEADS = 16
LRU_HEAD_DIM = D_LRU // LRU_HEADS
CONV_GROUPS = 16
LRU_CONV_W = 4
SHORT_CONV_W = 3
LRU_C = 8.0
RMS_EPS = 1e-6
SPLIT_SIZES = (D_LRU, D_LRU, D_CONV, D_CONV, D_CONV, D_CONV)
D_IN = sum(SPLIT_SIZES)
SPLIT_IDX = tuple(int(v) for v in np.cumsum(SPLIT_SIZES)[:-1])

kernel_name = "hymba_rglru_shortconv_trunk"


def rmsnorm(x, g):
    xf = x.astype(jnp.float32)
    y = xf * lax.rsqrt(jnp.mean(xf * xf, axis=-1, keepdims=True) + RMS_EPS)
    return (y * g.astype(jnp.float32)).astype(x.dtype)


def causal_depthwise_conv(x, w):
    k, c = w.shape
    return lax.conv_general_dilated(
        x, w[:, None, :].astype(x.dtype), window_strides=(1,),
        padding=((k - 1, 0),), dimension_numbers=("NWC", "WIO", "NWC"),
        feature_group_count=c)


def rg_lru(x, wr, br, wi, bi, lam):
    bsz, length, _ = x.shape
    xf = x.astype(jnp.float32)
    xh = xf.reshape(bsz, length, LRU_HEADS, LRU_HEAD_DIM)
    r = jax.nn.sigmoid(jnp.einsum("blhi,hij->blhj", xh, wr.astype(jnp.float32))
                       .reshape(bsz, length, D_LRU) + br.astype(jnp.float32))
    i = jax.nn.sigmoid(jnp.einsum("blhi,hij->blhj", xh, wi.astype(jnp.float32))
                       .reshape(bsz, length, D_LRU) + bi.astype(jnp.float32))
    log_a = -LRU_C * r * jax.nn.softplus(-lam.astype(jnp.float32))
    a = jnp.exp(log_a)
    b = jnp.sqrt(-jnp.expm1(2.0 * log_a)) * (i * xf)

    def combine(left, right):
        a1, b1 = left
        a2, b2 = right
        return a1 * a2, a2 * b1 + b2

    _, h = lax.associative_scan(combine, (a, b), axis=1)
    return h.astype(x.dtype)


def hybrid_layer(x, norm_g, w_in, conv_a_w, conv_a_b, lru_wr, lru_br, lru_wi,
                 lru_bi, lru_lambda, conv_b_w, w_out):
    h = rmsnorm(x, norm_g)
    u = jnp.einsum("bld,de->ble", h, w_in.astype(h.dtype))
    xa, ga, gate_b, gate_c, xb, gb = jnp.split(u, SPLIT_IDX, axis=-1)
    xa = causal_depthwise_conv(xa, conv_a_w) + conv_a_b.astype(xa.dtype)
    ya = rg_lru(xa, lru_wr, lru_br, lru_wi, lru_bi, lru_lambda) * jax.nn.silu(ga)
    yb = gate_b * causal_depthwise_conv(gate_c * xb, conv_b_w) * jax.nn.silu(gb)
    y = jnp.concatenate([ya, yb], axis=-1)
    return x + jnp.einsum("ble,ed->bld", y, w_out.astype(y.dtype))


def _fwd_setup_inputs(seed: int = 0) -> dict:
    key = jax.random.key(seed)
    ks = jax.random.split(key, 16)
    f32 = jnp.float32
    x = jax.random.normal(ks[0], (BATCH, SEQ, D_MODEL), f32)
    meta = jax.random.normal(ks[1], (N_META, D_MODEL), f32)
    norm_g = 1.0 + 0.01 * jax.random.normal(ks[2], (DEPTH, D_MODEL), f32)
    w_in = jax.random.normal(ks[3], (DEPTH, D_MODEL, D_IN), f32) * D_MODEL ** -0.5
    conv_a_w = jax.random.normal(ks[4], (DEPTH, LRU_CONV_W, D_LRU), f32) * LRU_CONV_W ** -0.5
    conv_a_b = 0.01 * jax.random.normal(ks[5], (DEPTH, D_LRU), f32)
    lru_wr = jax.random.normal(ks[6], (DEPTH, LRU_HEADS, LRU_HEAD_DIM, LRU_HEAD_DIM), f32) * LRU_HEAD_DIM ** -0.5
    lru_br = 0.01 * jax.random.normal(ks[7], (DEPTH, D_LRU), f32)
    lru_wi = jax.random.normal(ks[8], (DEPTH, LRU_HEADS, LRU_HEAD_DIM, LRU_HEAD_DIM), f32) * LRU_HEAD_DIM ** -0.5
    lru_bi = 0.01 * jax.random.normal(ks[9], (DEPTH, D_LRU), f32)
    a_c = jax.random.uniform(ks[10], (DEPTH, D_LRU), f32, 0.9, 0.999)
    a0 = a_c ** (1.0 / LRU_C)
    lru_lambda = jnp.log(a0) - jnp.log1p(-a0)
    conv_b_w = jax.random.normal(ks[11], (DEPTH, SHORT_CONV_W, D_CONV), f32) * SHORT_CONV_W ** -0.5
    w_out = jax.random.normal(ks[12], (DEPTH, D_MIX, D_MODEL), f32) * D_MIX ** -0.5
    final_g = 1.0 + 0.01 * jax.random.normal(ks[13], (D_MODEL,), f32)
    return {"x": x, "meta": meta, "norm_g": norm_g, "w_in": w_in,
            "conv_a_w": conv_a_w, "conv_a_b": conv_a_b, "lru_wr": lru_wr,
            "lru_br": lru_br, "lru_wi": lru_wi, "lru_bi": lru_bi,
            "lru_lambda": lru_lambda, "conv_b_w": conv_b_w, "w_out": w_out,
            "final_g": final_g}


def _fwd_reference(x, meta, norm_g, w_in, conv_a_w, conv_a_b, lru_wr, lru_br, lru_wi,
              lru_bi, lru_lambda, conv_b_w, w_out, final_g):
    bsz = x.shape[0]
    m = jnp.broadcast_to(meta.astype(x.dtype)[None], (bsz, N_META, D_MODEL))
    h = jnp.concatenate([m, x], axis=1)
    for layer in range(DEPTH):
        h = hybrid_layer(h, norm_g[layer], w_in[layer], conv_a_w[layer],
                         conv_a_b[layer], lru_wr[layer], lru_br[layer],
                         lru_wi[layer], lru_bi[layer], lru_lambda[layer],
                         conv_b_w[layer], w_out[layer])
    return rmsnorm(h[:, N_META:], final_g)


import jax as _jax
import jax.numpy as _jnp

TWIN_FORMAT = 'train_step'
FWD_PARAMS = ['x', 'meta', 'norm_g', 'w_in', 'conv_a_w', 'conv_a_b', 'lru_wr', 'lru_br', 'lru_wi', 'lru_bi', 'lru_lambda', 'conv_b_w', 'w_out', 'final_g']
TWIN_WEIGHTS = ['meta', 'norm_g', 'w_in', 'conv_a_w', 'conv_a_b', 'lru_wr', 'lru_br', 'lru_wi', 'lru_bi', 'lru_lambda', 'conv_b_w', 'w_out', 'final_g']
TWIN_DIFF_INPUT = 'x'
TWIN_INPUTS = ['x', 'meta', 'norm_g', 'w_in', 'conv_a_w', 'conv_a_b', 'lru_wr', 'lru_br', 'lru_wi', 'lru_bi', 'lru_lambda', 'conv_b_w', 'w_out', 'final_g', 'loss_target', 'm_meta', 'm_norm_g', 'm_w_in', 'm_conv_a_w', 'm_conv_a_b', 'm_lru_wr', 'm_lru_br', 'm_lru_wi', 'm_lru_bi', 'm_lru_lambda', 'm_conv_b_w', 'm_w_out', 'm_final_g', 'v_meta', 'v_norm_g', 'v_w_in', 'v_conv_a_w', 'v_conv_a_b', 'v_lru_wr', 'v_lru_br', 'v_lru_wi', 'v_lru_bi', 'v_lru_lambda', 'v_conv_b_w', 'v_w_out', 'v_final_g']
TWIN_OUTPUTS = ['loss', 'grad_x', 'grad_meta', 'grad_norm_g', 'grad_w_in', 'grad_conv_a_w', 'grad_conv_a_b', 'grad_lru_wr', 'grad_lru_br', 'grad_lru_wi', 'grad_lru_bi', 'grad_lru_lambda', 'grad_conv_b_w', 'grad_w_out', 'grad_final_g', 'delta_meta', 'delta_norm_g', 'delta_w_in', 'delta_conv_a_w', 'delta_conv_a_b', 'delta_lru_wr', 'delta_lru_br', 'delta_lru_wi', 'delta_lru_bi', 'delta_lru_lambda', 'delta_conv_b_w', 'delta_w_out', 'delta_final_g', 'new_m_meta', 'new_m_norm_g', 'new_m_w_in', 'new_m_conv_a_w', 'new_m_conv_a_b', 'new_m_lru_wr', 'new_m_lru_br', 'new_m_lru_wi', 'new_m_lru_bi', 'new_m_lru_lambda', 'new_m_conv_b_w', 'new_m_w_out', 'new_m_final_g', 'new_v_meta', 'new_v_norm_g', 'new_v_w_in', 'new_v_conv_a_w', 'new_v_conv_a_b', 'new_v_lru_wr', 'new_v_lru_br', 'new_v_lru_wi', 'new_v_lru_bi', 'new_v_lru_lambda', 'new_v_conv_b_w', 'new_v_w_out', 'new_v_final_g']
TWIN_LEAF_KINDS = {'loss': 'loss', 'grad_x': 'grad_x', 'grad_meta': 'grad_w', 'grad_norm_g': 'grad_w', 'grad_w_in': 'grad_w', 'grad_conv_a_w': 'grad_w', 'grad_conv_a_b': 'grad_w', 'grad_lru_wr': 'grad_w', 'grad_lru_br': 'grad_w', 'grad_lru_wi': 'grad_w', 'grad_lru_bi': 'grad_w', 'grad_lru_lambda': 'grad_w', 'grad_conv_b_w': 'grad_w', 'grad_w_out': 'grad_w', 'grad_final_g': 'grad_w', 'delta_meta': 'delta_w', 'delta_norm_g': 'delta_w', 'delta_w_in': 'delta_w', 'delta_conv_a_w': 'delta_w', 'delta_conv_a_b': 'delta_w', 'delta_lru_wr': 'delta_w', 'delta_lru_br': 'delta_w', 'delta_lru_wi': 'delta_w', 'delta_lru_bi': 'delta_w', 'delta_lru_lambda': 'delta_w', 'delta_conv_b_w': 'delta_w', 'delta_w_out': 'delta_w', 'delta_final_g': 'delta_w', 'new_m_meta': 'new_m', 'new_m_norm_g': 'new_m', 'new_m_w_in': 'new_m', 'new_m_conv_a_w': 'new_m', 'new_m_conv_a_b': 'new_m', 'new_m_lru_wr': 'new_m', 'new_m_lru_br': 'new_m', 'new_m_lru_wi': 'new_m', 'new_m_lru_bi': 'new_m', 'new_m_lru_lambda': 'new_m', 'new_m_conv_b_w': 'new_m', 'new_m_w_out': 'new_m', 'new_m_final_g': 'new_m', 'new_v_meta': 'new_v', 'new_v_norm_g': 'new_v', 'new_v_w_in': 'new_v', 'new_v_conv_a_w': 'new_v', 'new_v_conv_a_b': 'new_v', 'new_v_lru_wr': 'new_v', 'new_v_lru_br': 'new_v', 'new_v_lru_wi': 'new_v', 'new_v_lru_bi': 'new_v', 'new_v_lru_lambda': 'new_v', 'new_v_conv_b_w': 'new_v', 'new_v_w_out': 'new_v', 'new_v_final_g': 'new_v'}


def _forward(args):
    return _fwd_reference(*[args[k] for k in FWD_PARAMS])


def _output_shape():
    def fwd():
        inp = _fwd_setup_inputs(0)
        return _fwd_reference(*[inp[k] for k in FWD_PARAMS])
    out = _jax.eval_shape(fwd)
    return out.shape, out.dtype

N_MICROBATCH = 1
ADAM_LR = 0.001
ADAM_B1 = 0.9
ADAM_B2 = 0.999
ADAM_EPS = 1e-08
ADAM_WD = 0.01
ADAM_STEP = 10
PER_EXAMPLE_BATCH_AXIS = {'x': 0, 'loss_target': 0}
SHARED_INPUTS = []
_WEIGHT_DTYPES = {'meta': _jnp.float32, 'norm_g': _jnp.float32, 'w_in': _jnp.float32, 'conv_a_w': _jnp.float32, 'conv_a_b': _jnp.float32, 'lru_wr': _jnp.float32, 'lru_br': _jnp.float32, 'lru_wi': _jnp.float32, 'lru_bi': _jnp.float32, 'lru_lambda': _jnp.float32, 'conv_b_w': _jnp.float32, 'w_out': _jnp.float32, 'final_g': _jnp.float32}
MOMENT_SCALE = {'meta': 2.221025e-03, 'norm_g': 7.992477e-02, 'w_in': 4.520475e-02, 'conv_a_w': 3.938800e-02, 'conv_a_b': 3.931354e-01, 'lru_wr': 1.527207e-02, 'lru_br': 9.347456e-03, 'lru_wi': 2.823442e-02, 'lru_bi': 1.413601e-02, 'lru_lambda': 1.878242e-02, 'conv_b_w': 5.041531e-02, 'w_out': 4.456015e-02, 'final_g': 1.599501e+01}


def _to_microbatches(a, axis):
    t = _jnp.moveaxis(a, axis, 0)
    t = t.reshape((N_MICROBATCH, t.shape[0] // N_MICROBATCH) + t.shape[1:])
    return _jnp.moveaxis(t, 1, axis + 1)


def setup_inputs(seed: int = 0) -> dict:
    inp = _fwd_setup_inputs(seed)
    key = _jax.random.fold_in(_jax.random.key(seed), 7919)
    shape, _ = _output_shape()
    out = dict(inp)
    out["loss_target"] = _jax.random.normal(_jax.random.fold_in(key, 0), shape, _jnp.float32)
    for i, name in enumerate(TWIN_WEIGHTS):
        w = inp[name].astype(_jnp.float32)
        if MOMENT_SCALE is None:
            s = _jnp.sqrt(_jnp.mean(_jnp.square(w)) + 1e-30)
        else:
            s = MOMENT_SCALE[name]
        km, kv = _jax.random.split(_jax.random.fold_in(key, i + 1))
        out[name] = w
        out["m_" + name] = s * _jax.random.normal(km, w.shape, _jnp.float32)
        out["v_" + name] = (s * s) * _jax.random.uniform(kv, w.shape, _jnp.float32, 0.5, 1.5)
    if N_MICROBATCH > 1:
        for name, axis in PER_EXAMPLE_BATCH_AXIS.items():
            out[name] = _to_microbatches(out[name], axis)
    return {'x': out['x'], 'meta': out['meta'], 'norm_g': out['norm_g'], 'w_in': out['w_in'], 'conv_a_w': out['conv_a_w'], 'conv_a_b': out['conv_a_b'], 'lru_wr': out['lru_wr'], 'lru_br': out['lru_br'], 'lru_wi': out['lru_wi'], 'lru_bi': out['lru_bi'], 'lru_lambda': out['lru_lambda'], 'conv_b_w': out['conv_b_w'], 'w_out': out['w_out'], 'final_g': out['final_g'], 'loss_target': out['loss_target'], 'm_meta': out['m_meta'], 'm_norm_g': out['m_norm_g'], 'm_w_in': out['m_w_in'], 'm_conv_a_w': out['m_conv_a_w'], 'm_conv_a_b': out['m_conv_a_b'], 'm_lru_wr': out['m_lru_wr'], 'm_lru_br': out['m_lru_br'], 'm_lru_wi': out['m_lru_wi'], 'm_lru_bi': out['m_lru_bi'], 'm_lru_lambda': out['m_lru_lambda'], 'm_conv_b_w': out['m_conv_b_w'], 'm_w_out': out['m_w_out'], 'm_final_g': out['m_final_g'], 'v_meta': out['v_meta'], 'v_norm_g': out['v_norm_g'], 'v_w_in': out['v_w_in'], 'v_conv_a_w': out['v_conv_a_w'], 'v_conv_a_b': out['v_conv_a_b'], 'v_lru_wr': out['v_lru_wr'], 'v_lru_br': out['v_lru_br'], 'v_lru_wi': out['v_lru_wi'], 'v_lru_bi': out['v_lru_bi'], 'v_lru_lambda': out['v_lru_lambda'], 'v_conv_b_w': out['v_conv_b_w'], 'v_w_out': out['v_w_out'], 'v_final_g': out['v_final_g']}


def _loss(weights, diff, rest, loss_target):
    with _jax.named_scope("forward"):
        args = {**rest, TWIN_DIFF_INPUT: diff, **{k: w.astype(_WEIGHT_DTYPES[k]) for k, w in weights.items()}}
        y = _forward(args)
    with _jax.named_scope("loss_head"):
        err = _jnp.square(y.astype(_jnp.float32) - loss_target)
        return 0.5 * _jnp.sum(_jnp.mean(err, axis=-1)) if err.ndim else 0.5 * err


def _adamw(w, g, m, v):
    m = ADAM_B1 * m + (1.0 - ADAM_B1) * g
    v = ADAM_B2 * v + (1.0 - ADAM_B2) * _jnp.square(g)
    m_hat = m / (1.0 - ADAM_B1 ** ADAM_STEP)
    v_hat = v / (1.0 - ADAM_B2 ** ADAM_STEP)
    delta = -ADAM_LR * (m_hat / (_jnp.sqrt(v_hat) + ADAM_EPS) + ADAM_WD * w)
    return delta, m, v


def reference(x, meta, norm_g, w_in, conv_a_w, conv_a_b, lru_wr, lru_br, lru_wi, lru_bi, lru_lambda, conv_b_w, w_out, final_g, loss_target, m_meta, m_norm_g, m_w_in, m_conv_a_w, m_conv_a_b, m_lru_wr, m_lru_br, m_lru_wi, m_lru_bi, m_lru_lambda, m_conv_b_w, m_w_out, m_final_g, v_meta, v_norm_g, v_w_in, v_conv_a_w, v_conv_a_b, v_lru_wr, v_lru_br, v_lru_wi, v_lru_bi, v_lru_lambda, v_conv_b_w, v_w_out, v_final_g):
    given = dict(x=x, meta=meta, norm_g=norm_g, w_in=w_in, conv_a_w=conv_a_w, conv_a_b=conv_a_b, lru_wr=lru_wr, lru_br=lru_br, lru_wi=lru_wi, lru_bi=lru_bi, lru_lambda=lru_lambda, conv_b_w=conv_b_w, w_out=w_out, final_g=final_g, loss_target=loss_target, m_meta=m_meta, m_norm_g=m_norm_g, m_w_in=m_w_in, m_conv_a_w=m_conv_a_w, m_conv_a_b=m_conv_a_b, m_lru_wr=m_lru_wr, m_lru_br=m_lru_br, m_lru_wi=m_lru_wi, m_lru_bi=m_lru_bi, m_lru_lambda=m_lru_lambda, m_conv_b_w=m_conv_b_w, m_w_out=m_w_out, m_final_g=m_final_g, v_meta=v_meta, v_norm_g=v_norm_g, v_w_in=v_w_in, v_conv_a_w=v_conv_a_w, v_conv_a_b=v_conv_a_b, v_lru_wr=v_lru_wr, v_lru_br=v_lru_br, v_lru_wi=v_lru_wi, v_lru_bi=v_lru_bi, v_lru_lambda=v_lru_lambda, v_conv_b_w=v_conv_b_w, v_w_out=v_w_out, v_final_g=v_final_g)
    weights = {n: given[n] for n in TWIN_WEIGHTS}
    shared = {n: given[n] for n in SHARED_INPUTS}
    per_example = {n: given[n] for n in ['x']}
    grad_fn = _jax.value_and_grad(_loss, argnums=(0, 1))

    def one_microbatch(ex, loss_target):
        ex = dict(ex)
        diff = ex.pop(TWIN_DIFF_INPUT)
        return grad_fn(weights, diff, {**shared, **ex}, loss_target)

    if N_MICROBATCH == 1:
        loss, (grad_w, grad_x) = one_microbatch(per_example, given["loss_target"])
    else:
        def body(carry, xs):
            loss_sum, grad_sum = carry
            l_k, (gw_k, gx_k) = one_microbatch(xs[0], xs[1])
            with _jax.named_scope("update"):
                return (loss_sum + l_k, _jax.tree.map(_jnp.add, grad_sum, gw_k)), gx_k

        init = (_jnp.zeros((), _jnp.float32), _jax.tree.map(_jnp.zeros_like, weights))
        (loss, grad_w), grad_x = _jax.lax.scan(body, init, (per_example, given["loss_target"]))
    with _jax.named_scope("update"):
        delta_w, new_m, new_v = {}, {}, {}
        for n in TWIN_WEIGHTS:
            delta_w[n], new_m[n], new_v[n] = _adamw(weights[n], grad_w[n], given["m_" + n], given["v_" + n])
    return (loss, grad_x, *[grad_w[n] for n in TWIN_WEIGHTS], *[delta_w[n] for n in TWIN_WEIGHTS],
            *[new_m[n] for n in TWIN_WEIGHTS], *[new_v[n] for n in TWIN_WEIGHTS])
```

```python
import functools

import jax
import jax.numpy as jnp
from jax import lax
from jax.experimental import pallas as pl
from jax.experimental.pallas import tpu as pltpu

F32 = jnp.float32
BF16 = jnp.bfloat16

RMS_EPS = 1e-6
LRU_C = 8.0
ADAM_LR = 0.001
ADAM_B1 = 0.9
ADAM_B2 = 0.999
ADAM_EPS = 1e-08
ADAM_WD = 0.01
ADAM_STEP = 10

N_CHIPS = 4
N_CORES = 2
VMEM_LIMIT_BYTES = 56 * 1024 * 1024
SUBLANES = 8
LANES = 128
ROW_QUANTUM = 384
MIX_CHUNK = 192
GATE_BLOCK = 256
MESH = pl.DeviceIdType.MESH
ANY = pl.BlockSpec(memory_space=pl.ANY)

NT_DIMS = (((1,), (1,)), ((), ()))
TN_DIMS = (((0,), (0,)), ((), ()))


def _params(sem):
    return pltpu.CompilerParams(dimension_semantics=sem, vmem_limit_bytes=VMEM_LIMIT_BYTES)


def _sig(x):
    return 1.0 / (1.0 + jnp.exp(-x))


def _row_tile(t):
    return 704 if t % 704 == 0 else 192


def _col_tile(n, prefs):
    for p in prefs:
        if n % p == 0:
            return p
    return n


def _norm_in(h, g, wg, name):
    t, d = h.shape
    s, _, ns = wg.shape
    tm = _row_tile(t)
    tn = _col_tile(ns, (512, 384, 128))
    nb = ns // tn

    def body(h_ref, g_ref, w_ref, u_ref, hn_ref):
        @pl.when(pl.program_id(1) == 0)
        def _():
            x = h_ref[...]
            r = lax.rsqrt(jnp.mean(x * x, axis=-1, keepdims=True) + RMS_EPS)
            hn_ref[...] = ((x * r) * g_ref[...]).astype(BF16)

        u_ref[...] = jnp.dot(hn_ref[...], w_ref[...], preferred_element_type=F32)

    return pl.pallas_call(
        body, name=name, grid=(t // tm, s * nb),
        in_specs=[pl.BlockSpec((tm, d), lambda i, n: (i, 0)),
                  pl.BlockSpec((1, d), lambda i, n: (0, 0)),
                  pl.BlockSpec((None, d, tn), lambda i, n: (n // nb, 0, n % nb))],
        out_specs=[pl.BlockSpec((tm, tn), lambda i, n: (i, n)),
                   pl.BlockSpec((tm, d), lambda i, n: (i, 0))],
        out_shape=[jax.ShapeDtypeStruct((t, s * ns), F32), jax.ShapeDtypeStruct((t, d), BF16)],
        compiler_params=_params(("arbitrary", "arbitrary")),
    )(h, g, wg)


def _decay_consts(lam):
    z = -lam
    e = jnp.exp(-jnp.abs(z))
    u = 1.0 + e
    log1p_e = jnp.where(u == 1.0, e, jnp.log(u) * (e / (u - 1.0)))
    sp = jnp.maximum(z, 0.0) + log1p_e
    return -LRU_C * sp, LRU_C * _sig(z)


def _gates(xc, wr_ref, br_ref, wi_ref, bi_ref, c8, j, gb):
    sl = slice(j * gb, (j + 1) * gb)
    x16 = xc.astype(BF16)
    r = _sig(jnp.dot(x16, wr_ref[j], preferred_element_type=F32) + br_ref[:, sl])
    ig = _sig(jnp.dot(x16, wi_ref[j], preferred_element_type=F32) + bi_ref[:, sl])
    la = c8[:, sl] * r
    a = jnp.exp(la)
    sq = jnp.sqrt(-jnp.tanh(la) * (a * a + 1.0))
    return r, ig, a, sq


def _mix_fwd(u, wa, ba, wr, br, wi, bi, lam, wb, name):
    t = u.shape[0]
    c = u.shape[1] // 6
    tc = MIX_CHUNK
    gb = wr.shape[1]
    nblk = c // gb
    ka, kb = wa.shape[0], wb.shape[0]

    def body(u_ref, wa_ref, ba_ref, wr_ref, br_ref, wi_ref, bi_ref, lam_ref, wb_ref,
             y_ref, hs_ref, xa_ext, v_ext, xc_s, a_s, b_s, carry_s):
        @pl.when(pl.program_id(0) == 0)
        def _():
            xa_ext[0:SUBLANES, :] = jnp.zeros((SUBLANES, c), F32)
            v_ext[0:SUBLANES, :] = jnp.zeros((SUBLANES, c), F32)
            carry_s[...] = jnp.zeros_like(carry_s)

        xa_ext[SUBLANES:SUBLANES + tc, :] = u_ref[:, 0:c]
        xc = ba_ref[...]
        for k in range(ka):
            xc = xc + wa_ref[pl.ds(k, 1), :] * xa_ext[pl.ds(SUBLANES - (ka - 1) + k, tc), :]
        xc_s[...] = xc
        c8, _ = _decay_consts(lam_ref[...])
        for j in range(nblk):
            sl = slice(j * gb, (j + 1) * gb)
            xcj = xc_s[:, sl]
            _, ig, a, sq = _gates(xcj, wr_ref, br_ref, wi_ref, bi_ref, c8, j, gb)
            a_s[:, sl] = a
            b_s[:, sl] = sq * (ig * xcj)

        row = lax.broadcasted_iota(jnp.int32, (SUBLANES, c), 0)

        def scan_step(j, _):
            off = pl.multiple_of(j * SUBLANES, SUBLANES)
            av = a_s[pl.ds(off, SUBLANES), :]
            bv = b_s[pl.ds(off, SUBLANES), :]
            for d in (1, 2, 4):
                keep = row >= d
                bsh = jnp.where(keep, pltpu.roll(bv, d, axis=0), 0.0)
                ash = jnp.where(keep, pltpu.roll(av, d, axis=0), 1.0)
                bv = av * bsh + bv
                av = av * ash
            hv = av * carry_s[...] + bv
            hs_ref[pl.ds(off, SUBLANES), :] = hv
            carry_s[...] = hs_ref[pl.ds(off + SUBLANES - 1, 1), :]
            return 0

        lax.fori_loop(0, tc // SUBLANES, scan_step, 0)

        ga = u_ref[:, c:2 * c]
        y_ref[:, 0:c] = (hs_ref[...] * (ga * _sig(ga))).astype(BF16)

        v_ext[SUBLANES:SUBLANES + tc, :] = u_ref[:, 3 * c:4 * c] * u_ref[:, 4 * c:5 * c]
        cv = wb_ref[pl.ds(0, 1), :] * v_ext[pl.ds(SUBLANES - (kb - 1), tc), :]
        for k in range(1, kb):
            cv = cv + wb_ref[pl.ds(k, 1), :] * v_ext[pl.ds(SUBLANES - (kb - 1) + k, tc), :]
        gbv = u_ref[:, 5 * c:6 * c]
        y_ref[:, c:2 * c] = (u_ref[:, 2 * c:3 * c] * cv * (gbv * _sig(gbv))).astype(BF16)

        xa_ext[0:SUBLANES, :] = xa_ext[tc:tc + SUBLANES, :]
        v_ext[0:SUBLANES, :] = v_ext[tc:tc + SUBLANES, :]

    full = lambda shape: pl.BlockSpec(shape, lambda i: (0,) * len(shape))
    return pl.pallas_call(
        body, name=name, grid=(t // tc,),
        in_specs=[pl.BlockSpec((tc, 6 * c), lambda i: (i, 0)),
                  full(wa.shape), full(ba.shape), full(wr.shape), full(br.shape),
                  full(wi.shape), full(bi.shape), full(lam.shape), full(wb.shape)],
        out_specs=[pl.BlockSpec((tc, 2 * c), lambda i: (i, 0)),
                   pl.BlockSpec((tc, c), lambda i: (i, 0))],
        out_shape=[jax.ShapeDtypeStruct((t, 2 * c), BF16), jax.ShapeDtypeStruct((t, c), F32)],
        scratch_shapes=[pltpu.VMEM((tc + SUBLANES, c), F32), pltpu.VMEM((tc + SUBLANES, c), F32),
                        pltpu.VMEM((tc, c), F32), pltpu.VMEM((tc, c), F32), pltpu.VMEM((tc, c), F32),
                        pltpu.VMEM((1, c), F32)],
        compiler_params=_params(("arbitrary",)),
    )(u, wa, ba, wr, br, wi, bi, lam, wb)


ROW_DWA = 0
ROW_DBA = 4
ROW_DBR = 5
ROW_DBI = 6
ROW_DLAM = 7
ROW_DWB = 8
SMALL_ROWS = 16


def _mix_bwd(u, hs, dy, wa, ba, wr, br, wi, bi, lam, wb, name):
    t = u.shape[0]
    c = u.shape[1] // 6
    tc = MIX_CHUNK
    nt = t // tc
    gb = wr.shape[1]
    nblk = c // gb
    ka, kb = wa.shape[0], wb.shape[0]
    assert ka <= ROW_DBA and kb <= SMALL_ROWS - ROW_DWB
    hb = tc // SUBLANES

    def body(u_ref, uh_ref, hs_ref, hsh_ref, dy_ref, wa_ref, ba_ref, wr_ref, br_ref, wi_ref, bi_ref, lam_ref, wb_ref,
             du_ref, dsm_ref, dwr_ref, dwi_ref,
             xa_ext, v_ext, hs_ext, a_ext, ds_ext, dxc_ext, dcv_ext, xc_s, r_s, i_s, sq_s, g_s, an_s):
        i = pl.program_id(0)
        chunk = nt - 1 - i
        tail = slice(tc, tc + SUBLANES)
        head = slice(0, SUBLANES)

        @pl.when(i == 0)
        def _():
            zero = jnp.zeros((SUBLANES, c), F32)
            a_ext[tail, :] = zero
            ds_ext[tail, :] = zero
            dxc_ext[tail, :] = zero
            dcv_ext[tail, :] = zero
            dsm_ref[...] = jnp.zeros_like(dsm_ref)
            dwr_ref[...] = jnp.zeros_like(dwr_ref)
            dwi_ref[...] = jnp.zeros_like(dwi_ref)

        prev = jnp.where(chunk > 0, 1.0, 0.0)
        xa_ext[head, :] = uh_ref[:, 0:c] * prev
        xa_ext[SUBLANES:SUBLANES + tc, :] = u_ref[:, 0:c]
        v_ext[head, :] = uh_ref[:, 3 * c:4 * c] * uh_ref[:, 4 * c:5 * c] * prev
        v_ext[SUBLANES:SUBLANES + tc, :] = u_ref[:, 3 * c:4 * c] * u_ref[:, 4 * c:5 * c]
        hs_ext[head, :] = hsh_ref[...] * prev
        hs_ext[SUBLANES:SUBLANES + tc, :] = hs_ref[...]

        xc = ba_ref[...]
        for k in range(ka):
            xc = xc + wa_ref[pl.ds(k, 1), :] * xa_ext[pl.ds(SUBLANES - (ka - 1) + k, tc), :]
        xc_s[...] = xc
        c8, dc8 = _decay_consts(lam_ref[...])
        for j in range(nblk):
            sl = slice(j * gb, (j + 1) * gb)
            r, ig, a, sq = _gates(xc_s[:, sl], wr_ref, br_ref, wi_ref, bi_ref, c8, j, gb)
            r_s[:, sl] = r
            i_s[:, sl] = ig
            sq_s[:, sl] = sq
            a_ext[0:tc, sl] = a

        ga = u_ref[:, c:2 * c]
        sga = _sig(ga)
        g_s[...] = dy_ref[:, 0:c] * (ga * sga)
        an_s[...] = a_ext[pl.ds(1, tc), :]

        row = lax.broadcasted_iota(jnp.int32, (SUBLANES, c), 0)

        def scan_step(j, _):
            off = pl.multiple_of(tc - SUBLANES - j * SUBLANES, SUBLANES)
            av = an_s[pl.ds(off, SUBLANES), :]
            bv = g_s[pl.ds(off, SUBLANES), :]
            for d in (1, 2, 4):
                keep = row < SUBLANES - d
                bsh = jnp.where(keep, pltpu.roll(bv, SUBLANES - d, axis=0), 0.0)
                ash = jnp.where(keep, pltpu.roll(av, SUBLANES - d, axis=0), 1.0)
                bv = av * bsh + bv
                av = av * ash
            ds_ext[pl.ds(off, SUBLANES), :] = av * ds_ext[pl.ds(off + SUBLANES, 1), :] + bv
            return 0

        lax.fori_loop(0, tc // SUBLANES, scan_step, 0)

        def acc(row_index, val):
            dsm_ref[pl.ds(row_index, 1), :] += jnp.sum(val, axis=0, keepdims=True)

        def acc_block(row_index, sl, val):
            dsm_ref[pl.ds(row_index, 1), sl] += jnp.sum(val, axis=0, keepdims=True)

        for j in range(nblk):
            sl = slice(j * gb, (j + 1) * gb)
            ds = ds_ext[0:tc, sl]
            hprev = hs_ext[pl.ds(SUBLANES - 1, tc), sl]
            a = a_ext[0:tc, sl]
            sq = sq_s[:, sl]
            ig = i_s[:, sl]
            r = r_s[:, sl]
            xcj = xc_s[:, sl]
            t1 = ds * xcj
            dla = (ds * hprev) * a - (t1 * ig) * ((a * a) / sq)
            acc_block(ROW_DLAM, sl, dla * r)
            dpr = (dla * c8[:, sl]) * (r * (1.0 - r))
            dpi = (t1 * sq) * (ig * (1.0 - ig))
            acc_block(ROW_DBR, sl, dpr)
            acc_block(ROW_DBI, sl, dpi)
            p16 = dpr.astype(BF16)
            q16 = dpi.astype(BF16)
            x16 = xcj.astype(BF16)
            dwr_ref[j] += lax.dot_general(x16, p16, TN_DIMS, preferred_element_type=F32)
            dwi_ref[j] += lax.dot_general(x16, q16, TN_DIMS, preferred_element_type=F32)
            dxc = (ds * (sq * ig)
                   + lax.dot_general(p16, wr_ref[j], NT_DIMS, preferred_element_type=F32)
                   + lax.dot_general(q16, wi_ref[j], NT_DIMS, preferred_element_type=F32))
            dxc_ext[0:tc, sl] = dxc
            acc_block(ROW_DBA, sl, dxc)

        dsilu_a = sga * (1.0 + ga * (1.0 - sga))
        du_ref[:, c:2 * c] = (dy_ref[:, 0:c] * hs_ref[...] * dsilu_a).astype(BF16)

        dxc = dxc_ext[0:tc, :]
        dxa = wa_ref[pl.ds(ka - 1, 1), :] * dxc
        acc(ROW_DWA + ka - 1, dxc * xa_ext[SUBLANES:SUBLANES + tc, :])
        for k in range(ka - 1):
            acc(ROW_DWA + k, dxc * xa_ext[pl.ds(SUBLANES - (ka - 1) + k, tc), :])
            dxa = dxa + wa_ref[pl.ds(k, 1), :] * dxc_ext[pl.ds(ka - 1 - k, tc), :]
        du_ref[:, 0:c] = dxa.astype(BF16)

        cv = wb_ref[pl.ds(0, 1), :] * v_ext[pl.ds(SUBLANES - (kb - 1), tc), :]
        for k in range(1, kb):
            cv = cv + wb_ref[pl.ds(k, 1), :] * v_ext[pl.ds(SUBLANES - (kb - 1) + k, tc), :]
        gbv = u_ref[:, 5 * c:6 * c]
        sgb = _sig(gbv)
        silu_b = gbv * sgb
        dyb = dy_ref[:, c:2 * c]
        gB = u_ref[:, 2 * c:3 * c]
        du_ref[:, 2 * c:3 * c] = (dyb * cv * silu_b).astype(BF16)
        du_ref[:, 5 * c:6 * c] = (dyb * gB * cv * (sgb * (1.0 + gbv * (1.0 - sgb)))).astype(BF16)
        dcv = dyb * gB * silu_b
        dcv_ext[0:tc, :] = dcv
        dv = wb_ref[pl.ds(kb - 1, 1), :] * dcv
        acc(ROW_DWB + kb - 1, dcv * v_ext[SUBLANES:SUBLANES + tc, :])
        for k in range(kb - 1):
            acc(ROW_DWB + k, dcv * v_ext[pl.ds(SUBLANES - (kb - 1) + k, tc), :])
            dv = dv + wb_ref[pl.ds(k, 1), :] * dcv_ext[pl.ds(kb - 1 - k, tc), :]
        du_ref[:, 3 * c:4 * c] = (dv * u_ref[:, 4 * c:5 * c]).astype(BF16)
        du_ref[:, 4 * c:5 * c] = (dv * u_ref[:, 3 * c:4 * c]).astype(BF16)

        a_ext[tail, :] = a_ext[head, :]
        ds_ext[tail, :] = ds_ext[head, :]
        dxc_ext[tail, :] = dxc_ext[head, :]
        dcv_ext[tail, :] = dcv_ext[head, :]

        @pl.when(i == nt - 1)
        def _():
            dsm_ref[pl.ds(ROW_DLAM, 1), :] = dsm_ref[pl.ds(ROW_DLAM, 1), :] * dc8

    full = lambda shape: pl.BlockSpec(shape, lambda i: (0,) * len(shape))
    rev = lambda i: (nt - 1 - i, 0)
    halo = lambda i: (jnp.maximum((nt - 1 - i) * hb - 1, 0), 0)
    ext = pltpu.VMEM((tc + SUBLANES, c), F32)
    blk = pltpu.VMEM((tc, c), F32)
    return pl.pallas_call(
        body, name=name, grid=(nt,),
        in_specs=[pl.BlockSpec((tc, 6 * c), rev), pl.BlockSpec((SUBLANES, 6 * c), halo),
                  pl.BlockSpec((tc, c), rev), pl.BlockSpec((SUBLANES, c), halo),
                  pl.BlockSpec((tc, 2 * c), rev),
                  full(wa.shape), full(ba.shape), full(wr.shape), full(br.shape),
                  full(wi.shape), full(bi.shape), full(lam.shape), full(wb.shape)],
        out_specs=[pl.BlockSpec((tc, 6 * c), rev), full((SMALL_ROWS, c)), full(wr.shape), full(wi.shape)],
        out_shape=[jax.ShapeDtypeStruct((t, 6 * c), BF16), jax.ShapeDtypeStruct((SMALL_ROWS, c), F32),
                   jax.ShapeDtypeStruct(wr.shape, F32), jax.ShapeDtypeStruct(wi.shape, F32)],
        scratch_shapes=[ext] * 7 + [blk] * 6,
        compiler_params=_params(("arbitrary",)),
    )(u, u, hs, hs, dy, wa, ba, wr, br, wi, bi, lam, wb)


def _out_proj(h, y, w, name):
    t, d = h.shape
    dm = y.shape[1]
    tm = _row_tile(t)
    tn = _col_tile(d, (1024, 512, 256))

    def body(h_ref, y_ref, w_ref, o_ref):
        o_ref[...] = h_ref[...] + jnp.dot(y_ref[...], w_ref[...], preferred_element_type=F32)

    return pl.pallas_call(
        body, name=name, grid=(d // tn, t // tm),
        in_specs=[pl.BlockSpec((tm, tn), lambda n, i: (i, n)),
                  pl.BlockSpec((tm, dm), lambda n, i: (i, 0)),
                  pl.BlockSpec((dm, tn), lambda n, i: (0, n))],
        out_specs=pl.BlockSpec((tm, tn), lambda n, i: (i, n)),
        out_shape=jax.ShapeDtypeStruct((t, d), F32),
        compiler_params=_params(("arbitrary", "arbitrary")),
    )(h, y, w)


def _out_proj_dy(dout, w, name):
    t, d = dout.shape
    dm = w.shape[0]
    tm = _row_tile(t)
    tn = _col_tile(dm, (1024, 512, 256))

    def body(g_ref, w_ref, o_ref):
        o_ref[...] = lax.dot_general(g_ref[...].astype(BF16), w_ref[...], NT_DIMS, preferred_element_type=F32)

    return pl.pallas_call(
        body, name=name, grid=(dm // tn, t // tm),
        in_specs=[pl.BlockSpec((tm, d), lambda n, i: (i, 0)),
                  pl.BlockSpec((tn, d), lambda n, i: (n, 0))],
        out_specs=pl.BlockSpec((tm, tn), lambda n, i: (i, n)),
        out_shape=jax.ShapeDtypeStruct((t, dm), F32),
        compiler_params=_params(("arbitrary", "arbitrary")),
    )(dout, w)


def _out_proj_dw(y, dout, name):
    t, dm = y.shape
    d = dout.shape[1]
    tk = _col_tile(t, (384, 192))
    tmm = _col_tile(dm, (1024, 512, 256))

    def body(y_ref, g_ref, o_ref):
        @pl.when(pl.program_id(1) == 0)
        def _():
            o_ref[...] = jnp.zeros_like(o_ref)

        o_ref[...] += lax.dot_general(y_ref[...], g_ref[...].astype(BF16), TN_DIMS, preferred_element_type=F32)

    return pl.pallas_call(
        body, name=name, grid=(dm // tmm, t // tk),
        in_specs=[pl.BlockSpec((tk, tmm), lambda m, k: (k, m)),
                  pl.BlockSpec((tk, d), lambda m, k: (k, 0))],
        out_specs=pl.BlockSpec((tmm, d), lambda m, k: (m, 0)),
        out_shape=jax.ShapeDtypeStruct((dm, d), F32),
        compiler_params=_params(("arbitrary", "arbitrary")),
    )(y, dout)


def _in_proj_bwd(du, wg, h, g, dout, name):
    t, d = h.shape
    s, _, ns = wg.shape
    tm = _row_tile(t)
    tk = _col_tile(ns, (512, 384, 128))
    nb = ns // tk
    nk = s * nb

    def body(du_ref, w_ref, h_ref, g_ref, dout_ref, dh_ref, dg_ref, acc_ref):
        i, k = pl.program_id(0), pl.program_id(1)

        @pl.when(k == 0)
        def _():
            acc_ref[...] = jnp.zeros_like(acc_ref)

        @pl.when((k == 0) & (i == 0))
        def _():
            dg_ref[...] = jnp.zeros_like(dg_ref)

        acc_ref[...] += lax.dot_general(du_ref[...], w_ref[...], NT_DIMS, preferred_element_type=F32)

        @pl.when(k == nk - 1)
        def _():
            x = h_ref[...]
            dhn = acc_ref[...]
            r = lax.rsqrt(jnp.mean(x * x, axis=-1, keepdims=True) + RMS_EPS)
            gd = dhn * g_ref[...]
            dot = jnp.mean(gd * x, axis=-1, keepdims=True)
            dh_ref[...] = dout_ref[...] + (r * gd - x * ((r * r * r) * dot))
            dg_ref[...] += jnp.sum(dhn * (x * r), axis=0, keepdims=True)

    return pl.pallas_call(
        body, name=name, grid=(t // tm, nk),
        in_specs=[pl.BlockSpec((tm, tk), lambda i, k: (i, k)),
                  pl.BlockSpec((None, d, tk), lambda i, k: (k // nb, 0, k % nb)),
                  pl.BlockSpec((tm, d), lambda i, k: (i, 0)),
                  pl.BlockSpec((1, d), lambda i, k: (0, 0)),
                  pl.BlockSpec((tm, d), lambda i, k: (i, 0))],
        out_specs=[pl.BlockSpec((tm, d), lambda i, k: (i, 0)),
                   pl.BlockSpec((1, d), lambda i, k: (0, 0))],
        out_shape=[jax.ShapeDtypeStruct((t, d), F32), jax.ShapeDtypeStruct((1, d), F32)],
        scratch_shapes=[pltpu.VMEM((tm, d), F32)],
        compiler_params=_params(("arbitrary", "arbitrary")),
    )(du, wg, h, g, dout)


def _in_proj_dw(hn, du, s, name):
    t, d = hn.shape
    ns = du.shape[1] // s
    tk = _col_tile(t, (384, 192))
    tn = _col_tile(ns, (768, 384, 128))
    nb = ns // tn

    def body(hn_ref, du_ref, o_ref):
        @pl.when(pl.program_id(1) == 0)
        def _():
            o_ref[...] = jnp.zeros_like(o_ref)

        o_ref[...] += lax.dot_general(hn_ref[...], du_ref[...], TN_DIMS, preferred_element_type=F32)

    return pl.pallas_call(
        body, name=name, grid=(s * nb, t // tk),
        in_specs=[pl.BlockSpec((tk, d), lambda n, k: (k, 0)),
                  pl.BlockSpec((tk, tn), lambda n, k: (k, n))],
        out_specs=pl.BlockSpec((None, d, tn), lambda n, k: (n // nb, 0, n % nb)),
        out_shape=jax.ShapeDtypeStruct((s, d, ns), F32),
        compiler_params=_params(("arbitrary", "arbitrary")),
    )(hn, du)


def _loss_head(h, tgt, g, n_meta, t_real, name):
    t, d = h.shape
    tm = _row_tile(t)

    def body(h_ref, t_ref, g_ref, dh_ref, loss_ref, dg_ref):
        i = pl.program_id(0)

        @pl.when(i == 0)
        def _():
            loss_ref[...] = jnp.zeros_like(loss_ref)
            dg_ref[...] = jnp.zeros_like(dg_ref)

        x = h_ref[...]
        gv = g_ref[...]
        r = lax.rsqrt(jnp.mean(x * x, axis=-1, keepdims=True) + RMS_EPS)
        xr = x * r
        rows = i * tm + lax.broadcasted_iota(jnp.int32, (tm, 1), 0)
        valid = (rows >= n_meta) & (rows < t_real)
        err = jnp.where(valid, xr * gv - t_ref[...], 0.0)
        loss_ref[...] += 0.5 * jnp.sum(jnp.mean(err * err, axis=-1, keepdims=True))
        dy = err * (1.0 / d)
        gd = dy * gv
        dot = jnp.mean(gd * x, axis=-1, keepdims=True)
        dh_ref[...] = r * gd - x * ((r * r * r) * dot)
        dg_ref[...] += jnp.sum(dy * xr, axis=0, keepdims=True)

    return pl.pallas_call(
        body, name=name, grid=(t // tm,),
        in_specs=[pl.BlockSpec((tm, d), lambda i: (i, 0)),
                  pl.BlockSpec((tm, d), lambda i: (i, 0)),
                  pl.BlockSpec((1, d), lambda i: (0, 0))],
        out_specs=[pl.BlockSpec((tm, d), lambda i: (i, 0)),
                   pl.BlockSpec((1, LANES), lambda i: (0, 0)),
                   pl.BlockSpec((1, d), lambda i: (0, 0))],
        out_shape=[jax.ShapeDtypeStruct((t, d), F32), jax.ShapeDtypeStruct((1, LANES), F32),
                   jax.ShapeDtypeStruct((1, d), F32)],
        compiler_params=_params(("arbitrary",)),
    )(h, tgt, g)


def _adamw(w, g, m, v, name):
    rows, cols = w.shape
    tr = rows
    for cand in (512, 256, 128, 64, 32, 16, 8):
        if rows % cand == 0 and cand * cols * 4 <= 2 * 1024 * 1024:
            tr = cand
            break

    def body(w_ref, g_ref, m_ref, v_ref, d_ref, nm_ref, nv_ref):
        gv = g_ref[...]
        m2 = ADAM_B1 * m_ref[...] + (1.0 - ADAM_B1) * gv
        v2 = ADAM_B2 * v_ref[...] + (1.0 - ADAM_B2) * (gv * gv)
        m_hat = m2 / (1.0 - ADAM_B1 ** ADAM_STEP)
        v_hat = v2 / (1.0 - ADAM_B2 ** ADAM_STEP)
        d_ref[...] = -ADAM_LR * (m_hat / (jnp.sqrt(v_hat) + ADAM_EPS) + ADAM_WD * w_ref[...])
        nm_ref[...] = m2
        nv_ref[...] = v2

    spec = pl.BlockSpec((tr, cols), lambda i: (i, 0))
    return pl.pallas_call(
        body, name=name, grid=(rows // tr,),
        in_specs=[spec] * 4, out_specs=[spec] * 3,
        out_shape=[jax.ShapeDtypeStruct((rows, cols), F32)] * 3,
        compiler_params=_params(("arbitrary",)),
    )(w, g, m, v)


def _pair_add(x, ra, c_idx, name):
    s, _, rows, cols = x.shape
    tr = _col_tile(rows, (256, 128, 64, 32, 16, 8))

    def body(c_ref, x_ref, r_ref, o_ref):
        o_ref[...] = x_ref[...] + r_ref[...]

    return pl.pallas_call(
        body, name=name,
        grid_spec=pltpu.PrefetchScalarGridSpec(
            num_scalar_prefetch=1, grid=(s, rows // tr),
            in_specs=[pl.BlockSpec((None, None, tr, cols), lambda a, i, c_ref: (a, c_ref[0], i, 0)),
                      pl.BlockSpec((None, tr, cols), lambda a, i, c_ref: (a, i, 0))],
            out_specs=pl.BlockSpec((None, tr, cols), lambda a, i, c_ref: (a, i, 0))),
        out_shape=jax.ShapeDtypeStruct((s, rows, cols), F32),
        compiler_params=_params(("arbitrary", "arbitrary")),
    )(c_idx, x, ra)


def _chip_sum(rc, name):
    s, rows, cols = rc.shape
    tr = _col_tile(rows, (256, 128, 64, 32, 16, 8))

    def body(x_ref, o_ref):
        total = x_ref[0]
        for a in range(1, s):
            total = total + x_ref[a]
        o_ref[...] = total

    return pl.pallas_call(
        body, name=name, grid=(rows // tr,),
        in_specs=[pl.BlockSpec((s, tr, cols), lambda i: (0, i, 0))],
        out_specs=pl.BlockSpec((tr, cols), lambda i: (i, 0)),
        out_shape=jax.ShapeDtypeStruct((rows, cols), F32),
        compiler_params=_params(("arbitrary",)),
    )(rc)


def _place():
    x, y, c = lax.axis_index("x"), lax.axis_index("y"), lax.axis_index("c")
    chips = [(1 - x, y), (x, 1 - y), (1 - x, 1 - y)]
    return x, y, c, chips


def _chip_index(cx, cy):
    return 2 * cx + cy


def _gather_weights(win, wout, small):
    nl = win.shape[0]
    assert nl == N_CORES

    def body(win_ref, wout_ref, sm_ref, wing_ref, woutg_ref, smg_ref, lsem, ssem, rsem):
        x, y, c, chips = _place()
        me = _chip_index(x, y)
        sibling = (x, y, 1 - c)

        def remote(k, src, dst, dev):
            return pltpu.make_async_remote_copy(src_ref=src, dst_ref=dst, send_sem=ssem.at[k], recv_sem=rsem.at[k],
                                                device_id=dev, device_id_type=MESH)

        local = [pltpu.make_async_copy(sm_ref, smg_ref.at[me], lsem.at[0])]
        for l in range(nl):
            local.append(pltpu.make_async_copy(win_ref.at[l], wing_ref.at[l, me], lsem.at[1 + 2 * l]))
            local.append(pltpu.make_async_copy(wout_ref.at[l], woutg_ref.at[l, me], lsem.at[2 + 2 * l]))
        for cp in local:
            cp.start()

        sends = []
        for j, chip in enumerate(chips):
            sends.append(remote(j, win_ref.at[c], wing_ref.at[c, me], (*chip, c)))
            sends.append(remote(3 + j, wout_ref.at[c], woutg_ref.at[c, me], (*chip, c)))
            sends.append(remote(6 + j, sm_ref, smg_ref.at[me], (*chip, c)))
        for cp in sends:
            cp.start()

        passed = []
        for j, chip in enumerate(chips):
            src = _chip_index(*chip)
            remote(j, win_ref.at[c], wing_ref.at[c, src], (*chip, c)).wait_recv()
            fwd = remote(9 + j, wing_ref.at[c, src], wing_ref.at[c, src], sibling)
            fwd.start()
            passed.append(fwd)
            remote(3 + j, wout_ref.at[c], woutg_ref.at[c, src], (*chip, c)).wait_recv()
            fwd = remote(12 + j, woutg_ref.at[c, src], woutg_ref.at[c, src], sibling)
            fwd.start()
            passed.append(fwd)
        for j, chip in enumerate(chips):
            src = _chip_index(*chip)
            remote(6 + j, sm_ref, smg_ref.at[src], (*chip, c)).wait_recv()
            remote(9 + j, wing_ref.at[1 - c, src], wing_ref.at[1 - c, src], sibling).wait_recv()
            remote(12 + j, woutg_ref.at[1 - c, src], woutg_ref.at[1 - c, src], sibling).wait_recv()
        for cp in sends + passed:
            cp.wait_send()
        for cp in local:
            cp.wait()

    s = N_CHIPS
    return pl.pallas_call(
        body, name="gather_weights",
        in_specs=[ANY, ANY, ANY], out_specs=[ANY, ANY, ANY],
        out_shape=[jax.ShapeDtypeStruct((nl, s) + win.shape[1:], win.dtype),
                   jax.ShapeDtypeStruct((nl, s) + wout.shape[1:], wout.dtype),
                   jax.ShapeDtypeStruct((s,) + small.shape, small.dtype)],
        scratch_shapes=[pltpu.SemaphoreType.DMA((1 + 2 * nl,)), pltpu.SemaphoreType.DMA((15,)),
                        pltpu.SemaphoreType.DMA((15,))],
    )(win, wout, small)


def _pair_swap(xs):
    n = len(xs)

    def body(*refs):
        x_refs, o_refs, ssem, rsem = refs[:n], refs[n:2 * n], refs[2 * n], refs[2 * n + 1]
        x, y, c, _ = _place()
        copies = [pltpu.make_async_remote_copy(src_ref=x_refs[a].at[:, 1 - c], dst_ref=o_refs[a],
                                               send_sem=ssem.at[a], recv_sem=rsem.at[a],
                                               device_id=(x, y, 1 - c), device_id_type=MESH) for a in range(n)]
        for cp in copies:
            cp.start()
        for cp in copies:
            cp.wait()

    return pl.pallas_call(
        body, name="pair_swap", in_specs=[ANY] * n, out_specs=[ANY] * n,
        out_shape=[jax.ShapeDtypeStruct((a.shape[0],) + a.shape[2:], a.dtype) for a in xs],
        scratch_shapes=[pltpu.SemaphoreType.DMA((n,)), pltpu.SemaphoreType.DMA((n,))],
    )(*xs)


def _chip_scatter(ps):
    n = len(ps)

    def body(*refs):
        p_refs, o_refs, lsem, ssem, rsem = refs[:n], refs[n:2 * n], refs[2 * n], refs[2 * n + 1], refs[2 * n + 2]
        x, y, c, chips = _place()
        me = _chip_index(x, y)
        local = [pltpu.make_async_copy(p_refs[a].at[me], o_refs[a].at[me], lsem.at[a]) for a in range(n)]
        for cp in local:
            cp.start()
        sends = []
        for a in range(n):
            for j, chip in enumerate(chips):
                sends.append(pltpu.make_async_remote_copy(
                    src_ref=p_refs[a].at[_chip_index(*chip)], dst_ref=o_refs[a].at[me],
                    send_sem=ssem.at[3 * a + j], recv_sem=rsem.at[3 * a + j],
                    device_id=(*chip, c), device_id_type=MESH))
        for cp in sends:
            cp.start()
        for a in range(n):
            for j, chip in enumerate(chips):
                src = _chip_index(*chip)
                pltpu.make_async_remote_copy(
                    src_ref=p_refs[a].at[src], dst_ref=o_refs[a].at[src],
                    send_sem=ssem.at[3 * a + j], recv_sem=rsem.at[3 * a + j],
                    device_id=(*chip, c), device_id_type=MESH).wait_recv()
        for cp in sends:
            cp.wait_send()
        for cp in local:
            cp.wait()

    return pl.pallas_call(
        body, name="chip_scatter", in_specs=[ANY] * n, out_specs=[ANY] * n,
        out_shape=[jax.ShapeDtypeStruct(a.shape, a.dtype) for a in ps],
        scratch_shapes=[pltpu.SemaphoreType.DMA((n,)), pltpu.SemaphoreType.DMA((3 * n,)),
                        pltpu.SemaphoreType.DMA((3 * n,))],
    )(*ps)


def _final_gather(fs, rep):
    n = len(fs)

    def body(*refs):
        f_refs, rep_ref = refs[:n], refs[n]
        o_refs, repo_ref = refs[n + 1:2 * n + 1], refs[2 * n + 1]
        lsem, ssem, rsem = refs[2 * n + 2:]
        x, y, c, chips = _place()
        slot = 4 * x + 2 * y + c
        local = [pltpu.make_async_copy(f_refs[a], o_refs[a].at[c], lsem.at[a]) for a in range(n)]
        local.append(pltpu.make_async_copy(rep_ref, repo_ref.at[slot], lsem.at[n]))
        for cp in local:
            cp.start()
        copies = [pltpu.make_async_remote_copy(src_ref=f_refs[a], dst_ref=o_refs[a].at[c],
                                               send_sem=ssem.at[a], recv_sem=rsem.at[a],
                                               device_id=(x, y, 1 - c), device_id_type=MESH) for a in range(n)]
        peers = [(x, y, 1 - c)] + [(*chip, c) for chip in chips] + [(*chip, 1 - c) for chip in chips]
        for k, peer in enumerate(peers):
            copies.append(pltpu.make_async_remote_copy(src_ref=rep_ref, dst_ref=repo_ref.at[slot],
                                                       send_sem=ssem.at[n + k], recv_sem=rsem.at[n + k],
                                                       device_id=peer, device_id_type=MESH))
        for cp in copies:
            cp.start()
        for a in range(n):
            pltpu.make_async_remote_copy(src_ref=f_refs[a], dst_ref=o_refs[a].at[1 - c],
                                         send_sem=ssem.at[a], recv_sem=rsem.at[a],
                                         device_id=(x, y, 1 - c), device_id_type=MESH).wait_recv()
        for k, peer in enumerate(peers):
            px, py, pc = peer
            pltpu.make_async_remote_copy(src_ref=rep_ref, dst_ref=repo_ref.at[4 * px + 2 * py + pc],
                                         send_sem=ssem.at[n + k], recv_sem=rsem.at[n + k],
                                         device_id=peer, device_id_type=MESH).wait_recv()
        for cp in copies:
            cp.wait_send()
        for cp in local:
            cp.wait()

    return pl.pallas_call(
        body, name="final_gather", in_specs=[ANY] * (n + 1), out_specs=[ANY] * (n + 1),
        out_shape=[jax.ShapeDtypeStruct((N_CORES,) + a.shape, a.dtype) for a in fs]
        + [jax.ShapeDtypeStruct((N_CHIPS * N_CORES,) + rep.shape, rep.dtype)],
        scratch_shapes=[pltpu.SemaphoreType.DMA((n + 1,)), pltpu.SemaphoreType.DMA((n + 7,)),
                        pltpu.SemaphoreType.DMA((n + 7,))],
    )(*fs, rep)


def _block_diag(w, gb):
    nh, hd, _ = w.shape
    per = gb // hd
    w4 = w.reshape(nh // per, per, hd, hd)
    eye = jnp.eye(per, dtype=w.dtype)
    return jnp.einsum("jaik,ab->jaibk", w4, eye).reshape(nh // per, gb, gb)


def _diag_blocks(dense, hd):
    nj, gb, _ = dense.shape
    per = gb // hd
    d5 = dense.reshape(nj, per, hd, per, hd)
    return jnp.stack([d5[:, a, :, a, :] for a in range(per)], axis=1).reshape(nj * per, hd, hd)


def _round_up(n, q):
    return (n + q - 1) // q * q


def kernel(x, meta, norm_g, w_in, conv_a_w, conv_a_b, lru_wr, lru_br, lru_wi, lru_bi, lru_lambda, conv_b_w, w_out, final_g, loss_target, m_meta, m_norm_g, m_w_in, m_conv_a_w, m_conv_a_b, m_lru_wr, m_lru_br, m_lru_wi, m_lru_bi, m_lru_lambda, m_conv_b_w, m_w_out, m_final_g, v_meta, v_norm_g, v_w_in, v_conv_a_w, v_conv_a_b, v_lru_wr, v_lru_br, v_lru_wi, v_lru_bi, v_lru_lambda, v_conv_b_w, v_w_out, v_final_g):
    weights = dict(meta=meta, norm_g=norm_g, w_in=w_in, conv_a_w=conv_a_w, conv_a_b=conv_a_b, lru_wr=lru_wr,
                   lru_br=lru_br, lru_wi=lru_wi, lru_bi=lru_bi, lru_lambda=lru_lambda, conv_b_w=conv_b_w,
                   w_out=w_out, final_g=final_g)
    mom1 = dict(meta=m_meta, norm_g=m_norm_g, w_in=m_w_in, conv_a_w=m_conv_a_w, conv_a_b=m_conv_a_b,
                lru_wr=m_lru_wr, lru_br=m_lru_br, lru_wi=m_lru_wi, lru_bi=m_lru_bi, lru_lambda=m_lru_lambda,
                conv_b_w=m_conv_b_w, w_out=m_w_out, final_g=m_final_g)
    mom2 = dict(meta=v_meta, norm_g=v_norm_g, w_in=v_w_in, conv_a_w=v_conv_a_w, conv_a_b=v_conv_a_b,
                lru_wr=v_lru_wr, lru_br=v_lru_br, lru_wi=v_lru_wi, lru_bi=v_lru_bi, lru_lambda=v_lru_lambda,
                conv_b_w=v_conv_b_w, w_out=v_w_out, final_g=v_final_g)
    names = list(weights)

    assert x.shape[0] == 1
    seq, d = x.shape[1], x.shape[2]
    n_meta, ds = meta.shape
    depth = norm_g.shape[0]
    c = lru_lambda.shape[1]
    nh, hd = lru_wr.shape[1], lru_wr.shape[2]
    ns = w_in.shape[2]
    dms = w_out.shape[1]
    cs = conv_a_w.shape[2]
    ka, kb = conv_a_w.shape[1], conv_b_w.shape[1]
    s = N_CHIPS
    assert depth == N_CORES and d == s * ds and c == s * cs and s * ns == 6 * c and s * dms == 2 * c
    gb = min(GATE_BLOCK, c)
    t_real = n_meta + seq
    t = _round_up(t_real, ROW_QUANTUM)
    c_idx = lax.axis_index("c").astype(jnp.int32).reshape(1)

    sm_rows = _round_up(n_meta + depth * SUBLANES, 2 * SUBLANES)
    small = jnp.zeros((sm_rows, ds), F32)
    small = small.at[0:n_meta, :].set(meta)
    for l in range(depth):
        base = n_meta + l * SUBLANES
        small = small.at[base:base + ka, 0:cs].set(conv_a_w[l])
        small = small.at[base + ka:base + ka + kb, 0:cs].set(conv_b_w[l])
    win_g, wout_g, small_g = _gather_weights(w_in.astype(BF16), w_out.astype(BF16), small)
    meta_full = jnp.transpose(small_g[:, 0:n_meta, :], (1, 0, 2)).reshape(n_meta, d)
    wa_full, wb_full = [], []
    for l in range(depth):
        base = n_meta + l * SUBLANES
        wa_full.append(jnp.transpose(small_g[:, base:base + ka, 0:cs], (1, 0, 2)).reshape(ka, c))
        wb_full.append(jnp.transpose(small_g[:, base + ka:base + ka + kb, 0:cs], (1, 0, 2)).reshape(kb, c))

    h = jnp.concatenate([meta_full, x[0], jnp.zeros((t - t_real, d), F32)], axis=0)
    tgt = jnp.concatenate([jnp.zeros((n_meta, d), F32), loss_target[0], jnp.zeros((t - t_real, d), F32)], axis=0)
    layer_w = []
    for l in range(depth):
        layer_w.append(dict(
            g=norm_g[l].reshape(1, d), win=win_g[l], wout=wout_g[l].reshape(2 * c, d),
            wa=wa_full[l], ba=conv_a_b[l].reshape(1, c),
            wr=_block_diag(lru_wr[l], gb).astype(BF16), br=lru_br[l].reshape(1, c),
            wi=_block_diag(lru_wi[l], gb).astype(BF16), bi=lru_bi[l].reshape(1, c),
            lam=lru_lambda[l].reshape(1, c), wb=wb_full[l]))
    saved = []
    for l, lw in enumerate(layer_w):
        u, hn = _norm_in(h, lw["g"], lw["win"], f"norm_in_{l}")
        y, hs = _mix_fwd(u, lw["wa"], lw["ba"], lw["wr"], lw["br"], lw["wi"], lw["bi"], lw["lam"], lw["wb"],
                         f"mix_fwd_{l}")
        saved.append((h, u, hn, y, hs))
        h = _out_proj(h, y, lw["wout"], f"out_proj_{l}")
    dh, loss_lanes, d_final_g = _loss_head(h, tgt, final_g.reshape(1, d), n_meta, t_real, "loss_head")
    loss = lax.psum(loss_lanes[0, 0], ("x", "y", "c"))

    grads = [None] * depth
    for l in reversed(range(depth)):
        lw = layer_w[l]
        h_in, u, hn, y, hs = saved[l]
        dy = _out_proj_dy(dh, lw["wout"], f"out_proj_dy_{l}")
        d_wout = _out_proj_dw(y, dh, f"out_proj_dw_{l}")
        du, dsm, d_wr, d_wi = _mix_bwd(u, hs, dy, lw["wa"], lw["ba"], lw["wr"], lw["br"], lw["wi"], lw["bi"],
                                       lw["lam"], lw["wb"], f"mix_bwd_{l}")
        dh, d_g = _in_proj_bwd(du, lw["win"], h_in, lw["g"], dh, f"in_proj_bwd_{l}")
        d_win = _in_proj_dw(hn, du, s, f"in_proj_dw_{l}")
        grads[l] = dict(win=d_win, wout=d_wout, dsm=dsm, wr=_diag_blocks(d_wr, hd), wi=_diag_blocks(d_wi, hd), g=d_g)
    grad_x = dh[n_meta:t_real][None]

    sharded = []
    for l in range(depth):
        sharded.append(grads[l]["win"].reshape(s, 2, d // 2, ns))
        sharded.append(grads[l]["wout"].reshape(s, 2, dms // 2, d))
    sp = jnp.zeros((sm_rows, s, ds), F32)
    sp = sp.at[0:n_meta].set(dh[0:n_meta].reshape(n_meta, s, ds))
    for l in range(depth):
        base = n_meta + l * SUBLANES
        dsm = grads[l]["dsm"]
        sp = sp.at[base:base + ka, :, 0:cs].set(dsm[ROW_DWA:ROW_DWA + ka].reshape(ka, s, cs))
        sp = sp.at[base + ka:base + ka + kb, :, 0:cs].set(dsm[ROW_DWB:ROW_DWB + kb].reshape(kb, s, cs))
    sharded.append(jnp.transpose(sp, (1, 0, 2)).reshape(s, 2, sm_rows // 2, ds))
    rep_parts = [jnp.concatenate([grads[l]["g"].reshape(-1) for l in range(depth)]), d_final_g.reshape(-1)]
    for row in (ROW_DBA, ROW_DBR, ROW_DBI, ROW_DLAM):
        rep_parts.append(jnp.concatenate([grads[l]["dsm"][row] for l in range(depth)]))
    rep_parts.append(jnp.concatenate([grads[l]["wr"].reshape(-1) for l in range(depth)]))
    rep_parts.append(jnp.concatenate([grads[l]["wi"].reshape(-1) for l in range(depth)]))
    rep_sizes = [p.shape[0] for p in rep_parts]
    piece = _round_up(-(-sum(rep_sizes) // (s * 2)), SUBLANES * LANES)
    flat = jnp.concatenate(rep_parts + [jnp.zeros((s * 2 * piece - sum(rep_sizes),), F32)])
    sharded.append(flat.reshape(s, 2, piece // LANES, LANES))

    from_sibling = _pair_swap(sharded)
    pair_sums = [_pair_add(a, b, c_idx, f"pair_add_{k}") for k, (a, b) in enumerate(zip(sharded, from_sibling))]
    by_chip = _chip_scatter(pair_sums)
    halves = [_chip_sum(a, f"chip_sum_{k}") for k, a in enumerate(by_chip)]
    *full, rep_all = _final_gather(halves[:-1], halves[-1])

    g_win = [full[2 * l].reshape(d, ns) for l in range(depth)]
    g_wout = [full[2 * l + 1].reshape(dms, d) for l in range(depth)]
    g_sp = full[2 * depth].reshape(sm_rows, ds)
    rep_flat = rep_all.reshape(-1)
    rep_out, off = [], 0
    for n in rep_sizes:
        rep_out.append(rep_flat[off:off + n])
        off += n
    grad = dict(
        meta=g_sp[0:n_meta],
        norm_g=rep_out[0].reshape(depth, d),
        w_in=jnp.stack(g_win),
        conv_a_w=jnp.stack([g_sp[n_meta + l * SUBLANES:n_meta + l * SUBLANES + ka, 0:cs] for l in range(depth)]),
        conv_a_b=rep_out[2].reshape(depth, c),
        lru_wr=rep_out[6].reshape(depth, nh, hd, hd),
        lru_br=rep_out[3].reshape(depth, c),
        lru_wi=rep_out[7].reshape(depth, nh, hd, hd),
        lru_bi=rep_out[4].reshape(depth, c),
        lru_lambda=rep_out[5].reshape(depth, c),
        conv_b_w=jnp.stack([g_sp[n_meta + l * SUBLANES + ka:n_meta + l * SUBLANES + ka + kb, 0:cs]
                            for l in range(depth)]),
        w_out=jnp.stack(g_wout),
        final_g=rep_out[1].reshape(d),
    )

    delta, new_m, new_v = {}, {}, {}
    for n in names:
        shape = weights[n].shape
        two_d = (-1, shape[-1]) if len(shape) > 1 else (1, -1)
        if n in ("lru_wr", "lru_wi"):
            two_d = (-1, LANES)
        out = _adamw(weights[n].reshape(two_d), grad[n].reshape(two_d), mom1[n].reshape(two_d),
                     mom2[n].reshape(two_d), f"adamw_{n}")
        delta[n], new_m[n], new_v[n] = (o.reshape(shape) for o in out)

    return (loss, grad_x, *[grad[n] for n in names], *[delta[n] for n in names],
            *[new_m[n] for n in names], *[new_v[n] for n in names])
```

```python
import functools

import jax
import jax.numpy as jnp
from jax import lax
from jax.experimental import pallas as pl
from jax.experimental.pallas import tpu as pltpu

F32 = jnp.float32
BF16 = jnp.bfloat16

RMS_EPS = 1e-6
LRU_C = 8.0
ADAM_LR = 0.001
ADAM_B1 = 0.9
ADAM_B2 = 0.999
ADAM_EPS = 1e-08
ADAM_WD = 0.01
ADAM_STEP = 10

N_CHIPS = 4
N_CORES = 2
VMEM_LIMIT_BYTES = 56 * 1024 * 1024
SUBLANES = 8
LANES = 128
ROW_QUANTUM = 384
MIX_CHUNK = 192
GATE_BLOCK = 256
MESH = pl.DeviceIdType.MESH
ANY = pl.BlockSpec(memory_space=pl.ANY)

NT_DIMS = (((1,), (1,)), ((), ()))
TN_DIMS = (((0,), (0,)), ((), ()))


def _params(sem):
    return pltpu.CompilerParams(dimension_semantics=sem, vmem_limit_bytes=VMEM_LIMIT_BYTES)


def _sig(x):
    return 1.0 / (1.0 + jnp.exp(-x))


def _row_tile(t):
    return 704 if t % 704 == 0 else 192


def _col_tile(n, prefs):
    for p in prefs:
        if n % p == 0:
            return p
    return n


def _norm_in(h, g, wg, name):
    t, d = h.shape
    s, _, ns = wg.shape
    tm = _row_tile(t)
    tn = _col_tile(ns, (512, 384, 128))
    nb = ns // tn

    def body(h_ref, g_ref, w_ref, u_ref, hn_ref):
        @pl.when(pl.program_id(1) == 0)
        def _():
            x = h_ref[...]
            r = lax.rsqrt(jnp.mean(x * x, axis=-1, keepdims=True) + RMS_EPS)
            hn_ref[...] = ((x * r) * g_ref[...]).astype(BF16)

        u_ref[...] = jnp.dot(hn_ref[...], w_ref[...], preferred_element_type=F32)

    return pl.pallas_call(
        body, name=name, grid=(t // tm, s * nb),
        in_specs=[pl.BlockSpec((tm, d), lambda i, n: (i, 0)),
                  pl.BlockSpec((1, d), lambda i, n: (0, 0)),
                  pl.BlockSpec((None, d, tn), lambda i, n: (n // nb, 0, n % nb))],
        out_specs=[pl.BlockSpec((tm, tn), lambda i, n: (i, n)),
                   pl.BlockSpec((tm, d), lambda i, n: (i, 0))],
        out_shape=[jax.ShapeDtypeStruct((t, s * ns), F32), jax.ShapeDtypeStruct((t, d), BF16)],
        compiler_params=_params(("arbitrary", "arbitrary")),
    )(h, g, wg)


def _decay_consts(lam):
    z = -lam
    e = jnp.exp(-jnp.abs(z))
    u = 1.0 + e
    log1p_e = jnp.where(u == 1.0, e, jnp.log(u) * (e / (u - 1.0)))
    sp = jnp.maximum(z, 0.0) + log1p_e
    return -LRU_C * sp, LRU_C * _sig(z)


def _gates(xc, wr_ref, br_ref, wi_ref, bi_ref, c8, j, gb):
    sl = slice(j * gb, (j + 1) * gb)
    x16 = xc.astype(BF16)
    r = _sig(jnp.dot(x16, wr_ref[j], preferred_element_type=F32) + br_ref[:, sl])
    ig = _sig(jnp.dot(x16, wi_ref[j], preferred_element_type=F32) + bi_ref[:, sl])
    la = c8[:, sl] * r
    a = jnp.exp(la)
    sq = jnp.sqrt(-jnp.tanh(la) * (a * a + 1.0))
    return r, ig, a, sq


def _mix_fwd(u, wa, ba, wr, br, wi, bi, lam, wb, name):
    t = u.shape[0]
    c = u.shape[1] // 6
    tc = MIX_CHUNK
    gb = wr.shape[1]
    nblk = c // gb
    ka, kb = wa.shape[0], wb.shape[0]

    def body(u_ref, wa_ref, ba_ref, wr_ref, br_ref, wi_ref, bi_ref, lam_ref, wb_ref,
             y_ref, hs_ref, xa_ext, v_ext, xc_s, a_s, b_s, carry_s):
        @pl.when(pl.program_id(0) == 0)
        def _():
            xa_ext[0:SUBLANES, :] = jnp.zeros((SUBLANES, c), F32)
            v_ext[0:SUBLANES, :] = jnp.zeros((SUBLANES, c), F32)
            carry_s[...] = jnp.zeros_like(carry_s)

        xa_ext[SUBLANES:SUBLANES + tc, :] = u_ref[:, 0:c]
        xc = ba_ref[...]
        for k in range(ka):
            xc = xc + wa_ref[pl.ds(k, 1), :] * xa_ext[pl.ds(SUBLANES - (ka - 1) + k, tc), :]
        xc_s[...] = xc
        c8, _ = _decay_consts(lam_ref[...])
        for j in range(nblk):
            sl = slice(j * gb, (j + 1) * gb)
            xcj = xc_s[:, sl]
            _, ig, a, sq = _gates(xcj, wr_ref, br_ref, wi_ref, bi_ref, c8, j, gb)
            a_s[:, sl] = a
            b_s[:, sl] = sq * (ig * xcj)

        row = lax.broadcasted_iota(jnp.int32, (SUBLANES, c), 0)

        def scan_step(j, _):
            off = pl.multiple_of(j * SUBLANES, SUBLANES)
            av = a_s[pl.ds(off, SUBLANES), :]
            bv = b_s[pl.ds(off, SUBLANES), :]
            for d in (1, 2, 4):
                keep = row >= d
                bsh = jnp.where(keep, pltpu.roll(bv, d, axis=0), 0.0)
                ash = jnp.where(keep, pltpu.roll(av, d, axis=0), 1.0)
                bv = av * bsh + bv
                av = av * ash
            hv = av * carry_s[...] + bv
            hs_ref[pl.ds(off, SUBLANES), :] = hv
            carry_s[...] = hs_ref[pl.ds(off + SUBLANES - 1, 1), :]
            return 0

        lax.fori_loop(0, tc // SUBLANES, scan_step, 0)

        ga = u_ref[:, c:2 * c]
        y_ref[:, 0:c] = (hs_ref[...] * (ga * _sig(ga))).astype(BF16)

        v_ext[SUBLANES:SUBLANES + tc, :] = u_ref[:, 3 * c:4 * c] * u_ref[:, 4 * c:5 * c]
        cv = wb_ref[pl.ds(0, 1), :] * v_ext[pl.ds(SUBLANES - (kb - 1), tc), :]
        for k in range(1, kb):
            cv = cv + wb_ref[pl.ds(k, 1), :] * v_ext[pl.ds(SUBLANES - (kb - 1) + k, tc), :]
        gbv = u_ref[:, 5 * c:6 * c]
        y_ref[:, c:2 * c] = (u_ref[:, 2 * c:3 * c] * cv * (gbv * _sig(gbv))).astype(BF16)

        xa_ext[0:SUBLANES, :] = xa_ext[tc:tc + SUBLANES, :]
        v_ext[0:SUBLANES, :] = v_ext[tc:tc + SUBLANES, :]

    full = lambda shape: pl.BlockSpec(shape, lambda i: (0,) * len(shape))
    return pl.pallas_call(
        body, name=name, grid=(t // tc,),
        in_specs=[pl.BlockSpec((tc, 6 * c), lambda i: (i, 0)),
                  full(wa.shape), full(ba.shape), full(wr.shape), full(br.shape),
                  full(wi.shape), full(bi.shape), full(lam.shape), full(wb.shape)],
        out_specs=[pl.BlockSpec((tc, 2 * c), lambda i: (i, 0)),
                   pl.BlockSpec((tc, c), lambda i: (i, 0))],
        out_shape=[jax.ShapeDtypeStruct((t, 2 * c), BF16), jax.ShapeDtypeStruct((t, c), F32)],
        scratch_shapes=[pltpu.VMEM((tc + SUBLANES, c), F32), pltpu.VMEM((tc + SUBLANES, c), F32),
                        pltpu.VMEM((tc, c), F32), pltpu.VMEM((tc, c), F32), pltpu.VMEM((tc, c), F32),
                        pltpu.VMEM((1, c), F32)],
        compiler_params=_params(("arbitrary",)),
    )(u, wa, ba, wr, br, wi, bi, lam, wb)


ROW_DWA = 0
ROW_DBA = 4
ROW_DBR = 5
ROW_DBI = 6
ROW_DLAM = 7
ROW_DWB = 8
SMALL_ROWS = 16


def _mix_bwd(u, hs, dy, wa, ba, wr, br, wi, bi, lam, wb, name):
    t = u.shape[0]
    c = u.shape[1] // 6
    tc = MIX_CHUNK
    nt = t // tc
    gb = wr.shape[1]
    nblk = c // gb
    ka, kb = wa.shape[0], wb.shape[0]
    assert ka <= ROW_DBA and kb <= SMALL_ROWS - ROW_DWB
    hb = tc // SUBLANES

    def body(u_ref, uh_ref, hs_ref, hsh_ref, dy_ref, wa_ref, ba_ref, wr_ref, br_ref, wi_ref, bi_ref, lam_ref, wb_ref,
             du_ref, dsm_ref, dwr_ref, dwi_ref,
             xa_ext, v_ext, hs_ext, a_ext, ds_ext, dxc_ext, dcv_ext, xc_s, r_s, i_s, sq_s, g_s, an_s):
        i = pl.program_id(0)
        chunk = nt - 1 - i
        tail = slice(tc, tc + SUBLANES)
        head = slice(0, SUBLANES)

        @pl.when(i == 0)
        def _():
            zero = jnp.zeros((SUBLANES, c), F32)
            a_ext[tail, :] = zero
            ds_ext[tail, :] = zero
            dxc_ext[tail, :] = zero
            dcv_ext[tail, :] = zero
            dsm_ref[...] = jnp.zeros_like(dsm_ref)
            dwr_ref[...] = jnp.zeros_like(dwr_ref)
            dwi_ref[...] = jnp.zeros_like(dwi_ref)

        prev = jnp.where(chunk > 0, 1.0, 0.0)
        xa_ext[head, :] = uh_ref[:, 0:c] * prev
        xa_ext[SUBLANES:SUBLANES + tc, :] = u_ref[:, 0:c]
        v_ext[head, :] = uh_ref[:, 3 * c:4 * c] * uh_ref[:, 4 * c:5 * c] * prev
        v_ext[SUBLANES:SUBLANES + tc, :] = u_ref[:, 3 * c:4 * c] * u_ref[:, 4 * c:5 * c]
        hs_ext[head, :] = hsh_ref[...] * prev
        hs_ext[SUBLANES:SUBLANES + tc, :] = hs_ref[...]

        xc = ba_ref[...]
        for k in range(ka):
            xc = xc + wa_ref[pl.ds(k, 1), :] * xa_ext[pl.ds(SUBLANES - (ka - 1) + k, tc), :]
        xc_s[...] = xc
        c8, dc8 = _decay_consts(lam_ref[...])
        for j in range(nblk):
            sl = slice(j * gb, (j + 1) * gb)
            r, ig, a, sq = _gates(xc_s[:, sl], wr_ref, br_ref, wi_ref, bi_ref, c8, j, gb)
            r_s[:, sl] = r
            i_s[:, sl] = ig
            sq_s[:, sl] = sq
            a_ext[0:tc, sl] = a

        ga = u_ref[:, c:2 * c]
        sga = _sig(ga)
        g_s[...] = dy_ref[:, 0:c] * (ga * sga)
        an_s[...] = a_ext[pl.ds(1, tc), :]

        row = lax.broadcasted_iota(jnp.int32, (SUBLANES, c), 0)

        def scan_step(j, _):
            off = pl.multiple_of(tc - SUBLANES - j * SUBLANES, SUBLANES)
            av = an_s[pl.ds(off, SUBLANES), :]
            bv = g_s[pl.ds(off, SUBLANES), :]
            for d in (1, 2, 4):
                keep = row < SUBLANES - d
                bsh = jnp.where(keep, pltpu.roll(bv, SUBLANES - d, axis=0), 0.0)
                ash = jnp.where(keep, pltpu.roll(av, SUBLANES - d, axis=0), 1.0)
                bv = av * bsh + bv
                av = av * ash
            ds_ext[pl.ds(off, SUBLANES), :] = av * ds_ext[pl.ds(off + SUBLANES, 1), :] + bv
            return 0

        lax.fori_loop(0, tc // SUBLANES, scan_step, 0)

        def acc(row_index, val):
            dsm_ref[pl.ds(row_index, 1), :] += jnp.sum(val, axis=0, keepdims=True)

        def acc_block(row_index, sl, val):
            dsm_ref[pl.ds(row_index, 1), sl] += jnp.sum(val, axis=0, keepdims=True)

        for j in range(nblk):
            sl = slice(j * gb, (j + 1) * gb)
            ds = ds_ext[0:tc, sl]
            hprev = hs_ext[pl.ds(SUBLANES - 1, tc), sl]
            a = a_ext[0:tc, sl]
            sq = sq_s[:, sl]
            ig = i_s[:, sl]
            r = r_s[:, sl]
            xcj = xc_s[:, sl]
            t1 = ds * xcj
            dla = (ds * hprev) * a - (t1 * ig) * ((a * a) / sq)
            acc_block(ROW_DLAM, sl, dla * r)
            dpr = (dla * c8[:, sl]) * (r * (1.0 - r))
            dpi = (t1 * sq) * (ig * (1.0 - ig))
            acc_block(ROW_DBR, sl, dpr)
            acc_block(ROW_DBI, sl, dpi)
            p16 = dpr.astype(BF16)
            q16 = dpi.astype(BF16)
            x16 = xcj.astype(BF16)
            dwr_ref[j] += lax.dot_general(x16, p16, TN_DIMS, preferred_element_type=F32)
            dwi_ref[j] += lax.dot_general(x16, q16, TN_DIMS, preferred_element_type=F32)
            dxc = (ds * (sq * ig)
                   + lax.dot_general(p16, wr_ref[j], NT_DIMS, preferred_element_type=F32)
                   + lax.dot_general(q16, wi_ref[j], NT_DIMS, preferred_element_type=F32))
            dxc_ext[0:tc, sl] = dxc
            acc_block(ROW_DBA, sl, dxc)

        dsilu_a = sga * (1.0 + ga * (1.0 - sga))
        du_ref[:, c:2 * c] = (dy_ref[:, 0:c] * hs_ref[...] * dsilu_a).astype(BF16)

        dxc = dxc_ext[0:tc, :]
        dxa = wa_ref[pl.ds(ka - 1, 1), :] * dxc
        acc(ROW_DWA + ka - 1, dxc * xa_ext[SUBLANES:SUBLANES + tc, :])
        for k in range(ka - 1):
            acc(ROW_DWA + k, dxc * xa_ext[pl.ds(SUBLANES - (ka - 1) + k, tc), :])
            dxa = dxa + wa_ref[pl.ds(k, 1), :] * dxc_ext[pl.ds(ka - 1 - k, tc), :]
        du_ref[:, 0:c] = dxa.astype(BF16)

        cv = wb_ref[pl.ds(0, 1), :] * v_ext[pl.ds(SUBLANES - (kb - 1), tc), :]
        for k in range(1, kb):
            cv = cv + wb_ref[pl.ds(k, 1), :] * v_ext[pl.ds(SUBLANES - (kb - 1) + k, tc), :]
        gbv = u_ref[:, 5 * c:6 * c]
        sgb = _sig(gbv)
        silu_b = gbv * sgb
        dyb = dy_ref[:, c:2 * c]
        gB = u_ref[:, 2 * c:3 * c]
        du_ref[:, 2 * c:3 * c] = (dyb * cv * silu_b).astype(BF16)
        du_ref[:, 5 * c:6 * c] = (dyb * gB * cv * (sgb * (1.0 + gbv * (1.0 - sgb)))).astype(BF16)
        dcv = dyb * gB * silu_b
        dcv_ext[0:tc, :] = dcv
        dv = wb_ref[pl.ds(kb - 1, 1), :] * dcv
        acc(ROW_DWB + kb - 1, dcv * v_ext[SUBLANES:SUBLANES + tc, :])
        for k in range(kb - 1):
            acc(ROW_DWB + k, dcv * v_ext[pl.ds(SUBLANES - (kb - 1) + k, tc), :])
            dv = dv + wb_ref[pl.ds(k, 1), :] * dcv_ext[pl.ds(kb - 1 - k, tc), :]
        du_ref[:, 3 * c:4 * c] = (dv * u_ref[:, 4 * c:5 * c]).astype(BF16)
        du_ref[:, 4 * c:5 * c] = (dv * u_ref[:, 3 * c:4 * c]).astype(BF16)

        a_ext[tail, :] = a_ext[head, :]
        ds_ext[tail, :] = ds_ext[head, :]
        dxc_ext[tail, :] = dxc_ext[head, :]
        dcv_ext[tail, :] = dcv_ext[head, :]

        @pl.when(i == nt - 1)
        def _():
            dsm_ref[pl.ds(ROW_DLAM, 1), :] = dsm_ref[pl.ds(ROW_DLAM, 1), :] * dc8

    full = lambda shape: pl.BlockSpec(shape, lambda i: (0,) * len(shape))
    rev = lambda i: (nt - 1 - i, 0)
    halo = lambda i: (jnp.maximum((nt - 1 - i) * hb - 1, 0), 0)
    ext = pltpu.VMEM((tc + SUBLANES, c), F32)
    blk = pltpu.VMEM((tc, c), F32)
    return pl.pallas_call(
        body, name=name, grid=(nt,),
        in_specs=[pl.BlockSpec((tc, 6 * c), rev), pl.BlockSpec((SUBLANES, 6 * c), halo),
                  pl.BlockSpec((tc, c), rev), pl.BlockSpec((SUBLANES, c), halo),
                  pl.BlockSpec((tc, 2 * c), rev),
                  full(wa.shape), full(ba.shape), full(wr.shape), full(br.shape),
                  full(wi.shape), full(bi.shape), full(lam.shape), full(wb.shape)],
        out_specs=[pl.BlockSpec((tc, 6 * c), rev), full((SMALL_ROWS, c)), full(wr.shape), full(wi.shape)],
        out_shape=[jax.ShapeDtypeStruct((t, 6 * c), BF16), jax.ShapeDtypeStruct((SMALL_ROWS, c), F32),
                   jax.ShapeDtypeStruct(wr.shape, F32), jax.ShapeDtypeStruct(wi.shape, F32)],
        scratch_shapes=[ext] * 7 + [blk] * 6,
        compiler_params=_params(("arbitrary",)),
    )(u, u, hs, hs, dy, wa, ba, wr, br, wi, bi, lam, wb)


def _out_proj(h, y, w, name):
    t, d = h.shape
    dm = y.shape[1]
    tm = _row_tile(t)
    tn = _col_tile(d, (1024, 512, 256))

    def body(h_ref, y_ref, w_ref, o_ref):
        o_ref[...] = h_ref[...] + jnp.dot(y_ref[...], w_ref[...], preferred_element_type=F32)

    return pl.pallas_call(
        body, name=name, grid=(d // tn, t // tm),
        in_specs=[pl.BlockSpec((tm, tn), lambda n, i: (i, n)),
                  pl.BlockSpec((tm, dm), lambda n, i: (i, 0)),
                  pl.BlockSpec((dm, tn), lambda n, i: (0, n))],
        out_specs=pl.BlockSpec((tm, tn), lambda n, i: (i, n)),
        out_shape=jax.ShapeDtypeStruct((t, d), F32),
        compiler_params=_params(("arbitrary", "arbitrary")),
    )(h, y, w)


def _out_proj_dy(dout, w, name):
    t, d = dout.shape
    dm = w.shape[0]
    tm = _row_tile(t)
    tn = _col_tile(dm, (1024, 512, 256))

    def body(g_ref, w_ref, o_ref):
        o_ref[...] = lax.dot_general(g_ref[...].astype(BF16), w_ref[...], NT_DIMS, preferred_element_type=F32)

    return pl.pallas_call(
        body, name=name, grid=(dm // tn, t // tm),
        in_specs=[pl.BlockSpec((tm, d), lambda n, i: (i, 0)),
                  pl.BlockSpec((tn, d), lambda n, i: (n, 0))],
        out_specs=pl.BlockSpec((tm, tn), lambda n, i: (i, n)),
        out_shape=jax.ShapeDtypeStruct((t, dm), F32),
        compiler_params=_params(("arbitrary", "arbitrary")),
    )(dout, w)


def _out_proj_dw(y, dout, name):
    t, dm = y.shape
    d = dout.shape[1]
    tk = _col_tile(t, (384, 192))
    tmm = _col_tile(dm, (1024, 512, 256))

    def body(y_ref, g_ref, o_ref):
        @pl.when(pl.program_id(1) == 0)
        def _():
            o_ref[...] = jnp.zeros_like(o_ref)

        o_ref[...] += lax.dot_general(y_ref[...], g_ref[...].astype(BF16), TN_DIMS, preferred_element_type=F32)

    return pl.pallas_call(
        body, name=name, grid=(dm // tmm, t // tk),
        in_specs=[pl.BlockSpec((tk, tmm), lambda m, k: (k, m)),
                  pl.BlockSpec((tk, d), lambda m, k: (k, 0))],
        out_specs=pl.BlockSpec((tmm, d), lambda m, k: (m, 0)),
        out_shape=jax.ShapeDtypeStruct((dm, d), F32),
        compiler_params=_params(("arbitrary", "arbitrary")),
    )(y, dout)


def _in_proj_bwd(du, wg, h, g, dout, name):
    t, d = h.shape
    s, _, ns = wg.shape
    tm = _row_tile(t)
    tk = _col_tile(ns, (512, 384, 128))
    nb = ns // tk
    nk = s * nb

    def body(du_ref, w_ref, h_ref, g_ref, dout_ref, dh_ref, dg_ref, acc_ref):
        i, k = pl.program_id(0), pl.program_id(1)

        @pl.when(k == 0)
        def _():
            acc_ref[...] = jnp.zeros_like(acc_ref)

        @pl.when((k == 0) & (i == 0))
        def _():
            dg_ref[...] = jnp.zeros_like(dg_ref)

        acc_ref[...] += lax.dot_general(du_ref[...], w_ref[...], NT_DIMS, preferred_element_type=F32)

        @pl.when(k == nk - 1)
        def _():
            x = h_ref[...]
            dhn = acc_ref[...]
            r = lax.rsqrt(jnp.mean(x * x, axis=-1, keepdims=True) + RMS_EPS)
            gd = dhn * g_ref[...]
            dot = jnp.mean(gd * x, axis=-1, keepdims=True)
            dh_ref[...] = dout_ref[...] + (r * gd - x * ((r * r * r) * dot))
            dg_ref[...] += jnp.sum(dhn * (x * r), axis=0, keepdims=True)

    return pl.pallas_call(
        body, name=name, grid=(t // tm, nk),
        in_specs=[pl.BlockSpec((tm, tk), lambda i, k: (i, k)),
                  pl.BlockSpec((None, d, tk), lambda i, k: (k // nb, 0, k % nb)),
                  pl.BlockSpec((tm, d), lambda i, k: (i, 0)),
                  pl.BlockSpec((1, d), lambda i, k: (0, 0)),
                  pl.BlockSpec((tm, d), lambda i, k: (i, 0))],
        out_specs=[pl.BlockSpec((tm, d), lambda i, k: (i, 0)),
                   pl.BlockSpec((1, d), lambda i, k: (0, 0))],
        out_shape=[jax.ShapeDtypeStruct((t, d), F32), jax.ShapeDtypeStruct((1, d), F32)],
        scratch_shapes=[pltpu.VMEM((tm, d), F32)],
        compiler_params=_params(("arbitrary", "arbitrary")),
    )(du, wg, h, g, dout)


def _in_proj_dw(hn, du, s, name):
    t, d = hn.shape
    ns = du.shape[1] // s
    tk = _col_tile(t, (384, 192))
    tn = _col_tile(ns, (768, 384, 128))
    nb = ns // tn

    def body(hn_ref, du_ref, o_ref):
        @pl.when(pl.program_id(1) == 0)
        def _():
            o_ref[...] = jnp.zeros_like(o_ref)

        o_ref[...] += lax.dot_general(hn_ref[...], du_ref[...], TN_DIMS, preferred_element_type=F32)

    return pl.pallas_call(
        body, name=name, grid=(s * nb, t // tk),
        in_specs=[pl.BlockSpec((tk, d), lambda n, k: (k, 0)),
                  pl.BlockSpec((tk, tn), lambda n, k: (k, n))],
        out_specs=pl.BlockSpec((None, d, tn), lambda n, k: (n // nb, 0, n % nb)),
        out_shape=jax.ShapeDtypeStruct((s, d, ns), F32),
        compiler_params=_params(("arbitrary", "arbitrary")),
    )(hn, du)


def _loss_head(h, tgt, g, n_meta, t_real, name):
    t, d = h.shape
    tm = _row_tile(t)

    def body(h_ref, t_ref, g_ref, dh_ref, loss_ref, dg_ref):
        i = pl.program_id(0)

        @pl.when(i == 0)
        def _():
            loss_ref[...] = jnp.zeros_like(loss_ref)
            dg_ref[...] = jnp.zeros_like(dg_ref)

        x = h_ref[...]
        gv = g_ref[...]
        r = lax.rsqrt(jnp.mean(x * x, axis=-1, keepdims=True) + RMS_EPS)
        xr = x * r
        rows = i * tm + lax.broadcasted_iota(jnp.int32, (tm, 1), 0)
        valid = (rows >= n_meta) & (rows < t_real)
        err = jnp.where(valid, xr * gv - t_ref[...], 0.0)
        loss_ref[...] += 0.5 * jnp.sum(jnp.mean(err * err, axis=-1, keepdims=True))
        dy = err * (1.0 / d)
        gd = dy * gv
        dot = jnp.mean(gd * x, axis=-1, keepdims=True)
        dh_ref[...] = r * gd - x * ((r * r * r) * dot)
        dg_ref[...] += jnp.sum(dy * xr, axis=0, keepdims=True)

    return pl.pallas_call(
        body, name=name, grid=(t // tm,),
        in_specs=[pl.BlockSpec((tm, d), lambda i: (i, 0)),
                  pl.BlockSpec((tm, d), lambda i: (i, 0)),
                  pl.BlockSpec((1, d), lambda i: (0, 0))],
        out_specs=[pl.BlockSpec((tm, d), lambda i: (i, 0)),
                   pl.BlockSpec((1, LANES), lambda i: (0, 0)),
                   pl.BlockSpec((1, d), lambda i: (0, 0))],
        out_shape=[jax.ShapeDtypeStruct((t, d), F32), jax.ShapeDtypeStruct((1, LANES), F32),
                   jax.ShapeDtypeStruct((1, d), F32)],
        compiler_params=_params(("arbitrary",)),
    )(h, tgt, g)


def _adamw(w, g, m, v, name):
    rows, cols = w.shape
    tr = rows
    for cand in (512, 256, 128, 64, 32, 16, 8):
        if rows % cand == 0 and cand * cols * 4 <= 2 * 1024 * 1024:
            tr = cand
            break

    def body(w_ref, g_ref, m_ref, v_ref, d_ref, nm_ref, nv_ref):
        gv = g_ref[...]
        m2 = ADAM_B1 * m_ref[...] + (1.0 - ADAM_B1) * gv
        v2 = ADAM_B2 * v_ref[...] + (1.0 - ADAM_B2) * (gv * gv)
        m_hat = m2 / (1.0 - ADAM_B1 ** ADAM_STEP)
        v_hat = v2 / (1.0 - ADAM_B2 ** ADAM_STEP)
        d_ref[...] = -ADAM_LR * (m_hat / (jnp.sqrt(v_hat) + ADAM_EPS) + ADAM_WD * w_ref[...])
        nm_ref[...] = m2
        nv_ref[...] = v2

    spec = pl.BlockSpec((tr, cols), lambda i: (i, 0))
    return pl.pallas_call(
        body, name=name, grid=(rows // tr,),
        in_specs=[spec] * 4, out_specs=[spec] * 3,
        out_shape=[jax.ShapeDtypeStruct((rows, cols), F32)] * 3,
        compiler_params=_params(("arbitrary",)),
    )(w, g, m, v)


def _pair_add(x, ra, c_idx, name):
    s, _, rows, cols = x.shape
    tr = _col_tile(rows, (256, 128, 64, 32, 16))

    def body(c_ref, x_ref, r_ref, o_ref):
        o_ref[...] = (x_ref[...] + r_ref[...]).astype(BF16)

    return pl.pallas_call(
        body, name=name,
        grid_spec=pltpu.PrefetchScalarGridSpec(
            num_scalar_prefetch=1, grid=(s, rows // tr),
            in_specs=[pl.BlockSpec((None, None, tr, cols), lambda a, i, c_ref: (a, c_ref[0], i, 0)),
                      pl.BlockSpec((None, tr, cols), lambda a, i, c_ref: (a, i, 0))],
            out_specs=pl.BlockSpec((None, tr, cols), lambda a, i, c_ref: (a, i, 0))),
        out_shape=jax.ShapeDtypeStruct((s, rows, cols), BF16),
        compiler_params=_params(("arbitrary", "arbitrary")),
    )(c_idx, x, ra)


def _chip_sum(rc, p, where, n_slots, name):
    s, rows, cols = rc.shape
    tr = _col_tile(rows, (256, 128, 64, 32, 16))

    def body(w_ref, x_ref, p_ref, o_ref):
        me = w_ref[0]
        total = jnp.where(me == 0, p_ref[...], x_ref[0]).astype(F32)
        for a in range(1, s):
            total = total + jnp.where(me == a, p_ref[...], x_ref[a]).astype(F32)
        o_ref[...] = total

    return pl.pallas_call(
        body, name=name,
        grid_spec=pltpu.PrefetchScalarGridSpec(
            num_scalar_prefetch=1, grid=(rows // tr,),
            in_specs=[pl.BlockSpec((s, tr, cols), lambda i, w_ref: (0, i, 0)),
                      pl.BlockSpec((None, tr, cols), lambda i, w_ref: (w_ref[0], i, 0))],
            out_specs=pl.BlockSpec((None, tr, cols), lambda i, w_ref: (w_ref[1], i, 0))),
        out_shape=jax.ShapeDtypeStruct((n_slots, rows, cols), F32),
        compiler_params=_params(("arbitrary",)),
    )(where, rc, p)


def _cast_place(w, me_idx, name):
    nl, rows, cols = w.shape
    tr = _col_tile(rows, (256, 128, 64, 32, 16))

    def body(m_ref, w_ref, o_ref):
        o_ref[...] = w_ref[...].astype(BF16)

    return pl.pallas_call(
        body, name=name,
        grid_spec=pltpu.PrefetchScalarGridSpec(
            num_scalar_prefetch=1, grid=(nl, rows // tr),
            in_specs=[pl.BlockSpec((None, tr, cols), lambda l, i, m_ref: (l, i, 0))],
            out_specs=pl.BlockSpec((None, None, tr, cols), lambda l, i, m_ref: (l, m_ref[0], i, 0))),
        out_shape=jax.ShapeDtypeStruct((nl, N_CHIPS, rows, cols), BF16),
        compiler_params=_params(("arbitrary", "arbitrary")),
    )(me_idx, w)


def _place():
    x, y, c = lax.axis_index("x"), lax.axis_index("y"), lax.axis_index("c")
    chips = [(1 - x, y), (x, 1 - y), (1 - x, 1 - y)]
    return x, y, c, chips


def _chip_index(cx, cy):
    return 2 * cx + cy


def _gather_weights(win, wout, small):
    nl = win.shape[0]
    assert nl == N_CORES

    def body(win_in, wout_in, sm_ref, wing_ref, woutg_ref, smg_ref, lsem, ssem, rsem):
        del win_in, wout_in
        x, y, c, chips = _place()
        me = _chip_index(x, y)
        sibling = (x, y, 1 - c)

        def remote(k, src, dst, dev):
            return pltpu.make_async_remote_copy(src_ref=src, dst_ref=dst, send_sem=ssem.at[k], recv_sem=rsem.at[k],
                                                device_id=dev, device_id_type=MESH)

        local = pltpu.make_async_copy(sm_ref, smg_ref.at[me], lsem)
        local.start()

        sends = []
        for j, chip in enumerate(chips):
            sends.append(remote(j, wing_ref.at[c, me], wing_ref.at[c, me], (*chip, c)))
            sends.append(remote(3 + j, woutg_ref.at[c, me], woutg_ref.at[c, me], (*chip, c)))
            sends.append(remote(6 + j, sm_ref, smg_ref.at[me], (*chip, c)))
        for cp in sends:
            cp.start()

        passed = []
        for j, chip in enumerate(chips):
            src = _chip_index(*chip)
            remote(j, wing_ref.at[c, src], wing_ref.at[c, src], (*chip, c)).wait_recv()
            fwd = remote(9 + j, wing_ref.at[c, src], wing_ref.at[c, src], sibling)
            fwd.start()
            passed.append(fwd)
            remote(3 + j, woutg_ref.at[c, src], woutg_ref.at[c, src], (*chip, c)).wait_recv()
            fwd = remote(12 + j, woutg_ref.at[c, src], woutg_ref.at[c, src], sibling)
            fwd.start()
            passed.append(fwd)
        for j, chip in enumerate(chips):
            src = _chip_index(*chip)
            remote(6 + j, sm_ref, smg_ref.at[src], (*chip, c)).wait_recv()
            remote(9 + j, wing_ref.at[1 - c, src], wing_ref.at[1 - c, src], sibling).wait_recv()
            remote(12 + j, woutg_ref.at[1 - c, src], woutg_ref.at[1 - c, src], sibling).wait_recv()
        for cp in sends + passed:
            cp.wait_send()
        local.wait()

    return pl.pallas_call(
        body, name="gather_weights",
        in_specs=[ANY, ANY, ANY], out_specs=[ANY, ANY, ANY],
        out_shape=[jax.ShapeDtypeStruct(win.shape, win.dtype), jax.ShapeDtypeStruct(wout.shape, wout.dtype),
                   jax.ShapeDtypeStruct((N_CHIPS,) + small.shape, small.dtype)],
        input_output_aliases={0: 0, 1: 1},
        scratch_shapes=[pltpu.SemaphoreType.DMA, pltpu.SemaphoreType.DMA((15,)), pltpu.SemaphoreType.DMA((15,))],
    )(win, wout, small)


def _pair_swap(xs):
    n = len(xs)

    def body(*refs):
        x_refs, o_refs, ssem, rsem = refs[:n], refs[n:2 * n], refs[2 * n], refs[2 * n + 1]
        x, y, c, _ = _place()
        copies = [pltpu.make_async_remote_copy(src_ref=x_refs[a].at[:, 1 - c], dst_ref=o_refs[a],
                                               send_sem=ssem.at[a], recv_sem=rsem.at[a],
                                               device_id=(x, y, 1 - c), device_id_type=MESH) for a in range(n)]
        for cp in copies:
            cp.start()
        for cp in copies:
            cp.wait()

    return pl.pallas_call(
        body, name="pair_swap", in_specs=[ANY] * n, out_specs=[ANY] * n,
        out_shape=[jax.ShapeDtypeStruct((a.shape[0],) + a.shape[2:], a.dtype) for a in xs],
        scratch_shapes=[pltpu.SemaphoreType.DMA((n,)), pltpu.SemaphoreType.DMA((n,))],
    )(*xs)


def _chip_scatter(ps):
    n = len(ps)

    def body(*refs):
        p_refs, o_refs, ssem, rsem = refs[:n], refs[n:2 * n], refs[2 * n], refs[2 * n + 1]
        x, y, c, chips = _place()
        me = _chip_index(x, y)
        sends = []
        for a in range(n):
            for j, chip in enumerate(chips):
                sends.append(pltpu.make_async_remote_copy(
                    src_ref=p_refs[a].at[_chip_index(*chip)], dst_ref=o_refs[a].at[me],
                    send_sem=ssem.at[3 * a + j], recv_sem=rsem.at[3 * a + j],
                    device_id=(*chip, c), device_id_type=MESH))
        for cp in sends:
            cp.start()
        for a in range(n):
            for j, chip in enumerate(chips):
                src = _chip_index(*chip)
                pltpu.make_async_remote_copy(
                    src_ref=p_refs[a].at[src], dst_ref=o_refs[a].at[src],
                    send_sem=ssem.at[3 * a + j], recv_sem=rsem.at[3 * a + j],
                    device_id=(*chip, c), device_id_type=MESH).wait_recv()
        for cp in sends:
            cp.wait_send()

    return pl.pallas_call(
        body, name="chip_scatter", in_specs=[ANY] * n, out_specs=[ANY] * n,
        out_shape=[jax.ShapeDtypeStruct(a.shape, a.dtype) for a in ps],
        scratch_shapes=[pltpu.SemaphoreType.DMA((3 * n,)), pltpu.SemaphoreType.DMA((3 * n,))],
    )(*ps)


def _final_gather(fs, rep):
    n = len(fs)

    def body(*refs):
        o_refs, repo_ref = refs[n + 1:2 * n + 1], refs[2 * n + 1]
        ssem, rsem = refs[2 * n + 2:]
        x, y, c, chips = _place()
        slot = 4 * x + 2 * y + c
        copies = [pltpu.make_async_remote_copy(src_ref=o_refs[a].at[c], dst_ref=o_refs[a].at[c],
                                               send_sem=ssem.at[a], recv_sem=rsem.at[a],
                                               device_id=(x, y, 1 - c), device_id_type=MESH) for a in range(n)]
        peers = [(x, y, 1 - c)] + [(*chip, c) for chip in chips] + [(*chip, 1 - c) for chip in chips]
        for k, peer in enumerate(peers):
            copies.append(pltpu.make_async_remote_copy(src_ref=repo_ref.at[slot], dst_ref=repo_ref.at[slot],
                                                       send_sem=ssem.at[n + k], recv_sem=rsem.at[n + k],
                                                       device_id=peer, device_id_type=MESH))
        for cp in copies:
            cp.start()
        for a in range(n):
            pltpu.make_async_remote_copy(src_ref=o_refs[a].at[1 - c], dst_ref=o_refs[a].at[1 - c],
                                         send_sem=ssem.at[a], recv_sem=rsem.at[a],
                                         device_id=(x, y, 1 - c), device_id_type=MESH).wait_recv()
        for k, peer in enumerate(peers):
            px, py, pc = peer
            theirs = repo_ref.at[4 * px + 2 * py + pc]
            pltpu.make_async_remote_copy(src_ref=theirs, dst_ref=theirs, send_sem=ssem.at[n + k], recv_sem=rsem.at[n + k],
                                         device_id=peer, device_id_type=MESH).wait_recv()
        for cp in copies:
            cp.wait_send()

    return pl.pallas_call(
        body, name="final_gather", in_specs=[ANY] * (n + 1), out_specs=[ANY] * (n + 1),
        out_shape=[jax.ShapeDtypeStruct(a.shape, a.dtype) for a in fs] + [jax.ShapeDtypeStruct(rep.shape, rep.dtype)],
        input_output_aliases={k: k for k in range(n + 1)},
        scratch_shapes=[pltpu.SemaphoreType.DMA((n + 7,)), pltpu.SemaphoreType.DMA((n + 7,))],
    )(*fs, rep)


def _block_diag(w, gb):
    nh, hd, _ = w.shape
    per = gb // hd
    w4 = w.reshape(nh // per, per, hd, hd)
    eye = jnp.eye(per, dtype=w.dtype)
    return jnp.einsum("jaik,ab->jaibk", w4, eye).reshape(nh // per, gb, gb)


def _diag_blocks(dense, hd):
    nj, gb, _ = dense.shape
    per = gb // hd
    d5 = dense.reshape(nj, per, hd, per, hd)
    return jnp.stack([d5[:, a, :, a, :] for a in range(per)], axis=1).reshape(nj * per, hd, hd)


def _round_up(n, q):
    return (n + q - 1) // q * q


def kernel(x, meta, norm_g, w_in, conv_a_w, conv_a_b, lru_wr, lru_br, lru_wi, lru_bi, lru_lambda, conv_b_w, w_out, final_g, loss_target, m_meta, m_norm_g, m_w_in, m_conv_a_w, m_conv_a_b, m_lru_wr, m_lru_br, m_lru_wi, m_lru_bi, m_lru_lambda, m_conv_b_w, m_w_out, m_final_g, v_meta, v_norm_g, v_w_in, v_conv_a_w, v_conv_a_b, v_lru_wr, v_lru_br, v_lru_wi, v_lru_bi, v_lru_lambda, v_conv_b_w, v_w_out, v_final_g):
    weights = dict(meta=meta, norm_g=norm_g, w_in=w_in, conv_a_w=conv_a_w, conv_a_b=conv_a_b, lru_wr=lru_wr,
                   lru_br=lru_br, lru_wi=lru_wi, lru_bi=lru_bi, lru_lambda=lru_lambda, conv_b_w=conv_b_w,
                   w_out=w_out, final_g=final_g)
    mom1 = dict(meta=m_meta, norm_g=m_norm_g, w_in=m_w_in, conv_a_w=m_conv_a_w, conv_a_b=m_conv_a_b,
                lru_wr=m_lru_wr, lru_br=m_lru_br, lru_wi=m_lru_wi, lru_bi=m_lru_bi, lru_lambda=m_lru_lambda,
                conv_b_w=m_conv_b_w, w_out=m_w_out, final_g=m_final_g)
    mom2 = dict(meta=v_meta, norm_g=v_norm_g, w_in=v_w_in, conv_a_w=v_conv_a_w, conv_a_b=v_conv_a_b,
                lru_wr=v_lru_wr, lru_br=v_lru_br, lru_wi=v_lru_wi, lru_bi=v_lru_bi, lru_lambda=v_lru_lambda,
                conv_b_w=v_conv_b_w, w_out=v_w_out, final_g=v_final_g)
    names = list(weights)

    assert x.shape[0] == 1
    seq, d = x.shape[1], x.shape[2]
    n_meta, ds = meta.shape
    depth = norm_g.shape[0]
    c = lru_lambda.shape[1]
    nh, hd = lru_wr.shape[1], lru_wr.shape[2]
    ns = w_in.shape[2]
    dms = w_out.shape[1]
    cs = conv_a_w.shape[2]
    ka, kb = conv_a_w.shape[1], conv_b_w.shape[1]
    s = N_CHIPS
    assert depth == N_CORES and d == s * ds and c == s * cs and s * ns == 6 * c and s * dms == 2 * c
    gb = min(GATE_BLOCK, c)
    t_real = n_meta + seq
    t = _round_up(t_real, ROW_QUANTUM)
    my_c = lax.axis_index("c").astype(jnp.int32)
    my_chip = (2 * lax.axis_index("x") + lax.axis_index("y")).astype(jnp.int32)
    c_idx = my_c.reshape(1)
    chip_idx = my_chip.reshape(1)

    sm_rows = _round_up(n_meta + depth * SUBLANES, 2 * SUBLANES)
    small = jnp.zeros((sm_rows, ds), F32)
    small = small.at[0:n_meta, :].set(meta)
    for l in range(depth):
        base = n_meta + l * SUBLANES
        small = small.at[base:base + ka, 0:cs].set(conv_a_w[l])
        small = small.at[base + ka:base + ka + kb, 0:cs].set(conv_b_w[l])
    win_g, wout_g, small_g = _gather_weights(_cast_place(w_in, chip_idx, "cast_w_in"),
                                             _cast_place(w_out, chip_idx, "cast_w_out"), small)
    meta_full = jnp.transpose(small_g[:, 0:n_meta, :], (1, 0, 2)).reshape(n_meta, d)
    wa_full, wb_full = [], []
    for l in range(depth):
        base = n_meta + l * SUBLANES
        wa_full.append(jnp.transpose(small_g[:, base:base + ka, 0:cs], (1, 0, 2)).reshape(ka, c))
        wb_full.append(jnp.transpose(small_g[:, base + ka:base + ka + kb, 0:cs], (1, 0, 2)).reshape(kb, c))

    h = jnp.concatenate([meta_full, x[0], jnp.zeros((t - t_real, d), F32)], axis=0)
    tgt = jnp.concatenate([jnp.zeros((n_meta, d), F32), loss_target[0], jnp.zeros((t - t_real, d), F32)], axis=0)
    layer_w = []
    for l in range(depth):
        layer_w.append(dict(
            g=norm_g[l].reshape(1, d), win=win_g[l], wout=wout_g[l].reshape(2 * c, d),
            wa=wa_full[l], ba=conv_a_b[l].reshape(1, c),
            wr=_block_diag(lru_wr[l], gb).astype(BF16), br=lru_br[l].reshape(1, c),
            wi=_block_diag(lru_wi[l], gb).astype(BF16), bi=lru_bi[l].reshape(1, c),
            lam=lru_lambda[l].reshape(1, c), wb=wb_full[l]))
    saved = []
    for l, lw in enumerate(layer_w):
        u, hn = _norm_in(h, lw["g"], lw["win"], f"norm_in_{l}")
        y, hs = _mix_fwd(u, lw["wa"], lw["ba"], lw["wr"], lw["br"], lw["wi"], lw["bi"], lw["lam"], lw["wb"],
                         f"mix_fwd_{l}")
        saved.append((h, u, hn, y, hs))
        h = _out_proj(h, y, lw["wout"], f"out_proj_{l}")
    dh, loss_lanes, d_final_g = _loss_head(h, tgt, final_g.reshape(1, d), n_meta, t_real, "loss_head")
    loss = lax.psum(loss_lanes[0, 0], ("x", "y", "c"))

    grads = [None] * depth
    for l in reversed(range(depth)):
        lw = layer_w[l]
        h_in, u, hn, y, hs = saved[l]
        dy = _out_proj_dy(dh, lw["wout"], f"out_proj_dy_{l}")
        d_wout = _out_proj_dw(y, dh, f"out_proj_dw_{l}")
        du, dsm, d_wr, d_wi = _mix_bwd(u, hs, dy, lw["wa"], lw["ba"], lw["wr"], lw["br"], lw["wi"], lw["bi"],
                                       lw["lam"], lw["wb"], f"mix_bwd_{l}")
        dh, d_g = _in_proj_bwd(du, lw["win"], h_in, lw["g"], dh, f"in_proj_bwd_{l}")
        d_win = _in_proj_dw(hn, du, s, f"in_proj_dw_{l}")
        grads[l] = dict(win=d_win, wout=d_wout, dsm=dsm, wr=_diag_blocks(d_wr, hd), wi=_diag_blocks(d_wi, hd), g=d_g)
    grad_x = dh[n_meta:t_real][None]

    sharded = []
    for l in range(depth):
        sharded.append(grads[l]["win"].reshape(s, 2, d // 2, ns))
        sharded.append(grads[l]["wout"].reshape(s, 2, dms // 2, d))
    sp = jnp.zeros((sm_rows, s, ds), F32)
    sp = sp.at[0:n_meta].set(dh[0:n_meta].reshape(n_meta, s, ds))
    for l in range(depth):
        base = n_meta + l * SUBLANES
        dsm = grads[l]["dsm"]
        sp = sp.at[base:base + ka, :, 0:cs].set(dsm[ROW_DWA:ROW_DWA + ka].reshape(ka, s, cs))
        sp = sp.at[base + ka:base + ka + kb, :, 0:cs].set(dsm[ROW_DWB:ROW_DWB + kb].reshape(kb, s, cs))
    sharded.append(jnp.transpose(sp, (1, 0, 2)).reshape(s, 2, sm_rows // 2, ds))
    rep_parts = [jnp.concatenate([grads[l]["g"].reshape(-1) for l in range(depth)]), d_final_g.reshape(-1)]
    for row in (ROW_DBA, ROW_DBR, ROW_DBI, ROW_DLAM):
        rep_parts.append(jnp.concatenate([grads[l]["dsm"][row] for l in range(depth)]))
    rep_parts.append(jnp.concatenate([grads[l]["wr"].reshape(-1) for l in range(depth)]))
    rep_parts.append(jnp.concatenate([grads[l]["wi"].reshape(-1) for l in range(depth)]))
    rep_sizes = [p.shape[0] for p in rep_parts]
    piece = _round_up(-(-sum(rep_sizes) // (s * 2)), 2 * SUBLANES * LANES)
    flat = jnp.concatenate(rep_parts + [jnp.zeros((s * 2 * piece - sum(rep_sizes),), F32)])
    sharded.append(flat.reshape(s, 2, piece // LANES, LANES))

    from_sibling = _pair_swap(sharded)
    pair_sums = [_pair_add(a, b, c_idx, f"pair_add_{k}") for k, (a, b) in enumerate(zip(sharded, from_sibling))]
    by_chip = _chip_scatter(pair_sums)
    to_core = jnp.stack([my_chip, my_c])
    to_device = jnp.stack([my_chip, 2 * my_chip + my_c])
    reduced = [_chip_sum(a, p, to_core, N_CORES, f"chip_sum_{k}")
               for k, (a, p) in enumerate(zip(by_chip[:-1], pair_sums[:-1]))]
    reduced_rep = _chip_sum(by_chip[-1], pair_sums[-1], to_device, N_CHIPS * N_CORES, "chip_sum_rep")
    *full, rep_all = _final_gather(reduced, reduced_rep)

    g_win = [full[2 * l].reshape(d, ns) for l in range(depth)]
    g_wout = [full[2 * l + 1].reshape(dms, d) for l in range(depth)]
    g_sp = full[2 * depth].reshape(sm_rows, ds)
    rep_flat = rep_all.reshape(-1)
    rep_out, off = [], 0
    for n in rep_sizes:
        rep_out.append(rep_flat[off:off + n])
        off += n
    grad = dict(
        meta=g_sp[0:n_meta],
        norm_g=rep_out[0].reshape(depth, d),
        w_in=jnp.stack(g_win),
        conv_a_w=jnp.stack([g_sp[n_meta + l * SUBLANES:n_meta + l * SUBLANES + ka, 0:cs] for l in range(depth)]),
        conv_a_b=rep_out[2].reshape(depth, c),
        lru_wr=rep_out[6].reshape(depth, nh, hd, hd),
        lru_br=rep_out[3].reshape(depth, c),
        lru_wi=rep_out[7].reshape(depth, nh, hd, hd),
        lru_bi=rep_out[4].reshape(depth, c),
        lru_lambda=rep_out[5].reshape(depth, c),
        conv_b_w=jnp.stack([g_sp[n_meta + l * SUBLANES + ka:n_meta + l * SUBLANES + ka + kb, 0:cs]
                            for l in range(depth)]),
        w_out=jnp.stack(g_wout),
        final_g=rep_out[1].reshape(d),
    )

    delta, new_m, new_v = {}, {}, {}
    for n in names:
        shape = weights[n].shape
        two_d = (-1, shape[-1]) if len(shape) > 1 else (1, -1)
        if n in ("lru_wr", "lru_wi"):
            two_d = (-1, LANES)
        out = _adamw(weights[n].reshape(two_d), grad[n].reshape(two_d), mom1[n].reshape(two_d),
                     mom2[n].reshape(two_d), f"adamw_{n}")
        delta[n], new_m[n], new_v[n] = (o.reshape(shape) for o in out)

    return (loss, grad_x, *[grad[n] for n in names], *[delta[n] for n in names],
            *[new_m[n] for n in names], *[new_v[n] for n in names])
```

```python
import functools

import jax
import jax.numpy as jnp
from jax import lax
from jax.experimental import pallas as pl
from jax.experimental.pallas import tpu as pltpu

F32 = jnp.float32
BF16 = jnp.bfloat16

RMS_EPS = 1e-6
LRU_C = 8.0
ADAM_LR = 0.001
ADAM_B1 = 0.9
ADAM_B2 = 0.999
ADAM_EPS = 1e-08
ADAM_WD = 0.01
ADAM_STEP = 10

N_CHIPS = 4
N_CORES = 2
VMEM_LIMIT_BYTES = 56 * 1024 * 1024
SUBLANES = 8
LANES = 128
ROW_QUANTUM = 384
MIX_CHUNK = 192
GATE_BLOCK = 256
MESH = pl.DeviceIdType.MESH
ANY = pl.BlockSpec(memory_space=pl.ANY)

NT_DIMS = (((1,), (1,)), ((), ()))
TN_DIMS = (((0,), (0,)), ((), ()))


def _params(sem):
    return pltpu.CompilerParams(dimension_semantics=sem, vmem_limit_bytes=VMEM_LIMIT_BYTES)


def _sig(x):
    return 1.0 / (1.0 + jnp.exp(-x))


def _row_tile(t):
    return 704 if t % 704 == 0 else 192


def _col_tile(n, prefs):
    for p in prefs:
        if n % p == 0:
            return p
    return n


def _norm_in(h, g, wg, name):
    t, d = h.shape
    s, _, ns = wg.shape
    tm = _row_tile(t)
    tn = _col_tile(ns, (512, 384, 128))
    nb = ns // tn

    def body(h_ref, g_ref, w_ref, u_ref, hn_ref):
        @pl.when(pl.program_id(1) == 0)
        def _():
            x = h_ref[...]
            r = lax.rsqrt(jnp.mean(x * x, axis=-1, keepdims=True) + RMS_EPS)
            hn_ref[...] = ((x * r) * g_ref[...]).astype(BF16)

        u_ref[...] = jnp.dot(hn_ref[...], w_ref[...], preferred_element_type=F32)

    return pl.pallas_call(
        body, name=name, grid=(t // tm, s * nb),
        in_specs=[pl.BlockSpec((tm, d), lambda i, n: (i, 0)),
                  pl.BlockSpec((1, d), lambda i, n: (0, 0)),
                  pl.BlockSpec((None, d, tn), lambda i, n: (n // nb, 0, n % nb))],
        out_specs=[pl.BlockSpec((tm, tn), lambda i, n: (i, n)),
                   pl.BlockSpec((tm, d), lambda i, n: (i, 0))],
        out_shape=[jax.ShapeDtypeStruct((t, s * ns), F32), jax.ShapeDtypeStruct((t, d), BF16)],
        compiler_params=_params(("arbitrary", "arbitrary")),
    )(h, g, wg)


def _decay_consts(lam):
    z = -lam
    e = jnp.exp(-jnp.abs(z))
    u = 1.0 + e
    log1p_e = jnp.where(u == 1.0, e, jnp.log(u) * (e / (u - 1.0)))
    sp = jnp.maximum(z, 0.0) + log1p_e
    return -LRU_C * sp, LRU_C * _sig(z)


def _gates(xc, wr_ref, br_ref, wi_ref, bi_ref, c8, j, gb):
    sl = slice(j * gb, (j + 1) * gb)
    x16 = xc.astype(BF16)
    r = _sig(jnp.dot(x16, wr_ref[j], preferred_element_type=F32) + br_ref[:, sl])
    ig = _sig(jnp.dot(x16, wi_ref[j], preferred_element_type=F32) + bi_ref[:, sl])
    la = c8[:, sl] * r
    a = jnp.exp(la)
    sq = jnp.sqrt(-jnp.tanh(la) * (a * a + 1.0))
    return r, ig, a, sq


def _mix_fwd(u, wa, ba, wr, br, wi, bi, lam, wb, name):
    t = u.shape[0]
    c = u.shape[1] // 6
    tc = MIX_CHUNK
    gb = wr.shape[1]
    nblk = c // gb
    ka, kb = wa.shape[0], wb.shape[0]

    def body(u_ref, wa_ref, ba_ref, wr_ref, br_ref, wi_ref, bi_ref, lam_ref, wb_ref,
             y_ref, hs_ref, xa_ext, v_ext, xc_s, a_s, b_s, carry_s):
        @pl.when(pl.program_id(0) == 0)
        def _():
            xa_ext[0:SUBLANES, :] = jnp.zeros((SUBLANES, c), F32)
            v_ext[0:SUBLANES, :] = jnp.zeros((SUBLANES, c), F32)
            carry_s[...] = jnp.zeros_like(carry_s)

        xa_ext[SUBLANES:SUBLANES + tc, :] = u_ref[:, 0:c]
        xc = ba_ref[...]
        for k in range(ka):
            xc = xc + wa_ref[pl.ds(k, 1), :] * xa_ext[pl.ds(SUBLANES - (ka - 1) + k, tc), :]
        xc_s[...] = xc
        c8, _ = _decay_consts(lam_ref[...])
        for j in range(nblk):
            sl = slice(j * gb, (j + 1) * gb)
            xcj = xc_s[:, sl]
            _, ig, a, sq = _gates(xcj, wr_ref, br_ref, wi_ref, bi_ref, c8, j, gb)
            a_s[:, sl] = a
            b_s[:, sl] = sq * (ig * xcj)

        row = lax.broadcasted_iota(jnp.int32, (SUBLANES, c), 0)

        def scan_step(j, _):
            off = pl.multiple_of(j * SUBLANES, SUBLANES)
            av = a_s[pl.ds(off, SUBLANES), :]
            bv = b_s[pl.ds(off, SUBLANES), :]
            for d in (1, 2, 4):
                keep = row >= d
                bsh = jnp.where(keep, pltpu.roll(bv, d, axis=0), 0.0)
                ash = jnp.where(keep, pltpu.roll(av, d, axis=0), 1.0)
                bv = av * bsh + bv
                av = av * ash
            hv = av * carry_s[...] + bv
            hs_ref[pl.ds(off, SUBLANES), :] = hv
            carry_s[...] = hs_ref[pl.ds(off + SUBLANES - 1, 1), :]
            return 0

        lax.fori_loop(0, tc // SUBLANES, scan_step, 0)

        ga = u_ref[:, c:2 * c]
        y_ref[:, 0:c] = (hs_ref[...] * (ga * _sig(ga))).astype(BF16)

        v_ext[SUBLANES:SUBLANES + tc, :] = u_ref[:, 3 * c:4 * c] * u_ref[:, 4 * c:5 * c]
        cv = wb_ref[pl.ds(0, 1), :] * v_ext[pl.ds(SUBLANES - (kb - 1), tc), :]
        for k in range(1, kb):
            cv = cv + wb_ref[pl.ds(k, 1), :] * v_ext[pl.ds(SUBLANES - (kb - 1) + k, tc), :]
        gbv = u_ref[:, 5 * c:6 * c]
        y_ref[:, c:2 * c] = (u_ref[:, 2 * c:3 * c] * cv * (gbv * _sig(gbv))).astype(BF16)

        xa_ext[0:SUBLANES, :] = xa_ext[tc:tc + SUBLANES, :]
        v_ext[0:SUBLANES, :] = v_ext[tc:tc + SUBLANES, :]

    full = lambda shape: pl.BlockSpec(shape, lambda i: (0,) * len(shape))
    return pl.pallas_call(
        body, name=name, grid=(t // tc,),
        in_specs=[pl.BlockSpec((tc, 6 * c), lambda i: (i, 0)),
                  full(wa.shape), full(ba.shape), full(wr.shape), full(br.shape),
                  full(wi.shape), full(bi.shape), full(lam.shape), full(wb.shape)],
        out_specs=[pl.BlockSpec((tc, 2 * c), lambda i: (i, 0)),
                   pl.BlockSpec((tc, c), lambda i: (i, 0))],
        out_shape=[jax.ShapeDtypeStruct((t, 2 * c), BF16), jax.ShapeDtypeStruct((t, c), F32)],
        scratch_shapes=[pltpu.VMEM((tc + SUBLANES, c), F32), pltpu.VMEM((tc + SUBLANES, c), F32),
                        pltpu.VMEM((tc, c), F32), pltpu.VMEM((tc, c), F32), pltpu.VMEM((tc, c), F32),
                        pltpu.VMEM((1, c), F32)],
        compiler_params=_params(("arbitrary",)),
    )(u, wa, ba, wr, br, wi, bi, lam, wb)


ROW_DWA = 0
ROW_DBA = 4
ROW_DBR = 5
ROW_DBI = 6
ROW_DLAM = 7
ROW_DWB = 8
SMALL_ROWS = 16


def _mix_bwd(u, hs, dy, wa, ba, wr, br, wi, bi, lam, wb, name):
    t = u.shape[0]
    c = u.shape[1] // 6
    tc = MIX_CHUNK
    nt = t // tc
    gb = wr.shape[1]
    nblk = c // gb
    ka, kb = wa.shape[0], wb.shape[0]
    assert ka <= ROW_DBA and kb <= SMALL_ROWS - ROW_DWB
    hb = tc // SUBLANES

    def body(u_ref, uh_ref, hs_ref, hsh_ref, dy_ref, wa_ref, ba_ref, wr_ref, br_ref, wi_ref, bi_ref, lam_ref, wb_ref,
             du_ref, dsm_ref, dwr_ref, dwi_ref,
             xa_ext, v_ext, hs_ext, a_ext, ds_ext, dxc_ext, dcv_ext, xc_s, r_s, i_s, sq_s, g_s, an_s):
        i = pl.program_id(0)
        chunk = nt - 1 - i
        tail = slice(tc, tc + SUBLANES)
        head = slice(0, SUBLANES)

        @pl.when(i == 0)
        def _():
            zero = jnp.zeros((SUBLANES, c), F32)
            a_ext[tail, :] = zero
            ds_ext[tail, :] = zero
            dxc_ext[tail, :] = zero
            dcv_ext[tail, :] = zero
            dsm_ref[...] = jnp.zeros_like(dsm_ref)
            dwr_ref[...] = jnp.zeros_like(dwr_ref)
            dwi_ref[...] = jnp.zeros_like(dwi_ref)

        prev = jnp.where(chunk > 0, 1.0, 0.0)
        xa_ext[head, :] = uh_ref[:, 0:c] * prev
        xa_ext[SUBLANES:SUBLANES + tc, :] = u_ref[:, 0:c]
        v_ext[head, :] = uh_ref[:, 3 * c:4 * c] * uh_ref[:, 4 * c:5 * c] * prev
        v_ext[SUBLANES:SUBLANES + tc, :] = u_ref[:, 3 * c:4 * c] * u_ref[:, 4 * c:5 * c]
        hs_ext[head, :] = hsh_ref[...] * prev
        hs_ext[SUBLANES:SUBLANES + tc, :] = hs_ref[...]

        xc = ba_ref[...]
        for k in range(ka):
            xc = xc + wa_ref[pl.ds(k, 1), :] * xa_ext[pl.ds(SUBLANES - (ka - 1) + k, tc), :]
        xc_s[...] = xc
        c8, dc8 = _decay_consts(lam_ref[...])
        for j in range(nblk):
            sl = slice(j * gb, (j + 1) * gb)
            r, ig, a, sq = _gates(xc_s[:, sl], wr_ref, br_ref, wi_ref, bi_ref, c8, j, gb)
            r_s[:, sl] = r
            i_s[:, sl] = ig
            sq_s[:, sl] = sq
            a_ext[0:tc, sl] = a

        ga = u_ref[:, c:2 * c]
        sga = _sig(ga)
        g_s[...] = dy_ref[:, 0:c] * (ga * sga)
        an_s[...] = a_ext[pl.ds(1, tc), :]

        row = lax.broadcasted_iota(jnp.int32, (SUBLANES, c), 0)

        def scan_step(j, _):
            off = pl.multiple_of(tc - SUBLANES - j * SUBLANES, SUBLANES)
            av = an_s[pl.ds(off, SUBLANES), :]
            bv = g_s[pl.ds(off, SUBLANES), :]
            for d in (1, 2, 4):
                keep = row < SUBLANES - d
                bsh = jnp.where(keep, pltpu.roll(bv, SUBLANES - d, axis=0), 0.0)
                ash = jnp.where(keep, pltpu.roll(av, SUBLANES - d, axis=0), 1.0)
                bv = av * bsh + bv
                av = av * ash
            ds_ext[pl.ds(off, SUBLANES), :] = av * ds_ext[pl.ds(off + SUBLANES, 1), :] + bv
            return 0

        lax.fori_loop(0, tc // SUBLANES, scan_step, 0)

        def acc(row_index, val):
            dsm_ref[pl.ds(row_index, 1), :] += jnp.sum(val, axis=0, keepdims=True)

        def acc_block(row_index, sl, val):
            dsm_ref[pl.ds(row_index, 1), sl] += jnp.sum(val, axis=0, keepdims=True)

        for j in range(nblk):
            sl = slice(j * gb, (j + 1) * gb)
            ds = ds_ext[0:tc, sl]
            hprev = hs_ext[pl.ds(SUBLANES - 1, tc), sl]
            a = a_ext[0:tc, sl]
            sq = sq_s[:, sl]
            ig = i_s[:, sl]
            r = r_s[:, sl]
            xcj = xc_s[:, sl]
            t1 = ds * xcj
            dla = (ds * hprev) * a - (t1 * ig) * ((a * a) / sq)
            acc_block(ROW_DLAM, sl, dla * r)
            dpr = (dla * c8[:, sl]) * (r * (1.0 - r))
            dpi = (t1 * sq) * (ig * (1.0 - ig))
            acc_block(ROW_DBR, sl, dpr)
            acc_block(ROW_DBI, sl, dpi)
            p16 = dpr.astype(BF16)
            q16 = dpi.astype(BF16)
            x16 = xcj.astype(BF16)
            dwr_ref[j] += lax.dot_general(x16, p16, TN_DIMS, preferred_element_type=F32)
            dwi_ref[j] += lax.dot_general(x16, q16, TN_DIMS, preferred_element_type=F32)
            dxc = (ds * (sq * ig)
                   + lax.dot_general(p16, wr_ref[j], NT_DIMS, preferred_element_type=F32)
                   + lax.dot_general(q16, wi_ref[j], NT_DIMS, preferred_element_type=F32))
            dxc_ext[0:tc, sl] = dxc
            acc_block(ROW_DBA, sl, dxc)

        dsilu_a = sga * (1.0 + ga * (1.0 - sga))
        du_ref[:, c:2 * c] = (dy_ref[:, 0:c] * hs_ref[...] * dsilu_a).astype(BF16)

        dxc = dxc_ext[0:tc, :]
        dxa = wa_ref[pl.ds(ka - 1, 1), :] * dxc
        acc(ROW_DWA + ka - 1, dxc * xa_ext[SUBLANES:SUBLANES + tc, :])
        for k in range(ka - 1):
            acc(ROW_DWA + k, dxc * xa_ext[pl.ds(SUBLANES - (ka - 1) + k, tc), :])
            dxa = dxa + wa_ref[pl.ds(k, 1), :] * dxc_ext[pl.ds(ka - 1 - k, tc), :]
        du_ref[:, 0:c] = dxa.astype(BF16)

        cv = wb_ref[pl.ds(0, 1), :] * v_ext[pl.ds(SUBLANES - (kb - 1), tc), :]
        for k in range(1, kb):
            cv = cv + wb_ref[pl.ds(k, 1), :] * v_ext[pl.ds(SUBLANES - (kb - 1) + k, tc), :]
        gbv = u_ref[:, 5 * c:6 * c]
        sgb = _sig(gbv)
        silu_b = gbv * sgb
        dyb = dy_ref[:, c:2 * c]
        gB = u_ref[:, 2 * c:3 * c]
        du_ref[:, 2 * c:3 * c] = (dyb * cv * silu_b).astype(BF16)
        du_ref[:, 5 * c:6 * c] = (dyb * gB * cv * (sgb * (1.0 + gbv * (1.0 - sgb)))).astype(BF16)
        dcv = dyb * gB * silu_b
        dcv_ext[0:tc, :] = dcv
        dv = wb_ref[pl.ds(kb - 1, 1), :] * dcv
        acc(ROW_DWB + kb - 1, dcv * v_ext[SUBLANES:SUBLANES + tc, :])
        for k in range(kb - 1):
            acc(ROW_DWB + k, dcv * v_ext[pl.ds(SUBLANES - (kb - 1) + k, tc), :])
            dv = dv + wb_ref[pl.ds(k, 1), :] * dcv_ext[pl.ds(kb - 1 - k, tc), :]
        du_ref[:, 3 * c:4 * c] = (dv * u_ref[:, 4 * c:5 * c]).astype(BF16)
        du_ref[:, 4 * c:5 * c] = (dv * u_ref[:, 3 * c:4 * c]).astype(BF16)

        a_ext[tail, :] = a_ext[head, :]
        ds_ext[tail, :] = ds_ext[head, :]
        dxc_ext[tail, :] = dxc_ext[head, :]
        dcv_ext[tail, :] = dcv_ext[head, :]

        @pl.when(i == nt - 1)
        def _():
            dsm_ref[pl.ds(ROW_DLAM, 1), :] = dsm_ref[pl.ds(ROW_DLAM, 1), :] * dc8

    full = lambda shape: pl.BlockSpec(shape, lambda i: (0,) * len(shape))
    rev = lambda i: (nt - 1 - i, 0)
    halo = lambda i: (jnp.maximum((nt - 1 - i) * hb - 1, 0), 0)
    ext = pltpu.VMEM((tc + SUBLANES, c), F32)
    blk = pltpu.VMEM((tc, c), F32)
    return pl.pallas_call(
        body, name=name, grid=(nt,),
        in_specs=[pl.BlockSpec((tc, 6 * c), rev), pl.BlockSpec((SUBLANES, 6 * c), halo),
                  pl.BlockSpec((tc, c), rev), pl.BlockSpec((SUBLANES, c), halo),
                  pl.BlockSpec((tc, 2 * c), rev),
                  full(wa.shape), full(ba.shape), full(wr.shape), full(br.shape),
                  full(wi.shape), full(bi.shape), full(lam.shape), full(wb.shape)],
        out_specs=[pl.BlockSpec((tc, 6 * c), rev), full((SMALL_ROWS, c)), full(wr.shape), full(wi.shape)],
        out_shape=[jax.ShapeDtypeStruct((t, 6 * c), BF16), jax.ShapeDtypeStruct((SMALL_ROWS, c), F32),
                   jax.ShapeDtypeStruct(wr.shape, F32), jax.ShapeDtypeStruct(wi.shape, F32)],
        scratch_shapes=[ext] * 7 + [blk] * 6,
        compiler_params=_params(("arbitrary",)),
    )(u, u, hs, hs, dy, wa, ba, wr, br, wi, bi, lam, wb)


def _out_proj(h, y, w, name):
    t, d = h.shape
    dm = y.shape[1]
    tm = _row_tile(t)
    tn = _col_tile(d, (1024, 512, 256))

    def body(h_ref, y_ref, w_ref, o_ref):
        o_ref[...] = h_ref[...] + jnp.dot(y_ref[...], w_ref[...], preferred_element_type=F32)

    return pl.pallas_call(
        body, name=name, grid=(d // tn, t // tm),
        in_specs=[pl.BlockSpec((tm, tn), lambda n, i: (i, n)),
                  pl.BlockSpec((tm, dm), lambda n, i: (i, 0)),
                  pl.BlockSpec((dm, tn), lambda n, i: (0, n))],
        out_specs=pl.BlockSpec((tm, tn), lambda n, i: (i, n)),
        out_shape=jax.ShapeDtypeStruct((t, d), F32),
        compiler_params=_params(("arbitrary", "arbitrary")),
    )(h, y, w)


def _out_proj_dy(dout, w, name):
    t, d = dout.shape
    dm = w.shape[0]
    tm = _row_tile(t)
    tn = _col_tile(dm, (1024, 512, 256))

    def body(g_ref, w_ref, o_ref):
        o_ref[...] = lax.dot_general(g_ref[...].astype(BF16), w_ref[...], NT_DIMS, preferred_element_type=F32)

    return pl.pallas_call(
        body, name=name, grid=(dm // tn, t // tm),
        in_specs=[pl.BlockSpec((tm, d), lambda n, i: (i, 0)),
                  pl.BlockSpec((tn, d), lambda n, i: (n, 0))],
        out_specs=pl.BlockSpec((tm, tn), lambda n, i: (i, n)),
        out_shape=jax.ShapeDtypeStruct((t, dm), F32),
        compiler_params=_params(("arbitrary", "arbitrary")),
    )(dout, w)


def _out_proj_dw(y, dout, name):
    t, dm = y.shape
    d = dout.shape[1]
    tk = _col_tile(t, (384, 192))
    tmm = _col_tile(dm, (1024, 512, 256))

    def body(y_ref, g_ref, o_ref):
        @pl.when(pl.program_id(1) == 0)
        def _():
            o_ref[...] = jnp.zeros_like(o_ref)

        o_ref[...] += lax.dot_general(y_ref[...], g_ref[...].astype(BF16), TN_DIMS, preferred_element_type=F32)

    return pl.pallas_call(
        body, name=name, grid=(dm // tmm, t // tk),
        in_specs=[pl.BlockSpec((tk, tmm), lambda m, k: (k, m)),
                  pl.BlockSpec((tk, d), lambda m, k: (k, 0))],
        out_specs=pl.BlockSpec((tmm, d), lambda m, k: (m, 0)),
        out_shape=jax.ShapeDtypeStruct((dm, d), F32),
        compiler_params=_params(("arbitrary", "arbitrary")),
    )(y, dout)


def _in_proj_bwd(du, wg, h, g, dout, name):
    t, d = h.shape
    s, _, ns = wg.shape
    tm = _row_tile(t)
    tk = _col_tile(ns, (512, 384, 128))
    nb = ns // tk
    nk = s * nb

    def body(du_ref, w_ref, h_ref, g_ref, dout_ref, dh_ref, dg_ref, acc_ref):
        i, k = pl.program_id(0), pl.program_id(1)

        @pl.when(k == 0)
        def _():
            acc_ref[...] = jnp.zeros_like(acc_ref)

        @pl.when((k == 0) & (i == 0))
        def _():
            dg_ref[...] = jnp.zeros_like(dg_ref)

        acc_ref[...] += lax.dot_general(du_ref[...], w_ref[...], NT_DIMS, preferred_element_type=F32)

        @pl.when(k == nk - 1)
        def _():
            x = h_ref[...]
            dhn = acc_ref[...]
            r = lax.rsqrt(jnp.mean(x * x, axis=-1, keepdims=True) + RMS_EPS)
            gd = dhn * g_ref[...]
            dot = jnp.mean(gd * x, axis=-1, keepdims=True)
            dh_ref[...] = dout_ref[...] + (r * gd - x * ((r * r * r) * dot))
            dg_ref[...] += jnp.sum(dhn * (x * r), axis=0, keepdims=True)

    return pl.pallas_call(
        body, name=name, grid=(t // tm, nk),
        in_specs=[pl.BlockSpec((tm, tk), lambda i, k: (i, k)),
                  pl.BlockSpec((None, d, tk), lambda i, k: (k // nb, 0, k % nb)),
                  pl.BlockSpec((tm, d), lambda i, k: (i, 0)),
                  pl.BlockSpec((1, d), lambda i, k: (0, 0)),
                  pl.BlockSpec((tm, d), lambda i, k: (i, 0))],
        out_specs=[pl.BlockSpec((tm, d), lambda i, k: (i, 0)),
                   pl.BlockSpec((1, d), lambda i, k: (0, 0))],
        out_shape=[jax.ShapeDtypeStruct((t, d), F32), jax.ShapeDtypeStruct((1, d), F32)],
        scratch_shapes=[pltpu.VMEM((tm, d), F32)],
        compiler_params=_params(("arbitrary", "arbitrary")),
    )(du, wg, h, g, dout)


def _in_proj_dw(hn, du, s, name):
    t, d = hn.shape
    ns = du.shape[1] // s
    tk = _col_tile(t, (384, 192))
    tn = _col_tile(ns, (768, 384, 128))
    nb = ns // tn

    def body(hn_ref, du_ref, o_ref):
        @pl.when(pl.program_id(1) == 0)
        def _():
            o_ref[...] = jnp.zeros_like(o_ref)

        o_ref[...] += lax.dot_general(hn_ref[...], du_ref[...], TN_DIMS, preferred_element_type=F32)

    return pl.pallas_call(
        body, name=name, grid=(s * nb, t // tk),
        in_specs=[pl.BlockSpec((tk, d), lambda n, k: (k, 0)),
                  pl.BlockSpec((tk, tn), lambda n, k: (k, n))],
        out_specs=pl.BlockSpec((None, d, tn), lambda n, k: (n // nb, 0, n % nb)),
        out_shape=jax.ShapeDtypeStruct((s, d, ns), F32),
        compiler_params=_params(("arbitrary", "arbitrary")),
    )(hn, du)


def _loss_head(h, tgt, g, n_meta, t_real, name):
    t, d = h.shape
    tm = _row_tile(t)

    def body(h_ref, t_ref, g_ref, dh_ref, loss_ref, dg_ref):
        i = pl.program_id(0)

        @pl.when(i == 0)
        def _():
            loss_ref[...] = jnp.zeros_like(loss_ref)
            dg_ref[...] = jnp.zeros_like(dg_ref)

        x = h_ref[...]
        gv = g_ref[...]
        r = lax.rsqrt(jnp.mean(x * x, axis=-1, keepdims=True) + RMS_EPS)
        xr = x * r
        rows = i * tm + lax.broadcasted_iota(jnp.int32, (tm, 1), 0)
        valid = (rows >= n_meta) & (rows < t_real)
        err = jnp.where(valid, xr * gv - t_ref[...], 0.0)
        loss_ref[...] += 0.5 * jnp.sum(jnp.mean(err * err, axis=-1, keepdims=True))
        dy = err * (1.0 / d)
        gd = dy * gv
        dot = jnp.mean(gd * x, axis=-1, keepdims=True)
        dh_ref[...] = r * gd - x * ((r * r * r) * dot)
        dg_ref[...] += jnp.sum(dy * xr, axis=0, keepdims=True)

    return pl.pallas_call(
        body, name=name, grid=(t // tm,),
        in_specs=[pl.BlockSpec((tm, d), lambda i: (i, 0)),
                  pl.BlockSpec((tm, d), lambda i: (i, 0)),
                  pl.BlockSpec((1, d), lambda i: (0, 0))],
        out_specs=[pl.BlockSpec((tm, d), lambda i: (i, 0)),
                   pl.BlockSpec((1, LANES), lambda i: (0, 0)),
                   pl.BlockSpec((1, d), lambda i: (0, 0))],
        out_shape=[jax.ShapeDtypeStruct((t, d), F32), jax.ShapeDtypeStruct((1, LANES), F32),
                   jax.ShapeDtypeStruct((1, d), F32)],
        compiler_params=_params(("arbitrary",)),
    )(h, tgt, g)


def _adamw(w, g, m, v, name):
    rows, cols = w.shape
    tr = rows
    for cand in (512, 256, 128, 64, 32, 16, 8):
        if rows % cand == 0 and cand * cols * 4 <= 2 * 1024 * 1024:
            tr = cand
            break

    def body(w_ref, g_ref, m_ref, v_ref, d_ref, nm_ref, nv_ref):
        gv = g_ref[...]
        m2 = ADAM_B1 * m_ref[...] + (1.0 - ADAM_B1) * gv
        v2 = ADAM_B2 * v_ref[...] + (1.0 - ADAM_B2) * (gv * gv)
        m_hat = m2 / (1.0 - ADAM_B1 ** ADAM_STEP)
        v_hat = v2 / (1.0 - ADAM_B2 ** ADAM_STEP)
        d_ref[...] = -ADAM_LR * (m_hat / (jnp.sqrt(v_hat) + ADAM_EPS) + ADAM_WD * w_ref[...])
        nm_ref[...] = m2
        nv_ref[...] = v2

    spec = pl.BlockSpec((tr, cols), lambda i: (i, 0))
    return pl.pallas_call(
        body, name=name, grid=(rows // tr,),
        in_specs=[spec] * 4, out_specs=[spec] * 3,
        out_shape=[jax.ShapeDtypeStruct((rows, cols), F32)] * 3,
        compiler_params=_params(("arbitrary",)),
    )(w, g, m, v)


def _pair_add(x, ra, c_idx, name):
    s, _, rows, cols = x.shape
    tr = _col_tile(rows, (256, 128, 64, 32, 16))

    def body(c_ref, x_ref, r_ref, o_ref):
        o_ref[...] = (x_ref[...] + r_ref[...]).astype(BF16)

    return pl.pallas_call(
        body, name=name,
        grid_spec=pltpu.PrefetchScalarGridSpec(
            num_scalar_prefetch=1, grid=(s, rows // tr),
            in_specs=[pl.BlockSpec((None, None, tr, cols), lambda a, i, c_ref: (a, c_ref[0], i, 0)),
                      pl.BlockSpec((None, tr, cols), lambda a, i, c_ref: (a, i, 0))],
            out_specs=pl.BlockSpec((None, tr, cols), lambda a, i, c_ref: (a, i, 0))),
        out_shape=jax.ShapeDtypeStruct((s, rows, cols), BF16),
        compiler_params=_params(("arbitrary", "arbitrary")),
    )(c_idx, x, ra)


def _chip_sum(rc, p, where, n_slots, name):
    s, rows, cols = rc.shape
    tr = _col_tile(rows, (256, 128, 64, 32, 16))

    def body(w_ref, x_ref, p_ref, o_ref):
        me = w_ref[0]
        total = jnp.where(me == 0, p_ref[...], x_ref[0]).astype(F32)
        for a in range(1, s):
            total = total + jnp.where(me == a, p_ref[...], x_ref[a]).astype(F32)
        o_ref[...] = total

    return pl.pallas_call(
        body, name=name,
        grid_spec=pltpu.PrefetchScalarGridSpec(
            num_scalar_prefetch=1, grid=(rows // tr,),
            in_specs=[pl.BlockSpec((s, tr, cols), lambda i, w_ref: (0, i, 0)),
                      pl.BlockSpec((None, tr, cols), lambda i, w_ref: (w_ref[0], i, 0))],
            out_specs=pl.BlockSpec((None, tr, cols), lambda i, w_ref: (w_ref[1], i, 0))),
        out_shape=jax.ShapeDtypeStruct((n_slots, rows, cols), F32),
        compiler_params=_params(("arbitrary",)),
    )(where, rc, p)


def _cast_place(w, layer, me_idx, name):
    _, rows, cols = w.shape
    tr = _col_tile(rows, (256, 128, 64, 32, 16))

    def body(m_ref, w_ref, o_ref):
        o_ref[...] = w_ref[...].astype(BF16)

    return pl.pallas_call(
        body, name=name,
        grid_spec=pltpu.PrefetchScalarGridSpec(
            num_scalar_prefetch=1, grid=(rows // tr,),
            in_specs=[pl.BlockSpec((None, tr, cols), lambda i, m_ref: (layer, i, 0))],
            out_specs=pl.BlockSpec((None, tr, cols), lambda i, m_ref: (m_ref[0], i, 0))),
        out_shape=jax.ShapeDtypeStruct((N_CHIPS, rows, cols), BF16),
        compiler_params=_params(("arbitrary",)),
    )(me_idx, w)


def _place():
    x, y, c = lax.axis_index("x"), lax.axis_index("y"), lax.axis_index("c")
    chips = [(1 - x, y), (x, 1 - y), (1 - x, 1 - y)]
    return x, y, c, chips


def _chip_index(cx, cy):
    return 2 * cx + cy


def _gather_copies(bufs, stage):
    x, y, c, chips = _place()
    me = _chip_index(x, y)
    copies = []
    for b in bufs:
        for chip in chips:
            src = _chip_index(*chip)
            if stage == 0:
                copies.append((b.at[me, c], (*chip, c), b.at[src, c]))
            else:
                copies.append((b.at[src, c], (x, y, 1 - c), b.at[src, 1 - c]))
    return copies


def _remote(ref, peer, ssem, rsem, k):
    return pltpu.make_async_remote_copy(src_ref=ref, dst_ref=ref, send_sem=ssem.at[k], recv_sem=rsem.at[k],
                                        device_id=peer, device_id_type=MESH)


def _gather_first(bufs, small):
    n = len(bufs)
    k = 3 * n

    def body(*refs):
        sm_ref = refs[n]
        b_refs, smg_ref = refs[n + 1:2 * n + 1], refs[2 * n + 1]
        lsem, ssem, rsem = refs[2 * n + 2:]
        x, y, c, chips = _place()
        me = _chip_index(x, y)
        local = pltpu.make_async_copy(sm_ref, smg_ref.at[me], lsem)
        local.start()
        first = _gather_copies(b_refs, 0)
        second = _gather_copies(b_refs, 1)
        started = []
        for i, (ref, peer, _) in enumerate(first):
            started.append(_remote(ref, peer, ssem, rsem, i))
        for j, chip in enumerate(chips):
            started.append(pltpu.make_async_remote_copy(
                src_ref=sm_ref, dst_ref=smg_ref.at[me], send_sem=ssem.at[2 * k + j], recv_sem=rsem.at[2 * k + j],
                device_id=(*chip, c), device_id_type=MESH))
        for cp in started:
            cp.start()
        for i, (_, peer, lands) in enumerate(first):
            _remote(lands, peer, ssem, rsem, i).wait_recv()
            ref, sib, _ = second[i]
            fwd = _remote(ref, sib, ssem, rsem, k + i)
            fwd.start()
            started.append(fwd)
        for i, (_, sib, lands) in enumerate(second):
            _remote(lands, sib, ssem, rsem, k + i).wait_recv()
        for j, chip in enumerate(chips):
            theirs = smg_ref.at[_chip_index(*chip)]
            pltpu.make_async_remote_copy(src_ref=theirs, dst_ref=theirs, send_sem=ssem.at[2 * k + j],
                                         recv_sem=rsem.at[2 * k + j], device_id=(*chip, c),
                                         device_id_type=MESH).wait_recv()
        for cp in started:
            cp.wait_send()
        local.wait()

    return pl.pallas_call(
        body, name="gather_first",
        in_specs=[ANY] * (n + 1), out_specs=[ANY] * (n + 1),
        out_shape=[jax.ShapeDtypeStruct(b.shape, b.dtype) for b in bufs]
        + [jax.ShapeDtypeStruct((N_CHIPS,) + small.shape, small.dtype)],
        input_output_aliases={i: i for i in range(n)},
        scratch_shapes=[pltpu.SemaphoreType.DMA, pltpu.SemaphoreType.DMA((2 * k + 3,)),
                        pltpu.SemaphoreType.DMA((2 * k + 3,))],
    )(*bufs, small)


HBM = pl.BlockSpec(memory_space=pltpu.HBM)
SEM = pl.BlockSpec(memory_space=pltpu.SEMAPHORE)
DATAFLOW = pltpu.SideEffectType.DATAFLOW_SIDE_EFFECTING


def _gather_start(bufs, stage, name):
    n = len(bufs)

    def body(*refs):
        ssem, rsem = refs[n], refs[n + 1]
        b_refs, token = refs[n + 2:2 * n + 2], refs[2 * n + 2]
        for i, (ref, peer, _) in enumerate(_gather_copies(b_refs, stage)):
            _remote(ref, peer, ssem, rsem, i).start()
        token[...] = jnp.zeros_like(token)

    return pl.pallas_call(
        body, name=name,
        out_shape=(pltpu.SemaphoreType.DMA((3 * n,)), pltpu.SemaphoreType.DMA((3 * n,)),
                   *[pltpu.HBM(b.shape, b.dtype) for b in bufs], jax.ShapeDtypeStruct((SUBLANES, LANES), F32)),
        in_specs=[HBM] * n,
        out_specs=(SEM, SEM, *[HBM] * n, pl.BlockSpec(memory_space=pltpu.VMEM)),
        input_output_aliases={i: 2 + i for i in range(n)},
        compiler_params=pltpu.CompilerParams(has_side_effects=DATAFLOW),
    )(*[pltpu.with_memory_space_constraint(b, pltpu.HBM) for b in bufs])


def _gather_wait(bufs, ssem, rsem, after, stage, name):
    n = len(bufs)

    def body(*refs):
        b_refs, ssem_ref, rsem_ref = refs[:n], refs[n], refs[n + 1]
        for i, (ref, peer, lands) in enumerate(_gather_copies(b_refs, stage)):
            _remote(ref, peer, ssem_ref, rsem_ref, i).wait_send()
            _remote(lands, peer, ssem_ref, rsem_ref, i).wait_recv()

    return pl.pallas_call(
        body, name=name,
        out_shape=tuple(pltpu.HBM(b.shape, b.dtype) for b in bufs),
        in_specs=[HBM] * n + [SEM, SEM, ANY], out_specs=tuple([HBM] * n),
        input_output_aliases={i: i for i in range(n)},
        compiler_params=pltpu.CompilerParams(has_side_effects=DATAFLOW),
    )(*bufs, ssem, rsem, after)


def _pair_swap(xs):
    n = len(xs)

    def body(*refs):
        x_refs, o_refs, ssem, rsem = refs[:n], refs[n:2 * n], refs[2 * n], refs[2 * n + 1]
        x, y, c, _ = _place()
        copies = [pltpu.make_async_remote_copy(src_ref=x_refs[a].at[:, 1 - c], dst_ref=o_refs[a],
                                               send_sem=ssem.at[a], recv_sem=rsem.at[a],
                                               device_id=(x, y, 1 - c), device_id_type=MESH) for a in range(n)]
        for cp in copies:
            cp.start()
        for cp in copies:
            cp.wait()

    return pl.pallas_call(
        body, name="pair_swap", in_specs=[ANY] * n, out_specs=[ANY] * n,
        out_shape=[jax.ShapeDtypeStruct((a.shape[0],) + a.shape[2:], a.dtype) for a in xs],
        scratch_shapes=[pltpu.SemaphoreType.DMA((n,)), pltpu.SemaphoreType.DMA((n,))],
    )(*xs)


def _chip_scatter(ps):
    n = len(ps)

    def body(*refs):
        p_refs, o_refs, ssem, rsem = refs[:n], refs[n:2 * n], refs[2 * n], refs[2 * n + 1]
        x, y, c, chips = _place()
        me = _chip_index(x, y)
        sends = []
        for a in range(n):
            for j, chip in enumerate(chips):
                sends.append(pltpu.make_async_remote_copy(
                    src_ref=p_refs[a].at[_chip_index(*chip)], dst_ref=o_refs[a].at[me],
                    send_sem=ssem.at[3 * a + j], recv_sem=rsem.at[3 * a + j],
                    device_id=(*chip, c), device_id_type=MESH))
        for cp in sends:
            cp.start()
        for a in range(n):
            for j, chip in enumerate(chips):
                src = _chip_index(*chip)
                pltpu.make_async_remote_copy(
                    src_ref=p_refs[a].at[src], dst_ref=o_refs[a].at[src],
                    send_sem=ssem.at[3 * a + j], recv_sem=rsem.at[3 * a + j],
                    device_id=(*chip, c), device_id_type=MESH).wait_recv()
        for cp in sends:
            cp.wait_send()

    return pl.pallas_call(
        body, name="chip_scatter", in_specs=[ANY] * n, out_specs=[ANY] * n,
        out_shape=[jax.ShapeDtypeStruct(a.shape, a.dtype) for a in ps],
        scratch_shapes=[pltpu.SemaphoreType.DMA((3 * n,)), pltpu.SemaphoreType.DMA((3 * n,))],
    )(*ps)


def _final_gather(fs, rep):
    n = len(fs)

    def body(*refs):
        o_refs, repo_ref = refs[n + 1:2 * n + 1], refs[2 * n + 1]
        ssem, rsem = refs[2 * n + 2:]
        x, y, c, chips = _place()
        slot = 4 * x + 2 * y + c
        copies = [pltpu.make_async_remote_copy(src_ref=o_refs[a].at[c], dst_ref=o_refs[a].at[c],
                                               send_sem=ssem.at[a], recv_sem=rsem.at[a],
                                               device_id=(x, y, 1 - c), device_id_type=MESH) for a in range(n)]
        peers = [(x, y, 1 - c)] + [(*chip, c) for chip in chips] + [(*chip, 1 - c) for chip in chips]
        for k, peer in enumerate(peers):
            copies.append(pltpu.make_async_remote_copy(src_ref=repo_ref.at[slot], dst_ref=repo_ref.at[slot],
                                                       send_sem=ssem.at[n + k], recv_sem=rsem.at[n + k],
                                                       device_id=peer, device_id_type=MESH))
        for cp in copies:
            cp.start()
        for a in range(n):
            pltpu.make_async_remote_copy(src_ref=o_refs[a].at[1 - c], dst_ref=o_refs[a].at[1 - c],
                                         send_sem=ssem.at[a], recv_sem=rsem.at[a],
                                         device_id=(x, y, 1 - c), device_id_type=MESH).wait_recv()
        for k, peer in enumerate(peers):
            px, py, pc = peer
            theirs = repo_ref.at[4 * px + 2 * py + pc]
            pltpu.make_async_remote_copy(src_ref=theirs, dst_ref=theirs, send_sem=ssem.at[n + k], recv_sem=rsem.at[n + k],
                                         device_id=peer, device_id_type=MESH).wait_recv()
        for cp in copies:
            cp.wait_send()

    return pl.pallas_call(
        body, name="final_gather", in_specs=[ANY] * (n + 1), out_specs=[ANY] * (n + 1),
        out_shape=[jax.ShapeDtypeStruct(a.shape, a.dtype) for a in fs] + [jax.ShapeDtypeStruct(rep.shape, rep.dtype)],
        input_output_aliases={k: k for k in range(n + 1)},
        scratch_shapes=[pltpu.SemaphoreType.DMA((n + 7,)), pltpu.SemaphoreType.DMA((n + 7,))],
    )(*fs, rep)


def _block_diag(w, gb):
    nh, hd, _ = w.shape
    per = gb // hd
    w4 = w.reshape(nh // per, per, hd, hd)
    eye = jnp.eye(per, dtype=w.dtype)
    return jnp.einsum("jaik,ab->jaibk", w4, eye).reshape(nh // per, gb, gb)


def _diag_blocks(dense, hd):
    nj, gb, _ = dense.shape
    per = gb // hd
    d5 = dense.reshape(nj, per, hd, per, hd)
    return jnp.stack([d5[:, a, :, a, :] for a in range(per)], axis=1).reshape(nj * per, hd, hd)


def _round_up(n, q):
    return (n + q - 1) // q * q


def kernel(x, meta, norm_g, w_in, conv_a_w, conv_a_b, lru_wr, lru_br, lru_wi, lru_bi, lru_lambda, conv_b_w, w_out, final_g, loss_target, m_meta, m_norm_g, m_w_in, m_conv_a_w, m_conv_a_b, m_lru_wr, m_lru_br, m_lru_wi, m_lru_bi, m_lru_lambda, m_conv_b_w, m_w_out, m_final_g, v_meta, v_norm_g, v_w_in, v_conv_a_w, v_conv_a_b, v_lru_wr, v_lru_br, v_lru_wi, v_lru_bi, v_lru_lambda, v_conv_b_w, v_w_out, v_final_g):
    weights = dict(meta=meta, norm_g=norm_g, w_in=w_in, conv_a_w=conv_a_w, conv_a_b=conv_a_b, lru_wr=lru_wr,
                   lru_br=lru_br, lru_wi=lru_wi, lru_bi=lru_bi, lru_lambda=lru_lambda, conv_b_w=conv_b_w,
                   w_out=w_out, final_g=final_g)
    mom1 = dict(meta=m_meta, norm_g=m_norm_g, w_in=m_w_in, conv_a_w=m_conv_a_w, conv_a_b=m_conv_a_b,
                lru_wr=m_lru_wr, lru_br=m_lru_br, lru_wi=m_lru_wi, lru_bi=m_lru_bi, lru_lambda=m_lru_lambda,
                conv_b_w=m_conv_b_w, w_out=m_w_out, final_g=m_final_g)
    mom2 = dict(meta=v_meta, norm_g=v_norm_g, w_in=v_w_in, conv_a_w=v_conv_a_w, conv_a_b=v_conv_a_b,
                lru_wr=v_lru_wr, lru_br=v_lru_br, lru_wi=v_lru_wi, lru_bi=v_lru_bi, lru_lambda=v_lru_lambda,
                conv_b_w=v_conv_b_w, w_out=v_w_out, final_g=v_final_g)
    names = list(weights)

    assert x.shape[0] == 1
    seq, d = x.shape[1], x.shape[2]
    n_meta, ds = meta.shape
    depth = norm_g.shape[0]
    c = lru_lambda.shape[1]
    nh, hd = lru_wr.shape[1], lru_wr.shape[2]
    ns = w_in.shape[2]
    dms = w_out.shape[1]
    cs = conv_a_w.shape[2]
    ka, kb = conv_a_w.shape[1], conv_b_w.shape[1]
    s = N_CHIPS
    assert depth == N_CORES and d == s * ds and c == s * cs and s * ns == 6 * c and s * dms == 2 * c
    gb = min(GATE_BLOCK, c)
    t_real = n_meta + seq
    t = _round_up(t_real, ROW_QUANTUM)
    my_c = lax.axis_index("c").astype(jnp.int32)
    my_chip = (2 * lax.axis_index("x") + lax.axis_index("y")).astype(jnp.int32)
    c_idx = my_c.reshape(1)
    chip_idx = my_chip.reshape(1)

    sm_rows = _round_up(n_meta + depth * SUBLANES, 2 * SUBLANES)
    small = jnp.zeros((sm_rows, ds), F32)
    small = small.at[0:n_meta, :].set(meta)
    for l in range(depth):
        base = n_meta + l * SUBLANES
        small = small.at[base:base + ka, 0:cs].set(conv_a_w[l])
        small = small.at[base + ka:base + ka + kb, 0:cs].set(conv_b_w[l])
    win_b = [_cast_place(w_in, l, chip_idx, f"cast_w_in_{l}").reshape(s, 2, d // 2, ns) for l in range(depth)]
    wout_b = [_cast_place(w_out, l, chip_idx, f"cast_w_out_{l}").reshape(s, 2, dms // 2, d) for l in range(depth)]
    win_b[0], wout_b[0], small_g = _gather_first([win_b[0], wout_b[0]], small)
    later = [win_b[1], wout_b[1]]
    ssem, rsem, *later, token = _gather_start(later, 0, "gather_ici_start")
    meta_full = jnp.transpose(small_g[:, 0:n_meta, :], (1, 0, 2)).reshape(n_meta, d)
    wa_full, wb_full = [], []
    for l in range(depth):
        base = n_meta + l * SUBLANES
        wa_full.append(jnp.transpose(small_g[:, base:base + ka, 0:cs], (1, 0, 2)).reshape(ka, c))
        wb_full.append(jnp.transpose(small_g[:, base + ka:base + ka + kb, 0:cs], (1, 0, 2)).reshape(kb, c))

    h = jnp.concatenate([meta_full, x[0], jnp.zeros((t - t_real, d), F32)], axis=0)
    tgt = jnp.concatenate([jnp.zeros((n_meta, d), F32), loss_target[0], jnp.zeros((t - t_real, d), F32)], axis=0)
    layer_w = []
    for l in range(depth):
        layer_w.append(dict(
            g=norm_g[l].reshape(1, d), wa=wa_full[l], ba=conv_a_b[l].reshape(1, c),
            wr=_block_diag(lru_wr[l], gb).astype(BF16), br=lru_br[l].reshape(1, c),
            wi=_block_diag(lru_wi[l], gb).astype(BF16), bi=lru_bi[l].reshape(1, c),
            lam=lru_lambda[l].reshape(1, c), wb=wb_full[l]))
    saved = []
    for l, lw in enumerate(layer_w):
        first = l == 0
        lw["win"] = win_b[l].reshape(s, d, ns)
        lw["wout"] = wout_b[l].reshape(2 * c, d)
        u, hn = _norm_in(h, lw["g"] + token[0, 0] if first else lw["g"], lw["win"], f"norm_in_{l}")
        if first:
            later = _gather_wait(later, ssem, rsem, u, 0, "gather_ici_wait")
            ssem, rsem, *later, token = _gather_start(later, 1, "gather_d2d_start")
        y, hs = _mix_fwd(u, lw["wa"], lw["ba"] + token[0, 0] if first else lw["ba"], lw["wr"], lw["br"], lw["wi"],
                         lw["bi"], lw["lam"], lw["wb"], f"mix_fwd_{l}")
        if first:
            win_b[1], wout_b[1] = _gather_wait(later, ssem, rsem, y, 1, "gather_d2d_wait")
        saved.append((h, u, hn, y, hs))
        h = _out_proj(h, y, lw["wout"], f"out_proj_{l}")
    dh, loss_lanes, d_final_g = _loss_head(h, tgt, final_g.reshape(1, d), n_meta, t_real, "loss_head")
    loss = lax.psum(loss_lanes[0, 0], ("x", "y", "c"))

    grads = [None] * depth
    for l in reversed(range(depth)):
        lw = layer_w[l]
        h_in, u, hn, y, hs = saved[l]
        dy = _out_proj_dy(dh, lw["wout"], f"out_proj_dy_{l}")
        d_wout = _out_proj_dw(y, dh, f"out_proj_dw_{l}")
        du, dsm, d_wr, d_wi = _mix_bwd(u, hs, dy, lw["wa"], lw["ba"], lw["wr"], lw["br"], lw["wi"], lw["bi"],
                                       lw["lam"], lw["wb"], f"mix_bwd_{l}")
        dh, d_g = _in_proj_bwd(du, lw["win"], h_in, lw["g"], dh, f"in_proj_bwd_{l}")
        d_win = _in_proj_dw(hn, du, s, f"in_proj_dw_{l}")
        grads[l] = dict(win=d_win, wout=d_wout, dsm=dsm, wr=_diag_blocks(d_wr, hd), wi=_diag_blocks(d_wi, hd), g=d_g)
    grad_x = dh[n_meta:t_real][None]

    sharded = []
    for l in range(depth):
        sharded.append(grads[l]["win"].reshape(s, 2, d // 2, ns))
        sharded.append(grads[l]["wout"].reshape(s, 2, dms // 2, d))
    sp = jnp.zeros((sm_rows, s, ds), F32)
    sp = sp.at[0:n_meta].set(dh[0:n_meta].reshape(n_meta, s, ds))
    for l in range(depth):
        base = n_meta + l * SUBLANES
        dsm = grads[l]["dsm"]
        sp = sp.at[base:base + ka, :, 0:cs].set(dsm[ROW_DWA:ROW_DWA + ka].reshape(ka, s, cs))
        sp = sp.at[base + ka:base + ka + kb, :, 0:cs].set(dsm[ROW_DWB:ROW_DWB + kb].reshape(kb, s, cs))
    sharded.append(jnp.transpose(sp, (1, 0, 2)).reshape(s, 2, sm_rows // 2, ds))
    rep_parts = [jnp.concatenate([grads[l]["g"].reshape(-1) for l in range(depth)]), d_final_g.reshape(-1)]
    for row in (ROW_DBA, ROW_DBR, ROW_DBI, ROW_DLAM):
        rep_parts.append(jnp.concatenate([grads[l]["dsm"][row] for l in range(depth)]))
    rep_parts.append(jnp.concatenate([grads[l]["wr"].reshape(-1) for l in range(depth)]))
    rep_parts.append(jnp.concatenate([grads[l]["wi"].reshape(-1) for l in range(depth)]))
    rep_sizes = [p.shape[0] for p in rep_parts]
    piece = _round_up(-(-sum(rep_sizes) // (s * 2)), 2 * SUBLANES * LANES)
    flat = jnp.concatenate(rep_parts + [jnp.zeros((s * 2 * piece - sum(rep_sizes),), F32)])
    sharded.append(flat.reshape(s, 2, piece // LANES, LANES))

    from_sibling = _pair_swap(sharded)
    pair_sums = [_pair_add(a, b, c_idx, f"pair_add_{k}") for k, (a, b) in enumerate(zip(sharded, from_sibling))]
    by_chip = _chip_scatter(pair_sums)
    to_core = jnp.stack([my_chip, my_c])
    to_device = jnp.stack([my_chip, 2 * my_chip + my_c])
    reduced = [_chip_sum(a, p, to_core, N_CORES, f"chip_sum_{k}")
               for k, (a, p) in enumerate(zip(by_chip[:-1], pair_sums[:-1]))]
    reduced_rep = _chip_sum(by_chip[-1], pair_sums[-1], to_device, N_CHIPS * N_CORES, "chip_sum_rep")
    *full, rep_all = _final_gather(reduced, reduced_rep)

    g_win = [full[2 * l].reshape(d, ns) for l in range(depth)]
    g_wout = [full[2 * l + 1].reshape(dms, d) for l in range(depth)]
    g_sp = full[2 * depth].reshape(sm_rows, ds)
    rep_flat = rep_all.reshape(-1)
    rep_out, off = [], 0
    for n in rep_sizes:
        rep_out.append(rep_flat[off:off + n])
        off += n
    grad = dict(
        meta=g_sp[0:n_meta],
        norm_g=rep_out[0].reshape(depth, d),
        w_in=jnp.stack(g_win),
        conv_a_w=jnp.stack([g_sp[n_meta + l * SUBLANES:n_meta + l * SUBLANES + ka, 0:cs] for l in range(depth)]),
        conv_a_b=rep_out[2].reshape(depth, c),
        lru_wr=rep_out[6].reshape(depth, nh, hd, hd),
        lru_br=rep_out[3].reshape(depth, c),
        lru_wi=rep_out[7].reshape(depth, nh, hd, hd),
        lru_bi=rep_out[4].reshape(depth, c),
        lru_lambda=rep_out[5].reshape(depth, c),
        conv_b_w=jnp.stack([g_sp[n_meta + l * SUBLANES + ka:n_meta + l * SUBLANES + ka + kb, 0:cs]
                            for l in range(depth)]),
        w_out=jnp.stack(g_wout),
        final_g=rep_out[1].reshape(d),
    )

    delta, new_m, new_v = {}, {}, {}
    for n in names:
        shape = weights[n].shape
        two_d = (-1, shape[-1]) if len(shape) > 1 else (1, -1)
        if n in ("lru_wr", "lru_wi"):
            two_d = (-1, LANES)
        out = _adamw(weights[n].reshape(two_d), grad[n].reshape(two_d), mom1[n].reshape(two_d),
                     mom2[n].reshape(two_d), f"adamw_{n}")
        delta[n], new_m[n], new_v[n] = (o.reshape(shape) for o in out)

    return (loss, grad_x, *[grad[n] for n in names], *[delta[n] for n in names],
            *[new_m[n] for n in names], *[new_v[n] for n in names])
```

```python
import functools

import jax
import jax.numpy as jnp
from jax import lax
from jax.experimental import pallas as pl
from jax.experimental.pallas import tpu as pltpu

F32 = jnp.float32
BF16 = jnp.bfloat16

RMS_EPS = 1e-6
LRU_C = 8.0
ADAM_LR = 0.001
ADAM_B1 = 0.9
ADAM_B2 = 0.999
ADAM_EPS = 1e-08
ADAM_WD = 0.01
ADAM_STEP = 10

N_CHIPS = 4
N_CORES = 2
VMEM_LIMIT_BYTES = 56 * 1024 * 1024
SUBLANES = 8
LANES = 128
ROW_QUANTUM = 384
MIX_CHUNK = 192
GATE_BLOCK = 256
MESH = pl.DeviceIdType.MESH
ANY = pl.BlockSpec(memory_space=pl.ANY)

NT_DIMS = (((1,), (1,)), ((), ()))
TN_DIMS = (((0,), (0,)), ((), ()))


def _params(sem):
    return pltpu.CompilerParams(dimension_semantics=sem, vmem_limit_bytes=VMEM_LIMIT_BYTES)


def _sig(x):
    return 1.0 / (1.0 + jnp.exp(-x))


def _row_tile(t):
    return 704 if t % 704 == 0 else 192


def _col_tile(n, prefs):
    for p in prefs:
        if n % p == 0:
            return p
    return n


def _slab_rows(rows, cols):
    if rows * cols * 4 <= 1024 * 1024:
        return rows
    return _col_tile(rows, (256, 128, 64, 32, 16))


def _norm_in(h, g, wg, name):
    t, d = h.shape
    s, _, ns = wg.shape
    tm = _row_tile(t)
    tn = _col_tile(ns, (512, 384, 128))
    nb = ns // tn

    def body(h_ref, g_ref, w_ref, u_ref, hn_ref):
        @pl.when(pl.program_id(1) == 0)
        def _():
            x = h_ref[...]
            r = lax.rsqrt(jnp.mean(x * x, axis=-1, keepdims=True) + RMS_EPS)
            hn_ref[...] = ((x * r) * g_ref[...]).astype(BF16)

        u_ref[...] = jnp.dot(hn_ref[...], w_ref[...], preferred_element_type=F32)

    return pl.pallas_call(
        body, name=name, grid=(t // tm, s * nb),
        in_specs=[pl.BlockSpec((tm, d), lambda i, n: (i, 0)),
                  pl.BlockSpec((1, d), lambda i, n: (0, 0)),
                  pl.BlockSpec((None, d, tn), lambda i, n: (n // nb, 0, n % nb))],
        out_specs=[pl.BlockSpec((tm, tn), lambda i, n: (i, n)),
                   pl.BlockSpec((tm, d), lambda i, n: (i, 0))],
        out_shape=[jax.ShapeDtypeStruct((t, s * ns), F32), jax.ShapeDtypeStruct((t, d), BF16)],
        compiler_params=_params(("arbitrary", "arbitrary")),
    )(h, g, wg)


def _decay_consts(lam):
    z = -lam
    e = jnp.exp(-jnp.abs(z))
    u = 1.0 + e
    log1p_e = jnp.where(u == 1.0, e, jnp.log(u) * (e / (u - 1.0)))
    sp = jnp.maximum(z, 0.0) + log1p_e
    return -LRU_C * sp, LRU_C * _sig(z)


def _gates(xc, wr_ref, br_ref, wi_ref, bi_ref, c8, j, gb):
    sl = slice(j * gb, (j + 1) * gb)
    x16 = xc.astype(BF16)
    r = _sig(jnp.dot(x16, wr_ref[j], preferred_element_type=F32) + br_ref[:, sl])
    ig = _sig(jnp.dot(x16, wi_ref[j], preferred_element_type=F32) + bi_ref[:, sl])
    la = c8[:, sl] * r
    a = jnp.exp(la)
    sq = jnp.sqrt(-jnp.tanh(la) * (a * a + 1.0))
    return r, ig, a, sq


def _mix_fwd(u, wa, ba, wr, br, wi, bi, lam, wb, name):
    t = u.shape[0]
    c = u.shape[1] // 6
    tc = MIX_CHUNK
    gb = wr.shape[1]
    nblk = c // gb
    ka, kb = wa.shape[0], wb.shape[0]

    def body(u_ref, wa_ref, ba_ref, wr_ref, br_ref, wi_ref, bi_ref, lam_ref, wb_ref,
             y_ref, hs_ref, xa_ext, v_ext, xc_s, a_s, b_s, carry_s):
        @pl.when(pl.program_id(0) == 0)
        def _():
            xa_ext[0:SUBLANES, :] = jnp.zeros((SUBLANES, c), F32)
            v_ext[0:SUBLANES, :] = jnp.zeros((SUBLANES, c), F32)
            carry_s[...] = jnp.zeros_like(carry_s)

        xa_ext[SUBLANES:SUBLANES + tc, :] = u_ref[:, 0:c]
        xc = ba_ref[...]
        for k in range(ka):
            xc = xc + wa_ref[pl.ds(k, 1), :] * xa_ext[pl.ds(SUBLANES - (ka - 1) + k, tc), :]
        xc_s[...] = xc
        c8, _ = _decay_consts(lam_ref[...])
        for j in range(nblk):
            sl = slice(j * gb, (j + 1) * gb)
            xcj = xc_s[:, sl]
            _, ig, a, sq = _gates(xcj, wr_ref, br_ref, wi_ref, bi_ref, c8, j, gb)
            a_s[:, sl] = a
            b_s[:, sl] = sq * (ig * xcj)

        row = lax.broadcasted_iota(jnp.int32, (SUBLANES, c), 0)

        def scan_step(j, _):
            off = pl.multiple_of(j * SUBLANES, SUBLANES)
            av = a_s[pl.ds(off, SUBLANES), :]
            bv = b_s[pl.ds(off, SUBLANES), :]
            for d in (1, 2, 4):
                keep = row >= d
                bsh = jnp.where(keep, pltpu.roll(bv, d, axis=0), 0.0)
                ash = jnp.where(keep, pltpu.roll(av, d, axis=0), 1.0)
                bv = av * bsh + bv
                av = av * ash
            hv = av * carry_s[...] + bv
            hs_ref[pl.ds(off, SUBLANES), :] = hv
            carry_s[...] = hs_ref[pl.ds(off + SUBLANES - 1, 1), :]
            return 0

        lax.fori_loop(0, tc // SUBLANES, scan_step, 0)

        ga = u_ref[:, c:2 * c]
        y_ref[:, 0:c] = (hs_ref[...] * (ga * _sig(ga))).astype(BF16)

        v_ext[SUBLANES:SUBLANES + tc, :] = u_ref[:, 3 * c:4 * c] * u_ref[:, 4 * c:5 * c]
        cv = wb_ref[pl.ds(0, 1), :] * v_ext[pl.ds(SUBLANES - (kb - 1), tc), :]
        for k in range(1, kb):
            cv = cv + wb_ref[pl.ds(k, 1), :] * v_ext[pl.ds(SUBLANES - (kb - 1) + k, tc), :]
        gbv = u_ref[:, 5 * c:6 * c]
        y_ref[:, c:2 * c] = (u_ref[:, 2 * c:3 * c] * cv * (gbv * _sig(gbv))).astype(BF16)

        xa_ext[0:SUBLANES, :] = xa_ext[tc:tc + SUBLANES, :]
        v_ext[0:SUBLANES, :] = v_ext[tc:tc + SUBLANES, :]

    full = lambda shape: pl.BlockSpec(shape, lambda i: (0,) * len(shape))
    return pl.pallas_call(
        body, name=name, grid=(t // tc,),
        in_specs=[pl.BlockSpec((tc, 6 * c), lambda i: (i, 0)),
                  full(wa.shape), full(ba.shape), full(wr.shape), full(br.shape),
                  full(wi.shape), full(bi.shape), full(lam.shape), full(wb.shape)],
        out_specs=[pl.BlockSpec((tc, 2 * c), lambda i: (i, 0)),
                   pl.BlockSpec((tc, c), lambda i: (i, 0))],
        out_shape=[jax.ShapeDtypeStruct((t, 2 * c), BF16), jax.ShapeDtypeStruct((t, c), F32)],
        scratch_shapes=[pltpu.VMEM((tc + SUBLANES, c), F32), pltpu.VMEM((tc + SUBLANES, c), F32),
                        pltpu.VMEM((tc, c), F32), pltpu.VMEM((tc, c), F32), pltpu.VMEM((tc, c), F32),
                        pltpu.VMEM((1, c), F32)],
        compiler_params=_params(("arbitrary",)),
    )(u, wa, ba, wr, br, wi, bi, lam, wb)


ROW_DWA = 0
ROW_DBA = 4
ROW_DBR = 5
ROW_DBI = 6
ROW_DLAM = 7
ROW_DWB = 8
SMALL_ROWS = 16


def _mix_bwd(u, hs, dy, wa, ba, wr, br, wi, bi, lam, wb, name, after=None):
    t = u.shape[0]
    c = u.shape[1] // 6
    tc = MIX_CHUNK
    nt = t // tc
    gb = wr.shape[1]
    nblk = c // gb
    ka, kb = wa.shape[0], wb.shape[0]
    assert ka <= ROW_DBA and kb <= SMALL_ROWS - ROW_DWB
    hb = tc // SUBLANES

    def body(u_ref, uh_ref, hs_ref, hsh_ref, dy_ref, wa_ref, ba_ref, wr_ref, br_ref, wi_ref, bi_ref, lam_ref, wb_ref,
             du_ref, dsm_ref, dwr_ref, dwi_ref,
             xa_ext, v_ext, hs_ext, a_ext, ds_ext, dxc_ext, dcv_ext, xc_s, r_s, i_s, sq_s, g_s, an_s):
        i = pl.program_id(0)
        chunk = nt - 1 - i
        tail = slice(tc, tc + SUBLANES)
        head = slice(0, SUBLANES)

        @pl.when(i == 0)
        def _():
            zero = jnp.zeros((SUBLANES, c), F32)
            a_ext[tail, :] = zero
            ds_ext[tail, :] = zero
            dxc_ext[tail, :] = zero
            dcv_ext[tail, :] = zero
            dsm_ref[...] = jnp.zeros_like(dsm_ref)
            dwr_ref[...] = jnp.zeros_like(dwr_ref)
            dwi_ref[...] = jnp.zeros_like(dwi_ref)

        prev = jnp.where(chunk > 0, 1.0, 0.0)
        xa_ext[head, :] = uh_ref[:, 0:c] * prev
        xa_ext[SUBLANES:SUBLANES + tc, :] = u_ref[:, 0:c]
        v_ext[head, :] = uh_ref[:, 3 * c:4 * c] * uh_ref[:, 4 * c:5 * c] * prev
        v_ext[SUBLANES:SUBLANES + tc, :] = u_ref[:, 3 * c:4 * c] * u_ref[:, 4 * c:5 * c]
        hs_ext[head, :] = hsh_ref[...] * prev
        hs_ext[SUBLANES:SUBLANES + tc, :] = hs_ref[...]

        xc = ba_ref[...]
        for k in range(ka):
            xc = xc + wa_ref[pl.ds(k, 1), :] * xa_ext[pl.ds(SUBLANES - (ka - 1) + k, tc), :]
        xc_s[...] = xc
        c8, dc8 = _decay_consts(lam_ref[...])
        for j in range(nblk):
            sl = slice(j * gb, (j + 1) * gb)
            r, ig, a, sq = _gates(xc_s[:, sl], wr_ref, br_ref, wi_ref, bi_ref, c8, j, gb)
            r_s[:, sl] = r
            i_s[:, sl] = ig
            sq_s[:, sl] = sq
            a_ext[0:tc, sl] = a

        ga = u_ref[:, c:2 * c]
        sga = _sig(ga)
        g_s[...] = dy_ref[:, 0:c] * (ga * sga)
        an_s[...] = a_ext[pl.ds(1, tc), :]

        row = lax.broadcasted_iota(jnp.int32, (SUBLANES, c), 0)

        def scan_step(j, _):
            off = pl.multiple_of(tc - SUBLANES - j * SUBLANES, SUBLANES)
            av = an_s[pl.ds(off, SUBLANES), :]
            bv = g_s[pl.ds(off, SUBLANES), :]
            for d in (1, 2, 4):
                keep = row < SUBLANES - d
                bsh = jnp.where(keep, pltpu.roll(bv, SUBLANES - d, axis=0), 0.0)
                ash = jnp.where(keep, pltpu.roll(av, SUBLANES - d, axis=0), 1.0)
                bv = av * bsh + bv
                av = av * ash
            ds_ext[pl.ds(off, SUBLANES), :] = av * ds_ext[pl.ds(off + SUBLANES, 1), :] + bv
            return 0

        lax.fori_loop(0, tc // SUBLANES, scan_step, 0)

        def acc(row_index, val):
            dsm_ref[pl.ds(row_index, 1), :] += jnp.sum(val, axis=0, keepdims=True)

        def acc_block(row_index, sl, val):
            dsm_ref[pl.ds(row_index, 1), sl] += jnp.sum(val, axis=0, keepdims=True)

        for j in range(nblk):
            sl = slice(j * gb, (j + 1) * gb)
            ds = ds_ext[0:tc, sl]
            hprev = hs_ext[pl.ds(SUBLANES - 1, tc), sl]
            a = a_ext[0:tc, sl]
            sq = sq_s[:, sl]
            ig = i_s[:, sl]
            r = r_s[:, sl]
            xcj = xc_s[:, sl]
            t1 = ds * xcj
            dla = (ds * hprev) * a - (t1 * ig) * ((a * a) / sq)
            acc_block(ROW_DLAM, sl, dla * r)
            dpr = (dla * c8[:, sl]) * (r * (1.0 - r))
            dpi = (t1 * sq) * (ig * (1.0 - ig))
            acc_block(ROW_DBR, sl, dpr)
            acc_block(ROW_DBI, sl, dpi)
            p16 = dpr.astype(BF16)
            q16 = dpi.astype(BF16)
            x16 = xcj.astype(BF16)
            dwr_ref[j] += lax.dot_general(x16, p16, TN_DIMS, preferred_element_type=F32)
            dwi_ref[j] += lax.dot_general(x16, q16, TN_DIMS, preferred_element_type=F32)
            dxc = (ds * (sq * ig)
                   + lax.dot_general(p16, wr_ref[j], NT_DIMS, preferred_element_type=F32)
                   + lax.dot_general(q16, wi_ref[j], NT_DIMS, preferred_element_type=F32))
            dxc_ext[0:tc, sl] = dxc
            acc_block(ROW_DBA, sl, dxc)

        dsilu_a = sga * (1.0 + ga * (1.0 - sga))
        du_ref[:, c:2 * c] = (dy_ref[:, 0:c] * hs_ref[...] * dsilu_a).astype(BF16)

        dxc = dxc_ext[0:tc, :]
        dxa = wa_ref[pl.ds(ka - 1, 1), :] * dxc
        acc(ROW_DWA + ka - 1, dxc * xa_ext[SUBLANES:SUBLANES + tc, :])
        for k in range(ka - 1):
            acc(ROW_DWA + k, dxc * xa_ext[pl.ds(SUBLANES - (ka - 1) + k, tc), :])
            dxa = dxa + wa_ref[pl.ds(k, 1), :] * dxc_ext[pl.ds(ka - 1 - k, tc), :]
        du_ref[:, 0:c] = dxa.astype(BF16)

        cv = wb_ref[pl.ds(0, 1), :] * v_ext[pl.ds(SUBLANES - (kb - 1), tc), :]
        for k in range(1, kb):
            cv = cv + wb_ref[pl.ds(k, 1), :] * v_ext[pl.ds(SUBLANES - (kb - 1) + k, tc), :]
        gbv = u_ref[:, 5 * c:6 * c]
        sgb = _sig(gbv)
        silu_b = gbv * sgb
        dyb = dy_ref[:, c:2 * c]
        gB = u_ref[:, 2 * c:3 * c]
        du_ref[:, 2 * c:3 * c] = (dyb * cv * silu_b).astype(BF16)
        du_ref[:, 5 * c:6 * c] = (dyb * gB * cv * (sgb * (1.0 + gbv * (1.0 - sgb)))).astype(BF16)
        dcv = dyb * gB * silu_b
        dcv_ext[0:tc, :] = dcv
        dv = wb_ref[pl.ds(kb - 1, 1), :] * dcv
        acc(ROW_DWB + kb - 1, dcv * v_ext[SUBLANES:SUBLANES + tc, :])
        for k in range(kb - 1):
            acc(ROW_DWB + k, dcv * v_ext[pl.ds(SUBLANES - (kb - 1) + k, tc), :])
            dv = dv + wb_ref[pl.ds(k, 1), :] * dcv_ext[pl.ds(kb - 1 - k, tc), :]
        du_ref[:, 3 * c:4 * c] = (dv * u_ref[:, 4 * c:5 * c]).astype(BF16)
        du_ref[:, 4 * c:5 * c] = (dv * u_ref[:, 3 * c:4 * c]).astype(BF16)

        a_ext[tail, :] = a_ext[head, :]
        ds_ext[tail, :] = ds_ext[head, :]
        dxc_ext[tail, :] = dxc_ext[head, :]
        dcv_ext[tail, :] = dcv_ext[head, :]

        @pl.when(i == nt - 1)
        def _():
            dsm_ref[pl.ds(ROW_DLAM, 1), :] = dsm_ref[pl.ds(ROW_DLAM, 1), :] * dc8

    full = lambda shape: pl.BlockSpec(shape, lambda i: (0,) * len(shape))
    rev = lambda i: (nt - 1 - i, 0)
    halo = lambda i: (jnp.maximum((nt - 1 - i) * hb - 1, 0), 0)
    ext = pltpu.VMEM((tc + SUBLANES, c), F32)
    blk = pltpu.VMEM((tc, c), F32)
    body, more_specs, more = _behind(body, 13, after)
    return pl.pallas_call(
        body, name=name, grid=(nt,),
        in_specs=[pl.BlockSpec((tc, 6 * c), rev), pl.BlockSpec((SUBLANES, 6 * c), halo),
                  pl.BlockSpec((tc, c), rev), pl.BlockSpec((SUBLANES, c), halo),
                  pl.BlockSpec((tc, 2 * c), rev),
                  full(wa.shape), full(ba.shape), full(wr.shape), full(br.shape),
                  full(wi.shape), full(bi.shape), full(lam.shape), full(wb.shape)] + more_specs,
        out_specs=[pl.BlockSpec((tc, 6 * c), rev), full((SMALL_ROWS, c)), full(wr.shape), full(wi.shape)],
        out_shape=[jax.ShapeDtypeStruct((t, 6 * c), BF16), jax.ShapeDtypeStruct((SMALL_ROWS, c), F32),
                   jax.ShapeDtypeStruct(wr.shape, F32), jax.ShapeDtypeStruct(wi.shape, F32)],
        scratch_shapes=[ext] * 7 + [blk] * 6,
        compiler_params=_params(("arbitrary",)),
    )(u, u, hs, hs, dy, wa, ba, wr, br, wi, bi, lam, wb, *more)


def _out_proj(h, y, w, name):
    t, d = h.shape
    dm = y.shape[1]
    tm = _row_tile(t)
    tn = _col_tile(d, (1024, 512, 256))

    def body(h_ref, y_ref, w_ref, o_ref):
        o_ref[...] = h_ref[...] + jnp.dot(y_ref[...], w_ref[...], preferred_element_type=F32)

    return pl.pallas_call(
        body, name=name, grid=(d // tn, t // tm),
        in_specs=[pl.BlockSpec((tm, tn), lambda n, i: (i, n)),
                  pl.BlockSpec((tm, dm), lambda n, i: (i, 0)),
                  pl.BlockSpec((dm, tn), lambda n, i: (0, n))],
        out_specs=pl.BlockSpec((tm, tn), lambda n, i: (i, n)),
        out_shape=jax.ShapeDtypeStruct((t, d), F32),
        compiler_params=_params(("arbitrary", "arbitrary")),
    )(h, y, w)


def _behind(body, n_in, after):
    if after is None:
        return body, [], []
    return (lambda *refs: body(*refs[:n_in], *refs[n_in + 1:])), [ANY], [after]


def _out_proj_dy(dout, w, name, after=None):
    t, d = dout.shape
    dm = w.shape[0]
    tm = _row_tile(t)
    tn = _col_tile(dm, (1024, 512, 256))

    def body(g_ref, w_ref, o_ref):
        o_ref[...] = lax.dot_general(g_ref[...].astype(BF16), w_ref[...], NT_DIMS, preferred_element_type=F32)

    body, more_specs, more = _behind(body, 2, after)
    return pl.pallas_call(
        body, name=name, grid=(dm // tn, t // tm),
        in_specs=[pl.BlockSpec((tm, d), lambda n, i: (i, 0)),
                  pl.BlockSpec((tn, d), lambda n, i: (n, 0))] + more_specs,
        out_specs=pl.BlockSpec((tm, tn), lambda n, i: (i, n)),
        out_shape=jax.ShapeDtypeStruct((t, dm), F32),
        compiler_params=_params(("arbitrary", "arbitrary")),
    )(dout, w, *more)


def _out_proj_dw(y, dout, name):
    t, dm = y.shape
    d = dout.shape[1]
    tk = _col_tile(t, (384, 192))
    tmm = _col_tile(dm, (1024, 512, 256))

    def body(y_ref, g_ref, o_ref):
        @pl.when(pl.program_id(1) == 0)
        def _():
            o_ref[...] = jnp.zeros_like(o_ref)

        o_ref[...] += lax.dot_general(y_ref[...], g_ref[...].astype(BF16), TN_DIMS, preferred_element_type=F32)

    return pl.pallas_call(
        body, name=name, grid=(dm // tmm, t // tk),
        in_specs=[pl.BlockSpec((tk, tmm), lambda m, k: (k, m)),
                  pl.BlockSpec((tk, d), lambda m, k: (k, 0))],
        out_specs=pl.BlockSpec((tmm, d), lambda m, k: (m, 0)),
        out_shape=jax.ShapeDtypeStruct((dm, d), F32),
        compiler_params=_params(("arbitrary", "arbitrary")),
    )(y, dout)


def _in_proj_bwd(du, wg, h, g, dout, name, after=None):
    t, d = h.shape
    s, _, ns = wg.shape
    tm = _row_tile(t)
    tk = _col_tile(ns, (512, 384, 128))
    nb = ns // tk
    nk = s * nb

    def body(du_ref, w_ref, h_ref, g_ref, dout_ref, dh_ref, dg_ref, acc_ref):
        i, k = pl.program_id(0), pl.program_id(1)

        @pl.when(k == 0)
        def _():
            acc_ref[...] = jnp.zeros_like(acc_ref)

        @pl.when((k == 0) & (i == 0))
        def _():
            dg_ref[...] = jnp.zeros_like(dg_ref)

        acc_ref[...] += lax.dot_general(du_ref[...], w_ref[...], NT_DIMS, preferred_element_type=F32)

        @pl.when(k == nk - 1)
        def _():
            x = h_ref[...]
            dhn = acc_ref[...]
            r = lax.rsqrt(jnp.mean(x * x, axis=-1, keepdims=True) + RMS_EPS)
            gd = dhn * g_ref[...]
            dot = jnp.mean(gd * x, axis=-1, keepdims=True)
            dh_ref[...] = dout_ref[...] + (r * gd - x * ((r * r * r) * dot))
            dg_ref[...] += jnp.sum(dhn * (x * r), axis=0, keepdims=True)

    body, more_specs, more = _behind(body, 5, after)
    return pl.pallas_call(
        body, name=name, grid=(t // tm, nk),
        in_specs=[pl.BlockSpec((tm, tk), lambda i, k: (i, k)),
                  pl.BlockSpec((None, d, tk), lambda i, k: (k // nb, 0, k % nb)),
                  pl.BlockSpec((tm, d), lambda i, k: (i, 0)),
                  pl.BlockSpec((1, d), lambda i, k: (0, 0)),
                  pl.BlockSpec((tm, d), lambda i, k: (i, 0))] + more_specs,
        out_specs=[pl.BlockSpec((tm, d), lambda i, k: (i, 0)),
                   pl.BlockSpec((1, d), lambda i, k: (0, 0))],
        out_shape=[jax.ShapeDtypeStruct((t, d), F32), jax.ShapeDtypeStruct((1, d), F32)],
        scratch_shapes=[pltpu.VMEM((tm, d), F32)],
        compiler_params=_params(("arbitrary", "arbitrary")),
    )(du, wg, h, g, dout, *more)


def _in_proj_dw(hn, du, s, name):
    t, d = hn.shape
    ns = du.shape[1] // s
    tk = _col_tile(t, (384, 192))
    tn = _col_tile(ns, (768, 384, 128))
    nb = ns // tn

    def body(hn_ref, du_ref, o_ref):
        @pl.when(pl.program_id(1) == 0)
        def _():
            o_ref[...] = jnp.zeros_like(o_ref)

        o_ref[...] += lax.dot_general(hn_ref[...], du_ref[...], TN_DIMS, preferred_element_type=F32)

    return pl.pallas_call(
        body, name=name, grid=(s * nb, t // tk),
        in_specs=[pl.BlockSpec((tk, d), lambda n, k: (k, 0)),
                  pl.BlockSpec((tk, tn), lambda n, k: (k, n))],
        out_specs=pl.BlockSpec((None, d, tn), lambda n, k: (n // nb, 0, n % nb)),
        out_shape=jax.ShapeDtypeStruct((s, d, ns), F32),
        compiler_params=_params(("arbitrary", "arbitrary")),
    )(hn, du)


def _loss_head(h, tgt, g, n_meta, t_real, name):
    t, d = h.shape
    tm = _row_tile(t)

    def body(h_ref, t_ref, g_ref, dh_ref, loss_ref, dg_ref):
        i = pl.program_id(0)

        @pl.when(i == 0)
        def _():
            loss_ref[...] = jnp.zeros_like(loss_ref)
            dg_ref[...] = jnp.zeros_like(dg_ref)

        x = h_ref[...]
        gv = g_ref[...]
        r = lax.rsqrt(jnp.mean(x * x, axis=-1, keepdims=True) + RMS_EPS)
        xr = x * r
        rows = i * tm + lax.broadcasted_iota(jnp.int32, (tm, 1), 0)
        valid = (rows >= n_meta) & (rows < t_real)
        err = jnp.where(valid, xr * gv - t_ref[...], 0.0)
        loss_ref[...] += 0.5 * jnp.sum(jnp.mean(err * err, axis=-1, keepdims=True))
        dy = err * (1.0 / d)
        gd = dy * gv
        dot = jnp.mean(gd * x, axis=-1, keepdims=True)
        dh_ref[...] = r * gd - x * ((r * r * r) * dot)
        dg_ref[...] += jnp.sum(dy * xr, axis=0, keepdims=True)

    return pl.pallas_call(
        body, name=name, grid=(t // tm,),
        in_specs=[pl.BlockSpec((tm, d), lambda i: (i, 0)),
                  pl.BlockSpec((tm, d), lambda i: (i, 0)),
                  pl.BlockSpec((1, d), lambda i: (0, 0))],
        out_specs=[pl.BlockSpec((tm, d), lambda i: (i, 0)),
                   pl.BlockSpec((1, LANES), lambda i: (0, 0)),
                   pl.BlockSpec((1, d), lambda i: (0, 0))],
        out_shape=[jax.ShapeDtypeStruct((t, d), F32), jax.ShapeDtypeStruct((1, LANES), F32),
                   jax.ShapeDtypeStruct((1, d), F32)],
        compiler_params=_params(("arbitrary",)),
    )(h, tgt, g)


def _adamw(w, g, m, v, name):
    rows, cols = w.shape
    tr = rows
    for cand in (512, 256, 128, 64, 32, 16, 8):
        if rows % cand == 0 and cand * cols * 4 <= 2 * 1024 * 1024:
            tr = cand
            break

    def body(w_ref, g_ref, m_ref, v_ref, d_ref, nm_ref, nv_ref):
        gv = g_ref[...]
        m2 = ADAM_B1 * m_ref[...] + (1.0 - ADAM_B1) * gv
        v2 = ADAM_B2 * v_ref[...] + (1.0 - ADAM_B2) * (gv * gv)
        m_hat = m2 / (1.0 - ADAM_B1 ** ADAM_STEP)
        v_hat = v2 / (1.0 - ADAM_B2 ** ADAM_STEP)
        d_ref[...] = -ADAM_LR * (m_hat / (jnp.sqrt(v_hat) + ADAM_EPS) + ADAM_WD * w_ref[...])
        nm_ref[...] = m2
        nv_ref[...] = v2

    spec = pl.BlockSpec((tr, cols), lambda i: (i, 0))
    return pl.pallas_call(
        body, name=name, grid=(rows // tr,),
        in_specs=[spec] * 4, out_specs=[spec] * 3,
        out_shape=[jax.ShapeDtypeStruct((rows, cols), F32)] * 3,
        compiler_params=_params(("arbitrary",)),
    )(w, g, m, v)


def _pair_add(x, ra, c_idx, name):
    s, _, rows, cols = x.shape
    tr = _slab_rows(rows, cols)

    def body(c_ref, x_ref, r_ref, o_ref):
        o_ref[...] = (x_ref[...] + r_ref[...]).astype(BF16)

    return pl.pallas_call(
        body, name=name,
        grid_spec=pltpu.PrefetchScalarGridSpec(
            num_scalar_prefetch=1, grid=(s, rows // tr),
            in_specs=[pl.BlockSpec((None, None, tr, cols), lambda a, i, c_ref: (a, c_ref[0], i, 0)),
                      pl.BlockSpec((None, tr, cols), lambda a, i, c_ref: (a, i, 0))],
            out_specs=pl.BlockSpec((None, tr, cols), lambda a, i, c_ref: (a, i, 0))),
        out_shape=jax.ShapeDtypeStruct((s, rows, cols), BF16),
        compiler_params=_params(("arbitrary", "arbitrary")),
    )(c_idx, x, ra)


def _chip_sum(rc, p, where, n_slots, name):
    s, rows, cols = rc.shape
    tr = _slab_rows(rows, cols)

    def body(w_ref, x_ref, p_ref, o_ref):
        me = w_ref[0]
        total = jnp.where(me == 0, p_ref[...], x_ref[0]).astype(F32)
        for a in range(1, s):
            total = total + jnp.where(me == a, p_ref[...], x_ref[a]).astype(F32)
        o_ref[...] = total

    return pl.pallas_call(
        body, name=name,
        grid_spec=pltpu.PrefetchScalarGridSpec(
            num_scalar_prefetch=1, grid=(rows // tr,),
            in_specs=[pl.BlockSpec((s, tr, cols), lambda i, w_ref: (0, i, 0)),
                      pl.BlockSpec((None, tr, cols), lambda i, w_ref: (w_ref[0], i, 0))],
            out_specs=pl.BlockSpec((None, tr, cols), lambda i, w_ref: (w_ref[1], i, 0))),
        out_shape=jax.ShapeDtypeStruct((n_slots, rows, cols), F32),
        compiler_params=_params(("arbitrary",)),
    )(where, rc, p)


def _cast_place(w, layer, me_idx, name):
    _, rows, cols = w.shape
    tr = _slab_rows(rows, cols)

    def body(m_ref, w_ref, o_ref):
        o_ref[...] = w_ref[...].astype(BF16)

    return pl.pallas_call(
        body, name=name,
        grid_spec=pltpu.PrefetchScalarGridSpec(
            num_scalar_prefetch=1, grid=(rows // tr,),
            in_specs=[pl.BlockSpec((None, tr, cols), lambda i, m_ref: (layer, i, 0))],
            out_specs=pl.BlockSpec((None, tr, cols), lambda i, m_ref: (m_ref[0], i, 0))),
        out_shape=jax.ShapeDtypeStruct((N_CHIPS, rows, cols), BF16),
        compiler_params=_params(("arbitrary",)),
    )(me_idx, w)


def _place():
    x, y, c = lax.axis_index("x"), lax.axis_index("y"), lax.axis_index("c")
    chips = [(1 - x, y), (x, 1 - y), (1 - x, 1 - y)]
    return x, y, c, chips


def _chip_index(cx, cy):
    return 2 * cx + cy


def _gather_copies(bufs, stage):
    x, y, c, chips = _place()
    me = _chip_index(x, y)
    copies = []
    for b in bufs:
        for chip in chips:
            src = _chip_index(*chip)
            if stage == 0:
                copies.append((b.at[me, c], (*chip, c), b.at[src, c]))
            else:
                copies.append((b.at[src, c], (x, y, 1 - c), b.at[src, 1 - c]))
    return copies


def _remote(ref, peer, ssem, rsem, k):
    return pltpu.make_async_remote_copy(src_ref=ref, dst_ref=ref, send_sem=ssem.at[k], recv_sem=rsem.at[k],
                                        device_id=peer, device_id_type=MESH)


def _gather_first(bufs, small):
    n = len(bufs)
    k = 3 * n

    def body(*refs):
        sm_ref = refs[n]
        b_refs, smg_ref = refs[n + 1:2 * n + 1], refs[2 * n + 1]
        lsem, ssem, rsem = refs[2 * n + 2:]
        x, y, c, chips = _place()
        me = _chip_index(x, y)
        local = pltpu.make_async_copy(sm_ref, smg_ref.at[me], lsem)
        local.start()
        first = _gather_copies(b_refs, 0)
        second = _gather_copies(b_refs, 1)
        started = []
        for i, (ref, peer, _) in enumerate(first):
            started.append(_remote(ref, peer, ssem, rsem, i))
        for j, chip in enumerate(chips):
            started.append(pltpu.make_async_remote_copy(
                src_ref=sm_ref, dst_ref=smg_ref.at[me], send_sem=ssem.at[2 * k + j], recv_sem=rsem.at[2 * k + j],
                device_id=(*chip, c), device_id_type=MESH))
        for cp in started:
            cp.start()
        for i, (_, peer, lands) in enumerate(first):
            _remote(lands, peer, ssem, rsem, i).wait_recv()
            ref, sib, _ = second[i]
            fwd = _remote(ref, sib, ssem, rsem, k + i)
            fwd.start()
            started.append(fwd)
        for i, (_, sib, lands) in enumerate(second):
            _remote(lands, sib, ssem, rsem, k + i).wait_recv()
        for j, chip in enumerate(chips):
            theirs = smg_ref.at[_chip_index(*chip)]
            pltpu.make_async_remote_copy(src_ref=theirs, dst_ref=theirs, send_sem=ssem.at[2 * k + j],
                                         recv_sem=rsem.at[2 * k + j], device_id=(*chip, c),
                                         device_id_type=MESH).wait_recv()
        for cp in started:
            cp.wait_send()
        local.wait()

    return pl.pallas_call(
        body, name="gather_first",
        in_specs=[ANY] * (n + 1), out_specs=[ANY] * (n + 1),
        out_shape=[jax.ShapeDtypeStruct(b.shape, b.dtype) for b in bufs]
        + [jax.ShapeDtypeStruct((N_CHIPS,) + small.shape, small.dtype)],
        input_output_aliases={i: i for i in range(n)},
        scratch_shapes=[pltpu.SemaphoreType.DMA, pltpu.SemaphoreType.DMA((2 * k + 3,)),
                        pltpu.SemaphoreType.DMA((2 * k + 3,))],
    )(*bufs, small)


HBM = pl.BlockSpec(memory_space=pltpu.HBM)
SEM = pl.BlockSpec(memory_space=pltpu.SEMAPHORE)
DATAFLOW = pltpu.SideEffectType.DATAFLOW_SIDE_EFFECTING


def _copies_start(bufs, plan, n_copies, name):
    n = len(bufs)

    def body(*refs):
        ssem, rsem = refs[n], refs[n + 1]
        b_refs, token = refs[n + 2:2 * n + 2], refs[2 * n + 2]
        copies = plan(b_refs)
        assert len(copies) == n_copies
        for i, (src, dst, peer, _) in enumerate(copies):
            pltpu.make_async_remote_copy(src_ref=src, dst_ref=dst, send_sem=ssem.at[i], recv_sem=rsem.at[i],
                                         device_id=peer, device_id_type=MESH).start()
        token[...] = jnp.zeros_like(token)

    return pl.pallas_call(
        body, name=name,
        out_shape=(pltpu.SemaphoreType.DMA((n_copies,)), pltpu.SemaphoreType.DMA((n_copies,)),
                   *[pltpu.HBM(b.shape, b.dtype) for b in bufs], jax.ShapeDtypeStruct((SUBLANES, LANES), F32)),
        in_specs=[HBM] * n,
        out_specs=(SEM, SEM, *[HBM] * n, pl.BlockSpec(memory_space=pltpu.VMEM)),
        input_output_aliases={i: 2 + i for i in range(n)},
        compiler_params=pltpu.CompilerParams(has_side_effects=DATAFLOW),
    )(*[pltpu.with_memory_space_constraint(b, pltpu.HBM) for b in bufs])


def _copies_wait(bufs, ssem, rsem, after, plan, name):
    n = len(bufs)

    def body(*refs):
        b_refs, ssem_ref, rsem_ref = refs[:n], refs[n], refs[n + 1]
        for i, (src, dst, peer, lands) in enumerate(plan(b_refs)):
            pltpu.make_async_remote_copy(src_ref=src, dst_ref=dst, send_sem=ssem_ref.at[i], recv_sem=rsem_ref.at[i],
                                         device_id=peer, device_id_type=MESH).wait_send()
            pltpu.make_async_remote_copy(src_ref=lands, dst_ref=lands, send_sem=ssem_ref.at[i],
                                         recv_sem=rsem_ref.at[i], device_id=peer, device_id_type=MESH).wait_recv()

    return pl.pallas_call(
        body, name=name,
        out_shape=tuple(pltpu.HBM(b.shape, b.dtype) for b in bufs),
        in_specs=[HBM] * n + [SEM, SEM, ANY], out_specs=tuple([HBM] * n),
        input_output_aliases={i: i for i in range(n)},
        compiler_params=pltpu.CompilerParams(has_side_effects=DATAFLOW),
    )(*bufs, ssem, rsem, after)


def _gather_plan(stage):
    return lambda refs: [(ref, ref, peer, lands) for ref, peer, lands in _gather_copies(refs, stage)]


def _swap_plan(refs):
    n = len(refs) // 2
    x, y, c, _ = _place()
    return [(refs[a].at[:, 1 - c], refs[n + a], (x, y, 1 - c), refs[n + a]) for a in range(n)]


def _scatter_plan(refs):
    n = len(refs) // 2
    x, y, c, chips = _place()
    me = _chip_index(x, y)
    return [(refs[a].at[_chip_index(*chip)], refs[n + a].at[me], (*chip, c), refs[n + a].at[_chip_index(*chip)])
            for a in range(n) for chip in chips]


def _pair_gather_plan(refs):
    x, y, c, _ = _place()
    return [(r.at[c], r.at[c], (x, y, 1 - c), r.at[1 - c]) for r in refs]


def _pair_swap(xs):
    n = len(xs)

    def body(*refs):
        x_refs, o_refs, ssem, rsem = refs[:n], refs[n:2 * n], refs[2 * n], refs[2 * n + 1]
        x, y, c, _ = _place()
        copies = [pltpu.make_async_remote_copy(src_ref=x_refs[a].at[:, 1 - c], dst_ref=o_refs[a],
                                               send_sem=ssem.at[a], recv_sem=rsem.at[a],
                                               device_id=(x, y, 1 - c), device_id_type=MESH) for a in range(n)]
        for cp in copies:
            cp.start()
        for cp in copies:
            cp.wait()

    return pl.pallas_call(
        body, name="pair_swap", in_specs=[ANY] * n, out_specs=[ANY] * n,
        out_shape=[jax.ShapeDtypeStruct((a.shape[0],) + a.shape[2:], a.dtype) for a in xs],
        scratch_shapes=[pltpu.SemaphoreType.DMA((n,)), pltpu.SemaphoreType.DMA((n,))],
    )(*xs)


def _chip_scatter(ps):
    n = len(ps)

    def body(*refs):
        p_refs, o_refs, ssem, rsem = refs[:n], refs[n:2 * n], refs[2 * n], refs[2 * n + 1]
        x, y, c, chips = _place()
        me = _chip_index(x, y)
        sends = []
        for a in range(n):
            for j, chip in enumerate(chips):
                sends.append(pltpu.make_async_remote_copy(
                    src_ref=p_refs[a].at[_chip_index(*chip)], dst_ref=o_refs[a].at[me],
                    send_sem=ssem.at[3 * a + j], recv_sem=rsem.at[3 * a + j],
                    device_id=(*chip, c), device_id_type=MESH))
        for cp in sends:
            cp.start()
        for a in range(n):
            for j, chip in enumerate(chips):
                src = _chip_index(*chip)
                pltpu.make_async_remote_copy(
                    src_ref=p_refs[a].at[src], dst_ref=o_refs[a].at[src],
                    send_sem=ssem.at[3 * a + j], recv_sem=rsem.at[3 * a + j],
                    device_id=(*chip, c), device_id_type=MESH).wait_recv()
        for cp in sends:
            cp.wait_send()

    return pl.pallas_call(
        body, name="chip_scatter", in_specs=[ANY] * n, out_specs=[ANY] * n,
        out_shape=[jax.ShapeDtypeStruct(a.shape, a.dtype) for a in ps],
        scratch_shapes=[pltpu.SemaphoreType.DMA((3 * n,)), pltpu.SemaphoreType.DMA((3 * n,))],
    )(*ps)


def _final_gather(fs, rep):
    n = len(fs)

    def body(*refs):
        o_refs, repo_ref = refs[n + 1:2 * n + 1], refs[2 * n + 1]
        ssem, rsem = refs[2 * n + 2:]
        x, y, c, chips = _place()
        slot = 4 * x + 2 * y + c
        copies = [pltpu.make_async_remote_copy(src_ref=o_refs[a].at[c], dst_ref=o_refs[a].at[c],
                                               send_sem=ssem.at[a], recv_sem=rsem.at[a],
                                               device_id=(x, y, 1 - c), device_id_type=MESH) for a in range(n)]
        peers = [(x, y, 1 - c)] + [(*chip, c) for chip in chips] + [(*chip, 1 - c) for chip in chips]
        for k, peer in enumerate(peers):
            copies.append(pltpu.make_async_remote_copy(src_ref=repo_ref.at[slot], dst_ref=repo_ref.at[slot],
                                                       send_sem=ssem.at[n + k], recv_sem=rsem.at[n + k],
                                                       device_id=peer, device_id_type=MESH))
        for cp in copies:
            cp.start()
        for a in range(n):
            pltpu.make_async_remote_copy(src_ref=o_refs[a].at[1 - c], dst_ref=o_refs[a].at[1 - c],
                                         send_sem=ssem.at[a], recv_sem=rsem.at[a],
                                         device_id=(x, y, 1 - c), device_id_type=MESH).wait_recv()
        for k, peer in enumerate(peers):
            px, py, pc = peer
            theirs = repo_ref.at[4 * px + 2 * py + pc]
            pltpu.make_async_remote_copy(src_ref=theirs, dst_ref=theirs, send_sem=ssem.at[n + k], recv_sem=rsem.at[n + k],
                                         device_id=peer, device_id_type=MESH).wait_recv()
        for cp in copies:
            cp.wait_send()

    return pl.pallas_call(
        body, name="final_gather", in_specs=[ANY] * (n + 1), out_specs=[ANY] * (n + 1),
        out_shape=[jax.ShapeDtypeStruct(a.shape, a.dtype) for a in fs] + [jax.ShapeDtypeStruct(rep.shape, rep.dtype)],
        input_output_aliases={k: k for k in range(n + 1)},
        scratch_shapes=[pltpu.SemaphoreType.DMA((n + 7,)), pltpu.SemaphoreType.DMA((n + 7,))],
    )(*fs, rep)


def _block_diag(w, gb):
    nh, hd, _ = w.shape
    per = gb // hd
    w4 = w.reshape(nh // per, per, hd, hd)
    eye = jnp.eye(per, dtype=w.dtype)
    return jnp.einsum("jaik,ab->jaibk", w4, eye).reshape(nh // per, gb, gb)


def _diag_blocks(dense, hd):
    nj, gb, _ = dense.shape
    per = gb // hd
    d5 = dense.reshape(nj, per, hd, per, hd)
    return jnp.stack([d5[:, a, :, a, :] for a in range(per)], axis=1).reshape(nj * per, hd, hd)


def _round_up(n, q):
    return (n + q - 1) // q * q


def kernel(x, meta, norm_g, w_in, conv_a_w, conv_a_b, lru_wr, lru_br, lru_wi, lru_bi, lru_lambda, conv_b_w, w_out, final_g, loss_target, m_meta, m_norm_g, m_w_in, m_conv_a_w, m_conv_a_b, m_lru_wr, m_lru_br, m_lru_wi, m_lru_bi, m_lru_lambda, m_conv_b_w, m_w_out, m_final_g, v_meta, v_norm_g, v_w_in, v_conv_a_w, v_conv_a_b, v_lru_wr, v_lru_br, v_lru_wi, v_lru_bi, v_lru_lambda, v_conv_b_w, v_w_out, v_final_g):
    weights = dict(meta=meta, norm_g=norm_g, w_in=w_in, conv_a_w=conv_a_w, conv_a_b=conv_a_b, lru_wr=lru_wr,
                   lru_br=lru_br, lru_wi=lru_wi, lru_bi=lru_bi, lru_lambda=lru_lambda, conv_b_w=conv_b_w,
                   w_out=w_out, final_g=final_g)
    mom1 = dict(meta=m_meta, norm_g=m_norm_g, w_in=m_w_in, conv_a_w=m_conv_a_w, conv_a_b=m_conv_a_b,
                lru_wr=m_lru_wr, lru_br=m_lru_br, lru_wi=m_lru_wi, lru_bi=m_lru_bi, lru_lambda=m_lru_lambda,
                conv_b_w=m_conv_b_w, w_out=m_w_out, final_g=m_final_g)
    mom2 = dict(meta=v_meta, norm_g=v_norm_g, w_in=v_w_in, conv_a_w=v_conv_a_w, conv_a_b=v_conv_a_b,
                lru_wr=v_lru_wr, lru_br=v_lru_br, lru_wi=v_lru_wi, lru_bi=v_lru_bi, lru_lambda=v_lru_lambda,
                conv_b_w=v_conv_b_w, w_out=v_w_out, final_g=v_final_g)
    names = list(weights)

    assert x.shape[0] == 1
    seq, d = x.shape[1], x.shape[2]
    n_meta, ds = meta.shape
    depth = norm_g.shape[0]
    c = lru_lambda.shape[1]
    nh, hd = lru_wr.shape[1], lru_wr.shape[2]
    ns = w_in.shape[2]
    dms = w_out.shape[1]
    cs = conv_a_w.shape[2]
    ka, kb = conv_a_w.shape[1], conv_b_w.shape[1]
    s = N_CHIPS
    assert depth == N_CORES and d == s * ds and c == s * cs and s * ns == 6 * c and s * dms == 2 * c
    gb = min(GATE_BLOCK, c)
    t_real = n_meta + seq
    t = _round_up(t_real, ROW_QUANTUM)
    my_c = lax.axis_index("c").astype(jnp.int32)
    my_chip = (2 * lax.axis_index("x") + lax.axis_index("y")).astype(jnp.int32)
    c_idx = my_c.reshape(1)
    chip_idx = my_chip.reshape(1)

    sm_rows = _round_up(n_meta + depth * SUBLANES, 2 * SUBLANES)
    small = jnp.zeros((sm_rows, ds), F32)
    small = small.at[0:n_meta, :].set(meta)
    for l in range(depth):
        base = n_meta + l * SUBLANES
        small = small.at[base:base + ka, 0:cs].set(conv_a_w[l])
        small = small.at[base + ka:base + ka + kb, 0:cs].set(conv_b_w[l])
    win_b = [_cast_place(w_in, l, chip_idx, f"cast_w_in_{l}").reshape(s, 2, d // 2, ns) for l in range(depth)]
    wout_b = [_cast_place(w_out, l, chip_idx, f"cast_w_out_{l}").reshape(s, 2, dms // 2, d) for l in range(depth)]
    win_b[0], wout_b[0], small_g = _gather_first([win_b[0], wout_b[0]], small)
    later = [win_b[1], wout_b[1]]
    ssem, rsem, *later, token = _copies_start(later, _gather_plan(0), 3 * len(later), "gather_ici_start")
    meta_full = jnp.transpose(small_g[:, 0:n_meta, :], (1, 0, 2)).reshape(n_meta, d)
    wa_full, wb_full = [], []
    for l in range(depth):
        base = n_meta + l * SUBLANES
        wa_full.append(jnp.transpose(small_g[:, base:base + ka, 0:cs], (1, 0, 2)).reshape(ka, c))
        wb_full.append(jnp.transpose(small_g[:, base + ka:base + ka + kb, 0:cs], (1, 0, 2)).reshape(kb, c))

    h = jnp.concatenate([meta_full, x[0], jnp.zeros((t - t_real, d), F32)], axis=0)
    tgt = jnp.concatenate([jnp.zeros((n_meta, d), F32), loss_target[0], jnp.zeros((t - t_real, d), F32)], axis=0)
    layer_w = []
    for l in range(depth):
        layer_w.append(dict(
            g=norm_g[l].reshape(1, d), wa=wa_full[l], ba=conv_a_b[l].reshape(1, c),
            wr=_block_diag(lru_wr[l], gb).astype(BF16), br=lru_br[l].reshape(1, c),
            wi=_block_diag(lru_wi[l], gb).astype(BF16), bi=lru_bi[l].reshape(1, c),
            lam=lru_lambda[l].reshape(1, c), wb=wb_full[l]))
    saved = []
    for l, lw in enumerate(layer_w):
        first = l == 0
        lw["win"] = win_b[l].reshape(s, d, ns)
        lw["wout"] = wout_b[l].reshape(2 * c, d)
        u, hn = _norm_in(h, lw["g"] + token[0, 0] if first else lw["g"], lw["win"], f"norm_in_{l}")
        if first:
            later = _copies_wait(later, ssem, rsem, u, _gather_plan(0), "gather_ici_wait")
            ssem, rsem, *later, token = _copies_start(later, _gather_plan(1), 3 * len(later), "gather_d2d_start")
        y, hs = _mix_fwd(u, lw["wa"], lw["ba"] + token[0, 0] if first else lw["ba"], lw["wr"], lw["br"], lw["wi"],
                         lw["bi"], lw["lam"], lw["wb"], f"mix_fwd_{l}")
        if first:
            win_b[1], wout_b[1] = _copies_wait(later, ssem, rsem, y, _gather_plan(1), "gather_d2d_wait")
        saved.append((h, u, hn, y, hs))
        h = _out_proj(h, y, lw["wout"], f"out_proj_{l}")
    dh, loss_lanes, d_final_g = _loss_head(h, tgt, final_g.reshape(1, d), n_meta, t_real, "loss_head")
    loss = lax.psum(loss_lanes[0, 0], ("x", "y", "c"))

    to_core = jnp.stack([my_chip, my_c])
    grads = [None] * depth
    early = None
    for l in reversed(range(depth)):
        lw = layer_w[l]
        h_in, u, hn, y, hs = saved[l]
        token = early[-1] if early else None
        dy = _out_proj_dy(dh, lw["wout"], f"out_proj_dy_{l}", after=token)
        d_wout = _out_proj_dw(y, dh, f"out_proj_dw_{l}")
        if early:
            ssem, rsem, bufs, _ = early
            bufs = _copies_wait(bufs, ssem, rsem, d_wout, _swap_plan, "early_swap_wait")
            half = len(bufs) // 2
            sums = [_pair_add(a, b, c_idx, f"early_pair_add_{k}") for k, (a, b) in enumerate(zip(bufs[:half], bufs[half:]))]
            lands = [lax.empty(p.shape, p.dtype) for p in sums]
            ssem, rsem, *bufs, token = _copies_start(sums + lands, _scatter_plan, 3 * half, "early_scatter_start")
        du, dsm, d_wr, d_wi = _mix_bwd(u, hs, dy, lw["wa"], lw["ba"], lw["wr"], lw["br"], lw["wi"], lw["bi"],
                                       lw["lam"], lw["wb"], f"mix_bwd_{l}", after=token)
        if early:
            bufs = _copies_wait(bufs, ssem, rsem, du, _scatter_plan, "early_scatter_wait")
            halves = [_chip_sum(rc, p, to_core, N_CORES, f"early_chip_sum_{k}")
                      for k, (p, rc) in enumerate(zip(bufs[:half], bufs[half:]))]
            ssem, rsem, *bufs, token = _copies_start(halves, _pair_gather_plan, half, "early_gather_start")
        dh, d_g = _in_proj_bwd(du, lw["win"], h_in, lw["g"], dh, f"in_proj_bwd_{l}", after=token)
        if early:
            early_full = _copies_wait(bufs, ssem, rsem, dh, _pair_gather_plan, "early_gather_wait")
        d_win = _in_proj_dw(hn, du, s, f"in_proj_dw_{l}")
        grads[l] = dict(win=d_win, wout=d_wout, dsm=dsm, wr=_diag_blocks(d_wr, hd), wi=_diag_blocks(d_wi, hd), g=d_g)
        if l == depth - 1:
            srcs = [d_win.reshape(s, 2, d // 2, ns), d_wout.reshape(s, 2, dms // 2, d)]
            lands = [lax.empty((a.shape[0],) + a.shape[2:], a.dtype) for a in srcs]
            ssem, rsem, *bufs, token = _copies_start(srcs + lands, _swap_plan, len(srcs), "early_swap_start")
            early = (ssem, rsem, bufs, token)
        else:
            early = None
    grad_x = dh[n_meta:t_real][None]

    sharded = [grads[0]["win"].reshape(s, 2, d // 2, ns), grads[0]["wout"].reshape(s, 2, dms // 2, d)]
    sp = jnp.zeros((sm_rows, s, ds), F32)
    sp = sp.at[0:n_meta].set(dh[0:n_meta].reshape(n_meta, s, ds))
    for l in range(depth):
        base = n_meta + l * SUBLANES
        dsm = grads[l]["dsm"]
        sp = sp.at[base:base + ka, :, 0:cs].set(dsm[ROW_DWA:ROW_DWA + ka].reshape(ka, s, cs))
        sp = sp.at[base + ka:base + ka + kb, :, 0:cs].set(dsm[ROW_DWB:ROW_DWB + kb].reshape(kb, s, cs))
    sharded.append(jnp.transpose(sp, (1, 0, 2)).reshape(s, 2, sm_rows // 2, ds))
    rep_parts = [jnp.concatenate([grads[l]["g"].reshape(-1) for l in range(depth)]), d_final_g.reshape(-1)]
    for row in (ROW_DBA, ROW_DBR, ROW_DBI, ROW_DLAM):
        rep_parts.append(jnp.concatenate([grads[l]["dsm"][row] for l in range(depth)]))
    rep_parts.append(jnp.concatenate([grads[l]["wr"].reshape(-1) for l in range(depth)]))
    rep_parts.append(jnp.concatenate([grads[l]["wi"].reshape(-1) for l in range(depth)]))
    rep_sizes = [p.shape[0] for p in rep_parts]
    piece = _round_up(-(-sum(rep_sizes) // (s * 2)), 2 * SUBLANES * LANES)
    flat = jnp.concatenate(rep_parts + [jnp.zeros((s * 2 * piece - sum(rep_sizes),), F32)])
    sharded.append(flat.reshape(s, 2, piece // LANES, LANES))

    from_sibling = _pair_swap(sharded)
    pair_sums = [_pair_add(a, b, c_idx, f"pair_add_{k}") for k, (a, b) in enumerate(zip(sharded, from_sibling))]
    by_chip = _chip_scatter(pair_sums)
    to_device = jnp.stack([my_chip, 2 * my_chip + my_c])
    reduced = [_chip_sum(a, p, to_core, N_CORES, f"chip_sum_{k}")
               for k, (a, p) in enumerate(zip(by_chip[:-1], pair_sums[:-1]))]
    reduced_rep = _chip_sum(by_chip[-1], pair_sums[-1], to_device, N_CHIPS * N_CORES, "chip_sum_rep")
    *full, rep_all = _final_gather(reduced, reduced_rep)

    g_win = [full[0].reshape(d, ns), early_full[0].reshape(d, ns)]
    g_wout = [full[1].reshape(dms, d), early_full[1].reshape(dms, d)]
    g_sp = full[2].reshape(sm_rows, ds)
    rep_flat = rep_all.reshape(-1)
    rep_out, off = [], 0
    for n in rep_sizes:
        rep_out.append(rep_flat[off:off + n])
        off += n
    grad = dict(
        meta=g_sp[0:n_meta],
        norm_g=rep_out[0].reshape(depth, d),
        w_in=jnp.stack(g_win),
        conv_a_w=jnp.stack([g_sp[n_meta + l * SUBLANES:n_meta + l * SUBLANES + ka, 0:cs] for l in range(depth)]),
        conv_a_b=rep_out[2].reshape(depth, c),
        lru_wr=rep_out[6].reshape(depth, nh, hd, hd),
        lru_br=rep_out[3].reshape(depth, c),
        lru_wi=rep_out[7].reshape(depth, nh, hd, hd),
        lru_bi=rep_out[4].reshape(depth, c),
        lru_lambda=rep_out[5].reshape(depth, c),
        conv_b_w=jnp.stack([g_sp[n_meta + l * SUBLANES + ka:n_meta + l * SUBLANES + ka + kb, 0:cs]
                            for l in range(depth)]),
        w_out=jnp.stack(g_wout),
        final_g=rep_out[1].reshape(d),
    )

    delta, new_m, new_v = {}, {}, {}
    for n in names:
        shape = weights[n].shape
        two_d = (-1, shape[-1]) if len(shape) > 1 else (1, -1)
        if n in ("lru_wr", "lru_wi"):
            two_d = (-1, LANES)
        out = _adamw(weights[n].reshape(two_d), grad[n].reshape(two_d), mom1[n].reshape(two_d),
                     mom2[n].reshape(two_d), f"adamw_{n}")
        delta[n], new_m[n], new_v[n] = (o.reshape(shape) for o in out)

    return (loss, grad_x, *[grad[n] for n in names], *[delta[n] for n in names],
            *[new_m[n] for n in names], *[new_v[n] for n in names])
```

```python
import functools

import jax
import jax.numpy as jnp
from jax import lax
from jax.experimental import pallas as pl
from jax.experimental.pallas import tpu as pltpu

F32 = jnp.float32
BF16 = jnp.bfloat16

RMS_EPS = 1e-6
LRU_C = 8.0
ADAM_LR = 0.001
ADAM_B1 = 0.9
ADAM_B2 = 0.999
ADAM_EPS = 1e-08
ADAM_WD = 0.01
ADAM_STEP = 10

N_CHIPS = 4
N_CORES = 2
VMEM_LIMIT_BYTES = 56 * 1024 * 1024
SUBLANES = 8
LANES = 128
ROW_QUANTUM = 384
MIX_CHUNK = 192
GATE_BLOCK = 256
MESH = pl.DeviceIdType.MESH
ANY = pl.BlockSpec(memory_space=pl.ANY)

NT_DIMS = (((1,), (1,)), ((), ()))
TN_DIMS = (((0,), (0,)), ((), ()))


def _params(sem):
    return pltpu.CompilerParams(dimension_semantics=sem, vmem_limit_bytes=VMEM_LIMIT_BYTES)


def _sig(x):
    return 1.0 / (1.0 + jnp.exp(-x))


def _row_tile(t):
    return 704 if t % 704 == 0 else 192


def _col_tile(n, prefs):
    for p in prefs:
        if n % p == 0:
            return p
    return n


def _slab_rows(rows, cols):
    if rows * cols * 4 <= 1024 * 1024:
        return rows
    return _col_tile(rows, (256, 128, 64, 32, 16))


def _norm_in(h, g, wg, name):
    t, d = h.shape
    s, _, ns = wg.shape
    tm = _row_tile(t)
    tn = _col_tile(ns, (1536, 512, 384, 128))
    nb = ns // tn

    def body(h_ref, g_ref, w_ref, u_ref, hn_ref):
        @pl.when(pl.program_id(1) == 0)
        def _():
            x = h_ref[...]
            r = lax.rsqrt(jnp.mean(x * x, axis=-1, keepdims=True) + RMS_EPS)
            hn_ref[...] = ((x * r) * g_ref[...]).astype(BF16)

        u_ref[...] = jnp.dot(hn_ref[...], w_ref[...], preferred_element_type=F32)

    return pl.pallas_call(
        body, name=name, grid=(t // tm, s * nb),
        in_specs=[pl.BlockSpec((tm, d), lambda i, n: (i, 0)),
                  pl.BlockSpec((1, d), lambda i, n: (0, 0)),
                  pl.BlockSpec((None, d, tn), lambda i, n: (n // nb, 0, n % nb))],
        out_specs=[pl.BlockSpec((tm, tn), lambda i, n: (i, n)),
                   pl.BlockSpec((tm, d), lambda i, n: (i, 0))],
        out_shape=[jax.ShapeDtypeStruct((t, s * ns), F32), jax.ShapeDtypeStruct((t, d), BF16)],
        compiler_params=_params(("arbitrary", "arbitrary")),
    )(h, g, wg)


def _decay_consts(lam):
    z = -lam
    e = jnp.exp(-jnp.abs(z))
    u = 1.0 + e
    log1p_e = jnp.where(u == 1.0, e, jnp.log(u) * (e / (u - 1.0)))
    sp = jnp.maximum(z, 0.0) + log1p_e
    return -LRU_C * sp, LRU_C * _sig(z)


def _gates(xc, wr_ref, br_ref, wi_ref, bi_ref, c8, j, gb):
    sl = slice(j * gb, (j + 1) * gb)
    x16 = xc.astype(BF16)
    r = _sig(jnp.dot(x16, wr_ref[j], preferred_element_type=F32) + br_ref[:, sl])
    ig = _sig(jnp.dot(x16, wi_ref[j], preferred_element_type=F32) + bi_ref[:, sl])
    la = c8[:, sl] * r
    a = jnp.exp(la)
    sq = jnp.sqrt(-jnp.tanh(la) * (a * a + 1.0))
    return r, ig, a, sq


def _mix_fwd(u, wa, ba, wr, br, wi, bi, lam, wb, name):
    t = u.shape[0]
    c = u.shape[1] // 6
    tc = MIX_CHUNK
    gb = wr.shape[1]
    nblk = c // gb
    ka, kb = wa.shape[0], wb.shape[0]

    def body(u_ref, wa_ref, ba_ref, wr_ref, br_ref, wi_ref, bi_ref, lam_ref, wb_ref,
             y_ref, hs_ref, xa_ext, v_ext, xc_s, a_s, b_s, carry_s):
        @pl.when(pl.program_id(0) == 0)
        def _():
            xa_ext[0:SUBLANES, :] = jnp.zeros((SUBLANES, c), F32)
            v_ext[0:SUBLANES, :] = jnp.zeros((SUBLANES, c), F32)
            carry_s[...] = jnp.zeros_like(carry_s)

        xa_ext[SUBLANES:SUBLANES + tc, :] = u_ref[:, 0:c]
        xc = ba_ref[...]
        for k in range(ka):
            xc = xc + wa_ref[pl.ds(k, 1), :] * xa_ext[pl.ds(SUBLANES - (ka - 1) + k, tc), :]
        xc_s[...] = xc
        c8, _ = _decay_consts(lam_ref[...])
        for j in range(nblk):
            sl = slice(j * gb, (j + 1) * gb)
            xcj = xc_s[:, sl]
            _, ig, a, sq = _gates(xcj, wr_ref, br_ref, wi_ref, bi_ref, c8, j, gb)
            a_s[:, sl] = a
            b_s[:, sl] = sq * (ig * xcj)

        row = lax.broadcasted_iota(jnp.int32, (SUBLANES, c), 0)

        def scan_step(j, _):
            off = pl.multiple_of(j * SUBLANES, SUBLANES)
            av = a_s[pl.ds(off, SUBLANES), :]
            bv = b_s[pl.ds(off, SUBLANES), :]
            for d in (1, 2, 4):
                keep = row >= d
                bsh = jnp.where(keep, pltpu.roll(bv, d, axis=0), 0.0)
                ash = jnp.where(keep, pltpu.roll(av, d, axis=0), 1.0)
                bv = av * bsh + bv
                av = av * ash
            hv = av * carry_s[...] + bv
            hs_ref[pl.ds(off, SUBLANES), :] = hv
            carry_s[...] = hs_ref[pl.ds(off + SUBLANES - 1, 1), :]
            return 0

        lax.fori_loop(0, tc // SUBLANES, scan_step, 0)

        ga = u_ref[:, c:2 * c]
        y_ref[:, 0:c] = (hs_ref[...] * (ga * _sig(ga))).astype(BF16)

        v_ext[SUBLANES:SUBLANES + tc, :] = u_ref[:, 3 * c:4 * c] * u_ref[:, 4 * c:5 * c]
        cv = wb_ref[pl.ds(0, 1), :] * v_ext[pl.ds(SUBLANES - (kb - 1), tc), :]
        for k in range(1, kb):
            cv = cv + wb_ref[pl.ds(k, 1), :] * v_ext[pl.ds(SUBLANES - (kb - 1) + k, tc), :]
        gbv = u_ref[:, 5 * c:6 * c]
        y_ref[:, c:2 * c] = (u_ref[:, 2 * c:3 * c] * cv * (gbv * _sig(gbv))).astype(BF16)

        xa_ext[0:SUBLANES, :] = xa_ext[tc:tc + SUBLANES, :]
        v_ext[0:SUBLANES, :] = v_ext[tc:tc + SUBLANES, :]

    full = lambda shape: pl.BlockSpec(shape, lambda i: (0,) * len(shape))
    return pl.pallas_call(
        body, name=name, grid=(t // tc,),
        in_specs=[pl.BlockSpec((tc, 6 * c), lambda i: (i, 0)),
                  full(wa.shape), full(ba.shape), full(wr.shape), full(br.shape),
                  full(wi.shape), full(bi.shape), full(lam.shape), full(wb.shape)],
        out_specs=[pl.BlockSpec((tc, 2 * c), lambda i: (i, 0)),
                   pl.BlockSpec((tc, c), lambda i: (i, 0))],
        out_shape=[jax.ShapeDtypeStruct((t, 2 * c), BF16), jax.ShapeDtypeStruct((t, c), F32)],
        scratch_shapes=[pltpu.VMEM((tc + SUBLANES, c), F32), pltpu.VMEM((tc + SUBLANES, c), F32),
                        pltpu.VMEM((tc, c), F32), pltpu.VMEM((tc, c), F32), pltpu.VMEM((tc, c), F32),
                        pltpu.VMEM((1, c), F32)],
        compiler_params=_params(("arbitrary",)),
    )(u, wa, ba, wr, br, wi, bi, lam, wb)


ROW_DWA = 0
ROW_DBA = 4
ROW_DBR = 5
ROW_DBI = 6
ROW_DLAM = 7
ROW_DWB = 8
SMALL_ROWS = 16


def _mix_bwd(u, hs, dy, wa, ba, wr, br, wi, bi, lam, wb, name, after=None):
    t = u.shape[0]
    c = u.shape[1] // 6
    tc = MIX_CHUNK
    nt = t // tc
    gb = wr.shape[1]
    nblk = c // gb
    ka, kb = wa.shape[0], wb.shape[0]
    assert ka <= ROW_DBA and kb <= SMALL_ROWS - ROW_DWB
    hb = tc // SUBLANES

    def body(u_ref, uh_ref, hs_ref, hsh_ref, dy_ref, wa_ref, ba_ref, wr_ref, br_ref, wi_ref, bi_ref, lam_ref, wb_ref,
             du_ref, dsm_ref, dwr_ref, dwi_ref,
             xa_ext, v_ext, hs_ext, a_ext, ds_ext, dxc_ext, dcv_ext, xc_s, r_s, i_s, sq_s, g_s, an_s):
        i = pl.program_id(0)
        chunk = nt - 1 - i
        tail = slice(tc, tc + SUBLANES)
        head = slice(0, SUBLANES)

        @pl.when(i == 0)
        def _():
            zero = jnp.zeros((SUBLANES, c), F32)
            a_ext[tail, :] = zero
            ds_ext[tail, :] = zero
            dxc_ext[tail, :] = zero
            dcv_ext[tail, :] = zero
            dsm_ref[...] = jnp.zeros_like(dsm_ref)
            dwr_ref[...] = jnp.zeros_like(dwr_ref)
            dwi_ref[...] = jnp.zeros_like(dwi_ref)

        prev = jnp.where(chunk > 0, 1.0, 0.0)
        xa_ext[head, :] = uh_ref[:, 0:c] * prev
        xa_ext[SUBLANES:SUBLANES + tc, :] = u_ref[:, 0:c]
        v_ext[head, :] = uh_ref[:, 3 * c:4 * c] * uh_ref[:, 4 * c:5 * c] * prev
        v_ext[SUBLANES:SUBLANES + tc, :] = u_ref[:, 3 * c:4 * c] * u_ref[:, 4 * c:5 * c]
        hs_ext[head, :] = hsh_ref[...] * prev
        hs_ext[SUBLANES:SUBLANES + tc, :] = hs_ref[...]

        xc = ba_ref[...]
        for k in range(ka):
            xc = xc + wa_ref[pl.ds(k, 1), :] * xa_ext[pl.ds(SUBLANES - (ka - 1) + k, tc), :]
        xc_s[...] = xc
        c8, dc8 = _decay_consts(lam_ref[...])
        for j in range(nblk):
            sl = slice(j * gb, (j + 1) * gb)
            r, ig, a, sq = _gates(xc_s[:, sl], wr_ref, br_ref, wi_ref, bi_ref, c8, j, gb)
            r_s[:, sl] = r
            i_s[:, sl] = ig
            sq_s[:, sl] = sq
            a_ext[0:tc, sl] = a

        ga = u_ref[:, c:2 * c]
        sga = _sig(ga)
        g_s[...] = dy_ref[:, 0:c] * (ga * sga)
        an_s[...] = a_ext[pl.ds(1, tc), :]

        row = lax.broadcasted_iota(jnp.int32, (SUBLANES, c), 0)

        def scan_step(j, _):
            off = pl.multiple_of(tc - SUBLANES - j * SUBLANES, SUBLANES)
            av = an_s[pl.ds(off, SUBLANES), :]
            bv = g_s[pl.ds(off, SUBLANES), :]
            for d in (1, 2, 4):
                keep = row < SUBLANES - d
                bsh = jnp.where(keep, pltpu.roll(bv, SUBLANES - d, axis=0), 0.0)
                ash = jnp.where(keep, pltpu.roll(av, SUBLANES - d, axis=0), 1.0)
                bv = av * bsh + bv
                av = av * ash
            ds_ext[pl.ds(off, SUBLANES), :] = av * ds_ext[pl.ds(off + SUBLANES, 1), :] + bv
            return 0

        lax.fori_loop(0, tc // SUBLANES, scan_step, 0)

        def acc(row_index, val):
            dsm_ref[pl.ds(row_index, 1), :] += jnp.sum(val, axis=0, keepdims=True)

        def acc_block(row_index, sl, val):
            dsm_ref[pl.ds(row_index, 1), sl] += jnp.sum(val, axis=0, keepdims=True)

        for j in range(nblk):
            sl = slice(j * gb, (j + 1) * gb)
            ds = ds_ext[0:tc, sl]
            hprev = hs_ext[pl.ds(SUBLANES - 1, tc), sl]
            a = a_ext[0:tc, sl]
            sq = sq_s[:, sl]
            ig = i_s[:, sl]
            r = r_s[:, sl]
            xcj = xc_s[:, sl]
            t1 = ds * xcj
            dla = (ds * hprev) * a - (t1 * ig) * ((a * a) / sq)
            acc_block(ROW_DLAM, sl, dla * r)
            dpr = (dla * c8[:, sl]) * (r * (1.0 - r))
            dpi = (t1 * sq) * (ig * (1.0 - ig))
            acc_block(ROW_DBR, sl, dpr)
            acc_block(ROW_DBI, sl, dpi)
            p16 = dpr.astype(BF16)
            q16 = dpi.astype(BF16)
            x16 = xcj.astype(BF16)
            dwr_ref[j] += lax.dot_general(x16, p16, TN_DIMS, preferred_element_type=F32)
            dwi_ref[j] += lax.dot_general(x16, q16, TN_DIMS, preferred_element_type=F32)
            dxc = (ds * (sq * ig)
                   + lax.dot_general(p16, wr_ref[j], NT_DIMS, preferred_element_type=F32)
                   + lax.dot_general(q16, wi_ref[j], NT_DIMS, preferred_element_type=F32))
            dxc_ext[0:tc, sl] = dxc
            acc_block(ROW_DBA, sl, dxc)

        dsilu_a = sga * (1.0 + ga * (1.0 - sga))
        du_ref[:, c:2 * c] = (dy_ref[:, 0:c] * hs_ref[...] * dsilu_a).astype(BF16)

        dxc = dxc_ext[0:tc, :]
        dxa = wa_ref[pl.ds(ka - 1, 1), :] * dxc
        acc(ROW_DWA + ka - 1, dxc * xa_ext[SUBLANES:SUBLANES + tc, :])
        for k in range(ka - 1):
            acc(ROW_DWA + k, dxc * xa_ext[pl.ds(SUBLANES - (ka - 1) + k, tc), :])
            dxa = dxa + wa_ref[pl.ds(k, 1), :] * dxc_ext[pl.ds(ka - 1 - k, tc), :]
        du_ref[:, 0:c] = dxa.astype(BF16)

        cv = wb_ref[pl.ds(0, 1), :] * v_ext[pl.ds(SUBLANES - (kb - 1), tc), :]
        for k in range(1, kb):
            cv = cv + wb_ref[pl.ds(k, 1), :] * v_ext[pl.ds(SUBLANES - (kb - 1) + k, tc), :]
        gbv = u_ref[:, 5 * c:6 * c]
        sgb = _sig(gbv)
        silu_b = gbv * sgb
        dyb = dy_ref[:, c:2 * c]
        gB = u_ref[:, 2 * c:3 * c]
        du_ref[:, 2 * c:3 * c] = (dyb * cv * silu_b).astype(BF16)
        du_ref[:, 5 * c:6 * c] = (dyb * gB * cv * (sgb * (1.0 + gbv * (1.0 - sgb)))).astype(BF16)
        dcv = dyb * gB * silu_b
        dcv_ext[0:tc, :] = dcv
        dv = wb_ref[pl.ds(kb - 1, 1), :] * dcv
        acc(ROW_DWB + kb - 1, dcv * v_ext[SUBLANES:SUBLANES + tc, :])
        for k in range(kb - 1):
            acc(ROW_DWB + k, dcv * v_ext[pl.ds(SUBLANES - (kb - 1) + k, tc), :])
            dv = dv + wb_ref[pl.ds(k, 1), :] * dcv_ext[pl.ds(kb - 1 - k, tc), :]
        du_ref[:, 3 * c:4 * c] = (dv * u_ref[:, 4 * c:5 * c]).astype(BF16)
        du_ref[:, 4 * c:5 * c] = (dv * u_ref[:, 3 * c:4 * c]).astype(BF16)

        a_ext[tail, :] = a_ext[head, :]
        ds_ext[tail, :] = ds_ext[head, :]
        dxc_ext[tail, :] = dxc_ext[head, :]
        dcv_ext[tail, :] = dcv_ext[head, :]

        @pl.when(i == nt - 1)
        def _():
            dsm_ref[pl.ds(ROW_DLAM, 1), :] = dsm_ref[pl.ds(ROW_DLAM, 1), :] * dc8

    full = lambda shape: pl.BlockSpec(shape, lambda i: (0,) * len(shape))
    rev = lambda i: (nt - 1 - i, 0)
    halo = lambda i: (jnp.maximum((nt - 1 - i) * hb - 1, 0), 0)
    ext = pltpu.VMEM((tc + SUBLANES, c), F32)
    blk = pltpu.VMEM((tc, c), F32)
    body, more_specs, more = _behind(body, 13, after)
    return pl.pallas_call(
        body, name=name, grid=(nt,),
        in_specs=[pl.BlockSpec((tc, 6 * c), rev), pl.BlockSpec((SUBLANES, 6 * c), halo),
                  pl.BlockSpec((tc, c), rev), pl.BlockSpec((SUBLANES, c), halo),
                  pl.BlockSpec((tc, 2 * c), rev),
                  full(wa.shape), full(ba.shape), full(wr.shape), full(br.shape),
                  full(wi.shape), full(bi.shape), full(lam.shape), full(wb.shape)] + more_specs,
        out_specs=[pl.BlockSpec((tc, 6 * c), rev), full((SMALL_ROWS, c)), full(wr.shape), full(wi.shape)],
        out_shape=[jax.ShapeDtypeStruct((t, 6 * c), BF16), jax.ShapeDtypeStruct((SMALL_ROWS, c), F32),
                   jax.ShapeDtypeStruct(wr.shape, F32), jax.ShapeDtypeStruct(wi.shape, F32)],
        scratch_shapes=[ext] * 7 + [blk] * 6,
        compiler_params=_params(("arbitrary",)),
    )(u, u, hs, hs, dy, wa, ba, wr, br, wi, bi, lam, wb, *more)


def _out_proj(h, y, w, name):
    t, d = h.shape
    dm = y.shape[1]
    tm = _row_tile(t)
    tn = _col_tile(d, (1024, 512, 256))

    def body(h_ref, y_ref, w_ref, o_ref):
        o_ref[...] = h_ref[...] + jnp.dot(y_ref[...], w_ref[...], preferred_element_type=F32)

    return pl.pallas_call(
        body, name=name, grid=(d // tn, t // tm),
        in_specs=[pl.BlockSpec((tm, tn), lambda n, i: (i, n)),
                  pl.BlockSpec((tm, dm), lambda n, i: (i, 0)),
                  pl.BlockSpec((dm, tn), lambda n, i: (0, n))],
        out_specs=pl.BlockSpec((tm, tn), lambda n, i: (i, n)),
        out_shape=jax.ShapeDtypeStruct((t, d), F32),
        compiler_params=_params(("arbitrary", "arbitrary")),
    )(h, y, w)


def _behind(body, n_in, after):
    if after is None:
        return body, [], []
    return (lambda *refs: body(*refs[:n_in], *refs[n_in + 1:])), [ANY], [after]


def _out_proj_dy(dout, w, name, after=None):
    t, d = dout.shape
    dm = w.shape[0]
    tm = _row_tile(t)
    tn = _col_tile(dm, (1024, 512, 256))

    def body(g_ref, w_ref, o_ref):
        o_ref[...] = lax.dot_general(g_ref[...].astype(BF16), w_ref[...], NT_DIMS, preferred_element_type=F32)

    body, more_specs, more = _behind(body, 2, after)
    return pl.pallas_call(
        body, name=name, grid=(dm // tn, t // tm),
        in_specs=[pl.BlockSpec((tm, d), lambda n, i: (i, 0)),
                  pl.BlockSpec((tn, d), lambda n, i: (n, 0))] + more_specs,
        out_specs=pl.BlockSpec((tm, tn), lambda n, i: (i, n)),
        out_shape=jax.ShapeDtypeStruct((t, dm), F32),
        compiler_params=_params(("arbitrary", "arbitrary")),
    )(dout, w, *more)


def _out_proj_dw(y, dout, name):
    t, dm = y.shape
    d = dout.shape[1]
    tmm = _col_tile(dm, (512, 256))
    tn = _col_tile(d, (512, 256))

    def body(y_ref, g_ref, o_ref):
        o_ref[...] = lax.dot_general(y_ref[...], g_ref[...].astype(BF16), TN_DIMS, preferred_element_type=F32)

    return pl.pallas_call(
        body, name=name, grid=(d // tn, dm // tmm),
        in_specs=[pl.BlockSpec((t, tmm), lambda n, m: (0, m)),
                  pl.BlockSpec((t, tn), lambda n, m: (0, n))],
        out_specs=pl.BlockSpec((tmm, tn), lambda n, m: (m, n)),
        out_shape=jax.ShapeDtypeStruct((dm, d), F32),
        compiler_params=_params(("arbitrary", "arbitrary")),
    )(y, dout)


def _in_proj_bwd(du, wg, h, g, dout, name, after=None):
    t, d = h.shape
    s, _, ns = wg.shape
    tm = _row_tile(t)
    tk = _col_tile(ns, (512, 384, 128))
    nb = ns // tk
    nk = s * nb

    def body(du_ref, w_ref, h_ref, g_ref, dout_ref, dh_ref, dg_ref, acc_ref):
        i, k = pl.program_id(0), pl.program_id(1)

        @pl.when(k == 0)
        def _():
            acc_ref[...] = jnp.zeros_like(acc_ref)

        @pl.when((k == 0) & (i == 0))
        def _():
            dg_ref[...] = jnp.zeros_like(dg_ref)

        acc_ref[...] += lax.dot_general(du_ref[...], w_ref[...], NT_DIMS, preferred_element_type=F32)

        @pl.when(k == nk - 1)
        def _():
            x = h_ref[...]
            dhn = acc_ref[...]
            r = lax.rsqrt(jnp.mean(x * x, axis=-1, keepdims=True) + RMS_EPS)
            gd = dhn * g_ref[...]
            dot = jnp.mean(gd * x, axis=-1, keepdims=True)
            dh_ref[...] = dout_ref[...] + (r * gd - x * ((r * r * r) * dot))
            dg_ref[...] += jnp.sum(dhn * (x * r), axis=0, keepdims=True)

    body, more_specs, more = _behind(body, 5, after)
    return pl.pallas_call(
        body, name=name, grid=(t // tm, nk),
        in_specs=[pl.BlockSpec((tm, tk), lambda i, k: (i, k)),
                  pl.BlockSpec((None, d, tk), lambda i, k: (k // nb, 0, k % nb)),
                  pl.BlockSpec((tm, d), lambda i, k: (i, 0)),
                  pl.BlockSpec((1, d), lambda i, k: (0, 0)),
                  pl.BlockSpec((tm, d), lambda i, k: (i, 0))] + more_specs,
        out_specs=[pl.BlockSpec((tm, d), lambda i, k: (i, 0)),
                   pl.BlockSpec((1, d), lambda i, k: (0, 0))],
        out_shape=[jax.ShapeDtypeStruct((t, d), F32), jax.ShapeDtypeStruct((1, d), F32)],
        scratch_shapes=[pltpu.VMEM((tm, d), F32)],
        compiler_params=_params(("arbitrary", "arbitrary")),
    )(du, wg, h, g, dout, *more)


def _in_proj_dw(hn, du, s, name):
    t, d = hn.shape
    ns = du.shape[1] // s
    tmm = _col_tile(d, (512, 256))
    tn = _col_tile(ns, (768, 384, 128))
    nb = ns // tn

    def body(hn_ref, du_ref, o_ref):
        o_ref[...] = lax.dot_general(hn_ref[...], du_ref[...], TN_DIMS, preferred_element_type=F32)

    return pl.pallas_call(
        body, name=name, grid=(s * nb, d // tmm),
        in_specs=[pl.BlockSpec((t, tmm), lambda n, m: (0, m)),
                  pl.BlockSpec((t, tn), lambda n, m: (0, n))],
        out_specs=pl.BlockSpec((None, tmm, tn), lambda n, m: (n // nb, m, n % nb)),
        out_shape=jax.ShapeDtypeStruct((s, d, ns), F32),
        compiler_params=_params(("arbitrary", "arbitrary")),
    )(hn, du)


def _loss_head(h, tgt, g, n_meta, t_real, name):
    t, d = h.shape
    tm = _row_tile(t)

    def body(h_ref, t_ref, g_ref, dh_ref, loss_ref, dg_ref):
        i = pl.program_id(0)

        @pl.when(i == 0)
        def _():
            loss_ref[...] = jnp.zeros_like(loss_ref)
            dg_ref[...] = jnp.zeros_like(dg_ref)

        x = h_ref[...]
        gv = g_ref[...]
        r = lax.rsqrt(jnp.mean(x * x, axis=-1, keepdims=True) + RMS_EPS)
        xr = x * r
        rows = i * tm + lax.broadcasted_iota(jnp.int32, (tm, 1), 0)
        valid = (rows >= n_meta) & (rows < t_real)
        err = jnp.where(valid, xr * gv - t_ref[...], 0.0)
        loss_ref[...] += 0.5 * jnp.sum(jnp.mean(err * err, axis=-1, keepdims=True))
        dy = err * (1.0 / d)
        gd = dy * gv
        dot = jnp.mean(gd * x, axis=-1, keepdims=True)
        dh_ref[...] = r * gd - x * ((r * r * r) * dot)
        dg_ref[...] += jnp.sum(dy * xr, axis=0, keepdims=True)

    return pl.pallas_call(
        body, name=name, grid=(t // tm,),
        in_specs=[pl.BlockSpec((tm, d), lambda i: (i, 0)),
                  pl.BlockSpec((tm, d), lambda i: (i, 0)),
                  pl.BlockSpec((1, d), lambda i: (0, 0))],
        out_specs=[pl.BlockSpec((tm, d), lambda i: (i, 0)),
                   pl.BlockSpec((1, LANES), lambda i: (0, 0)),
                   pl.BlockSpec((1, d), lambda i: (0, 0))],
        out_shape=[jax.ShapeDtypeStruct((t, d), F32), jax.ShapeDtypeStruct((1, LANES), F32),
                   jax.ShapeDtypeStruct((1, d), F32)],
        compiler_params=_params(("arbitrary",)),
    )(h, tgt, g)


def _adamw(w, g, m, v, name):
    rows, cols = w.shape
    tr = rows
    for cand in (512, 256, 128, 64, 32, 16, 8):
        if rows % cand == 0 and cand * cols * 4 <= 2 * 1024 * 1024:
            tr = cand
            break

    def body(w_ref, g_ref, m_ref, v_ref, d_ref, nm_ref, nv_ref):
        gv = g_ref[...]
        m2 = ADAM_B1 * m_ref[...] + (1.0 - ADAM_B1) * gv
        v2 = ADAM_B2 * v_ref[...] + (1.0 - ADAM_B2) * (gv * gv)
        m_hat = m2 / (1.0 - ADAM_B1 ** ADAM_STEP)
        v_hat = v2 / (1.0 - ADAM_B2 ** ADAM_STEP)
        d_ref[...] = -ADAM_LR * (m_hat / (jnp.sqrt(v_hat) + ADAM_EPS) + ADAM_WD * w_ref[...])
        nm_ref[...] = m2
        nv_ref[...] = v2

    spec = pl.BlockSpec((tr, cols), lambda i: (i, 0))
    return pl.pallas_call(
        body, name=name, grid=(rows // tr,),
        in_specs=[spec] * 4, out_specs=[spec] * 3,
        out_shape=[jax.ShapeDtypeStruct((rows, cols), F32)] * 3,
        compiler_params=_params(("arbitrary",)),
    )(w, g, m, v)


def _pair_add(x, ra, c_idx, name):
    s, _, rows, cols = x.shape
    tr = _slab_rows(rows, cols)

    def body(c_ref, x_ref, r_ref, o_ref):
        o_ref[...] = (x_ref[...] + r_ref[...]).astype(BF16)

    return pl.pallas_call(
        body, name=name,
        grid_spec=pltpu.PrefetchScalarGridSpec(
            num_scalar_prefetch=1, grid=(s, rows // tr),
            in_specs=[pl.BlockSpec((None, None, tr, cols), lambda a, i, c_ref: (a, c_ref[0], i, 0)),
                      pl.BlockSpec((None, tr, cols), lambda a, i, c_ref: (a, i, 0))],
            out_specs=pl.BlockSpec((None, tr, cols), lambda a, i, c_ref: (a, i, 0))),
        out_shape=jax.ShapeDtypeStruct((s, rows, cols), BF16),
        compiler_params=_params(("arbitrary", "arbitrary")),
    )(c_idx, x, ra)


def _chip_sum(rc, p, where, n_slots, name):
    s, rows, cols = rc.shape
    tr = _slab_rows(rows, cols)

    def body(w_ref, x_ref, p_ref, o_ref):
        me = w_ref[0]
        total = jnp.where(me == 0, p_ref[...], x_ref[0]).astype(F32)
        for a in range(1, s):
            total = total + jnp.where(me == a, p_ref[...], x_ref[a]).astype(F32)
        o_ref[...] = total

    return pl.pallas_call(
        body, name=name,
        grid_spec=pltpu.PrefetchScalarGridSpec(
            num_scalar_prefetch=1, grid=(rows // tr,),
            in_specs=[pl.BlockSpec((s, tr, cols), lambda i, w_ref: (0, i, 0)),
                      pl.BlockSpec((None, tr, cols), lambda i, w_ref: (w_ref[0], i, 0))],
            out_specs=pl.BlockSpec((None, tr, cols), lambda i, w_ref: (w_ref[1], i, 0))),
        out_shape=jax.ShapeDtypeStruct((n_slots, rows, cols), F32),
        compiler_params=_params(("arbitrary",)),
    )(where, rc, p)


def _cast_place(w, layer, me_idx, name):
    _, rows, cols = w.shape
    tr = _slab_rows(rows, cols)

    def body(m_ref, w_ref, o_ref):
        o_ref[...] = w_ref[...].astype(BF16)

    return pl.pallas_call(
        body, name=name,
        grid_spec=pltpu.PrefetchScalarGridSpec(
            num_scalar_prefetch=1, grid=(rows // tr,),
            in_specs=[pl.BlockSpec((None, tr, cols), lambda i, m_ref: (layer, i, 0))],
            out_specs=pl.BlockSpec((None, tr, cols), lambda i, m_ref: (m_ref[0], i, 0))),
        out_shape=jax.ShapeDtypeStruct((N_CHIPS, rows, cols), BF16),
        compiler_params=_params(("arbitrary",)),
    )(me_idx, w)


def _place():
    x, y, c = lax.axis_index("x"), lax.axis_index("y"), lax.axis_index("c")
    chips = [(1 - x, y), (x, 1 - y), (1 - x, 1 - y)]
    return x, y, c, chips


def _chip_index(cx, cy):
    return 2 * cx + cy


def _gather_copies(bufs, stage):
    x, y, c, chips = _place()
    me = _chip_index(x, y)
    copies = []
    for b in bufs:
        for chip in chips:
            src = _chip_index(*chip)
            if stage == 0:
                copies.append((b.at[me, c], (*chip, c), b.at[src, c]))
            else:
                copies.append((b.at[src, c], (x, y, 1 - c), b.at[src, 1 - c]))
    return copies


def _remote(ref, peer, ssem, rsem, k):
    return pltpu.make_async_remote_copy(src_ref=ref, dst_ref=ref, send_sem=ssem.at[k], recv_sem=rsem.at[k],
                                        device_id=peer, device_id_type=MESH)


def _gather_first(bufs, small):
    n = len(bufs)
    k = 3 * n

    def body(*refs):
        sm_ref = refs[n]
        b_refs, smg_ref = refs[n + 1:2 * n + 1], refs[2 * n + 1]
        lsem, ssem, rsem = refs[2 * n + 2:]
        x, y, c, chips = _place()
        me = _chip_index(x, y)
        local = pltpu.make_async_copy(sm_ref, smg_ref.at[me], lsem)
        local.start()
        first = _gather_copies(b_refs, 0)
        second = _gather_copies(b_refs, 1)
        started = []
        for i, (ref, peer, _) in enumerate(first):
            started.append(_remote(ref, peer, ssem, rsem, i))
        for j, chip in enumerate(chips):
            started.append(pltpu.make_async_remote_copy(
                src_ref=sm_ref, dst_ref=smg_ref.at[me], send_sem=ssem.at[2 * k + j], recv_sem=rsem.at[2 * k + j],
                device_id=(*chip, c), device_id_type=MESH))
        for cp in started:
            cp.start()
        for i, (_, peer, lands) in enumerate(first):
            _remote(lands, peer, ssem, rsem, i).wait_recv()
            ref, sib, _ = second[i]
            fwd = _remote(ref, sib, ssem, rsem, k + i)
            fwd.start()
            started.append(fwd)
        for i, (_, sib, lands) in enumerate(second):
            _remote(lands, sib, ssem, rsem, k + i).wait_recv()
        for j, chip in enumerate(chips):
            theirs = smg_ref.at[_chip_index(*chip)]
            pltpu.make_async_remote_copy(src_ref=theirs, dst_ref=theirs, send_sem=ssem.at[2 * k + j],
                                         recv_sem=rsem.at[2 * k + j], device_id=(*chip, c),
                                         device_id_type=MESH).wait_recv()
        for cp in started:
            cp.wait_send()
        local.wait()

    return pl.pallas_call(
        body, name="gather_first",
        in_specs=[ANY] * (n + 1), out_specs=[ANY] * (n + 1),
        out_shape=[jax.ShapeDtypeStruct(b.shape, b.dtype) for b in bufs]
        + [jax.ShapeDtypeStruct((N_CHIPS,) + small.shape, small.dtype)],
        input_output_aliases={i: i for i in range(n)},
        scratch_shapes=[pltpu.SemaphoreType.DMA, pltpu.SemaphoreType.DMA((2 * k + 3,)),
                        pltpu.SemaphoreType.DMA((2 * k + 3,))],
    )(*bufs, small)


HBM = pl.BlockSpec(memory_space=pltpu.HBM)
SEM = pl.BlockSpec(memory_space=pltpu.SEMAPHORE)
DATAFLOW = pltpu.SideEffectType.DATAFLOW_SIDE_EFFECTING


def _copies_start(bufs, plan, n_copies, name):
    n = len(bufs)

    def body(*refs):
        ssem, rsem = refs[n], refs[n + 1]
        b_refs, token = refs[n + 2:2 * n + 2], refs[2 * n + 2]
        copies = plan(b_refs)
        assert len(copies) == n_copies
        for i, (src, dst, peer, _) in enumerate(copies):
            pltpu.make_async_remote_copy(src_ref=src, dst_ref=dst, send_sem=ssem.at[i], recv_sem=rsem.at[i],
                                         device_id=peer, device_id_type=MESH).start()
        token[...] = jnp.zeros_like(token)

    return pl.pallas_call(
        body, name=name,
        out_shape=(pltpu.SemaphoreType.DMA((n_copies,)), pltpu.SemaphoreType.DMA((n_copies,)),
                   *[pltpu.HBM(b.shape, b.dtype) for b in bufs], jax.ShapeDtypeStruct((SUBLANES, LANES), F32)),
        in_specs=[HBM] * n,
        out_specs=(SEM, SEM, *[HBM] * n, pl.BlockSpec(memory_space=pltpu.VMEM)),
        input_output_aliases={i: 2 + i for i in range(n)},
        compiler_params=pltpu.CompilerParams(has_side_effects=DATAFLOW),
    )(*[pltpu.with_memory_space_constraint(b, pltpu.HBM) for b in bufs])


def _copies_wait(bufs, ssem, rsem, after, plan, name):
    n = len(bufs)

    def body(*refs):
        b_refs, ssem_ref, rsem_ref = refs[:n], refs[n], refs[n + 1]
        for i, (src, dst, peer, lands) in enumerate(plan(b_refs)):
            pltpu.make_async_remote_copy(src_ref=src, dst_ref=dst, send_sem=ssem_ref.at[i], recv_sem=rsem_ref.at[i],
                                         device_id=peer, device_id_type=MESH).wait_send()
            pltpu.make_async_remote_copy(src_ref=lands, dst_ref=lands, send_sem=ssem_ref.at[i],
                                         recv_sem=rsem_ref.at[i], device_id=peer, device_id_type=MESH).wait_recv()

    return pl.pallas_call(
        body, name=name,
        out_shape=tuple(pltpu.HBM(b.shape, b.dtype) for b in bufs),
        in_specs=[HBM] * n + [SEM, SEM, ANY], out_specs=tuple([HBM] * n),
        input_output_aliases={i: i for i in range(n)},
        compiler_params=pltpu.CompilerParams(has_side_effects=DATAFLOW),
    )(*bufs, ssem, rsem, after)


def _gather_plan(stage):
    return lambda refs: [(ref, ref, peer, lands) for ref, peer, lands in _gather_copies(refs, stage)]


def _swap_plan(refs):
    n = len(refs) // 2
    x, y, c, _ = _place()
    return [(refs[a].at[:, 1 - c], refs[n + a], (x, y, 1 - c), refs[n + a]) for a in range(n)]


def _scatter_plan(refs):
    n = len(refs) // 2
    x, y, c, chips = _place()
    me = _chip_index(x, y)
    return [(refs[a].at[_chip_index(*chip)], refs[n + a].at[me], (*chip, c), refs[n + a].at[_chip_index(*chip)])
            for a in range(n) for chip in chips]


def _pair_gather_plan(refs):
    x, y, c, _ = _place()
    return [(r.at[c], r.at[c], (x, y, 1 - c), r.at[1 - c]) for r in refs]


def _pair_swap(xs):
    n = len(xs)

    def body(*refs):
        x_refs, o_refs, ssem, rsem = refs[:n], refs[n:2 * n], refs[2 * n], refs[2 * n + 1]
        x, y, c, _ = _place()
        copies = [pltpu.make_async_remote_copy(src_ref=x_refs[a].at[:, 1 - c], dst_ref=o_refs[a],
                                               send_sem=ssem.at[a], recv_sem=rsem.at[a],
                                               device_id=(x, y, 1 - c), device_id_type=MESH) for a in range(n)]
        for cp in copies:
            cp.start()
        for cp in copies:
            cp.wait()

    return pl.pallas_call(
        body, name="pair_swap", in_specs=[ANY] * n, out_specs=[ANY] * n,
        out_shape=[jax.ShapeDtypeStruct((a.shape[0],) + a.shape[2:], a.dtype) for a in xs],
        scratch_shapes=[pltpu.SemaphoreType.DMA((n,)), pltpu.SemaphoreType.DMA((n,))],
    )(*xs)


def _chip_scatter(ps):
    n = len(ps)

    def body(*refs):
        p_refs, o_refs, ssem, rsem = refs[:n], refs[n:2 * n], refs[2 * n], refs[2 * n + 1]
        x, y, c, chips = _place()
        me = _chip_index(x, y)
        sends = []
        for a in range(n):
            for j, chip in enumerate(chips):
                sends.append(pltpu.make_async_remote_copy(
                    src_ref=p_refs[a].at[_chip_index(*chip)], dst_ref=o_refs[a].at[me],
                    send_sem=ssem.at[3 * a + j], recv_sem=rsem.at[3 * a + j],
                    device_id=(*chip, c), device_id_type=MESH))
        for cp in sends:
            cp.start()
        for a in range(n):
            for j, chip in enumerate(chips):
                src = _chip_index(*chip)
                pltpu.make_async_remote_copy(
                    src_ref=p_refs[a].at[src], dst_ref=o_refs[a].at[src],
                    send_sem=ssem.at[3 * a + j], recv_sem=rsem.at[3 * a + j],
                    device_id=(*chip, c), device_id_type=MESH).wait_recv()
        for cp in sends:
            cp.wait_send()

    return pl.pallas_call(
        body, name="chip_scatter", in_specs=[ANY] * n, out_specs=[ANY] * n,
        out_shape=[jax.ShapeDtypeStruct(a.shape, a.dtype) for a in ps],
        scratch_shapes=[pltpu.SemaphoreType.DMA((3 * n,)), pltpu.SemaphoreType.DMA((3 * n,))],
    )(*ps)


def _final_gather(fs, rep):
    n = len(fs)

    def body(*refs):
        o_refs, repo_ref = refs[n + 1:2 * n + 1], refs[2 * n + 1]
        ssem, rsem = refs[2 * n + 2:]
        x, y, c, chips = _place()
        slot = 4 * x + 2 * y + c
        copies = [pltpu.make_async_remote_copy(src_ref=o_refs[a].at[c], dst_ref=o_refs[a].at[c],
                                               send_sem=ssem.at[a], recv_sem=rsem.at[a],
                                               device_id=(x, y, 1 - c), device_id_type=MESH) for a in range(n)]
        peers = [(x, y, 1 - c)] + [(*chip, c) for chip in chips] + [(*chip, 1 - c) for chip in chips]
        for k, peer in enumerate(peers):
            copies.append(pltpu.make_async_remote_copy(src_ref=repo_ref.at[slot], dst_ref=repo_ref.at[slot],
                                                       send_sem=ssem.at[n + k], recv_sem=rsem.at[n + k],
                                                       device_id=peer, device_id_type=MESH))
        for cp in copies:
            cp.start()
        for a in range(n):
            pltpu.make_async_remote_copy(src_ref=o_refs[a].at[1 - c], dst_ref=o_refs[a].at[1 - c],
                                         send_sem=ssem.at[a], recv_sem=rsem.at[a],
                                         device_id=(x, y, 1 - c), device_id_type=MESH).wait_recv()
        for k, peer in enumerate(peers):
            px, py, pc = peer
            theirs = repo_ref.at[4 * px + 2 * py + pc]
            pltpu.make_async_remote_copy(src_ref=theirs, dst_ref=theirs, send_sem=ssem.at[n + k], recv_sem=rsem.at[n + k],
                                         device_id=peer, device_id_type=MESH).wait_recv()
        for cp in copies:
            cp.wait_send()

    return pl.pallas_call(
        body, name="final_gather", in_specs=[ANY] * (n + 1), out_specs=[ANY] * (n + 1),
        out_shape=[jax.ShapeDtypeStruct(a.shape, a.dtype) for a in fs] + [jax.ShapeDtypeStruct(rep.shape, rep.dtype)],
        input_output_aliases={k: k for k in range(n + 1)},
        scratch_shapes=[pltpu.SemaphoreType.DMA((n + 7,)), pltpu.SemaphoreType.DMA((n + 7,))],
    )(*fs, rep)


def _block_diag(w, gb):
    nh, hd, _ = w.shape
    per = gb // hd
    w4 = w.reshape(nh // per, per, hd, hd)
    eye = jnp.eye(per, dtype=w.dtype)
    return jnp.einsum("jaik,ab->jaibk", w4, eye).reshape(nh // per, gb, gb)


def _diag_blocks(dense, hd):
    nj, gb, _ = dense.shape
    per = gb // hd
    d5 = dense.reshape(nj, per, hd, per, hd)
    return jnp.stack([d5[:, a, :, a, :] for a in range(per)], axis=1).reshape(nj * per, hd, hd)


def _round_up(n, q):
    return (n + q - 1) // q * q


def kernel(x, meta, norm_g, w_in, conv_a_w, conv_a_b, lru_wr, lru_br, lru_wi, lru_bi, lru_lambda, conv_b_w, w_out, final_g, loss_target, m_meta, m_norm_g, m_w_in, m_conv_a_w, m_conv_a_b, m_lru_wr, m_lru_br, m_lru_wi, m_lru_bi, m_lru_lambda, m_conv_b_w, m_w_out, m_final_g, v_meta, v_norm_g, v_w_in, v_conv_a_w, v_conv_a_b, v_lru_wr, v_lru_br, v_lru_wi, v_lru_bi, v_lru_lambda, v_conv_b_w, v_w_out, v_final_g):
    weights = dict(meta=meta, norm_g=norm_g, w_in=w_in, conv_a_w=conv_a_w, conv_a_b=conv_a_b, lru_wr=lru_wr,
                   lru_br=lru_br, lru_wi=lru_wi, lru_bi=lru_bi, lru_lambda=lru_lambda, conv_b_w=conv_b_w,
                   w_out=w_out, final_g=final_g)
    mom1 = dict(meta=m_meta, norm_g=m_norm_g, w_in=m_w_in, conv_a_w=m_conv_a_w, conv_a_b=m_conv_a_b,
                lru_wr=m_lru_wr, lru_br=m_lru_br, lru_wi=m_lru_wi, lru_bi=m_lru_bi, lru_lambda=m_lru_lambda,
                conv_b_w=m_conv_b_w, w_out=m_w_out, final_g=m_final_g)
    mom2 = dict(meta=v_meta, norm_g=v_norm_g, w_in=v_w_in, conv_a_w=v_conv_a_w, conv_a_b=v_conv_a_b,
                lru_wr=v_lru_wr, lru_br=v_lru_br, lru_wi=v_lru_wi, lru_bi=v_lru_bi, lru_lambda=v_lru_lambda,
                conv_b_w=v_conv_b_w, w_out=v_w_out, final_g=v_final_g)
    names = list(weights)

    assert x.shape[0] == 1
    seq, d = x.shape[1], x.shape[2]
    n_meta, ds = meta.shape
    depth = norm_g.shape[0]
    c = lru_lambda.shape[1]
    nh, hd = lru_wr.shape[1], lru_wr.shape[2]
    ns = w_in.shape[2]
    dms = w_out.shape[1]
    cs = conv_a_w.shape[2]
    ka, kb = conv_a_w.shape[1], conv_b_w.shape[1]
    s = N_CHIPS
    assert depth == N_CORES and d == s * ds and c == s * cs and s * ns == 6 * c and s * dms == 2 * c
    gb = min(GATE_BLOCK, c)
    t_real = n_meta + seq
    t = _round_up(t_real, ROW_QUANTUM)
    my_c = lax.axis_index("c").astype(jnp.int32)
    my_chip = (2 * lax.axis_index("x") + lax.axis_index("y")).astype(jnp.int32)
    c_idx = my_c.reshape(1)
    chip_idx = my_chip.reshape(1)

    sm_rows = _round_up(n_meta + depth * SUBLANES, 2 * SUBLANES)
    small = jnp.zeros((sm_rows, ds), F32)
    small = small.at[0:n_meta, :].set(meta)
    for l in range(depth):
        base = n_meta + l * SUBLANES
        small = small.at[base:base + ka, 0:cs].set(conv_a_w[l])
        small = small.at[base + ka:base + ka + kb, 0:cs].set(conv_b_w[l])
    win_b = [_cast_place(w_in, l, chip_idx, f"cast_w_in_{l}").reshape(s, 2, d // 2, ns) for l in range(depth)]
    wout_b = [_cast_place(w_out, l, chip_idx, f"cast_w_out_{l}").reshape(s, 2, dms // 2, d) for l in range(depth)]
    win_b[0], wout_b[0], small_g = _gather_first([win_b[0], wout_b[0]], small)
    later = [win_b[1], wout_b[1]]
    ssem, rsem, *later, token = _copies_start(later, _gather_plan(0), 3 * len(later), "gather_ici_start")
    meta_full = jnp.transpose(small_g[:, 0:n_meta, :], (1, 0, 2)).reshape(n_meta, d)
    wa_full, wb_full = [], []
    for l in range(depth):
        base = n_meta + l * SUBLANES
        wa_full.append(jnp.transpose(small_g[:, base:base + ka, 0:cs], (1, 0, 2)).reshape(ka, c))
        wb_full.append(jnp.transpose(small_g[:, base + ka:base + ka + kb, 0:cs], (1, 0, 2)).reshape(kb, c))

    h = jnp.concatenate([meta_full, x[0], jnp.zeros((t - t_real, d), F32)], axis=0)
    tgt = jnp.concatenate([jnp.zeros((n_meta, d), F32), loss_target[0], jnp.zeros((t - t_real, d), F32)], axis=0)
    layer_w = []
    for l in range(depth):
        layer_w.append(dict(
            g=norm_g[l].reshape(1, d), wa=wa_full[l], ba=conv_a_b[l].reshape(1, c),
            wr=_block_diag(lru_wr[l], gb).astype(BF16), br=lru_br[l].reshape(1, c),
            wi=_block_diag(lru_wi[l], gb).astype(BF16), bi=lru_bi[l].reshape(1, c),
            lam=lru_lambda[l].reshape(1, c), wb=wb_full[l]))
    saved = []
    for l, lw in enumerate(layer_w):
        first = l == 0
        lw["win"] = win_b[l].reshape(s, d, ns)
        lw["wout"] = wout_b[l].reshape(2 * c, d)
        u, hn = _norm_in(h, lw["g"] + token[0, 0] if first else lw["g"], lw["win"], f"norm_in_{l}")
        if first:
            later = _copies_wait(later, ssem, rsem, u, _gather_plan(0), "gather_ici_wait")
            ssem, rsem, *later, token = _copies_start(later, _gather_plan(1), 3 * len(later), "gather_d2d_start")
        y, hs = _mix_fwd(u, lw["wa"], lw["ba"] + token[0, 0] if first else lw["ba"], lw["wr"], lw["br"], lw["wi"],
                         lw["bi"], lw["lam"], lw["wb"], f"mix_fwd_{l}")
        if first:
            win_b[1], wout_b[1] = _copies_wait(later, ssem, rsem, y, _gather_plan(1), "gather_d2d_wait")
        saved.append((h, u, hn, y, hs))
        h = _out_proj(h, y, lw["wout"], f"out_proj_{l}")
    dh, loss_lanes, d_final_g = _loss_head(h, tgt, final_g.reshape(1, d), n_meta, t_real, "loss_head")
    loss = lax.psum(loss_lanes[0, 0], ("x", "y", "c"))

    to_core = jnp.stack([my_chip, my_c])
    grads = [None] * depth
    early = None
    for l in reversed(range(depth)):
        lw = layer_w[l]
        h_in, u, hn, y, hs = saved[l]
        token = early[-1] if early else None
        dy = _out_proj_dy(dh, lw["wout"], f"out_proj_dy_{l}", after=token)
        d_wout = _out_proj_dw(y, dh, f"out_proj_dw_{l}")
        if early:
            ssem, rsem, bufs, _ = early
            bufs = _copies_wait(bufs, ssem, rsem, d_wout, _swap_plan, "early_swap_wait")
            half = len(bufs) // 2
            sums = [_pair_add(a, b, c_idx, f"early_pair_add_{k}") for k, (a, b) in enumerate(zip(bufs[:half], bufs[half:]))]
            lands = [lax.empty(p.shape, p.dtype) for p in sums]
            ssem, rsem, *bufs, token = _copies_start(sums + lands, _scatter_plan, 3 * half, "early_scatter_start")
        du, dsm, d_wr, d_wi = _mix_bwd(u, hs, dy, lw["wa"], lw["ba"], lw["wr"], lw["br"], lw["wi"], lw["bi"],
                                       lw["lam"], lw["wb"], f"mix_bwd_{l}", after=token)
        if early:
            bufs = _copies_wait(bufs, ssem, rsem, du, _scatter_plan, "early_scatter_wait")
            halves = [_chip_sum(rc, p, to_core, N_CORES, f"early_chip_sum_{k}")
                      for k, (p, rc) in enumerate(zip(bufs[:half], bufs[half:]))]
            ssem, rsem, *bufs, token = _copies_start(halves, _pair_gather_plan, half, "early_gather_start")
        dh, d_g = _in_proj_bwd(du, lw["win"], h_in, lw["g"], dh, f"in_proj_bwd_{l}", after=token)
        if early:
            early_full = _copies_wait(bufs, ssem, rsem, dh, _pair_gather_plan, "early_gather_wait")
        d_win = _in_proj_dw(hn, du, s, f"in_proj_dw_{l}")
        grads[l] = dict(win=d_win, wout=d_wout, dsm=dsm, wr=_diag_blocks(d_wr, hd), wi=_diag_blocks(d_wi, hd), g=d_g)
        if l == depth - 1:
            srcs = [d_win.reshape(s, 2, d // 2, ns), d_wout.reshape(s, 2, dms // 2, d)]
            lands = [lax.empty((a.shape[0],) + a.shape[2:], a.dtype) for a in srcs]
            ssem, rsem, *bufs, token = _copies_start(srcs + lands, _swap_plan, len(srcs), "early_swap_start")
            early = (ssem, rsem, bufs, token)
        else:
            early = None
    grad_x = dh[n_meta:t_real][None]

    sharded = [grads[0]["win"].reshape(s, 2, d // 2, ns), grads[0]["wout"].reshape(s, 2, dms // 2, d)]
    sp = jnp.zeros((sm_rows, s, ds), F32)
    sp = sp.at[0:n_meta].set(dh[0:n_meta].reshape(n_meta, s, ds))
    for l in range(depth):
        base = n_meta + l * SUBLANES
        dsm = grads[l]["dsm"]
        sp = sp.at[base:base + ka, :, 0:cs].set(dsm[ROW_DWA:ROW_DWA + ka].reshape(ka, s, cs))
        sp = sp.at[base + ka:base + ka + kb, :, 0:cs].set(dsm[ROW_DWB:ROW_DWB + kb].reshape(kb, s, cs))
    sharded.append(jnp.transpose(sp, (1, 0, 2)).reshape(s, 2, sm_rows // 2, ds))
    rep_parts = [jnp.concatenate([grads[l]["g"].reshape(-1) for l in range(depth)]), d_final_g.reshape(-1)]
    for row in (ROW_DBA, ROW_DBR, ROW_DBI, ROW_DLAM):
        rep_parts.append(jnp.concatenate([grads[l]["dsm"][row] for l in range(depth)]))
    rep_parts.append(jnp.concatenate([grads[l]["wr"].reshape(-1) for l in range(depth)]))
    rep_parts.append(jnp.concatenate([grads[l]["wi"].reshape(-1) for l in range(depth)]))
    rep_sizes = [p.shape[0] for p in rep_parts]
    piece = _round_up(-(-sum(rep_sizes) // (s * 2)), 2 * SUBLANES * LANES)
    flat = jnp.concatenate(rep_parts + [jnp.zeros((s * 2 * piece - sum(rep_sizes),), F32)])
    sharded.append(flat.reshape(s, 2, piece // LANES, LANES))

    from_sibling = _pair_swap(sharded)
    pair_sums = [_pair_add(a, b, c_idx, f"pair_add_{k}") for k, (a, b) in enumerate(zip(sharded, from_sibling))]
    by_chip = _chip_scatter(pair_sums)
    to_device = jnp.stack([my_chip, 2 * my_chip + my_c])
    reduced = [_chip_sum(a, p, to_core, N_CORES, f"chip_sum_{k}")
               for k, (a, p) in enumerate(zip(by_chip[:-1], pair_sums[:-1]))]
    reduced_rep = _chip_sum(by_chip[-1], pair_sums[-1], to_device, N_CHIPS * N_CORES, "chip_sum_rep")
    *full, rep_all = _final_gather(reduced, reduced_rep)

    g_win = [full[0].reshape(d, ns), early_full[0].reshape(d, ns)]
    g_wout = [full[1].reshape(dms, d), early_full[1].reshape(dms, d)]
    g_sp = full[2].reshape(sm_rows, ds)
    rep_flat = rep_all.reshape(-1)
    rep_out, off = [], 0
    for n in rep_sizes:
        rep_out.append(rep_flat[off:off + n])
        off += n
    grad = dict(
        meta=g_sp[0:n_meta],
        norm_g=rep_out[0].reshape(depth, d),
        w_in=jnp.stack(g_win),
        conv_a_w=jnp.stack([g_sp[n_meta + l * SUBLANES:n_meta + l * SUBLANES + ka, 0:cs] for l in range(depth)]),
        conv_a_b=rep_out[2].reshape(depth, c),
        lru_wr=rep_out[6].reshape(depth, nh, hd, hd),
        lru_br=rep_out[3].reshape(depth, c),
        lru_wi=rep_out[7].reshape(depth, nh, hd, hd),
        lru_bi=rep_out[4].reshape(depth, c),
        lru_lambda=rep_out[5].reshape(depth, c),
        conv_b_w=jnp.stack([g_sp[n_meta + l * SUBLANES + ka:n_meta + l * SUBLANES + ka + kb, 0:cs]
                            for l in range(depth)]),
        w_out=jnp.stack(g_wout),
        final_g=rep_out[1].reshape(d),
    )

    delta, new_m, new_v = {}, {}, {}
    for n in names:
        shape = weights[n].shape
        two_d = (-1, shape[-1]) if len(shape) > 1 else (1, -1)
        if n in ("lru_wr", "lru_wi"):
            two_d = (-1, LANES)
        out = _adamw(weights[n].reshape(two_d), grad[n].reshape(two_d), mom1[n].reshape(two_d),
                     mom2[n].reshape(two_d), f"adamw_{n}")
        delta[n], new_m[n], new_v[n] = (o.reshape(shape) for o in out)

    return (loss, grad_x, *[grad[n] for n in names], *[delta[n] for n in names],
            *[new_m[n] for n in names], *[new_v[n] for n in names])
```

```python
import functools

import jax
import jax.numpy as jnp
from jax import lax
from jax.experimental import pallas as pl
from jax.experimental.pallas import tpu as pltpu

F32 = jnp.float32
BF16 = jnp.bfloat16

RMS_EPS = 1e-6
LRU_C = 8.0
ADAM_LR = 0.001
ADAM_B1 = 0.9
ADAM_B2 = 0.999
ADAM_EPS = 1e-08
ADAM_WD = 0.01
ADAM_STEP = 10

N_CHIPS = 4
N_CORES = 2
VMEM_LIMIT_BYTES = 56 * 1024 * 1024
SUBLANES = 8
LANES = 128
ROW_QUANTUM = 384
MIX_CHUNK = 192
GATE_BLOCK = 256
MESH = pl.DeviceIdType.MESH
ANY = pl.BlockSpec(memory_space=pl.ANY)

NT_DIMS = (((1,), (1,)), ((), ()))
TN_DIMS = (((0,), (0,)), ((), ()))


def _params(sem):
    return pltpu.CompilerParams(dimension_semantics=sem, vmem_limit_bytes=VMEM_LIMIT_BYTES)


def _sig(x):
    return 0.5 * jnp.tanh(0.5 * x) + 0.5


def _row_tile(t):
    return 704 if t % 704 == 0 else 192


def _col_tile(n, prefs):
    for p in prefs:
        if n % p == 0:
            return p
    return n


def _slab_rows(rows, cols):
    if rows * cols * 4 <= 1024 * 1024:
        return rows
    return _col_tile(rows, (256, 128, 64, 32, 16))


def _norm_in(h, g, wg, name):
    t, d = h.shape
    s, _, ns = wg.shape
    tm = _row_tile(t)
    tn = _col_tile(ns, (1536, 512, 384, 128))
    nb = ns // tn

    def body(h_ref, g_ref, w_ref, u_ref, hn_ref):
        @pl.when(pl.program_id(1) == 0)
        def _():
            x = h_ref[...]
            r = lax.rsqrt(jnp.mean(x * x, axis=-1, keepdims=True) + RMS_EPS)
            hn_ref[...] = ((x * r) * g_ref[...]).astype(BF16)

        u_ref[...] = jnp.dot(hn_ref[...], w_ref[...], preferred_element_type=F32)

    return pl.pallas_call(
        body, name=name, grid=(t // tm, s * nb),
        in_specs=[pl.BlockSpec((tm, d), lambda i, n: (i, 0)),
                  pl.BlockSpec((1, d), lambda i, n: (0, 0)),
                  pl.BlockSpec((None, d, tn), lambda i, n: (n // nb, 0, n % nb))],
        out_specs=[pl.BlockSpec((tm, tn), lambda i, n: (i, n)),
                   pl.BlockSpec((tm, d), lambda i, n: (i, 0))],
        out_shape=[jax.ShapeDtypeStruct((t, s * ns), F32), jax.ShapeDtypeStruct((t, d), BF16)],
        compiler_params=_params(("arbitrary", "arbitrary")),
    )(h, g, wg)


def _decay_consts(lam):
    z = -lam
    e = jnp.exp(-jnp.abs(z))
    u = 1.0 + e
    log1p_e = jnp.where(u == 1.0, e, jnp.log(u) * (e / (u - 1.0)))
    sp = jnp.maximum(z, 0.0) + log1p_e
    return -LRU_C * sp, LRU_C * _sig(z)


def _gates(xc, wr_ref, br_ref, wi_ref, bi_ref, c8, j, gb):
    sl = slice(j * gb, (j + 1) * gb)
    x16 = xc.astype(BF16)
    r = _sig(jnp.dot(x16, wr_ref[j], preferred_element_type=F32) + br_ref[:, sl])
    ig = _sig(jnp.dot(x16, wi_ref[j], preferred_element_type=F32) + bi_ref[:, sl])
    la = c8[:, sl] * r
    a = jnp.exp(la)
    sq = jnp.sqrt(-jnp.tanh(la) * (a * a + 1.0))
    return r, ig, a, sq


def _mix_fwd(u, wa, ba, wr, br, wi, bi, lam, wb, name):
    t = u.shape[0]
    c = u.shape[1] // 6
    tc = MIX_CHUNK
    gb = wr.shape[1]
    nblk = c // gb
    ka, kb = wa.shape[0], wb.shape[0]

    def body(u_ref, wa_ref, ba_ref, wr_ref, br_ref, wi_ref, bi_ref, lam_ref, wb_ref,
             y_ref, hs_ref, xa_ext, v_ext, xc_s, a_s, b_s, carry_s):
        @pl.when(pl.program_id(0) == 0)
        def _():
            xa_ext[0:SUBLANES, :] = jnp.zeros((SUBLANES, c), F32)
            v_ext[0:SUBLANES, :] = jnp.zeros((SUBLANES, c), F32)
            carry_s[...] = jnp.zeros_like(carry_s)

        xa_ext[SUBLANES:SUBLANES + tc, :] = u_ref[:, 0:c]
        xc = ba_ref[...]
        for k in range(ka):
            xc = xc + wa_ref[pl.ds(k, 1), :] * xa_ext[pl.ds(SUBLANES - (ka - 1) + k, tc), :]
        xc_s[...] = xc
        c8, _ = _decay_consts(lam_ref[...])
        for j in range(nblk):
            sl = slice(j * gb, (j + 1) * gb)
            xcj = xc_s[:, sl]
            _, ig, a, sq = _gates(xcj, wr_ref, br_ref, wi_ref, bi_ref, c8, j, gb)
            a_s[:, sl] = a
            b_s[:, sl] = sq * (ig * xcj)

        row = lax.broadcasted_iota(jnp.int32, (SUBLANES, c), 0)

        def scan_step(j, _):
            off = pl.multiple_of(j * SUBLANES, SUBLANES)
            av = a_s[pl.ds(off, SUBLANES), :]
            bv = b_s[pl.ds(off, SUBLANES), :]
            for d in (1, 2, 4):
                keep = row >= d
                bsh = jnp.where(keep, pltpu.roll(bv, d, axis=0), 0.0)
                ash = jnp.where(keep, pltpu.roll(av, d, axis=0), 1.0)
                bv = av * bsh + bv
                av = av * ash
            hv = av * carry_s[...] + bv
            hs_ref[pl.ds(off, SUBLANES), :] = hv
            carry_s[...] = hs_ref[pl.ds(off + SUBLANES - 1, 1), :]
            return 0

        lax.fori_loop(0, tc // SUBLANES, scan_step, 0)

        ga = u_ref[:, c:2 * c]
        y_ref[:, 0:c] = (hs_ref[...] * (ga * _sig(ga))).astype(BF16)

        v_ext[SUBLANES:SUBLANES + tc, :] = u_ref[:, 3 * c:4 * c] * u_ref[:, 4 * c:5 * c]
        cv = wb_ref[pl.ds(0, 1), :] * v_ext[pl.ds(SUBLANES - (kb - 1), tc), :]
        for k in range(1, kb):
            cv = cv + wb_ref[pl.ds(k, 1), :] * v_ext[pl.ds(SUBLANES - (kb - 1) + k, tc), :]
        gbv = u_ref[:, 5 * c:6 * c]
        y_ref[:, c:2 * c] = (u_ref[:, 2 * c:3 * c] * cv * (gbv * _sig(gbv))).astype(BF16)

        xa_ext[0:SUBLANES, :] = xa_ext[tc:tc + SUBLANES, :]
        v_ext[0:SUBLANES, :] = v_ext[tc:tc + SUBLANES, :]

    full = lambda shape: pl.BlockSpec(shape, lambda i: (0,) * len(shape))
    return pl.pallas_call(
        body, name=name, grid=(t // tc,),
        in_specs=[pl.BlockSpec((tc, 6 * c), lambda i: (i, 0)),
                  full(wa.shape), full(ba.shape), full(wr.shape), full(br.shape),
                  full(wi.shape), full(bi.shape), full(lam.shape), full(wb.shape)],
        out_specs=[pl.BlockSpec((tc, 2 * c), lambda i: (i, 0)),
                   pl.BlockSpec((tc, c), lambda i: (i, 0))],
        out_shape=[jax.ShapeDtypeStruct((t, 2 * c), BF16), jax.ShapeDtypeStruct((t, c), F32)],
        scratch_shapes=[pltpu.VMEM((tc + SUBLANES, c), F32), pltpu.VMEM((tc + SUBLANES, c), F32),
                        pltpu.VMEM((tc, c), F32), pltpu.VMEM((tc, c), F32), pltpu.VMEM((tc, c), F32),
                        pltpu.VMEM((1, c), F32)],
        compiler_params=_params(("arbitrary",)),
    )(u, wa, ba, wr, br, wi, bi, lam, wb)


ROW_DWA = 0
ROW_DBA = 4
ROW_DBR = 5
ROW_DBI = 6
ROW_DLAM = 7
ROW_DWB = 8
SMALL_ROWS = 16


def _mix_bwd(u, hs, dy, wa, ba, wr, br, wi, bi, lam, wb, name, after=None):
    t = u.shape[0]
    c = u.shape[1] // 6
    tc = MIX_CHUNK
    nt = t // tc
    gb = wr.shape[1]
    nblk = c // gb
    ka, kb = wa.shape[0], wb.shape[0]
    assert ka <= ROW_DBA and kb <= SMALL_ROWS - ROW_DWB
    hb = tc // SUBLANES

    def body(u_ref, uh_ref, hs_ref, hsh_ref, dy_ref, wa_ref, ba_ref, wr_ref, br_ref, wi_ref, bi_ref, lam_ref, wb_ref,
             du_ref, dsm_ref, dwr_ref, dwi_ref,
             xa_ext, v_ext, hs_ext, a_ext, ds_ext, dxc_ext, dcv_ext, xc_s, r_s, i_s, sq_s, g_s, an_s):
        i = pl.program_id(0)
        chunk = nt - 1 - i
        tail = slice(tc, tc + SUBLANES)
        head = slice(0, SUBLANES)

        @pl.when(i == 0)
        def _():
            zero = jnp.zeros((SUBLANES, c), F32)
            a_ext[tail, :] = zero
            ds_ext[tail, :] = zero
            dxc_ext[tail, :] = zero
            dcv_ext[tail, :] = zero
            dsm_ref[...] = jnp.zeros_like(dsm_ref)
            dwr_ref[...] = jnp.zeros_like(dwr_ref)
            dwi_ref[...] = jnp.zeros_like(dwi_ref)

        prev = jnp.where(chunk > 0, 1.0, 0.0)
        xa_ext[head, :] = uh_ref[:, 0:c] * prev
        xa_ext[SUBLANES:SUBLANES + tc, :] = u_ref[:, 0:c]
        v_ext[head, :] = uh_ref[:, 3 * c:4 * c] * uh_ref[:, 4 * c:5 * c] * prev
        v_ext[SUBLANES:SUBLANES + tc, :] = u_ref[:, 3 * c:4 * c] * u_ref[:, 4 * c:5 * c]
        hs_ext[head, :] = hsh_ref[...] * prev
        hs_ext[SUBLANES:SUBLANES + tc, :] = hs_ref[...]

        xc = ba_ref[...]
        for k in range(ka):
            xc = xc + wa_ref[pl.ds(k, 1), :] * xa_ext[pl.ds(SUBLANES - (ka - 1) + k, tc), :]
        xc_s[...] = xc
        c8, dc8 = _decay_consts(lam_ref[...])
        for j in range(nblk):
            sl = slice(j * gb, (j + 1) * gb)
            r, ig, a, sq = _gates(xc_s[:, sl], wr_ref, br_ref, wi_ref, bi_ref, c8, j, gb)
            r_s[:, sl] = r
            i_s[:, sl] = ig
            sq_s[:, sl] = sq
            a_ext[0:tc, sl] = a

        ga = u_ref[:, c:2 * c]
        sga = _sig(ga)
        g_s[...] = dy_ref[:, 0:c] * (ga * sga)
        an_s[...] = a_ext[pl.ds(1, tc), :]

        row = lax.broadcasted_iota(jnp.int32, (SUBLANES, c), 0)

        def scan_step(j, _):
            off = pl.multiple_of(tc - SUBLANES - j * SUBLANES, SUBLANES)
            av = an_s[pl.ds(off, SUBLANES), :]
            bv = g_s[pl.ds(off, SUBLANES), :]
            for d in (1, 2, 4):
                keep = row < SUBLANES - d
                bsh = jnp.where(keep, pltpu.roll(bv, SUBLANES - d, axis=0), 0.0)
                ash = jnp.where(keep, pltpu.roll(av, SUBLANES - d, axis=0), 1.0)
                bv = av * bsh + bv
                av = av * ash
            ds_ext[pl.ds(off, SUBLANES), :] = av * ds_ext[pl.ds(off + SUBLANES, 1), :] + bv
            return 0

        lax.fori_loop(0, tc // SUBLANES, scan_step, 0)

        def acc(row_index, val):
            dsm_ref[pl.ds(row_index, 1), :] += jnp.sum(val, axis=0, keepdims=True)

        def acc_block(row_index, sl, val):
            dsm_ref[pl.ds(row_index, 1), sl] += jnp.sum(val, axis=0, keepdims=True)

        for j in range(nblk):
            sl = slice(j * gb, (j + 1) * gb)
            ds = ds_ext[0:tc, sl]
            hprev = hs_ext[pl.ds(SUBLANES - 1, tc), sl]
            a = a_ext[0:tc, sl]
            sq = sq_s[:, sl]
            ig = i_s[:, sl]
            r = r_s[:, sl]
            xcj = xc_s[:, sl]
            t1 = ds * xcj
            dla = (ds * hprev) * a - (t1 * ig) * ((a * a) / sq)
            acc_block(ROW_DLAM, sl, dla * r)
            dpr = (dla * c8[:, sl]) * (r * (1.0 - r))
            dpi = (t1 * sq) * (ig * (1.0 - ig))
            acc_block(ROW_DBR, sl, dpr)
            acc_block(ROW_DBI, sl, dpi)
            p16 = dpr.astype(BF16)
            q16 = dpi.astype(BF16)
            x16 = xcj.astype(BF16)
            dwr_ref[j] += lax.dot_general(x16, p16, TN_DIMS, preferred_element_type=F32)
            dwi_ref[j] += lax.dot_general(x16, q16, TN_DIMS, preferred_element_type=F32)
            dxc = (ds * (sq * ig)
                   + lax.dot_general(p16, wr_ref[j], NT_DIMS, preferred_element_type=F32)
                   + lax.dot_general(q16, wi_ref[j], NT_DIMS, preferred_element_type=F32))
            dxc_ext[0:tc, sl] = dxc
            acc_block(ROW_DBA, sl, dxc)

        dsilu_a = sga * (1.0 + ga * (1.0 - sga))
        du_ref[:, c:2 * c] = (dy_ref[:, 0:c] * hs_ref[...] * dsilu_a).astype(BF16)

        dxc = dxc_ext[0:tc, :]
        dxa = wa_ref[pl.ds(ka - 1, 1), :] * dxc
        acc(ROW_DWA + ka - 1, dxc * xa_ext[SUBLANES:SUBLANES + tc, :])
        for k in range(ka - 1):
            acc(ROW_DWA + k, dxc * xa_ext[pl.ds(SUBLANES - (ka - 1) + k, tc), :])
            dxa = dxa + wa_ref[pl.ds(k, 1), :] * dxc_ext[pl.ds(ka - 1 - k, tc), :]
        du_ref[:, 0:c] = dxa.astype(BF16)

        cv = wb_ref[pl.ds(0, 1), :] * v_ext[pl.ds(SUBLANES - (kb - 1), tc), :]
        for k in range(1, kb):
            cv = cv + wb_ref[pl.ds(k, 1), :] * v_ext[pl.ds(SUBLANES - (kb - 1) + k, tc), :]
        gbv = u_ref[:, 5 * c:6 * c]
        sgb = _sig(gbv)
        silu_b = gbv * sgb
        dyb = dy_ref[:, c:2 * c]
        gB = u_ref[:, 2 * c:3 * c]
        du_ref[:, 2 * c:3 * c] = (dyb * cv * silu_b).astype(BF16)
        du_ref[:, 5 * c:6 * c] = (dyb * gB * cv * (sgb * (1.0 + gbv * (1.0 - sgb)))).astype(BF16)
        dcv = dyb * gB * silu_b
        dcv_ext[0:tc, :] = dcv
        dv = wb_ref[pl.ds(kb - 1, 1), :] * dcv
        acc(ROW_DWB + kb - 1, dcv * v_ext[SUBLANES:SUBLANES + tc, :])
        for k in range(kb - 1):
            acc(ROW_DWB + k, dcv * v_ext[pl.ds(SUBLANES - (kb - 1) + k, tc), :])
            dv = dv + wb_ref[pl.ds(k, 1), :] * dcv_ext[pl.ds(kb - 1 - k, tc), :]
        du_ref[:, 3 * c:4 * c] = (dv * u_ref[:, 4 * c:5 * c]).astype(BF16)
        du_ref[:, 4 * c:5 * c] = (dv * u_ref[:, 3 * c:4 * c]).astype(BF16)

        a_ext[tail, :] = a_ext[head, :]
        ds_ext[tail, :] = ds_ext[head, :]
        dxc_ext[tail, :] = dxc_ext[head, :]
        dcv_ext[tail, :] = dcv_ext[head, :]

        @pl.when(i == nt - 1)
        def _():
            dsm_ref[pl.ds(ROW_DLAM, 1), :] = dsm_ref[pl.ds(ROW_DLAM, 1), :] * dc8

    full = lambda shape: pl.BlockSpec(shape, lambda i: (0,) * len(shape))
    rev = lambda i: (nt - 1 - i, 0)
    halo = lambda i: (jnp.maximum((nt - 1 - i) * hb - 1, 0), 0)
    ext = pltpu.VMEM((tc + SUBLANES, c), F32)
    blk = pltpu.VMEM((tc, c), F32)
    body, more_specs, more = _behind(body, 13, after)
    return pl.pallas_call(
        body, name=name, grid=(nt,),
        in_specs=[pl.BlockSpec((tc, 6 * c), rev), pl.BlockSpec((SUBLANES, 6 * c), halo),
                  pl.BlockSpec((tc, c), rev), pl.BlockSpec((SUBLANES, c), halo),
                  pl.BlockSpec((tc, 2 * c), rev),
                  full(wa.shape), full(ba.shape), full(wr.shape), full(br.shape),
                  full(wi.shape), full(bi.shape), full(lam.shape), full(wb.shape)] + more_specs,
        out_specs=[pl.BlockSpec((tc, 6 * c), rev), full((SMALL_ROWS, c)), full(wr.shape), full(wi.shape)],
        out_shape=[jax.ShapeDtypeStruct((t, 6 * c), BF16), jax.ShapeDtypeStruct((SMALL_ROWS, c), F32),
                   jax.ShapeDtypeStruct(wr.shape, F32), jax.ShapeDtypeStruct(wi.shape, F32)],
        scratch_shapes=[ext] * 7 + [blk] * 6,
        compiler_params=_params(("arbitrary",)),
    )(u, u, hs, hs, dy, wa, ba, wr, br, wi, bi, lam, wb, *more)


def _behind(body, n_in, after):
    if after is None:
        return body, [], []
    return (lambda *refs: body(*refs[:n_in], *refs[n_in + 1:])), [ANY], [after]


def _out_proj(h, y, w, name, after=None):
    t, d = h.shape
    dm = y.shape[1]
    tm = _row_tile(t)
    tn = _col_tile(d, (1024, 512, 256))

    def body(h_ref, y_ref, w_ref, o_ref):
        o_ref[...] = h_ref[...] + jnp.dot(y_ref[...], w_ref[...], preferred_element_type=F32)

    body, more_specs, more = _behind(body, 3, after)
    return pl.pallas_call(
        body, name=name, grid=(d // tn, t // tm),
        in_specs=[pl.BlockSpec((tm, tn), lambda n, i: (i, n)),
                  pl.BlockSpec((tm, dm), lambda n, i: (i, 0)),
                  pl.BlockSpec((dm, tn), lambda n, i: (0, n))] + more_specs,
        out_specs=pl.BlockSpec((tm, tn), lambda n, i: (i, n)),
        out_shape=jax.ShapeDtypeStruct((t, d), F32),
        compiler_params=_params(("arbitrary", "arbitrary")),
    )(h, y, w, *more)


def _out_proj_dy(dout, w, name, after=None):
    t, d = dout.shape
    dm = w.shape[0]
    tm = _row_tile(t)
    tn = _col_tile(dm, (1024, 512, 256))

    def body(g_ref, w_ref, o_ref):
        o_ref[...] = lax.dot_general(g_ref[...].astype(BF16), w_ref[...], NT_DIMS, preferred_element_type=F32)

    body, more_specs, more = _behind(body, 2, after)
    return pl.pallas_call(
        body, name=name, grid=(dm // tn, t // tm),
        in_specs=[pl.BlockSpec((tm, d), lambda n, i: (i, 0)),
                  pl.BlockSpec((tn, d), lambda n, i: (n, 0))] + more_specs,
        out_specs=pl.BlockSpec((tm, tn), lambda n, i: (i, n)),
        out_shape=jax.ShapeDtypeStruct((t, dm), F32),
        compiler_params=_params(("arbitrary", "arbitrary")),
    )(dout, w, *more)


def _out_proj_dw(y, dout, name):
    t, dm = y.shape
    d = dout.shape[1]
    tmm = _col_tile(dm, (512, 256))
    tn = _col_tile(d, (512, 256))

    def body(y_ref, g_ref, o_ref):
        o_ref[...] = lax.dot_general(y_ref[...], g_ref[...].astype(BF16), TN_DIMS, preferred_element_type=F32)

    return pl.pallas_call(
        body, name=name, grid=(d // tn, dm // tmm),
        in_specs=[pl.BlockSpec((t, tmm), lambda n, m: (0, m)),
                  pl.BlockSpec((t, tn), lambda n, m: (0, n))],
        out_specs=pl.BlockSpec((tmm, tn), lambda n, m: (m, n)),
        out_shape=jax.ShapeDtypeStruct((dm, d), F32),
        compiler_params=_params(("arbitrary", "arbitrary")),
    )(y, dout)


def _in_proj_bwd(du, wg, h, g, dout, name, after=None):
    t, d = h.shape
    s, _, ns = wg.shape
    tm = _row_tile(t)
    tk = _col_tile(ns, (512, 384, 128))
    nb = ns // tk
    nk = s * nb

    def body(du_ref, w_ref, h_ref, g_ref, dout_ref, dh_ref, dg_ref, acc_ref):
        i, k = pl.program_id(0), pl.program_id(1)

        @pl.when(k == 0)
        def _():
            acc_ref[...] = jnp.zeros_like(acc_ref)

        @pl.when((k == 0) & (i == 0))
        def _():
            dg_ref[...] = jnp.zeros_like(dg_ref)

        acc_ref[...] += lax.dot_general(du_ref[...], w_ref[...], NT_DIMS, preferred_element_type=F32)

        @pl.when(k == nk - 1)
        def _():
            x = h_ref[...]
            dhn = acc_ref[...]
            r = lax.rsqrt(jnp.mean(x * x, axis=-1, keepdims=True) + RMS_EPS)
            gd = dhn * g_ref[...]
            dot = jnp.mean(gd * x, axis=-1, keepdims=True)
            dh_ref[...] = dout_ref[...] + (r * gd - x * ((r * r * r) * dot))
            dg_ref[...] += jnp.sum(dhn * (x * r), axis=0, keepdims=True)

    body, more_specs, more = _behind(body, 5, after)
    return pl.pallas_call(
        body, name=name, grid=(t // tm, nk),
        in_specs=[pl.BlockSpec((tm, tk), lambda i, k: (i, k)),
                  pl.BlockSpec((None, d, tk), lambda i, k: (k // nb, 0, k % nb)),
                  pl.BlockSpec((tm, d), lambda i, k: (i, 0)),
                  pl.BlockSpec((1, d), lambda i, k: (0, 0)),
                  pl.BlockSpec((tm, d), lambda i, k: (i, 0))] + more_specs,
        out_specs=[pl.BlockSpec((tm, d), lambda i, k: (i, 0)),
                   pl.BlockSpec((1, d), lambda i, k: (0, 0))],
        out_shape=[jax.ShapeDtypeStruct((t, d), F32), jax.ShapeDtypeStruct((1, d), F32)],
        scratch_shapes=[pltpu.VMEM((tm, d), F32)],
        compiler_params=_params(("arbitrary", "arbitrary")),
    )(du, wg, h, g, dout, *more)


def _in_proj_dw(hn, du, s, name, after=None):
    t, d = hn.shape
    ns = du.shape[1] // s
    tmm = _col_tile(d, (512, 256))
    tn = _col_tile(ns, (768, 384, 128))
    nb = ns // tn

    def body(hn_ref, du_ref, o_ref):
        o_ref[...] = lax.dot_general(hn_ref[...], du_ref[...], TN_DIMS, preferred_element_type=F32)

    body, more_specs, more = _behind(body, 2, after)
    return pl.pallas_call(
        body, name=name, grid=(s * nb, d // tmm),
        in_specs=[pl.BlockSpec((t, tmm), lambda n, m: (0, m)),
                  pl.BlockSpec((t, tn), lambda n, m: (0, n))] + more_specs,
        out_specs=pl.BlockSpec((None, tmm, tn), lambda n, m: (n // nb, m, n % nb)),
        out_shape=jax.ShapeDtypeStruct((s, d, ns), F32),
        compiler_params=_params(("arbitrary", "arbitrary")),
    )(hn, du, *more)


def _loss_head(h, tgt, g, n_meta, t_real, name):
    t, d = h.shape
    tm = _row_tile(t)

    def body(h_ref, t_ref, g_ref, dh_ref, loss_ref, dg_ref):
        i = pl.program_id(0)

        @pl.when(i == 0)
        def _():
            loss_ref[...] = jnp.zeros_like(loss_ref)
            dg_ref[...] = jnp.zeros_like(dg_ref)

        x = h_ref[...]
        gv = g_ref[...]
        r = lax.rsqrt(jnp.mean(x * x, axis=-1, keepdims=True) + RMS_EPS)
        xr = x * r
        rows = i * tm + lax.broadcasted_iota(jnp.int32, (tm, 1), 0)
        valid = (rows >= n_meta) & (rows < t_real)
        err = jnp.where(valid, xr * gv - t_ref[...], 0.0)
        loss_ref[...] += 0.5 * jnp.sum(jnp.mean(err * err, axis=-1, keepdims=True))
        dy = err * (1.0 / d)
        gd = dy * gv
        dot = jnp.mean(gd * x, axis=-1, keepdims=True)
        dh_ref[...] = r * gd - x * ((r * r * r) * dot)
        dg_ref[...] += jnp.sum(dy * xr, axis=0, keepdims=True)

    return pl.pallas_call(
        body, name=name, grid=(t // tm,),
        in_specs=[pl.BlockSpec((tm, d), lambda i: (i, 0)),
                  pl.BlockSpec((tm, d), lambda i: (i, 0)),
                  pl.BlockSpec((1, d), lambda i: (0, 0))],
        out_specs=[pl.BlockSpec((tm, d), lambda i: (i, 0)),
                   pl.BlockSpec((1, LANES), lambda i: (0, 0)),
                   pl.BlockSpec((1, d), lambda i: (0, 0))],
        out_shape=[jax.ShapeDtypeStruct((t, d), F32), jax.ShapeDtypeStruct((1, LANES), F32),
                   jax.ShapeDtypeStruct((1, d), F32)],
        compiler_params=_params(("arbitrary",)),
    )(h, tgt, g)


def _adamw(w, g, m, v, name):
    rows, cols = w.shape
    tr = rows
    for cand in (512, 256, 128, 64, 32, 16, 8):
        if rows % cand == 0 and cand * cols * 4 <= 2 * 1024 * 1024:
            tr = cand
            break

    def body(w_ref, g_ref, m_ref, v_ref, d_ref, nm_ref, nv_ref):
        gv = g_ref[...]
        m2 = ADAM_B1 * m_ref[...] + (1.0 - ADAM_B1) * gv
        v2 = ADAM_B2 * v_ref[...] + (1.0 - ADAM_B2) * (gv * gv)
        m_hat = m2 / (1.0 - ADAM_B1 ** ADAM_STEP)
        v_hat = v2 / (1.0 - ADAM_B2 ** ADAM_STEP)
        d_ref[...] = -ADAM_LR * (m_hat / (jnp.sqrt(v_hat) + ADAM_EPS) + ADAM_WD * w_ref[...])
        nm_ref[...] = m2
        nv_ref[...] = v2

    spec = pl.BlockSpec((tr, cols), lambda i: (i, 0))
    return pl.pallas_call(
        body, name=name, grid=(rows // tr,),
        in_specs=[spec] * 4, out_specs=[spec] * 3,
        out_shape=[jax.ShapeDtypeStruct((rows, cols), F32)] * 3,
        compiler_params=_params(("arbitrary",)),
    )(w, g, m, v)


def _pair_add(x, ra, c_idx, name):
    s, _, rows, cols = x.shape
    tr = _slab_rows(rows, cols)

    def body(c_ref, x_ref, r_ref, o_ref):
        o_ref[...] = (x_ref[...] + r_ref[...]).astype(BF16)

    return pl.pallas_call(
        body, name=name,
        grid_spec=pltpu.PrefetchScalarGridSpec(
            num_scalar_prefetch=1, grid=(s, rows // tr),
            in_specs=[pl.BlockSpec((None, None, tr, cols), lambda a, i, c_ref: (a, c_ref[0], i, 0)),
                      pl.BlockSpec((None, tr, cols), lambda a, i, c_ref: (a, i, 0))],
            out_specs=pl.BlockSpec((None, tr, cols), lambda a, i, c_ref: (a, i, 0))),
        out_shape=jax.ShapeDtypeStruct((s, rows, cols), BF16),
        compiler_params=_params(("arbitrary", "arbitrary")),
    )(c_idx, x, ra)


def _chip_sum(rc, p, where, n_slots, name):
    s, rows, cols = rc.shape
    tr = _slab_rows(rows, cols)

    def body(w_ref, x_ref, p_ref, o_ref):
        me = w_ref[0]
        total = jnp.where(me == 0, p_ref[...], x_ref[0]).astype(F32)
        for a in range(1, s):
            total = total + jnp.where(me == a, p_ref[...], x_ref[a]).astype(F32)
        o_ref[...] = total

    return pl.pallas_call(
        body, name=name,
        grid_spec=pltpu.PrefetchScalarGridSpec(
            num_scalar_prefetch=1, grid=(rows // tr,),
            in_specs=[pl.BlockSpec((s, tr, cols), lambda i, w_ref: (0, i, 0)),
                      pl.BlockSpec((None, tr, cols), lambda i, w_ref: (w_ref[0], i, 0))],
            out_specs=pl.BlockSpec((None, tr, cols), lambda i, w_ref: (w_ref[1], i, 0))),
        out_shape=jax.ShapeDtypeStruct((n_slots, rows, cols), F32),
        compiler_params=_params(("arbitrary",)),
    )(where, rc, p)


def _cast_place(w, layer, me_idx, name):
    _, rows, cols = w.shape
    tr = _slab_rows(rows, cols)

    def body(m_ref, w_ref, o_ref):
        o_ref[...] = w_ref[...].astype(BF16)

    return pl.pallas_call(
        body, name=name,
        grid_spec=pltpu.PrefetchScalarGridSpec(
            num_scalar_prefetch=1, grid=(rows // tr,),
            in_specs=[pl.BlockSpec((None, tr, cols), lambda i, m_ref: (layer, i, 0))],
            out_specs=pl.BlockSpec((None, tr, cols), lambda i, m_ref: (m_ref[0], i, 0))),
        out_shape=jax.ShapeDtypeStruct((N_CHIPS, rows, cols), BF16),
        compiler_params=_params(("arbitrary",)),
    )(me_idx, w)


def _place():
    x, y, c = lax.axis_index("x"), lax.axis_index("y"), lax.axis_index("c")
    chips = [(1 - x, y), (x, 1 - y), (1 - x, 1 - y)]
    return x, y, c, chips


def _chip_index(cx, cy):
    return 2 * cx + cy


def _gather_copies(bufs, stage):
    x, y, c, chips = _place()
    me = _chip_index(x, y)
    copies = []
    for b in bufs:
        for chip in chips:
            src = _chip_index(*chip)
            if stage == 0:
                copies.append((b.at[me, c], (*chip, c), b.at[src, c]))
            else:
                copies.append((b.at[src, c], (x, y, 1 - c), b.at[src, 1 - c]))
    return copies


def _remote(ref, peer, ssem, rsem, k):
    return pltpu.make_async_remote_copy(src_ref=ref, dst_ref=ref, send_sem=ssem.at[k], recv_sem=rsem.at[k],
                                        device_id=peer, device_id_type=MESH)


def _gather_first(bufs, small):
    n = len(bufs)
    k = 3 * n

    def body(*refs):
        sm_ref = refs[n]
        b_refs, smg_ref = refs[n + 1:2 * n + 1], refs[2 * n + 1]
        lsem, ssem, rsem = refs[2 * n + 2:]
        x, y, c, chips = _place()
        me = _chip_index(x, y)
        local = pltpu.make_async_copy(sm_ref, smg_ref.at[me], lsem)
        local.start()
        first = _gather_copies(b_refs, 0)
        second = _gather_copies(b_refs, 1)
        started = []
        for i, (ref, peer, _) in enumerate(first):
            started.append(_remote(ref, peer, ssem, rsem, i))
        for j, chip in enumerate(chips):
            started.append(pltpu.make_async_remote_copy(
                src_ref=sm_ref, dst_ref=smg_ref.at[me], send_sem=ssem.at[2 * k + j], recv_sem=rsem.at[2 * k + j],
                device_id=(*chip, c), device_id_type=MESH))
        for cp in started:
            cp.start()
        for i, (_, peer, lands) in enumerate(first):
            _remote(lands, peer, ssem, rsem, i).wait_recv()
            ref, sib, _ = second[i]
            fwd = _remote(ref, sib, ssem, rsem, k + i)
            fwd.start()
            started.append(fwd)
        for i, (_, sib, lands) in enumerate(second):
            _remote(lands, sib, ssem, rsem, k + i).wait_recv()
        for j, chip in enumerate(chips):
            theirs = smg_ref.at[_chip_index(*chip)]
            pltpu.make_async_remote_copy(src_ref=theirs, dst_ref=theirs, send_sem=ssem.at[2 * k + j],
                                         recv_sem=rsem.at[2 * k + j], device_id=(*chip, c),
                                         device_id_type=MESH).wait_recv()
        for cp in started:
            cp.wait_send()
        local.wait()

    return pl.pallas_call(
        body, name="gather_first",
        in_specs=[ANY] * (n + 1), out_specs=[ANY] * (n + 1),
        out_shape=[jax.ShapeDtypeStruct(b.shape, b.dtype) for b in bufs]
        + [jax.ShapeDtypeStruct((N_CHIPS,) + small.shape, small.dtype)],
        input_output_aliases={i: i for i in range(n)},
        scratch_shapes=[pltpu.SemaphoreType.DMA, pltpu.SemaphoreType.DMA((2 * k + 3,)),
                        pltpu.SemaphoreType.DMA((2 * k + 3,))],
    )(*bufs, small)


HBM = pl.BlockSpec(memory_space=pltpu.HBM)
SEM = pl.BlockSpec(memory_space=pltpu.SEMAPHORE)
DATAFLOW = pltpu.SideEffectType.DATAFLOW_SIDE_EFFECTING


def _copies_start(bufs, plan, n_copies, name, after=None):
    n = len(bufs)
    extra = [] if after is None else [after]

    def body(*refs):
        refs = refs[:n] + refs[n + len(extra):]
        ssem, rsem = refs[n], refs[n + 1]
        b_refs, token = refs[n + 2:2 * n + 2], refs[2 * n + 2]
        copies = plan(b_refs)
        assert len(copies) == n_copies
        for i, (src, dst, peer, _) in enumerate(copies):
            pltpu.make_async_remote_copy(src_ref=src, dst_ref=dst, send_sem=ssem.at[i], recv_sem=rsem.at[i],
                                         device_id=peer, device_id_type=MESH).start()
        token[...] = jnp.zeros_like(token)

    return pl.pallas_call(
        body, name=name,
        out_shape=(pltpu.SemaphoreType.DMA((n_copies,)), pltpu.SemaphoreType.DMA((n_copies,)),
                   *[pltpu.HBM(b.shape, b.dtype) for b in bufs], jax.ShapeDtypeStruct((SUBLANES, LANES), F32)),
        in_specs=[HBM] * n + [ANY] * len(extra),
        out_specs=(SEM, SEM, *[HBM] * n, pl.BlockSpec(memory_space=pltpu.VMEM)),
        input_output_aliases={i: 2 + i for i in range(n)},
        compiler_params=pltpu.CompilerParams(has_side_effects=DATAFLOW),
    )(*[pltpu.with_memory_space_constraint(b, pltpu.HBM) for b in bufs], *extra)


def _copies_wait(bufs, ssem, rsem, after, plan, name):
    n = len(bufs)

    def body(*refs):
        b_refs, ssem_ref, rsem_ref = refs[:n], refs[n], refs[n + 1]
        for i, (src, dst, peer, lands) in enumerate(plan(b_refs)):
            pltpu.make_async_remote_copy(src_ref=src, dst_ref=dst, send_sem=ssem_ref.at[i], recv_sem=rsem_ref.at[i],
                                         device_id=peer, device_id_type=MESH).wait_send()
            pltpu.make_async_remote_copy(src_ref=lands, dst_ref=lands, send_sem=ssem_ref.at[i],
                                         recv_sem=rsem_ref.at[i], device_id=peer, device_id_type=MESH).wait_recv()

    return pl.pallas_call(
        body, name=name,
        out_shape=tuple(pltpu.HBM(b.shape, b.dtype) for b in bufs),
        in_specs=[HBM] * n + [SEM, SEM, ANY], out_specs=tuple([HBM] * n),
        input_output_aliases={i: i for i in range(n)},
        compiler_params=pltpu.CompilerParams(has_side_effects=DATAFLOW),
    )(*bufs, ssem, rsem, after)


def _gather_plan(stage):
    return lambda refs: [(ref, ref, peer, lands) for ref, peer, lands in _gather_copies(refs, stage)]


def _swap_plan(refs):
    n = len(refs) // 2
    x, y, c, _ = _place()
    return [(refs[a].at[:, 1 - c], refs[n + a], (x, y, 1 - c), refs[n + a]) for a in range(n)]


def _scatter_plan(refs):
    n = len(refs) // 2
    x, y, c, chips = _place()
    me = _chip_index(x, y)
    return [(refs[a].at[_chip_index(*chip)], refs[n + a].at[me], (*chip, c), refs[n + a].at[_chip_index(*chip)])
            for a in range(n) for chip in chips]


def _pair_gather_plan(refs):
    x, y, c, _ = _place()
    return [(r.at[c], r.at[c], (x, y, 1 - c), r.at[1 - c]) for r in refs]


def _pair_swap(xs, name):
    n = len(xs)

    def body(*refs):
        x_refs, o_refs, ssem, rsem = refs[:n], refs[n:2 * n], refs[2 * n], refs[2 * n + 1]
        x, y, c, _ = _place()
        copies = [pltpu.make_async_remote_copy(src_ref=x_refs[a].at[:, 1 - c], dst_ref=o_refs[a],
                                               send_sem=ssem.at[a], recv_sem=rsem.at[a],
                                               device_id=(x, y, 1 - c), device_id_type=MESH) for a in range(n)]
        for cp in copies:
            cp.start()
        for cp in copies:
            cp.wait()

    return pl.pallas_call(
        body, name=name, in_specs=[ANY] * n, out_specs=[ANY] * n,
        out_shape=[jax.ShapeDtypeStruct((a.shape[0],) + a.shape[2:], a.dtype) for a in xs],
        scratch_shapes=[pltpu.SemaphoreType.DMA((n,)), pltpu.SemaphoreType.DMA((n,))],
    )(*xs)


def _chip_scatter(ps):
    n = len(ps)

    def body(*refs):
        p_refs, o_refs, ssem, rsem = refs[:n], refs[n:2 * n], refs[2 * n], refs[2 * n + 1]
        x, y, c, chips = _place()
        me = _chip_index(x, y)
        sends = []
        for a in range(n):
            for j, chip in enumerate(chips):
                sends.append(pltpu.make_async_remote_copy(
                    src_ref=p_refs[a].at[_chip_index(*chip)], dst_ref=o_refs[a].at[me],
                    send_sem=ssem.at[3 * a + j], recv_sem=rsem.at[3 * a + j],
                    device_id=(*chip, c), device_id_type=MESH))
        for cp in sends:
            cp.start()
        for a in range(n):
            for j, chip in enumerate(chips):
                src = _chip_index(*chip)
                pltpu.make_async_remote_copy(
                    src_ref=p_refs[a].at[src], dst_ref=o_refs[a].at[src],
                    send_sem=ssem.at[3 * a + j], recv_sem=rsem.at[3 * a + j],
                    device_id=(*chip, c), device_id_type=MESH).wait_recv()
        for cp in sends:
            cp.wait_send()

    return pl.pallas_call(
        body, name="chip_scatter", in_specs=[ANY] * n, out_specs=[ANY] * n,
        out_shape=[jax.ShapeDtypeStruct(a.shape, a.dtype) for a in ps],
        scratch_shapes=[pltpu.SemaphoreType.DMA((3 * n,)), pltpu.SemaphoreType.DMA((3 * n,))],
    )(*ps)


def _final_gather(fs, rep):
    n = len(fs)

    def body(*refs):
        o_refs, repo_ref = refs[n + 1:2 * n + 1], refs[2 * n + 1]
        ssem, rsem = refs[2 * n + 2:]
        x, y, c, chips = _place()
        slot = 4 * x + 2 * y + c
        copies = [pltpu.make_async_remote_copy(src_ref=o_refs[a].at[c], dst_ref=o_refs[a].at[c],
                                               send_sem=ssem.at[a], recv_sem=rsem.at[a],
                                               device_id=(x, y, 1 - c), device_id_type=MESH) for a in range(n)]
        peers = [(x, y, 1 - c)] + [(*chip, c) for chip in chips] + [(*chip, 1 - c) for chip in chips]
        for k, peer in enumerate(peers):
            copies.append(pltpu.make_async_remote_copy(src_ref=repo_ref.at[slot], dst_ref=repo_ref.at[slot],
                                                       send_sem=ssem.at[n + k], recv_sem=rsem.at[n + k],
                                                       device_id=peer, device_id_type=MESH))
        for cp in copies:
            cp.start()
        for a in range(n):
            pltpu.make_async_remote_copy(src_ref=o_refs[a].at[1 - c], dst_ref=o_refs[a].at[1 - c],
                                         send_sem=ssem.at[a], recv_sem=rsem.at[a],
                                         device_id=(x, y, 1 - c), device_id_type=MESH).wait_recv()
        for k, peer in enumerate(peers):
            px, py, pc = peer
            theirs = repo_ref.at[4 * px + 2 * py + pc]
            pltpu.make_async_remote_copy(src_ref=theirs, dst_ref=theirs, send_sem=ssem.at[n + k], recv_sem=rsem.at[n + k],
                                         device_id=peer, device_id_type=MESH).wait_recv()
        for cp in copies:
            cp.wait_send()

    return pl.pallas_call(
        body, name="final_gather", in_specs=[ANY] * (n + 1), out_specs=[ANY] * (n + 1),
        out_shape=[jax.ShapeDtypeStruct(a.shape, a.dtype) for a in fs] + [jax.ShapeDtypeStruct(rep.shape, rep.dtype)],
        input_output_aliases={k: k for k in range(n + 1)},
        scratch_shapes=[pltpu.SemaphoreType.DMA((n + 7,)), pltpu.SemaphoreType.DMA((n + 7,))],
    )(*fs, rep)


def _block_diag(w, gb):
    nh, hd, _ = w.shape
    per = gb // hd
    w4 = w.reshape(nh // per, per, hd, hd)
    eye = jnp.eye(per, dtype=w.dtype)
    return jnp.einsum("jaik,ab->jaibk", w4, eye).reshape(nh // per, gb, gb)


def _diag_blocks(dense, hd):
    nj, gb, _ = dense.shape
    per = gb // hd
    d5 = dense.reshape(nj, per, hd, per, hd)
    return jnp.stack([d5[:, a, :, a, :] for a in range(per)], axis=1).reshape(nj * per, hd, hd)


def _round_up(n, q):
    return (n + q - 1) // q * q


def kernel(x, meta, norm_g, w_in, conv_a_w, conv_a_b, lru_wr, lru_br, lru_wi, lru_bi, lru_lambda, conv_b_w, w_out, final_g, loss_target, m_meta, m_norm_g, m_w_in, m_conv_a_w, m_conv_a_b, m_lru_wr, m_lru_br, m_lru_wi, m_lru_bi, m_lru_lambda, m_conv_b_w, m_w_out, m_final_g, v_meta, v_norm_g, v_w_in, v_conv_a_w, v_conv_a_b, v_lru_wr, v_lru_br, v_lru_wi, v_lru_bi, v_lru_lambda, v_conv_b_w, v_w_out, v_final_g):
    weights = dict(meta=meta, norm_g=norm_g, w_in=w_in, conv_a_w=conv_a_w, conv_a_b=conv_a_b, lru_wr=lru_wr,
                   lru_br=lru_br, lru_wi=lru_wi, lru_bi=lru_bi, lru_lambda=lru_lambda, conv_b_w=conv_b_w,
                   w_out=w_out, final_g=final_g)
    mom1 = dict(meta=m_meta, norm_g=m_norm_g, w_in=m_w_in, conv_a_w=m_conv_a_w, conv_a_b=m_conv_a_b,
                lru_wr=m_lru_wr, lru_br=m_lru_br, lru_wi=m_lru_wi, lru_bi=m_lru_bi, lru_lambda=m_lru_lambda,
                conv_b_w=m_conv_b_w, w_out=m_w_out, final_g=m_final_g)
    mom2 = dict(meta=v_meta, norm_g=v_norm_g, w_in=v_w_in, conv_a_w=v_conv_a_w, conv_a_b=v_conv_a_b,
                lru_wr=v_lru_wr, lru_br=v_lru_br, lru_wi=v_lru_wi, lru_bi=v_lru_bi, lru_lambda=v_lru_lambda,
                conv_b_w=v_conv_b_w, w_out=v_w_out, final_g=v_final_g)
    names = list(weights)

    assert x.shape[0] == 1
    seq, d = x.shape[1], x.shape[2]
    n_meta, ds = meta.shape
    depth = norm_g.shape[0]
    c = lru_lambda.shape[1]
    nh, hd = lru_wr.shape[1], lru_wr.shape[2]
    ns = w_in.shape[2]
    dms = w_out.shape[1]
    cs = conv_a_w.shape[2]
    ka, kb = conv_a_w.shape[1], conv_b_w.shape[1]
    s = N_CHIPS
    assert depth == N_CORES and d == s * ds and c == s * cs and s * ns == 6 * c and s * dms == 2 * c
    gb = min(GATE_BLOCK, c)
    t_real = n_meta + seq
    t = _round_up(t_real, ROW_QUANTUM)
    my_c = lax.axis_index("c").astype(jnp.int32)
    my_chip = (2 * lax.axis_index("x") + lax.axis_index("y")).astype(jnp.int32)
    c_idx = my_c.reshape(1)
    chip_idx = my_chip.reshape(1)

    sm_rows = _round_up(n_meta + depth * SUBLANES, 2 * SUBLANES)
    small = jnp.zeros((sm_rows, ds), F32)
    small = small.at[0:n_meta, :].set(meta)
    for l in range(depth):
        base = n_meta + l * SUBLANES
        small = small.at[base:base + ka, 0:cs].set(conv_a_w[l])
        small = small.at[base + ka:base + ka + kb, 0:cs].set(conv_b_w[l])
    win_b = [_cast_place(w_in, l, chip_idx, f"cast_w_in_{l}").reshape(s, 2, d // 2, ns) for l in range(depth)]
    wout_b = [_cast_place(w_out, l, chip_idx, f"cast_w_out_{l}").reshape(s, 2, dms // 2, d) for l in range(depth)]
    win_b[0], small_g = _gather_first([win_b[0]], small)
    ssem_o, rsem_o, wout0, token_o = _copies_start([wout_b[0]], _gather_plan(0), 3, "gather_wout0_ici_start")
    later = [win_b[1], wout_b[1]]
    ssem, rsem, *later, token = _copies_start(later, _gather_plan(0), 3 * len(later), "gather_next_ici_start",
                                              after=token_o)
    meta_full = jnp.transpose(small_g[:, 0:n_meta, :], (1, 0, 2)).reshape(n_meta, d)
    wa_full, wb_full = [], []
    for l in range(depth):
        base = n_meta + l * SUBLANES
        wa_full.append(jnp.transpose(small_g[:, base:base + ka, 0:cs], (1, 0, 2)).reshape(ka, c))
        wb_full.append(jnp.transpose(small_g[:, base + ka:base + ka + kb, 0:cs], (1, 0, 2)).reshape(kb, c))

    h = jnp.concatenate([meta_full, x[0], jnp.zeros((t - t_real, d), F32)], axis=0)
    tgt = jnp.concatenate([jnp.zeros((n_meta, d), F32), loss_target[0], jnp.zeros((t - t_real, d), F32)], axis=0)
    layer_w = []
    for l in range(depth):
        layer_w.append(dict(
            g=norm_g[l].reshape(1, d), wa=wa_full[l], ba=conv_a_b[l].reshape(1, c),
            wr=_block_diag(lru_wr[l], gb).astype(BF16), br=lru_br[l].reshape(1, c),
            wi=_block_diag(lru_wi[l], gb).astype(BF16), bi=lru_bi[l].reshape(1, c),
            lam=lru_lambda[l].reshape(1, c), wb=wb_full[l]))
    saved = []
    for l, lw in enumerate(layer_w):
        first = l == 0
        lw["win"] = win_b[l].reshape(s, d, ns)
        u, hn = _norm_in(h, lw["g"] + token[0, 0] if first else lw["g"], lw["win"], f"norm_in_{l}")
        if first:
            (wout0,) = _copies_wait([wout0], ssem_o, rsem_o, u, _gather_plan(0), "gather_wout0_ici_wait")
            ssem_o, rsem_o, wout0, token_o = _copies_start([wout0], _gather_plan(1), 3, "gather_wout0_d2d_start")
        y, hs = _mix_fwd(u, lw["wa"], lw["ba"] + token_o[0, 0] if first else lw["ba"], lw["wr"], lw["br"], lw["wi"],
                         lw["bi"], lw["lam"], lw["wb"], f"mix_fwd_{l}")
        token = None
        if first:
            (wout_b[0],) = _copies_wait([wout0], ssem_o, rsem_o, y, _gather_plan(1), "gather_wout0_d2d_wait")
            later = _copies_wait(later, ssem, rsem, y, _gather_plan(0), "gather_next_ici_wait")
            ssem, rsem, *later, token = _copies_start(later, _gather_plan(1), 3 * len(later), "gather_next_d2d_start")
        lw["wout"] = wout_b[l].reshape(2 * c, d)
        saved.append((h, u, hn, y, hs))
        h = _out_proj(h, y, lw["wout"], f"out_proj_{l}", after=token)
        if first:
            win_b[1], wout_b[1] = _copies_wait(later, ssem, rsem, h, _gather_plan(1), "gather_next_d2d_wait")
    dh, loss_lanes, d_final_g = _loss_head(h, tgt, final_g.reshape(1, d), n_meta, t_real, "loss_head")
    loss = lax.psum(loss_lanes[0, 0], ("x", "y", "c"))

    to_core = jnp.stack([my_chip, my_c])
    grads = [None] * depth
    early = None
    for l in reversed(range(depth)):
        lw = layer_w[l]
        h_in, u, hn, y, hs = saved[l]
        token = early[-1] if early else None
        dy = _out_proj_dy(dh, lw["wout"], f"out_proj_dy_{l}", after=token)
        d_wout = _out_proj_dw(y, dh, f"out_proj_dw_{l}")
        if early:
            ssem, rsem, bufs, _ = early
            bufs = _copies_wait(bufs, ssem, rsem, d_wout, _swap_plan, "early_swap_wait")
            half = len(bufs) // 2
            sums = [_pair_add(a, b, c_idx, f"early_pair_add_{k}") for k, (a, b) in enumerate(zip(bufs[:half], bufs[half:]))]
            lands = [lax.empty(p.shape, p.dtype) for p in sums]
            ssem, rsem, *bufs, token = _copies_start(sums + lands, _scatter_plan, 3 * half, "early_scatter_start")
        du, dsm, d_wr, d_wi = _mix_bwd(u, hs, dy, lw["wa"], lw["ba"], lw["wr"], lw["br"], lw["wi"], lw["bi"],
                                       lw["lam"], lw["wb"], f"mix_bwd_{l}", after=token)
        if early:
            bufs = _copies_wait(bufs, ssem, rsem, du, _scatter_plan, "early_scatter_wait")
            halves = [_chip_sum(rc, p, to_core, N_CORES, f"early_chip_sum_{k}")
                      for k, (p, rc) in enumerate(zip(bufs[:half], bufs[half:]))]
            ssem, rsem, *bufs, token = _copies_start(halves, _pair_gather_plan, half, "early_gather_start")
        d_win = _in_proj_dw(hn, du, s, f"in_proj_dw_{l}", after=token)
        srcs = [d_win.reshape(s, 2, d // 2, ns), d_wout.reshape(s, 2, dms // 2, d)]
        if early:
            early_full = _copies_wait(bufs, ssem, rsem, d_win, _pair_gather_plan, "early_gather_wait")
            from_sibling = _pair_swap(srcs, "pair_swap")
            late_sums = [_pair_add(a, b, c_idx, f"pair_add_{k}") for k, (a, b) in enumerate(zip(srcs, from_sibling))]
            lands = [lax.empty(p.shape, p.dtype) for p in late_sums]
            ssem, rsem, *bufs, token = _copies_start(late_sums + lands, _scatter_plan, 3 * len(srcs), "late_scatter_start")
        dh, d_g = _in_proj_bwd(du, lw["win"], h_in, lw["g"], dh, f"in_proj_bwd_{l}", after=token)
        if early:
            bufs = _copies_wait(bufs, ssem, rsem, dh, _scatter_plan, "late_scatter_wait")
            late_reduced = [_chip_sum(rc, p, to_core, N_CORES, f"chip_sum_{k}")
                            for k, (p, rc) in enumerate(zip(bufs[:len(srcs)], bufs[len(srcs):]))]
        grads[l] = dict(dsm=dsm, wr=_diag_blocks(d_wr, hd), wi=_diag_blocks(d_wi, hd), g=d_g)
        if l == depth - 1:
            lands = [lax.empty((a.shape[0],) + a.shape[2:], a.dtype) for a in srcs]
            ssem, rsem, *bufs, token = _copies_start(srcs + lands, _swap_plan, len(srcs), "early_swap_start")
            early = (ssem, rsem, bufs, token)
        else:
            early = None
    grad_x = dh[n_meta:t_real][None]

    sharded = []
    sp = jnp.zeros((sm_rows, s, ds), F32)
    sp = sp.at[0:n_meta].set(dh[0:n_meta].reshape(n_meta, s, ds))
    for l in range(depth):
        base = n_meta + l * SUBLANES
        dsm = grads[l]["dsm"]
        sp = sp.at[base:base + ka, :, 0:cs].set(dsm[ROW_DWA:ROW_DWA + ka].reshape(ka, s, cs))
        sp = sp.at[base + ka:base + ka + kb, :, 0:cs].set(dsm[ROW_DWB:ROW_DWB + kb].reshape(kb, s, cs))
    sharded.append(jnp.transpose(sp, (1, 0, 2)).reshape(s, 2, sm_rows // 2, ds))
    rep_parts = [jnp.concatenate([grads[l]["g"].reshape(-1) for l in range(depth)]), d_final_g.reshape(-1)]
    for row in (ROW_DBA, ROW_DBR, ROW_DBI, ROW_DLAM):
        rep_parts.append(jnp.concatenate([grads[l]["dsm"][row] for l in range(depth)]))
    rep_parts.append(jnp.concatenate([grads[l]["wr"].reshape(-1) for l in range(depth)]))
    rep_parts.append(jnp.concatenate([grads[l]["wi"].reshape(-1) for l in range(depth)]))
    rep_sizes = [p.shape[0] for p in rep_parts]
    piece = _round_up(-(-sum(rep_sizes) // (s * 2)), 2 * SUBLANES * LANES)
    flat = jnp.concatenate(rep_parts + [jnp.zeros((s * 2 * piece - sum(rep_sizes),), F32)])
    sharded.append(flat.reshape(s, 2, piece // LANES, LANES))

    from_sibling = _pair_swap(sharded, "small_pair_swap")
    pair_sums = [_pair_add(a, b, c_idx, f"small_pair_add_{k}") for k, (a, b) in enumerate(zip(sharded, from_sibling))]
    by_chip = _chip_scatter(pair_sums)
    to_device = jnp.stack([my_chip, 2 * my_chip + my_c])
    reduced_sp = _chip_sum(by_chip[0], pair_sums[0], to_core, N_CORES, "small_chip_sum")
    reduced_rep = _chip_sum(by_chip[1], pair_sums[1], to_device, N_CHIPS * N_CORES, "chip_sum_rep")
    *full, rep_all = _final_gather(late_reduced + [reduced_sp], reduced_rep)

    g_win = [full[0].reshape(d, ns), early_full[0].reshape(d, ns)]
    g_wout = [full[1].reshape(dms, d), early_full[1].reshape(dms, d)]
    g_sp = full[2].reshape(sm_rows, ds)
    rep_flat = rep_all.reshape(-1)
    rep_out, off = [], 0
    for n in rep_sizes:
        rep_out.append(rep_flat[off:off + n])
        off += n
    grad = dict(
        meta=g_sp[0:n_meta],
        norm_g=rep_out[0].reshape(depth, d),
        w_in=jnp.stack(g_win),
        conv_a_w=jnp.stack([g_sp[n_meta + l * SUBLANES:n_meta + l * SUBLANES + ka, 0:cs] for l in range(depth)]),
        conv_a_b=rep_out[2].reshape(depth, c),
        lru_wr=rep_out[6].reshape(depth, nh, hd, hd),
        lru_br=rep_out[3].reshape(depth, c),
        lru_wi=rep_out[7].reshape(depth, nh, hd, hd),
        lru_bi=rep_out[4].reshape(depth, c),
        lru_lambda=rep_out[5].reshape(depth, c),
        conv_b_w=jnp.stack([g_sp[n_meta + l * SUBLANES + ka:n_meta + l * SUBLANES + ka + kb, 0:cs]
                            for l in range(depth)]),
        w_out=jnp.stack(g_wout),
        final_g=rep_out[1].reshape(d),
    )

    delta, new_m, new_v = {}, {}, {}
    for n in names:
        shape = weights[n].shape
        two_d = (-1, shape[-1]) if len(shape) > 1 else (1, -1)
        if n in ("lru_wr", "lru_wi"):
            two_d = (-1, LANES)
        out = _adamw(weights[n].reshape(two_d), grad[n].reshape(two_d), mom1[n].reshape(two_d),
                     mom2[n].reshape(two_d), f"adamw_{n}")
        delta[n], new_m[n], new_v[n] = (o.reshape(shape) for o in out)

    return (loss, grad_x, *[grad[n] for n in names], *[delta[n] for n in names],
            *[new_m[n] for n in names], *[new_v[n] for n in names])
```

```python
import functools

import jax
import jax.numpy as jnp
from jax import lax
from jax.experimental import pallas as pl
from jax.experimental.pallas import tpu as pltpu

F32 = jnp.float32
BF16 = jnp.bfloat16

RMS_EPS = 1e-6
LRU_C = 8.0
ADAM_LR = 0.001
ADAM_B1 = 0.9
ADAM_B2 = 0.999
ADAM_EPS = 1e-08
ADAM_WD = 0.01
ADAM_STEP = 10

N_CHIPS = 4
N_CORES = 2
VMEM_LIMIT_BYTES = 56 * 1024 * 1024
SUBLANES = 8
LANES = 128
ROW_QUANTUM = 384
MIX_CHUNK = 192
GATE_BLOCK = 256
MESH = pl.DeviceIdType.MESH
ANY = pl.BlockSpec(memory_space=pl.ANY)

NT_DIMS = (((1,), (1,)), ((), ()))
TN_DIMS = (((0,), (0,)), ((), ()))


def _params(sem):
    return pltpu.CompilerParams(dimension_semantics=sem, vmem_limit_bytes=VMEM_LIMIT_BYTES)


def _sig(x):
    return 0.5 * jnp.tanh(0.5 * x) + 0.5


def _row_tile(t):
    return 704 if t % 704 == 0 else 192


def _col_tile(n, prefs):
    for p in prefs:
        if n % p == 0:
            return p
    return n


def _slab_rows(rows, cols):
    if rows * cols * 4 <= 1024 * 1024:
        return rows
    return _col_tile(rows, (256, 128, 64, 32, 16))


def _norm_in(h, g, wg, name):
    t, d = h.shape
    s, _, ns = wg.shape
    tm = _row_tile(t)
    tn = _col_tile(ns, (1536, 512, 384, 128))
    nb = ns // tn

    def body(h_ref, g_ref, w_ref, u_ref, hn_ref):
        @pl.when(pl.program_id(1) == 0)
        def _():
            x = h_ref[...]
            r = lax.rsqrt(jnp.mean(x * x, axis=-1, keepdims=True) + RMS_EPS)
            hn_ref[...] = ((x * r) * g_ref[...]).astype(BF16)

        u_ref[...] = jnp.dot(hn_ref[...], w_ref[...], preferred_element_type=F32)

    return pl.pallas_call(
        body, name=name, grid=(t // tm, s * nb),
        in_specs=[pl.BlockSpec((tm, d), lambda i, n: (i, 0)),
                  pl.BlockSpec((1, d), lambda i, n: (0, 0)),
                  pl.BlockSpec((None, d, tn), lambda i, n: (n // nb, 0, n % nb))],
        out_specs=[pl.BlockSpec((tm, tn), lambda i, n: (i, n)),
                   pl.BlockSpec((tm, d), lambda i, n: (i, 0))],
        out_shape=[jax.ShapeDtypeStruct((t, s * ns), F32), jax.ShapeDtypeStruct((t, d), BF16)],
        compiler_params=_params(("arbitrary", "arbitrary")),
    )(h, g, wg)


def _decay_consts(lam):
    z = -lam
    e = jnp.exp(-jnp.abs(z))
    u = 1.0 + e
    log1p_e = jnp.where(u == 1.0, e, jnp.log(u) * (e / (u - 1.0)))
    sp = jnp.maximum(z, 0.0) + log1p_e
    return -LRU_C * sp, LRU_C * _sig(z)


def _gates(xc, wr_ref, br_ref, wi_ref, bi_ref, c8, j, gb):
    sl = slice(j * gb, (j + 1) * gb)
    x16 = xc.astype(BF16)
    r = _sig(jnp.dot(x16, wr_ref[j], preferred_element_type=F32) + br_ref[:, sl])
    ig = _sig(jnp.dot(x16, wi_ref[j], preferred_element_type=F32) + bi_ref[:, sl])
    la = c8[:, sl] * r
    a = jnp.exp(la)
    sq = jnp.sqrt(-jnp.tanh(la) * (a * a + 1.0))
    return r, ig, a, sq


def _mix_fwd(u, wa, ba, wr, br, wi, bi, lam, wb, name):
    t = u.shape[0]
    c = u.shape[1] // 6
    tc = MIX_CHUNK
    gb = wr.shape[1]
    nblk = c // gb
    ka, kb = wa.shape[0], wb.shape[0]

    def body(u_ref, wa_ref, ba_ref, wr_ref, br_ref, wi_ref, bi_ref, lam_ref, wb_ref,
             y_ref, hs_ref, xa_ext, v_ext, xc_s, a_s, b_s, carry_s):
        @pl.when(pl.program_id(0) == 0)
        def _():
            xa_ext[0:SUBLANES, :] = jnp.zeros((SUBLANES, c), F32)
            v_ext[0:SUBLANES, :] = jnp.zeros((SUBLANES, c), F32)
            carry_s[...] = jnp.zeros_like(carry_s)

        xa_ext[SUBLANES:SUBLANES + tc, :] = u_ref[:, 0:c]
        xc = ba_ref[...]
        for k in range(ka):
            xc = xc + wa_ref[pl.ds(k, 1), :] * xa_ext[pl.ds(SUBLANES - (ka - 1) + k, tc), :]
        xc_s[...] = xc
        c8, _ = _decay_consts(lam_ref[...])
        for j in range(nblk):
            sl = slice(j * gb, (j + 1) * gb)
            xcj = xc_s[:, sl]
            _, ig, a, sq = _gates(xcj, wr_ref, br_ref, wi_ref, bi_ref, c8, j, gb)
            a_s[:, sl] = a
            b_s[:, sl] = sq * (ig * xcj)

        row = lax.broadcasted_iota(jnp.int32, (SUBLANES, c), 0)

        def scan_step(j, _):
            off = pl.multiple_of(j * SUBLANES, SUBLANES)
            av = a_s[pl.ds(off, SUBLANES), :]
            bv = b_s[pl.ds(off, SUBLANES), :]
            for d in (1, 2, 4):
                keep = row >= d
                bsh = jnp.where(keep, pltpu.roll(bv, d, axis=0), 0.0)
                ash = jnp.where(keep, pltpu.roll(av, d, axis=0), 1.0)
                bv = av * bsh + bv
                av = av * ash
            hv = av * carry_s[...] + bv
            hs_ref[pl.ds(off, SUBLANES), :] = hv
            carry_s[...] = hs_ref[pl.ds(off + SUBLANES - 1, 1), :]
            return 0

        lax.fori_loop(0, tc // SUBLANES, scan_step, 0)

        ga = u_ref[:, c:2 * c]
        y_ref[:, 0:c] = (hs_ref[...] * (ga * _sig(ga))).astype(BF16)

        v_ext[SUBLANES:SUBLANES + tc, :] = u_ref[:, 3 * c:4 * c] * u_ref[:, 4 * c:5 * c]
        cv = wb_ref[pl.ds(0, 1), :] * v_ext[pl.ds(SUBLANES - (kb - 1), tc), :]
        for k in range(1, kb):
            cv = cv + wb_ref[pl.ds(k, 1), :] * v_ext[pl.ds(SUBLANES - (kb - 1) + k, tc), :]
        gbv = u_ref[:, 5 * c:6 * c]
        y_ref[:, c:2 * c] = (u_ref[:, 2 * c:3 * c] * cv * (gbv * _sig(gbv))).astype(BF16)

        xa_ext[0:SUBLANES, :] = xa_ext[tc:tc + SUBLANES, :]
        v_ext[0:SUBLANES, :] = v_ext[tc:tc + SUBLANES, :]

    full = lambda shape: pl.BlockSpec(shape, lambda i: (0,) * len(shape))
    return pl.pallas_call(
        body, name=name, grid=(t // tc,),
        in_specs=[pl.BlockSpec((tc, 6 * c), lambda i: (i, 0)),
                  full(wa.shape), full(ba.shape), full(wr.shape), full(br.shape),
                  full(wi.shape), full(bi.shape), full(lam.shape), full(wb.shape)],
        out_specs=[pl.BlockSpec((tc, 2 * c), lambda i: (i, 0)),
                   pl.BlockSpec((tc, c), lambda i: (i, 0))],
        out_shape=[jax.ShapeDtypeStruct((t, 2 * c), BF16), jax.ShapeDtypeStruct((t, c), F32)],
        scratch_shapes=[pltpu.VMEM((tc + SUBLANES, c), F32), pltpu.VMEM((tc + SUBLANES, c), F32),
                        pltpu.VMEM((tc, c), F32), pltpu.VMEM((tc, c), F32), pltpu.VMEM((tc, c), F32),
                        pltpu.VMEM((1, c), F32)],
        compiler_params=_params(("arbitrary",)),
    )(u, wa, ba, wr, br, wi, bi, lam, wb)


ROW_DWA = 0
ROW_DBA = 4
ROW_DBR = 5
ROW_DBI = 6
ROW_DLAM = 7
ROW_DWB = 8
SMALL_ROWS = 16


def _mix_bwd(u, hs, dy, wa, ba, wr, br, wi, bi, lam, wb, name, after=None):
    t = u.shape[0]
    c = u.shape[1] // 6
    tc = MIX_CHUNK
    nt = t // tc
    gb = wr.shape[1]
    nblk = c // gb
    ka, kb = wa.shape[0], wb.shape[0]
    assert ka <= ROW_DBA and kb <= SMALL_ROWS - ROW_DWB
    hb = tc // SUBLANES

    def body(u_ref, uh_ref, hs_ref, hsh_ref, dy_ref, wa_ref, ba_ref, wr_ref, br_ref, wi_ref, bi_ref, lam_ref, wb_ref,
             du_ref, dsm_ref, dwr_ref, dwi_ref,
             xa_ext, v_ext, hs_ext, a_ext, ds_ext, dxc_ext, dcv_ext, xc_s, r_s, i_s, sq_s, g_s, an_s):
        i = pl.program_id(0)
        chunk = nt - 1 - i
        tail = slice(tc, tc + SUBLANES)
        head = slice(0, SUBLANES)

        @pl.when(i == 0)
        def _():
            zero = jnp.zeros((SUBLANES, c), F32)
            a_ext[tail, :] = zero
            ds_ext[tail, :] = zero
            dxc_ext[tail, :] = zero
            dcv_ext[tail, :] = zero
            dsm_ref[...] = jnp.zeros_like(dsm_ref)
            dwr_ref[...] = jnp.zeros_like(dwr_ref)
            dwi_ref[...] = jnp.zeros_like(dwi_ref)

        prev = jnp.where(chunk > 0, 1.0, 0.0)
        xa_ext[head, :] = uh_ref[:, 0:c] * prev
        xa_ext[SUBLANES:SUBLANES + tc, :] = u_ref[:, 0:c]
        v_ext[head, :] = uh_ref[:, 3 * c:4 * c] * uh_ref[:, 4 * c:5 * c] * prev
        v_ext[SUBLANES:SUBLANES + tc, :] = u_ref[:, 3 * c:4 * c] * u_ref[:, 4 * c:5 * c]
        hs_ext[head, :] = hsh_ref[...] * prev
        hs_ext[SUBLANES:SUBLANES + tc, :] = hs_ref[...]

        xc = ba_ref[...]
        for k in range(ka):
            xc = xc + wa_ref[pl.ds(k, 1), :] * xa_ext[pl.ds(SUBLANES - (ka - 1) + k, tc), :]
        xc_s[...] = xc
        c8, dc8 = _decay_consts(lam_ref[...])
        for j in range(nblk):
            sl = slice(j * gb, (j + 1) * gb)
            r, ig, a, sq = _gates(xc_s[:, sl], wr_ref, br_ref, wi_ref, bi_ref, c8, j, gb)
            r_s[:, sl] = r
            i_s[:, sl] = ig
            sq_s[:, sl] = sq
            a_ext[0:tc, sl] = a

        ga = u_ref[:, c:2 * c]
        sga = _sig(ga)
        g_s[...] = dy_ref[:, 0:c] * (ga * sga)
        an_s[...] = a_ext[pl.ds(1, tc), :]

        row = lax.broadcasted_iota(jnp.int32, (SUBLANES, c), 0)

        def scan_step(j, _):
            off = pl.multiple_of(tc - SUBLANES - j * SUBLANES, SUBLANES)
            av = an_s[pl.ds(off, SUBLANES), :]
            bv = g_s[pl.ds(off, SUBLANES), :]
            for d in (1, 2, 4):
                keep = row < SUBLANES - d
                bsh = jnp.where(keep, pltpu.roll(bv, SUBLANES - d, axis=0), 0.0)
                ash = jnp.where(keep, pltpu.roll(av, SUBLANES - d, axis=0), 1.0)
                bv = av * bsh + bv
                av = av * ash
            ds_ext[pl.ds(off, SUBLANES), :] = av * ds_ext[pl.ds(off + SUBLANES, 1), :] + bv
            return 0

        lax.fori_loop(0, tc // SUBLANES, scan_step, 0)

        def acc(row_index, val):
            dsm_ref[pl.ds(row_index, 1), :] += jnp.sum(val, axis=0, keepdims=True)

        def acc_block(row_index, sl, val):
            dsm_ref[pl.ds(row_index, 1), sl] += jnp.sum(val, axis=0, keepdims=True)

        for j in range(nblk):
            sl = slice(j * gb, (j + 1) * gb)
            ds = ds_ext[0:tc, sl]
            hprev = hs_ext[pl.ds(SUBLANES - 1, tc), sl]
            a = a_ext[0:tc, sl]
            sq = sq_s[:, sl]
            ig = i_s[:, sl]
            r = r_s[:, sl]
            xcj = xc_s[:, sl]
            t1 = ds * xcj
            dla = (ds * hprev) * a - (t1 * ig) * ((a * a) / sq)
            acc_block(ROW_DLAM, sl, dla * r)
            dpr = (dla * c8[:, sl]) * (r * (1.0 - r))
            dpi = (t1 * sq) * (ig * (1.0 - ig))
            acc_block(ROW_DBR, sl, dpr)
            acc_block(ROW_DBI, sl, dpi)
            p16 = dpr.astype(BF16)
            q16 = dpi.astype(BF16)
            x16 = xcj.astype(BF16)
            dwr_ref[j] += lax.dot_general(x16, p16, TN_DIMS, preferred_element_type=F32)
            dwi_ref[j] += lax.dot_general(x16, q16, TN_DIMS, preferred_element_type=F32)
            dxc = (ds * (sq * ig)
                   + lax.dot_general(p16, wr_ref[j], NT_DIMS, preferred_element_type=F32)
                   + lax.dot_general(q16, wi_ref[j], NT_DIMS, preferred_element_type=F32))
            dxc_ext[0:tc, sl] = dxc
            acc_block(ROW_DBA, sl, dxc)

        dsilu_a = sga * (1.0 + ga * (1.0 - sga))
        du_ref[:, c:2 * c] = (dy_ref[:, 0:c] * hs_ref[...] * dsilu_a).astype(BF16)

        dxc = dxc_ext[0:tc, :]
        xa_now = xa_ext[SUBLANES:SUBLANES + tc, :]
        dxa = wa_ref[pl.ds(ka - 1, 1), :] * dxc
        acc(ROW_DWA + ka - 1, dxc * xa_now)
        for k in range(ka - 1):
            ahead = dxc_ext[pl.ds(ka - 1 - k, tc), :]
            acc(ROW_DWA + k, ahead * xa_now)
            dxa = dxa + wa_ref[pl.ds(k, 1), :] * ahead
        du_ref[:, 0:c] = dxa.astype(BF16)

        cv = wb_ref[pl.ds(0, 1), :] * v_ext[pl.ds(SUBLANES - (kb - 1), tc), :]
        for k in range(1, kb):
            cv = cv + wb_ref[pl.ds(k, 1), :] * v_ext[pl.ds(SUBLANES - (kb - 1) + k, tc), :]
        gbv = u_ref[:, 5 * c:6 * c]
        sgb = _sig(gbv)
        silu_b = gbv * sgb
        dyb = dy_ref[:, c:2 * c]
        gB = u_ref[:, 2 * c:3 * c]
        du_ref[:, 2 * c:3 * c] = (dyb * cv * silu_b).astype(BF16)
        du_ref[:, 5 * c:6 * c] = (dyb * gB * cv * (sgb * (1.0 + gbv * (1.0 - sgb)))).astype(BF16)
        dcv = dyb * gB * silu_b
        dcv_ext[0:tc, :] = dcv
        dv = wb_ref[pl.ds(kb - 1, 1), :] * dcv
        v_now = v_ext[SUBLANES:SUBLANES + tc, :]
        acc(ROW_DWB + kb - 1, dcv * v_now)
        for k in range(kb - 1):
            ahead = dcv_ext[pl.ds(kb - 1 - k, tc), :]
            acc(ROW_DWB + k, ahead * v_now)
            dv = dv + wb_ref[pl.ds(k, 1), :] * ahead
        du_ref[:, 3 * c:4 * c] = (dv * u_ref[:, 4 * c:5 * c]).astype(BF16)
        du_ref[:, 4 * c:5 * c] = (dv * u_ref[:, 3 * c:4 * c]).astype(BF16)

        a_ext[tail, :] = a_ext[head, :]
        ds_ext[tail, :] = ds_ext[head, :]
        dxc_ext[tail, :] = dxc_ext[head, :]
        dcv_ext[tail, :] = dcv_ext[head, :]

        @pl.when(i == nt - 1)
        def _():
            dsm_ref[pl.ds(ROW_DLAM, 1), :] = dsm_ref[pl.ds(ROW_DLAM, 1), :] * dc8

    full = lambda shape: pl.BlockSpec(shape, lambda i: (0,) * len(shape))
    rev = lambda i: (nt - 1 - i, 0)
    halo = lambda i: (jnp.maximum((nt - 1 - i) * hb - 1, 0), 0)
    ext = pltpu.VMEM((tc + SUBLANES, c), F32)
    blk = pltpu.VMEM((tc, c), F32)
    body, more_specs, more = _behind(body, 13, after)
    return pl.pallas_call(
        body, name=name, grid=(nt,),
        in_specs=[pl.BlockSpec((tc, 6 * c), rev), pl.BlockSpec((SUBLANES, 6 * c), halo),
                  pl.BlockSpec((tc, c), rev), pl.BlockSpec((SUBLANES, c), halo),
                  pl.BlockSpec((tc, 2 * c), rev),
                  full(wa.shape), full(ba.shape), full(wr.shape), full(br.shape),
                  full(wi.shape), full(bi.shape), full(lam.shape), full(wb.shape)] + more_specs,
        out_specs=[pl.BlockSpec((tc, 6 * c), rev), full((SMALL_ROWS, c)), full(wr.shape), full(wi.shape)],
        out_shape=[jax.ShapeDtypeStruct((t, 6 * c), BF16), jax.ShapeDtypeStruct((SMALL_ROWS, c), F32),
                   jax.ShapeDtypeStruct(wr.shape, F32), jax.ShapeDtypeStruct(wi.shape, F32)],
        scratch_shapes=[ext] * 7 + [blk] * 6,
        compiler_params=_params(("arbitrary",)),
    )(u, u, hs, hs, dy, wa, ba, wr, br, wi, bi, lam, wb, *more)


def _behind(body, n_in, after):
    if after is None:
        return body, [], []
    return (lambda *refs: body(*refs[:n_in], *refs[n_in + 1:])), [ANY], [after]


def _out_proj(h, y, w, name, after=None):
    t, d = h.shape
    dm = y.shape[1]
    tm = _row_tile(t)
    tn = _col_tile(d, (1024, 512, 256))

    def body(h_ref, y_ref, w_ref, o_ref):
        o_ref[...] = h_ref[...] + jnp.dot(y_ref[...], w_ref[...], preferred_element_type=F32)

    body, more_specs, more = _behind(body, 3, after)
    return pl.pallas_call(
        body, name=name, grid=(d // tn, t // tm),
        in_specs=[pl.BlockSpec((tm, tn), lambda n, i: (i, n)),
                  pl.BlockSpec((tm, dm), lambda n, i: (i, 0)),
                  pl.BlockSpec((dm, tn), lambda n, i: (0, n))] + more_specs,
        out_specs=pl.BlockSpec((tm, tn), lambda n, i: (i, n)),
        out_shape=jax.ShapeDtypeStruct((t, d), F32),
        compiler_params=_params(("arbitrary", "arbitrary")),
    )(h, y, w, *more)


def _out_proj_dy(dout, w, name, after=None):
    t, d = dout.shape
    dm = w.shape[0]
    tm = _row_tile(t)
    tn = _col_tile(dm, (1024, 512, 256))

    def body(g_ref, w_ref, o_ref):
        o_ref[...] = lax.dot_general(g_ref[...].astype(BF16), w_ref[...], NT_DIMS, preferred_element_type=F32)

    body, more_specs, more = _behind(body, 2, after)
    return pl.pallas_call(
        body, name=name, grid=(dm // tn, t // tm),
        in_specs=[pl.BlockSpec((tm, d), lambda n, i: (i, 0)),
                  pl.BlockSpec((tn, d), lambda n, i: (n, 0))] + more_specs,
        out_specs=pl.BlockSpec((tm, tn), lambda n, i: (i, n)),
        out_shape=jax.ShapeDtypeStruct((t, dm), F32),
        compiler_params=_params(("arbitrary", "arbitrary")),
    )(dout, w, *more)


def _out_proj_dw(y, dout, name):
    t, dm = y.shape
    d = dout.shape[1]
    tmm = _col_tile(dm, (512, 256))
    tn = _col_tile(d, (512, 256))

    def body(y_ref, g_ref, o_ref):
        o_ref[...] = lax.dot_general(y_ref[...], g_ref[...].astype(BF16), TN_DIMS, preferred_element_type=F32)

    return pl.pallas_call(
        body, name=name, grid=(d // tn, dm // tmm),
        in_specs=[pl.BlockSpec((t, tmm), lambda n, m: (0, m)),
                  pl.BlockSpec((t, tn), lambda n, m: (0, n))],
        out_specs=pl.BlockSpec((tmm, tn), lambda n, m: (m, n)),
        out_shape=jax.ShapeDtypeStruct((dm, d), F32),
        compiler_params=_params(("arbitrary", "arbitrary")),
    )(y, dout)


def _in_proj_bwd(du, wg, h, g, dout, name, after=None):
    t, d = h.shape
    s, _, ns = wg.shape
    tm = _row_tile(t)
    tk = _col_tile(ns, (512, 384, 128))
    nb = ns // tk
    nk = s * nb

    def body(du_ref, w_ref, h_ref, g_ref, dout_ref, dh_ref, dg_ref, acc_ref):
        i, k = pl.program_id(0), pl.program_id(1)

        @pl.when(k == 0)
        def _():
            acc_ref[...] = jnp.zeros_like(acc_ref)

        @pl.when((k == 0) & (i == 0))
        def _():
            dg_ref[...] = jnp.zeros_like(dg_ref)

        acc_ref[...] += lax.dot_general(du_ref[...], w_ref[...], NT_DIMS, preferred_element_type=F32)

        @pl.when(k == nk - 1)
        def _():
            x = h_ref[...]
            dhn = acc_ref[...]
            r = lax.rsqrt(jnp.mean(x * x, axis=-1, keepdims=True) + RMS_EPS)
            gd = dhn * g_ref[...]
            dot = jnp.mean(gd * x, axis=-1, keepdims=True)
            dh_ref[...] = dout_ref[...] + (r * gd - x * ((r * r * r) * dot))
            dg_ref[...] += jnp.sum(dhn * (x * r), axis=0, keepdims=True)

    body, more_specs, more = _behind(body, 5, after)
    return pl.pallas_call(
        body, name=name, grid=(t // tm, nk),
        in_specs=[pl.BlockSpec((tm, tk), lambda i, k: (i, k)),
                  pl.BlockSpec((None, d, tk), lambda i, k: (k // nb, 0, k % nb)),
                  pl.BlockSpec((tm, d), lambda i, k: (i, 0)),
                  pl.BlockSpec((1, d), lambda i, k: (0, 0)),
                  pl.BlockSpec((tm, d), lambda i, k: (i, 0))] + more_specs,
        out_specs=[pl.BlockSpec((tm, d), lambda i, k: (i, 0)),
                   pl.BlockSpec((1, d), lambda i, k: (0, 0))],
        out_shape=[jax.ShapeDtypeStruct((t, d), F32), jax.ShapeDtypeStruct((1, d), F32)],
        scratch_shapes=[pltpu.VMEM((tm, d), F32)],
        compiler_params=_params(("arbitrary", "arbitrary")),
    )(du, wg, h, g, dout, *more)


def _in_proj_dw(hn, du, s, name, after=None):
    t, d = hn.shape
    ns = du.shape[1] // s
    tmm = _col_tile(d, (512, 256))
    tn = _col_tile(ns, (768, 384, 128))
    nb = ns // tn

    def body(hn_ref, du_ref, o_ref):
        o_ref[...] = lax.dot_general(hn_ref[...], du_ref[...], TN_DIMS, preferred_element_type=F32)

    body, more_specs, more = _behind(body, 2, after)
    return pl.pallas_call(
        body, name=name, grid=(s * nb, d // tmm),
        in_specs=[pl.BlockSpec((t, tmm), lambda n, m: (0, m)),
                  pl.BlockSpec((t, tn), lambda n, m: (0, n))] + more_specs,
        out_specs=pl.BlockSpec((None, tmm, tn), lambda n, m: (n // nb, m, n % nb)),
        out_shape=jax.ShapeDtypeStruct((s, d, ns), F32),
        compiler_params=_params(("arbitrary", "arbitrary")),
    )(hn, du, *more)


def _loss_head(h, tgt, g, n_meta, t_real, name):
    t, d = h.shape
    tm = _row_tile(t)

    def body(h_ref, t_ref, g_ref, dh_ref, loss_ref, dg_ref):
        i = pl.program_id(0)

        @pl.when(i == 0)
        def _():
            loss_ref[...] = jnp.zeros_like(loss_ref)
            dg_ref[...] = jnp.zeros_like(dg_ref)

        x = h_ref[...]
        gv = g_ref[...]
        r = lax.rsqrt(jnp.mean(x * x, axis=-1, keepdims=True) + RMS_EPS)
        xr = x * r
        rows = i * tm + lax.broadcasted_iota(jnp.int32, (tm, 1), 0)
        valid = (rows >= n_meta) & (rows < t_real)
        err = jnp.where(valid, xr * gv - t_ref[...], 0.0)
        loss_ref[...] += 0.5 * jnp.sum(jnp.mean(err * err, axis=-1, keepdims=True))
        dy = err * (1.0 / d)
        gd = dy * gv
        dot = jnp.mean(gd * x, axis=-1, keepdims=True)
        dh_ref[...] = r * gd - x * ((r * r * r) * dot)
        dg_ref[...] += jnp.sum(dy * xr, axis=0, keepdims=True)

    return pl.pallas_call(
        body, name=name, grid=(t // tm,),
        in_specs=[pl.BlockSpec((tm, d), lambda i: (i, 0)),
                  pl.BlockSpec((tm, d), lambda i: (i, 0)),
                  pl.BlockSpec((1, d), lambda i: (0, 0))],
        out_specs=[pl.BlockSpec((tm, d), lambda i: (i, 0)),
                   pl.BlockSpec((1, LANES), lambda i: (0, 0)),
                   pl.BlockSpec((1, d), lambda i: (0, 0))],
        out_shape=[jax.ShapeDtypeStruct((t, d), F32), jax.ShapeDtypeStruct((1, LANES), F32),
                   jax.ShapeDtypeStruct((1, d), F32)],
        compiler_params=_params(("arbitrary",)),
    )(h, tgt, g)


def _adamw(w, g, m, v, name):
    rows, cols = w.shape
    tr = rows
    for cand in (512, 256, 128, 64, 32, 16, 8):
        if rows % cand == 0 and cand * cols * 4 <= 2 * 1024 * 1024:
            tr = cand
            break

    def body(w_ref, g_ref, m_ref, v_ref, d_ref, nm_ref, nv_ref):
        gv = g_ref[...]
        m2 = ADAM_B1 * m_ref[...] + (1.0 - ADAM_B1) * gv
        v2 = ADAM_B2 * v_ref[...] + (1.0 - ADAM_B2) * (gv * gv)
        m_hat = m2 / (1.0 - ADAM_B1 ** ADAM_STEP)
        v_hat = v2 / (1.0 - ADAM_B2 ** ADAM_STEP)
        d_ref[...] = -ADAM_LR * (m_hat / (jnp.sqrt(v_hat) + ADAM_EPS) + ADAM_WD * w_ref[...])
        nm_ref[...] = m2
        nv_ref[...] = v2

    spec = pl.BlockSpec((tr, cols), lambda i: (i, 0))
    return pl.pallas_call(
        body, name=name, grid=(rows // tr,),
        in_specs=[spec] * 4, out_specs=[spec] * 3,
        out_shape=[jax.ShapeDtypeStruct((rows, cols), F32)] * 3,
        compiler_params=_params(("arbitrary",)),
    )(w, g, m, v)


def _pair_add(x, ra, c_idx, name):
    s, _, rows, cols = x.shape
    tr = _slab_rows(rows, cols)

    def body(c_ref, x_ref, r_ref, o_ref):
        o_ref[...] = (x_ref[...] + r_ref[...]).astype(BF16)

    return pl.pallas_call(
        body, name=name,
        grid_spec=pltpu.PrefetchScalarGridSpec(
            num_scalar_prefetch=1, grid=(s, rows // tr),
            in_specs=[pl.BlockSpec((None, None, tr, cols), lambda a, i, c_ref: (a, c_ref[0], i, 0)),
                      pl.BlockSpec((None, tr, cols), lambda a, i, c_ref: (a, i, 0))],
            out_specs=pl.BlockSpec((None, tr, cols), lambda a, i, c_ref: (a, i, 0))),
        out_shape=jax.ShapeDtypeStruct((s, rows, cols), BF16),
        compiler_params=_params(("arbitrary", "arbitrary")),
    )(c_idx, x, ra)


def _chip_sum(rc, p, where, n_slots, name):
    s, rows, cols = rc.shape
    tr = _slab_rows(rows, cols)

    def body(w_ref, x_ref, p_ref, o_ref):
        me = w_ref[0]
        total = jnp.where(me == 0, p_ref[...], x_ref[0]).astype(F32)
        for a in range(1, s):
            total = total + jnp.where(me == a, p_ref[...], x_ref[a]).astype(F32)
        o_ref[...] = total

    return pl.pallas_call(
        body, name=name,
        grid_spec=pltpu.PrefetchScalarGridSpec(
            num_scalar_prefetch=1, grid=(rows // tr,),
            in_specs=[pl.BlockSpec((s, tr, cols), lambda i, w_ref: (0, i, 0)),
                      pl.BlockSpec((None, tr, cols), lambda i, w_ref: (w_ref[0], i, 0))],
            out_specs=pl.BlockSpec((None, tr, cols), lambda i, w_ref: (w_ref[1], i, 0))),
        out_shape=jax.ShapeDtypeStruct((n_slots, rows, cols), F32),
        compiler_params=_params(("arbitrary",)),
    )(where, rc, p)


def _cast_place(w, layer, me_idx, name, after=None):
    _, rows, cols = w.shape
    tr = _slab_rows(rows, cols)

    def body(m_ref, w_ref, o_ref):
        o_ref[...] = w_ref[...].astype(BF16)

    body, more_specs, more = _behind(body, 2, after)
    return pl.pallas_call(
        body, name=name,
        grid_spec=pltpu.PrefetchScalarGridSpec(
            num_scalar_prefetch=1, grid=(rows // tr,),
            in_specs=[pl.BlockSpec((None, tr, cols), lambda i, m_ref: (layer, i, 0))] + more_specs,
            out_specs=pl.BlockSpec((None, tr, cols), lambda i, m_ref: (m_ref[0], i, 0))),
        out_shape=jax.ShapeDtypeStruct((N_CHIPS, rows, cols), BF16),
        compiler_params=_params(("arbitrary",)),
    )(me_idx, w, *more)


def _place():
    x, y, c = lax.axis_index("x"), lax.axis_index("y"), lax.axis_index("c")
    chips = [(1 - x, y), (x, 1 - y), (1 - x, 1 - y)]
    return x, y, c, chips


def _chip_index(cx, cy):
    return 2 * cx + cy


def _gather_copies(bufs, stage):
    x, y, c, chips = _place()
    me = _chip_index(x, y)
    copies = []
    for b in bufs:
        for chip in chips:
            src = _chip_index(*chip)
            if stage == 0:
                copies.append((b.at[me, c], (*chip, c), b.at[src, c]))
            else:
                copies.append((b.at[src, c], (x, y, 1 - c), b.at[src, 1 - c]))
    return copies


def _remote(ref, peer, ssem, rsem, k):
    return pltpu.make_async_remote_copy(src_ref=ref, dst_ref=ref, send_sem=ssem.at[k], recv_sem=rsem.at[k],
                                        device_id=peer, device_id_type=MESH)


def _gather_first(bufs, small):
    n = len(bufs)
    k = 3 * n

    def body(*refs):
        sm_ref = refs[n]
        b_refs, smg_ref = refs[n + 1:2 * n + 1], refs[2 * n + 1]
        lsem, ssem, rsem = refs[2 * n + 2:]
        x, y, c, chips = _place()
        me = _chip_index(x, y)
        local = pltpu.make_async_copy(sm_ref, smg_ref.at[me], lsem)
        local.start()
        first = _gather_copies(b_refs, 0)
        second = _gather_copies(b_refs, 1)
        started = []
        for i, (ref, peer, _) in enumerate(first):
            started.append(_remote(ref, peer, ssem, rsem, i))
        for j, chip in enumerate(chips):
            started.append(pltpu.make_async_remote_copy(
                src_ref=sm_ref, dst_ref=smg_ref.at[me], send_sem=ssem.at[2 * k + j], recv_sem=rsem.at[2 * k + j],
                device_id=(*chip, c), device_id_type=MESH))
        for cp in started:
            cp.start()
        for i, (_, peer, lands) in enumerate(first):
            _remote(lands, peer, ssem, rsem, i).wait_recv()
            ref, sib, _ = second[i]
            fwd = _remote(ref, sib, ssem, rsem, k + i)
            fwd.start()
            started.append(fwd)
        for i, (_, sib, lands) in enumerate(second):
            _remote(lands, sib, ssem, rsem, k + i).wait_recv()
        for j, chip in enumerate(chips):
            theirs = smg_ref.at[_chip_index(*chip)]
            pltpu.make_async_remote_copy(src_ref=theirs, dst_ref=theirs, send_sem=ssem.at[2 * k + j],
                                         recv_sem=rsem.at[2 * k + j], device_id=(*chip, c),
                                         device_id_type=MESH).wait_recv()
        for cp in started:
            cp.wait_send()
        local.wait()

    return pl.pallas_call(
        body, name="gather_first",
        in_specs=[ANY] * (n + 1), out_specs=[ANY] * (n + 1),
        out_shape=[jax.ShapeDtypeStruct(b.shape, b.dtype) for b in bufs]
        + [jax.ShapeDtypeStruct((N_CHIPS,) + small.shape, small.dtype)],
        input_output_aliases={i: i for i in range(n)},
        scratch_shapes=[pltpu.SemaphoreType.DMA, pltpu.SemaphoreType.DMA((2 * k + 3,)),
                        pltpu.SemaphoreType.DMA((2 * k + 3,))],
    )(*bufs, small)


HBM = pl.BlockSpec(memory_space=pltpu.HBM)
SEM = pl.BlockSpec(memory_space=pltpu.SEMAPHORE)
DATAFLOW = pltpu.SideEffectType.DATAFLOW_SIDE_EFFECTING


def _copies_start(bufs, plan, n_copies, name, after=None):
    n = len(bufs)
    extra = [] if after is None else [after]

    def body(*refs):
        refs = refs[:n] + refs[n + len(extra):]
        ssem, rsem = refs[n], refs[n + 1]
        b_refs, token = refs[n + 2:2 * n + 2], refs[2 * n + 2]
        copies = plan(b_refs)
        assert len(copies) == n_copies
        for i, (src, dst, peer, _) in enumerate(copies):
            pltpu.make_async_remote_copy(src_ref=src, dst_ref=dst, send_sem=ssem.at[i], recv_sem=rsem.at[i],
                                         device_id=peer, device_id_type=MESH).start()
        token[...] = jnp.zeros_like(token)

    return pl.pallas_call(
        body, name=name,
        out_shape=(pltpu.SemaphoreType.DMA((n_copies,)), pltpu.SemaphoreType.DMA((n_copies,)),
                   *[pltpu.HBM(b.shape, b.dtype) for b in bufs], jax.ShapeDtypeStruct((SUBLANES, LANES), F32)),
        in_specs=[HBM] * n + [ANY] * len(extra),
        out_specs=(SEM, SEM, *[HBM] * n, pl.BlockSpec(memory_space=pltpu.VMEM)),
        input_output_aliases={i: 2 + i for i in range(n)},
        compiler_params=pltpu.CompilerParams(has_side_effects=DATAFLOW),
    )(*[pltpu.with_memory_space_constraint(b, pltpu.HBM) for b in bufs], *extra)


def _copies_wait(bufs, ssem, rsem, after, plan, name):
    n = len(bufs)
    afters = list(after) if isinstance(after, (list, tuple)) else [after]

    def body(*refs):
        b_refs, ssem_ref, rsem_ref = refs[:n], refs[n], refs[n + 1]
        for i, (src, dst, peer, lands) in enumerate(plan(b_refs)):
            pltpu.make_async_remote_copy(src_ref=src, dst_ref=dst, send_sem=ssem_ref.at[i], recv_sem=rsem_ref.at[i],
                                         device_id=peer, device_id_type=MESH).wait_send()
            pltpu.make_async_remote_copy(src_ref=lands, dst_ref=lands, send_sem=ssem_ref.at[i],
                                         recv_sem=rsem_ref.at[i], device_id=peer, device_id_type=MESH).wait_recv()

    return pl.pallas_call(
        body, name=name,
        out_shape=tuple(pltpu.HBM(b.shape, b.dtype) for b in bufs),
        in_specs=[HBM] * n + [SEM, SEM] + [ANY] * len(afters), out_specs=tuple([HBM] * n),
        input_output_aliases={i: i for i in range(n)},
        compiler_params=pltpu.CompilerParams(has_side_effects=DATAFLOW),
    )(*bufs, ssem, rsem, *afters)


def _gather_plan(stage):
    return lambda refs: [(ref, ref, peer, lands) for ref, peer, lands in _gather_copies(refs, stage)]


def _swap_plan(refs):
    n = len(refs) // 2
    x, y, c, _ = _place()
    return [(refs[a].at[:, 1 - c], refs[n + a], (x, y, 1 - c), refs[n + a]) for a in range(n)]


def _scatter_plan(refs):
    n = len(refs) // 2
    x, y, c, chips = _place()
    me = _chip_index(x, y)
    return [(refs[a].at[_chip_index(*chip)], refs[n + a].at[me], (*chip, c), refs[n + a].at[_chip_index(*chip)])
            for a in range(n) for chip in chips]


def _pair_gather_plan(refs):
    x, y, c, _ = _place()
    return [(r.at[c], r.at[c], (x, y, 1 - c), r.at[1 - c]) for r in refs]


def _pair_swap(xs, name):
    n = len(xs)

    def body(*refs):
        x_refs, o_refs, ssem, rsem = refs[:n], refs[n:2 * n], refs[2 * n], refs[2 * n + 1]
        x, y, c, _ = _place()
        copies = [pltpu.make_async_remote_copy(src_ref=x_refs[a].at[:, 1 - c], dst_ref=o_refs[a],
                                               send_sem=ssem.at[a], recv_sem=rsem.at[a],
                                               device_id=(x, y, 1 - c), device_id_type=MESH) for a in range(n)]
        for cp in copies:
            cp.start()
        for cp in copies:
            cp.wait()

    return pl.pallas_call(
        body, name=name, in_specs=[ANY] * n, out_specs=[ANY] * n,
        out_shape=[jax.ShapeDtypeStruct((a.shape[0],) + a.shape[2:], a.dtype) for a in xs],
        scratch_shapes=[pltpu.SemaphoreType.DMA((n,)), pltpu.SemaphoreType.DMA((n,))],
    )(*xs)


def _chip_scatter(ps):
    n = len(ps)

    def body(*refs):
        p_refs, o_refs, ssem, rsem = refs[:n], refs[n:2 * n], refs[2 * n], refs[2 * n + 1]
        x, y, c, chips = _place()
        me = _chip_index(x, y)
        sends = []
        for a in range(n):
            for j, chip in enumerate(chips):
                sends.append(pltpu.make_async_remote_copy(
                    src_ref=p_refs[a].at[_chip_index(*chip)], dst_ref=o_refs[a].at[me],
                    send_sem=ssem.at[3 * a + j], recv_sem=rsem.at[3 * a + j],
                    device_id=(*chip, c), device_id_type=MESH))
        for cp in sends:
            cp.start()
        for a in range(n):
            for j, chip in enumerate(chips):
                src = _chip_index(*chip)
                pltpu.make_async_remote_copy(
                    src_ref=p_refs[a].at[src], dst_ref=o_refs[a].at[src],
                    send_sem=ssem.at[3 * a + j], recv_sem=rsem.at[3 * a + j],
                    device_id=(*chip, c), device_id_type=MESH).wait_recv()
        for cp in sends:
            cp.wait_send()

    return pl.pallas_call(
        body, name="chip_scatter", in_specs=[ANY] * n, out_specs=[ANY] * n,
        out_shape=[jax.ShapeDtypeStruct(a.shape, a.dtype) for a in ps],
        scratch_shapes=[pltpu.SemaphoreType.DMA((3 * n,)), pltpu.SemaphoreType.DMA((3 * n,))],
    )(*ps)


def _final_gather(fs, rep):
    n = len(fs)

    def body(*refs):
        o_refs, repo_ref = refs[n + 1:2 * n + 1], refs[2 * n + 1]
        ssem, rsem = refs[2 * n + 2:]
        x, y, c, chips = _place()
        slot = 4 * x + 2 * y + c
        copies = [pltpu.make_async_remote_copy(src_ref=o_refs[a].at[c], dst_ref=o_refs[a].at[c],
                                               send_sem=ssem.at[a], recv_sem=rsem.at[a],
                                               device_id=(x, y, 1 - c), device_id_type=MESH) for a in range(n)]
        peers = [(x, y, 1 - c)] + [(*chip, c) for chip in chips] + [(*chip, 1 - c) for chip in chips]
        for k, peer in enumerate(peers):
            copies.append(pltpu.make_async_remote_copy(src_ref=repo_ref.at[slot], dst_ref=repo_ref.at[slot],
                                                       send_sem=ssem.at[n + k], recv_sem=rsem.at[n + k],
                                                       device_id=peer, device_id_type=MESH))
        for cp in copies:
            cp.start()
        for a in range(n):
            pltpu.make_async_remote_copy(src_ref=o_refs[a].at[1 - c], dst_ref=o_refs[a].at[1 - c],
                                         send_sem=ssem.at[a], recv_sem=rsem.at[a],
                                         device_id=(x, y, 1 - c), device_id_type=MESH).wait_recv()
        for k, peer in enumerate(peers):
            px, py, pc = peer
            theirs = repo_ref.at[4 * px + 2 * py + pc]
            pltpu.make_async_remote_copy(src_ref=theirs, dst_ref=theirs, send_sem=ssem.at[n + k], recv_sem=rsem.at[n + k],
                                         device_id=peer, device_id_type=MESH).wait_recv()
        for cp in copies:
            cp.wait_send()

    return pl.pallas_call(
        body, name="final_gather", in_specs=[ANY] * (n + 1), out_specs=[ANY] * (n + 1),
        out_shape=[jax.ShapeDtypeStruct(a.shape, a.dtype) for a in fs] + [jax.ShapeDtypeStruct(rep.shape, rep.dtype)],
        input_output_aliases={k: k for k in range(n + 1)},
        scratch_shapes=[pltpu.SemaphoreType.DMA((n + 7,)), pltpu.SemaphoreType.DMA((n + 7,))],
    )(*fs, rep)


def _block_diag(w, gb):
    nh, hd, _ = w.shape
    per = gb // hd
    w4 = w.reshape(nh // per, per, hd, hd)
    eye = jnp.eye(per, dtype=w.dtype)
    return jnp.einsum("jaik,ab->jaibk", w4, eye).reshape(nh // per, gb, gb)


def _diag_blocks(dense, hd):
    nj, gb, _ = dense.shape
    per = gb // hd
    d5 = dense.reshape(nj, per, hd, per, hd)
    return jnp.stack([d5[:, a, :, a, :] for a in range(per)], axis=1).reshape(nj * per, hd, hd)


def _round_up(n, q):
    return (n + q - 1) // q * q


def kernel(x, meta, norm_g, w_in, conv_a_w, conv_a_b, lru_wr, lru_br, lru_wi, lru_bi, lru_lambda, conv_b_w, w_out, final_g, loss_target, m_meta, m_norm_g, m_w_in, m_conv_a_w, m_conv_a_b, m_lru_wr, m_lru_br, m_lru_wi, m_lru_bi, m_lru_lambda, m_conv_b_w, m_w_out, m_final_g, v_meta, v_norm_g, v_w_in, v_conv_a_w, v_conv_a_b, v_lru_wr, v_lru_br, v_lru_wi, v_lru_bi, v_lru_lambda, v_conv_b_w, v_w_out, v_final_g):
    weights = dict(meta=meta, norm_g=norm_g, w_in=w_in, conv_a_w=conv_a_w, conv_a_b=conv_a_b, lru_wr=lru_wr,
                   lru_br=lru_br, lru_wi=lru_wi, lru_bi=lru_bi, lru_lambda=lru_lambda, conv_b_w=conv_b_w,
                   w_out=w_out, final_g=final_g)
    mom1 = dict(meta=m_meta, norm_g=m_norm_g, w_in=m_w_in, conv_a_w=m_conv_a_w, conv_a_b=m_conv_a_b,
                lru_wr=m_lru_wr, lru_br=m_lru_br, lru_wi=m_lru_wi, lru_bi=m_lru_bi, lru_lambda=m_lru_lambda,
                conv_b_w=m_conv_b_w, w_out=m_w_out, final_g=m_final_g)
    mom2 = dict(meta=v_meta, norm_g=v_norm_g, w_in=v_w_in, conv_a_w=v_conv_a_w, conv_a_b=v_conv_a_b,
                lru_wr=v_lru_wr, lru_br=v_lru_br, lru_wi=v_lru_wi, lru_bi=v_lru_bi, lru_lambda=v_lru_lambda,
                conv_b_w=v_conv_b_w, w_out=v_w_out, final_g=v_final_g)
    names = list(weights)

    assert x.shape[0] == 1
    seq, d = x.shape[1], x.shape[2]
    n_meta, ds = meta.shape
    depth = norm_g.shape[0]
    c = lru_lambda.shape[1]
    nh, hd = lru_wr.shape[1], lru_wr.shape[2]
    ns = w_in.shape[2]
    dms = w_out.shape[1]
    cs = conv_a_w.shape[2]
    ka, kb = conv_a_w.shape[1], conv_b_w.shape[1]
    s = N_CHIPS
    assert depth == N_CORES and d == s * ds and c == s * cs and s * ns == 6 * c and s * dms == 2 * c
    gb = min(GATE_BLOCK, c)
    t_real = n_meta + seq
    t = _round_up(t_real, ROW_QUANTUM)
    my_c = lax.axis_index("c").astype(jnp.int32)
    my_chip = (2 * lax.axis_index("x") + lax.axis_index("y")).astype(jnp.int32)
    c_idx = my_c.reshape(1)
    chip_idx = my_chip.reshape(1)

    sm_rows = _round_up(n_meta + depth * SUBLANES, 2 * SUBLANES)
    small = jnp.zeros((sm_rows, ds), F32)
    small = small.at[0:n_meta, :].set(meta)
    for l in range(depth):
        base = n_meta + l * SUBLANES
        small = small.at[base:base + ka, 0:cs].set(conv_a_w[l])
        small = small.at[base + ka:base + ka + kb, 0:cs].set(conv_b_w[l])
    (small_g,) = _gather_first([], small)
    meta_full = jnp.transpose(small_g[:, 0:n_meta, :], (1, 0, 2)).reshape(n_meta, d)
    wa_full, wb_full = [], []
    for l in range(depth):
        base = n_meta + l * SUBLANES
        wa_full.append(jnp.transpose(small_g[:, base:base + ka, 0:cs], (1, 0, 2)).reshape(ka, c))
        wb_full.append(jnp.transpose(small_g[:, base + ka:base + ka + kb, 0:cs], (1, 0, 2)).reshape(kb, c))
    win0 = _cast_place(w_in, 0, chip_idx, "cast_w_in_0").reshape(s, 2, d // 2, ns)
    ssem_w, rsem_w, win0, token_w = _copies_start([win0], _gather_plan(0), 3, "gather_win0_ici_start")
    win_b = [None] + [_cast_place(w_in, l, chip_idx, f"cast_w_in_{l}", after=token_w).reshape(s, 2, d // 2, ns)
                      for l in range(1, depth)]
    wout_b = [_cast_place(w_out, l, chip_idx, f"cast_w_out_{l}", after=token_w).reshape(s, 2, dms // 2, d)
              for l in range(depth)]
    h = jnp.concatenate([meta_full, x[0], jnp.zeros((t - t_real, d), F32)], axis=0) + token_w[0, 0]
    tgt = jnp.concatenate([jnp.zeros((n_meta, d), F32), loss_target[0], jnp.zeros((t - t_real, d), F32)],
                          axis=0) + token_w[0, 0]
    (win0,) = _copies_wait([win0], ssem_w, rsem_w, [h, tgt] + win_b[1:] + wout_b, _gather_plan(0),
                           "gather_win0_ici_wait")
    ssem_w, rsem_w, win0, token_w = _copies_start([win0], _gather_plan(1), 3, "gather_win0_d2d_start")
    ssem_o, rsem_o, wout0, token_o = _copies_start([wout_b[0]], _gather_plan(0), 3, "gather_wout0_ici_start",
                                                   after=token_w)
    later = [win_b[1], wout_b[1]]
    ssem, rsem, *later, token = _copies_start(later, _gather_plan(0), 3 * len(later), "gather_next_ici_start",
                                              after=token_o)
    (win_b[0],) = _copies_wait([win0], ssem_w, rsem_w, token, _gather_plan(1), "gather_win0_d2d_wait")

    layer_w = []
    for l in range(depth):
        layer_w.append(dict(
            g=norm_g[l].reshape(1, d), wa=wa_full[l], ba=conv_a_b[l].reshape(1, c),
            wr=_block_diag(lru_wr[l], gb).astype(BF16), br=lru_br[l].reshape(1, c),
            wi=_block_diag(lru_wi[l], gb).astype(BF16), bi=lru_bi[l].reshape(1, c),
            lam=lru_lambda[l].reshape(1, c), wb=wb_full[l]))
    saved = []
    for l, lw in enumerate(layer_w):
        first = l == 0
        lw["win"] = win_b[l].reshape(s, d, ns)
        u, hn = _norm_in(h, lw["g"] + token[0, 0] if first else lw["g"], lw["win"], f"norm_in_{l}")
        if first:
            (wout0,) = _copies_wait([wout0], ssem_o, rsem_o, u, _gather_plan(0), "gather_wout0_ici_wait")
            ssem_o, rsem_o, wout0, token_o = _copies_start([wout0], _gather_plan(1), 3, "gather_wout0_d2d_start")
        y, hs = _mix_fwd(u, lw["wa"], lw["ba"] + token_o[0, 0] if first else lw["ba"], lw["wr"], lw["br"], lw["wi"],
                         lw["bi"], lw["lam"], lw["wb"], f"mix_fwd_{l}")
        token = None
        if first:
            (wout_b[0],) = _copies_wait([wout0], ssem_o, rsem_o, y, _gather_plan(1), "gather_wout0_d2d_wait")
            later = _copies_wait(later, ssem, rsem, y, _gather_plan(0), "gather_next_ici_wait")
            ssem, rsem, *later, token = _copies_start(later, _gather_plan(1), 3 * len(later), "gather_next_d2d_start")
        lw["wout"] = wout_b[l].reshape(2 * c, d)
        saved.append((h, u, hn, y, hs))
        h = _out_proj(h, y, lw["wout"], f"out_proj_{l}", after=token)
        if first:
            win_b[1], wout_b[1] = _copies_wait(later, ssem, rsem, h, _gather_plan(1), "gather_next_d2d_wait")
    dh, loss_lanes, d_final_g = _loss_head(h, tgt, final_g.reshape(1, d), n_meta, t_real, "loss_head")
    loss = lax.psum(loss_lanes[0, 0], ("x", "y", "c"))

    to_core = jnp.stack([my_chip, my_c])
    grads = [None] * depth
    early = None
    for l in reversed(range(depth)):
        lw = layer_w[l]
        h_in, u, hn, y, hs = saved[l]
        token = early[-1] if early else None
        dy = _out_proj_dy(dh, lw["wout"], f"out_proj_dy_{l}", after=token)
        d_wout = _out_proj_dw(y, dh, f"out_proj_dw_{l}")
        if early:
            ssem, rsem, bufs, _ = early
            bufs = _copies_wait(bufs, ssem, rsem, d_wout, _swap_plan, "early_swap_wait")
            half = len(bufs) // 2
            sums = [_pair_add(a, b, c_idx, f"early_pair_add_{k}") for k, (a, b) in enumerate(zip(bufs[:half], bufs[half:]))]
            lands = [lax.empty(p.shape, p.dtype) for p in sums]
            ssem, rsem, *bufs, token = _copies_start(sums + lands, _scatter_plan, 3 * half, "early_scatter_start")
        du, dsm, d_wr, d_wi = _mix_bwd(u, hs, dy, lw["wa"], lw["ba"], lw["wr"], lw["br"], lw["wi"], lw["bi"],
                                       lw["lam"], lw["wb"], f"mix_bwd_{l}", after=token)
        if early:
            bufs = _copies_wait(bufs, ssem, rsem, du, _scatter_plan, "early_scatter_wait")
            halves = [_chip_sum(rc, p, to_core, N_CORES, f"early_chip_sum_{k}")
                      for k, (p, rc) in enumerate(zip(bufs[:half], bufs[half:]))]
            ssem, rsem, *bufs, token = _copies_start(halves, _pair_gather_plan, half, "early_gather_start")
        d_win = _in_proj_dw(hn, du, s, f"in_proj_dw_{l}", after=token)
        srcs = [d_win.reshape(s, 2, d // 2, ns), d_wout.reshape(s, 2, dms // 2, d)]
        if early:
            early_full = _copies_wait(bufs, ssem, rsem, d_win, _pair_gather_plan, "early_gather_wait")
            from_sibling = _pair_swap(srcs, "pair_swap")
            late_sums = [_pair_add(a, b, c_idx, f"pair_add_{k}") for k, (a, b) in enumerate(zip(srcs, from_sibling))]
            lands = [lax.empty(p.shape, p.dtype) for p in late_sums]
            ssem, rsem, *bufs, token = _copies_start(late_sums + lands, _scatter_plan, 3 * len(srcs), "late_scatter_start")
        dh, d_g = _in_proj_bwd(du, lw["win"], h_in, lw["g"], dh, f"in_proj_bwd_{l}", after=token)
        if early:
            bufs = _copies_wait(bufs, ssem, rsem, dh, _scatter_plan, "late_scatter_wait")
            late_reduced = [_chip_sum(rc, p, to_core, N_CORES, f"chip_sum_{k}")
                            for k, (p, rc) in enumerate(zip(bufs[:len(srcs)], bufs[len(srcs):]))]
        grads[l] = dict(dsm=dsm, wr=_diag_blocks(d_wr, hd), wi=_diag_blocks(d_wi, hd), g=d_g)
        if l == depth - 1:
            lands = [lax.empty((a.shape[0],) + a.shape[2:], a.dtype) for a in srcs]
            ssem, rsem, *bufs, token = _copies_start(srcs + lands, _swap_plan, len(srcs), "early_swap_start")
            early = (ssem, rsem, bufs, token)
        else:
            early = None
    grad_x = dh[n_meta:t_real][None]

    sharded = []
    sp = jnp.zeros((sm_rows, s, ds), F32)
    sp = sp.at[0:n_meta].set(dh[0:n_meta].reshape(n_meta, s, ds))
    for l in range(depth):
        base = n_meta + l * SUBLANES
        dsm = grads[l]["dsm"]
        sp = sp.at[base:base + ka, :, 0:cs].set(dsm[ROW_DWA:ROW_DWA + ka].reshape(ka, s, cs))
        sp = sp.at[base + ka:base + ka + kb, :, 0:cs].set(dsm[ROW_DWB:ROW_DWB + kb].reshape(kb, s, cs))
    sharded.append(jnp.transpose(sp, (1, 0, 2)).reshape(s, 2, sm_rows // 2, ds))
    rep_parts = [jnp.concatenate([grads[l]["g"].reshape(-1) for l in range(depth)]), d_final_g.reshape(-1)]
    for row in (ROW_DBA, ROW_DBR, ROW_DBI, ROW_DLAM):
        rep_parts.append(jnp.concatenate([grads[l]["dsm"][row] for l in range(depth)]))
    rep_parts.append(jnp.concatenate([grads[l]["wr"].reshape(-1) for l in range(depth)]))
    rep_parts.append(jnp.concatenate([grads[l]["wi"].reshape(-1) for l in range(depth)]))
    rep_sizes = [p.shape[0] for p in rep_parts]
    piece = _round_up(-(-sum(rep_sizes) // (s * 2)), 2 * SUBLANES * LANES)
    flat = jnp.concatenate(rep_parts + [jnp.zeros((s * 2 * piece - sum(rep_sizes),), F32)])
    sharded.append(flat.reshape(s, 2, piece // LANES, LANES))

    from_sibling = _pair_swap(sharded, "small_pair_swap")
    pair_sums = [_pair_add(a, b, c_idx, f"small_pair_add_{k}") for k, (a, b) in enumerate(zip(sharded, from_sibling))]
    by_chip = _chip_scatter(pair_sums)
    to_device = jnp.stack([my_chip, 2 * my_chip + my_c])
    reduced_sp = _chip_sum(by_chip[0], pair_sums[0], to_core, N_CORES, "small_chip_sum")
    reduced_rep = _chip_sum(by_chip[1], pair_sums[1], to_device, N_CHIPS * N_CORES, "chip_sum_rep")
    *full, rep_all = _final_gather(late_reduced + [reduced_sp], reduced_rep)

    g_win = [full[0].reshape(d, ns), early_full[0].reshape(d, ns)]
    g_wout = [full[1].reshape(dms, d), early_full[1].reshape(dms, d)]
    g_sp = full[2].reshape(sm_rows, ds)
    rep_flat = rep_all.reshape(-1)
    rep_out, off = [], 0
    for n in rep_sizes:
        rep_out.append(rep_flat[off:off + n])
        off += n
    grad = dict(
        meta=g_sp[0:n_meta],
        norm_g=rep_out[0].reshape(depth, d),
        w_in=jnp.stack(g_win),
        conv_a_w=jnp.stack([g_sp[n_meta + l * SUBLANES:n_meta + l * SUBLANES + ka, 0:cs] for l in range(depth)]),
        conv_a_b=rep_out[2].reshape(depth, c),
        lru_wr=rep_out[6].reshape(depth, nh, hd, hd),
        lru_br=rep_out[3].reshape(depth, c),
        lru_wi=rep_out[7].reshape(depth, nh, hd, hd),
        lru_bi=rep_out[4].reshape(depth, c),
        lru_lambda=rep_out[5].reshape(depth, c),
        conv_b_w=jnp.stack([g_sp[n_meta + l * SUBLANES + ka:n_meta + l * SUBLANES + ka + kb, 0:cs]
                            for l in range(depth)]),
        w_out=jnp.stack(g_wout),
        final_g=rep_out[1].reshape(d),
    )

    delta, new_m, new_v = {}, {}, {}
    for n in names:
        shape = weights[n].shape
        two_d = (-1, shape[-1]) if len(shape) > 1 else (1, -1)
        if n in ("lru_wr", "lru_wi"):
            two_d = (-1, LANES)
        out = _adamw(weights[n].reshape(two_d), grad[n].reshape(two_d), mom1[n].reshape(two_d),
                     mom2[n].reshape(two_d), f"adamw_{n}")
        delta[n], new_m[n], new_v[n] = (o.reshape(shape) for o in out)

    return (loss, grad_x, *[grad[n] for n in names], *[delta[n] for n in names],
            *[new_m[n] for n in names], *[new_v[n] for n in names])
```

```python
import functools

import jax
import jax.numpy as jnp
from jax import lax
from jax.experimental import pallas as pl
from jax.experimental.pallas import tpu as pltpu

F32 = jnp.float32
BF16 = jnp.bfloat16

RMS_EPS = 1e-6
LRU_C = 8.0
ADAM_LR = 0.001
ADAM_B1 = 0.9
ADAM_B2 = 0.999
ADAM_EPS = 1e-08
ADAM_WD = 0.01
ADAM_STEP = 10

N_CHIPS = 4
N_CORES = 2
VMEM_LIMIT_BYTES = 56 * 1024 * 1024
SUBLANES = 8
LANES = 128
ROW_QUANTUM = 384
MIX_CHUNK = 192
GATE_BLOCK = 256
MESH = pl.DeviceIdType.MESH
ANY = pl.BlockSpec(memory_space=pl.ANY)

NT_DIMS = (((1,), (1,)), ((), ()))
TN_DIMS = (((0,), (0,)), ((), ()))


def _params(sem):
    return pltpu.CompilerParams(dimension_semantics=sem, vmem_limit_bytes=VMEM_LIMIT_BYTES)


def _sig(x):
    return 0.5 * jnp.tanh(0.5 * x) + 0.5


def _row_tile(t):
    return 704 if t % 704 == 0 else 192


def _col_tile(n, prefs):
    for p in prefs:
        if n % p == 0:
            return p
    return n


def _slab_rows(rows, cols):
    if rows * cols * 4 <= 1024 * 1024:
        return rows
    return _col_tile(rows, (256, 128, 64, 32, 16))


def _norm_in(h, g, wg, name):
    t, d = h.shape
    s, _, ns = wg.shape
    tm = _row_tile(t)
    tn = _col_tile(ns, (1536, 512, 384, 128))
    nb = ns // tn

    def body(h_ref, g_ref, w_ref, u_ref, hn_ref):
        @pl.when(pl.program_id(1) == 0)
        def _():
            x = h_ref[...]
            r = lax.rsqrt(jnp.mean(x * x, axis=-1, keepdims=True) + RMS_EPS)
            hn_ref[...] = ((x * r) * g_ref[...]).astype(BF16)

        u_ref[...] = jnp.dot(hn_ref[...], w_ref[...], preferred_element_type=F32)

    return pl.pallas_call(
        body, name=name, grid=(t // tm, s * nb),
        in_specs=[pl.BlockSpec((tm, d), lambda i, n: (i, 0)),
                  pl.BlockSpec((1, d), lambda i, n: (0, 0)),
                  pl.BlockSpec((None, d, tn), lambda i, n: (n // nb, 0, n % nb))],
        out_specs=[pl.BlockSpec((tm, tn), lambda i, n: (i, n)),
                   pl.BlockSpec((tm, d), lambda i, n: (i, 0))],
        out_shape=[jax.ShapeDtypeStruct((t, s * ns), F32), jax.ShapeDtypeStruct((t, d), BF16)],
        compiler_params=_params(("arbitrary", "arbitrary")),
    )(h, g, wg)


def _decay_consts(lam):
    z = -lam
    e = jnp.exp(-jnp.abs(z))
    u = 1.0 + e
    log1p_e = jnp.where(u == 1.0, e, jnp.log(u) * (e / (u - 1.0)))
    sp = jnp.maximum(z, 0.0) + log1p_e
    return -LRU_C * sp, LRU_C * _sig(z)


def _gates(xc, wr_ref, br_ref, wi_ref, bi_ref, c8, j, gb):
    sl = slice(j * gb, (j + 1) * gb)
    x16 = xc.astype(BF16)
    r = _sig(jnp.dot(x16, wr_ref[j], preferred_element_type=F32) + br_ref[:, sl])
    ig = _sig(jnp.dot(x16, wi_ref[j], preferred_element_type=F32) + bi_ref[:, sl])
    la = c8[:, sl] * r
    a = jnp.exp(la)
    sq = jnp.sqrt(-jnp.tanh(la) * (a * a + 1.0))
    return r, ig, a, sq


def _mix_fwd(u, wa, ba, wr, br, wi, bi, lam, wb, name):
    t = u.shape[0]
    c = u.shape[1] // 6
    tc = MIX_CHUNK
    gb = wr.shape[1]
    nblk = c // gb
    ka, kb = wa.shape[0], wb.shape[0]

    def body(u_ref, wa_ref, ba_ref, wr_ref, br_ref, wi_ref, bi_ref, lam_ref, wb_ref,
             y_ref, hs_ref, xa_ext, v_ext, xc_s, a_s, b_s, carry_s):
        @pl.when(pl.program_id(0) == 0)
        def _():
            xa_ext[0:SUBLANES, :] = jnp.zeros((SUBLANES, c), F32)
            v_ext[0:SUBLANES, :] = jnp.zeros((SUBLANES, c), F32)
            carry_s[...] = jnp.zeros_like(carry_s)

        xa_ext[SUBLANES:SUBLANES + tc, :] = u_ref[:, 0:c]
        xc = ba_ref[...]
        for k in range(ka):
            xc = xc + wa_ref[pl.ds(k, 1), :] * xa_ext[pl.ds(SUBLANES - (ka - 1) + k, tc), :]
        xc_s[...] = xc
        c8, _ = _decay_consts(lam_ref[...])
        for j in range(nblk):
            sl = slice(j * gb, (j + 1) * gb)
            xcj = xc_s[:, sl]
            _, ig, a, sq = _gates(xcj, wr_ref, br_ref, wi_ref, bi_ref, c8, j, gb)
            a_s[:, sl] = a
            b_s[:, sl] = sq * (ig * xcj)

        row = lax.broadcasted_iota(jnp.int32, (SUBLANES, c), 0)

        def scan_step(j, _):
            off = pl.multiple_of(j * SUBLANES, SUBLANES)
            av = a_s[pl.ds(off, SUBLANES), :]
            bv = b_s[pl.ds(off, SUBLANES), :]
            for d in (1, 2, 4):
                keep = row >= d
                bsh = jnp.where(keep, pltpu.roll(bv, d, axis=0), 0.0)
                ash = jnp.where(keep, pltpu.roll(av, d, axis=0), 1.0)
                bv = av * bsh + bv
                av = av * ash
            hv = av * carry_s[...] + bv
            hs_ref[pl.ds(off, SUBLANES), :] = hv
            carry_s[...] = hs_ref[pl.ds(off + SUBLANES - 1, 1), :]
            return 0

        lax.fori_loop(0, tc // SUBLANES, scan_step, 0)

        ga = u_ref[:, c:2 * c]
        y_ref[:, 0:c] = (hs_ref[...] * (ga * _sig(ga))).astype(BF16)

        v_ext[SUBLANES:SUBLANES + tc, :] = u_ref[:, 3 * c:4 * c] * u_ref[:, 4 * c:5 * c]
        cv = wb_ref[pl.ds(0, 1), :] * v_ext[pl.ds(SUBLANES - (kb - 1), tc), :]
        for k in range(1, kb):
            cv = cv + wb_ref[pl.ds(k, 1), :] * v_ext[pl.ds(SUBLANES - (kb - 1) + k, tc), :]
        gbv = u_ref[:, 5 * c:6 * c]
        y_ref[:, c:2 * c] = (u_ref[:, 2 * c:3 * c] * cv * (gbv * _sig(gbv))).astype(BF16)

        xa_ext[0:SUBLANES, :] = xa_ext[tc:tc + SUBLANES, :]
        v_ext[0:SUBLANES, :] = v_ext[tc:tc + SUBLANES, :]

    full = lambda shape: pl.BlockSpec(shape, lambda i: (0,) * len(shape))
    return pl.pallas_call(
        body, name=name, grid=(t // tc,),
        in_specs=[pl.BlockSpec((tc, 6 * c), lambda i: (i, 0)),
                  full(wa.shape), full(ba.shape), full(wr.shape), full(br.shape),
                  full(wi.shape), full(bi.shape), full(lam.shape), full(wb.shape)],
        out_specs=[pl.BlockSpec((tc, 2 * c), lambda i: (i, 0)),
                   pl.BlockSpec((tc, c), lambda i: (i, 0))],
        out_shape=[jax.ShapeDtypeStruct((t, 2 * c), BF16), jax.ShapeDtypeStruct((t, c), F32)],
        scratch_shapes=[pltpu.VMEM((tc + SUBLANES, c), F32), pltpu.VMEM((tc + SUBLANES, c), F32),
                        pltpu.VMEM((tc, c), F32), pltpu.VMEM((tc, c), F32), pltpu.VMEM((tc, c), F32),
                        pltpu.VMEM((1, c), F32)],
        compiler_params=_params(("arbitrary",)),
    )(u, wa, ba, wr, br, wi, bi, lam, wb)


ROW_DWA = 0
ROW_DBA = 4
ROW_DBR = 5
ROW_DBI = 6
ROW_DLAM = 7
ROW_DWB = 8
SMALL_ROWS = 16


def _mix_bwd(u, hs, dy, wa, ba, wr, br, wi, bi, lam, wb, name, after=None):
    t = u.shape[0]
    c = u.shape[1] // 6
    tc = MIX_CHUNK
    nt = t // tc
    gb = wr.shape[1]
    nblk = c // gb
    ka, kb = wa.shape[0], wb.shape[0]
    assert ka <= ROW_DBA and kb <= SMALL_ROWS - ROW_DWB
    hb = tc // SUBLANES

    def body(u_ref, uh_ref, hs_ref, hsh_ref, dy_ref, wa_ref, ba_ref, wr_ref, br_ref, wi_ref, bi_ref, lam_ref, wb_ref,
             du_ref, dsm_ref, dwr_ref, dwi_ref,
             xa_ext, v_ext, hs_ext, a_ext, ds_ext, dxc_ext, dcv_ext, xc_s, r_s, i_s, sq_s, g_s, an_s):
        i = pl.program_id(0)
        chunk = nt - 1 - i
        tail = slice(tc, tc + SUBLANES)
        head = slice(0, SUBLANES)

        @pl.when(i == 0)
        def _():
            zero = jnp.zeros((SUBLANES, c), F32)
            a_ext[tail, :] = zero
            ds_ext[tail, :] = zero
            dxc_ext[tail, :] = zero
            dcv_ext[tail, :] = zero
            dsm_ref[...] = jnp.zeros_like(dsm_ref)
            dwr_ref[...] = jnp.zeros_like(dwr_ref)
            dwi_ref[...] = jnp.zeros_like(dwi_ref)

        prev = jnp.where(chunk > 0, 1.0, 0.0)
        xa_ext[head, :] = uh_ref[:, 0:c] * prev
        xa_ext[SUBLANES:SUBLANES + tc, :] = u_ref[:, 0:c]
        v_ext[head, :] = uh_ref[:, 3 * c:4 * c] * uh_ref[:, 4 * c:5 * c] * prev
        v_ext[SUBLANES:SUBLANES + tc, :] = u_ref[:, 3 * c:4 * c] * u_ref[:, 4 * c:5 * c]
        hs_ext[head, :] = hsh_ref[...] * prev
        hs_ext[SUBLANES:SUBLANES + tc, :] = hs_ref[...]

        xc = ba_ref[...]
        for k in range(ka):
            xc = xc + wa_ref[pl.ds(k, 1), :] * xa_ext[pl.ds(SUBLANES - (ka - 1) + k, tc), :]
        xc_s[...] = xc
        c8, dc8 = _decay_consts(lam_ref[...])
        for j in range(nblk):
            sl = slice(j * gb, (j + 1) * gb)
            r, ig, a, sq = _gates(xc_s[:, sl], wr_ref, br_ref, wi_ref, bi_ref, c8, j, gb)
            r_s[:, sl] = r
            i_s[:, sl] = ig
            sq_s[:, sl] = sq
            a_ext[0:tc, sl] = a

        ga = u_ref[:, c:2 * c]
        sga = _sig(ga)
        g_s[...] = dy_ref[:, 0:c] * (ga * sga)
        an_s[...] = a_ext[pl.ds(1, tc), :]

        row = lax.broadcasted_iota(jnp.int32, (SUBLANES, c), 0)

        def scan_step(j, _):
            off = pl.multiple_of(tc - SUBLANES - j * SUBLANES, SUBLANES)
            av = an_s[pl.ds(off, SUBLANES), :]
            bv = g_s[pl.ds(off, SUBLANES), :]
            for d in (1, 2, 4):
                keep = row < SUBLANES - d
                bsh = jnp.where(keep, pltpu.roll(bv, SUBLANES - d, axis=0), 0.0)
                ash = jnp.where(keep, pltpu.roll(av, SUBLANES - d, axis=0), 1.0)
                bv = av * bsh + bv
                av = av * ash
            ds_ext[pl.ds(off, SUBLANES), :] = av * ds_ext[pl.ds(off + SUBLANES, 1), :] + bv
            return 0

        lax.fori_loop(0, tc // SUBLANES, scan_step, 0)

        def acc(row_index, val):
            dsm_ref[pl.ds(row_index, 1), :] += jnp.sum(val, axis=0, keepdims=True)

        def acc_block(row_index, sl, val):
            dsm_ref[pl.ds(row_index, 1), sl] += jnp.sum(val, axis=0, keepdims=True)

        for j in range(nblk):
            sl = slice(j * gb, (j + 1) * gb)
            ds = ds_ext[0:tc, sl]
            hprev = hs_ext[pl.ds(SUBLANES - 1, tc), sl]
            a = a_ext[0:tc, sl]
            sq = sq_s[:, sl]
            ig = i_s[:, sl]
            r = r_s[:, sl]
            xcj = xc_s[:, sl]
            t1 = ds * xcj
            dla = (ds * hprev) * a - (t1 * ig) * ((a * a) / sq)
            acc_block(ROW_DLAM, sl, dla * r)
            dpr = (dla * c8[:, sl]) * (r * (1.0 - r))
            dpi = (t1 * sq) * (ig * (1.0 - ig))
            acc_block(ROW_DBR, sl, dpr)
            acc_block(ROW_DBI, sl, dpi)
            p16 = dpr.astype(BF16)
            q16 = dpi.astype(BF16)
            x16 = xcj.astype(BF16)
            dwr_ref[j] += lax.dot_general(x16, p16, TN_DIMS, preferred_element_type=F32)
            dwi_ref[j] += lax.dot_general(x16, q16, TN_DIMS, preferred_element_type=F32)
            dxc = (ds * (sq * ig)
                   + lax.dot_general(p16, wr_ref[j], NT_DIMS, preferred_element_type=F32)
                   + lax.dot_general(q16, wi_ref[j], NT_DIMS, preferred_element_type=F32))
            dxc_ext[0:tc, sl] = dxc
            acc_block(ROW_DBA, sl, dxc)

        dsilu_a = sga * (1.0 + ga * (1.0 - sga))
        du_ref[:, c:2 * c] = (dy_ref[:, 0:c] * hs_ref[...] * dsilu_a).astype(BF16)

        dxc = dxc_ext[0:tc, :]
        dxa = wa_ref[pl.ds(ka - 1, 1), :] * dxc
        acc(ROW_DWA + ka - 1, dxc * xa_ext[SUBLANES:SUBLANES + tc, :])
        for k in range(ka - 1):
            acc(ROW_DWA + k, dxc * xa_ext[pl.ds(SUBLANES - (ka - 1) + k, tc), :])
            dxa = dxa + wa_ref[pl.ds(k, 1), :] * dxc_ext[pl.ds(ka - 1 - k, tc), :]
        du_ref[:, 0:c] = dxa.astype(BF16)

        cv = wb_ref[pl.ds(0, 1), :] * v_ext[pl.ds(SUBLANES - (kb - 1), tc), :]
        for k in range(1, kb):
            cv = cv + wb_ref[pl.ds(k, 1), :] * v_ext[pl.ds(SUBLANES - (kb - 1) + k, tc), :]
        gbv = u_ref[:, 5 * c:6 * c]
        sgb = _sig(gbv)
        silu_b = gbv * sgb
        dyb = dy_ref[:, c:2 * c]
        gB = u_ref[:, 2 * c:3 * c]
        du_ref[:, 2 * c:3 * c] = (dyb * cv * silu_b).astype(BF16)
        du_ref[:, 5 * c:6 * c] = (dyb * gB * cv * (sgb * (1.0 + gbv * (1.0 - sgb)))).astype(BF16)
        dcv = dyb * gB * silu_b
        dcv_ext[0:tc, :] = dcv
        dv = wb_ref[pl.ds(kb - 1, 1), :] * dcv
        acc(ROW_DWB + kb - 1, dcv * v_ext[SUBLANES:SUBLANES + tc, :])
        for k in range(kb - 1):
            acc(ROW_DWB + k, dcv * v_ext[pl.ds(SUBLANES - (kb - 1) + k, tc), :])
            dv = dv + wb_ref[pl.ds(k, 1), :] * dcv_ext[pl.ds(kb - 1 - k, tc), :]
        du_ref[:, 3 * c:4 * c] = (dv * u_ref[:, 4 * c:5 * c]).astype(BF16)
        du_ref[:, 4 * c:5 * c] = (dv * u_ref[:, 3 * c:4 * c]).astype(BF16)

        a_ext[tail, :] = a_ext[head, :]
        ds_ext[tail, :] = ds_ext[head, :]
        dxc_ext[tail, :] = dxc_ext[head, :]
        dcv_ext[tail, :] = dcv_ext[head, :]

        @pl.when(i == nt - 1)
        def _():
            dsm_ref[pl.ds(ROW_DLAM, 1), :] = dsm_ref[pl.ds(ROW_DLAM, 1), :] * dc8

    full = lambda shape: pl.BlockSpec(shape, lambda i: (0,) * len(shape))
    rev = lambda i: (nt - 1 - i, 0)
    halo = lambda i: (jnp.maximum((nt - 1 - i) * hb - 1, 0), 0)
    ext = pltpu.VMEM((tc + SUBLANES, c), F32)
    blk = pltpu.VMEM((tc, c), F32)
    body, more_specs, more = _behind(body, 13, after)
    return pl.pallas_call(
        body, name=name, grid=(nt,),
        in_specs=[pl.BlockSpec((tc, 6 * c), rev), pl.BlockSpec((SUBLANES, 6 * c), halo),
                  pl.BlockSpec((tc, c), rev), pl.BlockSpec((SUBLANES, c), halo),
                  pl.BlockSpec((tc, 2 * c), rev),
                  full(wa.shape), full(ba.shape), full(wr.shape), full(br.shape),
                  full(wi.shape), full(bi.shape), full(lam.shape), full(wb.shape)] + more_specs,
        out_specs=[pl.BlockSpec((tc, 6 * c), rev), full((SMALL_ROWS, c)), full(wr.shape), full(wi.shape)],
        out_shape=[jax.ShapeDtypeStruct((t, 6 * c), BF16), jax.ShapeDtypeStruct((SMALL_ROWS, c), F32),
                   jax.ShapeDtypeStruct(wr.shape, F32), jax.ShapeDtypeStruct(wi.shape, F32)],
        scratch_shapes=[ext] * 7 + [blk] * 6,
        compiler_params=_params(("arbitrary",)),
    )(u, u, hs, hs, dy, wa, ba, wr, br, wi, bi, lam, wb, *more)


def _behind(body, n_in, after):
    if after is None:
        return body, [], []
    return (lambda *refs: body(*refs[:n_in], *refs[n_in + 1:])), [ANY], [after]


def _out_proj(h, y, w, name, after=None):
    t, d = h.shape
    dm = y.shape[1]
    tm = _row_tile(t)
    tn = _col_tile(d, (1024, 512, 256))

    def body(h_ref, y_ref, w_ref, o_ref):
        o_ref[...] = h_ref[...] + jnp.dot(y_ref[...], w_ref[...], preferred_element_type=F32)

    body, more_specs, more = _behind(body, 3, after)
    return pl.pallas_call(
        body, name=name, grid=(d // tn, t // tm),
        in_specs=[pl.BlockSpec((tm, tn), lambda n, i: (i, n)),
                  pl.BlockSpec((tm, dm), lambda n, i: (i, 0)),
                  pl.BlockSpec((dm, tn), lambda n, i: (0, n))] + more_specs,
        out_specs=pl.BlockSpec((tm, tn), lambda n, i: (i, n)),
        out_shape=jax.ShapeDtypeStruct((t, d), F32),
        compiler_params=_params(("arbitrary", "arbitrary")),
    )(h, y, w, *more)


def _out_proj_dy(dout, w, name, after=None):
    t, d = dout.shape
    dm = w.shape[0]
    tm = _row_tile(t)
    tn = _col_tile(dm, (1024, 512, 256))

    def body(g_ref, w_ref, o_ref):
        o_ref[...] = lax.dot_general(g_ref[...].astype(BF16), w_ref[...], NT_DIMS, preferred_element_type=F32)

    body, more_specs, more = _behind(body, 2, after)
    return pl.pallas_call(
        body, name=name, grid=(dm // tn, t // tm),
        in_specs=[pl.BlockSpec((tm, d), lambda n, i: (i, 0)),
                  pl.BlockSpec((tn, d), lambda n, i: (n, 0))] + more_specs,
        out_specs=pl.BlockSpec((tm, tn), lambda n, i: (i, n)),
        out_shape=jax.ShapeDtypeStruct((t, dm), F32),
        compiler_params=_params(("arbitrary", "arbitrary")),
    )(dout, w, *more)


def _out_proj_dw(y, dout, name):
    t, dm = y.shape
    d = dout.shape[1]
    tmm = _col_tile(dm, (512, 256))
    tn = _col_tile(d, (512, 256))

    def body(y_ref, g_ref, o_ref):
        o_ref[...] = lax.dot_general(y_ref[...], g_ref[...].astype(BF16), TN_DIMS, preferred_element_type=F32)

    return pl.pallas_call(
        body, name=name, grid=(d // tn, dm // tmm),
        in_specs=[pl.BlockSpec((t, tmm), lambda n, m: (0, m)),
                  pl.BlockSpec((t, tn), lambda n, m: (0, n))],
        out_specs=pl.BlockSpec((tmm, tn), lambda n, m: (m, n)),
        out_shape=jax.ShapeDtypeStruct((dm, d), F32),
        compiler_params=_params(("arbitrary", "arbitrary")),
    )(y, dout)


def _in_proj_bwd(du, wg, h, g, dout, name, after=None):
    t, d = h.shape
    s, _, ns = wg.shape
    tm = _row_tile(t)
    tk = _col_tile(ns, (512, 384, 128))
    nb = ns // tk
    nk = s * nb

    def body(du_ref, w_ref, h_ref, g_ref, dout_ref, dh_ref, dg_ref, acc_ref):
        i, k = pl.program_id(0), pl.program_id(1)

        @pl.when(k == 0)
        def _():
            acc_ref[...] = jnp.zeros_like(acc_ref)

        @pl.when((k == 0) & (i == 0))
        def _():
            dg_ref[...] = jnp.zeros_like(dg_ref)

        acc_ref[...] += lax.dot_general(du_ref[...], w_ref[...], NT_DIMS, preferred_element_type=F32)

        @pl.when(k == nk - 1)
        def _():
            x = h_ref[...]
            dhn = acc_ref[...]
            r = lax.rsqrt(jnp.mean(x * x, axis=-1, keepdims=True) + RMS_EPS)
            gd = dhn * g_ref[...]
            dot = jnp.mean(gd * x, axis=-1, keepdims=True)
            dh_ref[...] = dout_ref[...] + (r * gd - x * ((r * r * r) * dot))
            dg_ref[...] += jnp.sum(dhn * (x * r), axis=0, keepdims=True)

    body, more_specs, more = _behind(body, 5, after)
    return pl.pallas_call(
        body, name=name, grid=(t // tm, nk),
        in_specs=[pl.BlockSpec((tm, tk), lambda i, k: (i, k)),
                  pl.BlockSpec((None, d, tk), lambda i, k: (k // nb, 0, k % nb)),
                  pl.BlockSpec((tm, d), lambda i, k: (i, 0)),
                  pl.BlockSpec((1, d), lambda i, k: (0, 0)),
                  pl.BlockSpec((tm, d), lambda i, k: (i, 0))] + more_specs,
        out_specs=[pl.BlockSpec((tm, d), lambda i, k: (i, 0)),
                   pl.BlockSpec((1, d), lambda i, k: (0, 0))],
        out_shape=[jax.ShapeDtypeStruct((t, d), F32), jax.ShapeDtypeStruct((1, d), F32)],
        scratch_shapes=[pltpu.VMEM((tm, d), F32)],
        compiler_params=_params(("arbitrary", "arbitrary")),
    )(du, wg, h, g, dout, *more)


def _in_proj_dw(hn, du, s, name, after=None):
    t, d = hn.shape
    ns = du.shape[1] // s
    tmm = _col_tile(d, (512, 256))
    tn = _col_tile(ns, (768, 384, 128))
    nb = ns // tn

    def body(hn_ref, du_ref, o_ref):
        o_ref[...] = lax.dot_general(hn_ref[...], du_ref[...], TN_DIMS, preferred_element_type=F32)

    body, more_specs, more = _behind(body, 2, after)
    return pl.pallas_call(
        body, name=name, grid=(s * nb, d // tmm),
        in_specs=[pl.BlockSpec((t, tmm), lambda n, m: (0, m)),
                  pl.BlockSpec((t, tn), lambda n, m: (0, n))] + more_specs,
        out_specs=pl.BlockSpec((None, tmm, tn), lambda n, m: (n // nb, m, n % nb)),
        out_shape=jax.ShapeDtypeStruct((s, d, ns), F32),
        compiler_params=_params(("arbitrary", "arbitrary")),
    )(hn, du, *more)


def _loss_head(h, tgt, g, n_meta, t_real, name):
    t, d = h.shape
    tm = _row_tile(t)

    def body(h_ref, t_ref, g_ref, dh_ref, loss_ref, dg_ref):
        i = pl.program_id(0)

        @pl.when(i == 0)
        def _():
            loss_ref[...] = jnp.zeros_like(loss_ref)
            dg_ref[...] = jnp.zeros_like(dg_ref)

        x = h_ref[...]
        gv = g_ref[...]
        r = lax.rsqrt(jnp.mean(x * x, axis=-1, keepdims=True) + RMS_EPS)
        xr = x * r
        rows = i * tm + lax.broadcasted_iota(jnp.int32, (tm, 1), 0)
        valid = (rows >= n_meta) & (rows < t_real)
        err = jnp.where(valid, xr * gv - t_ref[...], 0.0)
        loss_ref[...] += 0.5 * jnp.sum(jnp.mean(err * err, axis=-1, keepdims=True))
        dy = err * (1.0 / d)
        gd = dy * gv
        dot = jnp.mean(gd * x, axis=-1, keepdims=True)
        dh_ref[...] = r * gd - x * ((r * r * r) * dot)
        dg_ref[...] += jnp.sum(dy * xr, axis=0, keepdims=True)

    return pl.pallas_call(
        body, name=name, grid=(t // tm,),
        in_specs=[pl.BlockSpec((tm, d), lambda i: (i, 0)),
                  pl.BlockSpec((tm, d), lambda i: (i, 0)),
                  pl.BlockSpec((1, d), lambda i: (0, 0))],
        out_specs=[pl.BlockSpec((tm, d), lambda i: (i, 0)),
                   pl.BlockSpec((1, LANES), lambda i: (0, 0)),
                   pl.BlockSpec((1, d), lambda i: (0, 0))],
        out_shape=[jax.ShapeDtypeStruct((t, d), F32), jax.ShapeDtypeStruct((1, LANES), F32),
                   jax.ShapeDtypeStruct((1, d), F32)],
        compiler_params=_params(("arbitrary",)),
    )(h, tgt, g)


def _adamw(w, g, m, v, name):
    rows, cols = w.shape
    tr = rows
    for cand in (512, 256, 128, 64, 32, 16, 8):
        if rows % cand == 0 and cand * cols * 4 <= 2 * 1024 * 1024:
            tr = cand
            break

    def body(w_ref, g_ref, m_ref, v_ref, d_ref, nm_ref, nv_ref):
        gv = g_ref[...]
        m2 = ADAM_B1 * m_ref[...] + (1.0 - ADAM_B1) * gv
        v2 = ADAM_B2 * v_ref[...] + (1.0 - ADAM_B2) * (gv * gv)
        m_hat = m2 / (1.0 - ADAM_B1 ** ADAM_STEP)
        v_hat = v2 / (1.0 - ADAM_B2 ** ADAM_STEP)
        d_ref[...] = -ADAM_LR * (m_hat / (jnp.sqrt(v_hat) + ADAM_EPS) + ADAM_WD * w_ref[...])
        nm_ref[...] = m2
        nv_ref[...] = v2

    spec = pl.BlockSpec((tr, cols), lambda i: (i, 0))
    return pl.pallas_call(
        body, name=name, grid=(rows // tr,),
        in_specs=[spec] * 4, out_specs=[spec] * 3,
        out_shape=[jax.ShapeDtypeStruct((rows, cols), F32)] * 3,
        compiler_params=_params(("arbitrary",)),
    )(w, g, m, v)


def _pair_add(x, ra, c_idx, name):
    s, _, rows, cols = x.shape
    tr = _slab_rows(rows, cols)

    def body(c_ref, x_ref, r_ref, o_ref):
        o_ref[...] = (x_ref[...] + r_ref[...]).astype(BF16)

    return pl.pallas_call(
        body, name=name,
        grid_spec=pltpu.PrefetchScalarGridSpec(
            num_scalar_prefetch=1, grid=(s, rows // tr),
            in_specs=[pl.BlockSpec((None, None, tr, cols), lambda a, i, c_ref: (a, c_ref[0], i, 0)),
                      pl.BlockSpec((None, tr, cols), lambda a, i, c_ref: (a, i, 0))],
            out_specs=pl.BlockSpec((None, tr, cols), lambda a, i, c_ref: (a, i, 0))),
        out_shape=jax.ShapeDtypeStruct((s, rows, cols), BF16),
        compiler_params=_params(("arbitrary", "arbitrary")),
    )(c_idx, x, ra)


def _chip_sum(rc, p, where, n_slots, name):
    s, rows, cols = rc.shape
    tr = _slab_rows(rows, cols)

    def body(w_ref, x_ref, p_ref, o_ref):
        me = w_ref[0]
        total = jnp.where(me == 0, p_ref[...], x_ref[0]).astype(F32)
        for a in range(1, s):
            total = total + jnp.where(me == a, p_ref[...], x_ref[a]).astype(F32)
        o_ref[...] = total

    return pl.pallas_call(
        body, name=name,
        grid_spec=pltpu.PrefetchScalarGridSpec(
            num_scalar_prefetch=1, grid=(rows // tr,),
            in_specs=[pl.BlockSpec((s, tr, cols), lambda i, w_ref: (0, i, 0)),
                      pl.BlockSpec((None, tr, cols), lambda i, w_ref: (w_ref[0], i, 0))],
            out_specs=pl.BlockSpec((None, tr, cols), lambda i, w_ref: (w_ref[1], i, 0))),
        out_shape=jax.ShapeDtypeStruct((n_slots, rows, cols), F32),
        compiler_params=_params(("arbitrary",)),
    )(where, rc, p)


def _cast_place(w, layer, me_idx, name, after=None):
    _, rows, cols = w.shape
    tr = _slab_rows(rows, cols)

    def body(m_ref, w_ref, o_ref):
        o_ref[...] = w_ref[...].astype(BF16)

    body, more_specs, more = _behind(body, 2, after)
    return pl.pallas_call(
        body, name=name,
        grid_spec=pltpu.PrefetchScalarGridSpec(
            num_scalar_prefetch=1, grid=(rows // tr,),
            in_specs=[pl.BlockSpec((None, tr, cols), lambda i, m_ref: (layer, i, 0))] + more_specs,
            out_specs=pl.BlockSpec((None, tr, cols), lambda i, m_ref: (m_ref[0], i, 0))),
        out_shape=jax.ShapeDtypeStruct((N_CHIPS, rows, cols), BF16),
        compiler_params=_params(("arbitrary",)),
    )(me_idx, w, *more)


def _place():
    x, y, c = lax.axis_index("x"), lax.axis_index("y"), lax.axis_index("c")
    chips = [(1 - x, y), (x, 1 - y), (1 - x, 1 - y)]
    return x, y, c, chips


def _chip_index(cx, cy):
    return 2 * cx + cy


def _gather_copies(bufs, stage):
    x, y, c, chips = _place()
    me = _chip_index(x, y)
    copies = []
    for b in bufs:
        for chip in chips:
            src = _chip_index(*chip)
            if stage == 0:
                copies.append((b.at[me, c], (*chip, c), b.at[src, c]))
            else:
                copies.append((b.at[src, c], (x, y, 1 - c), b.at[src, 1 - c]))
    return copies


def _remote(ref, peer, ssem, rsem, k):
    return pltpu.make_async_remote_copy(src_ref=ref, dst_ref=ref, send_sem=ssem.at[k], recv_sem=rsem.at[k],
                                        device_id=peer, device_id_type=MESH)


def _gather_first(bufs, small):
    n = len(bufs)
    k = 3 * n

    def body(*refs):
        sm_ref = refs[n]
        b_refs, smg_ref = refs[n + 1:2 * n + 1], refs[2 * n + 1]
        lsem, ssem, rsem = refs[2 * n + 2:]
        x, y, c, chips = _place()
        me = _chip_index(x, y)
        local = pltpu.make_async_copy(sm_ref, smg_ref.at[me], lsem)
        local.start()
        first = _gather_copies(b_refs, 0)
        second = _gather_copies(b_refs, 1)
        started = []
        for i, (ref, peer, _) in enumerate(first):
            started.append(_remote(ref, peer, ssem, rsem, i))
        for j, chip in enumerate(chips):
            started.append(pltpu.make_async_remote_copy(
                src_ref=sm_ref, dst_ref=smg_ref.at[me], send_sem=ssem.at[2 * k + j], recv_sem=rsem.at[2 * k + j],
                device_id=(*chip, c), device_id_type=MESH))
        for cp in started:
            cp.start()
        for i, (_, peer, lands) in enumerate(first):
            _remote(lands, peer, ssem, rsem, i).wait_recv()
            ref, sib, _ = second[i]
            fwd = _remote(ref, sib, ssem, rsem, k + i)
            fwd.start()
            started.append(fwd)
        for i, (_, sib, lands) in enumerate(second):
            _remote(lands, sib, ssem, rsem, k + i).wait_recv()
        for j, chip in enumerate(chips):
            theirs = smg_ref.at[_chip_index(*chip)]
            pltpu.make_async_remote_copy(src_ref=theirs, dst_ref=theirs, send_sem=ssem.at[2 * k + j],
                                         recv_sem=rsem.at[2 * k + j], device_id=(*chip, c),
                                         device_id_type=MESH).wait_recv()
        for cp in started:
            cp.wait_send()
        local.wait()

    return pl.pallas_call(
        body, name="gather_first",
        in_specs=[ANY] * (n + 1), out_specs=[ANY] * (n + 1),
        out_shape=[jax.ShapeDtypeStruct(b.shape, b.dtype) for b in bufs]
        + [jax.ShapeDtypeStruct((N_CHIPS,) + small.shape, small.dtype)],
        input_output_aliases={i: i for i in range(n)},
        scratch_shapes=[pltpu.SemaphoreType.DMA, pltpu.SemaphoreType.DMA((2 * k + 3,)),
                        pltpu.SemaphoreType.DMA((2 * k + 3,))],
    )(*bufs, small)


HBM = pl.BlockSpec(memory_space=pltpu.HBM)
SEM = pl.BlockSpec(memory_space=pltpu.SEMAPHORE)
DATAFLOW = pltpu.SideEffectType.DATAFLOW_SIDE_EFFECTING


def _copies_start(bufs, plan, n_copies, name, after=None):
    n = len(bufs)
    extra = [] if after is None else [after]

    def body(*refs):
        refs = refs[:n] + refs[n + len(extra):]
        ssem, rsem = refs[n], refs[n + 1]
        b_refs, token = refs[n + 2:2 * n + 2], refs[2 * n + 2]
        copies = plan(b_refs)
        assert len(copies) == n_copies
        for i, (src, dst, peer, _) in enumerate(copies):
            pltpu.make_async_remote_copy(src_ref=src, dst_ref=dst, send_sem=ssem.at[i], recv_sem=rsem.at[i],
                                         device_id=peer, device_id_type=MESH).start()
        token[...] = jnp.zeros_like(token)

    return pl.pallas_call(
        body, name=name,
        out_shape=(pltpu.SemaphoreType.DMA((n_copies,)), pltpu.SemaphoreType.DMA((n_copies,)),
                   *[pltpu.HBM(b.shape, b.dtype) for b in bufs], jax.ShapeDtypeStruct((SUBLANES, LANES), F32)),
        in_specs=[HBM] * n + [ANY] * len(extra),
        out_specs=(SEM, SEM, *[HBM] * n, pl.BlockSpec(memory_space=pltpu.VMEM)),
        input_output_aliases={i: 2 + i for i in range(n)},
        compiler_params=pltpu.CompilerParams(has_side_effects=DATAFLOW),
    )(*[pltpu.with_memory_space_constraint(b, pltpu.HBM) for b in bufs], *extra)


def _copies_wait(bufs, ssem, rsem, after, plan, name):
    n = len(bufs)
    afters = list(after) if isinstance(after, (list, tuple)) else [after]

    def body(*refs):
        b_refs, ssem_ref, rsem_ref = refs[:n], refs[n], refs[n + 1]
        for i, (src, dst, peer, lands) in enumerate(plan(b_refs)):
            pltpu.make_async_remote_copy(src_ref=src, dst_ref=dst, send_sem=ssem_ref.at[i], recv_sem=rsem_ref.at[i],
                                         device_id=peer, device_id_type=MESH).wait_send()
            pltpu.make_async_remote_copy(src_ref=lands, dst_ref=lands, send_sem=ssem_ref.at[i],
                                         recv_sem=rsem_ref.at[i], device_id=peer, device_id_type=MESH).wait_recv()

    return pl.pallas_call(
        body, name=name,
        out_shape=tuple(pltpu.HBM(b.shape, b.dtype) for b in bufs),
        in_specs=[HBM] * n + [SEM, SEM] + [ANY] * len(afters), out_specs=tuple([HBM] * n),
        input_output_aliases={i: i for i in range(n)},
        compiler_params=pltpu.CompilerParams(has_side_effects=DATAFLOW),
    )(*bufs, ssem, rsem, *afters)


def _gather_plan(stage):
    return lambda refs: [(ref, ref, peer, lands) for ref, peer, lands in _gather_copies(refs, stage)]


def _swap_plan(refs):
    n = len(refs) // 2
    x, y, c, _ = _place()
    return [(refs[a].at[:, 1 - c], refs[n + a], (x, y, 1 - c), refs[n + a]) for a in range(n)]


def _scatter_plan(refs):
    n = len(refs) // 2
    x, y, c, chips = _place()
    me = _chip_index(x, y)
    return [(refs[a].at[_chip_index(*chip)], refs[n + a].at[me], (*chip, c), refs[n + a].at[_chip_index(*chip)])
            for a in range(n) for chip in chips]


def _pair_gather_plan(refs):
    x, y, c, _ = _place()
    return [(r.at[c], r.at[c], (x, y, 1 - c), r.at[1 - c]) for r in refs]


def _pair_swap(xs, name):
    n = len(xs)

    def body(*refs):
        x_refs, o_refs, ssem, rsem = refs[:n], refs[n:2 * n], refs[2 * n], refs[2 * n + 1]
        x, y, c, _ = _place()
        copies = [pltpu.make_async_remote_copy(src_ref=x_refs[a].at[:, 1 - c], dst_ref=o_refs[a],
                                               send_sem=ssem.at[a], recv_sem=rsem.at[a],
                                               device_id=(x, y, 1 - c), device_id_type=MESH) for a in range(n)]
        for cp in copies:
            cp.start()
        for cp in copies:
            cp.wait()

    return pl.pallas_call(
        body, name=name, in_specs=[ANY] * n, out_specs=[ANY] * n,
        out_shape=[jax.ShapeDtypeStruct((a.shape[0],) + a.shape[2:], a.dtype) for a in xs],
        scratch_shapes=[pltpu.SemaphoreType.DMA((n,)), pltpu.SemaphoreType.DMA((n,))],
    )(*xs)


def _chip_scatter(ps):
    n = len(ps)

    def body(*refs):
        p_refs, o_refs, ssem, rsem = refs[:n], refs[n:2 * n], refs[2 * n], refs[2 * n + 1]
        x, y, c, chips = _place()
        me = _chip_index(x, y)
        sends = []
        for a in range(n):
            for j, chip in enumerate(chips):
                sends.append(pltpu.make_async_remote_copy(
                    src_ref=p_refs[a].at[_chip_index(*chip)], dst_ref=o_refs[a].at[me],
                    send_sem=ssem.at[3 * a + j], recv_sem=rsem.at[3 * a + j],
                    device_id=(*chip, c), device_id_type=MESH))
        for cp in sends:
            cp.start()
        for a in range(n):
            for j, chip in enumerate(chips):
                src = _chip_index(*chip)
                pltpu.make_async_remote_copy(
                    src_ref=p_refs[a].at[src], dst_ref=o_refs[a].at[src],
                    send_sem=ssem.at[3 * a + j], recv_sem=rsem.at[3 * a + j],
                    device_id=(*chip, c), device_id_type=MESH).wait_recv()
        for cp in sends:
            cp.wait_send()

    return pl.pallas_call(
        body, name="chip_scatter", in_specs=[ANY] * n, out_specs=[ANY] * n,
        out_shape=[jax.ShapeDtypeStruct(a.shape, a.dtype) for a in ps],
        scratch_shapes=[pltpu.SemaphoreType.DMA((3 * n,)), pltpu.SemaphoreType.DMA((3 * n,))],
    )(*ps)


def _final_gather(fs, rep):
    n = len(fs)

    def body(*refs):
        o_refs, repo_ref = refs[n + 1:2 * n + 1], refs[2 * n + 1]
        ssem, rsem = refs[2 * n + 2:]
        x, y, c, chips = _place()
        slot = 4 * x + 2 * y + c
        copies = [pltpu.make_async_remote_copy(src_ref=o_refs[a].at[c], dst_ref=o_refs[a].at[c],
                                               send_sem=ssem.at[a], recv_sem=rsem.at[a],
                                               device_id=(x, y, 1 - c), device_id_type=MESH) for a in range(n)]
        peers = [(x, y, 1 - c)] + [(*chip, c) for chip in chips] + [(*chip, 1 - c) for chip in chips]
        for k, peer in enumerate(peers):
            copies.append(pltpu.make_async_remote_copy(src_ref=repo_ref.at[slot], dst_ref=repo_ref.at[slot],
                                                       send_sem=ssem.at[n + k], recv_sem=rsem.at[n + k],
                                                       device_id=peer, device_id_type=MESH))
        for cp in copies:
            cp.start()
        for a in range(n):
            pltpu.make_async_remote_copy(src_ref=o_refs[a].at[1 - c], dst_ref=o_refs[a].at[1 - c],
                                         send_sem=ssem.at[a], recv_sem=rsem.at[a],
                                         device_id=(x, y, 1 - c), device_id_type=MESH).wait_recv()
        for k, peer in enumerate(peers):
            px, py, pc = peer
            theirs = repo_ref.at[4 * px + 2 * py + pc]
            pltpu.make_async_remote_copy(src_ref=theirs, dst_ref=theirs, send_sem=ssem.at[n + k], recv_sem=rsem.at[n + k],
                                         device_id=peer, device_id_type=MESH).wait_recv()
        for cp in copies:
            cp.wait_send()

    return pl.pallas_call(
        body, name="final_gather", in_specs=[ANY] * (n + 1), out_specs=[ANY] * (n + 1),
        out_shape=[jax.ShapeDtypeStruct(a.shape, a.dtype) for a in fs] + [jax.ShapeDtypeStruct(rep.shape, rep.dtype)],
        input_output_aliases={k: k for k in range(n + 1)},
        scratch_shapes=[pltpu.SemaphoreType.DMA((n + 7,)), pltpu.SemaphoreType.DMA((n + 7,))],
    )(*fs, rep)


def _block_diag(w, gb):
    nh, hd, _ = w.shape
    per = gb // hd
    w4 = w.reshape(nh // per, per, hd, hd)
    eye = jnp.eye(per, dtype=w.dtype)
    return jnp.einsum("jaik,ab->jaibk", w4, eye).reshape(nh // per, gb, gb)


def _diag_blocks(dense, hd):
    nj, gb, _ = dense.shape
    per = gb // hd
    d5 = dense.reshape(nj, per, hd, per, hd)
    return jnp.stack([d5[:, a, :, a, :] for a in range(per)], axis=1).reshape(nj * per, hd, hd)


def _round_up(n, q):
    return (n + q - 1) // q * q


def kernel(x, meta, norm_g, w_in, conv_a_w, conv_a_b, lru_wr, lru_br, lru_wi, lru_bi, lru_lambda, conv_b_w, w_out, final_g, loss_target, m_meta, m_norm_g, m_w_in, m_conv_a_w, m_conv_a_b, m_lru_wr, m_lru_br, m_lru_wi, m_lru_bi, m_lru_lambda, m_conv_b_w, m_w_out, m_final_g, v_meta, v_norm_g, v_w_in, v_conv_a_w, v_conv_a_b, v_lru_wr, v_lru_br, v_lru_wi, v_lru_bi, v_lru_lambda, v_conv_b_w, v_w_out, v_final_g):
    weights = dict(meta=meta, norm_g=norm_g, w_in=w_in, conv_a_w=conv_a_w, conv_a_b=conv_a_b, lru_wr=lru_wr,
                   lru_br=lru_br, lru_wi=lru_wi, lru_bi=lru_bi, lru_lambda=lru_lambda, conv_b_w=conv_b_w,
                   w_out=w_out, final_g=final_g)
    mom1 = dict(meta=m_meta, norm_g=m_norm_g, w_in=m_w_in, conv_a_w=m_conv_a_w, conv_a_b=m_conv_a_b,
                lru_wr=m_lru_wr, lru_br=m_lru_br, lru_wi=m_lru_wi, lru_bi=m_lru_bi, lru_lambda=m_lru_lambda,
                conv_b_w=m_conv_b_w, w_out=m_w_out, final_g=m_final_g)
    mom2 = dict(meta=v_meta, norm_g=v_norm_g, w_in=v_w_in, conv_a_w=v_conv_a_w, conv_a_b=v_conv_a_b,
                lru_wr=v_lru_wr, lru_br=v_lru_br, lru_wi=v_lru_wi, lru_bi=v_lru_bi, lru_lambda=v_lru_lambda,
                conv_b_w=v_conv_b_w, w_out=v_w_out, final_g=v_final_g)
    names = list(weights)

    assert x.shape[0] == 1
    seq, d = x.shape[1], x.shape[2]
    n_meta, ds = meta.shape
    depth = norm_g.shape[0]
    c = lru_lambda.shape[1]
    nh, hd = lru_wr.shape[1], lru_wr.shape[2]
    ns = w_in.shape[2]
    dms = w_out.shape[1]
    cs = conv_a_w.shape[2]
    ka, kb = conv_a_w.shape[1], conv_b_w.shape[1]
    s = N_CHIPS
    assert depth == N_CORES and d == s * ds and c == s * cs and s * ns == 6 * c and s * dms == 2 * c
    gb = min(GATE_BLOCK, c)
    t_real = n_meta + seq
    t = _round_up(t_real, ROW_QUANTUM)
    my_c = lax.axis_index("c").astype(jnp.int32)
    my_chip = (2 * lax.axis_index("x") + lax.axis_index("y")).astype(jnp.int32)
    c_idx = my_c.reshape(1)
    chip_idx = my_chip.reshape(1)

    sm_rows = _round_up(n_meta + depth * SUBLANES, 2 * SUBLANES)
    small = jnp.zeros((sm_rows, ds), F32)
    small = small.at[0:n_meta, :].set(meta)
    for l in range(depth):
        base = n_meta + l * SUBLANES
        small = small.at[base:base + ka, 0:cs].set(conv_a_w[l])
        small = small.at[base + ka:base + ka + kb, 0:cs].set(conv_b_w[l])
    (small_g,) = _gather_first([], small)
    meta_full = jnp.transpose(small_g[:, 0:n_meta, :], (1, 0, 2)).reshape(n_meta, d)
    wa_full, wb_full = [], []
    for l in range(depth):
        base = n_meta + l * SUBLANES
        wa_full.append(jnp.transpose(small_g[:, base:base + ka, 0:cs], (1, 0, 2)).reshape(ka, c))
        wb_full.append(jnp.transpose(small_g[:, base + ka:base + ka + kb, 0:cs], (1, 0, 2)).reshape(kb, c))
    win0 = _cast_place(w_in, 0, chip_idx, "cast_w_in_0").reshape(s, 2, d // 2, ns)
    ssem_w, rsem_w, win0, token_w = _copies_start([win0], _gather_plan(0), 3, "gather_win0_ici_start", after=small_g)
    win_b = [None] + [_cast_place(w_in, l, chip_idx, f"cast_w_in_{l}", after=token_w).reshape(s, 2, d // 2, ns)
                      for l in range(1, depth)]
    wout_b = [_cast_place(w_out, l, chip_idx, f"cast_w_out_{l}", after=token_w).reshape(s, 2, dms // 2, d)
              for l in range(depth)]
    h = jnp.concatenate([meta_full, x[0], jnp.zeros((t - t_real, d), F32)], axis=0) + token_w[0, 0]
    tgt = jnp.concatenate([jnp.zeros((n_meta, d), F32), loss_target[0], jnp.zeros((t - t_real, d), F32)],
                          axis=0) + token_w[0, 0]
    (win0,) = _copies_wait([win0], ssem_w, rsem_w, [h, tgt] + win_b[1:] + wout_b, _gather_plan(0),
                           "gather_win0_ici_wait")
    ssem_w, rsem_w, win0, token_w = _copies_start([win0], _gather_plan(1), 3, "gather_win0_d2d_start")
    ssem_o, rsem_o, wout0, token_o = _copies_start([wout_b[0]], _gather_plan(0), 3, "gather_wout0_ici_start",
                                                   after=token_w)
    later = [win_b[1], wout_b[1]]
    ssem, rsem, *later, token = _copies_start(later, _gather_plan(0), 3 * len(later), "gather_next_ici_start",
                                              after=token_o)
    (win_b[0],) = _copies_wait([win0], ssem_w, rsem_w, token, _gather_plan(1), "gather_win0_d2d_wait")

    layer_w = []
    for l in range(depth):
        layer_w.append(dict(
            g=norm_g[l].reshape(1, d), wa=wa_full[l], ba=conv_a_b[l].reshape(1, c),
            wr=_block_diag(lru_wr[l], gb).astype(BF16), br=lru_br[l].reshape(1, c),
            wi=_block_diag(lru_wi[l], gb).astype(BF16), bi=lru_bi[l].reshape(1, c),
            lam=lru_lambda[l].reshape(1, c), wb=wb_full[l]))
    saved = []
    for l, lw in enumerate(layer_w):
        first = l == 0
        lw["win"] = win_b[l].reshape(s, d, ns)
        u, hn = _norm_in(h, lw["g"] + token[0, 0] if first else lw["g"], lw["win"], f"norm_in_{l}")
        if first:
            (wout0,) = _copies_wait([wout0], ssem_o, rsem_o, u, _gather_plan(0), "gather_wout0_ici_wait")
            ssem_o, rsem_o, wout0, token_o = _copies_start([wout0], _gather_plan(1), 3, "gather_wout0_d2d_start")
        y, hs = _mix_fwd(u, lw["wa"], lw["ba"] + token_o[0, 0] if first else lw["ba"], lw["wr"], lw["br"], lw["wi"],
                         lw["bi"], lw["lam"], lw["wb"], f"mix_fwd_{l}")
        token = None
        if first:
            (wout_b[0],) = _copies_wait([wout0], ssem_o, rsem_o, y, _gather_plan(1), "gather_wout0_d2d_wait")
            later = _copies_wait(later, ssem, rsem, y, _gather_plan(0), "gather_next_ici_wait")
            ssem, rsem, *later, token = _copies_start(later, _gather_plan(1), 3 * len(later), "gather_next_d2d_start")
        lw["wout"] = wout_b[l].reshape(2 * c, d)
        saved.append((h, u, hn, y, hs))
        h = _out_proj(h, y, lw["wout"], f"out_proj_{l}", after=token)
        if first:
            win_b[1], wout_b[1] = _copies_wait(later, ssem, rsem, h, _gather_plan(1), "gather_next_d2d_wait")
    dh, loss_lanes, d_final_g = _loss_head(h, tgt, final_g.reshape(1, d), n_meta, t_real, "loss_head")
    loss = lax.psum(loss_lanes[0, 0], ("x", "y", "c"))

    to_core = jnp.stack([my_chip, my_c])
    grads = [None] * depth
    early = None
    for l in reversed(range(depth)):
        lw = layer_w[l]
        h_in, u, hn, y, hs = saved[l]
        token = early[-1] if early else None
        dy = _out_proj_dy(dh, lw["wout"], f"out_proj_dy_{l}", after=token)
        d_wout = _out_proj_dw(y, dh, f"out_proj_dw_{l}")
        if early:
            ssem, rsem, bufs, _ = early
            bufs = _copies_wait(bufs, ssem, rsem, d_wout, _swap_plan, "early_swap_wait")
            half = len(bufs) // 2
            sums = [_pair_add(a, b, c_idx, f"early_pair_add_{k}") for k, (a, b) in enumerate(zip(bufs[:half], bufs[half:]))]
            lands = [lax.empty(p.shape, p.dtype) for p in sums]
            ssem, rsem, *bufs, token = _copies_start(sums + lands, _scatter_plan, 3 * half, "early_scatter_start")
        du, dsm, d_wr, d_wi = _mix_bwd(u, hs, dy, lw["wa"], lw["ba"], lw["wr"], lw["br"], lw["wi"], lw["bi"],
                                       lw["lam"], lw["wb"], f"mix_bwd_{l}", after=token)
        if early:
            bufs = _copies_wait(bufs, ssem, rsem, du, _scatter_plan, "early_scatter_wait")
            halves = [_chip_sum(rc, p, to_core, N_CORES, f"early_chip_sum_{k}")
                      for k, (p, rc) in enumerate(zip(bufs[:half], bufs[half:]))]
            ssem, rsem, *bufs, token = _copies_start(halves, _pair_gather_plan, half, "early_gather_start")
        d_win = _in_proj_dw(hn, du, s, f"in_proj_dw_{l}", after=token)
        srcs = [d_win.reshape(s, 2, d // 2, ns), d_wout.reshape(s, 2, dms // 2, d)]
        if early:
            early_full = _copies_wait(bufs, ssem, rsem, d_win, _pair_gather_plan, "early_gather_wait")
            from_sibling = _pair_swap(srcs, "pair_swap")
            late_sums = [_pair_add(a, b, c_idx, f"pair_add_{k}") for k, (a, b) in enumerate(zip(srcs, from_sibling))]
            lands = [lax.empty(p.shape, p.dtype) for p in late_sums]
            ssem, rsem, *bufs, token = _copies_start(late_sums + lands, _scatter_plan, 3 * len(srcs), "late_scatter_start")
        dh, d_g = _in_proj_bwd(du, lw["win"], h_in, lw["g"], dh, f"in_proj_bwd_{l}", after=token)
        if early:
            bufs = _copies_wait(bufs, ssem, rsem, dh, _scatter_plan, "late_scatter_wait")
            late_reduced = [_chip_sum(rc, p, to_core, N_CORES, f"chip_sum_{k}")
                            for k, (p, rc) in enumerate(zip(bufs[:len(srcs)], bufs[len(srcs):]))]
        grads[l] = dict(dsm=dsm, wr=_diag_blocks(d_wr, hd), wi=_diag_blocks(d_wi, hd), g=d_g)
        if l == depth - 1:
            lands = [lax.empty((a.shape[0],) + a.shape[2:], a.dtype) for a in srcs]
            ssem, rsem, *bufs, token = _copies_start(srcs + lands, _swap_plan, len(srcs), "early_swap_start")
            early = (ssem, rsem, bufs, token)
        else:
            early = None
    grad_x = dh[n_meta:t_real][None]

    sharded = []
    sp = jnp.zeros((sm_rows, s, ds), F32)
    sp = sp.at[0:n_meta].set(dh[0:n_meta].reshape(n_meta, s, ds))
    for l in range(depth):
        base = n_meta + l * SUBLANES
        dsm = grads[l]["dsm"]
        sp = sp.at[base:base + ka, :, 0:cs].set(dsm[ROW_DWA:ROW_DWA + ka].reshape(ka, s, cs))
        sp = sp.at[base + ka:base + ka + kb, :, 0:cs].set(dsm[ROW_DWB:ROW_DWB + kb].reshape(kb, s, cs))
    sharded.append(jnp.transpose(sp, (1, 0, 2)).reshape(s, 2, sm_rows // 2, ds))
    rep_parts = [jnp.concatenate([grads[l]["g"].reshape(-1) for l in range(depth)]), d_final_g.reshape(-1)]
    for row in (ROW_DBA, ROW_DBR, ROW_DBI, ROW_DLAM):
        rep_parts.append(jnp.concatenate([grads[l]["dsm"][row] for l in range(depth)]))
    rep_parts.append(jnp.concatenate([grads[l]["wr"].reshape(-1) for l in range(depth)]))
    rep_parts.append(jnp.concatenate([grads[l]["wi"].reshape(-1) for l in range(depth)]))
    rep_sizes = [p.shape[0] for p in rep_parts]
    piece = _round_up(-(-sum(rep_sizes) // (s * 2)), 2 * SUBLANES * LANES)
    flat = jnp.concatenate(rep_parts + [jnp.zeros((s * 2 * piece - sum(rep_sizes),), F32)])
    sharded.append(flat.reshape(s, 2, piece // LANES, LANES))

    from_sibling = _pair_swap(sharded, "small_pair_swap")
    pair_sums = [_pair_add(a, b, c_idx, f"small_pair_add_{k}") for k, (a, b) in enumerate(zip(sharded, from_sibling))]
    by_chip = _chip_scatter(pair_sums)
    to_device = jnp.stack([my_chip, 2 * my_chip + my_c])
    reduced_sp = _chip_sum(by_chip[0], pair_sums[0], to_core, N_CORES, "small_chip_sum")
    reduced_rep = _chip_sum(by_chip[1], pair_sums[1], to_device, N_CHIPS * N_CORES, "chip_sum_rep")
    *full, rep_all = _final_gather(late_reduced + [reduced_sp], reduced_rep)

    g_win = [full[0].reshape(d, ns), early_full[0].reshape(d, ns)]
    g_wout = [full[1].reshape(dms, d), early_full[1].reshape(dms, d)]
    g_sp = full[2].reshape(sm_rows, ds)
    rep_flat = rep_all.reshape(-1)
    rep_out, off = [], 0
    for n in rep_sizes:
        rep_out.append(rep_flat[off:off + n])
        off += n
    grad = dict(
        meta=g_sp[0:n_meta],
        norm_g=rep_out[0].reshape(depth, d),
        w_in=jnp.stack(g_win),
        conv_a_w=jnp.stack([g_sp[n_meta + l * SUBLANES:n_meta + l * SUBLANES + ka, 0:cs] for l in range(depth)]),
        conv_a_b=rep_out[2].reshape(depth, c),
        lru_wr=rep_out[6].reshape(depth, nh, hd, hd),
        lru_br=rep_out[3].reshape(depth, c),
        lru_wi=rep_out[7].reshape(depth, nh, hd, hd),
        lru_bi=rep_out[4].reshape(depth, c),
        lru_lambda=rep_out[5].reshape(depth, c),
        conv_b_w=jnp.stack([g_sp[n_meta + l * SUBLANES + ka:n_meta + l * SUBLANES + ka + kb, 0:cs]
                            for l in range(depth)]),
        w_out=jnp.stack(g_wout),
        final_g=rep_out[1].reshape(d),
    )

    delta, new_m, new_v = {}, {}, {}
    for n in names:
        shape = weights[n].shape
        two_d = (-1, shape[-1]) if len(shape) > 1 else (1, -1)
        if n in ("lru_wr", "lru_wi"):
            two_d = (-1, LANES)
        out = _adamw(weights[n].reshape(two_d), grad[n].reshape(two_d), mom1[n].reshape(two_d),
                     mom2[n].reshape(two_d), f"adamw_{n}")
        delta[n], new_m[n], new_v[n] = (o.reshape(shape) for o in out)

    return (loss, grad_x, *[grad[n] for n in names], *[delta[n] for n in names],
            *[new_m[n] for n in names], *[new_v[n] for n in names])
```

```python
import functools

import jax
import jax.numpy as jnp
from jax import lax
from jax.experimental import pallas as pl
from jax.experimental.pallas import tpu as pltpu

F32 = jnp.float32
BF16 = jnp.bfloat16

RMS_EPS = 1e-6
LRU_C = 8.0
ADAM_LR = 0.001
ADAM_B1 = 0.9
ADAM_B2 = 0.999
ADAM_EPS = 1e-08
ADAM_WD = 0.01
ADAM_STEP = 10

N_CHIPS = 4
N_CORES = 2
VMEM_LIMIT_BYTES = 56 * 1024 * 1024
SUBLANES = 8
LANES = 128
ROW_QUANTUM = 384
MIX_CHUNK = 192
GATE_BLOCK = 256
MESH = pl.DeviceIdType.MESH
ANY = pl.BlockSpec(memory_space=pl.ANY)

NT_DIMS = (((1,), (1,)), ((), ()))
TN_DIMS = (((0,), (0,)), ((), ()))


def _params(sem):
    return pltpu.CompilerParams(dimension_semantics=sem, vmem_limit_bytes=VMEM_LIMIT_BYTES)


def _sig(x):
    return 0.5 * jnp.tanh(0.5 * x) + 0.5


def _row_tile(t):
    return 704 if t % 704 == 0 else 192


def _col_tile(n, prefs):
    for p in prefs:
        if n % p == 0:
            return p
    return n


def _slab_rows(rows, cols):
    if rows * cols * 4 <= 1024 * 1024:
        return rows
    return _col_tile(rows, (256, 128, 64, 32, 16))


def _norm_in(h, g, wg, name):
    t, d = h.shape
    s, _, ns = wg.shape
    tm = 1408 if t % 1408 == 0 else _row_tile(t)
    tn = _col_tile(ns, (768, 384, 128))
    nb = ns // tn

    def body(h_ref, g_ref, w_ref, u_ref, hn_ref):
        @pl.when(pl.program_id(1) == 0)
        def _():
            x = h_ref[...]
            r = lax.rsqrt(jnp.mean(x * x, axis=-1, keepdims=True) + RMS_EPS)
            hn_ref[...] = ((x * r) * g_ref[...]).astype(BF16)

        u_ref[...] = jnp.dot(hn_ref[...], w_ref[...], preferred_element_type=F32)

    return pl.pallas_call(
        body, name=name, grid=(t // tm, s * nb),
        in_specs=[pl.BlockSpec((tm, d), lambda i, n: (i, 0)),
                  pl.BlockSpec((1, d), lambda i, n: (0, 0)),
                  pl.BlockSpec((None, d, tn), lambda i, n: (n // nb, 0, n % nb))],
        out_specs=[pl.BlockSpec((tm, tn), lambda i, n: (i, n)),
                   pl.BlockSpec((tm, d), lambda i, n: (i, 0))],
        out_shape=[jax.ShapeDtypeStruct((t, s * ns), F32), jax.ShapeDtypeStruct((t, d), BF16)],
        compiler_params=_params(("arbitrary", "arbitrary")),
    )(h, g, wg)


def _decay_consts(lam):
    z = -lam
    e = jnp.exp(-jnp.abs(z))
    u = 1.0 + e
    log1p_e = jnp.where(u == 1.0, e, jnp.log(u) * (e / (u - 1.0)))
    sp = jnp.maximum(z, 0.0) + log1p_e
    return -LRU_C * sp, LRU_C * _sig(z)


def _gates(xc, wr_ref, br_ref, wi_ref, bi_ref, c8, j, gb):
    sl = slice(j * gb, (j + 1) * gb)
    x16 = xc.astype(BF16)
    r = _sig(jnp.dot(x16, wr_ref[j], preferred_element_type=F32) + br_ref[:, sl])
    ig = _sig(jnp.dot(x16, wi_ref[j], preferred_element_type=F32) + bi_ref[:, sl])
    la = c8[:, sl] * r
    a = jnp.exp(la)
    sq = jnp.sqrt(-jnp.tanh(la) * (a * a + 1.0))
    return r, ig, a, sq


def _mix_fwd(u, wa, ba, wr, br, wi, bi, lam, wb, name):
    t = u.shape[0]
    c = u.shape[1] // 6
    tc = MIX_CHUNK
    gb = wr.shape[1]
    nblk = c // gb
    ka, kb = wa.shape[0], wb.shape[0]

    def body(u_ref, wa_ref, ba_ref, wr_ref, br_ref, wi_ref, bi_ref, lam_ref, wb_ref,
             y_ref, hs_ref, xa_ext, v_ext, xc_s, a_s, b_s, carry_s):
        @pl.when(pl.program_id(0) == 0)
        def _():
            xa_ext[0:SUBLANES, :] = jnp.zeros((SUBLANES, c), F32)
            v_ext[0:SUBLANES, :] = jnp.zeros((SUBLANES, c), F32)
            carry_s[...] = jnp.zeros_like(carry_s)

        xa_ext[SUBLANES:SUBLANES + tc, :] = u_ref[:, 0:c]
        xc = ba_ref[...]
        for k in range(ka):
            xc = xc + wa_ref[pl.ds(k, 1), :] * xa_ext[pl.ds(SUBLANES - (ka - 1) + k, tc), :]
        xc_s[...] = xc
        c8, _ = _decay_consts(lam_ref[...])
        for j in range(nblk):
            sl = slice(j * gb, (j + 1) * gb)
            xcj = xc_s[:, sl]
            _, ig, a, sq = _gates(xcj, wr_ref, br_ref, wi_ref, bi_ref, c8, j, gb)
            a_s[:, sl] = a
            b_s[:, sl] = sq * (ig * xcj)

        row = lax.broadcasted_iota(jnp.int32, (SUBLANES, c), 0)

        def scan_step(j, _):
            off = pl.multiple_of(j * SUBLANES, SUBLANES)
            av = a_s[pl.ds(off, SUBLANES), :]
            bv = b_s[pl.ds(off, SUBLANES), :]
            for d in (1, 2, 4):
                keep = row >= d
                bsh = jnp.where(keep, pltpu.roll(bv, d, axis=0), 0.0)
                ash = jnp.where(keep, pltpu.roll(av, d, axis=0), 1.0)
                bv = av * bsh + bv
                av = av * ash
            hv = av * carry_s[...] + bv
            hs_ref[pl.ds(off, SUBLANES), :] = hv
            carry_s[...] = hs_ref[pl.ds(off + SUBLANES - 1, 1), :]
            return 0

        lax.fori_loop(0, tc // SUBLANES, scan_step, 0)

        ga = u_ref[:, c:2 * c]
        y_ref[:, 0:c] = (hs_ref[...] * (ga * _sig(ga))).astype(BF16)

        v_ext[SUBLANES:SUBLANES + tc, :] = u_ref[:, 3 * c:4 * c] * u_ref[:, 4 * c:5 * c]
        cv = wb_ref[pl.ds(0, 1), :] * v_ext[pl.ds(SUBLANES - (kb - 1), tc), :]
        for k in range(1, kb):
            cv = cv + wb_ref[pl.ds(k, 1), :] * v_ext[pl.ds(SUBLANES - (kb - 1) + k, tc), :]
        gbv = u_ref[:, 5 * c:6 * c]
        y_ref[:, c:2 * c] = (u_ref[:, 2 * c:3 * c] * cv * (gbv * _sig(gbv))).astype(BF16)

        xa_ext[0:SUBLANES, :] = xa_ext[tc:tc + SUBLANES, :]
        v_ext[0:SUBLANES, :] = v_ext[tc:tc + SUBLANES, :]

    full = lambda shape: pl.BlockSpec(shape, lambda i: (0,) * len(shape))
    return pl.pallas_call(
        body, name=name, grid=(t // tc,),
        in_specs=[pl.BlockSpec((tc, 6 * c), lambda i: (i, 0)),
                  full(wa.shape), full(ba.shape), full(wr.shape), full(br.shape),
                  full(wi.shape), full(bi.shape), full(lam.shape), full(wb.shape)],
        out_specs=[pl.BlockSpec((tc, 2 * c), lambda i: (i, 0)),
                   pl.BlockSpec((tc, c), lambda i: (i, 0))],
        out_shape=[jax.ShapeDtypeStruct((t, 2 * c), BF16), jax.ShapeDtypeStruct((t, c), F32)],
        scratch_shapes=[pltpu.VMEM((tc + SUBLANES, c), F32), pltpu.VMEM((tc + SUBLANES, c), F32),
                        pltpu.VMEM((tc, c), F32), pltpu.VMEM((tc, c), F32), pltpu.VMEM((tc, c), F32),
                        pltpu.VMEM((1, c), F32)],
        compiler_params=_params(("arbitrary",)),
    )(u, wa, ba, wr, br, wi, bi, lam, wb)


ROW_DWA = 0
ROW_DBA = 4
ROW_DBR = 5
ROW_DBI = 6
ROW_DLAM = 7
ROW_DWB = 8
SMALL_ROWS = 16


def _mix_bwd(u, hs, dy, wa, ba, wr, br, wi, bi, lam, wb, name, after=None):
    t = u.shape[0]
    c = u.shape[1] // 6
    tc = MIX_CHUNK
    nt = t // tc
    gb = wr.shape[1]
    nblk = c // gb
    ka, kb = wa.shape[0], wb.shape[0]
    assert ka <= ROW_DBA and kb <= SMALL_ROWS - ROW_DWB
    hb = tc // SUBLANES

    def body(u_ref, uh_ref, hs_ref, hsh_ref, dy_ref, wa_ref, ba_ref, wr_ref, br_ref, wi_ref, bi_ref, lam_ref, wb_ref,
             du_ref, dsm_ref, dwr_ref, dwi_ref,
             xa_ext, v_ext, hs_ext, a_ext, ds_ext, dxc_ext, dcv_ext, xc_s, r_s, i_s, sq_s, g_s, an_s):
        i = pl.program_id(0)
        chunk = nt - 1 - i
        tail = slice(tc, tc + SUBLANES)
        head = slice(0, SUBLANES)

        @pl.when(i == 0)
        def _():
            zero = jnp.zeros((SUBLANES, c), F32)
            a_ext[tail, :] = zero
            ds_ext[tail, :] = zero
            dxc_ext[tail, :] = zero
            dcv_ext[tail, :] = zero
            dsm_ref[...] = jnp.zeros_like(dsm_ref)
            dwr_ref[...] = jnp.zeros_like(dwr_ref)
            dwi_ref[...] = jnp.zeros_like(dwi_ref)

        prev = jnp.where(chunk > 0, 1.0, 0.0)
        xa_ext[head, :] = uh_ref[:, 0:c] * prev
        xa_ext[SUBLANES:SUBLANES + tc, :] = u_ref[:, 0:c]
        v_ext[head, :] = uh_ref[:, 3 * c:4 * c] * uh_ref[:, 4 * c:5 * c] * prev
        v_ext[SUBLANES:SUBLANES + tc, :] = u_ref[:, 3 * c:4 * c] * u_ref[:, 4 * c:5 * c]
        hs_ext[head, :] = hsh_ref[...] * prev
        hs_ext[SUBLANES:SUBLANES + tc, :] = hs_ref[...]

        xc = ba_ref[...]
        for k in range(ka):
            xc = xc + wa_ref[pl.ds(k, 1), :] * xa_ext[pl.ds(SUBLANES - (ka - 1) + k, tc), :]
        xc_s[...] = xc
        c8, dc8 = _decay_consts(lam_ref[...])
        for j in range(nblk):
            sl = slice(j * gb, (j + 1) * gb)
            r, ig, a, sq = _gates(xc_s[:, sl], wr_ref, br_ref, wi_ref, bi_ref, c8, j, gb)
            r_s[:, sl] = r
            i_s[:, sl] = ig
            sq_s[:, sl] = sq
            a_ext[0:tc, sl] = a

        ga = u_ref[:, c:2 * c]
        sga = _sig(ga)
        g_s[...] = dy_ref[:, 0:c] * (ga * sga)
        an_s[...] = a_ext[pl.ds(1, tc), :]

        row = lax.broadcasted_iota(jnp.int32, (SUBLANES, c), 0)

        def scan_step(j, _):
            off = pl.multiple_of(tc - SUBLANES - j * SUBLANES, SUBLANES)
            av = an_s[pl.ds(off, SUBLANES), :]
            bv = g_s[pl.ds(off, SUBLANES), :]
            for d in (1, 2, 4):
                keep = row < SUBLANES - d
                bsh = jnp.where(keep, pltpu.roll(bv, SUBLANES - d, axis=0), 0.0)
                ash = jnp.where(keep, pltpu.roll(av, SUBLANES - d, axis=0), 1.0)
                bv = av * bsh + bv
                av = av * ash
            ds_ext[pl.ds(off, SUBLANES), :] = av * ds_ext[pl.ds(off + SUBLANES, 1), :] + bv
            return 0

        lax.fori_loop(0, tc // SUBLANES, scan_step, 0)

        def acc(row_index, val):
            dsm_ref[pl.ds(row_index, 1), :] += jnp.sum(val, axis=0, keepdims=True)

        def acc_block(row_index, sl, val):
            dsm_ref[pl.ds(row_index, 1), sl] += jnp.sum(val, axis=0, keepdims=True)

        for j in range(nblk):
            sl = slice(j * gb, (j + 1) * gb)
            ds = ds_ext[0:tc, sl]
            hprev = hs_ext[pl.ds(SUBLANES - 1, tc), sl]
            a = a_ext[0:tc, sl]
            sq = sq_s[:, sl]
            ig = i_s[:, sl]
            r = r_s[:, sl]
            xcj = xc_s[:, sl]
            t1 = ds * xcj
            dla = (ds * hprev) * a - (t1 * ig) * ((a * a) / sq)
            acc_block(ROW_DLAM, sl, dla * r)
            dpr = (dla * c8[:, sl]) * (r * (1.0 - r))
            dpi = (t1 * sq) * (ig * (1.0 - ig))
            acc_block(ROW_DBR, sl, dpr)
            acc_block(ROW_DBI, sl, dpi)
            p16 = dpr.astype(BF16)
            q16 = dpi.astype(BF16)
            x16 = xcj.astype(BF16)
            dwr_ref[j] += lax.dot_general(x16, p16, TN_DIMS, preferred_element_type=F32)
            dwi_ref[j] += lax.dot_general(x16, q16, TN_DIMS, preferred_element_type=F32)
            dxc = (ds * (sq * ig)
                   + lax.dot_general(p16, wr_ref[j], NT_DIMS, preferred_element_type=F32)
                   + lax.dot_general(q16, wi_ref[j], NT_DIMS, preferred_element_type=F32))
            dxc_ext[0:tc, sl] = dxc
            acc_block(ROW_DBA, sl, dxc)

        dsilu_a = sga * (1.0 + ga * (1.0 - sga))
        du_ref[:, c:2 * c] = (dy_ref[:, 0:c] * hs_ref[...] * dsilu_a).astype(BF16)

        dxc = dxc_ext[0:tc, :]
        dxa = wa_ref[pl.ds(ka - 1, 1), :] * dxc
        acc(ROW_DWA + ka - 1, dxc * xa_ext[SUBLANES:SUBLANES + tc, :])
        for k in range(ka - 1):
            acc(ROW_DWA + k, dxc * xa_ext[pl.ds(SUBLANES - (ka - 1) + k, tc), :])
            dxa = dxa + wa_ref[pl.ds(k, 1), :] * dxc_ext[pl.ds(ka - 1 - k, tc), :]
        du_ref[:, 0:c] = dxa.astype(BF16)

        cv = wb_ref[pl.ds(0, 1), :] * v_ext[pl.ds(SUBLANES - (kb - 1), tc), :]
        for k in range(1, kb):
            cv = cv + wb_ref[pl.ds(k, 1), :] * v_ext[pl.ds(SUBLANES - (kb - 1) + k, tc), :]
        gbv = u_ref[:, 5 * c:6 * c]
        sgb = _sig(gbv)
        silu_b = gbv * sgb
        dyb = dy_ref[:, c:2 * c]
        gB = u_ref[:, 2 * c:3 * c]
        du_ref[:, 2 * c:3 * c] = (dyb * cv * silu_b).astype(BF16)
        du_ref[:, 5 * c:6 * c] = (dyb * gB * cv * (sgb * (1.0 + gbv * (1.0 - sgb)))).astype(BF16)
        dcv = dyb * gB * silu_b
        dcv_ext[0:tc, :] = dcv
        dv = wb_ref[pl.ds(kb - 1, 1), :] * dcv
        acc(ROW_DWB + kb - 1, dcv * v_ext[SUBLANES:SUBLANES + tc, :])
        for k in range(kb - 1):
            acc(ROW_DWB + k, dcv * v_ext[pl.ds(SUBLANES - (kb - 1) + k, tc), :])
            dv = dv + wb_ref[pl.ds(k, 1), :] * dcv_ext[pl.ds(kb - 1 - k, tc), :]
        du_ref[:, 3 * c:4 * c] = (dv * u_ref[:, 4 * c:5 * c]).astype(BF16)
        du_ref[:, 4 * c:5 * c] = (dv * u_ref[:, 3 * c:4 * c]).astype(BF16)

        a_ext[tail, :] = a_ext[head, :]
        ds_ext[tail, :] = ds_ext[head, :]
        dxc_ext[tail, :] = dxc_ext[head, :]
        dcv_ext[tail, :] = dcv_ext[head, :]

        @pl.when(i == nt - 1)
        def _():
            dsm_ref[pl.ds(ROW_DLAM, 1), :] = dsm_ref[pl.ds(ROW_DLAM, 1), :] * dc8

    full = lambda shape: pl.BlockSpec(shape, lambda i: (0,) * len(shape))
    rev = lambda i: (nt - 1 - i, 0)
    halo = lambda i: (jnp.maximum((nt - 1 - i) * hb - 1, 0), 0)
    ext = pltpu.VMEM((tc + SUBLANES, c), F32)
    blk = pltpu.VMEM((tc, c), F32)
    body, more_specs, more = _behind(body, 13, after)
    return pl.pallas_call(
        body, name=name, grid=(nt,),
        in_specs=[pl.BlockSpec((tc, 6 * c), rev), pl.BlockSpec((SUBLANES, 6 * c), halo),
                  pl.BlockSpec((tc, c), rev), pl.BlockSpec((SUBLANES, c), halo),
                  pl.BlockSpec((tc, 2 * c), rev),
                  full(wa.shape), full(ba.shape), full(wr.shape), full(br.shape),
                  full(wi.shape), full(bi.shape), full(lam.shape), full(wb.shape)] + more_specs,
        out_specs=[pl.BlockSpec((tc, 6 * c), rev), full((SMALL_ROWS, c)), full(wr.shape), full(wi.shape)],
        out_shape=[jax.ShapeDtypeStruct((t, 6 * c), BF16), jax.ShapeDtypeStruct((SMALL_ROWS, c), F32),
                   jax.ShapeDtypeStruct(wr.shape, F32), jax.ShapeDtypeStruct(wi.shape, F32)],
        scratch_shapes=[ext] * 7 + [blk] * 6,
        compiler_params=_params(("arbitrary",)),
    )(u, u, hs, hs, dy, wa, ba, wr, br, wi, bi, lam, wb, *more)


def _behind(body, n_in, after):
    if after is None:
        return body, [], []
    return (lambda *refs: body(*refs[:n_in], *refs[n_in + 1:])), [ANY], [after]


def _out_proj(h, y, w, name, after=None):
    t, d = h.shape
    dm = y.shape[1]
    tm = _row_tile(t)
    tn = _col_tile(d, (1024, 512, 256))

    def body(h_ref, y_ref, w_ref, o_ref):
        o_ref[...] = h_ref[...] + jnp.dot(y_ref[...], w_ref[...], preferred_element_type=F32)

    body, more_specs, more = _behind(body, 3, after)
    return pl.pallas_call(
        body, name=name, grid=(d // tn, t // tm),
        in_specs=[pl.BlockSpec((tm, tn), lambda n, i: (i, n)),
                  pl.BlockSpec((tm, dm), lambda n, i: (i, 0)),
                  pl.BlockSpec((dm, tn), lambda n, i: (0, n))] + more_specs,
        out_specs=pl.BlockSpec((tm, tn), lambda n, i: (i, n)),
        out_shape=jax.ShapeDtypeStruct((t, d), F32),
        compiler_params=_params(("arbitrary", "arbitrary")),
    )(h, y, w, *more)


def _out_proj_dy(dout, w, name, after=None):
    t, d = dout.shape
    dm = w.shape[0]
    tm = _row_tile(t)
    tn = _col_tile(dm, (1024, 512, 256))

    def body(g_ref, w_ref, o_ref):
        o_ref[...] = lax.dot_general(g_ref[...].astype(BF16), w_ref[...], NT_DIMS, preferred_element_type=F32)

    body, more_specs, more = _behind(body, 2, after)
    return pl.pallas_call(
        body, name=name, grid=(dm // tn, t // tm),
        in_specs=[pl.BlockSpec((tm, d), lambda n, i: (i, 0)),
                  pl.BlockSpec((tn, d), lambda n, i: (n, 0))] + more_specs,
        out_specs=pl.BlockSpec((tm, tn), lambda n, i: (i, n)),
        out_shape=jax.ShapeDtypeStruct((t, dm), F32),
        compiler_params=_params(("arbitrary", "arbitrary")),
    )(dout, w, *more)


def _out_proj_dw(y, dout, name):
    t, dm = y.shape
    d = dout.shape[1]
    tmm = _col_tile(dm, (512, 256))
    tn = _col_tile(d, (512, 256))

    def body(y_ref, g_ref, o_ref):
        o_ref[...] = lax.dot_general(y_ref[...], g_ref[...].astype(BF16), TN_DIMS, preferred_element_type=F32)

    return pl.pallas_call(
        body, name=name, grid=(d // tn, dm // tmm),
        in_specs=[pl.BlockSpec((t, tmm), lambda n, m: (0, m)),
                  pl.BlockSpec((t, tn), lambda n, m: (0, n))],
        out_specs=pl.BlockSpec((tmm, tn), lambda n, m: (m, n)),
        out_shape=jax.ShapeDtypeStruct((dm, d), F32),
        compiler_params=_params(("arbitrary", "arbitrary")),
    )(y, dout)


def _in_proj_bwd(du, wg, h, g, dout, name, after=None):
    t, d = h.shape
    s, _, ns = wg.shape
    tm = _row_tile(t)
    tn = _col_tile(d, (512, 256))

    def mm_body(du_ref, w_ref, o_ref):
        total = lax.dot_general(du_ref[:, 0:ns], w_ref[0], NT_DIMS, preferred_element_type=F32)
        for a in range(1, s):
            total = total + lax.dot_general(du_ref[:, a * ns:(a + 1) * ns], w_ref[a], NT_DIMS,
                                            preferred_element_type=F32)
        o_ref[...] = total

    mm_body, more_specs, more = _behind(mm_body, 2, after)
    dhn = pl.pallas_call(
        mm_body, name=name, grid=(t // tm, d // tn),
        in_specs=[pl.BlockSpec((tm, s * ns), lambda i, n: (i, 0)),
                  pl.BlockSpec((s, tn, ns), lambda i, n: (0, n, 0))] + more_specs,
        out_specs=pl.BlockSpec((tm, tn), lambda i, n: (i, n)),
        out_shape=jax.ShapeDtypeStruct((t, d), F32),
        compiler_params=_params(("arbitrary", "arbitrary")),
    )(du, wg, *more)

    tr = 352 if t % 352 == 0 else 192

    def norm_body(dhn_ref, h_ref, g_ref, dout_ref, dh_ref, dg_ref):
        @pl.when(pl.program_id(0) == 0)
        def _():
            dg_ref[...] = jnp.zeros_like(dg_ref)

        x = h_ref[...]
        dn = dhn_ref[...]
        r = lax.rsqrt(jnp.mean(x * x, axis=-1, keepdims=True) + RMS_EPS)
        gd = dn * g_ref[...]
        dot = jnp.mean(gd * x, axis=-1, keepdims=True)
        dh_ref[...] = dout_ref[...] + (r * gd - x * ((r * r * r) * dot))
        dg_ref[...] += jnp.sum(dn * (x * r), axis=0, keepdims=True)

    rows = pl.BlockSpec((tr, d), lambda i: (i, 0))
    one = pl.BlockSpec((1, d), lambda i: (0, 0))
    return pl.pallas_call(
        norm_body, name=name + "_norm", grid=(t // tr,),
        in_specs=[rows, rows, one, rows], out_specs=[rows, one],
        out_shape=[jax.ShapeDtypeStruct((t, d), F32), jax.ShapeDtypeStruct((1, d), F32)],
        compiler_params=_params(("arbitrary",)),
    )(dhn, h, g, dout)


def _in_proj_dw(hn, du, s, name, after=None):
    t, d = hn.shape
    ns = du.shape[1] // s
    tmm = _col_tile(d, (512, 256))
    tn = _col_tile(ns, (768, 384, 128))
    nb = ns // tn

    def body(hn_ref, du_ref, o_ref):
        o_ref[...] = lax.dot_general(hn_ref[...], du_ref[...], TN_DIMS, preferred_element_type=F32)

    body, more_specs, more = _behind(body, 2, after)
    return pl.pallas_call(
        body, name=name, grid=(s * nb, d // tmm),
        in_specs=[pl.BlockSpec((t, tmm), lambda n, m: (0, m)),
                  pl.BlockSpec((t, tn), lambda n, m: (0, n))] + more_specs,
        out_specs=pl.BlockSpec((None, tmm, tn), lambda n, m: (n // nb, m, n % nb)),
        out_shape=jax.ShapeDtypeStruct((s, d, ns), F32),
        compiler_params=_params(("arbitrary", "arbitrary")),
    )(hn, du, *more)


def _loss_head(h, tgt, g, n_meta, t_real, name):
    t, d = h.shape
    tm = _row_tile(t)

    def body(h_ref, t_ref, g_ref, dh_ref, loss_ref, dg_ref):
        i = pl.program_id(0)

        @pl.when(i == 0)
        def _():
            loss_ref[...] = jnp.zeros_like(loss_ref)
            dg_ref[...] = jnp.zeros_like(dg_ref)

        x = h_ref[...]
        gv = g_ref[...]
        r = lax.rsqrt(jnp.mean(x * x, axis=-1, keepdims=True) + RMS_EPS)
        xr = x * r
        rows = i * tm + lax.broadcasted_iota(jnp.int32, (tm, 1), 0)
        valid = (rows >= n_meta) & (rows < t_real)
        err = jnp.where(valid, xr * gv - t_ref[...], 0.0)
        loss_ref[...] += 0.5 * jnp.sum(jnp.mean(err * err, axis=-1, keepdims=True))
        dy = err * (1.0 / d)
        gd = dy * gv
        dot = jnp.mean(gd * x, axis=-1, keepdims=True)
        dh_ref[...] = r * gd - x * ((r * r * r) * dot)
        dg_ref[...] += jnp.sum(dy * xr, axis=0, keepdims=True)

    return pl.pallas_call(
        body, name=name, grid=(t // tm,),
        in_specs=[pl.BlockSpec((tm, d), lambda i: (i, 0)),
                  pl.BlockSpec((tm, d), lambda i: (i, 0)),
                  pl.BlockSpec((1, d), lambda i: (0, 0))],
        out_specs=[pl.BlockSpec((tm, d), lambda i: (i, 0)),
                   pl.BlockSpec((1, LANES), lambda i: (0, 0)),
                   pl.BlockSpec((1, d), lambda i: (0, 0))],
        out_shape=[jax.ShapeDtypeStruct((t, d), F32), jax.ShapeDtypeStruct((1, LANES), F32),
                   jax.ShapeDtypeStruct((1, d), F32)],
        compiler_params=_params(("arbitrary",)),
    )(h, tgt, g)


def _adamw(w, g, m, v, name):
    rows, cols = w.shape
    tr = rows
    for cand in (512, 256, 128, 64, 32, 16, 8):
        if rows % cand == 0 and cand * cols * 4 <= 2 * 1024 * 1024:
            tr = cand
            break

    def body(w_ref, g_ref, m_ref, v_ref, d_ref, nm_ref, nv_ref):
        gv = g_ref[...]
        m2 = ADAM_B1 * m_ref[...] + (1.0 - ADAM_B1) * gv
        v2 = ADAM_B2 * v_ref[...] + (1.0 - ADAM_B2) * (gv * gv)
        m_hat = m2 / (1.0 - ADAM_B1 ** ADAM_STEP)
        v_hat = v2 / (1.0 - ADAM_B2 ** ADAM_STEP)
        d_ref[...] = -ADAM_LR * (m_hat / (jnp.sqrt(v_hat) + ADAM_EPS) + ADAM_WD * w_ref[...])
        nm_ref[...] = m2
        nv_ref[...] = v2

    spec = pl.BlockSpec((tr, cols), lambda i: (i, 0))
    return pl.pallas_call(
        body, name=name, grid=(rows // tr,),
        in_specs=[spec] * 4, out_specs=[spec] * 3,
        out_shape=[jax.ShapeDtypeStruct((rows, cols), F32)] * 3,
        compiler_params=_params(("arbitrary",)),
    )(w, g, m, v)


def _pair_add(x, ra, c_idx, name):
    s, _, rows, cols = x.shape
    tr = _slab_rows(rows, cols)

    def body(c_ref, x_ref, r_ref, o_ref):
        o_ref[...] = (x_ref[...] + r_ref[...]).astype(BF16)

    return pl.pallas_call(
        body, name=name,
        grid_spec=pltpu.PrefetchScalarGridSpec(
            num_scalar_prefetch=1, grid=(s, rows // tr),
            in_specs=[pl.BlockSpec((None, None, tr, cols), lambda a, i, c_ref: (a, c_ref[0], i, 0)),
                      pl.BlockSpec((None, tr, cols), lambda a, i, c_ref: (a, i, 0))],
            out_specs=pl.BlockSpec((None, tr, cols), lambda a, i, c_ref: (a, i, 0))),
        out_shape=jax.ShapeDtypeStruct((s, rows, cols), BF16),
        compiler_params=_params(("arbitrary", "arbitrary")),
    )(c_idx, x, ra)


def _chip_sum(rc, p, where, n_slots, name):
    s, rows, cols = rc.shape
    tr = _slab_rows(rows, cols)

    def body(w_ref, x_ref, p_ref, o_ref):
        me = w_ref[0]
        total = jnp.where(me == 0, p_ref[...], x_ref[0]).astype(F32)
        for a in range(1, s):
            total = total + jnp.where(me == a, p_ref[...], x_ref[a]).astype(F32)
        o_ref[...] = total

    return pl.pallas_call(
        body, name=name,
        grid_spec=pltpu.PrefetchScalarGridSpec(
            num_scalar_prefetch=1, grid=(rows // tr,),
            in_specs=[pl.BlockSpec((s, tr, cols), lambda i, w_ref: (0, i, 0)),
                      pl.BlockSpec((None, tr, cols), lambda i, w_ref: (w_ref[0], i, 0))],
            out_specs=pl.BlockSpec((None, tr, cols), lambda i, w_ref: (w_ref[1], i, 0))),
        out_shape=jax.ShapeDtypeStruct((n_slots, rows, cols), F32),
        compiler_params=_params(("arbitrary",)),
    )(where, rc, p)


def _cast_place(w, layer, me_idx, name, after=None):
    _, rows, cols = w.shape
    tr = _slab_rows(rows, cols)

    def body(m_ref, w_ref, o_ref):
        o_ref[...] = w_ref[...].astype(BF16)

    body, more_specs, more = _behind(body, 2, after)
    return pl.pallas_call(
        body, name=name,
        grid_spec=pltpu.PrefetchScalarGridSpec(
            num_scalar_prefetch=1, grid=(rows // tr,),
            in_specs=[pl.BlockSpec((None, tr, cols), lambda i, m_ref: (layer, i, 0))] + more_specs,
            out_specs=pl.BlockSpec((None, tr, cols), lambda i, m_ref: (m_ref[0], i, 0))),
        out_shape=jax.ShapeDtypeStruct((N_CHIPS, rows, cols), BF16),
        compiler_params=_params(("arbitrary",)),
    )(me_idx, w, *more)


def _place():
    x, y, c = lax.axis_index("x"), lax.axis_index("y"), lax.axis_index("c")
    chips = [(1 - x, y), (x, 1 - y), (1 - x, 1 - y)]
    return x, y, c, chips


def _chip_index(cx, cy):
    return 2 * cx + cy


def _gather_copies(bufs, stage):
    x, y, c, chips = _place()
    me = _chip_index(x, y)
    copies = []
    for b in bufs:
        for chip in chips:
            src = _chip_index(*chip)
            if stage == 0:
                copies.append((b.at[me, c], (*chip, c), b.at[src, c]))
            else:
                copies.append((b.at[src, c], (x, y, 1 - c), b.at[src, 1 - c]))
    return copies


def _remote(ref, peer, ssem, rsem, k):
    return pltpu.make_async_remote_copy(src_ref=ref, dst_ref=ref, send_sem=ssem.at[k], recv_sem=rsem.at[k],
                                        device_id=peer, device_id_type=MESH)


def _gather_first(bufs, small):
    n = len(bufs)
    k = 3 * n

    def body(*refs):
        sm_ref = refs[n]
        b_refs, smg_ref = refs[n + 1:2 * n + 1], refs[2 * n + 1]
        lsem, ssem, rsem = refs[2 * n + 2:]
        x, y, c, chips = _place()
        me = _chip_index(x, y)
        local = pltpu.make_async_copy(sm_ref, smg_ref.at[me], lsem)
        local.start()
        first = _gather_copies(b_refs, 0)
        second = _gather_copies(b_refs, 1)
        started = []
        for i, (ref, peer, _) in enumerate(first):
            started.append(_remote(ref, peer, ssem, rsem, i))
        for j, chip in enumerate(chips):
            started.append(pltpu.make_async_remote_copy(
                src_ref=sm_ref, dst_ref=smg_ref.at[me], send_sem=ssem.at[2 * k + j], recv_sem=rsem.at[2 * k + j],
                device_id=(*chip, c), device_id_type=MESH))
        for cp in started:
            cp.start()
        for i, (_, peer, lands) in enumerate(first):
            _remote(lands, peer, ssem, rsem, i).wait_recv()
            ref, sib, _ = second[i]
            fwd = _remote(ref, sib, ssem, rsem, k + i)
            fwd.start()
            started.append(fwd)
        for i, (_, sib, lands) in enumerate(second):
            _remote(lands, sib, ssem, rsem, k + i).wait_recv()
        for j, chip in enumerate(chips):
            theirs = smg_ref.at[_chip_index(*chip)]
            pltpu.make_async_remote_copy(src_ref=theirs, dst_ref=theirs, send_sem=ssem.at[2 * k + j],
                                         recv_sem=rsem.at[2 * k + j], device_id=(*chip, c),
                                         device_id_type=MESH).wait_recv()
        for cp in started:
            cp.wait_send()
        local.wait()

    return pl.pallas_call(
        body, name="gather_first",
        in_specs=[ANY] * (n + 1), out_specs=[ANY] * (n + 1),
        out_shape=[jax.ShapeDtypeStruct(b.shape, b.dtype) for b in bufs]
        + [jax.ShapeDtypeStruct((N_CHIPS,) + small.shape, small.dtype)],
        input_output_aliases={i: i for i in range(n)},
        scratch_shapes=[pltpu.SemaphoreType.DMA, pltpu.SemaphoreType.DMA((2 * k + 3,)),
                        pltpu.SemaphoreType.DMA((2 * k + 3,))],
    )(*bufs, small)


HBM = pl.BlockSpec(memory_space=pltpu.HBM)
SEM = pl.BlockSpec(memory_space=pltpu.SEMAPHORE)
DATAFLOW = pltpu.SideEffectType.DATAFLOW_SIDE_EFFECTING


def _copies_start(bufs, plan, n_copies, name, after=None):
    n = len(bufs)
    extra = [] if after is None else [after]

    def body(*refs):
        refs = refs[:n] + refs[n + len(extra):]
        ssem, rsem = refs[n], refs[n + 1]
        b_refs, token = refs[n + 2:2 * n + 2], refs[2 * n + 2]
        copies = plan(b_refs)
        assert len(copies) == n_copies
        for i, (src, dst, peer, _) in enumerate(copies):
            pltpu.make_async_remote_copy(src_ref=src, dst_ref=dst, send_sem=ssem.at[i], recv_sem=rsem.at[i],
                                         device_id=peer, device_id_type=MESH).start()
        token[...] = jnp.zeros_like(token)

    return pl.pallas_call(
        body, name=name,
        out_shape=(pltpu.SemaphoreType.DMA((n_copies,)), pltpu.SemaphoreType.DMA((n_copies,)),
                   *[pltpu.HBM(b.shape, b.dtype) for b in bufs], jax.ShapeDtypeStruct((SUBLANES, LANES), F32)),
        in_specs=[HBM] * n + [ANY] * len(extra),
        out_specs=(SEM, SEM, *[HBM] * n, pl.BlockSpec(memory_space=pltpu.VMEM)),
        input_output_aliases={i: 2 + i for i in range(n)},
        compiler_params=pltpu.CompilerParams(has_side_effects=DATAFLOW),
    )(*[pltpu.with_memory_space_constraint(b, pltpu.HBM) for b in bufs], *extra)


def _copies_wait(bufs, ssem, rsem, after, plan, name):
    n = len(bufs)
    afters = list(after) if isinstance(after, (list, tuple)) else [after]

    def body(*refs):
        b_refs, ssem_ref, rsem_ref = refs[:n], refs[n], refs[n + 1]
        for i, (src, dst, peer, lands) in enumerate(plan(b_refs)):
            pltpu.make_async_remote_copy(src_ref=src, dst_ref=dst, send_sem=ssem_ref.at[i], recv_sem=rsem_ref.at[i],
                                         device_id=peer, device_id_type=MESH).wait_send()
            pltpu.make_async_remote_copy(src_ref=lands, dst_ref=lands, send_sem=ssem_ref.at[i],
                                         recv_sem=rsem_ref.at[i], device_id=peer, device_id_type=MESH).wait_recv()

    return pl.pallas_call(
        body, name=name,
        out_shape=tuple(pltpu.HBM(b.shape, b.dtype) for b in bufs),
        in_specs=[HBM] * n + [SEM, SEM] + [ANY] * len(afters), out_specs=tuple([HBM] * n),
        input_output_aliases={i: i for i in range(n)},
        compiler_params=pltpu.CompilerParams(has_side_effects=DATAFLOW),
    )(*bufs, ssem, rsem, *afters)


def _gather_plan(stage):
    return lambda refs: [(ref, ref, peer, lands) for ref, peer, lands in _gather_copies(refs, stage)]


def _swap_plan(refs):
    n = len(refs) // 2
    x, y, c, _ = _place()
    return [(refs[a].at[:, 1 - c], refs[n + a], (x, y, 1 - c), refs[n + a]) for a in range(n)]


def _scatter_plan(refs):
    n = len(refs) // 2
    x, y, c, chips = _place()
    me = _chip_index(x, y)
    return [(refs[a].at[_chip_index(*chip)], refs[n + a].at[me], (*chip, c), refs[n + a].at[_chip_index(*chip)])
            for a in range(n) for chip in chips]


def _pair_gather_plan(refs):
    x, y, c, _ = _place()
    return [(r.at[c], r.at[c], (x, y, 1 - c), r.at[1 - c]) for r in refs]


def _pair_swap(xs, name):
    n = len(xs)

    def body(*refs):
        x_refs, o_refs, ssem, rsem = refs[:n], refs[n:2 * n], refs[2 * n], refs[2 * n + 1]
        x, y, c, _ = _place()
        copies = [pltpu.make_async_remote_copy(src_ref=x_refs[a].at[:, 1 - c], dst_ref=o_refs[a],
                                               send_sem=ssem.at[a], recv_sem=rsem.at[a],
                                               device_id=(x, y, 1 - c), device_id_type=MESH) for a in range(n)]
        for cp in copies:
            cp.start()
        for cp in copies:
            cp.wait()

    return pl.pallas_call(
        body, name=name, in_specs=[ANY] * n, out_specs=[ANY] * n,
        out_shape=[jax.ShapeDtypeStruct((a.shape[0],) + a.shape[2:], a.dtype) for a in xs],
        scratch_shapes=[pltpu.SemaphoreType.DMA((n,)), pltpu.SemaphoreType.DMA((n,))],
    )(*xs)


def _chip_scatter(ps):
    n = len(ps)

    def body(*refs):
        p_refs, o_refs, ssem, rsem = refs[:n], refs[n:2 * n], refs[2 * n], refs[2 * n + 1]
        x, y, c, chips = _place()
        me = _chip_index(x, y)
        sends = []
        for a in range(n):
            for j, chip in enumerate(chips):
                sends.append(pltpu.make_async_remote_copy(
                    src_ref=p_refs[a].at[_chip_index(*chip)], dst_ref=o_refs[a].at[me],
                    send_sem=ssem.at[3 * a + j], recv_sem=rsem.at[3 * a + j],
                    device_id=(*chip, c), device_id_type=MESH))
        for cp in sends:
            cp.start()
        for a in range(n):
            for j, chip in enumerate(chips):
                src = _chip_index(*chip)
                pltpu.make_async_remote_copy(
                    src_ref=p_refs[a].at[src], dst_ref=o_refs[a].at[src],
                    send_sem=ssem.at[3 * a + j], recv_sem=rsem.at[3 * a + j],
                    device_id=(*chip, c), device_id_type=MESH).wait_recv()
        for cp in sends:
            cp.wait_send()

    return pl.pallas_call(
        body, name="chip_scatter", in_specs=[ANY] * n, out_specs=[ANY] * n,
        out_shape=[jax.ShapeDtypeStruct(a.shape, a.dtype) for a in ps],
        scratch_shapes=[pltpu.SemaphoreType.DMA((3 * n,)), pltpu.SemaphoreType.DMA((3 * n,))],
    )(*ps)


def _final_gather(fs, rep):
    n = len(fs)

    def body(*refs):
        o_refs, repo_ref = refs[n + 1:2 * n + 1], refs[2 * n + 1]
        ssem, rsem = refs[2 * n + 2:]
        x, y, c, chips = _place()
        slot = 4 * x + 2 * y + c
        copies = [pltpu.make_async_remote_copy(src_ref=o_refs[a].at[c], dst_ref=o_refs[a].at[c],
                                               send_sem=ssem.at[a], recv_sem=rsem.at[a],
                                               device_id=(x, y, 1 - c), device_id_type=MESH) for a in range(n)]
        peers = [(x, y, 1 - c)] + [(*chip, c) for chip in chips] + [(*chip, 1 - c) for chip in chips]
        for k, peer in enumerate(peers):
            copies.append(pltpu.make_async_remote_copy(src_ref=repo_ref.at[slot], dst_ref=repo_ref.at[slot],
                                                       send_sem=ssem.at[n + k], recv_sem=rsem.at[n + k],
                                                       device_id=peer, device_id_type=MESH))
        for cp in copies:
            cp.start()
        for a in range(n):
            pltpu.make_async_remote_copy(src_ref=o_refs[a].at[1 - c], dst_ref=o_refs[a].at[1 - c],
                                         send_sem=ssem.at[a], recv_sem=rsem.at[a],
                                         device_id=(x, y, 1 - c), device_id_type=MESH).wait_recv()
        for k, peer in enumerate(peers):
            px, py, pc = peer
            theirs = repo_ref.at[4 * px + 2 * py + pc]
            pltpu.make_async_remote_copy(src_ref=theirs, dst_ref=theirs, send_sem=ssem.at[n + k], recv_sem=rsem.at[n + k],
                                         device_id=peer, device_id_type=MESH).wait_recv()
        for cp in copies:
            cp.wait_send()

    return pl.pallas_call(
        body, name="final_gather", in_specs=[ANY] * (n + 1), out_specs=[ANY] * (n + 1),
        out_shape=[jax.ShapeDtypeStruct(a.shape, a.dtype) for a in fs] + [jax.ShapeDtypeStruct(rep.shape, rep.dtype)],
        input_output_aliases={k: k for k in range(n + 1)},
        scratch_shapes=[pltpu.SemaphoreType.DMA((n + 7,)), pltpu.SemaphoreType.DMA((n + 7,))],
    )(*fs, rep)


def _block_diag(w, gb):
    nh, hd, _ = w.shape
    per = gb // hd
    w4 = w.reshape(nh // per, per, hd, hd)
    eye = jnp.eye(per, dtype=w.dtype)
    return jnp.einsum("jaik,ab->jaibk", w4, eye).reshape(nh // per, gb, gb)


def _diag_blocks(dense, hd):
    nj, gb, _ = dense.shape
    per = gb // hd
    d5 = dense.reshape(nj, per, hd, per, hd)
    return jnp.stack([d5[:, a, :, a, :] for a in range(per)], axis=1).reshape(nj * per, hd, hd)


def _round_up(n, q):
    return (n + q - 1) // q * q


def kernel(x, meta, norm_g, w_in, conv_a_w, conv_a_b, lru_wr, lru_br, lru_wi, lru_bi, lru_lambda, conv_b_w, w_out, final_g, loss_target, m_meta, m_norm_g, m_w_in, m_conv_a_w, m_conv_a_b, m_lru_wr, m_lru_br, m_lru_wi, m_lru_bi, m_lru_lambda, m_conv_b_w, m_w_out, m_final_g, v_meta, v_norm_g, v_w_in, v_conv_a_w, v_conv_a_b, v_lru_wr, v_lru_br, v_lru_wi, v_lru_bi, v_lru_lambda, v_conv_b_w, v_w_out, v_final_g):
    weights = dict(meta=meta, norm_g=norm_g, w_in=w_in, conv_a_w=conv_a_w, conv_a_b=conv_a_b, lru_wr=lru_wr,
                   lru_br=lru_br, lru_wi=lru_wi, lru_bi=lru_bi, lru_lambda=lru_lambda, conv_b_w=conv_b_w,
                   w_out=w_out, final_g=final_g)
    mom1 = dict(meta=m_meta, norm_g=m_norm_g, w_in=m_w_in, conv_a_w=m_conv_a_w, conv_a_b=m_conv_a_b,
                lru_wr=m_lru_wr, lru_br=m_lru_br, lru_wi=m_lru_wi, lru_bi=m_lru_bi, lru_lambda=m_lru_lambda,
                conv_b_w=m_conv_b_w, w_out=m_w_out, final_g=m_final_g)
    mom2 = dict(meta=v_meta, norm_g=v_norm_g, w_in=v_w_in, conv_a_w=v_conv_a_w, conv_a_b=v_conv_a_b,
                lru_wr=v_lru_wr, lru_br=v_lru_br, lru_wi=v_lru_wi, lru_bi=v_lru_bi, lru_lambda=v_lru_lambda,
                conv_b_w=v_conv_b_w, w_out=v_w_out, final_g=v_final_g)
    names = list(weights)

    assert x.shape[0] == 1
    seq, d = x.shape[1], x.shape[2]
    n_meta, ds = meta.shape
    depth = norm_g.shape[0]
    c = lru_lambda.shape[1]
    nh, hd = lru_wr.shape[1], lru_wr.shape[2]
    ns = w_in.shape[2]
    dms = w_out.shape[1]
    cs = conv_a_w.shape[2]
    ka, kb = conv_a_w.shape[1], conv_b_w.shape[1]
    s = N_CHIPS
    assert depth == N_CORES and d == s * ds and c == s * cs and s * ns == 6 * c and s * dms == 2 * c
    gb = min(GATE_BLOCK, c)
    t_real = n_meta + seq
    t = _round_up(t_real, ROW_QUANTUM)
    my_c = lax.axis_index("c").astype(jnp.int32)
    my_chip = (2 * lax.axis_index("x") + lax.axis_index("y")).astype(jnp.int32)
    c_idx = my_c.reshape(1)
    chip_idx = my_chip.reshape(1)

    sm_rows = _round_up(n_meta + depth * SUBLANES, 2 * SUBLANES)
    small = jnp.zeros((sm_rows, ds), F32)
    small = small.at[0:n_meta, :].set(meta)
    for l in range(depth):
        base = n_meta + l * SUBLANES
        small = small.at[base:base + ka, 0:cs].set(conv_a_w[l])
        small = small.at[base + ka:base + ka + kb, 0:cs].set(conv_b_w[l])
    (small_g,) = _gather_first([], small)
    meta_full = jnp.transpose(small_g[:, 0:n_meta, :], (1, 0, 2)).reshape(n_meta, d)
    wa_full, wb_full = [], []
    for l in range(depth):
        base = n_meta + l * SUBLANES
        wa_full.append(jnp.transpose(small_g[:, base:base + ka, 0:cs], (1, 0, 2)).reshape(ka, c))
        wb_full.append(jnp.transpose(small_g[:, base + ka:base + ka + kb, 0:cs], (1, 0, 2)).reshape(kb, c))
    win0 = _cast_place(w_in, 0, chip_idx, "cast_w_in_0").reshape(s, 2, d // 2, ns)
    ssem_w, rsem_w, win0, token_w = _copies_start([win0], _gather_plan(0), 3, "gather_win0_ici_start", after=small_g)
    win_b = [None] + [_cast_place(w_in, l, chip_idx, f"cast_w_in_{l}", after=token_w).reshape(s, 2, d // 2, ns)
                      for l in range(1, depth)]
    wout_b = [_cast_place(w_out, l, chip_idx, f"cast_w_out_{l}", after=token_w).reshape(s, 2, dms // 2, d)
              for l in range(depth)]
    h = jnp.concatenate([meta_full, x[0], jnp.zeros((t - t_real, d), F32)], axis=0) + token_w[0, 0]
    tgt = jnp.concatenate([jnp.zeros((n_meta, d), F32), loss_target[0], jnp.zeros((t - t_real, d), F32)],
                          axis=0) + token_w[0, 0]
    (win0,) = _copies_wait([win0], ssem_w, rsem_w, [h, tgt] + win_b[1:] + wout_b, _gather_plan(0),
                           "gather_win0_ici_wait")
    ssem_w, rsem_w, win0, token_w = _copies_start([win0], _gather_plan(1), 3, "gather_win0_d2d_start")
    ssem_o, rsem_o, wout0, token_o = _copies_start([wout_b[0]], _gather_plan(0), 3, "gather_wout0_ici_start",
                                                   after=token_w)
    later = [win_b[1], wout_b[1]]
    ssem, rsem, *later, token = _copies_start(later, _gather_plan(0), 3 * len(later), "gather_next_ici_start",
                                              after=token_o)
    (win_b[0],) = _copies_wait([win0], ssem_w, rsem_w, token, _gather_plan(1), "gather_win0_d2d_wait")

    layer_w = []
    for l in range(depth):
        layer_w.append(dict(
            g=norm_g[l].reshape(1, d), wa=wa_full[l], ba=conv_a_b[l].reshape(1, c),
            wr=_block_diag(lru_wr[l], gb).astype(BF16), br=lru_br[l].reshape(1, c),
            wi=_block_diag(lru_wi[l], gb).astype(BF16), bi=lru_bi[l].reshape(1, c),
            lam=lru_lambda[l].reshape(1, c), wb=wb_full[l]))
    saved = []
    for l, lw in enumerate(layer_w):
        first = l == 0
        lw["win"] = win_b[l].reshape(s, d, ns)
        u, hn = _norm_in(h, lw["g"] + token[0, 0] if first else lw["g"], lw["win"], f"norm_in_{l}")
        if first:
            (wout0,) = _copies_wait([wout0], ssem_o, rsem_o, u, _gather_plan(0), "gather_wout0_ici_wait")
            ssem_o, rsem_o, wout0, token_o = _copies_start([wout0], _gather_plan(1), 3, "gather_wout0_d2d_start")
        y, hs = _mix_fwd(u, lw["wa"], lw["ba"] + token_o[0, 0] if first else lw["ba"], lw["wr"], lw["br"], lw["wi"],
                         lw["bi"], lw["lam"], lw["wb"], f"mix_fwd_{l}")
        token = None
        if first:
            (wout_b[0],) = _copies_wait([wout0], ssem_o, rsem_o, y, _gather_plan(1), "gather_wout0_d2d_wait")
            later = _copies_wait(later, ssem, rsem, y, _gather_plan(0), "gather_next_ici_wait")
            ssem, rsem, *later, token = _copies_start(later, _gather_plan(1), 3 * len(later), "gather_next_d2d_start")
        lw["wout"] = wout_b[l].reshape(2 * c, d)
        saved.append((h, u, hn, y, hs))
        h = _out_proj(h, y, lw["wout"], f"out_proj_{l}", after=token)
        if first:
            win_b[1], wout_b[1] = _copies_wait(later, ssem, rsem, h, _gather_plan(1), "gather_next_d2d_wait")
    dh, loss_lanes, d_final_g = _loss_head(h, tgt, final_g.reshape(1, d), n_meta, t_real, "loss_head")
    loss = lax.psum(loss_lanes[0, 0], ("x", "y", "c"))

    to_core = jnp.stack([my_chip, my_c])
    grads = [None] * depth
    early = None
    for l in reversed(range(depth)):
        lw = layer_w[l]
        h_in, u, hn, y, hs = saved[l]
        token = early[-1] if early else None
        dy = _out_proj_dy(dh, lw["wout"], f"out_proj_dy_{l}", after=token)
        d_wout = _out_proj_dw(y, dh, f"out_proj_dw_{l}")
        if early:
            ssem, rsem, bufs, _ = early
            bufs = _copies_wait(bufs, ssem, rsem, d_wout, _swap_plan, "early_swap_wait")
            half = len(bufs) // 2
            sums = [_pair_add(a, b, c_idx, f"early_pair_add_{k}") for k, (a, b) in enumerate(zip(bufs[:half], bufs[half:]))]
            lands = [lax.empty(p.shape, p.dtype) for p in sums]
            ssem, rsem, *bufs, token = _copies_start(sums + lands, _scatter_plan, 3 * half, "early_scatter_start")
        du, dsm, d_wr, d_wi = _mix_bwd(u, hs, dy, lw["wa"], lw["ba"], lw["wr"], lw["br"], lw["wi"], lw["bi"],
                                       lw["lam"], lw["wb"], f"mix_bwd_{l}", after=token)
        if early:
            bufs = _copies_wait(bufs, ssem, rsem, du, _scatter_plan, "early_scatter_wait")
            halves = [_chip_sum(rc, p, to_core, N_CORES, f"early_chip_sum_{k}")
                      for k, (p, rc) in enumerate(zip(bufs[:half], bufs[half:]))]
            ssem, rsem, *bufs, token = _copies_start(halves, _pair_gather_plan, half, "early_gather_start")
        d_win = _in_proj_dw(hn, du, s, f"in_proj_dw_{l}", after=token)
        srcs = [d_win.reshape(s, 2, d // 2, ns), d_wout.reshape(s, 2, dms // 2, d)]
        if early:
            early_full = _copies_wait(bufs, ssem, rsem, d_win, _pair_gather_plan, "early_gather_wait")
            from_sibling = _pair_swap(srcs, "pair_swap")
            late_sums = [_pair_add(a, b, c_idx, f"pair_add_{k}") for k, (a, b) in enumerate(zip(srcs, from_sibling))]
            lands = [lax.empty(p.shape, p.dtype) for p in late_sums]
            ssem, rsem, *bufs, token = _copies_start(late_sums + lands, _scatter_plan, 3 * len(srcs), "late_scatter_start")
        dh, d_g = _in_proj_bwd(du, lw["win"], h_in, lw["g"], dh, f"in_proj_bwd_{l}", after=token)
        if early:
            bufs = _copies_wait(bufs, ssem, rsem, dh, _scatter_plan, "late_scatter_wait")
            late_reduced = [_chip_sum(rc, p, to_core, N_CORES, f"chip_sum_{k}")
                            for k, (p, rc) in enumerate(zip(bufs[:len(srcs)], bufs[len(srcs):]))]
        grads[l] = dict(dsm=dsm, wr=_diag_blocks(d_wr, hd), wi=_diag_blocks(d_wi, hd), g=d_g)
        if l == depth - 1:
            lands = [lax.empty((a.shape[0],) + a.shape[2:], a.dtype) for a in srcs]
            ssem, rsem, *bufs, token = _copies_start(srcs + lands, _swap_plan, len(srcs), "early_swap_start")
            early = (ssem, rsem, bufs, token)
        else:
            early = None
    grad_x = dh[n_meta:t_real][None]

    sharded = []
    sp = jnp.zeros((sm_rows, s, ds), F32)
    sp = sp.at[0:n_meta].set(dh[0:n_meta].reshape(n_meta, s, ds))
    for l in range(depth):
        base = n_meta + l * SUBLANES
        dsm = grads[l]["dsm"]
        sp = sp.at[base:base + ka, :, 0:cs].set(dsm[ROW_DWA:ROW_DWA + ka].reshape(ka, s, cs))
        sp = sp.at[base + ka:base + ka + kb, :, 0:cs].set(dsm[ROW_DWB:ROW_DWB + kb].reshape(kb, s, cs))
    sharded.append(jnp.transpose(sp, (1, 0, 2)).reshape(s, 2, sm_rows // 2, ds))
    rep_parts = [jnp.concatenate([grads[l]["g"].reshape(-1) for l in range(depth)]), d_final_g.reshape(-1)]
    for row in (ROW_DBA, ROW_DBR, ROW_DBI, ROW_DLAM):
        rep_parts.append(jnp.concatenate([grads[l]["dsm"][row] for l in range(depth)]))
    rep_parts.append(jnp.concatenate([grads[l]["wr"].reshape(-1) for l in range(depth)]))
    rep_parts.append(jnp.concatenate([grads[l]["wi"].reshape(-1) for l in range(depth)]))
    rep_sizes = [p.shape[0] for p in rep_parts]
    piece = _round_up(-(-sum(rep_sizes) // (s * 2)), 2 * SUBLANES * LANES)
    flat = jnp.concatenate(rep_parts + [jnp.zeros((s * 2 * piece - sum(rep_sizes),), F32)])
    sharded.append(flat.reshape(s, 2, piece // LANES, LANES))

    from_sibling = _pair_swap(sharded, "small_pair_swap")
    pair_sums = [_pair_add(a, b, c_idx, f"small_pair_add_{k}") for k, (a, b) in enumerate(zip(sharded, from_sibling))]
    by_chip = _chip_scatter(pair_sums)
    to_device = jnp.stack([my_chip, 2 * my_chip + my_c])
    reduced_sp = _chip_sum(by_chip[0], pair_sums[0], to_core, N_CORES, "small_chip_sum")
    reduced_rep = _chip_sum(by_chip[1], pair_sums[1], to_device, N_CHIPS * N_CORES, "chip_sum_rep")
    *full, rep_all = _final_gather(late_reduced + [reduced_sp], reduced_rep)

    g_win = [full[0].reshape(d, ns), early_full[0].reshape(d, ns)]
    g_wout = [full[1].reshape(dms, d), early_full[1].reshape(dms, d)]
    g_sp = full[2].reshape(sm_rows, ds)
    rep_flat = rep_all.reshape(-1)
    rep_out, off = [], 0
    for n in rep_sizes:
        rep_out.append(rep_flat[off:off + n])
        off += n
    grad = dict(
        meta=g_sp[0:n_meta],
        norm_g=rep_out[0].reshape(depth, d),
        w_in=jnp.stack(g_win),
        conv_a_w=jnp.stack([g_sp[n_meta + l * SUBLANES:n_meta + l * SUBLANES + ka, 0:cs] for l in range(depth)]),
        conv_a_b=rep_out[2].reshape(depth, c),
        lru_wr=rep_out[6].reshape(depth, nh, hd, hd),
        lru_br=rep_out[3].reshape(depth, c),
        lru_wi=rep_out[7].reshape(depth, nh, hd, hd),
        lru_bi=rep_out[4].reshape(depth, c),
        lru_lambda=rep_out[5].reshape(depth, c),
        conv_b_w=jnp.stack([g_sp[n_meta + l * SUBLANES + ka:n_meta + l * SUBLANES + ka + kb, 0:cs]
                            for l in range(depth)]),
        w_out=jnp.stack(g_wout),
        final_g=rep_out[1].reshape(d),
    )

    delta, new_m, new_v = {}, {}, {}
    for n in names:
        shape = weights[n].shape
        two_d = (-1, shape[-1]) if len(shape) > 1 else (1, -1)
        if n in ("lru_wr", "lru_wi"):
            two_d = (-1, LANES)
        out = _adamw(weights[n].reshape(two_d), grad[n].reshape(two_d), mom1[n].reshape(two_d),
                     mom2[n].reshape(two_d), f"adamw_{n}")
        delta[n], new_m[n], new_v[n] = (o.reshape(shape) for o in out)

    return (loss, grad_x, *[grad[n] for n in names], *[delta[n] for n in names],
            *[new_m[n] for n in names], *[new_v[n] for n in names])
```

```python
import functools

import jax
import jax.numpy as jnp
from jax import lax
from jax.experimental import pallas as pl
from jax.experimental.pallas import tpu as pltpu

F32 = jnp.float32
BF16 = jnp.bfloat16

RMS_EPS = 1e-6
LRU_C = 8.0
ADAM_LR = 0.001
ADAM_B1 = 0.9
ADAM_B2 = 0.999
ADAM_EPS = 1e-08
ADAM_WD = 0.01
ADAM_STEP = 10

N_CHIPS = 4
N_CORES = 2
VMEM_LIMIT_BYTES = 56 * 1024 * 1024
SUBLANES = 8
LANES = 128
ROW_QUANTUM = 384
MIX_CHUNK = 192
GATE_BLOCK = 256
MESH = pl.DeviceIdType.MESH
ANY = pl.BlockSpec(memory_space=pl.ANY)

NT_DIMS = (((1,), (1,)), ((), ()))
TN_DIMS = (((0,), (0,)), ((), ()))


def _params(sem):
    return pltpu.CompilerParams(dimension_semantics=sem, vmem_limit_bytes=VMEM_LIMIT_BYTES)


def _sig(x):
    return 0.5 * jnp.tanh(0.5 * x) + 0.5


def _row_tile(t):
    return 704 if t % 704 == 0 else 192


def _col_tile(n, prefs):
    for p in prefs:
        if n % p == 0:
            return p
    return n


def _slab_rows(rows, cols):
    if rows * cols * 4 <= 1024 * 1024:
        return rows
    return _col_tile(rows, (256, 128, 64, 32, 16))


def _norm_in(h, g, wg, name):
    t, d = h.shape
    s, _, ns = wg.shape
    tm = 1408 if t % 1408 == 0 else _row_tile(t)
    tn = _col_tile(ns, (768, 384, 128))
    nb = ns // tn

    def body(h_ref, g_ref, w_ref, u_ref, hn_ref):
        @pl.when(pl.program_id(1) == 0)
        def _():
            x = h_ref[...]
            r = lax.rsqrt(jnp.mean(x * x, axis=-1, keepdims=True) + RMS_EPS)
            hn_ref[...] = ((x * r) * g_ref[...]).astype(BF16)

        u_ref[...] = jnp.dot(hn_ref[...], w_ref[...], preferred_element_type=F32)

    return pl.pallas_call(
        body, name=name, grid=(t // tm, s * nb),
        in_specs=[pl.BlockSpec((tm, d), lambda i, n: (i, 0)),
                  pl.BlockSpec((1, d), lambda i, n: (0, 0)),
                  pl.BlockSpec((None, d, tn), lambda i, n: (n // nb, 0, n % nb))],
        out_specs=[pl.BlockSpec((tm, tn), lambda i, n: (i, n)),
                   pl.BlockSpec((tm, d), lambda i, n: (i, 0))],
        out_shape=[jax.ShapeDtypeStruct((t, s * ns), F32), jax.ShapeDtypeStruct((t, d), BF16)],
        compiler_params=_params(("arbitrary", "arbitrary")),
    )(h, g, wg)


def _norm_in_own(h, g, wg, me_idx, name):
    t, d = h.shape
    s, _, ns = wg.shape
    tm = 1408 if t % 1408 == 0 else _row_tile(t)
    tn = _col_tile(ns, (768, 384, 128))
    nb = ns // tn

    def body(m_ref, h_ref, g_ref, w_ref, u_ref, hn_ref):
        @pl.when(pl.program_id(1) == 0)
        def _():
            x = h_ref[...]
            r = lax.rsqrt(jnp.mean(x * x, axis=-1, keepdims=True) + RMS_EPS)
            hn_ref[...] = ((x * r) * g_ref[...]).astype(BF16)

        u_ref[...] = jnp.dot(hn_ref[...], w_ref[...], preferred_element_type=F32)

    return pl.pallas_call(
        body, name=name,
        grid_spec=pltpu.PrefetchScalarGridSpec(
            num_scalar_prefetch=1, grid=(t // tm, nb),
            in_specs=[pl.BlockSpec((tm, d), lambda i, n, m: (i, 0)),
                      pl.BlockSpec((1, d), lambda i, n, m: (0, 0)),
                      pl.BlockSpec((None, d, tn), lambda i, n, m: (m[0], 0, n))],
            out_specs=[pl.BlockSpec((tm, tn), lambda i, n, m: (i, m[0] * nb + n)),
                       pl.BlockSpec((tm, d), lambda i, n, m: (i, 0))]),
        out_shape=[jax.ShapeDtypeStruct((t, s * ns), F32), jax.ShapeDtypeStruct((t, d), BF16)],
        compiler_params=_params(("arbitrary", "arbitrary")),
    )(me_idx, h, g, wg)


def _norm_in_rest(hn, wg, u, me_idx, name, after=None):
    t, d = hn.shape
    s, _, ns = wg.shape
    tm = 1408 if t % 1408 == 0 else _row_tile(t)
    tn = _col_tile(ns, (768, 384, 128))
    nb = ns // tn

    def body(m_ref, hn_ref, w_ref, u_in, u_ref):
        del u_in
        u_ref[...] = jnp.dot(hn_ref[...], w_ref[...], preferred_element_type=F32)

    def shard(n, m):
        return (m[0] + 1 + n // nb) % s

    body, more_specs, more = _behind(body, 4, after)
    return pl.pallas_call(
        body, name=name,
        grid_spec=pltpu.PrefetchScalarGridSpec(
            num_scalar_prefetch=1, grid=(t // tm, (s - 1) * nb),
            in_specs=[pl.BlockSpec((tm, d), lambda i, n, m: (i, 0)),
                      pl.BlockSpec((None, d, tn), lambda i, n, m: (shard(n, m), 0, n % nb)),
                      ANY] + more_specs,
            out_specs=pl.BlockSpec((tm, tn), lambda i, n, m: (i, shard(n, m) * nb + n % nb))),
        out_shape=jax.ShapeDtypeStruct(u.shape, u.dtype),
        input_output_aliases={3: 0},
        compiler_params=_params(("arbitrary", "arbitrary")),
    )(me_idx, hn, wg, u, *more)


def _decay_consts(lam):
    z = -lam
    e = jnp.exp(-jnp.abs(z))
    u = 1.0 + e
    log1p_e = jnp.where(u == 1.0, e, jnp.log(u) * (e / (u - 1.0)))
    sp = jnp.maximum(z, 0.0) + log1p_e
    return -LRU_C * sp, LRU_C * _sig(z)


def _gates(xc, wr_ref, br_ref, wi_ref, bi_ref, c8, j, gb):
    sl = slice(j * gb, (j + 1) * gb)
    x16 = xc.astype(BF16)
    r = _sig(jnp.dot(x16, wr_ref[j], preferred_element_type=F32) + br_ref[:, sl])
    ig = _sig(jnp.dot(x16, wi_ref[j], preferred_element_type=F32) + bi_ref[:, sl])
    la = c8[:, sl] * r
    a = jnp.exp(la)
    sq = jnp.sqrt(-jnp.tanh(la) * (a * a + 1.0))
    return r, ig, a, sq


def _mix_fwd(u, wa, ba, wr, br, wi, bi, lam, wb, name):
    t = u.shape[0]
    c = u.shape[1] // 6
    tc = MIX_CHUNK
    gb = wr.shape[1]
    nblk = c // gb
    ka, kb = wa.shape[0], wb.shape[0]

    def body(u_ref, wa_ref, ba_ref, wr_ref, br_ref, wi_ref, bi_ref, lam_ref, wb_ref,
             y_ref, hs_ref, xa_ext, v_ext, xc_s, a_s, b_s, carry_s):
        @pl.when(pl.program_id(0) == 0)
        def _():
            xa_ext[0:SUBLANES, :] = jnp.zeros((SUBLANES, c), F32)
            v_ext[0:SUBLANES, :] = jnp.zeros((SUBLANES, c), F32)
            carry_s[...] = jnp.zeros_like(carry_s)

        xa_ext[SUBLANES:SUBLANES + tc, :] = u_ref[:, 0:c]
        xc = ba_ref[...]
        for k in range(ka):
            xc = xc + wa_ref[pl.ds(k, 1), :] * xa_ext[pl.ds(SUBLANES - (ka - 1) + k, tc), :]
        xc_s[...] = xc
        c8, _ = _decay_consts(lam_ref[...])
        for j in range(nblk):
            sl = slice(j * gb, (j + 1) * gb)
            xcj = xc_s[:, sl]
            _, ig, a, sq = _gates(xcj, wr_ref, br_ref, wi_ref, bi_ref, c8, j, gb)
            a_s[:, sl] = a
            b_s[:, sl] = sq * (ig * xcj)

        row = lax.broadcasted_iota(jnp.int32, (SUBLANES, c), 0)

        def scan_step(j, _):
            off = pl.multiple_of(j * SUBLANES, SUBLANES)
            av = a_s[pl.ds(off, SUBLANES), :]
            bv = b_s[pl.ds(off, SUBLANES), :]
            for d in (1, 2, 4):
                keep = row >= d
                bsh = jnp.where(keep, pltpu.roll(bv, d, axis=0), 0.0)
                ash = jnp.where(keep, pltpu.roll(av, d, axis=0), 1.0)
                bv = av * bsh + bv
                av = av * ash
            hv = av * carry_s[...] + bv
            hs_ref[pl.ds(off, SUBLANES), :] = hv
            carry_s[...] = hs_ref[pl.ds(off + SUBLANES - 1, 1), :]
            return 0

        lax.fori_loop(0, tc // SUBLANES, scan_step, 0)

        ga = u_ref[:, c:2 * c]
        y_ref[:, 0:c] = (hs_ref[...] * (ga * _sig(ga))).astype(BF16)

        v_ext[SUBLANES:SUBLANES + tc, :] = u_ref[:, 3 * c:4 * c] * u_ref[:, 4 * c:5 * c]
        cv = wb_ref[pl.ds(0, 1), :] * v_ext[pl.ds(SUBLANES - (kb - 1), tc), :]
        for k in range(1, kb):
            cv = cv + wb_ref[pl.ds(k, 1), :] * v_ext[pl.ds(SUBLANES - (kb - 1) + k, tc), :]
        gbv = u_ref[:, 5 * c:6 * c]
        y_ref[:, c:2 * c] = (u_ref[:, 2 * c:3 * c] * cv * (gbv * _sig(gbv))).astype(BF16)

        xa_ext[0:SUBLANES, :] = xa_ext[tc:tc + SUBLANES, :]
        v_ext[0:SUBLANES, :] = v_ext[tc:tc + SUBLANES, :]

    full = lambda shape: pl.BlockSpec(shape, lambda i: (0,) * len(shape))
    return pl.pallas_call(
        body, name=name, grid=(t // tc,),
        in_specs=[pl.BlockSpec((tc, 6 * c), lambda i: (i, 0)),
                  full(wa.shape), full(ba.shape), full(wr.shape), full(br.shape),
                  full(wi.shape), full(bi.shape), full(lam.shape), full(wb.shape)],
        out_specs=[pl.BlockSpec((tc, 2 * c), lambda i: (i, 0)),
                   pl.BlockSpec((tc, c), lambda i: (i, 0))],
        out_shape=[jax.ShapeDtypeStruct((t, 2 * c), BF16), jax.ShapeDtypeStruct((t, c), F32)],
        scratch_shapes=[pltpu.VMEM((tc + SUBLANES, c), F32), pltpu.VMEM((tc + SUBLANES, c), F32),
                        pltpu.VMEM((tc, c), F32), pltpu.VMEM((tc, c), F32), pltpu.VMEM((tc, c), F32),
                        pltpu.VMEM((1, c), F32)],
        compiler_params=_params(("arbitrary",)),
    )(u, wa, ba, wr, br, wi, bi, lam, wb)


ROW_DWA = 0
ROW_DBA = 4
ROW_DBR = 5
ROW_DBI = 6
ROW_DLAM = 7
ROW_DWB = 8
SMALL_ROWS = 16


def _mix_bwd(u, hs, dy, wa, ba, wr, br, wi, bi, lam, wb, name, after=None):
    t = u.shape[0]
    c = u.shape[1] // 6
    tc = MIX_CHUNK
    nt = t // tc
    gb = wr.shape[1]
    nblk = c // gb
    ka, kb = wa.shape[0], wb.shape[0]
    assert ka <= ROW_DBA and kb <= SMALL_ROWS - ROW_DWB
    hb = tc // SUBLANES

    def body(u_ref, uh_ref, hs_ref, hsh_ref, dy_ref, wa_ref, ba_ref, wr_ref, br_ref, wi_ref, bi_ref, lam_ref, wb_ref,
             du_ref, dsm_ref, dwr_ref, dwi_ref,
             xa_ext, v_ext, hs_ext, a_ext, ds_ext, dxc_ext, dcv_ext, xc_s, r_s, i_s, sq_s, g_s, an_s):
        i = pl.program_id(0)
        chunk = nt - 1 - i
        tail = slice(tc, tc + SUBLANES)
        head = slice(0, SUBLANES)

        @pl.when(i == 0)
        def _():
            zero = jnp.zeros((SUBLANES, c), F32)
            a_ext[tail, :] = zero
            ds_ext[tail, :] = zero
            dxc_ext[tail, :] = zero
            dcv_ext[tail, :] = zero
            dsm_ref[...] = jnp.zeros_like(dsm_ref)
            dwr_ref[...] = jnp.zeros_like(dwr_ref)
            dwi_ref[...] = jnp.zeros_like(dwi_ref)

        prev = jnp.where(chunk > 0, 1.0, 0.0)
        xa_ext[head, :] = uh_ref[:, 0:c] * prev
        xa_ext[SUBLANES:SUBLANES + tc, :] = u_ref[:, 0:c]
        v_ext[head, :] = uh_ref[:, 3 * c:4 * c] * uh_ref[:, 4 * c:5 * c] * prev
        v_ext[SUBLANES:SUBLANES + tc, :] = u_ref[:, 3 * c:4 * c] * u_ref[:, 4 * c:5 * c]
        hs_ext[head, :] = hsh_ref[...] * prev
        hs_ext[SUBLANES:SUBLANES + tc, :] = hs_ref[...]

        xc = ba_ref[...]
        for k in range(ka):
            xc = xc + wa_ref[pl.ds(k, 1), :] * xa_ext[pl.ds(SUBLANES - (ka - 1) + k, tc), :]
        xc_s[...] = xc
        c8, dc8 = _decay_consts(lam_ref[...])
        for j in range(nblk):
            sl = slice(j * gb, (j + 1) * gb)
            r, ig, a, sq = _gates(xc_s[:, sl], wr_ref, br_ref, wi_ref, bi_ref, c8, j, gb)
            r_s[:, sl] = r
            i_s[:, sl] = ig
            sq_s[:, sl] = sq
            a_ext[0:tc, sl] = a

        ga = u_ref[:, c:2 * c]
        sga = _sig(ga)
        g_s[...] = dy_ref[:, 0:c] * (ga * sga)
        an_s[...] = a_ext[pl.ds(1, tc), :]

        row = lax.broadcasted_iota(jnp.int32, (SUBLANES, c), 0)

        def scan_step(j, _):
            off = pl.multiple_of(tc - SUBLANES - j * SUBLANES, SUBLANES)
            av = an_s[pl.ds(off, SUBLANES), :]
            bv = g_s[pl.ds(off, SUBLANES), :]
            for d in (1, 2, 4):
                keep = row < SUBLANES - d
                bsh = jnp.where(keep, pltpu.roll(bv, SUBLANES - d, axis=0), 0.0)
                ash = jnp.where(keep, pltpu.roll(av, SUBLANES - d, axis=0), 1.0)
                bv = av * bsh + bv
                av = av * ash
            ds_ext[pl.ds(off, SUBLANES), :] = av * ds_ext[pl.ds(off + SUBLANES, 1), :] + bv
            return 0

        lax.fori_loop(0, tc // SUBLANES, scan_step, 0)

        def acc(row_index, val):
            dsm_ref[pl.ds(row_index, 1), :] += jnp.sum(val, axis=0, keepdims=True)

        def acc_block(row_index, sl, val):
            dsm_ref[pl.ds(row_index, 1), sl] += jnp.sum(val, axis=0, keepdims=True)

        for j in range(nblk):
            sl = slice(j * gb, (j + 1) * gb)
            ds = ds_ext[0:tc, sl]
            hprev = hs_ext[pl.ds(SUBLANES - 1, tc), sl]
            a = a_ext[0:tc, sl]
            sq = sq_s[:, sl]
            ig = i_s[:, sl]
            r = r_s[:, sl]
            xcj = xc_s[:, sl]
            t1 = ds * xcj
            dla = (ds * hprev) * a - (t1 * ig) * ((a * a) / sq)
            acc_block(ROW_DLAM, sl, dla * r)
            dpr = (dla * c8[:, sl]) * (r * (1.0 - r))
            dpi = (t1 * sq) * (ig * (1.0 - ig))
            acc_block(ROW_DBR, sl, dpr)
            acc_block(ROW_DBI, sl, dpi)
            p16 = dpr.astype(BF16)
            q16 = dpi.astype(BF16)
            x16 = xcj.astype(BF16)
            dwr_ref[j] += lax.dot_general(x16, p16, TN_DIMS, preferred_element_type=F32)
            dwi_ref[j] += lax.dot_general(x16, q16, TN_DIMS, preferred_element_type=F32)
            dxc = (ds * (sq * ig)
                   + lax.dot_general(p16, wr_ref[j], NT_DIMS, preferred_element_type=F32)
                   + lax.dot_general(q16, wi_ref[j], NT_DIMS, preferred_element_type=F32))
            dxc_ext[0:tc, sl] = dxc
            acc_block(ROW_DBA, sl, dxc)

        dsilu_a = sga * (1.0 + ga * (1.0 - sga))
        du_ref[:, c:2 * c] = (dy_ref[:, 0:c] * hs_ref[...] * dsilu_a).astype(BF16)

        dxc = dxc_ext[0:tc, :]
        dxa = wa_ref[pl.ds(ka - 1, 1), :] * dxc
        acc(ROW_DWA + ka - 1, dxc * xa_ext[SUBLANES:SUBLANES + tc, :])
        for k in range(ka - 1):
            acc(ROW_DWA + k, dxc * xa_ext[pl.ds(SUBLANES - (ka - 1) + k, tc), :])
            dxa = dxa + wa_ref[pl.ds(k, 1), :] * dxc_ext[pl.ds(ka - 1 - k, tc), :]
        du_ref[:, 0:c] = dxa.astype(BF16)

        cv = wb_ref[pl.ds(0, 1), :] * v_ext[pl.ds(SUBLANES - (kb - 1), tc), :]
        for k in range(1, kb):
            cv = cv + wb_ref[pl.ds(k, 1), :] * v_ext[pl.ds(SUBLANES - (kb - 1) + k, tc), :]
        gbv = u_ref[:, 5 * c:6 * c]
        sgb = _sig(gbv)
        silu_b = gbv * sgb
        dyb = dy_ref[:, c:2 * c]
        gB = u_ref[:, 2 * c:3 * c]
        du_ref[:, 2 * c:3 * c] = (dyb * cv * silu_b).astype(BF16)
        du_ref[:, 5 * c:6 * c] = (dyb * gB * cv * (sgb * (1.0 + gbv * (1.0 - sgb)))).astype(BF16)
        dcv = dyb * gB * silu_b
        dcv_ext[0:tc, :] = dcv
        dv = wb_ref[pl.ds(kb - 1, 1), :] * dcv
        acc(ROW_DWB + kb - 1, dcv * v_ext[SUBLANES:SUBLANES + tc, :])
        for k in range(kb - 1):
            acc(ROW_DWB + k, dcv * v_ext[pl.ds(SUBLANES - (kb - 1) + k, tc), :])
            dv = dv + wb_ref[pl.ds(k, 1), :] * dcv_ext[pl.ds(kb - 1 - k, tc), :]
        du_ref[:, 3 * c:4 * c] = (dv * u_ref[:, 4 * c:5 * c]).astype(BF16)
        du_ref[:, 4 * c:5 * c] = (dv * u_ref[:, 3 * c:4 * c]).astype(BF16)

        a_ext[tail, :] = a_ext[head, :]
        ds_ext[tail, :] = ds_ext[head, :]
        dxc_ext[tail, :] = dxc_ext[head, :]
        dcv_ext[tail, :] = dcv_ext[head, :]

        @pl.when(i == nt - 1)
        def _():
            dsm_ref[pl.ds(ROW_DLAM, 1), :] = dsm_ref[pl.ds(ROW_DLAM, 1), :] * dc8

    full = lambda shape: pl.BlockSpec(shape, lambda i: (0,) * len(shape))
    rev = lambda i: (nt - 1 - i, 0)
    halo = lambda i: (jnp.maximum((nt - 1 - i) * hb - 1, 0), 0)
    ext = pltpu.VMEM((tc + SUBLANES, c), F32)
    blk = pltpu.VMEM((tc, c), F32)
    body, more_specs, more = _behind(body, 13, after)
    return pl.pallas_call(
        body, name=name, grid=(nt,),
        in_specs=[pl.BlockSpec((tc, 6 * c), rev), pl.BlockSpec((SUBLANES, 6 * c), halo),
                  pl.BlockSpec((tc, c), rev), pl.BlockSpec((SUBLANES, c), halo),
                  pl.BlockSpec((tc, 2 * c), rev),
                  full(wa.shape), full(ba.shape), full(wr.shape), full(br.shape),
                  full(wi.shape), full(bi.shape), full(lam.shape), full(wb.shape)] + more_specs,
        out_specs=[pl.BlockSpec((tc, 6 * c), rev), full((SMALL_ROWS, c)), full(wr.shape), full(wi.shape)],
        out_shape=[jax.ShapeDtypeStruct((t, 6 * c), BF16), jax.ShapeDtypeStruct((SMALL_ROWS, c), F32),
                   jax.ShapeDtypeStruct(wr.shape, F32), jax.ShapeDtypeStruct(wi.shape, F32)],
        scratch_shapes=[ext] * 7 + [blk] * 6,
        compiler_params=_params(("arbitrary",)),
    )(u, u, hs, hs, dy, wa, ba, wr, br, wi, bi, lam, wb, *more)


def _behind(body, n_in, after):
    if after is None:
        return body, [], []
    return (lambda *refs: body(*refs[:n_in], *refs[n_in + 1:])), [ANY], [after]


def _out_proj(h, y, w, name, after=None):
    t, d = h.shape
    dm = y.shape[1]
    tm = _row_tile(t)
    tn = _col_tile(d, (1024, 512, 256))

    def body(h_ref, y_ref, w_ref, o_ref):
        o_ref[...] = h_ref[...] + jnp.dot(y_ref[...], w_ref[...], preferred_element_type=F32)

    body, more_specs, more = _behind(body, 3, after)
    return pl.pallas_call(
        body, name=name, grid=(d // tn, t // tm),
        in_specs=[pl.BlockSpec((tm, tn), lambda n, i: (i, n)),
                  pl.BlockSpec((tm, dm), lambda n, i: (i, 0)),
                  pl.BlockSpec((dm, tn), lambda n, i: (0, n))] + more_specs,
        out_specs=pl.BlockSpec((tm, tn), lambda n, i: (i, n)),
        out_shape=jax.ShapeDtypeStruct((t, d), F32),
        compiler_params=_params(("arbitrary", "arbitrary")),
    )(h, y, w, *more)


def _out_proj_dy(dout, w, name, after=None):
    t, d = dout.shape
    dm = w.shape[0]
    tm = _row_tile(t)
    tn = _col_tile(dm, (1024, 512, 256))

    def body(g_ref, w_ref, o_ref):
        o_ref[...] = lax.dot_general(g_ref[...].astype(BF16), w_ref[...], NT_DIMS, preferred_element_type=F32)

    body, more_specs, more = _behind(body, 2, after)
    return pl.pallas_call(
        body, name=name, grid=(dm // tn, t // tm),
        in_specs=[pl.BlockSpec((tm, d), lambda n, i: (i, 0)),
                  pl.BlockSpec((tn, d), lambda n, i: (n, 0))] + more_specs,
        out_specs=pl.BlockSpec((tm, tn), lambda n, i: (i, n)),
        out_shape=jax.ShapeDtypeStruct((t, dm), F32),
        compiler_params=_params(("arbitrary", "arbitrary")),
    )(dout, w, *more)


def _out_proj_dw(y, dout, name):
    t, dm = y.shape
    d = dout.shape[1]
    tmm = _col_tile(dm, (512, 256))
    tn = _col_tile(d, (512, 256))

    def body(y_ref, g_ref, o_ref):
        o_ref[...] = lax.dot_general(y_ref[...], g_ref[...].astype(BF16), TN_DIMS, preferred_element_type=F32)

    return pl.pallas_call(
        body, name=name, grid=(d // tn, dm // tmm),
        in_specs=[pl.BlockSpec((t, tmm), lambda n, m: (0, m)),
                  pl.BlockSpec((t, tn), lambda n, m: (0, n))],
        out_specs=pl.BlockSpec((tmm, tn), lambda n, m: (m, n)),
        out_shape=jax.ShapeDtypeStruct((dm, d), F32),
        compiler_params=_params(("arbitrary", "arbitrary")),
    )(y, dout)


def _in_proj_bwd(du, wg, h, g, dout, name, after=None):
    t, d = h.shape
    s, _, ns = wg.shape
    tm = _row_tile(t)
    tn = _col_tile(d, (512, 256))

    def mm_body(du_ref, w_ref, o_ref):
        total = lax.dot_general(du_ref[:, 0:ns], w_ref[0], NT_DIMS, preferred_element_type=F32)
        for a in range(1, s):
            total = total + lax.dot_general(du_ref[:, a * ns:(a + 1) * ns], w_ref[a], NT_DIMS,
                                            preferred_element_type=F32)
        o_ref[...] = total

    mm_body, more_specs, more = _behind(mm_body, 2, after)
    dhn = pl.pallas_call(
        mm_body, name=name, grid=(t // tm, d // tn),
        in_specs=[pl.BlockSpec((tm, s * ns), lambda i, n: (i, 0)),
                  pl.BlockSpec((s, tn, ns), lambda i, n: (0, n, 0))] + more_specs,
        out_specs=pl.BlockSpec((tm, tn), lambda i, n: (i, n)),
        out_shape=jax.ShapeDtypeStruct((t, d), F32),
        compiler_params=_params(("arbitrary", "arbitrary")),
    )(du, wg, *more)

    tr = 352 if t % 352 == 0 else 192

    def norm_body(dhn_ref, h_ref, g_ref, dout_ref, dh_ref, dg_ref):
        @pl.when(pl.program_id(0) == 0)
        def _():
            dg_ref[...] = jnp.zeros_like(dg_ref)

        x = h_ref[...]
        dn = dhn_ref[...]
        r = lax.rsqrt(jnp.mean(x * x, axis=-1, keepdims=True) + RMS_EPS)
        gd = dn * g_ref[...]
        dot = jnp.mean(gd * x, axis=-1, keepdims=True)
        dh_ref[...] = dout_ref[...] + (r * gd - x * ((r * r * r) * dot))
        dg_ref[...] += jnp.sum(dn * (x * r), axis=0, keepdims=True)

    rows = pl.BlockSpec((tr, d), lambda i: (i, 0))
    one = pl.BlockSpec((1, d), lambda i: (0, 0))
    return pl.pallas_call(
        norm_body, name=name + "_norm", grid=(t // tr,),
        in_specs=[rows, rows, one, rows], out_specs=[rows, one],
        out_shape=[jax.ShapeDtypeStruct((t, d), F32), jax.ShapeDtypeStruct((1, d), F32)],
        compiler_params=_params(("arbitrary",)),
    )(dhn, h, g, dout)


def _in_proj_dw(hn, du, s, name, after=None):
    t, d = hn.shape
    ns = du.shape[1] // s
    tmm = _col_tile(d, (512, 256))
    tn = _col_tile(ns, (768, 384, 128))
    nb = ns // tn

    def body(hn_ref, du_ref, o_ref):
        o_ref[...] = lax.dot_general(hn_ref[...], du_ref[...], TN_DIMS, preferred_element_type=F32)

    body, more_specs, more = _behind(body, 2, after)
    return pl.pallas_call(
        body, name=name, grid=(s * nb, d // tmm),
        in_specs=[pl.BlockSpec((t, tmm), lambda n, m: (0, m)),
                  pl.BlockSpec((t, tn), lambda n, m: (0, n))] + more_specs,
        out_specs=pl.BlockSpec((None, tmm, tn), lambda n, m: (n // nb, m, n % nb)),
        out_shape=jax.ShapeDtypeStruct((s, d, ns), F32),
        compiler_params=_params(("arbitrary", "arbitrary")),
    )(hn, du, *more)


def _loss_head(h, tgt, g, n_meta, t_real, name):
    t, d = h.shape
    tm = _row_tile(t)

    def body(h_ref, t_ref, g_ref, dh_ref, loss_ref, dg_ref):
        i = pl.program_id(0)

        @pl.when(i == 0)
        def _():
            loss_ref[...] = jnp.zeros_like(loss_ref)
            dg_ref[...] = jnp.zeros_like(dg_ref)

        x = h_ref[...]
        gv = g_ref[...]
        r = lax.rsqrt(jnp.mean(x * x, axis=-1, keepdims=True) + RMS_EPS)
        xr = x * r
        rows = i * tm + lax.broadcasted_iota(jnp.int32, (tm, 1), 0)
        valid = (rows >= n_meta) & (rows < t_real)
        err = jnp.where(valid, xr * gv - t_ref[...], 0.0)
        loss_ref[...] += 0.5 * jnp.sum(jnp.mean(err * err, axis=-1, keepdims=True))
        dy = err * (1.0 / d)
        gd = dy * gv
        dot = jnp.mean(gd * x, axis=-1, keepdims=True)
        dh_ref[...] = r * gd - x * ((r * r * r) * dot)
        dg_ref[...] += jnp.sum(dy * xr, axis=0, keepdims=True)

    return pl.pallas_call(
        body, name=name, grid=(t // tm,),
        in_specs=[pl.BlockSpec((tm, d), lambda i: (i, 0)),
                  pl.BlockSpec((tm, d), lambda i: (i, 0)),
                  pl.BlockSpec((1, d), lambda i: (0, 0))],
        out_specs=[pl.BlockSpec((tm, d), lambda i: (i, 0)),
                   pl.BlockSpec((1, LANES), lambda i: (0, 0)),
                   pl.BlockSpec((1, d), lambda i: (0, 0))],
        out_shape=[jax.ShapeDtypeStruct((t, d), F32), jax.ShapeDtypeStruct((1, LANES), F32),
                   jax.ShapeDtypeStruct((1, d), F32)],
        compiler_params=_params(("arbitrary",)),
    )(h, tgt, g)


def _adamw_rows(rows, cols):
    for cand in (512, 256, 128, 64, 32, 16, 8):
        if rows % cand == 0 and cand * cols * 4 <= 2 * 1024 * 1024:
            return cand
    return rows


def _adamw_math(w_ref, g_ref, m_ref, v_ref, d_ref, nm_ref, nv_ref):
    gv = g_ref[...]
    m2 = ADAM_B1 * m_ref[...] + (1.0 - ADAM_B1) * gv
    v2 = ADAM_B2 * v_ref[...] + (1.0 - ADAM_B2) * (gv * gv)
    m_hat = m2 / (1.0 - ADAM_B1 ** ADAM_STEP)
    v_hat = v2 / (1.0 - ADAM_B2 ** ADAM_STEP)
    d_ref[...] = -ADAM_LR * (m_hat / (jnp.sqrt(v_hat) + ADAM_EPS) + ADAM_WD * w_ref[...])
    nm_ref[...] = m2
    nv_ref[...] = v2


def _adamw(w, g, m, v, name):
    rows, cols = w.shape
    tr = _adamw_rows(rows, cols)

    def body(*refs):
        _adamw_math(*refs)

    spec = pl.BlockSpec((tr, cols), lambda i: (i, 0))
    return pl.pallas_call(
        body, name=name, grid=(rows // tr,),
        in_specs=[spec] * 4, out_specs=[spec] * 3,
        out_shape=[jax.ShapeDtypeStruct((rows, cols), F32)] * 3,
        compiler_params=_params(("arbitrary",)),
    )(w, g, m, v)


def _adamw_layer(w, g, m, v, layer, kept, name, after=None):
    nl, rows, cols = w.shape
    tr = _adamw_rows(rows, cols)
    n_kept = 0 if kept is None else 3

    def body(*refs):
        _adamw_math(*refs[:4], *refs[4 + n_kept:])

    body, more_specs, more = _behind(body, 4 + n_kept, after)
    lay = pl.BlockSpec((None, tr, cols), lambda i: (layer, i, 0))
    return pl.pallas_call(
        body, name=name, grid=(rows // tr,),
        in_specs=[lay, pl.BlockSpec((tr, cols), lambda i: (i, 0)), lay, lay] + [ANY] * n_kept + more_specs,
        out_specs=[lay] * 3,
        out_shape=[jax.ShapeDtypeStruct((nl, rows, cols), F32)] * 3,
        input_output_aliases={4 + k: k for k in range(n_kept)},
        compiler_params=_params(("arbitrary",)),
    )(w, g, m, v, *([] if kept is None else kept), *more)


def _pair_add(x, ra, c_idx, name):
    s, _, rows, cols = x.shape
    tr = _slab_rows(rows, cols)

    def body(c_ref, x_ref, r_ref, o_ref):
        o_ref[...] = (x_ref[...] + r_ref[...]).astype(BF16)

    return pl.pallas_call(
        body, name=name,
        grid_spec=pltpu.PrefetchScalarGridSpec(
            num_scalar_prefetch=1, grid=(s, rows // tr),
            in_specs=[pl.BlockSpec((None, None, tr, cols), lambda a, i, c_ref: (a, c_ref[0], i, 0)),
                      pl.BlockSpec((None, tr, cols), lambda a, i, c_ref: (a, i, 0))],
            out_specs=pl.BlockSpec((None, tr, cols), lambda a, i, c_ref: (a, i, 0))),
        out_shape=jax.ShapeDtypeStruct((s, rows, cols), BF16),
        compiler_params=_params(("arbitrary", "arbitrary")),
    )(c_idx, x, ra)


def _chip_sum(rc, p, where, n_slots, name):
    s, rows, cols = rc.shape
    tr = _slab_rows(rows, cols)

    def body(w_ref, x_ref, p_ref, o_ref):
        me = w_ref[0]
        total = jnp.where(me == 0, p_ref[...], x_ref[0]).astype(F32)
        for a in range(1, s):
            total = total + jnp.where(me == a, p_ref[...], x_ref[a]).astype(F32)
        o_ref[...] = total

    return pl.pallas_call(
        body, name=name,
        grid_spec=pltpu.PrefetchScalarGridSpec(
            num_scalar_prefetch=1, grid=(rows // tr,),
            in_specs=[pl.BlockSpec((s, tr, cols), lambda i, w_ref: (0, i, 0)),
                      pl.BlockSpec((None, tr, cols), lambda i, w_ref: (w_ref[0], i, 0))],
            out_specs=pl.BlockSpec((None, tr, cols), lambda i, w_ref: (w_ref[1], i, 0))),
        out_shape=jax.ShapeDtypeStruct((n_slots, rows, cols), F32),
        compiler_params=_params(("arbitrary",)),
    )(where, rc, p)


def _cast_place(w, layer, me_idx, name, after=None):
    _, rows, cols = w.shape
    tr = _slab_rows(rows, cols)

    def body(m_ref, w_ref, o_ref):
        o_ref[...] = w_ref[...].astype(BF16)

    body, more_specs, more = _behind(body, 2, after)
    return pl.pallas_call(
        body, name=name,
        grid_spec=pltpu.PrefetchScalarGridSpec(
            num_scalar_prefetch=1, grid=(rows // tr,),
            in_specs=[pl.BlockSpec((None, tr, cols), lambda i, m_ref: (layer, i, 0))] + more_specs,
            out_specs=pl.BlockSpec((None, tr, cols), lambda i, m_ref: (m_ref[0], i, 0))),
        out_shape=jax.ShapeDtypeStruct((N_CHIPS, rows, cols), BF16),
        compiler_params=_params(("arbitrary",)),
    )(me_idx, w, *more)


def _place():
    x, y, c = lax.axis_index("x"), lax.axis_index("y"), lax.axis_index("c")
    chips = [(1 - x, y), (x, 1 - y), (1 - x, 1 - y)]
    return x, y, c, chips


def _chip_index(cx, cy):
    return 2 * cx + cy


def _gather_copies(bufs, stage):
    x, y, c, chips = _place()
    me = _chip_index(x, y)
    copies = []
    for b in bufs:
        for chip in chips:
            src = _chip_index(*chip)
            if stage == 0:
                copies.append((b.at[me, c], (*chip, c), b.at[src, c]))
            else:
                copies.append((b.at[src, c], (x, y, 1 - c), b.at[src, 1 - c]))
    return copies


def _remote(ref, peer, ssem, rsem, k):
    return pltpu.make_async_remote_copy(src_ref=ref, dst_ref=ref, send_sem=ssem.at[k], recv_sem=rsem.at[k],
                                        device_id=peer, device_id_type=MESH)


def _gather_first(bufs, small):
    n = len(bufs)
    k = 3 * n

    def body(*refs):
        sm_ref = refs[n]
        b_refs, smg_ref = refs[n + 1:2 * n + 1], refs[2 * n + 1]
        lsem, ssem, rsem = refs[2 * n + 2:]
        x, y, c, chips = _place()
        me = _chip_index(x, y)
        local = pltpu.make_async_copy(sm_ref, smg_ref.at[me], lsem)
        local.start()
        first = _gather_copies(b_refs, 0)
        second = _gather_copies(b_refs, 1)
        started = []
        for i, (ref, peer, _) in enumerate(first):
            started.append(_remote(ref, peer, ssem, rsem, i))
        for j, chip in enumerate(chips):
            started.append(pltpu.make_async_remote_copy(
                src_ref=sm_ref, dst_ref=smg_ref.at[me], send_sem=ssem.at[2 * k + j], recv_sem=rsem.at[2 * k + j],
                device_id=(*chip, c), device_id_type=MESH))
        for cp in started:
            cp.start()
        for i, (_, peer, lands) in enumerate(first):
            _remote(lands, peer, ssem, rsem, i).wait_recv()
            ref, sib, _ = second[i]
            fwd = _remote(ref, sib, ssem, rsem, k + i)
            fwd.start()
            started.append(fwd)
        for i, (_, sib, lands) in enumerate(second):
            _remote(lands, sib, ssem, rsem, k + i).wait_recv()
        for j, chip in enumerate(chips):
            theirs = smg_ref.at[_chip_index(*chip)]
            pltpu.make_async_remote_copy(src_ref=theirs, dst_ref=theirs, send_sem=ssem.at[2 * k + j],
                                         recv_sem=rsem.at[2 * k + j], device_id=(*chip, c),
                                         device_id_type=MESH).wait_recv()
        for cp in started:
            cp.wait_send()
        local.wait()

    return pl.pallas_call(
        body, name="gather_first",
        in_specs=[ANY] * (n + 1), out_specs=[ANY] * (n + 1),
        out_shape=[jax.ShapeDtypeStruct(b.shape, b.dtype) for b in bufs]
        + [jax.ShapeDtypeStruct((N_CHIPS,) + small.shape, small.dtype)],
        input_output_aliases={i: i for i in range(n)},
        scratch_shapes=[pltpu.SemaphoreType.DMA, pltpu.SemaphoreType.DMA((2 * k + 3,)),
                        pltpu.SemaphoreType.DMA((2 * k + 3,))],
    )(*bufs, small)


HBM = pl.BlockSpec(memory_space=pltpu.HBM)
SEM = pl.BlockSpec(memory_space=pltpu.SEMAPHORE)
DATAFLOW = pltpu.SideEffectType.DATAFLOW_SIDE_EFFECTING


def _copies_start(bufs, plan, n_copies, name, after=None):
    n = len(bufs)
    extra = [] if after is None else [after]

    def body(*refs):
        refs = refs[:n] + refs[n + len(extra):]
        ssem, rsem = refs[n], refs[n + 1]
        b_refs, token = refs[n + 2:2 * n + 2], refs[2 * n + 2]
        copies = plan(b_refs)
        assert len(copies) == n_copies
        for i, (src, dst, peer, _) in enumerate(copies):
            pltpu.make_async_remote_copy(src_ref=src, dst_ref=dst, send_sem=ssem.at[i], recv_sem=rsem.at[i],
                                         device_id=peer, device_id_type=MESH).start()
        token[...] = jnp.zeros_like(token)

    return pl.pallas_call(
        body, name=name,
        out_shape=(pltpu.SemaphoreType.DMA((n_copies,)), pltpu.SemaphoreType.DMA((n_copies,)),
                   *[pltpu.HBM(b.shape, b.dtype) for b in bufs], jax.ShapeDtypeStruct((SUBLANES, LANES), F32)),
        in_specs=[HBM] * n + [ANY] * len(extra),
        out_specs=(SEM, SEM, *[HBM] * n, pl.BlockSpec(memory_space=pltpu.VMEM)),
        input_output_aliases={i: 2 + i for i in range(n)},
        compiler_params=pltpu.CompilerParams(has_side_effects=DATAFLOW),
    )(*[pltpu.with_memory_space_constraint(b, pltpu.HBM) for b in bufs], *extra)


def _copies_wait(bufs, ssem, rsem, after, plan, name):
    n = len(bufs)
    afters = list(after) if isinstance(after, (list, tuple)) else [after]

    def body(*refs):
        b_refs, ssem_ref, rsem_ref = refs[:n], refs[n], refs[n + 1]
        for i, (src, dst, peer, lands) in enumerate(plan(b_refs)):
            pltpu.make_async_remote_copy(src_ref=src, dst_ref=dst, send_sem=ssem_ref.at[i], recv_sem=rsem_ref.at[i],
                                         device_id=peer, device_id_type=MESH).wait_send()
            pltpu.make_async_remote_copy(src_ref=lands, dst_ref=lands, send_sem=ssem_ref.at[i],
                                         recv_sem=rsem_ref.at[i], device_id=peer, device_id_type=MESH).wait_recv()

    return pl.pallas_call(
        body, name=name,
        out_shape=tuple(pltpu.HBM(b.shape, b.dtype) for b in bufs),
        in_specs=[HBM] * n + [SEM, SEM] + [ANY] * len(afters), out_specs=tuple([HBM] * n),
        input_output_aliases={i: i for i in range(n)},
        compiler_params=pltpu.CompilerParams(has_side_effects=DATAFLOW),
    )(*bufs, ssem, rsem, *afters)


def _gather_plan(stage):
    return lambda refs: [(ref, ref, peer, lands) for ref, peer, lands in _gather_copies(refs, stage)]


def _swap_plan(refs):
    n = len(refs) // 2
    x, y, c, _ = _place()
    return [(refs[a].at[:, 1 - c], refs[n + a], (x, y, 1 - c), refs[n + a]) for a in range(n)]


def _scatter_plan(refs):
    n = len(refs) // 2
    x, y, c, chips = _place()
    me = _chip_index(x, y)
    return [(refs[a].at[_chip_index(*chip)], refs[n + a].at[me], (*chip, c), refs[n + a].at[_chip_index(*chip)])
            for a in range(n) for chip in chips]


def _pair_gather_plan(refs):
    x, y, c, _ = _place()
    return [(r.at[c], r.at[c], (x, y, 1 - c), r.at[1 - c]) for r in refs]


def _pair_swap(xs, name):
    n = len(xs)

    def body(*refs):
        x_refs, o_refs, ssem, rsem = refs[:n], refs[n:2 * n], refs[2 * n], refs[2 * n + 1]
        x, y, c, _ = _place()
        copies = [pltpu.make_async_remote_copy(src_ref=x_refs[a].at[:, 1 - c], dst_ref=o_refs[a],
                                               send_sem=ssem.at[a], recv_sem=rsem.at[a],
                                               device_id=(x, y, 1 - c), device_id_type=MESH) for a in range(n)]
        for cp in copies:
            cp.start()
        for cp in copies:
            cp.wait()

    return pl.pallas_call(
        body, name=name, in_specs=[ANY] * n, out_specs=[ANY] * n,
        out_shape=[jax.ShapeDtypeStruct((a.shape[0],) + a.shape[2:], a.dtype) for a in xs],
        scratch_shapes=[pltpu.SemaphoreType.DMA((n,)), pltpu.SemaphoreType.DMA((n,))],
    )(*xs)


def _chip_scatter(ps):
    n = len(ps)

    def body(*refs):
        p_refs, o_refs, ssem, rsem = refs[:n], refs[n:2 * n], refs[2 * n], refs[2 * n + 1]
        x, y, c, chips = _place()
        me = _chip_index(x, y)
        sends = []
        for a in range(n):
            for j, chip in enumerate(chips):
                sends.append(pltpu.make_async_remote_copy(
                    src_ref=p_refs[a].at[_chip_index(*chip)], dst_ref=o_refs[a].at[me],
                    send_sem=ssem.at[3 * a + j], recv_sem=rsem.at[3 * a + j],
                    device_id=(*chip, c), device_id_type=MESH))
        for cp in sends:
            cp.start()
        for a in range(n):
            for j, chip in enumerate(chips):
                src = _chip_index(*chip)
                pltpu.make_async_remote_copy(
                    src_ref=p_refs[a].at[src], dst_ref=o_refs[a].at[src],
                    send_sem=ssem.at[3 * a + j], recv_sem=rsem.at[3 * a + j],
                    device_id=(*chip, c), device_id_type=MESH).wait_recv()
        for cp in sends:
            cp.wait_send()

    return pl.pallas_call(
        body, name="chip_scatter", in_specs=[ANY] * n, out_specs=[ANY] * n,
        out_shape=[jax.ShapeDtypeStruct(a.shape, a.dtype) for a in ps],
        scratch_shapes=[pltpu.SemaphoreType.DMA((3 * n,)), pltpu.SemaphoreType.DMA((3 * n,))],
    )(*ps)


def _final_gather(fs, rep):
    n = len(fs)

    def body(*refs):
        o_refs, repo_ref = refs[n + 1:2 * n + 1], refs[2 * n + 1]
        ssem, rsem = refs[2 * n + 2:]
        x, y, c, chips = _place()
        slot = 4 * x + 2 * y + c
        copies = [pltpu.make_async_remote_copy(src_ref=o_refs[a].at[c], dst_ref=o_refs[a].at[c],
                                               send_sem=ssem.at[a], recv_sem=rsem.at[a],
                                               device_id=(x, y, 1 - c), device_id_type=MESH) for a in range(n)]
        peers = [(x, y, 1 - c)] + [(*chip, c) for chip in chips] + [(*chip, 1 - c) for chip in chips]
        for k, peer in enumerate(peers):
            copies.append(pltpu.make_async_remote_copy(src_ref=repo_ref.at[slot], dst_ref=repo_ref.at[slot],
                                                       send_sem=ssem.at[n + k], recv_sem=rsem.at[n + k],
                                                       device_id=peer, device_id_type=MESH))
        for cp in copies:
            cp.start()
        for a in range(n):
            pltpu.make_async_remote_copy(src_ref=o_refs[a].at[1 - c], dst_ref=o_refs[a].at[1 - c],
                                         send_sem=ssem.at[a], recv_sem=rsem.at[a],
                                         device_id=(x, y, 1 - c), device_id_type=MESH).wait_recv()
        for k, peer in enumerate(peers):
            px, py, pc = peer
            theirs = repo_ref.at[4 * px + 2 * py + pc]
            pltpu.make_async_remote_copy(src_ref=theirs, dst_ref=theirs, send_sem=ssem.at[n + k], recv_sem=rsem.at[n + k],
                                         device_id=peer, device_id_type=MESH).wait_recv()
        for cp in copies:
            cp.wait_send()

    return pl.pallas_call(
        body, name="final_gather", in_specs=[ANY] * (n + 1), out_specs=[ANY] * (n + 1),
        out_shape=[jax.ShapeDtypeStruct(a.shape, a.dtype) for a in fs] + [jax.ShapeDtypeStruct(rep.shape, rep.dtype)],
        input_output_aliases={k: k for k in range(n + 1)},
        scratch_shapes=[pltpu.SemaphoreType.DMA((n + 7,)), pltpu.SemaphoreType.DMA((n + 7,))],
    )(*fs, rep)


def _block_diag(w, gb):
    nh, hd, _ = w.shape
    per = gb // hd
    w4 = w.reshape(nh // per, per, hd, hd)
    eye = jnp.eye(per, dtype=w.dtype)
    return jnp.einsum("jaik,ab->jaibk", w4, eye).reshape(nh // per, gb, gb)


def _diag_blocks(dense, hd):
    nj, gb, _ = dense.shape
    per = gb // hd
    d5 = dense.reshape(nj, per, hd, per, hd)
    return jnp.stack([d5[:, a, :, a, :] for a in range(per)], axis=1).reshape(nj * per, hd, hd)


def _round_up(n, q):
    return (n + q - 1) // q * q


def kernel(x, meta, norm_g, w_in, conv_a_w, conv_a_b, lru_wr, lru_br, lru_wi, lru_bi, lru_lambda, conv_b_w, w_out, final_g, loss_target, m_meta, m_norm_g, m_w_in, m_conv_a_w, m_conv_a_b, m_lru_wr, m_lru_br, m_lru_wi, m_lru_bi, m_lru_lambda, m_conv_b_w, m_w_out, m_final_g, v_meta, v_norm_g, v_w_in, v_conv_a_w, v_conv_a_b, v_lru_wr, v_lru_br, v_lru_wi, v_lru_bi, v_lru_lambda, v_conv_b_w, v_w_out, v_final_g):
    weights = dict(meta=meta, norm_g=norm_g, w_in=w_in, conv_a_w=conv_a_w, conv_a_b=conv_a_b, lru_wr=lru_wr,
                   lru_br=lru_br, lru_wi=lru_wi, lru_bi=lru_bi, lru_lambda=lru_lambda, conv_b_w=conv_b_w,
                   w_out=w_out, final_g=final_g)
    mom1 = dict(meta=m_meta, norm_g=m_norm_g, w_in=m_w_in, conv_a_w=m_conv_a_w, conv_a_b=m_conv_a_b,
                lru_wr=m_lru_wr, lru_br=m_lru_br, lru_wi=m_lru_wi, lru_bi=m_lru_bi, lru_lambda=m_lru_lambda,
                conv_b_w=m_conv_b_w, w_out=m_w_out, final_g=m_final_g)
    mom2 = dict(meta=v_meta, norm_g=v_norm_g, w_in=v_w_in, conv_a_w=v_conv_a_w, conv_a_b=v_conv_a_b,
                lru_wr=v_lru_wr, lru_br=v_lru_br, lru_wi=v_lru_wi, lru_bi=v_lru_bi, lru_lambda=v_lru_lambda,
                conv_b_w=v_conv_b_w, w_out=v_w_out, final_g=v_final_g)
    names = list(weights)

    assert x.shape[0] == 1
    seq, d = x.shape[1], x.shape[2]
    n_meta, ds = meta.shape
    depth = norm_g.shape[0]
    c = lru_lambda.shape[1]
    nh, hd = lru_wr.shape[1], lru_wr.shape[2]
    ns = w_in.shape[2]
    dms = w_out.shape[1]
    cs = conv_a_w.shape[2]
    ka, kb = conv_a_w.shape[1], conv_b_w.shape[1]
    s = N_CHIPS
    assert depth == N_CORES and d == s * ds and c == s * cs and s * ns == 6 * c and s * dms == 2 * c
    gb = min(GATE_BLOCK, c)
    t_real = n_meta + seq
    t = _round_up(t_real, ROW_QUANTUM)
    my_c = lax.axis_index("c").astype(jnp.int32)
    my_chip = (2 * lax.axis_index("x") + lax.axis_index("y")).astype(jnp.int32)
    c_idx = my_c.reshape(1)
    chip_idx = my_chip.reshape(1)

    sm_rows = _round_up(n_meta + depth * SUBLANES, 2 * SUBLANES)
    small = jnp.zeros((sm_rows, ds), F32)
    small = small.at[0:n_meta, :].set(meta)
    for l in range(depth):
        base = n_meta + l * SUBLANES
        small = small.at[base:base + ka, 0:cs].set(conv_a_w[l])
        small = small.at[base + ka:base + ka + kb, 0:cs].set(conv_b_w[l])
    (small_g,) = _gather_first([], small)
    meta_full = jnp.transpose(small_g[:, 0:n_meta, :], (1, 0, 2)).reshape(n_meta, d)
    wa_full, wb_full = [], []
    for l in range(depth):
        base = n_meta + l * SUBLANES
        wa_full.append(jnp.transpose(small_g[:, base:base + ka, 0:cs], (1, 0, 2)).reshape(ka, c))
        wb_full.append(jnp.transpose(small_g[:, base + ka:base + ka + kb, 0:cs], (1, 0, 2)).reshape(kb, c))
    win0 = _cast_place(w_in, 0, chip_idx, "cast_w_in_0").reshape(s, 2, d // 2, ns)
    ssem_w, rsem_w, win0, token_w = _copies_start([win0], _gather_plan(0), 3, "gather_win0_ici_start", after=small_g)
    win_b = [None] + [_cast_place(w_in, l, chip_idx, f"cast_w_in_{l}", after=token_w).reshape(s, 2, d // 2, ns)
                      for l in range(1, depth)]
    wout_b = [_cast_place(w_out, l, chip_idx, f"cast_w_out_{l}", after=token_w).reshape(s, 2, dms // 2, d)
              for l in range(depth)]
    h = jnp.concatenate([meta_full, x[0], jnp.zeros((t - t_real, d), F32)], axis=0) + token_w[0, 0]
    tgt = jnp.concatenate([jnp.zeros((n_meta, d), F32), loss_target[0], jnp.zeros((t - t_real, d), F32)],
                          axis=0) + token_w[0, 0]
    u_own, hn_own = _norm_in_own(h, norm_g[0].reshape(1, d), win0.reshape(s, d, ns), chip_idx, "norm_in_0_own")
    (win0,) = _copies_wait([win0], ssem_w, rsem_w, [u_own, tgt] + win_b[1:] + wout_b, _gather_plan(0),
                           "gather_win0_ici_wait")
    ssem_w, rsem_w, win0, token_w = _copies_start([win0], _gather_plan(1), 3, "gather_win0_d2d_start")
    ssem_o, rsem_o, wout0, token_o = _copies_start([wout_b[0]], _gather_plan(0), 3, "gather_wout0_ici_start",
                                                   after=token_w)
    later = [win_b[1], wout_b[1]]
    ssem, rsem, *later, token = _copies_start(later, _gather_plan(0), 3 * len(later), "gather_next_ici_start",
                                              after=token_o)
    (win_b[0],) = _copies_wait([win0], ssem_w, rsem_w, token, _gather_plan(1), "gather_win0_d2d_wait")

    layer_w = []
    for l in range(depth):
        layer_w.append(dict(
            g=norm_g[l].reshape(1, d), wa=wa_full[l], ba=conv_a_b[l].reshape(1, c),
            wr=_block_diag(lru_wr[l], gb).astype(BF16), br=lru_br[l].reshape(1, c),
            wi=_block_diag(lru_wi[l], gb).astype(BF16), bi=lru_bi[l].reshape(1, c),
            lam=lru_lambda[l].reshape(1, c), wb=wb_full[l]))
    saved = []
    for l, lw in enumerate(layer_w):
        first = l == 0
        lw["win"] = win_b[l].reshape(s, d, ns)
        if first:
            u = _norm_in_rest(hn_own, lw["win"], u_own, chip_idx, "norm_in_0_rest", after=token)
            hn = hn_own
        else:
            u, hn = _norm_in(h, lw["g"], lw["win"], f"norm_in_{l}")
        if first:
            (wout0,) = _copies_wait([wout0], ssem_o, rsem_o, u, _gather_plan(0), "gather_wout0_ici_wait")
            ssem_o, rsem_o, wout0, token_o = _copies_start([wout0], _gather_plan(1), 3, "gather_wout0_d2d_start")
        y, hs = _mix_fwd(u, lw["wa"], lw["ba"] + token_o[0, 0] if first else lw["ba"], lw["wr"], lw["br"], lw["wi"],
                         lw["bi"], lw["lam"], lw["wb"], f"mix_fwd_{l}")
        token = None
        if first:
            (wout_b[0],) = _copies_wait([wout0], ssem_o, rsem_o, y, _gather_plan(1), "gather_wout0_d2d_wait")
            later = _copies_wait(later, ssem, rsem, y, _gather_plan(0), "gather_next_ici_wait")
            ssem, rsem, *later, token = _copies_start(later, _gather_plan(1), 3 * len(later), "gather_next_d2d_start")
        lw["wout"] = wout_b[l].reshape(2 * c, d)
        saved.append((h, u, hn, y, hs))
        h = _out_proj(h, y, lw["wout"], f"out_proj_{l}", after=token)
        if first:
            win_b[1], wout_b[1] = _copies_wait(later, ssem, rsem, h, _gather_plan(1), "gather_next_d2d_wait")
    dh, loss_lanes, d_final_g = _loss_head(h, tgt, final_g.reshape(1, d), n_meta, t_real, "loss_head")
    loss = lax.psum(loss_lanes[0, 0], ("x", "y", "c"))

    to_core = jnp.stack([my_chip, my_c])
    grads = [None] * depth
    early = None
    for l in reversed(range(depth)):
        lw = layer_w[l]
        h_in, u, hn, y, hs = saved[l]
        token = early[-1] if early else None
        dy = _out_proj_dy(dh, lw["wout"], f"out_proj_dy_{l}", after=token)
        d_wout = _out_proj_dw(y, dh, f"out_proj_dw_{l}")
        if early:
            ssem, rsem, bufs, _ = early
            bufs = _copies_wait(bufs, ssem, rsem, d_wout, _swap_plan, "early_swap_wait")
            half = len(bufs) // 2
            sums = [_pair_add(a, b, c_idx, f"early_pair_add_{k}") for k, (a, b) in enumerate(zip(bufs[:half], bufs[half:]))]
            lands = [lax.empty(p.shape, p.dtype) for p in sums]
            ssem, rsem, *bufs, token = _copies_start(sums + lands, _scatter_plan, 3 * half, "early_scatter_start")
        du, dsm, d_wr, d_wi = _mix_bwd(u, hs, dy, lw["wa"], lw["ba"], lw["wr"], lw["br"], lw["wi"], lw["bi"],
                                       lw["lam"], lw["wb"], f"mix_bwd_{l}", after=token)
        if early:
            bufs = _copies_wait(bufs, ssem, rsem, du, _scatter_plan, "early_scatter_wait")
            halves = [_chip_sum(rc, p, to_core, N_CORES, f"early_chip_sum_{k}")
                      for k, (p, rc) in enumerate(zip(bufs[:half], bufs[half:]))]
            ssem, rsem, *bufs, token = _copies_start(halves, _pair_gather_plan, half, "early_gather_start")
        d_win = _in_proj_dw(hn, du, s, f"in_proj_dw_{l}", after=token)
        srcs = [d_win.reshape(s, 2, d // 2, ns), d_wout.reshape(s, 2, dms // 2, d)]
        if early:
            early_full = _copies_wait(bufs, ssem, rsem, d_win, _pair_gather_plan, "early_gather_wait")
            lands = [lax.empty((a.shape[0],) + a.shape[2:], a.dtype) for a in srcs]
            ssem, rsem, *bufs, token = _copies_start(srcs + lands, _swap_plan, len(srcs), "late_swap_start")
            last = depth - 1
            early_grad = dict(w_in=early_full[0].reshape(d, ns), w_out=early_full[1].reshape(dms, d))
            early_step = {n: _adamw_layer(weights[n], early_grad[n], mom1[n], mom2[n], last, None,
                                          f"adamw_{n}_{last}", after=token) for n in ("w_in", "w_out")}
            bufs = _copies_wait(bufs, ssem, rsem, [o[0] for o in early_step.values()], _swap_plan, "late_swap_wait")
            late_sums = [_pair_add(a, b, c_idx, f"pair_add_{k}")
                         for k, (a, b) in enumerate(zip(bufs[:len(srcs)], bufs[len(srcs):]))]
            lands = [lax.empty(p.shape, p.dtype) for p in late_sums]
            ssem, rsem, *bufs, token = _copies_start(late_sums + lands, _scatter_plan, 3 * len(srcs), "late_scatter_start")
        dh, d_g = _in_proj_bwd(du, lw["win"], h_in, lw["g"], dh, f"in_proj_bwd_{l}", after=token)
        if early:
            bufs = _copies_wait(bufs, ssem, rsem, dh, _scatter_plan, "late_scatter_wait")
            late_reduced = [_chip_sum(rc, p, to_core, N_CORES, f"chip_sum_{k}")
                            for k, (p, rc) in enumerate(zip(bufs[:len(srcs)], bufs[len(srcs):]))]
        grads[l] = dict(dsm=dsm, wr=_diag_blocks(d_wr, hd), wi=_diag_blocks(d_wi, hd), g=d_g)
        if l == depth - 1:
            lands = [lax.empty((a.shape[0],) + a.shape[2:], a.dtype) for a in srcs]
            ssem, rsem, *bufs, token = _copies_start(srcs + lands, _swap_plan, len(srcs), "early_swap_start")
            early = (ssem, rsem, bufs, token)
        else:
            early = None
    grad_x = dh[n_meta:t_real][None]

    sharded = []
    sp = jnp.zeros((sm_rows, s, ds), F32)
    sp = sp.at[0:n_meta].set(dh[0:n_meta].reshape(n_meta, s, ds))
    for l in range(depth):
        base = n_meta + l * SUBLANES
        dsm = grads[l]["dsm"]
        sp = sp.at[base:base + ka, :, 0:cs].set(dsm[ROW_DWA:ROW_DWA + ka].reshape(ka, s, cs))
        sp = sp.at[base + ka:base + ka + kb, :, 0:cs].set(dsm[ROW_DWB:ROW_DWB + kb].reshape(kb, s, cs))
    sharded.append(jnp.transpose(sp, (1, 0, 2)).reshape(s, 2, sm_rows // 2, ds))
    rep_parts = [jnp.concatenate([grads[l]["g"].reshape(-1) for l in range(depth)]), d_final_g.reshape(-1)]
    for row in (ROW_DBA, ROW_DBR, ROW_DBI, ROW_DLAM):
        rep_parts.append(jnp.concatenate([grads[l]["dsm"][row] for l in range(depth)]))
    rep_parts.append(jnp.concatenate([grads[l]["wr"].reshape(-1) for l in range(depth)]))
    rep_parts.append(jnp.concatenate([grads[l]["wi"].reshape(-1) for l in range(depth)]))
    rep_sizes = [p.shape[0] for p in rep_parts]
    piece = _round_up(-(-sum(rep_sizes) // (s * 2)), 2 * SUBLANES * LANES)
    flat = jnp.concatenate(rep_parts + [jnp.zeros((s * 2 * piece - sum(rep_sizes),), F32)])
    sharded.append(flat.reshape(s, 2, piece // LANES, LANES))

    from_sibling = _pair_swap(sharded, "small_pair_swap")
    pair_sums = [_pair_add(a, b, c_idx, f"small_pair_add_{k}") for k, (a, b) in enumerate(zip(sharded, from_sibling))]
    by_chip = _chip_scatter(pair_sums)
    to_device = jnp.stack([my_chip, 2 * my_chip + my_c])
    reduced_sp = _chip_sum(by_chip[0], pair_sums[0], to_core, N_CORES, "small_chip_sum")
    reduced_rep = _chip_sum(by_chip[1], pair_sums[1], to_device, N_CHIPS * N_CORES, "chip_sum_rep")
    *full, rep_all = _final_gather(late_reduced + [reduced_sp], reduced_rep)

    g_win = [full[0].reshape(d, ns), early_full[0].reshape(d, ns)]
    g_wout = [full[1].reshape(dms, d), early_full[1].reshape(dms, d)]
    g_sp = full[2].reshape(sm_rows, ds)
    rep_flat = rep_all.reshape(-1)
    rep_out, off = [], 0
    for n in rep_sizes:
        rep_out.append(rep_flat[off:off + n])
        off += n
    grad = dict(
        meta=g_sp[0:n_meta],
        norm_g=rep_out[0].reshape(depth, d),
        w_in=jnp.stack(g_win),
        conv_a_w=jnp.stack([g_sp[n_meta + l * SUBLANES:n_meta + l * SUBLANES + ka, 0:cs] for l in range(depth)]),
        conv_a_b=rep_out[2].reshape(depth, c),
        lru_wr=rep_out[6].reshape(depth, nh, hd, hd),
        lru_br=rep_out[3].reshape(depth, c),
        lru_wi=rep_out[7].reshape(depth, nh, hd, hd),
        lru_bi=rep_out[4].reshape(depth, c),
        lru_lambda=rep_out[5].reshape(depth, c),
        conv_b_w=jnp.stack([g_sp[n_meta + l * SUBLANES + ka:n_meta + l * SUBLANES + ka + kb, 0:cs]
                            for l in range(depth)]),
        w_out=jnp.stack(g_wout),
        final_g=rep_out[1].reshape(d),
    )

    delta, new_m, new_v = {}, {}, {}
    for n, g_first in (("w_in", g_win[0]), ("w_out", g_wout[0])):
        delta[n], new_m[n], new_v[n] = _adamw_layer(weights[n], g_first, mom1[n], mom2[n], 0, early_step[n],
                                                    f"adamw_{n}_0")
    for n in names:
        if n in delta:
            continue
        shape = weights[n].shape
        two_d = (-1, shape[-1]) if len(shape) > 1 else (1, -1)
        if n in ("lru_wr", "lru_wi"):
            two_d = (-1, LANES)
        out = _adamw(weights[n].reshape(two_d), grad[n].reshape(two_d), mom1[n].reshape(two_d),
                     mom2[n].reshape(two_d), f"adamw_{n}")
        delta[n], new_m[n], new_v[n] = (o.reshape(shape) for o in out)

    return (loss, grad_x, *[grad[n] for n in names], *[delta[n] for n in names],
            *[new_m[n] for n in names], *[new_v[n] for n in names])
```

```python
import functools

import jax
import jax.numpy as jnp
from jax import lax
from jax.experimental import pallas as pl
from jax.experimental.pallas import tpu as pltpu

F32 = jnp.float32
BF16 = jnp.bfloat16

RMS_EPS = 1e-6
LRU_C = 8.0
ADAM_LR = 0.001
ADAM_B1 = 0.9
ADAM_B2 = 0.999
ADAM_EPS = 1e-08
ADAM_WD = 0.01
ADAM_STEP = 10

N_CHIPS = 4
N_CORES = 2
VMEM_LIMIT_BYTES = 56 * 1024 * 1024
SUBLANES = 8
LANES = 128
ROW_QUANTUM = 384
MIX_CHUNK = 192
GATE_BLOCK = 256
MESH = pl.DeviceIdType.MESH
ANY = pl.BlockSpec(memory_space=pl.ANY)

NT_DIMS = (((1,), (1,)), ((), ()))
TN_DIMS = (((0,), (0,)), ((), ()))


def _params(sem):
    return pltpu.CompilerParams(dimension_semantics=sem, vmem_limit_bytes=VMEM_LIMIT_BYTES)


def _sig(x):
    return 0.5 * jnp.tanh(0.5 * x) + 0.5


def _row_tile(t):
    return 704 if t % 704 == 0 else 192


def _col_tile(n, prefs):
    for p in prefs:
        if n % p == 0:
            return p
    return n


def _slab_rows(rows, cols):
    if rows * cols * 4 <= 1024 * 1024:
        return rows
    return _col_tile(rows, (256, 128, 64, 32, 16))


def _norm_in(h, g, wg, name):
    t, d = h.shape
    s, _, ns = wg.shape
    tm = 1408 if t % 1408 == 0 else _row_tile(t)
    tn = _col_tile(ns, (768, 384, 128))
    nb = ns // tn

    def body(h_ref, g_ref, w_ref, u_ref, hn_ref):
        @pl.when(pl.program_id(1) == 0)
        def _():
            x = h_ref[...]
            r = lax.rsqrt(jnp.mean(x * x, axis=-1, keepdims=True) + RMS_EPS)
            hn_ref[...] = ((x * r) * g_ref[...]).astype(BF16)

        u_ref[...] = jnp.dot(hn_ref[...], w_ref[...], preferred_element_type=F32)

    return pl.pallas_call(
        body, name=name, grid=(t // tm, s * nb),
        in_specs=[pl.BlockSpec((tm, d), lambda i, n: (i, 0)),
                  pl.BlockSpec((1, d), lambda i, n: (0, 0)),
                  pl.BlockSpec((None, d, tn), lambda i, n: (n // nb, 0, n % nb))],
        out_specs=[pl.BlockSpec((tm, tn), lambda i, n: (i, n)),
                   pl.BlockSpec((tm, d), lambda i, n: (i, 0))],
        out_shape=[jax.ShapeDtypeStruct((t, s * ns), F32), jax.ShapeDtypeStruct((t, d), BF16)],
        compiler_params=_params(("arbitrary", "arbitrary")),
    )(h, g, wg)


def _norm_in_own(h, g, wg, me_idx, name):
    t, d = h.shape
    s, _, ns = wg.shape
    tm = 1408 if t % 1408 == 0 else _row_tile(t)
    tn = _col_tile(ns, (768, 384, 128))
    nb = ns // tn

    def body(m_ref, h_ref, g_ref, w_ref, u_ref, hn_ref):
        @pl.when(pl.program_id(1) == 0)
        def _():
            x = h_ref[...]
            r = lax.rsqrt(jnp.mean(x * x, axis=-1, keepdims=True) + RMS_EPS)
            hn_ref[...] = ((x * r) * g_ref[...]).astype(BF16)

        u_ref[...] = jnp.dot(hn_ref[...], w_ref[...], preferred_element_type=F32)

    return pl.pallas_call(
        body, name=name,
        grid_spec=pltpu.PrefetchScalarGridSpec(
            num_scalar_prefetch=1, grid=(t // tm, nb),
            in_specs=[pl.BlockSpec((tm, d), lambda i, n, m: (i, 0)),
                      pl.BlockSpec((1, d), lambda i, n, m: (0, 0)),
                      pl.BlockSpec((None, d, tn), lambda i, n, m: (m[0], 0, n))],
            out_specs=[pl.BlockSpec((tm, tn), lambda i, n, m: (i, m[0] * nb + n)),
                       pl.BlockSpec((tm, d), lambda i, n, m: (i, 0))]),
        out_shape=[jax.ShapeDtypeStruct((t, s * ns), F32), jax.ShapeDtypeStruct((t, d), BF16)],
        compiler_params=_params(("arbitrary", "arbitrary")),
    )(me_idx, h, g, wg)


def _norm_in_rest(hn, wg, u, me_idx, name, after=None):
    t, d = hn.shape
    s, _, ns = wg.shape
    tm = 1408 if t % 1408 == 0 else _row_tile(t)
    tn = _col_tile(ns, (768, 384, 128))
    nb = ns // tn

    def body(m_ref, hn_ref, w_ref, u_in, u_ref):
        del u_in
        u_ref[...] = jnp.dot(hn_ref[...], w_ref[...], preferred_element_type=F32)

    def shard(n, m):
        return (m[0] + 1 + n // nb) % s

    body, more_specs, more = _behind(body, 4, after)
    return pl.pallas_call(
        body, name=name,
        grid_spec=pltpu.PrefetchScalarGridSpec(
            num_scalar_prefetch=1, grid=(t // tm, (s - 1) * nb),
            in_specs=[pl.BlockSpec((tm, d), lambda i, n, m: (i, 0)),
                      pl.BlockSpec((None, d, tn), lambda i, n, m: (shard(n, m), 0, n % nb)),
                      ANY] + more_specs,
            out_specs=pl.BlockSpec((tm, tn), lambda i, n, m: (i, shard(n, m) * nb + n % nb))),
        out_shape=jax.ShapeDtypeStruct(u.shape, u.dtype),
        input_output_aliases={3: 0},
        compiler_params=_params(("arbitrary", "arbitrary")),
    )(me_idx, hn, wg, u, *more)


def _decay_consts(lam):
    z = -lam
    e = jnp.exp(-jnp.abs(z))
    u = 1.0 + e
    log1p_e = jnp.where(u == 1.0, e, jnp.log(u) * (e / (u - 1.0)))
    sp = jnp.maximum(z, 0.0) + log1p_e
    return -LRU_C * sp, LRU_C * _sig(z)


def _gates(xc, wr_ref, br_ref, wi_ref, bi_ref, c8, j, gb):
    sl = slice(j * gb, (j + 1) * gb)
    x16 = xc.astype(BF16)
    r = _sig(jnp.dot(x16, wr_ref[j], preferred_element_type=F32) + br_ref[:, sl])
    ig = _sig(jnp.dot(x16, wi_ref[j], preferred_element_type=F32) + bi_ref[:, sl])
    la = c8[:, sl] * r
    a = jnp.exp(la)
    sq = jnp.sqrt(-jnp.tanh(la) * (a * a + 1.0))
    return r, ig, a, sq


def _mix_fwd(u, wa, ba, wr, br, wi, bi, lam, wb, name):
    t = u.shape[0]
    c = u.shape[1] // 6
    tc = MIX_CHUNK
    gb = wr.shape[1]
    nblk = c // gb
    ka, kb = wa.shape[0], wb.shape[0]

    def body(u_ref, wa_ref, ba_ref, wr_ref, br_ref, wi_ref, bi_ref, lam_ref, wb_ref,
             y_ref, hs_ref, xa_ext, v_ext, xc_s, a_s, b_s, carry_s):
        @pl.when(pl.program_id(0) == 0)
        def _():
            xa_ext[0:SUBLANES, :] = jnp.zeros((SUBLANES, c), F32)
            v_ext[0:SUBLANES, :] = jnp.zeros((SUBLANES, c), F32)
            carry_s[...] = jnp.zeros_like(carry_s)

        xa_ext[SUBLANES:SUBLANES + tc, :] = u_ref[:, 0:c]
        xc = ba_ref[...]
        for k in range(ka):
            xc = xc + wa_ref[pl.ds(k, 1), :] * xa_ext[pl.ds(SUBLANES - (ka - 1) + k, tc), :]
        xc_s[...] = xc
        c8, _ = _decay_consts(lam_ref[...])
        for j in range(nblk):
            sl = slice(j * gb, (j + 1) * gb)
            xcj = xc_s[:, sl]
            _, ig, a, sq = _gates(xcj, wr_ref, br_ref, wi_ref, bi_ref, c8, j, gb)
            a_s[:, sl] = a
            b_s[:, sl] = sq * (ig * xcj)

        row = lax.broadcasted_iota(jnp.int32, (SUBLANES, c), 0)

        def scan_step(j, _):
            off = pl.multiple_of(j * SUBLANES, SUBLANES)
            av = a_s[pl.ds(off, SUBLANES), :]
            bv = b_s[pl.ds(off, SUBLANES), :]
            for d in (1, 2, 4):
                keep = row >= d
                bsh = jnp.where(keep, pltpu.roll(bv, d, axis=0), 0.0)
                ash = jnp.where(keep, pltpu.roll(av, d, axis=0), 1.0)
                bv = av * bsh + bv
                av = av * ash
            hv = av * carry_s[...] + bv
            hs_ref[pl.ds(off, SUBLANES), :] = hv
            carry_s[...] = hs_ref[pl.ds(off + SUBLANES - 1, 1), :]
            return 0

        lax.fori_loop(0, tc // SUBLANES, scan_step, 0)

        ga = u_ref[:, c:2 * c]
        y_ref[:, 0:c] = (hs_ref[...] * (ga * _sig(ga))).astype(BF16)

        v_ext[SUBLANES:SUBLANES + tc, :] = u_ref[:, 3 * c:4 * c] * u_ref[:, 4 * c:5 * c]
        cv = wb_ref[pl.ds(0, 1), :] * v_ext[pl.ds(SUBLANES - (kb - 1), tc), :]
        for k in range(1, kb):
            cv = cv + wb_ref[pl.ds(k, 1), :] * v_ext[pl.ds(SUBLANES - (kb - 1) + k, tc), :]
        gbv = u_ref[:, 5 * c:6 * c]
        y_ref[:, c:2 * c] = (u_ref[:, 2 * c:3 * c] * cv * (gbv * _sig(gbv))).astype(BF16)

        xa_ext[0:SUBLANES, :] = xa_ext[tc:tc + SUBLANES, :]
        v_ext[0:SUBLANES, :] = v_ext[tc:tc + SUBLANES, :]

    full = lambda shape: pl.BlockSpec(shape, lambda i: (0,) * len(shape))
    return pl.pallas_call(
        body, name=name, grid=(t // tc,),
        in_specs=[pl.BlockSpec((tc, 6 * c), lambda i: (i, 0)),
                  full(wa.shape), full(ba.shape), full(wr.shape), full(br.shape),
                  full(wi.shape), full(bi.shape), full(lam.shape), full(wb.shape)],
        out_specs=[pl.BlockSpec((tc, 2 * c), lambda i: (i, 0)),
                   pl.BlockSpec((tc, c), lambda i: (i, 0))],
        out_shape=[jax.ShapeDtypeStruct((t, 2 * c), BF16), jax.ShapeDtypeStruct((t, c), F32)],
        scratch_shapes=[pltpu.VMEM((tc + SUBLANES, c), F32), pltpu.VMEM((tc + SUBLANES, c), F32),
                        pltpu.VMEM((tc, c), F32), pltpu.VMEM((tc, c), F32), pltpu.VMEM((tc, c), F32),
                        pltpu.VMEM((1, c), F32)],
        compiler_params=_params(("arbitrary",)),
    )(u, wa, ba, wr, br, wi, bi, lam, wb)


ROW_DWA = 0
ROW_DBA = 4
ROW_DBR = 5
ROW_DBI = 6
ROW_DLAM = 7
ROW_DWB = 8
SMALL_ROWS = 16


def _mix_bwd(u, hs, dy, wa, ba, wr, br, wi, bi, lam, wb, name, after=None):
    t = u.shape[0]
    c = u.shape[1] // 6
    tc = MIX_CHUNK
    nt = t // tc
    gb = wr.shape[1]
    nblk = c // gb
    ka, kb = wa.shape[0], wb.shape[0]
    assert ka <= ROW_DBA and kb <= SMALL_ROWS - ROW_DWB
    hb = tc // SUBLANES

    def body(u_ref, uh_ref, hs_ref, hsh_ref, dy_ref, wa_ref, ba_ref, wr_ref, br_ref, wi_ref, bi_ref, lam_ref, wb_ref,
             du_ref, dsm_ref, dwr_ref, dwi_ref,
             xa_ext, v_ext, hs_ext, a_ext, ds_ext, dxc_ext, dcv_ext, xc_s, r_s, i_s, sq_s, g_s, an_s):
        i = pl.program_id(0)
        chunk = nt - 1 - i
        tail = slice(tc, tc + SUBLANES)
        head = slice(0, SUBLANES)

        @pl.when(i == 0)
        def _():
            zero = jnp.zeros((SUBLANES, c), F32)
            a_ext[tail, :] = zero
            ds_ext[tail, :] = zero
            dxc_ext[tail, :] = zero
            dcv_ext[tail, :] = zero
            dsm_ref[...] = jnp.zeros_like(dsm_ref)
            dwr_ref[...] = jnp.zeros_like(dwr_ref)
            dwi_ref[...] = jnp.zeros_like(dwi_ref)

        prev = jnp.where(chunk > 0, 1.0, 0.0)
        xa_ext[head, :] = uh_ref[:, 0:c] * prev
        xa_ext[SUBLANES:SUBLANES + tc, :] = u_ref[:, 0:c]
        v_ext[head, :] = uh_ref[:, 3 * c:4 * c] * uh_ref[:, 4 * c:5 * c] * prev
        v_ext[SUBLANES:SUBLANES + tc, :] = u_ref[:, 3 * c:4 * c] * u_ref[:, 4 * c:5 * c]
        hs_ext[head, :] = hsh_ref[...] * prev
        hs_ext[SUBLANES:SUBLANES + tc, :] = hs_ref[...]

        xc = ba_ref[...]
        for k in range(ka):
            xc = xc + wa_ref[pl.ds(k, 1), :] * xa_ext[pl.ds(SUBLANES - (ka - 1) + k, tc), :]
        xc_s[...] = xc
        c8, dc8 = _decay_consts(lam_ref[...])
        for j in range(nblk):
            sl = slice(j * gb, (j + 1) * gb)
            r, ig, a, sq = _gates(xc_s[:, sl], wr_ref, br_ref, wi_ref, bi_ref, c8, j, gb)
            r_s[:, sl] = r
            i_s[:, sl] = ig
            sq_s[:, sl] = sq
            a_ext[0:tc, sl] = a

        ga = u_ref[:, c:2 * c]
        sga = _sig(ga)
        g_s[...] = dy_ref[:, 0:c] * (ga * sga)
        an_s[...] = a_ext[pl.ds(1, tc), :]

        row = lax.broadcasted_iota(jnp.int32, (SUBLANES, c), 0)

        def scan_step(j, _):
            off = pl.multiple_of(tc - SUBLANES - j * SUBLANES, SUBLANES)
            av = an_s[pl.ds(off, SUBLANES), :]
            bv = g_s[pl.ds(off, SUBLANES), :]
            for d in (1, 2, 4):
                keep = row < SUBLANES - d
                bsh = jnp.where(keep, pltpu.roll(bv, SUBLANES - d, axis=0), 0.0)
                ash = jnp.where(keep, pltpu.roll(av, SUBLANES - d, axis=0), 1.0)
                bv = av * bsh + bv
                av = av * ash
            ds_ext[pl.ds(off, SUBLANES), :] = av * ds_ext[pl.ds(off + SUBLANES, 1), :] + bv
            return 0

        lax.fori_loop(0, tc // SUBLANES, scan_step, 0)

        def acc(row_index, val):
            dsm_ref[pl.ds(row_index, 1), :] += jnp.sum(val, axis=0, keepdims=True)

        def acc_block(row_index, sl, val):
            dsm_ref[pl.ds(row_index, 1), sl] += jnp.sum(val, axis=0, keepdims=True)

        for j in range(nblk):
            sl = slice(j * gb, (j + 1) * gb)
            ds = ds_ext[0:tc, sl]
            hprev = hs_ext[pl.ds(SUBLANES - 1, tc), sl]
            a = a_ext[0:tc, sl]
            sq = sq_s[:, sl]
            ig = i_s[:, sl]
            r = r_s[:, sl]
            xcj = xc_s[:, sl]
            t1 = ds * xcj
            dla = (ds * hprev) * a - (t1 * ig) * ((a * a) / sq)
            acc_block(ROW_DLAM, sl, dla * r)
            dpr = (dla * c8[:, sl]) * (r * (1.0 - r))
            dpi = (t1 * sq) * (ig * (1.0 - ig))
            acc_block(ROW_DBR, sl, dpr)
            acc_block(ROW_DBI, sl, dpi)
            p16 = dpr.astype(BF16)
            q16 = dpi.astype(BF16)
            x16 = xcj.astype(BF16)
            dwr_ref[j] += lax.dot_general(x16, p16, TN_DIMS, preferred_element_type=F32)
            dwi_ref[j] += lax.dot_general(x16, q16, TN_DIMS, preferred_element_type=F32)
            dxc = (ds * (sq * ig)
                   + lax.dot_general(p16, wr_ref[j], NT_DIMS, preferred_element_type=F32)
                   + lax.dot_general(q16, wi_ref[j], NT_DIMS, preferred_element_type=F32))
            dxc_ext[0:tc, sl] = dxc
            acc_block(ROW_DBA, sl, dxc)

        dsilu_a = sga * (1.0 + ga * (1.0 - sga))
        du_ref[:, c:2 * c] = (dy_ref[:, 0:c] * hs_ref[...] * dsilu_a).astype(BF16)

        dxc = dxc_ext[0:tc, :]
        dxa = wa_ref[pl.ds(ka - 1, 1), :] * dxc
        acc(ROW_DWA + ka - 1, dxc * xa_ext[SUBLANES:SUBLANES + tc, :])
        for k in range(ka - 1):
            acc(ROW_DWA + k, dxc * xa_ext[pl.ds(SUBLANES - (ka - 1) + k, tc), :])
            dxa = dxa + wa_ref[pl.ds(k, 1), :] * dxc_ext[pl.ds(ka - 1 - k, tc), :]
        du_ref[:, 0:c] = dxa.astype(BF16)

        cv = wb_ref[pl.ds(0, 1), :] * v_ext[pl.ds(SUBLANES - (kb - 1), tc), :]
        for k in range(1, kb):
            cv = cv + wb_ref[pl.ds(k, 1), :] * v_ext[pl.ds(SUBLANES - (kb - 1) + k, tc), :]
        gbv = u_ref[:, 5 * c:6 * c]
        sgb = _sig(gbv)
        silu_b = gbv * sgb
        dyb = dy_ref[:, c:2 * c]
        gB = u_ref[:, 2 * c:3 * c]
        du_ref[:, 2 * c:3 * c] = (dyb * cv * silu_b).astype(BF16)
        du_ref[:, 5 * c:6 * c] = (dyb * gB * cv * (sgb * (1.0 + gbv * (1.0 - sgb)))).astype(BF16)
        dcv = dyb * gB * silu_b
        dcv_ext[0:tc, :] = dcv
        dv = wb_ref[pl.ds(kb - 1, 1), :] * dcv
        acc(ROW_DWB + kb - 1, dcv * v_ext[SUBLANES:SUBLANES + tc, :])
        for k in range(kb - 1):
            acc(ROW_DWB + k, dcv * v_ext[pl.ds(SUBLANES - (kb - 1) + k, tc), :])
            dv = dv + wb_ref[pl.ds(k, 1), :] * dcv_ext[pl.ds(kb - 1 - k, tc), :]
        du_ref[:, 3 * c:4 * c] = (dv * u_ref[:, 4 * c:5 * c]).astype(BF16)
        du_ref[:, 4 * c:5 * c] = (dv * u_ref[:, 3 * c:4 * c]).astype(BF16)

        a_ext[tail, :] = a_ext[head, :]
        ds_ext[tail, :] = ds_ext[head, :]
        dxc_ext[tail, :] = dxc_ext[head, :]
        dcv_ext[tail, :] = dcv_ext[head, :]

        @pl.when(i == nt - 1)
        def _():
            dsm_ref[pl.ds(ROW_DLAM, 1), :] = dsm_ref[pl.ds(ROW_DLAM, 1), :] * dc8

    full = lambda shape: pl.BlockSpec(shape, lambda i: (0,) * len(shape))
    rev = lambda i: (nt - 1 - i, 0)
    halo = lambda i: (jnp.maximum((nt - 1 - i) * hb - 1, 0), 0)
    ext = pltpu.VMEM((tc + SUBLANES, c), F32)
    blk = pltpu.VMEM((tc, c), F32)
    body, more_specs, more = _behind(body, 13, after)
    return pl.pallas_call(
        body, name=name, grid=(nt,),
        in_specs=[pl.BlockSpec((tc, 6 * c), rev), pl.BlockSpec((SUBLANES, 6 * c), halo),
                  pl.BlockSpec((tc, c), rev), pl.BlockSpec((SUBLANES, c), halo),
                  pl.BlockSpec((tc, 2 * c), rev),
                  full(wa.shape), full(ba.shape), full(wr.shape), full(br.shape),
                  full(wi.shape), full(bi.shape), full(lam.shape), full(wb.shape)] + more_specs,
        out_specs=[pl.BlockSpec((tc, 6 * c), rev), full((SMALL_ROWS, c)), full(wr.shape), full(wi.shape)],
        out_shape=[jax.ShapeDtypeStruct((t, 6 * c), BF16), jax.ShapeDtypeStruct((SMALL_ROWS, c), F32),
                   jax.ShapeDtypeStruct(wr.shape, F32), jax.ShapeDtypeStruct(wi.shape, F32)],
        scratch_shapes=[ext] * 7 + [blk] * 6,
        compiler_params=_params(("arbitrary",)),
    )(u, u, hs, hs, dy, wa, ba, wr, br, wi, bi, lam, wb, *more)


def _behind(body, n_in, after):
    if after is None:
        return body, [], []
    return (lambda *refs: body(*refs[:n_in], *refs[n_in + 1:])), [ANY], [after]


def _out_proj(h, y, w, name, after=None):
    t, d = h.shape
    dm = y.shape[1]
    tm = _row_tile(t)
    tn = _col_tile(d, (1024, 512, 256))

    def body(h_ref, y_ref, w_ref, o_ref):
        o_ref[...] = h_ref[...] + jnp.dot(y_ref[...], w_ref[...], preferred_element_type=F32)

    body, more_specs, more = _behind(body, 3, after)
    return pl.pallas_call(
        body, name=name, grid=(d // tn, t // tm),
        in_specs=[pl.BlockSpec((tm, tn), lambda n, i: (i, n)),
                  pl.BlockSpec((tm, dm), lambda n, i: (i, 0)),
                  pl.BlockSpec((dm, tn), lambda n, i: (0, n))] + more_specs,
        out_specs=pl.BlockSpec((tm, tn), lambda n, i: (i, n)),
        out_shape=jax.ShapeDtypeStruct((t, d), F32),
        compiler_params=_params(("arbitrary", "arbitrary")),
    )(h, y, w, *more)


def _out_proj_dy(dout, w, name, after=None):
    t, d = dout.shape
    dm = w.shape[0]
    tm = _row_tile(t)
    tn = _col_tile(dm, (1024, 512, 256))

    def body(g_ref, w_ref, o_ref):
        o_ref[...] = lax.dot_general(g_ref[...].astype(BF16), w_ref[...], NT_DIMS, preferred_element_type=F32)

    body, more_specs, more = _behind(body, 2, after)
    return pl.pallas_call(
        body, name=name, grid=(dm // tn, t // tm),
        in_specs=[pl.BlockSpec((tm, d), lambda n, i: (i, 0)),
                  pl.BlockSpec((tn, d), lambda n, i: (n, 0))] + more_specs,
        out_specs=pl.BlockSpec((tm, tn), lambda n, i: (i, n)),
        out_shape=jax.ShapeDtypeStruct((t, dm), F32),
        compiler_params=_params(("arbitrary", "arbitrary")),
    )(dout, w, *more)


def _out_proj_dw(y, dout, name):
    t, dm = y.shape
    d = dout.shape[1]
    tmm = _col_tile(dm, (512, 256))
    tn = _col_tile(d, (512, 256))

    def body(y_ref, g_ref, o_ref):
        o_ref[...] = lax.dot_general(y_ref[...], g_ref[...].astype(BF16), TN_DIMS, preferred_element_type=F32)

    return pl.pallas_call(
        body, name=name, grid=(d // tn, dm // tmm),
        in_specs=[pl.BlockSpec((t, tmm), lambda n, m: (0, m)),
                  pl.BlockSpec((t, tn), lambda n, m: (0, n))],
        out_specs=pl.BlockSpec((tmm, tn), lambda n, m: (m, n)),
        out_shape=jax.ShapeDtypeStruct((dm, d), F32),
        compiler_params=_params(("arbitrary", "arbitrary")),
    )(y, dout)


def _in_proj_bwd(du, wg, h, g, dout, name, after=None, split=None):
    t, d = h.shape
    s, _, ns = wg.shape
    tm = _row_tile(t)
    tn = _col_tile(d, (512, 256))

    def mm_body(du_ref, w_ref, o_ref):
        total = lax.dot_general(du_ref[:, 0:ns], w_ref[0], NT_DIMS, preferred_element_type=F32)
        for a in range(1, s):
            total = total + lax.dot_general(du_ref[:, a * ns:(a + 1) * ns], w_ref[a], NT_DIMS,
                                            preferred_element_type=F32)
        o_ref[...] = total

    mm_body, more_specs, more = _behind(mm_body, 2, after)
    dhn = pl.pallas_call(
        mm_body, name=name, grid=(t // tm, d // tn),
        in_specs=[pl.BlockSpec((tm, s * ns), lambda i, n: (i, 0)),
                  pl.BlockSpec((s, tn, ns), lambda i, n: (0, n, 0))] + more_specs,
        out_specs=pl.BlockSpec((tm, tn), lambda i, n: (i, n)),
        out_shape=jax.ShapeDtypeStruct((t, d), F32),
        compiler_params=_params(("arbitrary", "arbitrary")),
    )(du, wg, *more)

    tr = 352 if t % 352 == 0 else 192
    nt = t // tr

    def row_grad(dhn_ref, h_ref, g_ref, dout_ref, dg_ref):
        @pl.when(pl.program_id(0) == 0)
        def _():
            dg_ref[...] = jnp.zeros_like(dg_ref)

        x = h_ref[...]
        dn = dhn_ref[...]
        r = lax.rsqrt(jnp.mean(x * x, axis=-1, keepdims=True) + RMS_EPS)
        gd = dn * g_ref[...]
        dot = jnp.mean(gd * x, axis=-1, keepdims=True)
        dg_ref[...] += jnp.sum(dn * (x * r), axis=0, keepdims=True)
        return dout_ref[...] + (r * gd - x * ((r * r * r) * dot))

    rows = pl.BlockSpec((tr, d), lambda i: (i, 0))
    one = pl.BlockSpec((1, d), lambda i: (0, 0))
    if split is None:
        def norm_body(dhn_ref, h_ref, g_ref, dout_ref, dh_ref, dg_ref):
            dh_ref[...] = row_grad(dhn_ref, h_ref, g_ref, dout_ref, dg_ref)

        return pl.pallas_call(
            norm_body, name=name + "_norm", grid=(nt,),
            in_specs=[rows, rows, one, rows], out_specs=[rows, one],
            out_shape=[jax.ShapeDtypeStruct((t, d), F32), jax.ShapeDtypeStruct((1, d), F32)],
            compiler_params=_params(("arbitrary",)),
        )(dhn, h, g, dout)

    n_head, n_body = split
    n_first = tr - n_head
    n_last = n_head + n_body - (nt - 1) * tr
    assert nt >= 2 and 0 < n_head < tr and 0 < n_last <= tr and n_head % SUBLANES == 0 and n_last % SUBLANES == 0

    def split_body(dhn_ref, h_ref, g_ref, dout_ref, body_ref, head_ref, dg_ref, stage, sems):
        i = pl.program_id(0)
        slot = i % 2

        def first_copy(sl):
            return pltpu.make_async_copy(stage.at[sl, pl.ds(n_head, n_first)], body_ref.at[pl.ds(0, n_first)], sems.at[sl])

        def middle_copy(sl, step):
            start = pl.multiple_of(step * tr - n_head, SUBLANES)
            return pltpu.make_async_copy(stage.at[sl], body_ref.at[pl.ds(start, tr)], sems.at[sl])

        def last_copy(sl):
            return pltpu.make_async_copy(stage.at[sl, pl.ds(0, n_last)],
                                         body_ref.at[pl.ds((nt - 1) * tr - n_head, n_last)], sems.at[sl])

        dh = row_grad(dhn_ref, h_ref, g_ref, dout_ref, dg_ref)

        @pl.when(i == 2)
        def _():
            first_copy(0).wait()

        @pl.when(i > 2)
        def _():
            middle_copy(slot, i - 2).wait()

        stage[slot] = dh

        @pl.when(i == 0)
        def _():
            head_ref[...] = stage[0, 0:n_head, :]
            first_copy(0).start()

        @pl.when((i > 0) & (i < nt - 1))
        def _():
            middle_copy(slot, i).start()

        @pl.when(i == nt - 1)
        def _():
            last = last_copy((nt - 1) % 2)
            last.start()
            if nt == 2:
                first_copy(0).wait()
            else:
                middle_copy((nt - 2) % 2, nt - 2).wait()
            last.wait()

    return pl.pallas_call(
        split_body, name=name + "_norm", grid=(nt,),
        in_specs=[rows, rows, one, rows],
        out_specs=[ANY, pl.BlockSpec((n_head, d), lambda i: (0, 0)), one],
        out_shape=[jax.ShapeDtypeStruct((n_body, d), F32), jax.ShapeDtypeStruct((n_head, d), F32),
                   jax.ShapeDtypeStruct((1, d), F32)],
        scratch_shapes=[pltpu.VMEM((2, tr, d), F32), pltpu.SemaphoreType.DMA((2,))],
        compiler_params=_params(("arbitrary",)),
    )(dhn, h, g, dout)


def _in_proj_dw(hn, du, s, name, after=None):
    t, d = hn.shape
    ns = du.shape[1] // s
    tmm = _col_tile(d, (512, 256))
    tn = _col_tile(ns, (768, 384, 128))
    nb = ns // tn

    def body(hn_ref, du_ref, o_ref):
        o_ref[...] = lax.dot_general(hn_ref[...], du_ref[...], TN_DIMS, preferred_element_type=F32)

    body, more_specs, more = _behind(body, 2, after)
    return pl.pallas_call(
        body, name=name, grid=(s * nb, d // tmm),
        in_specs=[pl.BlockSpec((t, tmm), lambda n, m: (0, m)),
                  pl.BlockSpec((t, tn), lambda n, m: (0, n))] + more_specs,
        out_specs=pl.BlockSpec((None, tmm, tn), lambda n, m: (n // nb, m, n % nb)),
        out_shape=jax.ShapeDtypeStruct((s, d, ns), F32),
        compiler_params=_params(("arbitrary", "arbitrary")),
    )(hn, du, *more)


def _loss_head(h, tgt, g, n_meta, t_real, name):
    t, d = h.shape
    tm = _row_tile(t)

    def body(h_ref, t_ref, g_ref, dh_ref, loss_ref, dg_ref):
        i = pl.program_id(0)

        @pl.when(i == 0)
        def _():
            loss_ref[...] = jnp.zeros_like(loss_ref)
            dg_ref[...] = jnp.zeros_like(dg_ref)

        x = h_ref[...]
        gv = g_ref[...]
        r = lax.rsqrt(jnp.mean(x * x, axis=-1, keepdims=True) + RMS_EPS)
        xr = x * r
        rows = i * tm + lax.broadcasted_iota(jnp.int32, (tm, 1), 0)
        valid = (rows >= n_meta) & (rows < t_real)
        err = jnp.where(valid, xr * gv - t_ref[...], 0.0)
        loss_ref[...] += 0.5 * jnp.sum(jnp.mean(err * err, axis=-1, keepdims=True))
        dy = err * (1.0 / d)
        gd = dy * gv
        dot = jnp.mean(gd * x, axis=-1, keepdims=True)
        dh_ref[...] = r * gd - x * ((r * r * r) * dot)
        dg_ref[...] += jnp.sum(dy * xr, axis=0, keepdims=True)

    return pl.pallas_call(
        body, name=name, grid=(t // tm,),
        in_specs=[pl.BlockSpec((tm, d), lambda i: (i, 0)),
                  pl.BlockSpec((tm, d), lambda i: (i, 0)),
                  pl.BlockSpec((1, d), lambda i: (0, 0))],
        out_specs=[pl.BlockSpec((tm, d), lambda i: (i, 0)),
                   pl.BlockSpec((1, LANES), lambda i: (0, 0)),
                   pl.BlockSpec((1, d), lambda i: (0, 0))],
        out_shape=[jax.ShapeDtypeStruct((t, d), F32), jax.ShapeDtypeStruct((1, LANES), F32),
                   jax.ShapeDtypeStruct((1, d), F32)],
        compiler_params=_params(("arbitrary",)),
    )(h, tgt, g)


def _adamw_rows(rows, cols):
    for cand in (512, 256, 128, 64, 32, 16, 8):
        if rows % cand == 0 and cand * cols * 4 <= 2 * 1024 * 1024:
            return cand
    return rows


def _adamw_math(w_ref, g_ref, m_ref, v_ref, d_ref, nm_ref, nv_ref):
    gv = g_ref[...]
    m2 = ADAM_B1 * m_ref[...] + (1.0 - ADAM_B1) * gv
    v2 = ADAM_B2 * v_ref[...] + (1.0 - ADAM_B2) * (gv * gv)
    m_hat = m2 / (1.0 - ADAM_B1 ** ADAM_STEP)
    v_hat = v2 / (1.0 - ADAM_B2 ** ADAM_STEP)
    d_ref[...] = -ADAM_LR * (m_hat / (jnp.sqrt(v_hat) + ADAM_EPS) + ADAM_WD * w_ref[...])
    nm_ref[...] = m2
    nv_ref[...] = v2


def _adamw(w, g, m, v, name):
    rows, cols = w.shape
    tr = _adamw_rows(rows, cols)

    def body(*refs):
        _adamw_math(*refs)

    spec = pl.BlockSpec((tr, cols), lambda i: (i, 0))
    return pl.pallas_call(
        body, name=name, grid=(rows // tr,),
        in_specs=[spec] * 4, out_specs=[spec] * 3,
        out_shape=[jax.ShapeDtypeStruct((rows, cols), F32)] * 3,
        compiler_params=_params(("arbitrary",)),
    )(w, g, m, v)


def _adamw_layer(w, g, m, v, layer, kept, name, after=None):
    nl, rows, cols = w.shape
    tr = _adamw_rows(rows, cols)
    n_kept = 0 if kept is None else 3

    def body(*refs):
        _adamw_math(*refs[:4], *refs[4 + n_kept:])

    body, more_specs, more = _behind(body, 4 + n_kept, after)
    lay = pl.BlockSpec((None, tr, cols), lambda i: (layer, i, 0))
    return pl.pallas_call(
        body, name=name, grid=(rows // tr,),
        in_specs=[lay, pl.BlockSpec((tr, cols), lambda i: (i, 0)), lay, lay] + [ANY] * n_kept + more_specs,
        out_specs=[lay] * 3,
        out_shape=[jax.ShapeDtypeStruct((nl, rows, cols), F32)] * 3,
        input_output_aliases={4 + k: k for k in range(n_kept)},
        compiler_params=_params(("arbitrary",)),
    )(w, g, m, v, *([] if kept is None else kept), *more)


def _pair_add(x, ra, c_idx, name):
    s, _, rows, cols = x.shape
    tr = _slab_rows(rows, cols)

    def body(c_ref, x_ref, r_ref, o_ref):
        o_ref[...] = (x_ref[...] + r_ref[...]).astype(BF16)

    return pl.pallas_call(
        body, name=name,
        grid_spec=pltpu.PrefetchScalarGridSpec(
            num_scalar_prefetch=1, grid=(s, rows // tr),
            in_specs=[pl.BlockSpec((None, None, tr, cols), lambda a, i, c_ref: (a, c_ref[0], i, 0)),
                      pl.BlockSpec((None, tr, cols), lambda a, i, c_ref: (a, i, 0))],
            out_specs=pl.BlockSpec((None, tr, cols), lambda a, i, c_ref: (a, i, 0))),
        out_shape=jax.ShapeDtypeStruct((s, rows, cols), BF16),
        compiler_params=_params(("arbitrary", "arbitrary")),
    )(c_idx, x, ra)


def _chip_sum(rc, p, where, n_slots, name):
    s, rows, cols = rc.shape
    tr = _slab_rows(rows, cols)

    def body(w_ref, x_ref, p_ref, o_ref):
        me = w_ref[0]
        total = jnp.where(me == 0, p_ref[...], x_ref[0]).astype(F32)
        for a in range(1, s):
            total = total + jnp.where(me == a, p_ref[...], x_ref[a]).astype(F32)
        o_ref[...] = total

    return pl.pallas_call(
        body, name=name,
        grid_spec=pltpu.PrefetchScalarGridSpec(
            num_scalar_prefetch=1, grid=(rows // tr,),
            in_specs=[pl.BlockSpec((s, tr, cols), lambda i, w_ref: (0, i, 0)),
                      pl.BlockSpec((None, tr, cols), lambda i, w_ref: (w_ref[0], i, 0))],
            out_specs=pl.BlockSpec((None, tr, cols), lambda i, w_ref: (w_ref[1], i, 0))),
        out_shape=jax.ShapeDtypeStruct((n_slots, rows, cols), F32),
        compiler_params=_params(("arbitrary",)),
    )(where, rc, p)


def _cast_place(w, layer, me_idx, name, after=None):
    _, rows, cols = w.shape
    tr = _slab_rows(rows, cols)

    def body(m_ref, w_ref, o_ref):
        o_ref[...] = w_ref[...].astype(BF16)

    body, more_specs, more = _behind(body, 2, after)
    return pl.pallas_call(
        body, name=name,
        grid_spec=pltpu.PrefetchScalarGridSpec(
            num_scalar_prefetch=1, grid=(rows // tr,),
            in_specs=[pl.BlockSpec((None, tr, cols), lambda i, m_ref: (layer, i, 0))] + more_specs,
            out_specs=pl.BlockSpec((None, tr, cols), lambda i, m_ref: (m_ref[0], i, 0))),
        out_shape=jax.ShapeDtypeStruct((N_CHIPS, rows, cols), BF16),
        compiler_params=_params(("arbitrary",)),
    )(me_idx, w, *more)


def _place():
    x, y, c = lax.axis_index("x"), lax.axis_index("y"), lax.axis_index("c")
    chips = [(1 - x, y), (x, 1 - y), (1 - x, 1 - y)]
    return x, y, c, chips


def _chip_index(cx, cy):
    return 2 * cx + cy


def _gather_copies(bufs, stage):
    x, y, c, chips = _place()
    me = _chip_index(x, y)
    copies = []
    for b in bufs:
        for chip in chips:
            src = _chip_index(*chip)
            if stage == 0:
                copies.append((b.at[me, c], (*chip, c), b.at[src, c]))
            else:
                copies.append((b.at[src, c], (x, y, 1 - c), b.at[src, 1 - c]))
    return copies


def _remote(ref, peer, ssem, rsem, k):
    return pltpu.make_async_remote_copy(src_ref=ref, dst_ref=ref, send_sem=ssem.at[k], recv_sem=rsem.at[k],
                                        device_id=peer, device_id_type=MESH)


def _gather_first(bufs, small):
    n = len(bufs)
    k = 3 * n

    def body(*refs):
        sm_ref = refs[n]
        b_refs, smg_ref = refs[n + 1:2 * n + 1], refs[2 * n + 1]
        lsem, ssem, rsem = refs[2 * n + 2:]
        x, y, c, chips = _place()
        me = _chip_index(x, y)
        local = pltpu.make_async_copy(sm_ref, smg_ref.at[me], lsem)
        local.start()
        first = _gather_copies(b_refs, 0)
        second = _gather_copies(b_refs, 1)
        started = []
        for i, (ref, peer, _) in enumerate(first):
            started.append(_remote(ref, peer, ssem, rsem, i))
        for j, chip in enumerate(chips):
            started.append(pltpu.make_async_remote_copy(
                src_ref=sm_ref, dst_ref=smg_ref.at[me], send_sem=ssem.at[2 * k + j], recv_sem=rsem.at[2 * k + j],
                device_id=(*chip, c), device_id_type=MESH))
        for cp in started:
            cp.start()
        for i, (_, peer, lands) in enumerate(first):
            _remote(lands, peer, ssem, rsem, i).wait_recv()
            ref, sib, _ = second[i]
            fwd = _remote(ref, sib, ssem, rsem, k + i)
            fwd.start()
            started.append(fwd)
        for i, (_, sib, lands) in enumerate(second):
            _remote(lands, sib, ssem, rsem, k + i).wait_recv()
        for j, chip in enumerate(chips):
            theirs = smg_ref.at[_chip_index(*chip)]
            pltpu.make_async_remote_copy(src_ref=theirs, dst_ref=theirs, send_sem=ssem.at[2 * k + j],
                                         recv_sem=rsem.at[2 * k + j], device_id=(*chip, c),
                                         device_id_type=MESH).wait_recv()
        for cp in started:
            cp.wait_send()
        local.wait()

    return pl.pallas_call(
        body, name="gather_first",
        in_specs=[ANY] * (n + 1), out_specs=[ANY] * (n + 1),
        out_shape=[jax.ShapeDtypeStruct(b.shape, b.dtype) for b in bufs]
        + [jax.ShapeDtypeStruct((N_CHIPS,) + small.shape, small.dtype)],
        input_output_aliases={i: i for i in range(n)},
        scratch_shapes=[pltpu.SemaphoreType.DMA, pltpu.SemaphoreType.DMA((2 * k + 3,)),
                        pltpu.SemaphoreType.DMA((2 * k + 3,))],
    )(*bufs, small)


HBM = pl.BlockSpec(memory_space=pltpu.HBM)
SEM = pl.BlockSpec(memory_space=pltpu.SEMAPHORE)
DATAFLOW = pltpu.SideEffectType.DATAFLOW_SIDE_EFFECTING


def _copies_start(bufs, plan, n_copies, name, after=None):
    n = len(bufs)
    extra = [] if after is None else [after]

    def body(*refs):
        refs = refs[:n] + refs[n + len(extra):]
        ssem, rsem = refs[n], refs[n + 1]
        b_refs, token = refs[n + 2:2 * n + 2], refs[2 * n + 2]
        copies = plan(b_refs)
        assert len(copies) == n_copies
        for i, (src, dst, peer, _) in enumerate(copies):
            pltpu.make_async_remote_copy(src_ref=src, dst_ref=dst, send_sem=ssem.at[i], recv_sem=rsem.at[i],
                                         device_id=peer, device_id_type=MESH).start()
        token[...] = jnp.zeros_like(token)

    return pl.pallas_call(
        body, name=name,
        out_shape=(pltpu.SemaphoreType.DMA((n_copies,)), pltpu.SemaphoreType.DMA((n_copies,)),
                   *[pltpu.HBM(b.shape, b.dtype) for b in bufs], jax.ShapeDtypeStruct((SUBLANES, LANES), F32)),
        in_specs=[HBM] * n + [ANY] * len(extra),
        out_specs=(SEM, SEM, *[HBM] * n, pl.BlockSpec(memory_space=pltpu.VMEM)),
        input_output_aliases={i: 2 + i for i in range(n)},
        compiler_params=pltpu.CompilerParams(has_side_effects=DATAFLOW),
    )(*[pltpu.with_memory_space_constraint(b, pltpu.HBM) for b in bufs], *extra)


def _copies_wait(bufs, ssem, rsem, after, plan, name):
    n = len(bufs)
    afters = list(after) if isinstance(after, (list, tuple)) else [after]

    def body(*refs):
        b_refs, ssem_ref, rsem_ref = refs[:n], refs[n], refs[n + 1]
        for i, (src, dst, peer, lands) in enumerate(plan(b_refs)):
            pltpu.make_async_remote_copy(src_ref=src, dst_ref=dst, send_sem=ssem_ref.at[i], recv_sem=rsem_ref.at[i],
                                         device_id=peer, device_id_type=MESH).wait_send()
            pltpu.make_async_remote_copy(src_ref=lands, dst_ref=lands, send_sem=ssem_ref.at[i],
                                         recv_sem=rsem_ref.at[i], device_id=peer, device_id_type=MESH).wait_recv()

    return pl.pallas_call(
        body, name=name,
        out_shape=tuple(pltpu.HBM(b.shape, b.dtype) for b in bufs),
        in_specs=[HBM] * n + [SEM, SEM] + [ANY] * len(afters), out_specs=tuple([HBM] * n),
        input_output_aliases={i: i for i in range(n)},
        compiler_params=pltpu.CompilerParams(has_side_effects=DATAFLOW),
    )(*bufs, ssem, rsem, *afters)


def _gather_plan(stage):
    return lambda refs: [(ref, ref, peer, lands) for ref, peer, lands in _gather_copies(refs, stage)]


def _swap_plan(refs):
    n = len(refs) // 2
    x, y, c, _ = _place()
    return [(refs[a].at[:, 1 - c], refs[n + a], (x, y, 1 - c), refs[n + a]) for a in range(n)]


def _scatter_plan(refs):
    n = len(refs) // 2
    x, y, c, chips = _place()
    me = _chip_index(x, y)
    return [(refs[a].at[_chip_index(*chip)], refs[n + a].at[me], (*chip, c), refs[n + a].at[_chip_index(*chip)])
            for a in range(n) for chip in chips]


def _pair_gather_plan(refs):
    x, y, c, _ = _place()
    return [(r.at[c], r.at[c], (x, y, 1 - c), r.at[1 - c]) for r in refs]


def _pair_swap(xs, name):
    n = len(xs)

    def body(*refs):
        x_refs, o_refs, ssem, rsem = refs[:n], refs[n:2 * n], refs[2 * n], refs[2 * n + 1]
        x, y, c, _ = _place()
        copies = [pltpu.make_async_remote_copy(src_ref=x_refs[a].at[:, 1 - c], dst_ref=o_refs[a],
                                               send_sem=ssem.at[a], recv_sem=rsem.at[a],
                                               device_id=(x, y, 1 - c), device_id_type=MESH) for a in range(n)]
        for cp in copies:
            cp.start()
        for cp in copies:
            cp.wait()

    return pl.pallas_call(
        body, name=name, in_specs=[ANY] * n, out_specs=[ANY] * n,
        out_shape=[jax.ShapeDtypeStruct((a.shape[0],) + a.shape[2:], a.dtype) for a in xs],
        scratch_shapes=[pltpu.SemaphoreType.DMA((n,)), pltpu.SemaphoreType.DMA((n,))],
    )(*xs)


def _chip_scatter(ps):
    n = len(ps)

    def body(*refs):
        p_refs, o_refs, ssem, rsem = refs[:n], refs[n:2 * n], refs[2 * n], refs[2 * n + 1]
        x, y, c, chips = _place()
        me = _chip_index(x, y)
        sends = []
        for a in range(n):
            for j, chip in enumerate(chips):
                sends.append(pltpu.make_async_remote_copy(
                    src_ref=p_refs[a].at[_chip_index(*chip)], dst_ref=o_refs[a].at[me],
                    send_sem=ssem.at[3 * a + j], recv_sem=rsem.at[3 * a + j],
                    device_id=(*chip, c), device_id_type=MESH))
        for cp in sends:
            cp.start()
        for a in range(n):
            for j, chip in enumerate(chips):
                src = _chip_index(*chip)
                pltpu.make_async_remote_copy(
                    src_ref=p_refs[a].at[src], dst_ref=o_refs[a].at[src],
                    send_sem=ssem.at[3 * a + j], recv_sem=rsem.at[3 * a + j],
                    device_id=(*chip, c), device_id_type=MESH).wait_recv()
        for cp in sends:
            cp.wait_send()

    return pl.pallas_call(
        body, name="chip_scatter", in_specs=[ANY] * n, out_specs=[ANY] * n,
        out_shape=[jax.ShapeDtypeStruct(a.shape, a.dtype) for a in ps],
        scratch_shapes=[pltpu.SemaphoreType.DMA((3 * n,)), pltpu.SemaphoreType.DMA((3 * n,))],
    )(*ps)


def _final_gather(fs, rep):
    n = len(fs)

    def body(*refs):
        o_refs, repo_ref = refs[n + 1:2 * n + 1], refs[2 * n + 1]
        ssem, rsem = refs[2 * n + 2:]
        x, y, c, chips = _place()
        slot = 4 * x + 2 * y + c
        copies = [pltpu.make_async_remote_copy(src_ref=o_refs[a].at[c], dst_ref=o_refs[a].at[c],
                                               send_sem=ssem.at[a], recv_sem=rsem.at[a],
                                               device_id=(x, y, 1 - c), device_id_type=MESH) for a in range(n)]
        peers = [(x, y, 1 - c)] + [(*chip, c) for chip in chips] + [(*chip, 1 - c) for chip in chips]
        for k, peer in enumerate(peers):
            copies.append(pltpu.make_async_remote_copy(src_ref=repo_ref.at[slot], dst_ref=repo_ref.at[slot],
                                                       send_sem=ssem.at[n + k], recv_sem=rsem.at[n + k],
                                                       device_id=peer, device_id_type=MESH))
        for cp in copies:
            cp.start()
        for a in range(n):
            pltpu.make_async_remote_copy(src_ref=o_refs[a].at[1 - c], dst_ref=o_refs[a].at[1 - c],
                                         send_sem=ssem.at[a], recv_sem=rsem.at[a],
                                         device_id=(x, y, 1 - c), device_id_type=MESH).wait_recv()
        for k, peer in enumerate(peers):
            px, py, pc = peer
            theirs = repo_ref.at[4 * px + 2 * py + pc]
            pltpu.make_async_remote_copy(src_ref=theirs, dst_ref=theirs, send_sem=ssem.at[n + k], recv_sem=rsem.at[n + k],
                                         device_id=peer, device_id_type=MESH).wait_recv()
        for cp in copies:
            cp.wait_send()

    return pl.pallas_call(
        body, name="final_gather", in_specs=[ANY] * (n + 1), out_specs=[ANY] * (n + 1),
        out_shape=[jax.ShapeDtypeStruct(a.shape, a.dtype) for a in fs] + [jax.ShapeDtypeStruct(rep.shape, rep.dtype)],
        input_output_aliases={k: k for k in range(n + 1)},
        scratch_shapes=[pltpu.SemaphoreType.DMA((n + 7,)), pltpu.SemaphoreType.DMA((n + 7,))],
    )(*fs, rep)


def _block_diag(w, gb):
    nh, hd, _ = w.shape
    per = gb // hd
    w4 = w.reshape(nh // per, per, hd, hd)
    eye = jnp.eye(per, dtype=w.dtype)
    return jnp.einsum("jaik,ab->jaibk", w4, eye).reshape(nh // per, gb, gb)


def _diag_blocks(dense, hd):
    nj, gb, _ = dense.shape
    per = gb // hd
    d5 = dense.reshape(nj, per, hd, per, hd)
    return jnp.stack([d5[:, a, :, a, :] for a in range(per)], axis=1).reshape(nj * per, hd, hd)


def _round_up(n, q):
    return (n + q - 1) // q * q


def kernel(x, meta, norm_g, w_in, conv_a_w, conv_a_b, lru_wr, lru_br, lru_wi, lru_bi, lru_lambda, conv_b_w, w_out, final_g, loss_target, m_meta, m_norm_g, m_w_in, m_conv_a_w, m_conv_a_b, m_lru_wr, m_lru_br, m_lru_wi, m_lru_bi, m_lru_lambda, m_conv_b_w, m_w_out, m_final_g, v_meta, v_norm_g, v_w_in, v_conv_a_w, v_conv_a_b, v_lru_wr, v_lru_br, v_lru_wi, v_lru_bi, v_lru_lambda, v_conv_b_w, v_w_out, v_final_g):
    weights = dict(meta=meta, norm_g=norm_g, w_in=w_in, conv_a_w=conv_a_w, conv_a_b=conv_a_b, lru_wr=lru_wr,
                   lru_br=lru_br, lru_wi=lru_wi, lru_bi=lru_bi, lru_lambda=lru_lambda, conv_b_w=conv_b_w,
                   w_out=w_out, final_g=final_g)
    mom1 = dict(meta=m_meta, norm_g=m_norm_g, w_in=m_w_in, conv_a_w=m_conv_a_w, conv_a_b=m_conv_a_b,
                lru_wr=m_lru_wr, lru_br=m_lru_br, lru_wi=m_lru_wi, lru_bi=m_lru_bi, lru_lambda=m_lru_lambda,
                conv_b_w=m_conv_b_w, w_out=m_w_out, final_g=m_final_g)
    mom2 = dict(meta=v_meta, norm_g=v_norm_g, w_in=v_w_in, conv_a_w=v_conv_a_w, conv_a_b=v_conv_a_b,
                lru_wr=v_lru_wr, lru_br=v_lru_br, lru_wi=v_lru_wi, lru_bi=v_lru_bi, lru_lambda=v_lru_lambda,
                conv_b_w=v_conv_b_w, w_out=v_w_out, final_g=v_final_g)
    names = list(weights)

    assert x.shape[0] == 1
    seq, d = x.shape[1], x.shape[2]
    n_meta, ds = meta.shape
    depth = norm_g.shape[0]
    c = lru_lambda.shape[1]
    nh, hd = lru_wr.shape[1], lru_wr.shape[2]
    ns = w_in.shape[2]
    dms = w_out.shape[1]
    cs = conv_a_w.shape[2]
    ka, kb = conv_a_w.shape[1], conv_b_w.shape[1]
    s = N_CHIPS
    assert depth == N_CORES and d == s * ds and c == s * cs and s * ns == 6 * c and s * dms == 2 * c
    gb = min(GATE_BLOCK, c)
    t_real = n_meta + seq
    t = _round_up(t_real, ROW_QUANTUM)
    my_c = lax.axis_index("c").astype(jnp.int32)
    my_chip = (2 * lax.axis_index("x") + lax.axis_index("y")).astype(jnp.int32)
    c_idx = my_c.reshape(1)
    chip_idx = my_chip.reshape(1)

    sm_rows = _round_up(n_meta + depth * SUBLANES, 2 * SUBLANES)
    small = jnp.zeros((sm_rows, ds), F32)
    small = small.at[0:n_meta, :].set(meta)
    for l in range(depth):
        base = n_meta + l * SUBLANES
        small = small.at[base:base + ka, 0:cs].set(conv_a_w[l])
        small = small.at[base + ka:base + ka + kb, 0:cs].set(conv_b_w[l])
    (small_g,) = _gather_first([], small)
    meta_full = jnp.transpose(small_g[:, 0:n_meta, :], (1, 0, 2)).reshape(n_meta, d)
    wa_full, wb_full = [], []
    for l in range(depth):
        base = n_meta + l * SUBLANES
        wa_full.append(jnp.transpose(small_g[:, base:base + ka, 0:cs], (1, 0, 2)).reshape(ka, c))
        wb_full.append(jnp.transpose(small_g[:, base + ka:base + ka + kb, 0:cs], (1, 0, 2)).reshape(kb, c))
    win0 = _cast_place(w_in, 0, chip_idx, "cast_w_in_0").reshape(s, 2, d // 2, ns)
    ssem_w, rsem_w, win0, token_w = _copies_start([win0], _gather_plan(0), 3, "gather_win0_ici_start", after=small_g)
    win_b = [None] + [_cast_place(w_in, l, chip_idx, f"cast_w_in_{l}", after=token_w).reshape(s, 2, d // 2, ns)
                      for l in range(1, depth)]
    wout_b = [_cast_place(w_out, l, chip_idx, f"cast_w_out_{l}", after=token_w).reshape(s, 2, dms // 2, d)
              for l in range(depth)]
    h = jnp.concatenate([meta_full, x[0], jnp.zeros((t - t_real, d), F32)], axis=0) + token_w[0, 0]
    tgt = jnp.concatenate([jnp.zeros((n_meta, d), F32), loss_target[0], jnp.zeros((t - t_real, d), F32)],
                          axis=0) + token_w[0, 0]
    u_own, hn_own = _norm_in_own(h, norm_g[0].reshape(1, d), win0.reshape(s, d, ns), chip_idx, "norm_in_0_own")
    (win0,) = _copies_wait([win0], ssem_w, rsem_w, [u_own, tgt] + win_b[1:] + wout_b, _gather_plan(0),
                           "gather_win0_ici_wait")
    ssem_w, rsem_w, win0, token_w = _copies_start([win0], _gather_plan(1), 3, "gather_win0_d2d_start")
    def travel(buf, stage, tag, after):
        return _copies_start([buf], _gather_plan(stage), 3, f"gather_{tag}_{'d2d' if stage else 'ici'}_start",
                             after=after)

    def arrived(state, stage, tag, after):
        (buf,) = _copies_wait([state[2]], state[0], state[1], after, _gather_plan(stage),
                              f"gather_{tag}_{'d2d' if stage else 'ici'}_wait")
        return buf

    on_wout0 = travel(wout_b[0], 0, "wout0", token_w)
    on_win1 = travel(win_b[1], 0, "win1", on_wout0[3])
    on_wout1 = travel(wout_b[1], 0, "wout1", on_win1[3])
    token = on_wout1[3]
    (win_b[0],) = _copies_wait([win0], ssem_w, rsem_w, token, _gather_plan(1), "gather_win0_d2d_wait")

    layer_w = []
    for l in range(depth):
        layer_w.append(dict(
            g=norm_g[l].reshape(1, d), wa=wa_full[l], ba=conv_a_b[l].reshape(1, c),
            wr=_block_diag(lru_wr[l], gb).astype(BF16), br=lru_br[l].reshape(1, c),
            wi=_block_diag(lru_wi[l], gb).astype(BF16), bi=lru_bi[l].reshape(1, c),
            lam=lru_lambda[l].reshape(1, c), wb=wb_full[l]))
    saved = []
    for l, lw in enumerate(layer_w):
        first = l == 0
        lw["win"] = win_b[l].reshape(s, d, ns)
        if first:
            u = _norm_in_rest(hn_own, lw["win"], u_own, chip_idx, "norm_in_0_rest", after=token)
            hn = hn_own
            on_wout0 = travel(arrived(on_wout0, 0, "wout0", u), 1, "wout0", None)
            token = on_wout0[3]
        else:
            u, hn = _norm_in(h, lw["g"] + token[0, 0], lw["win"], f"norm_in_{l}")
            wout_b[1] = arrived(on_wout1, 1, "wout1", u)
        y, hs = _mix_fwd(u, lw["wa"], lw["ba"] + token[0, 0] if first else lw["ba"], lw["wr"], lw["br"], lw["wi"],
                         lw["bi"], lw["lam"], lw["wb"], f"mix_fwd_{l}")
        token = None
        if first:
            wout_b[0] = arrived(on_wout0, 1, "wout0", y)
            on_win1 = travel(arrived(on_win1, 0, "win1", y), 1, "win1", None)
            token = on_win1[3]
        lw["wout"] = wout_b[l].reshape(2 * c, d)
        saved.append((h, u, hn, y, hs))
        h = _out_proj(h, y, lw["wout"], f"out_proj_{l}", after=token)
        if first:
            win_b[1] = arrived(on_win1, 1, "win1", h)
            on_wout1 = travel(arrived(on_wout1, 0, "wout1", h), 1, "wout1", None)
            token = on_wout1[3]
    dh, loss_lanes, d_final_g = _loss_head(h, tgt, final_g.reshape(1, d), n_meta, t_real, "loss_head")
    loss = lax.psum(loss_lanes[0, 0], ("x", "y", "c"))

    to_core = jnp.stack([my_chip, my_c])
    grads = [None] * depth
    early = None
    for l in reversed(range(depth)):
        lw = layer_w[l]
        h_in, u, hn, y, hs = saved[l]
        token = early[-1] if early else None
        dy = _out_proj_dy(dh, lw["wout"], f"out_proj_dy_{l}", after=token)
        d_wout = _out_proj_dw(y, dh, f"out_proj_dw_{l}")
        if early:
            ssem, rsem, bufs, _ = early
            bufs = _copies_wait(bufs, ssem, rsem, d_wout, _swap_plan, "early_swap_wait")
            half = len(bufs) // 2
            sums = [_pair_add(a, b, c_idx, f"early_pair_add_{k}") for k, (a, b) in enumerate(zip(bufs[:half], bufs[half:]))]
            lands = [lax.empty(p.shape, p.dtype) for p in sums]
            ssem, rsem, *bufs, token = _copies_start(sums + lands, _scatter_plan, 3 * half, "early_scatter_start")
        du, dsm, d_wr, d_wi = _mix_bwd(u, hs, dy, lw["wa"], lw["ba"], lw["wr"], lw["br"], lw["wi"], lw["bi"],
                                       lw["lam"], lw["wb"], f"mix_bwd_{l}", after=token)
        if early:
            bufs = _copies_wait(bufs, ssem, rsem, du, _scatter_plan, "early_scatter_wait")
            halves = [_chip_sum(rc, p, to_core, N_CORES, f"early_chip_sum_{k}")
                      for k, (p, rc) in enumerate(zip(bufs[:half], bufs[half:]))]
            ssem, rsem, *bufs, token = _copies_start(halves, _pair_gather_plan, half, "early_gather_start")
        d_win = _in_proj_dw(hn, du, s, f"in_proj_dw_{l}", after=token)
        srcs = [d_win.reshape(s, 2, d // 2, ns), d_wout.reshape(s, 2, dms // 2, d)]
        if early:
            early_full = _copies_wait(bufs, ssem, rsem, d_win, _pair_gather_plan, "early_gather_wait")
            lands = [lax.empty((a.shape[0],) + a.shape[2:], a.dtype) for a in srcs]
            ssem, rsem, *bufs, token = _copies_start(srcs + lands, _swap_plan, len(srcs), "late_swap_start")
            last = depth - 1
            early_grad = dict(w_in=early_full[0].reshape(d, ns), w_out=early_full[1].reshape(dms, d))
            early_step = {n: _adamw_layer(weights[n], early_grad[n], mom1[n], mom2[n], last, None,
                                          f"adamw_{n}_{last}", after=token) for n in ("w_in", "w_out")}
            bufs = _copies_wait(bufs, ssem, rsem, [o[0] for o in early_step.values()], _swap_plan, "late_swap_wait")
            late_sums = [_pair_add(a, b, c_idx, f"pair_add_{k}")
                         for k, (a, b) in enumerate(zip(bufs[:len(srcs)], bufs[len(srcs):]))]
            lands = [lax.empty(p.shape, p.dtype) for p in late_sums]
            ssem, rsem, *bufs, token = _copies_start(late_sums + lands, _scatter_plan, 3 * len(srcs), "late_scatter_start")
        if l > 0:
            dh, d_g = _in_proj_bwd(du, lw["win"], h_in, lw["g"], dh, f"in_proj_bwd_{l}", after=token)
        else:
            grad_x, d_meta, d_g = _in_proj_bwd(du, lw["win"], h_in, lw["g"], dh, f"in_proj_bwd_{l}", after=token,
                                               split=(n_meta, seq))
        if early:
            bufs = _copies_wait(bufs, ssem, rsem, grad_x, _scatter_plan, "late_scatter_wait")
            late_reduced = [_chip_sum(rc, p, to_core, N_CORES, f"chip_sum_{k}")
                            for k, (p, rc) in enumerate(zip(bufs[:len(srcs)], bufs[len(srcs):]))]
        grads[l] = dict(dsm=dsm, wr=_diag_blocks(d_wr, hd), wi=_diag_blocks(d_wi, hd), g=d_g)
        if l == depth - 1:
            lands = [lax.empty((a.shape[0],) + a.shape[2:], a.dtype) for a in srcs]
            ssem, rsem, *bufs, token = _copies_start(srcs + lands, _swap_plan, len(srcs), "early_swap_start")
            early = (ssem, rsem, bufs, token)
        else:
            early = None
    grad_x = grad_x[None]

    sharded = []
    sp = jnp.zeros((sm_rows, s, ds), F32)
    sp = sp.at[0:n_meta].set(d_meta.reshape(n_meta, s, ds))
    for l in range(depth):
        base = n_meta + l * SUBLANES
        dsm = grads[l]["dsm"]
        sp = sp.at[base:base + ka, :, 0:cs].set(dsm[ROW_DWA:ROW_DWA + ka].reshape(ka, s, cs))
        sp = sp.at[base + ka:base + ka + kb, :, 0:cs].set(dsm[ROW_DWB:ROW_DWB + kb].reshape(kb, s, cs))
    sharded.append(jnp.transpose(sp, (1, 0, 2)).reshape(s, 2, sm_rows // 2, ds))
    rep_parts = [jnp.concatenate([grads[l]["g"].reshape(-1) for l in range(depth)]), d_final_g.reshape(-1)]
    for row in (ROW_DBA, ROW_DBR, ROW_DBI, ROW_DLAM):
        rep_parts.append(jnp.concatenate([grads[l]["dsm"][row] for l in range(depth)]))
    rep_parts.append(jnp.concatenate([grads[l]["wr"].reshape(-1) for l in range(depth)]))
    rep_parts.append(jnp.concatenate([grads[l]["wi"].reshape(-1) for l in range(depth)]))
    rep_sizes = [p.shape[0] for p in rep_parts]
    piece = _round_up(-(-sum(rep_sizes) // (s * 2)), 2 * SUBLANES * LANES)
    flat = jnp.concatenate(rep_parts + [jnp.zeros((s * 2 * piece - sum(rep_sizes),), F32)])
    sharded.append(flat.reshape(s, 2, piece // LANES, LANES))

    from_sibling = _pair_swap(sharded, "small_pair_swap")
    pair_sums = [_pair_add(a, b, c_idx, f"small_pair_add_{k}") for k, (a, b) in enumerate(zip(sharded, from_sibling))]
    by_chip = _chip_scatter(pair_sums)
    to_device = jnp.stack([my_chip, 2 * my_chip + my_c])
    reduced_sp = _chip_sum(by_chip[0], pair_sums[0], to_core, N_CORES, "small_chip_sum")
    reduced_rep = _chip_sum(by_chip[1], pair_sums[1], to_device, N_CHIPS * N_CORES, "chip_sum_rep")
    *full, rep_all = _final_gather(late_reduced + [reduced_sp], reduced_rep)

    g_win = [full[0].reshape(d, ns), early_full[0].reshape(d, ns)]
    g_wout = [full[1].reshape(dms, d), early_full[1].reshape(dms, d)]
    g_sp = full[2].reshape(sm_rows, ds)
    rep_flat = rep_all.reshape(-1)
    rep_out, off = [], 0
    for n in rep_sizes:
        rep_out.append(rep_flat[off:off + n])
        off += n
    grad = dict(
        meta=g_sp[0:n_meta],
        norm_g=rep_out[0].reshape(depth, d),
        w_in=jnp.stack(g_win),
        conv_a_w=jnp.stack([g_sp[n_meta + l * SUBLANES:n_meta + l * SUBLANES + ka, 0:cs] for l in range(depth)]),
        conv_a_b=rep_out[2].reshape(depth, c),
        lru_wr=rep_out[6].reshape(depth, nh, hd, hd),
        lru_br=rep_out[3].reshape(depth, c),
        lru_wi=rep_out[7].reshape(depth, nh, hd, hd),
        lru_bi=rep_out[4].reshape(depth, c),
        lru_lambda=rep_out[5].reshape(depth, c),
        conv_b_w=jnp.stack([g_sp[n_meta + l * SUBLANES + ka:n_meta + l * SUBLANES + ka + kb, 0:cs]
                            for l in range(depth)]),
        w_out=jnp.stack(g_wout),
        final_g=rep_out[1].reshape(d),
    )

    delta, new_m, new_v = {}, {}, {}
    for n, g_first in (("w_in", g_win[0]), ("w_out", g_wout[0])):
        delta[n], new_m[n], new_v[n] = _adamw_layer(weights[n], g_first, mom1[n], mom2[n], 0, early_step[n],
                                                    f"adamw_{n}_0")
    for n in names:
        if n in delta:
            continue
        shape = weights[n].shape
        two_d = (-1, shape[-1]) if len(shape) > 1 else (1, -1)
        if n in ("lru_wr", "lru_wi"):
            two_d = (-1, LANES)
        out = _adamw(weights[n].reshape(two_d), grad[n].reshape(two_d), mom1[n].reshape(two_d),
                     mom2[n].reshape(two_d), f"adamw_{n}")
        delta[n], new_m[n], new_v[n] = (o.reshape(shape) for o in out)

    return (loss, grad_x, *[grad[n] for n in names], *[delta[n] for n in names],
            *[new_m[n] for n in names], *[new_v[n] for n in names])
```

```python
import functools

import jax
import jax.numpy as jnp
from jax import lax
from jax.experimental import pallas as pl
from jax.experimental.pallas import tpu as pltpu

F32 = jnp.float32
BF16 = jnp.bfloat16

RMS_EPS = 1e-6
LRU_C = 8.0
ADAM_LR = 0.001
ADAM_B1 = 0.9
ADAM_B2 = 0.999
ADAM_EPS = 1e-08
ADAM_WD = 0.01
ADAM_STEP = 10

N_CHIPS = 4
N_CORES = 2
VMEM_LIMIT_BYTES = 56 * 1024 * 1024
SUBLANES = 8
LANES = 128
ROW_QUANTUM = 384
MIX_CHUNK = 192
GATE_BLOCK = 256
MESH = pl.DeviceIdType.MESH
ANY = pl.BlockSpec(memory_space=pl.ANY)

NT_DIMS = (((1,), (1,)), ((), ()))
TN_DIMS = (((0,), (0,)), ((), ()))


def _params(sem):
    return pltpu.CompilerParams(dimension_semantics=sem, vmem_limit_bytes=VMEM_LIMIT_BYTES)


def _sig(x):
    return 0.5 * jnp.tanh(0.5 * x) + 0.5


def _row_tile(t):
    return 704 if t % 704 == 0 else 192


def _col_tile(n, prefs):
    for p in prefs:
        if n % p == 0:
            return p
    return n


def _slab_rows(rows, cols):
    if rows * cols * 4 <= 1024 * 1024:
        return rows
    return _col_tile(rows, (256, 128, 64, 32, 16))


def _norm_in(h, g, wg, name):
    t, d = h.shape
    s, _, ns = wg.shape
    tm = 1408 if t % 1408 == 0 else _row_tile(t)
    tn = _col_tile(ns, (768, 384, 128))
    nb = ns // tn

    def body(h_ref, g_ref, w_ref, u_ref, hn_ref):
        @pl.when(pl.program_id(1) == 0)
        def _():
            x = h_ref[...]
            r = lax.rsqrt(jnp.mean(x * x, axis=-1, keepdims=True) + RMS_EPS)
            hn_ref[...] = ((x * r) * g_ref[...]).astype(BF16)

        u_ref[...] = jnp.dot(hn_ref[...], w_ref[...], preferred_element_type=F32)

    return pl.pallas_call(
        body, name=name, grid=(t // tm, s * nb),
        in_specs=[pl.BlockSpec((tm, d), lambda i, n: (i, 0)),
                  pl.BlockSpec((1, d), lambda i, n: (0, 0)),
                  pl.BlockSpec((None, d, tn), lambda i, n: (n // nb, 0, n % nb))],
        out_specs=[pl.BlockSpec((tm, tn), lambda i, n: (i, n)),
                   pl.BlockSpec((tm, d), lambda i, n: (i, 0))],
        out_shape=[jax.ShapeDtypeStruct((t, s * ns), F32), jax.ShapeDtypeStruct((t, d), BF16)],
        compiler_params=_params(("arbitrary", "arbitrary")),
    )(h, g, wg)


def _norm_in_own(h, g, wg, me_idx, name):
    t, d = h.shape
    s, _, ns = wg.shape
    tm = 1408 if t % 1408 == 0 else _row_tile(t)
    tn = _col_tile(ns, (768, 384, 128))
    nb = ns // tn

    def body(m_ref, h_ref, g_ref, w_ref, u_ref, hn_ref):
        @pl.when(pl.program_id(1) == 0)
        def _():
            x = h_ref[...]
            r = lax.rsqrt(jnp.mean(x * x, axis=-1, keepdims=True) + RMS_EPS)
            hn_ref[...] = ((x * r) * g_ref[...]).astype(BF16)

        u_ref[...] = jnp.dot(hn_ref[...], w_ref[...], preferred_element_type=F32)

    return pl.pallas_call(
        body, name=name,
        grid_spec=pltpu.PrefetchScalarGridSpec(
            num_scalar_prefetch=1, grid=(t // tm, nb),
            in_specs=[pl.BlockSpec((tm, d), lambda i, n, m: (i, 0)),
                      pl.BlockSpec((1, d), lambda i, n, m: (0, 0)),
                      pl.BlockSpec((None, d, tn), lambda i, n, m: (m[0], 0, n))],
            out_specs=[pl.BlockSpec((tm, tn), lambda i, n, m: (i, m[0] * nb + n)),
                       pl.BlockSpec((tm, d), lambda i, n, m: (i, 0))]),
        out_shape=[jax.ShapeDtypeStruct((t, s * ns), F32), jax.ShapeDtypeStruct((t, d), BF16)],
        compiler_params=_params(("arbitrary", "arbitrary")),
    )(me_idx, h, g, wg)


def _norm_in_rest(hn, wg, u, me_idx, name, after=None):
    t, d = hn.shape
    s, _, ns = wg.shape
    tm = 1408 if t % 1408 == 0 else _row_tile(t)
    tn = _col_tile(ns, (768, 384, 128))
    nb = ns // tn

    def body(m_ref, hn_ref, w_ref, u_in, u_ref):
        del u_in
        u_ref[...] = jnp.dot(hn_ref[...], w_ref[...], preferred_element_type=F32)

    def shard(n, m):
        return (m[0] + 1 + n // nb) % s

    body, more_specs, more = _behind(body, 4, after)
    return pl.pallas_call(
        body, name=name,
        grid_spec=pltpu.PrefetchScalarGridSpec(
            num_scalar_prefetch=1, grid=(t // tm, (s - 1) * nb),
            in_specs=[pl.BlockSpec((tm, d), lambda i, n, m: (i, 0)),
                      pl.BlockSpec((None, d, tn), lambda i, n, m: (shard(n, m), 0, n % nb)),
                      ANY] + more_specs,
            out_specs=pl.BlockSpec((tm, tn), lambda i, n, m: (i, shard(n, m) * nb + n % nb))),
        out_shape=jax.ShapeDtypeStruct(u.shape, u.dtype),
        input_output_aliases={3: 0},
        compiler_params=_params(("arbitrary", "arbitrary")),
    )(me_idx, hn, wg, u, *more)


def _decay_consts(lam):
    z = -lam
    e = jnp.exp(-jnp.abs(z))
    u = 1.0 + e
    log1p_e = jnp.where(u == 1.0, e, jnp.log(u) * (e / (u - 1.0)))
    sp = jnp.maximum(z, 0.0) + log1p_e
    return -LRU_C * sp, LRU_C * _sig(z)


def _gates(xc, wr_ref, br_ref, wi_ref, bi_ref, c8, j, gb):
    sl = slice(j * gb, (j + 1) * gb)
    x16 = xc.astype(BF16)
    r = _sig(jnp.dot(x16, wr_ref[j], preferred_element_type=F32) + br_ref[:, sl])
    ig = _sig(jnp.dot(x16, wi_ref[j], preferred_element_type=F32) + bi_ref[:, sl])
    la = c8[:, sl] * r
    a = jnp.exp(la)
    sq = jnp.sqrt(-jnp.tanh(la) * (a * a + 1.0))
    return r, ig, a, sq


def _mix_fwd(u, wa, ba, wr, br, wi, bi, lam, wb, name):
    t = u.shape[0]
    c = u.shape[1] // 6
    tc = MIX_CHUNK
    gb = wr.shape[1]
    nblk = c // gb
    ka, kb = wa.shape[0], wb.shape[0]

    def body(u_ref, wa_ref, ba_ref, wr_ref, br_ref, wi_ref, bi_ref, lam_ref, wb_ref,
             y_ref, hs_ref, xa_ext, v_ext, xc_s, a_s, b_s, carry_s):
        @pl.when(pl.program_id(0) == 0)
        def _():
            xa_ext[0:SUBLANES, :] = jnp.zeros((SUBLANES, c), F32)
            v_ext[0:SUBLANES, :] = jnp.zeros((SUBLANES, c), F32)
            carry_s[...] = jnp.zeros_like(carry_s)

        xa_ext[SUBLANES:SUBLANES + tc, :] = u_ref[:, 0:c]
        xc = ba_ref[...]
        for k in range(ka):
            xc = xc + wa_ref[pl.ds(k, 1), :] * xa_ext[pl.ds(SUBLANES - (ka - 1) + k, tc), :]
        xc_s[...] = xc
        c8, _ = _decay_consts(lam_ref[...])
        for j in range(nblk):
            sl = slice(j * gb, (j + 1) * gb)
            xcj = xc_s[:, sl]
            _, ig, a, sq = _gates(xcj, wr_ref, br_ref, wi_ref, bi_ref, c8, j, gb)
            a_s[:, sl] = a
            b_s[:, sl] = sq * (ig * xcj)

        row = lax.broadcasted_iota(jnp.int32, (SUBLANES, c), 0)

        def scan_step(j, _):
            off = pl.multiple_of(j * SUBLANES, SUBLANES)
            av = a_s[pl.ds(off, SUBLANES), :]
            bv = b_s[pl.ds(off, SUBLANES), :]
            for d in (1, 2, 4):
                keep = row >= d
                bsh = jnp.where(keep, pltpu.roll(bv, d, axis=0), 0.0)
                ash = jnp.where(keep, pltpu.roll(av, d, axis=0), 1.0)
                bv = av * bsh + bv
                av = av * ash
            hv = av * carry_s[...] + bv
            hs_ref[pl.ds(off, SUBLANES), :] = hv
            carry_s[...] = hs_ref[pl.ds(off + SUBLANES - 1, 1), :]
            return 0

        lax.fori_loop(0, tc // SUBLANES, scan_step, 0)

        ga = u_ref[:, c:2 * c]
        y_ref[:, 0:c] = (hs_ref[...] * (ga * _sig(ga))).astype(BF16)

        v_ext[SUBLANES:SUBLANES + tc, :] = u_ref[:, 3 * c:4 * c] * u_ref[:, 4 * c:5 * c]
        cv = wb_ref[pl.ds(0, 1), :] * v_ext[pl.ds(SUBLANES - (kb - 1), tc), :]
        for k in range(1, kb):
            cv = cv + wb_ref[pl.ds(k, 1), :] * v_ext[pl.ds(SUBLANES - (kb - 1) + k, tc), :]
        gbv = u_ref[:, 5 * c:6 * c]
        y_ref[:, c:2 * c] = (u_ref[:, 2 * c:3 * c] * cv * (gbv * _sig(gbv))).astype(BF16)

        xa_ext[0:SUBLANES, :] = xa_ext[tc:tc + SUBLANES, :]
        v_ext[0:SUBLANES, :] = v_ext[tc:tc + SUBLANES, :]

    full = lambda shape: pl.BlockSpec(shape, lambda i: (0,) * len(shape))
    return pl.pallas_call(
        body, name=name, grid=(t // tc,),
        in_specs=[pl.BlockSpec((tc, 6 * c), lambda i: (i, 0)),
                  full(wa.shape), full(ba.shape), full(wr.shape), full(br.shape),
                  full(wi.shape), full(bi.shape), full(lam.shape), full(wb.shape)],
        out_specs=[pl.BlockSpec((tc, 2 * c), lambda i: (i, 0)),
                   pl.BlockSpec((tc, c), lambda i: (i, 0))],
        out_shape=[jax.ShapeDtypeStruct((t, 2 * c), BF16), jax.ShapeDtypeStruct((t, c), F32)],
        scratch_shapes=[pltpu.VMEM((tc + SUBLANES, c), F32), pltpu.VMEM((tc + SUBLANES, c), F32),
                        pltpu.VMEM((tc, c), F32), pltpu.VMEM((tc, c), F32), pltpu.VMEM((tc, c), F32),
                        pltpu.VMEM((1, c), F32)],
        compiler_params=_params(("arbitrary",)),
    )(u, wa, ba, wr, br, wi, bi, lam, wb)


ROW_DWA = 0
ROW_DBA = 4
ROW_DBR = 5
ROW_DBI = 6
ROW_DLAM = 7
ROW_DWB = 8
SMALL_ROWS = 16


def _mix_bwd(u, hs, dy, wa, ba, wr, br, wi, bi, lam, wb, name, after=None):
    t = u.shape[0]
    c = u.shape[1] // 6
    tc = MIX_CHUNK
    nt = t // tc
    gb = wr.shape[1]
    nblk = c // gb
    ka, kb = wa.shape[0], wb.shape[0]
    assert ka <= ROW_DBA and kb <= SMALL_ROWS - ROW_DWB
    hb = tc // SUBLANES

    def body(u_ref, uh_ref, hs_ref, hsh_ref, dy_ref, wa_ref, ba_ref, wr_ref, br_ref, wi_ref, bi_ref, lam_ref, wb_ref,
             du_ref, dsm_ref, dwr_ref, dwi_ref,
             xa_ext, v_ext, hs_ext, a_ext, ds_ext, dxc_ext, dcv_ext, xc_s, r_s, i_s, sq_s, g_s, an_s):
        i = pl.program_id(0)
        chunk = nt - 1 - i
        tail = slice(tc, tc + SUBLANES)
        head = slice(0, SUBLANES)

        @pl.when(i == 0)
        def _():
            zero = jnp.zeros((SUBLANES, c), F32)
            a_ext[tail, :] = zero
            ds_ext[tail, :] = zero
            dxc_ext[tail, :] = zero
            dcv_ext[tail, :] = zero
            dsm_ref[...] = jnp.zeros_like(dsm_ref)
            dwr_ref[...] = jnp.zeros_like(dwr_ref)
            dwi_ref[...] = jnp.zeros_like(dwi_ref)

        prev = jnp.where(chunk > 0, 1.0, 0.0)
        xa_ext[head, :] = uh_ref[:, 0:c] * prev
        xa_ext[SUBLANES:SUBLANES + tc, :] = u_ref[:, 0:c]
        v_ext[head, :] = uh_ref[:, 3 * c:4 * c] * uh_ref[:, 4 * c:5 * c] * prev
        v_ext[SUBLANES:SUBLANES + tc, :] = u_ref[:, 3 * c:4 * c] * u_ref[:, 4 * c:5 * c]
        hs_ext[head, :] = hsh_ref[...] * prev
        hs_ext[SUBLANES:SUBLANES + tc, :] = hs_ref[...]

        xc = ba_ref[...]
        for k in range(ka):
            xc = xc + wa_ref[pl.ds(k, 1), :] * xa_ext[pl.ds(SUBLANES - (ka - 1) + k, tc), :]
        xc_s[...] = xc
        c8, dc8 = _decay_consts(lam_ref[...])
        for j in range(nblk):
            sl = slice(j * gb, (j + 1) * gb)
            r, ig, a, sq = _gates(xc_s[:, sl], wr_ref, br_ref, wi_ref, bi_ref, c8, j, gb)
            r_s[:, sl] = r
            i_s[:, sl] = ig
            sq_s[:, sl] = sq
            a_ext[0:tc, sl] = a

        ga = u_ref[:, c:2 * c]
        sga = _sig(ga)
        g_s[...] = dy_ref[:, 0:c] * (ga * sga)
        an_s[...] = a_ext[pl.ds(1, tc), :]

        row = lax.broadcasted_iota(jnp.int32, (SUBLANES, c), 0)

        def scan_step(j, _):
            off = pl.multiple_of(tc - SUBLANES - j * SUBLANES, SUBLANES)
            av = an_s[pl.ds(off, SUBLANES), :]
            bv = g_s[pl.ds(off, SUBLANES), :]
            for d in (1, 2, 4):
                keep = row < SUBLANES - d
                bsh = jnp.where(keep, pltpu.roll(bv, SUBLANES - d, axis=0), 0.0)
                ash = jnp.where(keep, pltpu.roll(av, SUBLANES - d, axis=0), 1.0)
                bv = av * bsh + bv
                av = av * ash
            ds_ext[pl.ds(off, SUBLANES), :] = av * ds_ext[pl.ds(off + SUBLANES, 1), :] + bv
            return 0

        lax.fori_loop(0, tc // SUBLANES, scan_step, 0)

        def acc(row_index, val):
            dsm_ref[pl.ds(row_index, 1), :] += jnp.sum(val, axis=0, keepdims=True)

        def acc_block(row_index, sl, val):
            dsm_ref[pl.ds(row_index, 1), sl] += jnp.sum(val, axis=0, keepdims=True)

        for j in range(nblk):
            sl = slice(j * gb, (j + 1) * gb)
            ds = ds_ext[0:tc, sl]
            hprev = hs_ext[pl.ds(SUBLANES - 1, tc), sl]
            a = a_ext[0:tc, sl]
            sq = sq_s[:, sl]
            ig = i_s[:, sl]
            r = r_s[:, sl]
            xcj = xc_s[:, sl]
            t1 = ds * xcj
            dla = (ds * hprev) * a - (t1 * ig) * ((a * a) / sq)
            acc_block(ROW_DLAM, sl, dla * r)
            dpr = (dla * c8[:, sl]) * (r * (1.0 - r))
            dpi = (t1 * sq) * (ig * (1.0 - ig))
            acc_block(ROW_DBR, sl, dpr)
            acc_block(ROW_DBI, sl, dpi)
            p16 = dpr.astype(BF16)
            q16 = dpi.astype(BF16)
            x16 = xcj.astype(BF16)
            dwr_ref[j] += lax.dot_general(x16, p16, TN_DIMS, preferred_element_type=F32)
            dwi_ref[j] += lax.dot_general(x16, q16, TN_DIMS, preferred_element_type=F32)
            dxc = (ds * (sq * ig)
                   + lax.dot_general(p16, wr_ref[j], NT_DIMS, preferred_element_type=F32)
                   + lax.dot_general(q16, wi_ref[j], NT_DIMS, preferred_element_type=F32))
            dxc_ext[0:tc, sl] = dxc
            acc_block(ROW_DBA, sl, dxc)

        dsilu_a = sga * (1.0 + ga * (1.0 - sga))
        du_ref[:, c:2 * c] = (dy_ref[:, 0:c] * hs_ref[...] * dsilu_a).astype(BF16)

        dxc = dxc_ext[0:tc, :]
        dxa = wa_ref[pl.ds(ka - 1, 1), :] * dxc
        acc(ROW_DWA + ka - 1, dxc * xa_ext[SUBLANES:SUBLANES + tc, :])
        for k in range(ka - 1):
            acc(ROW_DWA + k, dxc * xa_ext[pl.ds(SUBLANES - (ka - 1) + k, tc), :])
            dxa = dxa + wa_ref[pl.ds(k, 1), :] * dxc_ext[pl.ds(ka - 1 - k, tc), :]
        du_ref[:, 0:c] = dxa.astype(BF16)

        cv = wb_ref[pl.ds(0, 1), :] * v_ext[pl.ds(SUBLANES - (kb - 1), tc), :]
        for k in range(1, kb):
            cv = cv + wb_ref[pl.ds(k, 1), :] * v_ext[pl.ds(SUBLANES - (kb - 1) + k, tc), :]
        gbv = u_ref[:, 5 * c:6 * c]
        sgb = _sig(gbv)
        silu_b = gbv * sgb
        dyb = dy_ref[:, c:2 * c]
        gB = u_ref[:, 2 * c:3 * c]
        du_ref[:, 2 * c:3 * c] = (dyb * cv * silu_b).astype(BF16)
        du_ref[:, 5 * c:6 * c] = (dyb * gB * cv * (sgb * (1.0 + gbv * (1.0 - sgb)))).astype(BF16)
        dcv = dyb * gB * silu_b
        dcv_ext[0:tc, :] = dcv
        dv = wb_ref[pl.ds(kb - 1, 1), :] * dcv
        acc(ROW_DWB + kb - 1, dcv * v_ext[SUBLANES:SUBLANES + tc, :])
        for k in range(kb - 1):
            acc(ROW_DWB + k, dcv * v_ext[pl.ds(SUBLANES - (kb - 1) + k, tc), :])
            dv = dv + wb_ref[pl.ds(k, 1), :] * dcv_ext[pl.ds(kb - 1 - k, tc), :]
        du_ref[:, 3 * c:4 * c] = (dv * u_ref[:, 4 * c:5 * c]).astype(BF16)
        du_ref[:, 4 * c:5 * c] = (dv * u_ref[:, 3 * c:4 * c]).astype(BF16)

        a_ext[tail, :] = a_ext[head, :]
        ds_ext[tail, :] = ds_ext[head, :]
        dxc_ext[tail, :] = dxc_ext[head, :]
        dcv_ext[tail, :] = dcv_ext[head, :]

        @pl.when(i == nt - 1)
        def _():
            dsm_ref[pl.ds(ROW_DLAM, 1), :] = dsm_ref[pl.ds(ROW_DLAM, 1), :] * dc8

    full = lambda shape: pl.BlockSpec(shape, lambda i: (0,) * len(shape))
    rev = lambda i: (nt - 1 - i, 0)
    halo = lambda i: (jnp.maximum((nt - 1 - i) * hb - 1, 0), 0)
    ext = pltpu.VMEM((tc + SUBLANES, c), F32)
    blk = pltpu.VMEM((tc, c), F32)
    body, more_specs, more = _behind(body, 13, after)
    return pl.pallas_call(
        body, name=name, grid=(nt,),
        in_specs=[pl.BlockSpec((tc, 6 * c), rev), pl.BlockSpec((SUBLANES, 6 * c), halo),
                  pl.BlockSpec((tc, c), rev), pl.BlockSpec((SUBLANES, c), halo),
                  pl.BlockSpec((tc, 2 * c), rev),
                  full(wa.shape), full(ba.shape), full(wr.shape), full(br.shape),
                  full(wi.shape), full(bi.shape), full(lam.shape), full(wb.shape)] + more_specs,
        out_specs=[pl.BlockSpec((tc, 6 * c), rev), full((SMALL_ROWS, c)), full(wr.shape), full(wi.shape)],
        out_shape=[jax.ShapeDtypeStruct((t, 6 * c), BF16), jax.ShapeDtypeStruct((SMALL_ROWS, c), F32),
                   jax.ShapeDtypeStruct(wr.shape, F32), jax.ShapeDtypeStruct(wi.shape, F32)],
        scratch_shapes=[ext] * 7 + [blk] * 6,
        compiler_params=_params(("arbitrary",)),
    )(u, u, hs, hs, dy, wa, ba, wr, br, wi, bi, lam, wb, *more)


def _behind(body, n_in, after):
    if after is None:
        return body, [], []
    return (lambda *refs: body(*refs[:n_in], *refs[n_in + 1:])), [ANY], [after]


def _out_proj(h, y, w, name, after=None):
    t, d = h.shape
    dm = y.shape[1]
    tm = _row_tile(t)
    tn = _col_tile(d, (2048, 1024, 512, 256))

    def body(h_ref, y_ref, w_ref, o_ref):
        o_ref[...] = h_ref[...] + jnp.dot(y_ref[...], w_ref[...], preferred_element_type=F32)

    body, more_specs, more = _behind(body, 3, after)
    return pl.pallas_call(
        body, name=name, grid=(d // tn, t // tm),
        in_specs=[pl.BlockSpec((tm, tn), lambda n, i: (i, n)),
                  pl.BlockSpec((tm, dm), lambda n, i: (i, 0)),
                  pl.BlockSpec((dm, tn), lambda n, i: (0, n))] + more_specs,
        out_specs=pl.BlockSpec((tm, tn), lambda n, i: (i, n)),
        out_shape=jax.ShapeDtypeStruct((t, d), F32),
        compiler_params=_params(("arbitrary", "arbitrary")),
    )(h, y, w, *more)


def _out_proj_dy(dout, w, name, after=None):
    t, d = dout.shape
    dm = w.shape[0]
    tm = _row_tile(t)
    tn = _col_tile(dm, (2048, 1024, 512, 256))

    def body(g_ref, w_ref, o_ref):
        o_ref[...] = lax.dot_general(g_ref[...].astype(BF16), w_ref[...], NT_DIMS, preferred_element_type=F32)

    body, more_specs, more = _behind(body, 2, after)
    return pl.pallas_call(
        body, name=name, grid=(dm // tn, t // tm),
        in_specs=[pl.BlockSpec((tm, d), lambda n, i: (i, 0)),
                  pl.BlockSpec((tn, d), lambda n, i: (n, 0))] + more_specs,
        out_specs=pl.BlockSpec((tm, tn), lambda n, i: (i, n)),
        out_shape=jax.ShapeDtypeStruct((t, dm), F32),
        compiler_params=_params(("arbitrary", "arbitrary")),
    )(dout, w, *more)


def _out_proj_dw(y, dout, name):
    t, dm = y.shape
    d = dout.shape[1]
    tmm = _col_tile(dm, (512, 256))
    tn = _col_tile(d, (512, 256))

    def body(y_ref, g_ref, o_ref):
        o_ref[...] = lax.dot_general(y_ref[...], g_ref[...].astype(BF16), TN_DIMS, preferred_element_type=F32)

    return pl.pallas_call(
        body, name=name, grid=(d // tn, dm // tmm),
        in_specs=[pl.BlockSpec((t, tmm), lambda n, m: (0, m)),
                  pl.BlockSpec((t, tn), lambda n, m: (0, n))],
        out_specs=pl.BlockSpec((tmm, tn), lambda n, m: (m, n)),
        out_shape=jax.ShapeDtypeStruct((dm, d), F32),
        compiler_params=_params(("arbitrary", "arbitrary")),
    )(y, dout)


def _in_proj_bwd(du, wg, h, g, dout, name, after=None, split=None):
    t, d = h.shape
    s, _, ns = wg.shape
    tm = _row_tile(t)
    tn = _col_tile(d, (1024, 512, 256))

    def mm_body(du_ref, w_ref, o_ref):
        total = lax.dot_general(du_ref[:, 0:ns], w_ref[0], NT_DIMS, preferred_element_type=F32)
        for a in range(1, s):
            total = total + lax.dot_general(du_ref[:, a * ns:(a + 1) * ns], w_ref[a], NT_DIMS,
                                            preferred_element_type=F32)
        o_ref[...] = total

    mm_body, more_specs, more = _behind(mm_body, 2, after)
    dhn = pl.pallas_call(
        mm_body, name=name, grid=(t // tm, d // tn),
        in_specs=[pl.BlockSpec((tm, s * ns), lambda i, n: (i, 0)),
                  pl.BlockSpec((s, tn, ns), lambda i, n: (0, n, 0))] + more_specs,
        out_specs=pl.BlockSpec((tm, tn), lambda i, n: (i, n)),
        out_shape=jax.ShapeDtypeStruct((t, d), F32),
        compiler_params=_params(("arbitrary", "arbitrary")),
    )(du, wg, *more)

    tr = 352 if t % 352 == 0 else 192
    nt = t // tr

    def row_grad(dhn_ref, h_ref, g_ref, dout_ref, dg_ref):
        @pl.when(pl.program_id(0) == 0)
        def _():
            dg_ref[...] = jnp.zeros_like(dg_ref)

        x = h_ref[...]
        dn = dhn_ref[...]
        r = lax.rsqrt(jnp.mean(x * x, axis=-1, keepdims=True) + RMS_EPS)
        gd = dn * g_ref[...]
        dot = jnp.mean(gd * x, axis=-1, keepdims=True)
        dg_ref[...] += jnp.sum(dn * (x * r), axis=0, keepdims=True)
        return dout_ref[...] + (r * gd - x * ((r * r * r) * dot))

    rows = pl.BlockSpec((tr, d), lambda i: (i, 0))
    one = pl.BlockSpec((1, d), lambda i: (0, 0))
    if split is None:
        def norm_body(dhn_ref, h_ref, g_ref, dout_ref, dh_ref, dg_ref):
            dh_ref[...] = row_grad(dhn_ref, h_ref, g_ref, dout_ref, dg_ref)

        return pl.pallas_call(
            norm_body, name=name + "_norm", grid=(nt,),
            in_specs=[rows, rows, one, rows], out_specs=[rows, one],
            out_shape=[jax.ShapeDtypeStruct((t, d), F32), jax.ShapeDtypeStruct((1, d), F32)],
            compiler_params=_params(("arbitrary",)),
        )(dhn, h, g, dout)

    n_head, n_body = split
    n_first = tr - n_head
    n_last = n_head + n_body - (nt - 1) * tr
    assert nt >= 2 and 0 < n_head < tr and 0 < n_last <= tr and n_head % SUBLANES == 0 and n_last % SUBLANES == 0

    def split_body(dhn_ref, h_ref, g_ref, dout_ref, body_ref, head_ref, dg_ref, stage, sems):
        i = pl.program_id(0)
        slot = i % 2

        def first_copy(sl):
            return pltpu.make_async_copy(stage.at[sl, pl.ds(n_head, n_first)], body_ref.at[pl.ds(0, n_first)], sems.at[sl])

        def middle_copy(sl, step):
            start = pl.multiple_of(step * tr - n_head, SUBLANES)
            return pltpu.make_async_copy(stage.at[sl], body_ref.at[pl.ds(start, tr)], sems.at[sl])

        def last_copy(sl):
            return pltpu.make_async_copy(stage.at[sl, pl.ds(0, n_last)],
                                         body_ref.at[pl.ds((nt - 1) * tr - n_head, n_last)], sems.at[sl])

        dh = row_grad(dhn_ref, h_ref, g_ref, dout_ref, dg_ref)

        @pl.when(i == 2)
        def _():
            first_copy(0).wait()

        @pl.when(i > 2)
        def _():
            middle_copy(slot, i - 2).wait()

        stage[slot] = dh

        @pl.when(i == 0)
        def _():
            head_ref[...] = stage[0, 0:n_head, :]
            first_copy(0).start()

        @pl.when((i > 0) & (i < nt - 1))
        def _():
            middle_copy(slot, i).start()

        @pl.when(i == nt - 1)
        def _():
            last = last_copy((nt - 1) % 2)
            last.start()
            if nt == 2:
                first_copy(0).wait()
            else:
                middle_copy((nt - 2) % 2, nt - 2).wait()
            last.wait()

    return pl.pallas_call(
        split_body, name=name + "_norm", grid=(nt,),
        in_specs=[rows, rows, one, rows],
        out_specs=[ANY, pl.BlockSpec((n_head, d), lambda i: (0, 0)), one],
        out_shape=[jax.ShapeDtypeStruct((n_body, d), F32), jax.ShapeDtypeStruct((n_head, d), F32),
                   jax.ShapeDtypeStruct((1, d), F32)],
        scratch_shapes=[pltpu.VMEM((2, tr, d), F32), pltpu.SemaphoreType.DMA((2,))],
        compiler_params=_params(("arbitrary",)),
    )(dhn, h, g, dout)


def _in_proj_dw(hn, du, s, name, after=None):
    t, d = hn.shape
    ns = du.shape[1] // s
    tmm = _col_tile(d, (1024, 512, 256))
    tn = _col_tile(ns, (768, 384, 128))
    nb = ns // tn

    def body(hn_ref, du_ref, o_ref):
        o_ref[...] = lax.dot_general(hn_ref[...], du_ref[...], TN_DIMS, preferred_element_type=F32)

    body, more_specs, more = _behind(body, 2, after)
    return pl.pallas_call(
        body, name=name, grid=(s * nb, d // tmm),
        in_specs=[pl.BlockSpec((t, tmm), lambda n, m: (0, m)),
                  pl.BlockSpec((t, tn), lambda n, m: (0, n))] + more_specs,
        out_specs=pl.BlockSpec((None, tmm, tn), lambda n, m: (n // nb, m, n % nb)),
        out_shape=jax.ShapeDtypeStruct((s, d, ns), F32),
        compiler_params=_params(("arbitrary", "arbitrary")),
    )(hn, du, *more)


def _loss_head(h, tgt, g, n_meta, t_real, name):
    t, d = h.shape
    tm = _row_tile(t)

    def body(h_ref, t_ref, g_ref, dh_ref, loss_ref, dg_ref):
        i = pl.program_id(0)

        @pl.when(i == 0)
        def _():
            loss_ref[...] = jnp.zeros_like(loss_ref)
            dg_ref[...] = jnp.zeros_like(dg_ref)

        x = h_ref[...]
        gv = g_ref[...]
        r = lax.rsqrt(jnp.mean(x * x, axis=-1, keepdims=True) + RMS_EPS)
        xr = x * r
        rows = i * tm + lax.broadcasted_iota(jnp.int32, (tm, 1), 0)
        valid = (rows >= n_meta) & (rows < t_real)
        err = jnp.where(valid, xr * gv - t_ref[...], 0.0)
        loss_ref[...] += 0.5 * jnp.sum(jnp.mean(err * err, axis=-1, keepdims=True))
        dy = err * (1.0 / d)
        gd = dy * gv
        dot = jnp.mean(gd * x, axis=-1, keepdims=True)
        dh_ref[...] = r * gd - x * ((r * r * r) * dot)
        dg_ref[...] += jnp.sum(dy * xr, axis=0, keepdims=True)

    return pl.pallas_call(
        body, name=name, grid=(t // tm,),
        in_specs=[pl.BlockSpec((tm, d), lambda i: (i, 0)),
                  pl.BlockSpec((tm, d), lambda i: (i, 0)),
                  pl.BlockSpec((1, d), lambda i: (0, 0))],
        out_specs=[pl.BlockSpec((tm, d), lambda i: (i, 0)),
                   pl.BlockSpec((1, LANES), lambda i: (0, 0)),
                   pl.BlockSpec((1, d), lambda i: (0, 0))],
        out_shape=[jax.ShapeDtypeStruct((t, d), F32), jax.ShapeDtypeStruct((1, LANES), F32),
                   jax.ShapeDtypeStruct((1, d), F32)],
        compiler_params=_params(("arbitrary",)),
    )(h, tgt, g)


def _adamw_rows(rows, cols):
    for cand in (512, 256, 128, 64, 32, 16, 8):
        if rows % cand == 0 and cand * cols * 4 <= 2 * 1024 * 1024:
            return cand
    return rows


def _adamw_math(w_ref, g_ref, m_ref, v_ref, d_ref, nm_ref, nv_ref):
    gv = g_ref[...]
    m2 = ADAM_B1 * m_ref[...] + (1.0 - ADAM_B1) * gv
    v2 = ADAM_B2 * v_ref[...] + (1.0 - ADAM_B2) * (gv * gv)
    m_hat = m2 / (1.0 - ADAM_B1 ** ADAM_STEP)
    v_hat = v2 / (1.0 - ADAM_B2 ** ADAM_STEP)
    d_ref[...] = -ADAM_LR * (m_hat / (jnp.sqrt(v_hat) + ADAM_EPS) + ADAM_WD * w_ref[...])
    nm_ref[...] = m2
    nv_ref[...] = v2


def _adamw(w, g, m, v, name):
    rows, cols = w.shape
    tr = _adamw_rows(rows, cols)

    def body(*refs):
        _adamw_math(*refs)

    spec = pl.BlockSpec((tr, cols), lambda i: (i, 0))
    return pl.pallas_call(
        body, name=name, grid=(rows // tr,),
        in_specs=[spec] * 4, out_specs=[spec] * 3,
        out_shape=[jax.ShapeDtypeStruct((rows, cols), F32)] * 3,
        compiler_params=_params(("arbitrary",)),
    )(w, g, m, v)


def _adamw_layer(w, g, m, v, layer, kept, name, after=None):
    nl, rows, cols = w.shape
    tr = _adamw_rows(rows, cols)
    n_kept = 0 if kept is None else 3

    def body(*refs):
        _adamw_math(*refs[:4], *refs[4 + n_kept:])

    body, more_specs, more = _behind(body, 4 + n_kept, after)
    lay = pl.BlockSpec((None, tr, cols), lambda i: (layer, i, 0))
    return pl.pallas_call(
        body, name=name, grid=(rows // tr,),
        in_specs=[lay, pl.BlockSpec((tr, cols), lambda i: (i, 0)), lay, lay] + [ANY] * n_kept + more_specs,
        out_specs=[lay] * 3,
        out_shape=[jax.ShapeDtypeStruct((nl, rows, cols), F32)] * 3,
        input_output_aliases={4 + k: k for k in range(n_kept)},
        compiler_params=_params(("arbitrary",)),
    )(w, g, m, v, *([] if kept is None else kept), *more)


def _pair_add(x, ra, c_idx, name):
    s, _, rows, cols = x.shape
    tr = _slab_rows(rows, cols)

    def body(c_ref, x_ref, r_ref, o_ref):
        o_ref[...] = (x_ref[...] + r_ref[...]).astype(BF16)

    return pl.pallas_call(
        body, name=name,
        grid_spec=pltpu.PrefetchScalarGridSpec(
            num_scalar_prefetch=1, grid=(s, rows // tr),
            in_specs=[pl.BlockSpec((None, None, tr, cols), lambda a, i, c_ref: (a, c_ref[0], i, 0)),
                      pl.BlockSpec((None, tr, cols), lambda a, i, c_ref: (a, i, 0))],
            out_specs=pl.BlockSpec((None, tr, cols), lambda a, i, c_ref: (a, i, 0))),
        out_shape=jax.ShapeDtypeStruct((s, rows, cols), BF16),
        compiler_params=_params(("arbitrary", "arbitrary")),
    )(c_idx, x, ra)


def _chip_sum(rc, p, where, n_slots, name):
    s, rows, cols = rc.shape
    tr = _slab_rows(rows, cols)

    def body(w_ref, x_ref, p_ref, o_ref):
        me = w_ref[0]
        total = jnp.where(me == 0, p_ref[...], x_ref[0]).astype(F32)
        for a in range(1, s):
            total = total + jnp.where(me == a, p_ref[...], x_ref[a]).astype(F32)
        o_ref[...] = total

    return pl.pallas_call(
        body, name=name,
        grid_spec=pltpu.PrefetchScalarGridSpec(
            num_scalar_prefetch=1, grid=(rows // tr,),
            in_specs=[pl.BlockSpec((s, tr, cols), lambda i, w_ref: (0, i, 0)),
                      pl.BlockSpec((None, tr, cols), lambda i, w_ref: (w_ref[0], i, 0))],
            out_specs=pl.BlockSpec((None, tr, cols), lambda i, w_ref: (w_ref[1], i, 0))),
        out_shape=jax.ShapeDtypeStruct((n_slots, rows, cols), F32),
        compiler_params=_params(("arbitrary",)),
    )(where, rc, p)


def _cast_place(w, layer, me_idx, name, after=None):
    _, rows, cols = w.shape
    tr = _slab_rows(rows, cols)

    def body(m_ref, w_ref, o_ref):
        o_ref[...] = w_ref[...].astype(BF16)

    body, more_specs, more = _behind(body, 2, after)
    return pl.pallas_call(
        body, name=name,
        grid_spec=pltpu.PrefetchScalarGridSpec(
            num_scalar_prefetch=1, grid=(rows // tr,),
            in_specs=[pl.BlockSpec((None, tr, cols), lambda i, m_ref: (layer, i, 0))] + more_specs,
            out_specs=pl.BlockSpec((None, tr, cols), lambda i, m_ref: (m_ref[0], i, 0))),
        out_shape=jax.ShapeDtypeStruct((N_CHIPS, rows, cols), BF16),
        compiler_params=_params(("arbitrary",)),
    )(me_idx, w, *more)


def _place():
    x, y, c = lax.axis_index("x"), lax.axis_index("y"), lax.axis_index("c")
    chips = [(1 - x, y), (x, 1 - y), (1 - x, 1 - y)]
    return x, y, c, chips


def _chip_index(cx, cy):
    return 2 * cx + cy


def _gather_copies(bufs, stage):
    x, y, c, chips = _place()
    me = _chip_index(x, y)
    copies = []
    for b in bufs:
        for chip in chips:
            src = _chip_index(*chip)
            if stage == 0:
                copies.append((b.at[me, c], (*chip, c), b.at[src, c]))
            else:
                copies.append((b.at[src, c], (x, y, 1 - c), b.at[src, 1 - c]))
    return copies


def _remote(ref, peer, ssem, rsem, k):
    return pltpu.make_async_remote_copy(src_ref=ref, dst_ref=ref, send_sem=ssem.at[k], recv_sem=rsem.at[k],
                                        device_id=peer, device_id_type=MESH)


def _gather_first(bufs, small):
    n = len(bufs)
    k = 3 * n

    def body(*refs):
        sm_ref = refs[n]
        b_refs, smg_ref = refs[n + 1:2 * n + 1], refs[2 * n + 1]
        lsem, ssem, rsem = refs[2 * n + 2:]
        x, y, c, chips = _place()
        me = _chip_index(x, y)
        local = pltpu.make_async_copy(sm_ref, smg_ref.at[me], lsem)
        local.start()
        first = _gather_copies(b_refs, 0)
        second = _gather_copies(b_refs, 1)
        started = []
        for i, (ref, peer, _) in enumerate(first):
            started.append(_remote(ref, peer, ssem, rsem, i))
        for j, chip in enumerate(chips):
            started.append(pltpu.make_async_remote_copy(
                src_ref=sm_ref, dst_ref=smg_ref.at[me], send_sem=ssem.at[2 * k + j], recv_sem=rsem.at[2 * k + j],
                device_id=(*chip, c), device_id_type=MESH))
        for cp in started:
            cp.start()
        for i, (_, peer, lands) in enumerate(first):
            _remote(lands, peer, ssem, rsem, i).wait_recv()
            ref, sib, _ = second[i]
            fwd = _remote(ref, sib, ssem, rsem, k + i)
            fwd.start()
            started.append(fwd)
        for i, (_, sib, lands) in enumerate(second):
            _remote(lands, sib, ssem, rsem, k + i).wait_recv()
        for j, chip in enumerate(chips):
            theirs = smg_ref.at[_chip_index(*chip)]
            pltpu.make_async_remote_copy(src_ref=theirs, dst_ref=theirs, send_sem=ssem.at[2 * k + j],
                                         recv_sem=rsem.at[2 * k + j], device_id=(*chip, c),
                                         device_id_type=MESH).wait_recv()
        for cp in started:
            cp.wait_send()
        local.wait()

    return pl.pallas_call(
        body, name="gather_first",
        in_specs=[ANY] * (n + 1), out_specs=[ANY] * (n + 1),
        out_shape=[jax.ShapeDtypeStruct(b.shape, b.dtype) for b in bufs]
        + [jax.ShapeDtypeStruct((N_CHIPS,) + small.shape, small.dtype)],
        input_output_aliases={i: i for i in range(n)},
        scratch_shapes=[pltpu.SemaphoreType.DMA, pltpu.SemaphoreType.DMA((2 * k + 3,)),
                        pltpu.SemaphoreType.DMA((2 * k + 3,))],
    )(*bufs, small)


HBM = pl.BlockSpec(memory_space=pltpu.HBM)
SEM = pl.BlockSpec(memory_space=pltpu.SEMAPHORE)
DATAFLOW = pltpu.SideEffectType.DATAFLOW_SIDE_EFFECTING


def _copies_start(bufs, plan, n_copies, name, after=None):
    n = len(bufs)
    extra = [] if after is None else [after]

    def body(*refs):
        refs = refs[:n] + refs[n + len(extra):]
        ssem, rsem = refs[n], refs[n + 1]
        b_refs, token = refs[n + 2:2 * n + 2], refs[2 * n + 2]
        copies = plan(b_refs)
        assert len(copies) == n_copies
        for i, (src, dst, peer, _) in enumerate(copies):
            pltpu.make_async_remote_copy(src_ref=src, dst_ref=dst, send_sem=ssem.at[i], recv_sem=rsem.at[i],
                                         device_id=peer, device_id_type=MESH).start()
        token[...] = jnp.zeros_like(token)

    return pl.pallas_call(
        body, name=name,
        out_shape=(pltpu.SemaphoreType.DMA((n_copies,)), pltpu.SemaphoreType.DMA((n_copies,)),
                   *[pltpu.HBM(b.shape, b.dtype) for b in bufs], jax.ShapeDtypeStruct((SUBLANES, LANES), F32)),
        in_specs=[HBM] * n + [ANY] * len(extra),
        out_specs=(SEM, SEM, *[HBM] * n, pl.BlockSpec(memory_space=pltpu.VMEM)),
        input_output_aliases={i: 2 + i for i in range(n)},
        compiler_params=pltpu.CompilerParams(has_side_effects=DATAFLOW),
    )(*[pltpu.with_memory_space_constraint(b, pltpu.HBM) for b in bufs], *extra)


def _copies_wait(bufs, ssem, rsem, after, plan, name):
    n = len(bufs)
    afters = list(after) if isinstance(after, (list, tuple)) else [after]

    def body(*refs):
        b_refs, ssem_ref, rsem_ref = refs[:n], refs[n], refs[n + 1]
        for i, (src, dst, peer, lands) in enumerate(plan(b_refs)):
            pltpu.make_async_remote_copy(src_ref=src, dst_ref=dst, send_sem=ssem_ref.at[i], recv_sem=rsem_ref.at[i],
                                         device_id=peer, device_id_type=MESH).wait_send()
            pltpu.make_async_remote_copy(src_ref=lands, dst_ref=lands, send_sem=ssem_ref.at[i],
                                         recv_sem=rsem_ref.at[i], device_id=peer, device_id_type=MESH).wait_recv()

    return pl.pallas_call(
        body, name=name,
        out_shape=tuple(pltpu.HBM(b.shape, b.dtype) for b in bufs),
        in_specs=[HBM] * n + [SEM, SEM] + [ANY] * len(afters), out_specs=tuple([HBM] * n),
        input_output_aliases={i: i for i in range(n)},
        compiler_params=pltpu.CompilerParams(has_side_effects=DATAFLOW),
    )(*bufs, ssem, rsem, *afters)


def _gather_plan(stage):
    return lambda refs: [(ref, ref, peer, lands) for ref, peer, lands in _gather_copies(refs, stage)]


def _swap_plan(refs):
    n = len(refs) // 2
    x, y, c, _ = _place()
    return [(refs[a].at[:, 1 - c], refs[n + a], (x, y, 1 - c), refs[n + a]) for a in range(n)]


def _scatter_plan(refs):
    n = len(refs) // 2
    x, y, c, chips = _place()
    me = _chip_index(x, y)
    return [(refs[a].at[_chip_index(*chip)], refs[n + a].at[me], (*chip, c), refs[n + a].at[_chip_index(*chip)])
            for a in range(n) for chip in chips]


def _pair_gather_plan(refs):
    x, y, c, _ = _place()
    return [(r.at[c], r.at[c], (x, y, 1 - c), r.at[1 - c]) for r in refs]


def _pair_swap(xs, name):
    n = len(xs)

    def body(*refs):
        x_refs, o_refs, ssem, rsem = refs[:n], refs[n:2 * n], refs[2 * n], refs[2 * n + 1]
        x, y, c, _ = _place()
        copies = [pltpu.make_async_remote_copy(src_ref=x_refs[a].at[:, 1 - c], dst_ref=o_refs[a],
                                               send_sem=ssem.at[a], recv_sem=rsem.at[a],
                                               device_id=(x, y, 1 - c), device_id_type=MESH) for a in range(n)]
        for cp in copies:
            cp.start()
        for cp in copies:
            cp.wait()

    return pl.pallas_call(
        body, name=name, in_specs=[ANY] * n, out_specs=[ANY] * n,
        out_shape=[jax.ShapeDtypeStruct((a.shape[0],) + a.shape[2:], a.dtype) for a in xs],
        scratch_shapes=[pltpu.SemaphoreType.DMA((n,)), pltpu.SemaphoreType.DMA((n,))],
    )(*xs)


def _chip_scatter(ps):
    n = len(ps)

    def body(*refs):
        p_refs, o_refs, ssem, rsem = refs[:n], refs[n:2 * n], refs[2 * n], refs[2 * n + 1]
        x, y, c, chips = _place()
        me = _chip_index(x, y)
        sends = []
        for a in range(n):
            for j, chip in enumerate(chips):
                sends.append(pltpu.make_async_remote_copy(
                    src_ref=p_refs[a].at[_chip_index(*chip)], dst_ref=o_refs[a].at[me],
                    send_sem=ssem.at[3 * a + j], recv_sem=rsem.at[3 * a + j],
                    device_id=(*chip, c), device_id_type=MESH))
        for cp in sends:
            cp.start()
        for a in range(n):
            for j, chip in enumerate(chips):
                src = _chip_index(*chip)
                pltpu.make_async_remote_copy(
                    src_ref=p_refs[a].at[src], dst_ref=o_refs[a].at[src],
                    send_sem=ssem.at[3 * a + j], recv_sem=rsem.at[3 * a + j],
                    device_id=(*chip, c), device_id_type=MESH).wait_recv()
        for cp in sends:
            cp.wait_send()

    return pl.pallas_call(
        body, name="chip_scatter", in_specs=[ANY] * n, out_specs=[ANY] * n,
        out_shape=[jax.ShapeDtypeStruct(a.shape, a.dtype) for a in ps],
        scratch_shapes=[pltpu.SemaphoreType.DMA((3 * n,)), pltpu.SemaphoreType.DMA((3 * n,))],
    )(*ps)


def _final_gather(fs, rep):
    n = len(fs)

    def body(*refs):
        o_refs, repo_ref = refs[n + 1:2 * n + 1], refs[2 * n + 1]
        ssem, rsem = refs[2 * n + 2:]
        x, y, c, chips = _place()
        slot = 4 * x + 2 * y + c
        copies = [pltpu.make_async_remote_copy(src_ref=o_refs[a].at[c], dst_ref=o_refs[a].at[c],
                                               send_sem=ssem.at[a], recv_sem=rsem.at[a],
                                               device_id=(x, y, 1 - c), device_id_type=MESH) for a in range(n)]
        peers = [(x, y, 1 - c)] + [(*chip, c) for chip in chips] + [(*chip, 1 - c) for chip in chips]
        for k, peer in enumerate(peers):
            copies.append(pltpu.make_async_remote_copy(src_ref=repo_ref.at[slot], dst_ref=repo_ref.at[slot],
                                                       send_sem=ssem.at[n + k], recv_sem=rsem.at[n + k],
                                                       device_id=peer, device_id_type=MESH))
        for cp in copies:
            cp.start()
        for a in range(n):
            pltpu.make_async_remote_copy(src_ref=o_refs[a].at[1 - c], dst_ref=o_refs[a].at[1 - c],
                                         send_sem=ssem.at[a], recv_sem=rsem.at[a],
                                         device_id=(x, y, 1 - c), device_id_type=MESH).wait_recv()
        for k, peer in enumerate(peers):
            px, py, pc = peer
            theirs = repo_ref.at[4 * px + 2 * py + pc]
            pltpu.make_async_remote_copy(src_ref=theirs, dst_ref=theirs, send_sem=ssem.at[n + k], recv_sem=rsem.at[n + k],
                                         device_id=peer, device_id_type=MESH).wait_recv()
        for cp in copies:
            cp.wait_send()

    return pl.pallas_call(
        body, name="final_gather", in_specs=[ANY] * (n + 1), out_specs=[ANY] * (n + 1),
        out_shape=[jax.ShapeDtypeStruct(a.shape, a.dtype) for a in fs] + [jax.ShapeDtypeStruct(rep.shape, rep.dtype)],
        input_output_aliases={k: k for k in range(n + 1)},
        scratch_shapes=[pltpu.SemaphoreType.DMA((n + 7,)), pltpu.SemaphoreType.DMA((n + 7,))],
    )(*fs, rep)


def _block_diag(w, gb):
    nh, hd, _ = w.shape
    per = gb // hd
    w4 = w.reshape(nh // per, per, hd, hd)
    eye = jnp.eye(per, dtype=w.dtype)
    return jnp.einsum("jaik,ab->jaibk", w4, eye).reshape(nh // per, gb, gb)


def _diag_blocks(dense, hd):
    nj, gb, _ = dense.shape
    per = gb // hd
    d5 = dense.reshape(nj, per, hd, per, hd)
    return jnp.stack([d5[:, a, :, a, :] for a in range(per)], axis=1).reshape(nj * per, hd, hd)


def _round_up(n, q):
    return (n + q - 1) // q * q


def kernel(x, meta, norm_g, w_in, conv_a_w, conv_a_b, lru_wr, lru_br, lru_wi, lru_bi, lru_lambda, conv_b_w, w_out, final_g, loss_target, m_meta, m_norm_g, m_w_in, m_conv_a_w, m_conv_a_b, m_lru_wr, m_lru_br, m_lru_wi, m_lru_bi, m_lru_lambda, m_conv_b_w, m_w_out, m_final_g, v_meta, v_norm_g, v_w_in, v_conv_a_w, v_conv_a_b, v_lru_wr, v_lru_br, v_lru_wi, v_lru_bi, v_lru_lambda, v_conv_b_w, v_w_out, v_final_g):
    weights = dict(meta=meta, norm_g=norm_g, w_in=w_in, conv_a_w=conv_a_w, conv_a_b=conv_a_b, lru_wr=lru_wr,
                   lru_br=lru_br, lru_wi=lru_wi, lru_bi=lru_bi, lru_lambda=lru_lambda, conv_b_w=conv_b_w,
                   w_out=w_out, final_g=final_g)
    mom1 = dict(meta=m_meta, norm_g=m_norm_g, w_in=m_w_in, conv_a_w=m_conv_a_w, conv_a_b=m_conv_a_b,
                lru_wr=m_lru_wr, lru_br=m_lru_br, lru_wi=m_lru_wi, lru_bi=m_lru_bi, lru_lambda=m_lru_lambda,
                conv_b_w=m_conv_b_w, w_out=m_w_out, final_g=m_final_g)
    mom2 = dict(meta=v_meta, norm_g=v_norm_g, w_in=v_w_in, conv_a_w=v_conv_a_w, conv_a_b=v_conv_a_b,
                lru_wr=v_lru_wr, lru_br=v_lru_br, lru_wi=v_lru_wi, lru_bi=v_lru_bi, lru_lambda=v_lru_lambda,
                conv_b_w=v_conv_b_w, w_out=v_w_out, final_g=v_final_g)
    names = list(weights)

    assert x.shape[0] == 1
    seq, d = x.shape[1], x.shape[2]
    n_meta, ds = meta.shape
    depth = norm_g.shape[0]
    c = lru_lambda.shape[1]
    nh, hd = lru_wr.shape[1], lru_wr.shape[2]
    ns = w_in.shape[2]
    dms = w_out.shape[1]
    cs = conv_a_w.shape[2]
    ka, kb = conv_a_w.shape[1], conv_b_w.shape[1]
    s = N_CHIPS
    assert depth == N_CORES and d == s * ds and c == s * cs and s * ns == 6 * c and s * dms == 2 * c
    gb = min(GATE_BLOCK, c)
    t_real = n_meta + seq
    t = _round_up(t_real, ROW_QUANTUM)
    my_c = lax.axis_index("c").astype(jnp.int32)
    my_chip = (2 * lax.axis_index("x") + lax.axis_index("y")).astype(jnp.int32)
    c_idx = my_c.reshape(1)
    chip_idx = my_chip.reshape(1)

    sm_rows = _round_up(n_meta + depth * SUBLANES, 2 * SUBLANES)
    small = jnp.zeros((sm_rows, ds), F32)
    small = small.at[0:n_meta, :].set(meta)
    for l in range(depth):
        base = n_meta + l * SUBLANES
        small = small.at[base:base + ka, 0:cs].set(conv_a_w[l])
        small = small.at[base + ka:base + ka + kb, 0:cs].set(conv_b_w[l])
    (small_g,) = _gather_first([], small)
    meta_full = jnp.transpose(small_g[:, 0:n_meta, :], (1, 0, 2)).reshape(n_meta, d)
    wa_full, wb_full = [], []
    for l in range(depth):
        base = n_meta + l * SUBLANES
        wa_full.append(jnp.transpose(small_g[:, base:base + ka, 0:cs], (1, 0, 2)).reshape(ka, c))
        wb_full.append(jnp.transpose(small_g[:, base + ka:base + ka + kb, 0:cs], (1, 0, 2)).reshape(kb, c))
    win0 = _cast_place(w_in, 0, chip_idx, "cast_w_in_0").reshape(s, 2, d // 2, ns)
    ssem_w, rsem_w, win0, token_w = _copies_start([win0], _gather_plan(0), 3, "gather_win0_ici_start", after=small_g)
    win_b = [None] + [_cast_place(w_in, l, chip_idx, f"cast_w_in_{l}", after=token_w).reshape(s, 2, d // 2, ns)
                      for l in range(1, depth)]
    wout_b = [_cast_place(w_out, l, chip_idx, f"cast_w_out_{l}", after=token_w).reshape(s, 2, dms // 2, d)
              for l in range(depth)]
    h = jnp.concatenate([meta_full, x[0], jnp.zeros((t - t_real, d), F32)], axis=0) + token_w[0, 0]
    tgt = jnp.concatenate([jnp.zeros((n_meta, d), F32), loss_target[0], jnp.zeros((t - t_real, d), F32)],
                          axis=0) + token_w[0, 0]
    u_own, hn_own = _norm_in_own(h, norm_g[0].reshape(1, d), win0.reshape(s, d, ns), chip_idx, "norm_in_0_own")
    (win0,) = _copies_wait([win0], ssem_w, rsem_w, [u_own, tgt] + win_b[1:] + wout_b, _gather_plan(0),
                           "gather_win0_ici_wait")
    ssem_w, rsem_w, win0, token_w = _copies_start([win0], _gather_plan(1), 3, "gather_win0_d2d_start")
    def travel(buf, stage, tag, after):
        return _copies_start([buf], _gather_plan(stage), 3, f"gather_{tag}_{'d2d' if stage else 'ici'}_start",
                             after=after)

    def arrived(state, stage, tag, after):
        (buf,) = _copies_wait([state[2]], state[0], state[1], after, _gather_plan(stage),
                              f"gather_{tag}_{'d2d' if stage else 'ici'}_wait")
        return buf

    on_wout0 = travel(wout_b[0], 0, "wout0", token_w)
    on_win1 = travel(win_b[1], 0, "win1", on_wout0[3])
    on_wout1 = travel(wout_b[1], 0, "wout1", on_win1[3])
    token = on_wout1[3]
    (win_b[0],) = _copies_wait([win0], ssem_w, rsem_w, token, _gather_plan(1), "gather_win0_d2d_wait")

    layer_w = []
    for l in range(depth):
        layer_w.append(dict(
            g=norm_g[l].reshape(1, d), wa=wa_full[l], ba=conv_a_b[l].reshape(1, c),
            wr=_block_diag(lru_wr[l], gb).astype(BF16), br=lru_br[l].reshape(1, c),
            wi=_block_diag(lru_wi[l], gb).astype(BF16), bi=lru_bi[l].reshape(1, c),
            lam=lru_lambda[l].reshape(1, c), wb=wb_full[l]))
    saved = []
    for l, lw in enumerate(layer_w):
        first = l == 0
        lw["win"] = win_b[l].reshape(s, d, ns)
        if first:
            u = _norm_in_rest(hn_own, lw["win"], u_own, chip_idx, "norm_in_0_rest", after=token)
            hn = hn_own
            on_wout0 = travel(arrived(on_wout0, 0, "wout0", u), 1, "wout0", None)
            token = on_wout0[3]
        else:
            u, hn = _norm_in(h, lw["g"] + token[0, 0], lw["win"], f"norm_in_{l}")
            wout_b[1] = arrived(on_wout1, 1, "wout1", u)
        y, hs = _mix_fwd(u, lw["wa"], lw["ba"] + token[0, 0] if first else lw["ba"], lw["wr"], lw["br"], lw["wi"],
                         lw["bi"], lw["lam"], lw["wb"], f"mix_fwd_{l}")
        token = None
        if first:
            wout_b[0] = arrived(on_wout0, 1, "wout0", y)
            on_win1 = travel(arrived(on_win1, 0, "win1", y), 1, "win1", None)
            token = on_win1[3]
        lw["wout"] = wout_b[l].reshape(2 * c, d)
        saved.append((h, u, hn, y, hs))
        h = _out_proj(h, y, lw["wout"], f"out_proj_{l}", after=token)
        if first:
            win_b[1] = arrived(on_win1, 1, "win1", h)
            on_wout1 = travel(arrived(on_wout1, 0, "wout1", h), 1, "wout1", None)
            token = on_wout1[3]
    dh, loss_lanes, d_final_g = _loss_head(h, tgt, final_g.reshape(1, d), n_meta, t_real, "loss_head")
    loss = lax.psum(loss_lanes[0, 0], ("x", "y", "c"))

    to_core = jnp.stack([my_chip, my_c])
    grads = [None] * depth
    early = None
    for l in reversed(range(depth)):
        lw = layer_w[l]
        h_in, u, hn, y, hs = saved[l]
        token = early[-1] if early else None
        dy = _out_proj_dy(dh, lw["wout"], f"out_proj_dy_{l}", after=token)
        d_wout = _out_proj_dw(y, dh, f"out_proj_dw_{l}")
        if early:
            ssem, rsem, bufs, _ = early
            bufs = _copies_wait(bufs, ssem, rsem, d_wout, _swap_plan, "early_swap_wait")
            half = len(bufs) // 2
            sums = [_pair_add(a, b, c_idx, f"early_pair_add_{k}") for k, (a, b) in enumerate(zip(bufs[:half], bufs[half:]))]
            lands = [lax.empty(p.shape, p.dtype) for p in sums]
            ssem, rsem, *bufs, token = _copies_start(sums + lands, _scatter_plan, 3 * half, "early_scatter_start")
        du, dsm, d_wr, d_wi = _mix_bwd(u, hs, dy, lw["wa"], lw["ba"], lw["wr"], lw["br"], lw["wi"], lw["bi"],
                                       lw["lam"], lw["wb"], f"mix_bwd_{l}", after=token)
        if early:
            bufs = _copies_wait(bufs, ssem, rsem, du, _scatter_plan, "early_scatter_wait")
            halves = [_chip_sum(rc, p, to_core, N_CORES, f"early_chip_sum_{k}")
                      for k, (p, rc) in enumerate(zip(bufs[:half], bufs[half:]))]
            ssem, rsem, *bufs, token = _copies_start(halves, _pair_gather_plan, half, "early_gather_start")
        d_win = _in_proj_dw(hn, du, s, f"in_proj_dw_{l}", after=token)
        srcs = [d_win.reshape(s, 2, d // 2, ns), d_wout.reshape(s, 2, dms // 2, d)]
        if early:
            early_full = _copies_wait(bufs, ssem, rsem, d_win, _pair_gather_plan, "early_gather_wait")
            lands = [lax.empty((a.shape[0],) + a.shape[2:], a.dtype) for a in srcs]
            ssem, rsem, *bufs, token = _copies_start(srcs + lands, _swap_plan, len(srcs), "late_swap_start")
            last = depth - 1
            early_grad = dict(w_in=early_full[0].reshape(d, ns), w_out=early_full[1].reshape(dms, d))
            early_step = {n: _adamw_layer(weights[n], early_grad[n], mom1[n], mom2[n], last, None,
                                          f"adamw_{n}_{last}", after=token) for n in ("w_in", "w_out")}
            bufs = _copies_wait(bufs, ssem, rsem, [o[0] for o in early_step.values()], _swap_plan, "late_swap_wait")
            late_sums = [_pair_add(a, b, c_idx, f"pair_add_{k}")
                         for k, (a, b) in enumerate(zip(bufs[:len(srcs)], bufs[len(srcs):]))]
            lands = [lax.empty(p.shape, p.dtype) for p in late_sums]
            ssem, rsem, *bufs, token = _copies_start(late_sums + lands, _scatter_plan, 3 * len(srcs), "late_scatter_start")
        if l > 0:
            dh, d_g = _in_proj_bwd(du, lw["win"], h_in, lw["g"], dh, f"in_proj_bwd_{l}", after=token)
        else:
            grad_x, d_meta, d_g = _in_proj_bwd(du, lw["win"], h_in, lw["g"], dh, f"in_proj_bwd_{l}", after=token,
                                               split=(n_meta, seq))
        if early:
            bufs = _copies_wait(bufs, ssem, rsem, grad_x, _scatter_plan, "late_scatter_wait")
            late_reduced = [_chip_sum(rc, p, to_core, N_CORES, f"chip_sum_{k}")
                            for k, (p, rc) in enumerate(zip(bufs[:len(srcs)], bufs[len(srcs):]))]
        grads[l] = dict(dsm=dsm, wr=_diag_blocks(d_wr, hd), wi=_diag_blocks(d_wi, hd), g=d_g)
        if l == depth - 1:
            lands = [lax.empty((a.shape[0],) + a.shape[2:], a.dtype) for a in srcs]
            ssem, rsem, *bufs, token = _copies_start(srcs + lands, _swap_plan, len(srcs), "early_swap_start")
            early = (ssem, rsem, bufs, token)
        else:
            early = None
    grad_x = grad_x[None]

    sharded = []
    sp = jnp.zeros((sm_rows, s, ds), F32)
    sp = sp.at[0:n_meta].set(d_meta.reshape(n_meta, s, ds))
    for l in range(depth):
        base = n_meta + l * SUBLANES
        dsm = grads[l]["dsm"]
        sp = sp.at[base:base + ka, :, 0:cs].set(dsm[ROW_DWA:ROW_DWA + ka].reshape(ka, s, cs))
        sp = sp.at[base + ka:base + ka + kb, :, 0:cs].set(dsm[ROW_DWB:ROW_DWB + kb].reshape(kb, s, cs))
    sharded.append(jnp.transpose(sp, (1, 0, 2)).reshape(s, 2, sm_rows // 2, ds))
    rep_parts = [jnp.concatenate([grads[l]["g"].reshape(-1) for l in range(depth)]), d_final_g.reshape(-1)]
    for row in (ROW_DBA, ROW_DBR, ROW_DBI, ROW_DLAM):
        rep_parts.append(jnp.concatenate([grads[l]["dsm"][row] for l in range(depth)]))
    rep_parts.append(jnp.concatenate([grads[l]["wr"].reshape(-1) for l in range(depth)]))
    rep_parts.append(jnp.concatenate([grads[l]["wi"].reshape(-1) for l in range(depth)]))
    rep_sizes = [p.shape[0] for p in rep_parts]
    piece = _round_up(-(-sum(rep_sizes) // (s * 2)), 2 * SUBLANES * LANES)
    flat = jnp.concatenate(rep_parts + [jnp.zeros((s * 2 * piece - sum(rep_sizes),), F32)])
    sharded.append(flat.reshape(s, 2, piece // LANES, LANES))

    from_sibling = _pair_swap(sharded, "small_pair_swap")
    pair_sums = [_pair_add(a, b, c_idx, f"small_pair_add_{k}") for k, (a, b) in enumerate(zip(sharded, from_sibling))]
    by_chip = _chip_scatter(pair_sums)
    to_device = jnp.stack([my_chip, 2 * my_chip + my_c])
    reduced_sp = _chip_sum(by_chip[0], pair_sums[0], to_core, N_CORES, "small_chip_sum")
    reduced_rep = _chip_sum(by_chip[1], pair_sums[1], to_device, N_CHIPS * N_CORES, "chip_sum_rep")
    *full, rep_all = _final_gather(late_reduced + [reduced_sp], reduced_rep)

    g_win = [full[0].reshape(d, ns), early_full[0].reshape(d, ns)]
    g_wout = [full[1].reshape(dms, d), early_full[1].reshape(dms, d)]
    g_sp = full[2].reshape(sm_rows, ds)
    rep_flat = rep_all.reshape(-1)
    rep_out, off = [], 0
    for n in rep_sizes:
        rep_out.append(rep_flat[off:off + n])
        off += n
    grad = dict(
        meta=g_sp[0:n_meta],
        norm_g=rep_out[0].reshape(depth, d),
        w_in=jnp.stack(g_win),
        conv_a_w=jnp.stack([g_sp[n_meta + l * SUBLANES:n_meta + l * SUBLANES + ka, 0:cs] for l in range(depth)]),
        conv_a_b=rep_out[2].reshape(depth, c),
        lru_wr=rep_out[6].reshape(depth, nh, hd, hd),
        lru_br=rep_out[3].reshape(depth, c),
        lru_wi=rep_out[7].reshape(depth, nh, hd, hd),
        lru_bi=rep_out[4].reshape(depth, c),
        lru_lambda=rep_out[5].reshape(depth, c),
        conv_b_w=jnp.stack([g_sp[n_meta + l * SUBLANES + ka:n_meta + l * SUBLANES + ka + kb, 0:cs]
                            for l in range(depth)]),
        w_out=jnp.stack(g_wout),
        final_g=rep_out[1].reshape(d),
    )

    delta, new_m, new_v = {}, {}, {}
    for n, g_first in (("w_in", g_win[0]), ("w_out", g_wout[0])):
        delta[n], new_m[n], new_v[n] = _adamw_layer(weights[n], g_first, mom1[n], mom2[n], 0, early_step[n],
                                                    f"adamw_{n}_0")
    for n in names:
        if n in delta:
            continue
        shape = weights[n].shape
        two_d = (-1, shape[-1]) if len(shape) > 1 else (1, -1)
        if n in ("lru_wr", "lru_wi"):
            two_d = (-1, LANES)
        out = _adamw(weights[n].reshape(two_d), grad[n].reshape(two_d), mom1[n].reshape(two_d),
                     mom2[n].reshape(two_d), f"adamw_{n}")
        delta[n], new_m[n], new_v[n] = (o.reshape(shape) for o in out)

    return (loss, grad_x, *[grad[n] for n in names], *[delta[n] for n in names],
            *[new_m[n] for n in names], *[new_v[n] for n in names])
```

```python
import functools

import jax
import jax.numpy as jnp
from jax import lax
from jax.experimental import pallas as pl
from jax.experimental.pallas import tpu as pltpu

F32 = jnp.float32
BF16 = jnp.bfloat16

RMS_EPS = 1e-6
LRU_C = 8.0
ADAM_LR = 0.001
ADAM_B1 = 0.9
ADAM_B2 = 0.999
ADAM_EPS = 1e-08
ADAM_WD = 0.01
ADAM_STEP = 10

N_CHIPS = 4
N_CORES = 2
VMEM_LIMIT_BYTES = 56 * 1024 * 1024
SUBLANES = 8
LANES = 128
ROW_QUANTUM = 384
MIX_CHUNK = 192
GATE_BLOCK = 256
MESH = pl.DeviceIdType.MESH
ANY = pl.BlockSpec(memory_space=pl.ANY)

NT_DIMS = (((1,), (1,)), ((), ()))
TN_DIMS = (((0,), (0,)), ((), ()))


def _params(sem):
    return pltpu.CompilerParams(dimension_semantics=sem, vmem_limit_bytes=VMEM_LIMIT_BYTES)


def _sig(x):
    return 0.5 * jnp.tanh(0.5 * x) + 0.5


def _row_tile(t):
    return 704 if t % 704 == 0 else 192


def _col_tile(n, prefs):
    for p in prefs:
        if n % p == 0:
            return p
    return n


def _slab_rows(rows, cols):
    if rows * cols * 4 <= 1024 * 1024:
        return rows
    return _col_tile(rows, (256, 128, 64, 32, 16))


def _norm_in_own(h, g, wg, me_idx, name):
    t, d = h.shape
    s, _, ns = wg.shape
    tm = 1408 if t % 1408 == 0 else _row_tile(t)
    tn = _col_tile(ns, (768, 384, 128))
    nb = ns // tn

    def body(m_ref, h_ref, g_ref, w_ref, u_ref, hn_ref):
        @pl.when(pl.program_id(1) == 0)
        def _():
            x = h_ref[...]
            r = lax.rsqrt(jnp.mean(x * x, axis=-1, keepdims=True) + RMS_EPS)
            hn_ref[...] = ((x * r) * g_ref[...]).astype(BF16)

        u_ref[...] = jnp.dot(hn_ref[...], w_ref[...], preferred_element_type=F32)

    return pl.pallas_call(
        body, name=name,
        grid_spec=pltpu.PrefetchScalarGridSpec(
            num_scalar_prefetch=1, grid=(t // tm, nb),
            in_specs=[pl.BlockSpec((tm, d), lambda i, n, m: (i, 0)),
                      pl.BlockSpec((1, d), lambda i, n, m: (0, 0)),
                      pl.BlockSpec((None, d, tn), lambda i, n, m: (m[0], 0, n))],
            out_specs=[pl.BlockSpec((tm, tn), lambda i, n, m: (i, m[0] * nb + n)),
                       pl.BlockSpec((tm, d), lambda i, n, m: (i, 0))]),
        out_shape=[jax.ShapeDtypeStruct((t, s * ns), F32), jax.ShapeDtypeStruct((t, d), BF16)],
        compiler_params=_params(("arbitrary", "arbitrary")),
    )(me_idx, h, g, wg)


def _norm_in_rest(hn, wg, u, me_idx, name, after=None):
    t, d = hn.shape
    s, _, ns = wg.shape
    tm = 1408 if t % 1408 == 0 else _row_tile(t)
    tn = _col_tile(ns, (1536, 768, 384, 128))
    nb = ns // tn

    def body(m_ref, hn_ref, w_ref, u_in, u_ref):
        del u_in
        u_ref[...] = jnp.dot(hn_ref[...], w_ref[...], preferred_element_type=F32)

    def shard(n, m):
        return (m[0] + 1 + n // nb) % s

    body, more_specs, more = _behind(body, 4, after)
    return pl.pallas_call(
        body, name=name,
        grid_spec=pltpu.PrefetchScalarGridSpec(
            num_scalar_prefetch=1, grid=(t // tm, (s - 1) * nb),
            in_specs=[pl.BlockSpec((tm, d), lambda i, n, m: (i, 0)),
                      pl.BlockSpec((None, d, tn), lambda i, n, m: (shard(n, m), 0, n % nb)),
                      ANY] + more_specs,
            out_specs=pl.BlockSpec((tm, tn), lambda i, n, m: (i, shard(n, m) * nb + n % nb))),
        out_shape=jax.ShapeDtypeStruct(u.shape, u.dtype),
        input_output_aliases={3: 0},
        compiler_params=_params(("arbitrary", "arbitrary")),
    )(me_idx, hn, wg, u, *more)


def _decay_consts(lam):
    z = -lam
    e = jnp.exp(-jnp.abs(z))
    u = 1.0 + e
    log1p_e = jnp.where(u == 1.0, e, jnp.log(u) * (e / (u - 1.0)))
    sp = jnp.maximum(z, 0.0) + log1p_e
    return -LRU_C * sp, LRU_C * _sig(z)


def _gates(xc, wr_ref, br_ref, wi_ref, bi_ref, c8, j, gb):
    sl = slice(j * gb, (j + 1) * gb)
    x16 = xc.astype(BF16)
    r = _sig(jnp.dot(x16, wr_ref[j], preferred_element_type=F32) + br_ref[:, sl])
    ig = _sig(jnp.dot(x16, wi_ref[j], preferred_element_type=F32) + bi_ref[:, sl])
    la = c8[:, sl] * r
    a = jnp.exp(la)
    sq = jnp.sqrt(-jnp.tanh(la) * (a * a + 1.0))
    return r, ig, a, sq


def _mix_fwd(u, wa, ba, wr, br, wi, bi, lam, wb, name):
    t = u.shape[0]
    c = u.shape[1] // 6
    tc = MIX_CHUNK
    gb = wr.shape[1]
    nblk = c // gb
    ka, kb = wa.shape[0], wb.shape[0]

    def body(u_ref, wa_ref, ba_ref, wr_ref, br_ref, wi_ref, bi_ref, lam_ref, wb_ref,
             y_ref, hs_ref, xa_ext, v_ext, xc_s, a_s, b_s, carry_s):
        @pl.when(pl.program_id(0) == 0)
        def _():
            xa_ext[0:SUBLANES, :] = jnp.zeros((SUBLANES, c), F32)
            v_ext[0:SUBLANES, :] = jnp.zeros((SUBLANES, c), F32)
            carry_s[...] = jnp.zeros_like(carry_s)

        xa_ext[SUBLANES:SUBLANES + tc, :] = u_ref[:, 0:c]
        xc = ba_ref[...]
        for k in range(ka):
            xc = xc + wa_ref[pl.ds(k, 1), :] * xa_ext[pl.ds(SUBLANES - (ka - 1) + k, tc), :]
        xc_s[...] = xc
        c8, _ = _decay_consts(lam_ref[...])
        for j in range(nblk):
            sl = slice(j * gb, (j + 1) * gb)
            xcj = xc_s[:, sl]
            _, ig, a, sq = _gates(xcj, wr_ref, br_ref, wi_ref, bi_ref, c8, j, gb)
            a_s[:, sl] = a
            b_s[:, sl] = sq * (ig * xcj)

        row = lax.broadcasted_iota(jnp.int32, (SUBLANES, c), 0)

        def scan_step(j, _):
            off = pl.multiple_of(j * SUBLANES, SUBLANES)
            av = a_s[pl.ds(off, SUBLANES), :]
            bv = b_s[pl.ds(off, SUBLANES), :]
            for d in (1, 2, 4):
                keep = row >= d
                bsh = jnp.where(keep, pltpu.roll(bv, d, axis=0), 0.0)
                ash = jnp.where(keep, pltpu.roll(av, d, axis=0), 1.0)
                bv = av * bsh + bv
                av = av * ash
            hv = av * carry_s[...] + bv
            hs_ref[pl.ds(off, SUBLANES), :] = hv
            carry_s[...] = hs_ref[pl.ds(off + SUBLANES - 1, 1), :]
            return 0

        lax.fori_loop(0, tc // SUBLANES, scan_step, 0)

        ga = u_ref[:, c:2 * c]
        y_ref[:, 0:c] = (hs_ref[...] * (ga * _sig(ga))).astype(BF16)

        v_ext[SUBLANES:SUBLANES + tc, :] = u_ref[:, 3 * c:4 * c] * u_ref[:, 4 * c:5 * c]
        cv = wb_ref[pl.ds(0, 1), :] * v_ext[pl.ds(SUBLANES - (kb - 1), tc), :]
        for k in range(1, kb):
            cv = cv + wb_ref[pl.ds(k, 1), :] * v_ext[pl.ds(SUBLANES - (kb - 1) + k, tc), :]
        gbv = u_ref[:, 5 * c:6 * c]
        y_ref[:, c:2 * c] = (u_ref[:, 2 * c:3 * c] * cv * (gbv * _sig(gbv))).astype(BF16)

        xa_ext[0:SUBLANES, :] = xa_ext[tc:tc + SUBLANES, :]
        v_ext[0:SUBLANES, :] = v_ext[tc:tc + SUBLANES, :]

    full = lambda shape: pl.BlockSpec(shape, lambda i: (0,) * len(shape))
    return pl.pallas_call(
        body, name=name, grid=(t // tc,),
        in_specs=[pl.BlockSpec((tc, 6 * c), lambda i: (i, 0)),
                  full(wa.shape), full(ba.shape), full(wr.shape), full(br.shape),
                  full(wi.shape), full(bi.shape), full(lam.shape), full(wb.shape)],
        out_specs=[pl.BlockSpec((tc, 2 * c), lambda i: (i, 0)),
                   pl.BlockSpec((tc, c), lambda i: (i, 0))],
        out_shape=[jax.ShapeDtypeStruct((t, 2 * c), BF16), jax.ShapeDtypeStruct((t, c), F32)],
        scratch_shapes=[pltpu.VMEM((tc + SUBLANES, c), F32), pltpu.VMEM((tc + SUBLANES, c), F32),
                        pltpu.VMEM((tc, c), F32), pltpu.VMEM((tc, c), F32), pltpu.VMEM((tc, c), F32),
                        pltpu.VMEM((1, c), F32)],
        compiler_params=_params(("arbitrary",)),
    )(u, wa, ba, wr, br, wi, bi, lam, wb)


ROW_DWA = 0
ROW_DBA = 4
ROW_DBR = 5
ROW_DBI = 6
ROW_DLAM = 7
ROW_DWB = 8
SMALL_ROWS = 16


def _mix_bwd(u, hs, dy, wa, ba, wr, br, wi, bi, lam, wb, name, after=None):
    t = u.shape[0]
    c = u.shape[1] // 6
    tc = MIX_CHUNK
    nt = t // tc
    gb = wr.shape[1]
    nblk = c // gb
    ka, kb = wa.shape[0], wb.shape[0]
    assert ka <= ROW_DBA and kb <= SMALL_ROWS - ROW_DWB
    hb = tc // SUBLANES

    def body(u_ref, uh_ref, hs_ref, hsh_ref, dy_ref, wa_ref, ba_ref, wr_ref, br_ref, wi_ref, bi_ref, lam_ref, wb_ref,
             du_ref, dsm_ref, dwr_ref, dwi_ref,
             xa_ext, v_ext, hs_ext, a_ext, ds_ext, dxc_ext, dcv_ext, xc_s, r_s, i_s, sq_s, g_s, an_s):
        i = pl.program_id(0)
        chunk = nt - 1 - i
        tail = slice(tc, tc + SUBLANES)
        head = slice(0, SUBLANES)

        @pl.when(i == 0)
        def _():
            zero = jnp.zeros((SUBLANES, c), F32)
            a_ext[tail, :] = zero
            ds_ext[tail, :] = zero
            dxc_ext[tail, :] = zero
            dcv_ext[tail, :] = zero
            dsm_ref[...] = jnp.zeros_like(dsm_ref)
            dwr_ref[...] = jnp.zeros_like(dwr_ref)
            dwi_ref[...] = jnp.zeros_like(dwi_ref)

        prev = jnp.where(chunk > 0, 1.0, 0.0)
        xa_ext[head, :] = uh_ref[:, 0:c] * prev
        xa_ext[SUBLANES:SUBLANES + tc, :] = u_ref[:, 0:c]
        v_ext[head, :] = uh_ref[:, 3 * c:4 * c] * uh_ref[:, 4 * c:5 * c] * prev
        v_ext[SUBLANES:SUBLANES + tc, :] = u_ref[:, 3 * c:4 * c] * u_ref[:, 4 * c:5 * c]
        hs_ext[head, :] = hsh_ref[...] * prev
        hs_ext[SUBLANES:SUBLANES + tc, :] = hs_ref[...]

        xc = ba_ref[...]
        for k in range(ka):
            xc = xc + wa_ref[pl.ds(k, 1), :] * xa_ext[pl.ds(SUBLANES - (ka - 1) + k, tc), :]
        xc_s[...] = xc
        c8, dc8 = _decay_consts(lam_ref[...])
        for j in range(nblk):
            sl = slice(j * gb, (j + 1) * gb)
            r, ig, a, sq = _gates(xc_s[:, sl], wr_ref, br_ref, wi_ref, bi_ref, c8, j, gb)
            r_s[:, sl] = r
            i_s[:, sl] = ig
            sq_s[:, sl] = sq
            a_ext[0:tc, sl] = a

        ga = u_ref[:, c:2 * c]
        sga = _sig(ga)
        g_s[...] = dy_ref[:, 0:c] * (ga * sga)
        an_s[...] = a_ext[pl.ds(1, tc), :]

        row = lax.broadcasted_iota(jnp.int32, (SUBLANES, c), 0)

        def scan_step(j, _):
            off = pl.multiple_of(tc - SUBLANES - j * SUBLANES, SUBLANES)
            av = an_s[pl.ds(off, SUBLANES), :]
            bv = g_s[pl.ds(off, SUBLANES), :]
            for d in (1, 2, 4):
                keep = row < SUBLANES - d
                bsh = jnp.where(keep, pltpu.roll(bv, SUBLANES - d, axis=0), 0.0)
                ash = jnp.where(keep, pltpu.roll(av, SUBLANES - d, axis=0), 1.0)
                bv = av * bsh + bv
                av = av * ash
            ds_ext[pl.ds(off, SUBLANES), :] = av * ds_ext[pl.ds(off + SUBLANES, 1), :] + bv
            return 0

        lax.fori_loop(0, tc // SUBLANES, scan_step, 0)

        def acc(row_index, val):
            dsm_ref[pl.ds(row_index, 1), :] += jnp.sum(val, axis=0, keepdims=True)

        def acc_block(row_index, sl, val):
            dsm_ref[pl.ds(row_index, 1), sl] += jnp.sum(val, axis=0, keepdims=True)

        for j in range(nblk):
            sl = slice(j * gb, (j + 1) * gb)
            ds = ds_ext[0:tc, sl]
            hprev = hs_ext[pl.ds(SUBLANES - 1, tc), sl]
            a = a_ext[0:tc, sl]
            sq = sq_s[:, sl]
            ig = i_s[:, sl]
            r = r_s[:, sl]
            xcj = xc_s[:, sl]
            t1 = ds * xcj
            dla = (ds * hprev) * a - (t1 * ig) * ((a * a) / sq)
            acc_block(ROW_DLAM, sl, dla * r)
            dpr = (dla * c8[:, sl]) * (r * (1.0 - r))
            dpi = (t1 * sq) * (ig * (1.0 - ig))
            acc_block(ROW_DBR, sl, dpr)
            acc_block(ROW_DBI, sl, dpi)
            p16 = dpr.astype(BF16)
            q16 = dpi.astype(BF16)
            x16 = xcj.astype(BF16)
            dwr_ref[j] += lax.dot_general(x16, p16, TN_DIMS, preferred_element_type=F32)
            dwi_ref[j] += lax.dot_general(x16, q16, TN_DIMS, preferred_element_type=F32)
            dxc = (ds * (sq * ig)
                   + lax.dot_general(p16, wr_ref[j], NT_DIMS, preferred_element_type=F32)
                   + lax.dot_general(q16, wi_ref[j], NT_DIMS, preferred_element_type=F32))
            dxc_ext[0:tc, sl] = dxc
            acc_block(ROW_DBA, sl, dxc)

        dsilu_a = sga * (1.0 + ga * (1.0 - sga))
        du_ref[:, c:2 * c] = (dy_ref[:, 0:c] * hs_ref[...] * dsilu_a).astype(BF16)

        dxc = dxc_ext[0:tc, :]
        dxa = wa_ref[pl.ds(ka - 1, 1), :] * dxc
        acc(ROW_DWA + ka - 1, dxc * xa_ext[SUBLANES:SUBLANES + tc, :])
        for k in range(ka - 1):
            acc(ROW_DWA + k, dxc * xa_ext[pl.ds(SUBLANES - (ka - 1) + k, tc), :])
            dxa = dxa + wa_ref[pl.ds(k, 1), :] * dxc_ext[pl.ds(ka - 1 - k, tc), :]
        du_ref[:, 0:c] = dxa.astype(BF16)

        cv = wb_ref[pl.ds(0, 1), :] * v_ext[pl.ds(SUBLANES - (kb - 1), tc), :]
        for k in range(1, kb):
            cv = cv + wb_ref[pl.ds(k, 1), :] * v_ext[pl.ds(SUBLANES - (kb - 1) + k, tc), :]
        gbv = u_ref[:, 5 * c:6 * c]
        sgb = _sig(gbv)
        silu_b = gbv * sgb
        dyb = dy_ref[:, c:2 * c]
        gB = u_ref[:, 2 * c:3 * c]
        du_ref[:, 2 * c:3 * c] = (dyb * cv * silu_b).astype(BF16)
        du_ref[:, 5 * c:6 * c] = (dyb * gB * cv * (sgb * (1.0 + gbv * (1.0 - sgb)))).astype(BF16)
        dcv = dyb * gB * silu_b
        dcv_ext[0:tc, :] = dcv
        dv = wb_ref[pl.ds(kb - 1, 1), :] * dcv
        acc(ROW_DWB + kb - 1, dcv * v_ext[SUBLANES:SUBLANES + tc, :])
        for k in range(kb - 1):
            acc(ROW_DWB + k, dcv * v_ext[pl.ds(SUBLANES - (kb - 1) + k, tc), :])
            dv = dv + wb_ref[pl.ds(k, 1), :] * dcv_ext[pl.ds(kb - 1 - k, tc), :]
        du_ref[:, 3 * c:4 * c] = (dv * u_ref[:, 4 * c:5 * c]).astype(BF16)
        du_ref[:, 4 * c:5 * c] = (dv * u_ref[:, 3 * c:4 * c]).astype(BF16)

        a_ext[tail, :] = a_ext[head, :]
        ds_ext[tail, :] = ds_ext[head, :]
        dxc_ext[tail, :] = dxc_ext[head, :]
        dcv_ext[tail, :] = dcv_ext[head, :]

        @pl.when(i == nt - 1)
        def _():
            dsm_ref[pl.ds(ROW_DLAM, 1), :] = dsm_ref[pl.ds(ROW_DLAM, 1), :] * dc8

    full = lambda shape: pl.BlockSpec(shape, lambda i: (0,) * len(shape))
    rev = lambda i: (nt - 1 - i, 0)
    halo = lambda i: (jnp.maximum((nt - 1 - i) * hb - 1, 0), 0)
    ext = pltpu.VMEM((tc + SUBLANES, c), F32)
    blk = pltpu.VMEM((tc, c), F32)
    body, more_specs, more = _behind(body, 13, after)
    return pl.pallas_call(
        body, name=name, grid=(nt,),
        in_specs=[pl.BlockSpec((tc, 6 * c), rev), pl.BlockSpec((SUBLANES, 6 * c), halo),
                  pl.BlockSpec((tc, c), rev), pl.BlockSpec((SUBLANES, c), halo),
                  pl.BlockSpec((tc, 2 * c), rev),
                  full(wa.shape), full(ba.shape), full(wr.shape), full(br.shape),
                  full(wi.shape), full(bi.shape), full(lam.shape), full(wb.shape)] + more_specs,
        out_specs=[pl.BlockSpec((tc, 6 * c), rev), full((SMALL_ROWS, c)), full(wr.shape), full(wi.shape)],
        out_shape=[jax.ShapeDtypeStruct((t, 6 * c), BF16), jax.ShapeDtypeStruct((SMALL_ROWS, c), F32),
                   jax.ShapeDtypeStruct(wr.shape, F32), jax.ShapeDtypeStruct(wi.shape, F32)],
        scratch_shapes=[ext] * 7 + [blk] * 6,
        compiler_params=_params(("arbitrary",)),
    )(u, u, hs, hs, dy, wa, ba, wr, br, wi, bi, lam, wb, *more)


def _behind(body, n_in, after):
    if after is None:
        return body, [], []
    return (lambda *refs: body(*refs[:n_in], *refs[n_in + 1:])), [ANY], [after]


def _out_proj(h, y, w, name, after=None):
    t, d = h.shape
    dm = y.shape[1]
    tm = _row_tile(t)
    tn = _col_tile(d, (2048, 1024, 512, 256))

    def body(h_ref, y_ref, w_ref, o_ref):
        o_ref[...] = h_ref[...] + jnp.dot(y_ref[...], w_ref[...], preferred_element_type=F32)

    body, more_specs, more = _behind(body, 3, after)
    return pl.pallas_call(
        body, name=name, grid=(d // tn, t // tm),
        in_specs=[pl.BlockSpec((tm, tn), lambda n, i: (i, n)),
                  pl.BlockSpec((tm, dm), lambda n, i: (i, 0)),
                  pl.BlockSpec((dm, tn), lambda n, i: (0, n))] + more_specs,
        out_specs=pl.BlockSpec((tm, tn), lambda n, i: (i, n)),
        out_shape=jax.ShapeDtypeStruct((t, d), F32),
        compiler_params=_params(("arbitrary", "arbitrary")),
    )(h, y, w, *more)


def _out_proj_norm(h, y, w, g_next, name, after=None):
    t, d = h.shape
    dm = y.shape[1]
    tm = _row_tile(t)

    def body(h_ref, y_ref, w_ref, g_ref, o_ref, hn_ref):
        x = h_ref[...] + jnp.dot(y_ref[...], w_ref[...], preferred_element_type=F32)
        o_ref[...] = x
        r = lax.rsqrt(jnp.mean(x * x, axis=-1, keepdims=True) + RMS_EPS)
        hn_ref[...] = ((x * r) * g_ref[...]).astype(BF16)

    body, more_specs, more = _behind(body, 4, after)
    rows = pl.BlockSpec((tm, d), lambda i: (i, 0))
    return pl.pallas_call(
        body, name=name, grid=(t // tm,),
        in_specs=[rows, pl.BlockSpec((tm, dm), lambda i: (i, 0)), pl.BlockSpec((dm, d), lambda i: (0, 0)),
                  pl.BlockSpec((1, d), lambda i: (0, 0))] + more_specs,
        out_specs=[rows, rows],
        out_shape=[jax.ShapeDtypeStruct((t, d), F32), jax.ShapeDtypeStruct((t, d), BF16)],
        compiler_params=_params(("arbitrary",)),
    )(h, y, w, g_next, *more)


def _in_proj(hn, wg, name, after=None):
    t, d = hn.shape
    s, _, ns = wg.shape
    tm = 1408 if t % 1408 == 0 else _row_tile(t)

    def body(hn_ref, w_ref, u_ref):
        u_ref[...] = jnp.dot(hn_ref[...], w_ref[...], preferred_element_type=F32)

    body, more_specs, more = _behind(body, 2, after)
    return pl.pallas_call(
        body, name=name, grid=(t // tm, s),
        in_specs=[pl.BlockSpec((tm, d), lambda i, n: (i, 0)),
                  pl.BlockSpec((None, d, ns), lambda i, n: (n, 0, 0))] + more_specs,
        out_specs=pl.BlockSpec((tm, ns), lambda i, n: (i, n)),
        out_shape=jax.ShapeDtypeStruct((t, s * ns), F32),
        compiler_params=_params(("arbitrary", "arbitrary")),
    )(hn, wg, *more)


def _out_proj_dy(dout, w, name, after=None):
    t, d = dout.shape
    dm = w.shape[0]
    tm = _row_tile(t)
    tn = _col_tile(dm, (2048, 1024, 512, 256))

    def body(g_ref, w_ref, o_ref):
        o_ref[...] = lax.dot_general(g_ref[...].astype(BF16), w_ref[...], NT_DIMS, preferred_element_type=F32)

    body, more_specs, more = _behind(body, 2, after)
    return pl.pallas_call(
        body, name=name, grid=(dm // tn, t // tm),
        in_specs=[pl.BlockSpec((tm, d), lambda n, i: (i, 0)),
                  pl.BlockSpec((tn, d), lambda n, i: (n, 0))] + more_specs,
        out_specs=pl.BlockSpec((tm, tn), lambda n, i: (i, n)),
        out_shape=jax.ShapeDtypeStruct((t, dm), F32),
        compiler_params=_params(("arbitrary", "arbitrary")),
    )(dout, w, *more)


def _out_proj_dw(y, dout, name):
    t, dm = y.shape
    d = dout.shape[1]
    tmm = _col_tile(dm, (1024, 512, 256))
    tn = _col_tile(d, (512, 256))

    def body(y_ref, g_ref, o_ref):
        o_ref[...] = lax.dot_general(y_ref[...], g_ref[...].astype(BF16), TN_DIMS, preferred_element_type=F32)

    return pl.pallas_call(
        body, name=name, grid=(d // tn, dm // tmm),
        in_specs=[pl.BlockSpec((t, tmm), lambda n, m: (0, m)),
                  pl.BlockSpec((t, tn), lambda n, m: (0, n))],
        out_specs=pl.BlockSpec((tmm, tn), lambda n, m: (m, n)),
        out_shape=jax.ShapeDtypeStruct((dm, d), F32),
        compiler_params=_params(("arbitrary", "arbitrary")),
    )(y, dout)


def _in_proj_bwd(du, wg, h, g, dout, name, after=None, split=None):
    t, d = h.shape
    s, _, ns = wg.shape
    tm = _row_tile(t)
    tn = _col_tile(d, (1024, 512, 256))

    def mm_body(du_ref, w_ref, o_ref):
        total = lax.dot_general(du_ref[:, 0:ns], w_ref[0], NT_DIMS, preferred_element_type=F32)
        for a in range(1, s):
            total = total + lax.dot_general(du_ref[:, a * ns:(a + 1) * ns], w_ref[a], NT_DIMS,
                                            preferred_element_type=F32)
        o_ref[...] = total

    mm_body, more_specs, more = _behind(mm_body, 2, after)
    dhn = pl.pallas_call(
        mm_body, name=name, grid=(t // tm, d // tn),
        in_specs=[pl.BlockSpec((tm, s * ns), lambda i, n: (i, 0)),
                  pl.BlockSpec((s, tn, ns), lambda i, n: (0, n, 0))] + more_specs,
        out_specs=pl.BlockSpec((tm, tn), lambda i, n: (i, n)),
        out_shape=jax.ShapeDtypeStruct((t, d), F32),
        compiler_params=_params(("arbitrary", "arbitrary")),
    )(du, wg, *more)

    tr = 352 if t % 352 == 0 else 192
    nt = t // tr

    def row_grad(dhn_ref, h_ref, g_ref, dout_ref, dg_ref):
        @pl.when(pl.program_id(0) == 0)
        def _():
            dg_ref[...] = jnp.zeros_like(dg_ref)

        x = h_ref[...]
        dn = dhn_ref[...]
        r = lax.rsqrt(jnp.mean(x * x, axis=-1, keepdims=True) + RMS_EPS)
        gd = dn * g_ref[...]
        dot = jnp.mean(gd * x, axis=-1, keepdims=True)
        dg_ref[...] += jnp.sum(dn * (x * r), axis=0, keepdims=True)
        return dout_ref[...] + (r * gd - x * ((r * r * r) * dot))

    rows = pl.BlockSpec((tr, d), lambda i: (i, 0))
    one = pl.BlockSpec((1, d), lambda i: (0, 0))
    if split is None:
        def norm_body(dhn_ref, h_ref, g_ref, dout_ref, dh_ref, dg_ref):
            dh_ref[...] = row_grad(dhn_ref, h_ref, g_ref, dout_ref, dg_ref)

        return pl.pallas_call(
            norm_body, name=name + "_norm", grid=(nt,),
            in_specs=[rows, rows, one, rows], out_specs=[rows, one],
            out_shape=[jax.ShapeDtypeStruct((t, d), F32), jax.ShapeDtypeStruct((1, d), F32)],
            compiler_params=_params(("arbitrary",)),
        )(dhn, h, g, dout)

    n_head, n_body = split
    n_first = tr - n_head
    n_last = n_head + n_body - (nt - 1) * tr
    assert nt >= 2 and 0 < n_head < tr and 0 < n_last <= tr and n_head % SUBLANES == 0 and n_last % SUBLANES == 0

    def split_body(dhn_ref, h_ref, g_ref, dout_ref, body_ref, head_ref, dg_ref, stage, sems):
        i = pl.program_id(0)
        slot = i % 2

        def first_copy(sl):
            return pltpu.make_async_copy(stage.at[sl, pl.ds(n_head, n_first)], body_ref.at[pl.ds(0, n_first)], sems.at[sl])

        def middle_copy(sl, step):
            start = pl.multiple_of(step * tr - n_head, SUBLANES)
            return pltpu.make_async_copy(stage.at[sl], body_ref.at[pl.ds(start, tr)], sems.at[sl])

        def last_copy(sl):
            return pltpu.make_async_copy(stage.at[sl, pl.ds(0, n_last)],
                                         body_ref.at[pl.ds((nt - 1) * tr - n_head, n_last)], sems.at[sl])

        dh = row_grad(dhn_ref, h_ref, g_ref, dout_ref, dg_ref)

        @pl.when(i == 2)
        def _():
            first_copy(0).wait()

        @pl.when(i > 2)
        def _():
            middle_copy(slot, i - 2).wait()

        stage[slot] = dh

        @pl.when(i == 0)
        def _():
            head_ref[...] = stage[0, 0:n_head, :]
            first_copy(0).start()

        @pl.when((i > 0) & (i < nt - 1))
        def _():
            middle_copy(slot, i).start()

        @pl.when(i == nt - 1)
        def _():
            last = last_copy((nt - 1) % 2)
            last.start()
            if nt == 2:
                first_copy(0).wait()
            else:
                middle_copy((nt - 2) % 2, nt - 2).wait()
            last.wait()

    return pl.pallas_call(
        split_body, name=name + "_norm", grid=(nt,),
        in_specs=[rows, rows, one, rows],
        out_specs=[ANY, pl.BlockSpec((n_head, d), lambda i: (0, 0)), one],
        out_shape=[jax.ShapeDtypeStruct((n_body, d), F32), jax.ShapeDtypeStruct((n_head, d), F32),
                   jax.ShapeDtypeStruct((1, d), F32)],
        scratch_shapes=[pltpu.VMEM((2, tr, d), F32), pltpu.SemaphoreType.DMA((2,))],
        compiler_params=_params(("arbitrary",)),
    )(dhn, h, g, dout)


def _in_proj_dw(hn, du, s, name, after=None):
    t, d = hn.shape
    ns = du.shape[1] // s
    tmm = _col_tile(d, (1024, 512, 256))
    tn = _col_tile(ns, (768, 384, 128))
    nb = ns // tn

    def body(hn_ref, du_ref, o_ref):
        o_ref[...] = lax.dot_general(hn_ref[...], du_ref[...], TN_DIMS, preferred_element_type=F32)

    body, more_specs, more = _behind(body, 2, after)
    return pl.pallas_call(
        body, name=name, grid=(s * nb, d // tmm),
        in_specs=[pl.BlockSpec((t, tmm), lambda n, m: (0, m)),
                  pl.BlockSpec((t, tn), lambda n, m: (0, n))] + more_specs,
        out_specs=pl.BlockSpec((None, tmm, tn), lambda n, m: (n // nb, m, n % nb)),
        out_shape=jax.ShapeDtypeStruct((s, d, ns), F32),
        compiler_params=_params(("arbitrary", "arbitrary")),
    )(hn, du, *more)


def _loss_head(h, tgt, g, n_meta, t_real, name):
    t, d = h.shape
    tm = _row_tile(t)

    def body(h_ref, t_ref, g_ref, dh_ref, loss_ref, dg_ref):
        i = pl.program_id(0)

        @pl.when(i == 0)
        def _():
            loss_ref[...] = jnp.zeros_like(loss_ref)
            dg_ref[...] = jnp.zeros_like(dg_ref)

        x = h_ref[...]
        gv = g_ref[...]
        r = lax.rsqrt(jnp.mean(x * x, axis=-1, keepdims=True) + RMS_EPS)
        xr = x * r
        rows = i * tm + lax.broadcasted_iota(jnp.int32, (tm, 1), 0)
        valid = (rows >= n_meta) & (rows < t_real)
        err = jnp.where(valid, xr * gv - t_ref[...], 0.0)
        loss_ref[...] += 0.5 * jnp.sum(jnp.mean(err * err, axis=-1, keepdims=True))
        dy = err * (1.0 / d)
        gd = dy * gv
        dot = jnp.mean(gd * x, axis=-1, keepdims=True)
        dh_ref[...] = r * gd - x * ((r * r * r) * dot)
        dg_ref[...] += jnp.sum(dy * xr, axis=0, keepdims=True)

    return pl.pallas_call(
        body, name=name, grid=(t // tm,),
        in_specs=[pl.BlockSpec((tm, d), lambda i: (i, 0)),
                  pl.BlockSpec((tm, d), lambda i: (i, 0)),
                  pl.BlockSpec((1, d), lambda i: (0, 0))],
        out_specs=[pl.BlockSpec((tm, d), lambda i: (i, 0)),
                   pl.BlockSpec((1, LANES), lambda i: (0, 0)),
                   pl.BlockSpec((1, d), lambda i: (0, 0))],
        out_shape=[jax.ShapeDtypeStruct((t, d), F32), jax.ShapeDtypeStruct((1, LANES), F32),
                   jax.ShapeDtypeStruct((1, d), F32)],
        compiler_params=_params(("arbitrary",)),
    )(h, tgt, g)


def _adamw_rows(rows, cols):
    for cand in (512, 256, 128, 64, 32, 16, 8):
        if rows % cand == 0 and cand * cols * 4 <= 2 * 1024 * 1024:
            return cand
    return rows


def _adamw_math(w_ref, g_ref, m_ref, v_ref, d_ref, nm_ref, nv_ref):
    gv = g_ref[...]
    m2 = ADAM_B1 * m_ref[...] + (1.0 - ADAM_B1) * gv
    v2 = ADAM_B2 * v_ref[...] + (1.0 - ADAM_B2) * (gv * gv)
    m_hat = m2 / (1.0 - ADAM_B1 ** ADAM_STEP)
    v_hat = v2 / (1.0 - ADAM_B2 ** ADAM_STEP)
    d_ref[...] = -ADAM_LR * (m_hat / (jnp.sqrt(v_hat) + ADAM_EPS) + ADAM_WD * w_ref[...])
    nm_ref[...] = m2
    nv_ref[...] = v2


def _adamw(w, g, m, v, name):
    rows, cols = w.shape
    tr = _adamw_rows(rows, cols)

    def body(*refs):
        _adamw_math(*refs)

    spec = pl.BlockSpec((tr, cols), lambda i: (i, 0))
    return pl.pallas_call(
        body, name=name, grid=(rows // tr,),
        in_specs=[spec] * 4, out_specs=[spec] * 3,
        out_shape=[jax.ShapeDtypeStruct((rows, cols), F32)] * 3,
        compiler_params=_params(("arbitrary",)),
    )(w, g, m, v)


def _adamw_layer(w, g, m, v, layer, kept, name, after=None):
    nl, rows, cols = w.shape
    tr = _adamw_rows(rows, cols)
    n_kept = 0 if kept is None else 3

    def body(*refs):
        _adamw_math(*refs[:4], *refs[4 + n_kept:])

    body, more_specs, more = _behind(body, 4 + n_kept, after)
    lay = pl.BlockSpec((None, tr, cols), lambda i: (layer, i, 0))
    return pl.pallas_call(
        body, name=name, grid=(rows // tr,),
        in_specs=[lay, pl.BlockSpec((tr, cols), lambda i: (i, 0)), lay, lay] + [ANY] * n_kept + more_specs,
        out_specs=[lay] * 3,
        out_shape=[jax.ShapeDtypeStruct((nl, rows, cols), F32)] * 3,
        input_output_aliases={4 + k: k for k in range(n_kept)},
        compiler_params=_params(("arbitrary",)),
    )(w, g, m, v, *([] if kept is None else kept), *more)


def _pair_add(x, ra, c_idx, name):
    s, _, rows, cols = x.shape
    tr = _slab_rows(rows, cols)

    def body(c_ref, x_ref, r_ref, o_ref):
        o_ref[...] = (x_ref[...] + r_ref[...]).astype(BF16)

    return pl.pallas_call(
        body, name=name,
        grid_spec=pltpu.PrefetchScalarGridSpec(
            num_scalar_prefetch=1, grid=(s, rows // tr),
            in_specs=[pl.BlockSpec((None, None, tr, cols), lambda a, i, c_ref: (a, c_ref[0], i, 0)),
                      pl.BlockSpec((None, tr, cols), lambda a, i, c_ref: (a, i, 0))],
            out_specs=pl.BlockSpec((None, tr, cols), lambda a, i, c_ref: (a, i, 0))),
        out_shape=jax.ShapeDtypeStruct((s, rows, cols), BF16),
        compiler_params=_params(("arbitrary", "arbitrary")),
    )(c_idx, x, ra)


def _chip_sum(rc, p, where, n_slots, name):
    s, rows, cols = rc.shape
    tr = _slab_rows(rows, cols)

    def body(w_ref, x_ref, p_ref, o_ref):
        me = w_ref[0]
        total = jnp.where(me == 0, p_ref[...], x_ref[0]).astype(F32)
        for a in range(1, s):
            total = total + jnp.where(me == a, p_ref[...], x_ref[a]).astype(F32)
        o_ref[...] = total

    return pl.pallas_call(
        body, name=name,
        grid_spec=pltpu.PrefetchScalarGridSpec(
            num_scalar_prefetch=1, grid=(rows // tr,),
            in_specs=[pl.BlockSpec((s, tr, cols), lambda i, w_ref: (0, i, 0)),
                      pl.BlockSpec((None, tr, cols), lambda i, w_ref: (w_ref[0], i, 0))],
            out_specs=pl.BlockSpec((None, tr, cols), lambda i, w_ref: (w_ref[1], i, 0))),
        out_shape=jax.ShapeDtypeStruct((n_slots, rows, cols), F32),
        compiler_params=_params(("arbitrary",)),
    )(where, rc, p)


def _cast_place(w, layer, me_idx, name, after=None):
    _, rows, cols = w.shape
    tr = _slab_rows(rows, cols)

    def body(m_ref, w_ref, o_ref):
        o_ref[...] = w_ref[...].astype(BF16)

    body, more_specs, more = _behind(body, 2, after)
    return pl.pallas_call(
        body, name=name,
        grid_spec=pltpu.PrefetchScalarGridSpec(
            num_scalar_prefetch=1, grid=(rows // tr,),
            in_specs=[pl.BlockSpec((None, tr, cols), lambda i, m_ref: (layer, i, 0))] + more_specs,
            out_specs=pl.BlockSpec((None, tr, cols), lambda i, m_ref: (m_ref[0], i, 0))),
        out_shape=jax.ShapeDtypeStruct((N_CHIPS, rows, cols), BF16),
        compiler_params=_params(("arbitrary",)),
    )(me_idx, w, *more)


def _place():
    x, y, c = lax.axis_index("x"), lax.axis_index("y"), lax.axis_index("c")
    chips = [(1 - x, y), (x, 1 - y), (1 - x, 1 - y)]
    return x, y, c, chips


def _chip_index(cx, cy):
    return 2 * cx + cy


def _gather_copies(bufs, stage):
    x, y, c, chips = _place()
    me = _chip_index(x, y)
    copies = []
    for b in bufs:
        for chip in chips:
            src = _chip_index(*chip)
            if stage == 0:
                copies.append((b.at[me, c], (*chip, c), b.at[src, c]))
            else:
                copies.append((b.at[src, c], (x, y, 1 - c), b.at[src, 1 - c]))
    return copies


def _remote(ref, peer, ssem, rsem, k):
    return pltpu.make_async_remote_copy(src_ref=ref, dst_ref=ref, send_sem=ssem.at[k], recv_sem=rsem.at[k],
                                        device_id=peer, device_id_type=MESH)


def _gather_first(bufs, small):
    n = len(bufs)
    k = 3 * n

    def body(*refs):
        sm_ref = refs[n]
        b_refs, smg_ref = refs[n + 1:2 * n + 1], refs[2 * n + 1]
        lsem, ssem, rsem = refs[2 * n + 2:]
        x, y, c, chips = _place()
        me = _chip_index(x, y)
        local = pltpu.make_async_copy(sm_ref, smg_ref.at[me], lsem)
        local.start()
        first = _gather_copies(b_refs, 0)
        second = _gather_copies(b_refs, 1)
        started = []
        for i, (ref, peer, _) in enumerate(first):
            started.append(_remote(ref, peer, ssem, rsem, i))
        for j, chip in enumerate(chips):
            started.append(pltpu.make_async_remote_copy(
                src_ref=sm_ref, dst_ref=smg_ref.at[me], send_sem=ssem.at[2 * k + j], recv_sem=rsem.at[2 * k + j],
                device_id=(*chip, c), device_id_type=MESH))
        for cp in started:
            cp.start()
        for i, (_, peer, lands) in enumerate(first):
            _remote(lands, peer, ssem, rsem, i).wait_recv()
            ref, sib, _ = second[i]
            fwd = _remote(ref, sib, ssem, rsem, k + i)
            fwd.start()
            started.append(fwd)
        for i, (_, sib, lands) in enumerate(second):
            _remote(lands, sib, ssem, rsem, k + i).wait_recv()
        for j, chip in enumerate(chips):
            theirs = smg_ref.at[_chip_index(*chip)]
            pltpu.make_async_remote_copy(src_ref=theirs, dst_ref=theirs, send_sem=ssem.at[2 * k + j],
                                         recv_sem=rsem.at[2 * k + j], device_id=(*chip, c),
                                         device_id_type=MESH).wait_recv()
        for cp in started:
            cp.wait_send()
        local.wait()

    return pl.pallas_call(
        body, name="gather_first",
        in_specs=[ANY] * (n + 1), out_specs=[ANY] * (n + 1),
        out_shape=[jax.ShapeDtypeStruct(b.shape, b.dtype) for b in bufs]
        + [jax.ShapeDtypeStruct((N_CHIPS,) + small.shape, small.dtype)],
        input_output_aliases={i: i for i in range(n)},
        scratch_shapes=[pltpu.SemaphoreType.DMA, pltpu.SemaphoreType.DMA((2 * k + 3,)),
                        pltpu.SemaphoreType.DMA((2 * k + 3,))],
    )(*bufs, small)


HBM = pl.BlockSpec(memory_space=pltpu.HBM)
SEM = pl.BlockSpec(memory_space=pltpu.SEMAPHORE)
DATAFLOW = pltpu.SideEffectType.DATAFLOW_SIDE_EFFECTING


def _copies_start(bufs, plan, n_copies, name, after=None):
    n = len(bufs)
    extra = [] if after is None else [after]

    def body(*refs):
        refs = refs[:n] + refs[n + len(extra):]
        ssem, rsem = refs[n], refs[n + 1]
        b_refs, token = refs[n + 2:2 * n + 2], refs[2 * n + 2]
        copies = plan(b_refs)
        assert len(copies) == n_copies
        for i, (src, dst, peer, _) in enumerate(copies):
            pltpu.make_async_remote_copy(src_ref=src, dst_ref=dst, send_sem=ssem.at[i], recv_sem=rsem.at[i],
                                         device_id=peer, device_id_type=MESH).start()
        token[...] = jnp.zeros_like(token)

    return pl.pallas_call(
        body, name=name,
        out_shape=(pltpu.SemaphoreType.DMA((n_copies,)), pltpu.SemaphoreType.DMA((n_copies,)),
                   *[pltpu.HBM(b.shape, b.dtype) for b in bufs], jax.ShapeDtypeStruct((SUBLANES, LANES), F32)),
        in_specs=[HBM] * n + [ANY] * len(extra),
        out_specs=(SEM, SEM, *[HBM] * n, pl.BlockSpec(memory_space=pltpu.VMEM)),
        input_output_aliases={i: 2 + i for i in range(n)},
        compiler_params=pltpu.CompilerParams(has_side_effects=DATAFLOW),
    )(*[pltpu.with_memory_space_constraint(b, pltpu.HBM) for b in bufs], *extra)


def _copies_wait(bufs, ssem, rsem, after, plan, name):
    n = len(bufs)
    afters = list(after) if isinstance(after, (list, tuple)) else [after]

    def body(*refs):
        b_refs, ssem_ref, rsem_ref = refs[:n], refs[n], refs[n + 1]
        for i, (src, dst, peer, lands) in enumerate(plan(b_refs)):
            pltpu.make_async_remote_copy(src_ref=src, dst_ref=dst, send_sem=ssem_ref.at[i], recv_sem=rsem_ref.at[i],
                                         device_id=peer, device_id_type=MESH).wait_send()
            pltpu.make_async_remote_copy(src_ref=lands, dst_ref=lands, send_sem=ssem_ref.at[i],
                                         recv_sem=rsem_ref.at[i], device_id=peer, device_id_type=MESH).wait_recv()

    return pl.pallas_call(
        body, name=name,
        out_shape=tuple(pltpu.HBM(b.shape, b.dtype) for b in bufs),
        in_specs=[HBM] * n + [SEM, SEM] + [ANY] * len(afters), out_specs=tuple([HBM] * n),
        input_output_aliases={i: i for i in range(n)},
        compiler_params=pltpu.CompilerParams(has_side_effects=DATAFLOW),
    )(*bufs, ssem, rsem, *afters)


def _gather_plan(stage):
    return lambda refs: [(ref, ref, peer, lands) for ref, peer, lands in _gather_copies(refs, stage)]


def _swap_plan(refs):
    n = len(refs) // 2
    x, y, c, _ = _place()
    return [(refs[a].at[:, 1 - c], refs[n + a], (x, y, 1 - c), refs[n + a]) for a in range(n)]


def _scatter_plan(refs):
    n = len(refs) // 2
    x, y, c, chips = _place()
    me = _chip_index(x, y)
    return [(refs[a].at[_chip_index(*chip)], refs[n + a].at[me], (*chip, c), refs[n + a].at[_chip_index(*chip)])
            for a in range(n) for chip in chips]


def _pair_gather_plan(refs):
    x, y, c, _ = _place()
    return [(r.at[c], r.at[c], (x, y, 1 - c), r.at[1 - c]) for r in refs]


def _pair_swap(xs, name):
    n = len(xs)

    def body(*refs):
        x_refs, o_refs, ssem, rsem = refs[:n], refs[n:2 * n], refs[2 * n], refs[2 * n + 1]
        x, y, c, _ = _place()
        copies = [pltpu.make_async_remote_copy(src_ref=x_refs[a].at[:, 1 - c], dst_ref=o_refs[a],
                                               send_sem=ssem.at[a], recv_sem=rsem.at[a],
                                               device_id=(x, y, 1 - c), device_id_type=MESH) for a in range(n)]
        for cp in copies:
            cp.start()
        for cp in copies:
            cp.wait()

    return pl.pallas_call(
        body, name=name, in_specs=[ANY] * n, out_specs=[ANY] * n,
        out_shape=[jax.ShapeDtypeStruct((a.shape[0],) + a.shape[2:], a.dtype) for a in xs],
        scratch_shapes=[pltpu.SemaphoreType.DMA((n,)), pltpu.SemaphoreType.DMA((n,))],
    )(*xs)


def _chip_scatter(ps):
    n = len(ps)

    def body(*refs):
        p_refs, o_refs, ssem, rsem = refs[:n], refs[n:2 * n], refs[2 * n], refs[2 * n + 1]
        x, y, c, chips = _place()
        me = _chip_index(x, y)
        sends = []
        for a in range(n):
            for j, chip in enumerate(chips):
                sends.append(pltpu.make_async_remote_copy(
                    src_ref=p_refs[a].at[_chip_index(*chip)], dst_ref=o_refs[a].at[me],
                    send_sem=ssem.at[3 * a + j], recv_sem=rsem.at[3 * a + j],
                    device_id=(*chip, c), device_id_type=MESH))
        for cp in sends:
            cp.start()
        for a in range(n):
            for j, chip in enumerate(chips):
                src = _chip_index(*chip)
                pltpu.make_async_remote_copy(
                    src_ref=p_refs[a].at[src], dst_ref=o_refs[a].at[src],
                    send_sem=ssem.at[3 * a + j], recv_sem=rsem.at[3 * a + j],
                    device_id=(*chip, c), device_id_type=MESH).wait_recv()
        for cp in sends:
            cp.wait_send()

    return pl.pallas_call(
        body, name="chip_scatter", in_specs=[ANY] * n, out_specs=[ANY] * n,
        out_shape=[jax.ShapeDtypeStruct(a.shape, a.dtype) for a in ps],
        scratch_shapes=[pltpu.SemaphoreType.DMA((3 * n,)), pltpu.SemaphoreType.DMA((3 * n,))],
    )(*ps)


def _final_gather(fs, rep):
    n = len(fs)

    def body(*refs):
        o_refs, repo_ref = refs[n + 1:2 * n + 1], refs[2 * n + 1]
        ssem, rsem = refs[2 * n + 2:]
        x, y, c, chips = _place()
        slot = 4 * x + 2 * y + c
        copies = [pltpu.make_async_remote_copy(src_ref=o_refs[a].at[c], dst_ref=o_refs[a].at[c],
                                               send_sem=ssem.at[a], recv_sem=rsem.at[a],
                                               device_id=(x, y, 1 - c), device_id_type=MESH) for a in range(n)]
        peers = [(x, y, 1 - c)] + [(*chip, c) for chip in chips] + [(*chip, 1 - c) for chip in chips]
        for k, peer in enumerate(peers):
            copies.append(pltpu.make_async_remote_copy(src_ref=repo_ref.at[slot], dst_ref=repo_ref.at[slot],
                                                       send_sem=ssem.at[n + k], recv_sem=rsem.at[n + k],
                                                       device_id=peer, device_id_type=MESH))
        for cp in copies:
            cp.start()
        for a in range(n):
            pltpu.make_async_remote_copy(src_ref=o_refs[a].at[1 - c], dst_ref=o_refs[a].at[1 - c],
                                         send_sem=ssem.at[a], recv_sem=rsem.at[a],
                                         device_id=(x, y, 1 - c), device_id_type=MESH).wait_recv()
        for k, peer in enumerate(peers):
            px, py, pc = peer
            theirs = repo_ref.at[4 * px + 2 * py + pc]
            pltpu.make_async_remote_copy(src_ref=theirs, dst_ref=theirs, send_sem=ssem.at[n + k], recv_sem=rsem.at[n + k],
                                         device_id=peer, device_id_type=MESH).wait_recv()
        for cp in copies:
            cp.wait_send()

    return pl.pallas_call(
        body, name="final_gather", in_specs=[ANY] * (n + 1), out_specs=[ANY] * (n + 1),
        out_shape=[jax.ShapeDtypeStruct(a.shape, a.dtype) for a in fs] + [jax.ShapeDtypeStruct(rep.shape, rep.dtype)],
        input_output_aliases={k: k for k in range(n + 1)},
        scratch_shapes=[pltpu.SemaphoreType.DMA((n + 7,)), pltpu.SemaphoreType.DMA((n + 7,))],
    )(*fs, rep)


def _block_diag(w, gb):
    nh, hd, _ = w.shape
    per = gb // hd
    w4 = w.reshape(nh // per, per, hd, hd)
    eye = jnp.eye(per, dtype=w.dtype)
    return jnp.einsum("jaik,ab->jaibk", w4, eye).reshape(nh // per, gb, gb)


def _diag_blocks(dense, hd):
    nj, gb, _ = dense.shape
    per = gb // hd
    d5 = dense.reshape(nj, per, hd, per, hd)
    return jnp.stack([d5[:, a, :, a, :] for a in range(per)], axis=1).reshape(nj * per, hd, hd)


def _round_up(n, q):
    return (n + q - 1) // q * q


def kernel(x, meta, norm_g, w_in, conv_a_w, conv_a_b, lru_wr, lru_br, lru_wi, lru_bi, lru_lambda, conv_b_w, w_out, final_g, loss_target, m_meta, m_norm_g, m_w_in, m_conv_a_w, m_conv_a_b, m_lru_wr, m_lru_br, m_lru_wi, m_lru_bi, m_lru_lambda, m_conv_b_w, m_w_out, m_final_g, v_meta, v_norm_g, v_w_in, v_conv_a_w, v_conv_a_b, v_lru_wr, v_lru_br, v_lru_wi, v_lru_bi, v_lru_lambda, v_conv_b_w, v_w_out, v_final_g):
    weights = dict(meta=meta, norm_g=norm_g, w_in=w_in, conv_a_w=conv_a_w, conv_a_b=conv_a_b, lru_wr=lru_wr,
                   lru_br=lru_br, lru_wi=lru_wi, lru_bi=lru_bi, lru_lambda=lru_lambda, conv_b_w=conv_b_w,
                   w_out=w_out, final_g=final_g)
    mom1 = dict(meta=m_meta, norm_g=m_norm_g, w_in=m_w_in, conv_a_w=m_conv_a_w, conv_a_b=m_conv_a_b,
                lru_wr=m_lru_wr, lru_br=m_lru_br, lru_wi=m_lru_wi, lru_bi=m_lru_bi, lru_lambda=m_lru_lambda,
                conv_b_w=m_conv_b_w, w_out=m_w_out, final_g=m_final_g)
    mom2 = dict(meta=v_meta, norm_g=v_norm_g, w_in=v_w_in, conv_a_w=v_conv_a_w, conv_a_b=v_conv_a_b,
                lru_wr=v_lru_wr, lru_br=v_lru_br, lru_wi=v_lru_wi, lru_bi=v_lru_bi, lru_lambda=v_lru_lambda,
                conv_b_w=v_conv_b_w, w_out=v_w_out, final_g=v_final_g)
    names = list(weights)

    assert x.shape[0] == 1
    seq, d = x.shape[1], x.shape[2]
    n_meta, ds = meta.shape
    depth = norm_g.shape[0]
    c = lru_lambda.shape[1]
    nh, hd = lru_wr.shape[1], lru_wr.shape[2]
    ns = w_in.shape[2]
    dms = w_out.shape[1]
    cs = conv_a_w.shape[2]
    ka, kb = conv_a_w.shape[1], conv_b_w.shape[1]
    s = N_CHIPS
    assert depth == N_CORES and d == s * ds and c == s * cs and s * ns == 6 * c and s * dms == 2 * c
    gb = min(GATE_BLOCK, c)
    t_real = n_meta + seq
    t = _round_up(t_real, ROW_QUANTUM)
    my_c = lax.axis_index("c").astype(jnp.int32)
    my_chip = (2 * lax.axis_index("x") + lax.axis_index("y")).astype(jnp.int32)
    c_idx = my_c.reshape(1)
    chip_idx = my_chip.reshape(1)

    sm_rows = _round_up(n_meta + depth * SUBLANES, 2 * SUBLANES)
    small = jnp.zeros((sm_rows, ds), F32)
    small = small.at[0:n_meta, :].set(meta)
    for l in range(depth):
        base = n_meta + l * SUBLANES
        small = small.at[base:base + ka, 0:cs].set(conv_a_w[l])
        small = small.at[base + ka:base + ka + kb, 0:cs].set(conv_b_w[l])
    (small_g,) = _gather_first([], small)
    meta_full = jnp.transpose(small_g[:, 0:n_meta, :], (1, 0, 2)).reshape(n_meta, d)
    wa_full, wb_full = [], []
    for l in range(depth):
        base = n_meta + l * SUBLANES
        wa_full.append(jnp.transpose(small_g[:, base:base + ka, 0:cs], (1, 0, 2)).reshape(ka, c))
        wb_full.append(jnp.transpose(small_g[:, base + ka:base + ka + kb, 0:cs], (1, 0, 2)).reshape(kb, c))
    win0 = _cast_place(w_in, 0, chip_idx, "cast_w_in_0").reshape(s, 2, d // 2, ns)
    ssem_w, rsem_w, win0, token_w = _copies_start([win0], _gather_plan(0), 3, "gather_win0_ici_start", after=small_g)
    win_b = [None] + [_cast_place(w_in, l, chip_idx, f"cast_w_in_{l}", after=token_w).reshape(s, 2, d // 2, ns)
                      for l in range(1, depth)]
    wout_b = [_cast_place(w_out, l, chip_idx, f"cast_w_out_{l}", after=token_w).reshape(s, 2, dms // 2, d)
              for l in range(depth)]
    h = jnp.concatenate([meta_full, x[0], jnp.zeros((t - t_real, d), F32)], axis=0) + token_w[0, 0]
    tgt = jnp.concatenate([jnp.zeros((n_meta, d), F32), loss_target[0], jnp.zeros((t - t_real, d), F32)],
                          axis=0) + token_w[0, 0]
    u_own, hn_own = _norm_in_own(h, norm_g[0].reshape(1, d), win0.reshape(s, d, ns), chip_idx, "norm_in_0_own")
    (win0,) = _copies_wait([win0], ssem_w, rsem_w, [u_own, tgt] + win_b[1:] + wout_b, _gather_plan(0),
                           "gather_win0_ici_wait")
    ssem_w, rsem_w, win0, token_w = _copies_start([win0], _gather_plan(1), 3, "gather_win0_d2d_start")
    def travel(buf, stage, tag, after):
        return _copies_start([buf], _gather_plan(stage), 3, f"gather_{tag}_{'d2d' if stage else 'ici'}_start",
                             after=after)

    def arrived(state, stage, tag, after):
        (buf,) = _copies_wait([state[2]], state[0], state[1], after, _gather_plan(stage),
                              f"gather_{tag}_{'d2d' if stage else 'ici'}_wait")
        return buf

    on_wout0 = travel(wout_b[0], 0, "wout0", token_w)
    on_win1 = travel(win_b[1], 0, "win1", on_wout0[3])
    on_wout1 = travel(wout_b[1], 0, "wout1", on_win1[3])
    token = on_wout1[3]
    (win_b[0],) = _copies_wait([win0], ssem_w, rsem_w, token, _gather_plan(1), "gather_win0_d2d_wait")

    layer_w = []
    for l in range(depth):
        layer_w.append(dict(
            g=norm_g[l].reshape(1, d), wa=wa_full[l], ba=conv_a_b[l].reshape(1, c),
            wr=_block_diag(lru_wr[l], gb).astype(BF16), br=lru_br[l].reshape(1, c),
            wi=_block_diag(lru_wi[l], gb).astype(BF16), bi=lru_bi[l].reshape(1, c),
            lam=lru_lambda[l].reshape(1, c), wb=wb_full[l]))
    saved = []
    for l, lw in enumerate(layer_w):
        first = l == 0
        lw["win"] = win_b[l].reshape(s, d, ns)
        if first:
            u = _norm_in_rest(hn_own, lw["win"], u_own, chip_idx, "norm_in_0_rest", after=token)
            hn = hn_own
            on_wout0 = travel(arrived(on_wout0, 0, "wout0", u), 1, "wout0", None)
            token = on_wout0[3]
        else:
            hn = hn_next
            u = _in_proj(hn, lw["win"], f"norm_in_{l}", after=token)
            wout_b[1] = arrived(on_wout1, 1, "wout1", u)
        y, hs = _mix_fwd(u, lw["wa"], lw["ba"] + token[0, 0] if first else lw["ba"], lw["wr"], lw["br"], lw["wi"],
                         lw["bi"], lw["lam"], lw["wb"], f"mix_fwd_{l}")
        token = None
        if first:
            wout_b[0] = arrived(on_wout0, 1, "wout0", y)
            on_win1 = travel(arrived(on_win1, 0, "win1", y), 1, "win1", None)
            token = on_win1[3]
        lw["wout"] = wout_b[l].reshape(2 * c, d)
        saved.append((h, u, hn, y, hs))
        if first:
            h, hn_next = _out_proj_norm(h, y, lw["wout"], layer_w[1]["g"], f"out_proj_{l}", after=token)
        else:
            h = _out_proj(h, y, lw["wout"], f"out_proj_{l}", after=token)
        if first:
            win_b[1] = arrived(on_win1, 1, "win1", h)
            on_wout1 = travel(arrived(on_wout1, 0, "wout1", h), 1, "wout1", None)
            token = on_wout1[3]
    dh, loss_lanes, d_final_g = _loss_head(h, tgt, final_g.reshape(1, d), n_meta, t_real, "loss_head")
    loss = lax.psum(loss_lanes[0, 0], ("x", "y", "c"))

    to_core = jnp.stack([my_chip, my_c])
    grads = [None] * depth
    early = None
    for l in reversed(range(depth)):
        lw = layer_w[l]
        h_in, u, hn, y, hs = saved[l]
        token = early[-1] if early else None
        dy = _out_proj_dy(dh, lw["wout"], f"out_proj_dy_{l}", after=token)
        d_wout = _out_proj_dw(y, dh, f"out_proj_dw_{l}")
        if early:
            ssem, rsem, bufs, _ = early
            bufs = _copies_wait(bufs, ssem, rsem, d_wout, _swap_plan, "early_swap_wait")
            half = len(bufs) // 2
            sums = [_pair_add(a, b, c_idx, f"early_pair_add_{k}") for k, (a, b) in enumerate(zip(bufs[:half], bufs[half:]))]
            lands = [lax.empty(p.shape, p.dtype) for p in sums]
            ssem, rsem, *bufs, token = _copies_start(sums + lands, _scatter_plan, 3 * half, "early_scatter_start")
        du, dsm, d_wr, d_wi = _mix_bwd(u, hs, dy, lw["wa"], lw["ba"], lw["wr"], lw["br"], lw["wi"], lw["bi"],
                                       lw["lam"], lw["wb"], f"mix_bwd_{l}", after=token)
        if early:
            bufs = _copies_wait(bufs, ssem, rsem, du, _scatter_plan, "early_scatter_wait")
            halves = [_chip_sum(rc, p, to_core, N_CORES, f"early_chip_sum_{k}")
                      for k, (p, rc) in enumerate(zip(bufs[:half], bufs[half:]))]
            ssem, rsem, *bufs, token = _copies_start(halves, _pair_gather_plan, half, "early_gather_start")
        d_win = _in_proj_dw(hn, du, s, f"in_proj_dw_{l}", after=token)
        srcs = [d_win.reshape(s, 2, d // 2, ns), d_wout.reshape(s, 2, dms // 2, d)]
        if early:
            early_full = _copies_wait(bufs, ssem, rsem, d_win, _pair_gather_plan, "early_gather_wait")
            lands = [lax.empty((a.shape[0],) + a.shape[2:], a.dtype) for a in srcs]
            ssem, rsem, *bufs, token = _copies_start(srcs + lands, _swap_plan, len(srcs), "late_swap_start")
            last = depth - 1
            early_grad = dict(w_in=early_full[0].reshape(d, ns), w_out=early_full[1].reshape(dms, d))
            early_step = {n: _adamw_layer(weights[n], early_grad[n], mom1[n], mom2[n], last, None,
                                          f"adamw_{n}_{last}", after=token) for n in ("w_in", "w_out")}
            bufs = _copies_wait(bufs, ssem, rsem, [o[0] for o in early_step.values()], _swap_plan, "late_swap_wait")
            late_sums = [_pair_add(a, b, c_idx, f"pair_add_{k}")
                         for k, (a, b) in enumerate(zip(bufs[:len(srcs)], bufs[len(srcs):]))]
            lands = [lax.empty(p.shape, p.dtype) for p in late_sums]
            ssem, rsem, *bufs, token = _copies_start(late_sums + lands, _scatter_plan, 3 * len(srcs), "late_scatter_start")
        if l > 0:
            dh, d_g = _in_proj_bwd(du, lw["win"], h_in, lw["g"], dh, f"in_proj_bwd_{l}", after=token)
        else:
            grad_x, d_meta, d_g = _in_proj_bwd(du, lw["win"], h_in, lw["g"], dh, f"in_proj_bwd_{l}", after=token,
                                               split=(n_meta, seq))
        if early:
            bufs = _copies_wait(bufs, ssem, rsem, grad_x, _scatter_plan, "late_scatter_wait")
            late_reduced = [_chip_sum(rc, p, to_core, N_CORES, f"chip_sum_{k}")
                            for k, (p, rc) in enumerate(zip(bufs[:len(srcs)], bufs[len(srcs):]))]
        grads[l] = dict(dsm=dsm, wr=_diag_blocks(d_wr, hd), wi=_diag_blocks(d_wi, hd), g=d_g)
        if l == depth - 1:
            lands = [lax.empty((a.shape[0],) + a.shape[2:], a.dtype) for a in srcs]
            ssem, rsem, *bufs, token = _copies_start(srcs + lands, _swap_plan, len(srcs), "early_swap_start")
            early = (ssem, rsem, bufs, token)
        else:
            early = None
    grad_x = grad_x[None]

    sharded = []
    sp = jnp.zeros((sm_rows, s, ds), F32)
    sp = sp.at[0:n_meta].set(d_meta.reshape(n_meta, s, ds))
    for l in range(depth):
        base = n_meta + l * SUBLANES
        dsm = grads[l]["dsm"]
        sp = sp.at[base:base + ka, :, 0:cs].set(dsm[ROW_DWA:ROW_DWA + ka].reshape(ka, s, cs))
        sp = sp.at[base + ka:base + ka + kb, :, 0:cs].set(dsm[ROW_DWB:ROW_DWB + kb].reshape(kb, s, cs))
    sharded.append(jnp.transpose(sp, (1, 0, 2)).reshape(s, 2, sm_rows // 2, ds))
    rep_parts = [jnp.concatenate([grads[l]["g"].reshape(-1) for l in range(depth)]), d_final_g.reshape(-1)]
    for row in (ROW_DBA, ROW_DBR, ROW_DBI, ROW_DLAM):
        rep_parts.append(jnp.concatenate([grads[l]["dsm"][row] for l in range(depth)]))
    rep_parts.append(jnp.concatenate([grads[l]["wr"].reshape(-1) for l in range(depth)]))
    rep_parts.append(jnp.concatenate([grads[l]["wi"].reshape(-1) for l in range(depth)]))
    rep_sizes = [p.shape[0] for p in rep_parts]
    piece = _round_up(-(-sum(rep_sizes) // (s * 2)), 2 * SUBLANES * LANES)
    flat = jnp.concatenate(rep_parts + [jnp.zeros((s * 2 * piece - sum(rep_sizes),), F32)])
    sharded.append(flat.reshape(s, 2, piece // LANES, LANES))

    from_sibling = _pair_swap(sharded, "small_pair_swap")
    pair_sums = [_pair_add(a, b, c_idx, f"small_pair_add_{k}") for k, (a, b) in enumerate(zip(sharded, from_sibling))]
    by_chip = _chip_scatter(pair_sums)
    to_device = jnp.stack([my_chip, 2 * my_chip + my_c])
    reduced_sp = _chip_sum(by_chip[0], pair_sums[0], to_core, N_CORES, "small_chip_sum")
    reduced_rep = _chip_sum(by_chip[1], pair_sums[1], to_device, N_CHIPS * N_CORES, "chip_sum_rep")
    *full, rep_all = _final_gather(late_reduced + [reduced_sp], reduced_rep)

    g_win = [full[0].reshape(d, ns), early_full[0].reshape(d, ns)]
    g_wout = [full[1].reshape(dms, d), early_full[1].reshape(dms, d)]
    g_sp = full[2].reshape(sm_rows, ds)
    rep_flat = rep_all.reshape(-1)
    rep_out, off = [], 0
    for n in rep_sizes:
        rep_out.append(rep_flat[off:off + n])
        off += n
    grad = dict(
        meta=g_sp[0:n_meta],
        norm_g=rep_out[0].reshape(depth, d),
        w_in=jnp.stack(g_win),
        conv_a_w=jnp.stack([g_sp[n_meta + l * SUBLANES:n_meta + l * SUBLANES + ka, 0:cs] for l in range(depth)]),
        conv_a_b=rep_out[2].reshape(depth, c),
        lru_wr=rep_out[6].reshape(depth, nh, hd, hd),
        lru_br=rep_out[3].reshape(depth, c),
        lru_wi=rep_out[7].reshape(depth, nh, hd, hd),
        lru_bi=rep_out[4].reshape(depth, c),
        lru_lambda=rep_out[5].reshape(depth, c),
        conv_b_w=jnp.stack([g_sp[n_meta + l * SUBLANES + ka:n_meta + l * SUBLANES + ka + kb, 0:cs]
                            for l in range(depth)]),
        w_out=jnp.stack(g_wout),
        final_g=rep_out[1].reshape(d),
    )

    delta, new_m, new_v = {}, {}, {}
    for n, g_first in (("w_in", g_win[0]), ("w_out", g_wout[0])):
        delta[n], new_m[n], new_v[n] = _adamw_layer(weights[n], g_first, mom1[n], mom2[n], 0, early_step[n],
                                                    f"adamw_{n}_0")
    for n in names:
        if n in delta:
            continue
        shape = weights[n].shape
        two_d = (-1, shape[-1]) if len(shape) > 1 else (1, -1)
        if n in ("lru_wr", "lru_wi"):
            two_d = (-1, LANES)
        out = _adamw(weights[n].reshape(two_d), grad[n].reshape(two_d), mom1[n].reshape(two_d),
                     mom2[n].reshape(two_d), f"adamw_{n}")
        delta[n], new_m[n], new_v[n] = (o.reshape(shape) for o in out)

    return (loss, grad_x, *[grad[n] for n in names], *[delta[n] for n in names],
            *[new_m[n] for n in names], *[new_v[n] for n in names])
```

```python
import functools

import jax
import jax.numpy as jnp
from jax import lax
from jax.experimental import pallas as pl
from jax.experimental.pallas import tpu as pltpu

F32 = jnp.float32
BF16 = jnp.bfloat16

RMS_EPS = 1e-6
LRU_C = 8.0
ADAM_LR = 0.001
ADAM_B1 = 0.9
ADAM_B2 = 0.999
ADAM_EPS = 1e-08
ADAM_WD = 0.01
ADAM_STEP = 10

N_CHIPS = 4
N_CORES = 2
VMEM_LIMIT_BYTES = 56 * 1024 * 1024
SUBLANES = 8
LANES = 128
ROW_QUANTUM = 384
MIX_CHUNK = 192
SCAN_UNROLL = 4
GATE_BLOCK = 256
MESH = pl.DeviceIdType.MESH
ANY = pl.BlockSpec(memory_space=pl.ANY)

NT_DIMS = (((1,), (1,)), ((), ()))
TN_DIMS = (((0,), (0,)), ((), ()))


def _params(sem):
    return pltpu.CompilerParams(dimension_semantics=sem, vmem_limit_bytes=VMEM_LIMIT_BYTES)


def _sig(x):
    return 0.5 * jnp.tanh(0.5 * x) + 0.5


def _row_tile(t):
    return 704 if t % 704 == 0 else 192


def _col_tile(n, prefs):
    for p in prefs:
        if n % p == 0:
            return p
    return n


def _slab_rows(rows, cols):
    if rows * cols * 4 <= 1024 * 1024:
        return rows
    return _col_tile(rows, (256, 128, 64, 32, 16))


def _norm_in_own(h, g, wg, me_idx, name):
    t, d = h.shape
    s, _, ns = wg.shape
    tm = 1408 if t % 1408 == 0 else _row_tile(t)
    tn = _col_tile(ns, (768, 384, 128))
    nb = ns // tn

    def body(m_ref, h_ref, g_ref, w_ref, u_ref, hn_ref):
        @pl.when(pl.program_id(1) == 0)
        def _():
            x = h_ref[...]
            r = lax.rsqrt(jnp.mean(x * x, axis=-1, keepdims=True) + RMS_EPS)
            hn_ref[...] = ((x * r) * g_ref[...]).astype(BF16)

        u_ref[...] = jnp.dot(hn_ref[...], w_ref[...], preferred_element_type=F32)

    return pl.pallas_call(
        body, name=name,
        grid_spec=pltpu.PrefetchScalarGridSpec(
            num_scalar_prefetch=1, grid=(t // tm, nb),
            in_specs=[pl.BlockSpec((tm, d), lambda i, n, m: (i, 0)),
                      pl.BlockSpec((1, d), lambda i, n, m: (0, 0)),
                      pl.BlockSpec((None, d, tn), lambda i, n, m: (m[0], 0, n))],
            out_specs=[pl.BlockSpec((tm, tn), lambda i, n, m: (i, m[0] * nb + n)),
                       pl.BlockSpec((tm, d), lambda i, n, m: (i, 0))]),
        out_shape=[jax.ShapeDtypeStruct((t, s * ns), F32), jax.ShapeDtypeStruct((t, d), BF16)],
        compiler_params=_params(("arbitrary", "arbitrary")),
    )(me_idx, h, g, wg)


def _norm_in_rest(hn, wg, u, me_idx, name, after=None):
    t, d = hn.shape
    s, _, ns = wg.shape
    tm = 1408 if t % 1408 == 0 else _row_tile(t)
    tn = _col_tile(ns, (1536, 768, 384, 128))
    nb = ns // tn

    def body(m_ref, hn_ref, w_ref, u_in, u_ref):
        del u_in
        u_ref[...] = jnp.dot(hn_ref[...], w_ref[...], preferred_element_type=F32)

    def shard(n, m):
        return (m[0] + 1 + n // nb) % s

    body, more_specs, more = _behind(body, 4, after)
    return pl.pallas_call(
        body, name=name,
        grid_spec=pltpu.PrefetchScalarGridSpec(
            num_scalar_prefetch=1, grid=(t // tm, (s - 1) * nb),
            in_specs=[pl.BlockSpec((tm, d), lambda i, n, m: (i, 0)),
                      pl.BlockSpec((None, d, tn), lambda i, n, m: (shard(n, m), 0, n % nb)),
                      ANY] + more_specs,
            out_specs=pl.BlockSpec((tm, tn), lambda i, n, m: (i, shard(n, m) * nb + n % nb))),
        out_shape=jax.ShapeDtypeStruct(u.shape, u.dtype),
        input_output_aliases={3: 0},
        compiler_params=_params(("arbitrary", "arbitrary")),
    )(me_idx, hn, wg, u, *more)


def _decay_consts(lam):
    z = -lam
    e = jnp.exp(-jnp.abs(z))
    u = 1.0 + e
    log1p_e = jnp.where(u == 1.0, e, jnp.log(u) * (e / (u - 1.0)))
    sp = jnp.maximum(z, 0.0) + log1p_e
    return -LRU_C * sp, LRU_C * _sig(z)


def _gates(xc, wr_ref, br_ref, wi_ref, bi_ref, c8, j, gb):
    sl = slice(j * gb, (j + 1) * gb)
    x16 = xc.astype(BF16)
    r = _sig(jnp.dot(x16, wr_ref[j], preferred_element_type=F32) + br_ref[:, sl])
    ig = _sig(jnp.dot(x16, wi_ref[j], preferred_element_type=F32) + bi_ref[:, sl])
    la = c8[:, sl] * r
    a = jnp.exp(la)
    sq = jnp.sqrt(-jnp.tanh(la) * (a * a + 1.0))
    return r, ig, a, sq


def _mix_fwd(u, wa, ba, wr, br, wi, bi, lam, wb, name):
    t = u.shape[0]
    c = u.shape[1] // 6
    tc = MIX_CHUNK
    gb = wr.shape[1]
    nblk = c // gb
    ka, kb = wa.shape[0], wb.shape[0]

    def body(u_ref, wa_ref, ba_ref, wr_ref, br_ref, wi_ref, bi_ref, lam_ref, wb_ref,
             y_ref, hs_ref, xa_ext, v_ext, xc_s, a_s, b_s, carry_s):
        @pl.when(pl.program_id(0) == 0)
        def _():
            xa_ext[0:SUBLANES, :] = jnp.zeros((SUBLANES, c), F32)
            v_ext[0:SUBLANES, :] = jnp.zeros((SUBLANES, c), F32)
            carry_s[...] = jnp.zeros_like(carry_s)

        xa_ext[SUBLANES:SUBLANES + tc, :] = u_ref[:, 0:c]
        xc = ba_ref[...]
        for k in range(ka):
            xc = xc + wa_ref[pl.ds(k, 1), :] * xa_ext[pl.ds(SUBLANES - (ka - 1) + k, tc), :]
        xc_s[...] = xc
        c8, _ = _decay_consts(lam_ref[...])
        for j in range(nblk):
            sl = slice(j * gb, (j + 1) * gb)
            xcj = xc_s[:, sl]
            _, ig, a, sq = _gates(xcj, wr_ref, br_ref, wi_ref, bi_ref, c8, j, gb)
            a_s[:, sl] = a
            b_s[:, sl] = sq * (ig * xcj)

        row = lax.broadcasted_iota(jnp.int32, (SUBLANES, c), 0)

        def scan_step(j, _):
            off = pl.multiple_of(j * SUBLANES, SUBLANES)
            av = a_s[pl.ds(off, SUBLANES), :]
            bv = b_s[pl.ds(off, SUBLANES), :]
            for d in (1, 2, 4):
                keep = row >= d
                bsh = jnp.where(keep, pltpu.roll(bv, d, axis=0), 0.0)
                ash = jnp.where(keep, pltpu.roll(av, d, axis=0), 1.0)
                bv = av * bsh + bv
                av = av * ash
            hv = av * carry_s[...] + bv
            hs_ref[pl.ds(off, SUBLANES), :] = hv
            carry_s[...] = hs_ref[pl.ds(off + SUBLANES - 1, 1), :]
            return 0

        lax.fori_loop(0, tc // SUBLANES, scan_step, 0, unroll=SCAN_UNROLL)

        ga = u_ref[:, c:2 * c]
        y_ref[:, 0:c] = (hs_ref[...] * (ga * _sig(ga))).astype(BF16)

        v_ext[SUBLANES:SUBLANES + tc, :] = u_ref[:, 3 * c:4 * c] * u_ref[:, 4 * c:5 * c]
        cv = wb_ref[pl.ds(0, 1), :] * v_ext[pl.ds(SUBLANES - (kb - 1), tc), :]
        for k in range(1, kb):
            cv = cv + wb_ref[pl.ds(k, 1), :] * v_ext[pl.ds(SUBLANES - (kb - 1) + k, tc), :]
        gbv = u_ref[:, 5 * c:6 * c]
        y_ref[:, c:2 * c] = (u_ref[:, 2 * c:3 * c] * cv * (gbv * _sig(gbv))).astype(BF16)

        xa_ext[0:SUBLANES, :] = xa_ext[tc:tc + SUBLANES, :]
        v_ext[0:SUBLANES, :] = v_ext[tc:tc + SUBLANES, :]

    full = lambda shape: pl.BlockSpec(shape, lambda i: (0,) * len(shape))
    return pl.pallas_call(
        body, name=name, grid=(t // tc,),
        in_specs=[pl.BlockSpec((tc, 6 * c), lambda i: (i, 0)),
                  full(wa.shape), full(ba.shape), full(wr.shape), full(br.shape),
                  full(wi.shape), full(bi.shape), full(lam.shape), full(wb.shape)],
        out_specs=[pl.BlockSpec((tc, 2 * c), lambda i: (i, 0)),
                   pl.BlockSpec((tc, c), lambda i: (i, 0))],
        out_shape=[jax.ShapeDtypeStruct((t, 2 * c), BF16), jax.ShapeDtypeStruct((t, c), F32)],
        scratch_shapes=[pltpu.VMEM((tc + SUBLANES, c), F32), pltpu.VMEM((tc + SUBLANES, c), F32),
                        pltpu.VMEM((tc, c), F32), pltpu.VMEM((tc, c), F32), pltpu.VMEM((tc, c), F32),
                        pltpu.VMEM((1, c), F32)],
        compiler_params=_params(("arbitrary",)),
    )(u, wa, ba, wr, br, wi, bi, lam, wb)


ROW_DWA = 0
ROW_DBA = 4
ROW_DBR = 5
ROW_DBI = 6
ROW_DLAM = 7
ROW_DWB = 8
SMALL_ROWS = 16


def _mix_bwd(u, hs, dy, wa, ba, wr, br, wi, bi, lam, wb, name, after=None):
    t = u.shape[0]
    c = u.shape[1] // 6
    tc = MIX_CHUNK
    nt = t // tc
    gb = wr.shape[1]
    nblk = c // gb
    ka, kb = wa.shape[0], wb.shape[0]
    assert ka <= ROW_DBA and kb <= SMALL_ROWS - ROW_DWB
    hb = tc // SUBLANES

    def body(u_ref, uh_ref, hs_ref, hsh_ref, dy_ref, wa_ref, ba_ref, wr_ref, br_ref, wi_ref, bi_ref, lam_ref, wb_ref,
             du_ref, dsm_ref, dwr_ref, dwi_ref,
             xa_ext, v_ext, hs_ext, a_ext, ds_ext, dxc_ext, dcv_ext, xc_s, r_s, i_s, sq_s, g_s, an_s):
        i = pl.program_id(0)
        chunk = nt - 1 - i
        tail = slice(tc, tc + SUBLANES)
        head = slice(0, SUBLANES)

        @pl.when(i == 0)
        def _():
            zero = jnp.zeros((SUBLANES, c), F32)
            a_ext[tail, :] = zero
            ds_ext[tail, :] = zero
            dxc_ext[tail, :] = zero
            dcv_ext[tail, :] = zero
            dsm_ref[...] = jnp.zeros_like(dsm_ref)
            dwr_ref[...] = jnp.zeros_like(dwr_ref)
            dwi_ref[...] = jnp.zeros_like(dwi_ref)

        prev = jnp.where(chunk > 0, 1.0, 0.0)
        xa_ext[head, :] = uh_ref[:, 0:c] * prev
        xa_ext[SUBLANES:SUBLANES + tc, :] = u_ref[:, 0:c]
        v_ext[head, :] = uh_ref[:, 3 * c:4 * c] * uh_ref[:, 4 * c:5 * c] * prev
        v_ext[SUBLANES:SUBLANES + tc, :] = u_ref[:, 3 * c:4 * c] * u_ref[:, 4 * c:5 * c]
        hs_ext[head, :] = hsh_ref[...] * prev
        hs_ext[SUBLANES:SUBLANES + tc, :] = hs_ref[...]

        xc = ba_ref[...]
        for k in range(ka):
            xc = xc + wa_ref[pl.ds(k, 1), :] * xa_ext[pl.ds(SUBLANES - (ka - 1) + k, tc), :]
        xc_s[...] = xc
        c8, dc8 = _decay_consts(lam_ref[...])
        for j in range(nblk):
            sl = slice(j * gb, (j + 1) * gb)
            r, ig, a, sq = _gates(xc_s[:, sl], wr_ref, br_ref, wi_ref, bi_ref, c8, j, gb)
            r_s[:, sl] = r
            i_s[:, sl] = ig
            sq_s[:, sl] = sq
            a_ext[0:tc, sl] = a

        ga = u_ref[:, c:2 * c]
        sga = _sig(ga)
        g_s[...] = dy_ref[:, 0:c] * (ga * sga)
        an_s[...] = a_ext[pl.ds(1, tc), :]

        row = lax.broadcasted_iota(jnp.int32, (SUBLANES, c), 0)

        def scan_step(j, _):
            off = pl.multiple_of(tc - SUBLANES - j * SUBLANES, SUBLANES)
            av = an_s[pl.ds(off, SUBLANES), :]
            bv = g_s[pl.ds(off, SUBLANES), :]
            for d in (1, 2, 4):
                keep = row < SUBLANES - d
                bsh = jnp.where(keep, pltpu.roll(bv, SUBLANES - d, axis=0), 0.0)
                ash = jnp.where(keep, pltpu.roll(av, SUBLANES - d, axis=0), 1.0)
                bv = av * bsh + bv
                av = av * ash
            ds_ext[pl.ds(off, SUBLANES), :] = av * ds_ext[pl.ds(off + SUBLANES, 1), :] + bv
            return 0

        lax.fori_loop(0, tc // SUBLANES, scan_step, 0, unroll=SCAN_UNROLL)

        def acc(row_index, val):
            dsm_ref[pl.ds(row_index, 1), :] += jnp.sum(val, axis=0, keepdims=True)

        def acc_block(row_index, sl, val):
            dsm_ref[pl.ds(row_index, 1), sl] += jnp.sum(val, axis=0, keepdims=True)

        for j in range(nblk):
            sl = slice(j * gb, (j + 1) * gb)
            ds = ds_ext[0:tc, sl]
            hprev = hs_ext[pl.ds(SUBLANES - 1, tc), sl]
            a = a_ext[0:tc, sl]
            sq = sq_s[:, sl]
            ig = i_s[:, sl]
            r = r_s[:, sl]
            xcj = xc_s[:, sl]
            t1 = ds * xcj
            dla = (ds * hprev) * a - (t1 * ig) * ((a * a) / sq)
            acc_block(ROW_DLAM, sl, dla * r)
            dpr = (dla * c8[:, sl]) * (r * (1.0 - r))
            dpi = (t1 * sq) * (ig * (1.0 - ig))
            acc_block(ROW_DBR, sl, dpr)
            acc_block(ROW_DBI, sl, dpi)
            p16 = dpr.astype(BF16)
            q16 = dpi.astype(BF16)
            x16 = xcj.astype(BF16)
            dwr_ref[j] += lax.dot_general(x16, p16, TN_DIMS, preferred_element_type=F32)
            dwi_ref[j] += lax.dot_general(x16, q16, TN_DIMS, preferred_element_type=F32)
            dxc = (ds * (sq * ig)
                   + lax.dot_general(p16, wr_ref[j], NT_DIMS, preferred_element_type=F32)
                   + lax.dot_general(q16, wi_ref[j], NT_DIMS, preferred_element_type=F32))
            dxc_ext[0:tc, sl] = dxc
            acc_block(ROW_DBA, sl, dxc)

        dsilu_a = sga * (1.0 + ga * (1.0 - sga))
        du_ref[:, c:2 * c] = (dy_ref[:, 0:c] * hs_ref[...] * dsilu_a).astype(BF16)

        dxc = dxc_ext[0:tc, :]
        dxa = wa_ref[pl.ds(ka - 1, 1), :] * dxc
        acc(ROW_DWA + ka - 1, dxc * xa_ext[SUBLANES:SUBLANES + tc, :])
        for k in range(ka - 1):
            acc(ROW_DWA + k, dxc * xa_ext[pl.ds(SUBLANES - (ka - 1) + k, tc), :])
            dxa = dxa + wa_ref[pl.ds(k, 1), :] * dxc_ext[pl.ds(ka - 1 - k, tc), :]
        du_ref[:, 0:c] = dxa.astype(BF16)

        cv = wb_ref[pl.ds(0, 1), :] * v_ext[pl.ds(SUBLANES - (kb - 1), tc), :]
        for k in range(1, kb):
            cv = cv + wb_ref[pl.ds(k, 1), :] * v_ext[pl.ds(SUBLANES - (kb - 1) + k, tc), :]
        gbv = u_ref[:, 5 * c:6 * c]
        sgb = _sig(gbv)
        silu_b = gbv * sgb
        dyb = dy_ref[:, c:2 * c]
        gB = u_ref[:, 2 * c:3 * c]
        du_ref[:, 2 * c:3 * c] = (dyb * cv * silu_b).astype(BF16)
        du_ref[:, 5 * c:6 * c] = (dyb * gB * cv * (sgb * (1.0 + gbv * (1.0 - sgb)))).astype(BF16)
        dcv = dyb * gB * silu_b
        dcv_ext[0:tc, :] = dcv
        dv = wb_ref[pl.ds(kb - 1, 1), :] * dcv
        acc(ROW_DWB + kb - 1, dcv * v_ext[SUBLANES:SUBLANES + tc, :])
        for k in range(kb - 1):
            acc(ROW_DWB + k, dcv * v_ext[pl.ds(SUBLANES - (kb - 1) + k, tc), :])
            dv = dv + wb_ref[pl.ds(k, 1), :] * dcv_ext[pl.ds(kb - 1 - k, tc), :]
        du_ref[:, 3 * c:4 * c] = (dv * u_ref[:, 4 * c:5 * c]).astype(BF16)
        du_ref[:, 4 * c:5 * c] = (dv * u_ref[:, 3 * c:4 * c]).astype(BF16)

        a_ext[tail, :] = a_ext[head, :]
        ds_ext[tail, :] = ds_ext[head, :]
        dxc_ext[tail, :] = dxc_ext[head, :]
        dcv_ext[tail, :] = dcv_ext[head, :]

        @pl.when(i == nt - 1)
        def _():
            dsm_ref[pl.ds(ROW_DLAM, 1), :] = dsm_ref[pl.ds(ROW_DLAM, 1), :] * dc8

    full = lambda shape: pl.BlockSpec(shape, lambda i: (0,) * len(shape))
    rev = lambda i: (nt - 1 - i, 0)
    halo = lambda i: (jnp.maximum((nt - 1 - i) * hb - 1, 0), 0)
    ext = pltpu.VMEM((tc + SUBLANES, c), F32)
    blk = pltpu.VMEM((tc, c), F32)
    body, more_specs, more = _behind(body, 13, after)
    return pl.pallas_call(
        body, name=name, grid=(nt,),
        in_specs=[pl.BlockSpec((tc, 6 * c), rev), pl.BlockSpec((SUBLANES, 6 * c), halo),
                  pl.BlockSpec((tc, c), rev), pl.BlockSpec((SUBLANES, c), halo),
                  pl.BlockSpec((tc, 2 * c), rev),
                  full(wa.shape), full(ba.shape), full(wr.shape), full(br.shape),
                  full(wi.shape), full(bi.shape), full(lam.shape), full(wb.shape)] + more_specs,
        out_specs=[pl.BlockSpec((tc, 6 * c), rev), full((SMALL_ROWS, c)), full(wr.shape), full(wi.shape)],
        out_shape=[jax.ShapeDtypeStruct((t, 6 * c), BF16), jax.ShapeDtypeStruct((SMALL_ROWS, c), F32),
                   jax.ShapeDtypeStruct(wr.shape, F32), jax.ShapeDtypeStruct(wi.shape, F32)],
        scratch_shapes=[ext] * 7 + [blk] * 6,
        compiler_params=_params(("arbitrary",)),
    )(u, u, hs, hs, dy, wa, ba, wr, br, wi, bi, lam, wb, *more)


def _behind(body, n_in, after):
    if after is None:
        return body, [], []
    return (lambda *refs: body(*refs[:n_in], *refs[n_in + 1:])), [ANY], [after]


def _out_proj(h, y, w, name, after=None):
    t, d = h.shape
    dm = y.shape[1]
    tm = _row_tile(t)
    tn = _col_tile(d, (2048, 1024, 512, 256))

    def body(h_ref, y_ref, w_ref, o_ref):
        o_ref[...] = h_ref[...] + jnp.dot(y_ref[...], w_ref[...], preferred_element_type=F32)

    body, more_specs, more = _behind(body, 3, after)
    return pl.pallas_call(
        body, name=name, grid=(d // tn, t // tm),
        in_specs=[pl.BlockSpec((tm, tn), lambda n, i: (i, n)),
                  pl.BlockSpec((tm, dm), lambda n, i: (i, 0)),
                  pl.BlockSpec((dm, tn), lambda n, i: (0, n))] + more_specs,
        out_specs=pl.BlockSpec((tm, tn), lambda n, i: (i, n)),
        out_shape=jax.ShapeDtypeStruct((t, d), F32),
        compiler_params=_params(("arbitrary", "arbitrary")),
    )(h, y, w, *more)


def _out_proj_norm(h, y, w, g_next, name, after=None):
    t, d = h.shape
    dm = y.shape[1]
    tm = _row_tile(t)

    def body(h_ref, y_ref, w_ref, g_ref, o_ref, hn_ref):
        x = h_ref[...] + jnp.dot(y_ref[...], w_ref[...], preferred_element_type=F32)
        o_ref[...] = x
        r = lax.rsqrt(jnp.mean(x * x, axis=-1, keepdims=True) + RMS_EPS)
        hn_ref[...] = ((x * r) * g_ref[...]).astype(BF16)

    body, more_specs, more = _behind(body, 4, after)
    rows = pl.BlockSpec((tm, d), lambda i: (i, 0))
    return pl.pallas_call(
        body, name=name, grid=(t // tm,),
        in_specs=[rows, pl.BlockSpec((tm, dm), lambda i: (i, 0)), pl.BlockSpec((dm, d), lambda i: (0, 0)),
                  pl.BlockSpec((1, d), lambda i: (0, 0))] + more_specs,
        out_specs=[rows, rows],
        out_shape=[jax.ShapeDtypeStruct((t, d), F32), jax.ShapeDtypeStruct((t, d), BF16)],
        compiler_params=_params(("arbitrary",)),
    )(h, y, w, g_next, *more)


def _in_proj(hn, wg, name, after=None):
    t, d = hn.shape
    s, _, ns = wg.shape
    tm = 1408 if t % 1408 == 0 else _row_tile(t)

    def body(hn_ref, w_ref, u_ref):
        u_ref[...] = jnp.dot(hn_ref[...], w_ref[...], preferred_element_type=F32)

    body, more_specs, more = _behind(body, 2, after)
    return pl.pallas_call(
        body, name=name, grid=(t // tm, s),
        in_specs=[pl.BlockSpec((tm, d), lambda i, n: (i, 0)),
                  pl.BlockSpec((None, d, ns), lambda i, n: (n, 0, 0))] + more_specs,
        out_specs=pl.BlockSpec((tm, ns), lambda i, n: (i, n)),
        out_shape=jax.ShapeDtypeStruct((t, s * ns), F32),
        compiler_params=_params(("arbitrary", "arbitrary")),
    )(hn, wg, *more)


def _out_proj_dy(dout, w, name, after=None):
    t, d = dout.shape
    dm = w.shape[0]
    tm = _row_tile(t)
    tn = _col_tile(dm, (2048, 1024, 512, 256))

    def body(g_ref, w_ref, o_ref):
        o_ref[...] = lax.dot_general(g_ref[...].astype(BF16), w_ref[...], NT_DIMS, preferred_element_type=F32)

    body, more_specs, more = _behind(body, 2, after)
    return pl.pallas_call(
        body, name=name, grid=(dm // tn, t // tm),
        in_specs=[pl.BlockSpec((tm, d), lambda n, i: (i, 0)),
                  pl.BlockSpec((tn, d), lambda n, i: (n, 0))] + more_specs,
        out_specs=pl.BlockSpec((tm, tn), lambda n, i: (i, n)),
        out_shape=jax.ShapeDtypeStruct((t, dm), F32),
        compiler_params=_params(("arbitrary", "arbitrary")),
    )(dout, w, *more)


def _out_proj_dw(y, dout, name):
    t, dm = y.shape
    d = dout.shape[1]
    tmm = _col_tile(dm, (1024, 512, 256))
    tn = _col_tile(d, (512, 256))

    def body(y_ref, g_ref, o_ref):
        o_ref[...] = lax.dot_general(y_ref[...], g_ref[...].astype(BF16), TN_DIMS, preferred_element_type=F32)

    return pl.pallas_call(
        body, name=name, grid=(d // tn, dm // tmm),
        in_specs=[pl.BlockSpec((t, tmm), lambda n, m: (0, m)),
                  pl.BlockSpec((t, tn), lambda n, m: (0, n))],
        out_specs=pl.BlockSpec((tmm, tn), lambda n, m: (m, n)),
        out_shape=jax.ShapeDtypeStruct((dm, d), F32),
        compiler_params=_params(("arbitrary", "arbitrary")),
    )(y, dout)


def _in_proj_bwd(du, wg, h, g, dout, name, after=None, split=None):
    t, d = h.shape
    s, _, ns = wg.shape
    tm = _row_tile(t)
    tn = _col_tile(d, (1024, 512, 256))

    def mm_body(du_ref, w_ref, o_ref):
        total = lax.dot_general(du_ref[:, 0:ns], w_ref[0], NT_DIMS, preferred_element_type=F32)
        for a in range(1, s):
            total = total + lax.dot_general(du_ref[:, a * ns:(a + 1) * ns], w_ref[a], NT_DIMS,
                                            preferred_element_type=F32)
        o_ref[...] = total

    mm_body, more_specs, more = _behind(mm_body, 2, after)
    dhn = pl.pallas_call(
        mm_body, name=name, grid=(t // tm, d // tn),
        in_specs=[pl.BlockSpec((tm, s * ns), lambda i, n: (i, 0)),
                  pl.BlockSpec((s, tn, ns), lambda i, n: (0, n, 0))] + more_specs,
        out_specs=pl.BlockSpec((tm, tn), lambda i, n: (i, n)),
        out_shape=jax.ShapeDtypeStruct((t, d), F32),
        compiler_params=_params(("arbitrary", "arbitrary")),
    )(du, wg, *more)

    tr = 352 if t % 352 == 0 else 192
    nt = t // tr

    def row_grad(dhn_ref, h_ref, g_ref, dout_ref, dg_ref):
        @pl.when(pl.program_id(0) == 0)
        def _():
            dg_ref[...] = jnp.zeros_like(dg_ref)

        x = h_ref[...]
        dn = dhn_ref[...]
        r = lax.rsqrt(jnp.mean(x * x, axis=-1, keepdims=True) + RMS_EPS)
        gd = dn * g_ref[...]
        dot = jnp.mean(gd * x, axis=-1, keepdims=True)
        dg_ref[...] += jnp.sum(dn * (x * r), axis=0, keepdims=True)
        return dout_ref[...] + (r * gd - x * ((r * r * r) * dot))

    rows = pl.BlockSpec((tr, d), lambda i: (i, 0))
    one = pl.BlockSpec((1, d), lambda i: (0, 0))
    if split is None:
        def norm_body(dhn_ref, h_ref, g_ref, dout_ref, dh_ref, dg_ref):
            dh_ref[...] = row_grad(dhn_ref, h_ref, g_ref, dout_ref, dg_ref)

        return pl.pallas_call(
            norm_body, name=name + "_norm", grid=(nt,),
            in_specs=[rows, rows, one, rows], out_specs=[rows, one],
            out_shape=[jax.ShapeDtypeStruct((t, d), F32), jax.ShapeDtypeStruct((1, d), F32)],
            compiler_params=_params(("arbitrary",)),
        )(dhn, h, g, dout)

    n_head, n_body = split
    n_first = tr - n_head
    n_last = n_head + n_body - (nt - 1) * tr
    assert nt >= 2 and 0 < n_head < tr and 0 < n_last <= tr and n_head % SUBLANES == 0 and n_last % SUBLANES == 0

    def split_body(dhn_ref, h_ref, g_ref, dout_ref, body_ref, head_ref, dg_ref, stage, sems):
        i = pl.program_id(0)
        slot = i % 2

        def first_copy(sl):
            return pltpu.make_async_copy(stage.at[sl, pl.ds(n_head, n_first)], body_ref.at[pl.ds(0, n_first)], sems.at[sl])

        def middle_copy(sl, step):
            start = pl.multiple_of(step * tr - n_head, SUBLANES)
            return pltpu.make_async_copy(stage.at[sl], body_ref.at[pl.ds(start, tr)], sems.at[sl])

        def last_copy(sl):
            return pltpu.make_async_copy(stage.at[sl, pl.ds(0, n_last)],
                                         body_ref.at[pl.ds((nt - 1) * tr - n_head, n_last)], sems.at[sl])

        dh = row_grad(dhn_ref, h_ref, g_ref, dout_ref, dg_ref)

        @pl.when(i == 2)
        def _():
            first_copy(0).wait()

        @pl.when(i > 2)
        def _():
            middle_copy(slot, i - 2).wait()

        stage[slot] = dh

        @pl.when(i == 0)
        def _():
            head_ref[...] = stage[0, 0:n_head, :]
            first_copy(0).start()

        @pl.when((i > 0) & (i < nt - 1))
        def _():
            middle_copy(slot, i).start()

        @pl.when(i == nt - 1)
        def _():
            last = last_copy((nt - 1) % 2)
            last.start()
            if nt == 2:
                first_copy(0).wait()
            else:
                middle_copy((nt - 2) % 2, nt - 2).wait()
            last.wait()

    return pl.pallas_call(
        split_body, name=name + "_norm", grid=(nt,),
        in_specs=[rows, rows, one, rows],
        out_specs=[ANY, pl.BlockSpec((n_head, d), lambda i: (0, 0)), one],
        out_shape=[jax.ShapeDtypeStruct((n_body, d), F32), jax.ShapeDtypeStruct((n_head, d), F32),
                   jax.ShapeDtypeStruct((1, d), F32)],
        scratch_shapes=[pltpu.VMEM((2, tr, d), F32), pltpu.SemaphoreType.DMA((2,))],
        compiler_params=_params(("arbitrary",)),
    )(dhn, h, g, dout)


def _in_proj_dw(hn, du, s, name, after=None):
    t, d = hn.shape
    ns = du.shape[1] // s
    tmm = _col_tile(d, (1024, 512, 256))
    tn = _col_tile(ns, (768, 384, 128))
    nb = ns // tn

    def body(hn_ref, du_ref, o_ref):
        o_ref[...] = lax.dot_general(hn_ref[...], du_ref[...], TN_DIMS, preferred_element_type=F32)

    body, more_specs, more = _behind(body, 2, after)
    return pl.pallas_call(
        body, name=name, grid=(s * nb, d // tmm),
        in_specs=[pl.BlockSpec((t, tmm), lambda n, m: (0, m)),
                  pl.BlockSpec((t, tn), lambda n, m: (0, n))] + more_specs,
        out_specs=pl.BlockSpec((None, tmm, tn), lambda n, m: (n // nb, m, n % nb)),
        out_shape=jax.ShapeDtypeStruct((s, d, ns), F32),
        compiler_params=_params(("arbitrary", "arbitrary")),
    )(hn, du, *more)


def _loss_head(h, tgt, g, n_meta, t_real, name):
    t, d = h.shape
    tm = _row_tile(t)

    def body(h_ref, t_ref, g_ref, dh_ref, loss_ref, dg_ref):
        i = pl.program_id(0)

        @pl.when(i == 0)
        def _():
            loss_ref[...] = jnp.zeros_like(loss_ref)
            dg_ref[...] = jnp.zeros_like(dg_ref)

        x = h_ref[...]
        gv = g_ref[...]
        r = lax.rsqrt(jnp.mean(x * x, axis=-1, keepdims=True) + RMS_EPS)
        xr = x * r
        rows = i * tm + lax.broadcasted_iota(jnp.int32, (tm, 1), 0)
        valid = (rows >= n_meta) & (rows < t_real)
        err = jnp.where(valid, xr * gv - t_ref[...], 0.0)
        loss_ref[...] += 0.5 * jnp.sum(jnp.mean(err * err, axis=-1, keepdims=True))
        dy = err * (1.0 / d)
        gd = dy * gv
        dot = jnp.mean(gd * x, axis=-1, keepdims=True)
        dh_ref[...] = r * gd - x * ((r * r * r) * dot)
        dg_ref[...] += jnp.sum(dy * xr, axis=0, keepdims=True)

    return pl.pallas_call(
        body, name=name, grid=(t // tm,),
        in_specs=[pl.BlockSpec((tm, d), lambda i: (i, 0)),
                  pl.BlockSpec((tm, d), lambda i: (i, 0)),
                  pl.BlockSpec((1, d), lambda i: (0, 0))],
        out_specs=[pl.BlockSpec((tm, d), lambda i: (i, 0)),
                   pl.BlockSpec((1, LANES), lambda i: (0, 0)),
                   pl.BlockSpec((1, d), lambda i: (0, 0))],
        out_shape=[jax.ShapeDtypeStruct((t, d), F32), jax.ShapeDtypeStruct((1, LANES), F32),
                   jax.ShapeDtypeStruct((1, d), F32)],
        compiler_params=_params(("arbitrary",)),
    )(h, tgt, g)


def _adamw_rows(rows, cols):
    for cand in (512, 256, 128, 64, 32, 16, 8):
        if rows % cand == 0 and cand * cols * 4 <= 2 * 1024 * 1024:
            return cand
    return rows


def _adamw_math(w_ref, g_ref, m_ref, v_ref, d_ref, nm_ref, nv_ref):
    gv = g_ref[...]
    m2 = ADAM_B1 * m_ref[...] + (1.0 - ADAM_B1) * gv
    v2 = ADAM_B2 * v_ref[...] + (1.0 - ADAM_B2) * (gv * gv)
    m_hat = m2 / (1.0 - ADAM_B1 ** ADAM_STEP)
    v_hat = v2 / (1.0 - ADAM_B2 ** ADAM_STEP)
    d_ref[...] = -ADAM_LR * (m_hat / (jnp.sqrt(v_hat) + ADAM_EPS) + ADAM_WD * w_ref[...])
    nm_ref[...] = m2
    nv_ref[...] = v2


def _adamw(w, g, m, v, name):
    shape = w.shape
    assert len(shape) >= 2 and w.size * 4 <= 2 * 1024 * 1024

    def body(*refs):
        _adamw_math(*refs)

    spec = pl.BlockSpec(shape, lambda i: (0,) * len(shape))
    return pl.pallas_call(
        body, name=name, grid=(1,),
        in_specs=[spec] * 4, out_specs=[spec] * 3,
        out_shape=[jax.ShapeDtypeStruct(shape, F32)] * 3,
        compiler_params=_params(("arbitrary",)),
    )(w, g, m, v)


def _adamw_layer(w, g, m, v, layer, kept, name, after=None):
    nl, rows, cols = w.shape
    tr = _adamw_rows(rows, cols)
    n_kept = 0 if kept is None else 3

    def body(*refs):
        _adamw_math(*refs[:4], *refs[4 + n_kept:])

    body, more_specs, more = _behind(body, 4 + n_kept, after)
    lay = pl.BlockSpec((None, tr, cols), lambda i: (layer, i, 0))
    return pl.pallas_call(
        body, name=name, grid=(rows // tr,),
        in_specs=[lay, pl.BlockSpec((tr, cols), lambda i: (i, 0)), lay, lay] + [ANY] * n_kept + more_specs,
        out_specs=[lay] * 3,
        out_shape=[jax.ShapeDtypeStruct((nl, rows, cols), F32)] * 3,
        input_output_aliases={4 + k: k for k in range(n_kept)},
        compiler_params=_params(("arbitrary",)),
    )(w, g, m, v, *([] if kept is None else kept), *more)


def _pair_add(x, ra, c_idx, name):
    s, _, rows, cols = x.shape
    tr = _slab_rows(rows, cols)

    def body(c_ref, x_ref, r_ref, o_ref):
        o_ref[...] = (x_ref[...] + r_ref[...]).astype(BF16)

    return pl.pallas_call(
        body, name=name,
        grid_spec=pltpu.PrefetchScalarGridSpec(
            num_scalar_prefetch=1, grid=(s, rows // tr),
            in_specs=[pl.BlockSpec((None, None, tr, cols), lambda a, i, c_ref: (a, c_ref[0], i, 0)),
                      pl.BlockSpec((None, tr, cols), lambda a, i, c_ref: (a, i, 0))],
            out_specs=pl.BlockSpec((None, tr, cols), lambda a, i, c_ref: (a, i, 0))),
        out_shape=jax.ShapeDtypeStruct((s, rows, cols), BF16),
        compiler_params=_params(("arbitrary", "arbitrary")),
    )(c_idx, x, ra)


def _chip_sum(rc, p, where, n_slots, name):
    s, rows, cols = rc.shape
    tr = _slab_rows(rows, cols)

    def body(w_ref, x_ref, p_ref, o_ref):
        me = w_ref[0]
        total = jnp.where(me == 0, p_ref[...], x_ref[0]).astype(F32)
        for a in range(1, s):
            total = total + jnp.where(me == a, p_ref[...], x_ref[a]).astype(F32)
        o_ref[...] = total

    return pl.pallas_call(
        body, name=name,
        grid_spec=pltpu.PrefetchScalarGridSpec(
            num_scalar_prefetch=1, grid=(rows // tr,),
            in_specs=[pl.BlockSpec((s, tr, cols), lambda i, w_ref: (0, i, 0)),
                      pl.BlockSpec((None, tr, cols), lambda i, w_ref: (w_ref[0], i, 0))],
            out_specs=pl.BlockSpec((None, tr, cols), lambda i, w_ref: (w_ref[1], i, 0))),
        out_shape=jax.ShapeDtypeStruct((n_slots, rows, cols), F32),
        compiler_params=_params(("arbitrary",)),
    )(where, rc, p)


def _cast_place(w, layer, me_idx, name, after=None):
    _, rows, cols = w.shape
    tr = _slab_rows(rows, cols)

    def body(m_ref, w_ref, o_ref):
        o_ref[...] = w_ref[...].astype(BF16)

    body, more_specs, more = _behind(body, 2, after)
    return pl.pallas_call(
        body, name=name,
        grid_spec=pltpu.PrefetchScalarGridSpec(
            num_scalar_prefetch=1, grid=(rows // tr,),
            in_specs=[pl.BlockSpec((None, tr, cols), lambda i, m_ref: (layer, i, 0))] + more_specs,
            out_specs=pl.BlockSpec((None, tr, cols), lambda i, m_ref: (m_ref[0], i, 0))),
        out_shape=jax.ShapeDtypeStruct((N_CHIPS, rows, cols), BF16),
        compiler_params=_params(("arbitrary",)),
    )(me_idx, w, *more)


def _place():
    x, y, c = lax.axis_index("x"), lax.axis_index("y"), lax.axis_index("c")
    chips = [(1 - x, y), (x, 1 - y), (1 - x, 1 - y)]
    return x, y, c, chips


def _chip_index(cx, cy):
    return 2 * cx + cy


def _gather_copies(bufs, stage):
    x, y, c, chips = _place()
    me = _chip_index(x, y)
    copies = []
    for b in bufs:
        for chip in chips:
            src = _chip_index(*chip)
            if stage == 0:
                copies.append((b.at[me, c], (*chip, c), b.at[src, c]))
            else:
                copies.append((b.at[src, c], (x, y, 1 - c), b.at[src, 1 - c]))
    return copies


def _remote(ref, peer, ssem, rsem, k):
    return pltpu.make_async_remote_copy(src_ref=ref, dst_ref=ref, send_sem=ssem.at[k], recv_sem=rsem.at[k],
                                        device_id=peer, device_id_type=MESH)


def _gather_first(bufs, small):
    n = len(bufs)
    k = 3 * n

    def body(*refs):
        sm_ref = refs[n]
        b_refs, smg_ref = refs[n + 1:2 * n + 1], refs[2 * n + 1]
        lsem, ssem, rsem = refs[2 * n + 2:]
        x, y, c, chips = _place()
        me = _chip_index(x, y)
        local = pltpu.make_async_copy(sm_ref, smg_ref.at[me], lsem)
        local.start()
        first = _gather_copies(b_refs, 0)
        second = _gather_copies(b_refs, 1)
        started = []
        for i, (ref, peer, _) in enumerate(first):
            started.append(_remote(ref, peer, ssem, rsem, i))
        for j, chip in enumerate(chips):
            started.append(pltpu.make_async_remote_copy(
                src_ref=sm_ref, dst_ref=smg_ref.at[me], send_sem=ssem.at[2 * k + j], recv_sem=rsem.at[2 * k + j],
                device_id=(*chip, c), device_id_type=MESH))
        for cp in started:
            cp.start()
        for i, (_, peer, lands) in enumerate(first):
            _remote(lands, peer, ssem, rsem, i).wait_recv()
            ref, sib, _ = second[i]
            fwd = _remote(ref, sib, ssem, rsem, k + i)
            fwd.start()
            started.append(fwd)
        for i, (_, sib, lands) in enumerate(second):
            _remote(lands, sib, ssem, rsem, k + i).wait_recv()
        for j, chip in enumerate(chips):
            theirs = smg_ref.at[_chip_index(*chip)]
            pltpu.make_async_remote_copy(src_ref=theirs, dst_ref=theirs, send_sem=ssem.at[2 * k + j],
                                         recv_sem=rsem.at[2 * k + j], device_id=(*chip, c),
                                         device_id_type=MESH).wait_recv()
        for cp in started:
            cp.wait_send()
        local.wait()

    return pl.pallas_call(
        body, name="gather_first",
        in_specs=[ANY] * (n + 1), out_specs=[ANY] * (n + 1),
        out_shape=[jax.ShapeDtypeStruct(b.shape, b.dtype) for b in bufs]
        + [jax.ShapeDtypeStruct((N_CHIPS,) + small.shape, small.dtype)],
        input_output_aliases={i: i for i in range(n)},
        scratch_shapes=[pltpu.SemaphoreType.DMA, pltpu.SemaphoreType.DMA((2 * k + 3,)),
                        pltpu.SemaphoreType.DMA((2 * k + 3,))],
    )(*bufs, small)


HBM = pl.BlockSpec(memory_space=pltpu.HBM)
SEM = pl.BlockSpec(memory_space=pltpu.SEMAPHORE)
DATAFLOW = pltpu.SideEffectType.DATAFLOW_SIDE_EFFECTING


def _copies_start(bufs, plan, n_copies, name, after=None):
    n = len(bufs)
    extra = [] if after is None else [after]

    def body(*refs):
        refs = refs[:n] + refs[n + len(extra):]
        ssem, rsem = refs[n], refs[n + 1]
        b_refs, token = refs[n + 2:2 * n + 2], refs[2 * n + 2]
        copies = plan(b_refs)
        assert len(copies) == n_copies
        for i, (src, dst, peer, _) in enumerate(copies):
            pltpu.make_async_remote_copy(src_ref=src, dst_ref=dst, send_sem=ssem.at[i], recv_sem=rsem.at[i],
                                         device_id=peer, device_id_type=MESH).start()
        token[...] = jnp.zeros_like(token)

    return pl.pallas_call(
        body, name=name,
        out_shape=(pltpu.SemaphoreType.DMA((n_copies,)), pltpu.SemaphoreType.DMA((n_copies,)),
                   *[pltpu.HBM(b.shape, b.dtype) for b in bufs], jax.ShapeDtypeStruct((SUBLANES, LANES), F32)),
        in_specs=[HBM] * n + [ANY] * len(extra),
        out_specs=(SEM, SEM, *[HBM] * n, pl.BlockSpec(memory_space=pltpu.VMEM)),
        input_output_aliases={i: 2 + i for i in range(n)},
        compiler_params=pltpu.CompilerParams(has_side_effects=DATAFLOW),
    )(*[pltpu.with_memory_space_constraint(b, pltpu.HBM) for b in bufs], *extra)


def _copies_wait(bufs, ssem, rsem, after, plan, name):
    n = len(bufs)
    afters = list(after) if isinstance(after, (list, tuple)) else [after]

    def body(*refs):
        b_refs, ssem_ref, rsem_ref = refs[:n], refs[n], refs[n + 1]
        for i, (src, dst, peer, lands) in enumerate(plan(b_refs)):
            pltpu.make_async_remote_copy(src_ref=src, dst_ref=dst, send_sem=ssem_ref.at[i], recv_sem=rsem_ref.at[i],
                                         device_id=peer, device_id_type=MESH).wait_send()
            pltpu.make_async_remote_copy(src_ref=lands, dst_ref=lands, send_sem=ssem_ref.at[i],
                                         recv_sem=rsem_ref.at[i], device_id=peer, device_id_type=MESH).wait_recv()

    return pl.pallas_call(
        body, name=name,
        out_shape=tuple(pltpu.HBM(b.shape, b.dtype) for b in bufs),
        in_specs=[HBM] * n + [SEM, SEM] + [ANY] * len(afters), out_specs=tuple([HBM] * n),
        input_output_aliases={i: i for i in range(n)},
        compiler_params=pltpu.CompilerParams(has_side_effects=DATAFLOW),
    )(*bufs, ssem, rsem, *afters)


def _gather_plan(stage):
    return lambda refs: [(ref, ref, peer, lands) for ref, peer, lands in _gather_copies(refs, stage)]


def _swap_plan(refs):
    n = len(refs) // 2
    x, y, c, _ = _place()
    return [(refs[a].at[:, 1 - c], refs[n + a], (x, y, 1 - c), refs[n + a]) for a in range(n)]


def _scatter_plan(refs):
    n = len(refs) // 2
    x, y, c, chips = _place()
    me = _chip_index(x, y)
    return [(refs[a].at[_chip_index(*chip)], refs[n + a].at[me], (*chip, c), refs[n + a].at[_chip_index(*chip)])
            for a in range(n) for chip in chips]


def _pair_gather_plan(refs):
    x, y, c, _ = _place()
    return [(r.at[c], r.at[c], (x, y, 1 - c), r.at[1 - c]) for r in refs]


def _pair_swap(xs, name):
    n = len(xs)

    def body(*refs):
        x_refs, o_refs, ssem, rsem = refs[:n], refs[n:2 * n], refs[2 * n], refs[2 * n + 1]
        x, y, c, _ = _place()
        copies = [pltpu.make_async_remote_copy(src_ref=x_refs[a].at[:, 1 - c], dst_ref=o_refs[a],
                                               send_sem=ssem.at[a], recv_sem=rsem.at[a],
                                               device_id=(x, y, 1 - c), device_id_type=MESH) for a in range(n)]
        for cp in copies:
            cp.start()
        for cp in copies:
            cp.wait()

    return pl.pallas_call(
        body, name=name, in_specs=[ANY] * n, out_specs=[ANY] * n,
        out_shape=[jax.ShapeDtypeStruct((a.shape[0],) + a.shape[2:], a.dtype) for a in xs],
        scratch_shapes=[pltpu.SemaphoreType.DMA((n,)), pltpu.SemaphoreType.DMA((n,))],
    )(*xs)


def _chip_scatter(ps):
    n = len(ps)

    def body(*refs):
        p_refs, o_refs, ssem, rsem = refs[:n], refs[n:2 * n], refs[2 * n], refs[2 * n + 1]
        x, y, c, chips = _place()
        me = _chip_index(x, y)
        sends = []
        for a in range(n):
            for j, chip in enumerate(chips):
                sends.append(pltpu.make_async_remote_copy(
                    src_ref=p_refs[a].at[_chip_index(*chip)], dst_ref=o_refs[a].at[me],
                    send_sem=ssem.at[3 * a + j], recv_sem=rsem.at[3 * a + j],
                    device_id=(*chip, c), device_id_type=MESH))
        for cp in sends:
            cp.start()
        for a in range(n):
            for j, chip in enumerate(chips):
                src = _chip_index(*chip)
                pltpu.make_async_remote_copy(
                    src_ref=p_refs[a].at[src], dst_ref=o_refs[a].at[src],
                    send_sem=ssem.at[3 * a + j], recv_sem=rsem.at[3 * a + j],
                    device_id=(*chip, c), device_id_type=MESH).wait_recv()
        for cp in sends:
            cp.wait_send()

    return pl.pallas_call(
        body, name="chip_scatter", in_specs=[ANY] * n, out_specs=[ANY] * n,
        out_shape=[jax.ShapeDtypeStruct(a.shape, a.dtype) for a in ps],
        scratch_shapes=[pltpu.SemaphoreType.DMA((3 * n,)), pltpu.SemaphoreType.DMA((3 * n,))],
    )(*ps)


def _final_gather(fs, rep):
    n = len(fs)

    def body(*refs):
        o_refs, repo_ref = refs[n + 1:2 * n + 1], refs[2 * n + 1]
        ssem, rsem = refs[2 * n + 2:]
        x, y, c, chips = _place()
        slot = 4 * x + 2 * y + c
        copies = [pltpu.make_async_remote_copy(src_ref=o_refs[a].at[c], dst_ref=o_refs[a].at[c],
                                               send_sem=ssem.at[a], recv_sem=rsem.at[a],
                                               device_id=(x, y, 1 - c), device_id_type=MESH) for a in range(n)]
        peers = [(x, y, 1 - c)] + [(*chip, c) for chip in chips] + [(*chip, 1 - c) for chip in chips]
        for k, peer in enumerate(peers):
            copies.append(pltpu.make_async_remote_copy(src_ref=repo_ref.at[slot], dst_ref=repo_ref.at[slot],
                                                       send_sem=ssem.at[n + k], recv_sem=rsem.at[n + k],
                                                       device_id=peer, device_id_type=MESH))
        for cp in copies:
            cp.start()
        for a in range(n):
            pltpu.make_async_remote_copy(src_ref=o_refs[a].at[1 - c], dst_ref=o_refs[a].at[1 - c],
                                         send_sem=ssem.at[a], recv_sem=rsem.at[a],
                                         device_id=(x, y, 1 - c), device_id_type=MESH).wait_recv()
        for k, peer in enumerate(peers):
            px, py, pc = peer
            theirs = repo_ref.at[4 * px + 2 * py + pc]
            pltpu.make_async_remote_copy(src_ref=theirs, dst_ref=theirs, send_sem=ssem.at[n + k], recv_sem=rsem.at[n + k],
                                         device_id=peer, device_id_type=MESH).wait_recv()
        for cp in copies:
            cp.wait_send()

    return pl.pallas_call(
        body, name="final_gather", in_specs=[ANY] * (n + 1), out_specs=[ANY] * (n + 1),
        out_shape=[jax.ShapeDtypeStruct(a.shape, a.dtype) for a in fs] + [jax.ShapeDtypeStruct(rep.shape, rep.dtype)],
        input_output_aliases={k: k for k in range(n + 1)},
        scratch_shapes=[pltpu.SemaphoreType.DMA((n + 7,)), pltpu.SemaphoreType.DMA((n + 7,))],
    )(*fs, rep)


def _block_diag(w, gb):
    nh, hd, _ = w.shape
    per = gb // hd
    w4 = w.reshape(nh // per, per, hd, hd)
    eye = jnp.eye(per, dtype=w.dtype)
    return jnp.einsum("jaik,ab->jaibk", w4, eye).reshape(nh // per, gb, gb)


def _diag_blocks(dense, hd):
    nj, gb, _ = dense.shape
    per = gb // hd
    d5 = dense.reshape(nj, per, hd, per, hd)
    return jnp.stack([d5[:, a, :, a, :] for a in range(per)], axis=1).reshape(nj * per, hd, hd)


def _round_up(n, q):
    return (n + q - 1) // q * q


def kernel(x, meta, norm_g, w_in, conv_a_w, conv_a_b, lru_wr, lru_br, lru_wi, lru_bi, lru_lambda, conv_b_w, w_out, final_g, loss_target, m_meta, m_norm_g, m_w_in, m_conv_a_w, m_conv_a_b, m_lru_wr, m_lru_br, m_lru_wi, m_lru_bi, m_lru_lambda, m_conv_b_w, m_w_out, m_final_g, v_meta, v_norm_g, v_w_in, v_conv_a_w, v_conv_a_b, v_lru_wr, v_lru_br, v_lru_wi, v_lru_bi, v_lru_lambda, v_conv_b_w, v_w_out, v_final_g):
    weights = dict(meta=meta, norm_g=norm_g, w_in=w_in, conv_a_w=conv_a_w, conv_a_b=conv_a_b, lru_wr=lru_wr,
                   lru_br=lru_br, lru_wi=lru_wi, lru_bi=lru_bi, lru_lambda=lru_lambda, conv_b_w=conv_b_w,
                   w_out=w_out, final_g=final_g)
    mom1 = dict(meta=m_meta, norm_g=m_norm_g, w_in=m_w_in, conv_a_w=m_conv_a_w, conv_a_b=m_conv_a_b,
                lru_wr=m_lru_wr, lru_br=m_lru_br, lru_wi=m_lru_wi, lru_bi=m_lru_bi, lru_lambda=m_lru_lambda,
                conv_b_w=m_conv_b_w, w_out=m_w_out, final_g=m_final_g)
    mom2 = dict(meta=v_meta, norm_g=v_norm_g, w_in=v_w_in, conv_a_w=v_conv_a_w, conv_a_b=v_conv_a_b,
                lru_wr=v_lru_wr, lru_br=v_lru_br, lru_wi=v_lru_wi, lru_bi=v_lru_bi, lru_lambda=v_lru_lambda,
                conv_b_w=v_conv_b_w, w_out=v_w_out, final_g=v_final_g)
    names = list(weights)

    assert x.shape[0] == 1
    seq, d = x.shape[1], x.shape[2]
    n_meta, ds = meta.shape
    depth = norm_g.shape[0]
    c = lru_lambda.shape[1]
    nh, hd = lru_wr.shape[1], lru_wr.shape[2]
    ns = w_in.shape[2]
    dms = w_out.shape[1]
    cs = conv_a_w.shape[2]
    ka, kb = conv_a_w.shape[1], conv_b_w.shape[1]
    s = N_CHIPS
    assert depth == N_CORES and d == s * ds and c == s * cs and s * ns == 6 * c and s * dms == 2 * c
    gb = min(GATE_BLOCK, c)
    t_real = n_meta + seq
    t = _round_up(t_real, ROW_QUANTUM)
    my_c = lax.axis_index("c").astype(jnp.int32)
    my_chip = (2 * lax.axis_index("x") + lax.axis_index("y")).astype(jnp.int32)
    c_idx = my_c.reshape(1)
    chip_idx = my_chip.reshape(1)

    sm_rows = _round_up(n_meta + depth * SUBLANES, 2 * SUBLANES)
    small = jnp.zeros((sm_rows, ds), F32)
    small = small.at[0:n_meta, :].set(meta)
    for l in range(depth):
        base = n_meta + l * SUBLANES
        small = small.at[base:base + ka, 0:cs].set(conv_a_w[l])
        small = small.at[base + ka:base + ka + kb, 0:cs].set(conv_b_w[l])
    (small_g,) = _gather_first([], small)
    meta_full = jnp.transpose(small_g[:, 0:n_meta, :], (1, 0, 2)).reshape(n_meta, d)
    wa_full, wb_full = [], []
    for l in range(depth):
        base = n_meta + l * SUBLANES
        wa_full.append(jnp.transpose(small_g[:, base:base + ka, 0:cs], (1, 0, 2)).reshape(ka, c))
        wb_full.append(jnp.transpose(small_g[:, base + ka:base + ka + kb, 0:cs], (1, 0, 2)).reshape(kb, c))
    win0 = _cast_place(w_in, 0, chip_idx, "cast_w_in_0").reshape(s, 2, d // 2, ns)
    ssem_w, rsem_w, win0, token_w = _copies_start([win0], _gather_plan(0), 3, "gather_win0_ici_start", after=small_g)
    win_b = [None] + [_cast_place(w_in, l, chip_idx, f"cast_w_in_{l}", after=token_w).reshape(s, 2, d // 2, ns)
                      for l in range(1, depth)]
    wout_b = [_cast_place(w_out, l, chip_idx, f"cast_w_out_{l}", after=token_w).reshape(s, 2, dms // 2, d)
              for l in range(depth)]
    h = jnp.concatenate([meta_full, x[0], jnp.zeros((t - t_real, d), F32)], axis=0) + token_w[0, 0]
    tgt = jnp.concatenate([jnp.zeros((n_meta, d), F32), loss_target[0], jnp.zeros((t - t_real, d), F32)],
                          axis=0) + token_w[0, 0]
    u_own, hn_own = _norm_in_own(h, norm_g[0].reshape(1, d), win0.reshape(s, d, ns), chip_idx, "norm_in_0_own")
    (win0,) = _copies_wait([win0], ssem_w, rsem_w, [u_own, tgt] + win_b[1:] + wout_b, _gather_plan(0),
                           "gather_win0_ici_wait")
    ssem_w, rsem_w, win0, token_w = _copies_start([win0], _gather_plan(1), 3, "gather_win0_d2d_start")
    def travel(buf, stage, tag, after):
        return _copies_start([buf], _gather_plan(stage), 3, f"gather_{tag}_{'d2d' if stage else 'ici'}_start",
                             after=after)

    def arrived(state, stage, tag, after):
        (buf,) = _copies_wait([state[2]], state[0], state[1], after, _gather_plan(stage),
                              f"gather_{tag}_{'d2d' if stage else 'ici'}_wait")
        return buf

    on_wout0 = travel(wout_b[0], 0, "wout0", token_w)
    on_win1 = travel(win_b[1], 0, "win1", on_wout0[3])
    on_wout1 = travel(wout_b[1], 0, "wout1", on_win1[3])
    token = on_wout1[3]
    (win_b[0],) = _copies_wait([win0], ssem_w, rsem_w, token, _gather_plan(1), "gather_win0_d2d_wait")

    layer_w = []
    for l in range(depth):
        layer_w.append(dict(
            g=norm_g[l].reshape(1, d), wa=wa_full[l], ba=conv_a_b[l].reshape(1, c),
            wr=_block_diag(lru_wr[l], gb).astype(BF16), br=lru_br[l].reshape(1, c),
            wi=_block_diag(lru_wi[l], gb).astype(BF16), bi=lru_bi[l].reshape(1, c),
            lam=lru_lambda[l].reshape(1, c), wb=wb_full[l]))
    saved = []
    for l, lw in enumerate(layer_w):
        first = l == 0
        lw["win"] = win_b[l].reshape(s, d, ns)
        if first:
            u = _norm_in_rest(hn_own, lw["win"], u_own, chip_idx, "norm_in_0_rest", after=token)
            hn = hn_own
            on_wout0 = travel(arrived(on_wout0, 0, "wout0", u), 1, "wout0", None)
            token = on_wout0[3]
        else:
            hn = hn_next
            u = _in_proj(hn, lw["win"], f"norm_in_{l}", after=token)
            wout_b[1] = arrived(on_wout1, 1, "wout1", u)
        y, hs = _mix_fwd(u, lw["wa"], lw["ba"] + token[0, 0] if first else lw["ba"], lw["wr"], lw["br"], lw["wi"],
                         lw["bi"], lw["lam"], lw["wb"], f"mix_fwd_{l}")
        token = None
        if first:
            wout_b[0] = arrived(on_wout0, 1, "wout0", y)
            on_win1 = travel(arrived(on_win1, 0, "win1", y), 1, "win1", None)
            token = on_win1[3]
        lw["wout"] = wout_b[l].reshape(2 * c, d)
        saved.append((h, u, hn, y, hs))
        if first:
            h, hn_next = _out_proj_norm(h, y, lw["wout"], layer_w[1]["g"], f"out_proj_{l}", after=token)
        else:
            h = _out_proj(h, y, lw["wout"], f"out_proj_{l}", after=token)
        if first:
            win_b[1] = arrived(on_win1, 1, "win1", h)
            on_wout1 = travel(arrived(on_wout1, 0, "wout1", h), 1, "wout1", None)
            token = on_wout1[3]
    dh, loss_lanes, d_final_g = _loss_head(h, tgt, final_g.reshape(1, d), n_meta, t_real, "loss_head")
    loss = lax.psum(loss_lanes[0, 0], ("x", "y", "c"))

    to_core = jnp.stack([my_chip, my_c])
    grads = [None] * depth
    early = None
    for l in reversed(range(depth)):
        lw = layer_w[l]
        h_in, u, hn, y, hs = saved[l]
        token = early[-1] if early else None
        dy = _out_proj_dy(dh, lw["wout"], f"out_proj_dy_{l}", after=token)
        d_wout = _out_proj_dw(y, dh, f"out_proj_dw_{l}")
        if early:
            ssem, rsem, bufs, _ = early
            bufs = _copies_wait(bufs, ssem, rsem, d_wout, _swap_plan, "early_swap_wait")
            half = len(bufs) // 2
            sums = [_pair_add(a, b, c_idx, f"early_pair_add_{k}") for k, (a, b) in enumerate(zip(bufs[:half], bufs[half:]))]
            lands = [lax.empty(p.shape, p.dtype) for p in sums]
            ssem, rsem, *bufs, token = _copies_start(sums + lands, _scatter_plan, 3 * half, "early_scatter_start")
        du, dsm, d_wr, d_wi = _mix_bwd(u, hs, dy, lw["wa"], lw["ba"], lw["wr"], lw["br"], lw["wi"], lw["bi"],
                                       lw["lam"], lw["wb"], f"mix_bwd_{l}", after=token)
        if early:
            bufs = _copies_wait(bufs, ssem, rsem, du, _scatter_plan, "early_scatter_wait")
            halves = [_chip_sum(rc, p, to_core, N_CORES, f"early_chip_sum_{k}")
                      for k, (p, rc) in enumerate(zip(bufs[:half], bufs[half:]))]
            ssem, rsem, *bufs, token = _copies_start(halves, _pair_gather_plan, half, "early_gather_start")
        d_win = _in_proj_dw(hn, du, s, f"in_proj_dw_{l}", after=token)
        srcs = [d_win.reshape(s, 2, d // 2, ns), d_wout.reshape(s, 2, dms // 2, d)]
        if early:
            early_full = _copies_wait(bufs, ssem, rsem, d_win, _pair_gather_plan, "early_gather_wait")
            lands = [lax.empty((a.shape[0],) + a.shape[2:], a.dtype) for a in srcs]
            ssem, rsem, *bufs, token = _copies_start(srcs + lands, _swap_plan, len(srcs), "late_swap_start")
            last = depth - 1
            early_grad = dict(w_in=early_full[0].reshape(d, ns), w_out=early_full[1].reshape(dms, d))
            early_step = {n: _adamw_layer(weights[n], early_grad[n], mom1[n], mom2[n], last, None,
                                          f"adamw_{n}_{last}", after=token) for n in ("w_in", "w_out")}
            bufs = _copies_wait(bufs, ssem, rsem, [o[0] for o in early_step.values()], _swap_plan, "late_swap_wait")
            late_sums = [_pair_add(a, b, c_idx, f"pair_add_{k}")
                         for k, (a, b) in enumerate(zip(bufs[:len(srcs)], bufs[len(srcs):]))]
            lands = [lax.empty(p.shape, p.dtype) for p in late_sums]
            ssem, rsem, *bufs, token = _copies_start(late_sums + lands, _scatter_plan, 3 * len(srcs), "late_scatter_start")
        if l > 0:
            dh, d_g = _in_proj_bwd(du, lw["win"], h_in, lw["g"], dh, f"in_proj_bwd_{l}", after=token)
        else:
            grad_x, d_meta, d_g = _in_proj_bwd(du, lw["win"], h_in, lw["g"], dh, f"in_proj_bwd_{l}", after=token,
                                               split=(n_meta, seq))
        if early:
            bufs = _copies_wait(bufs, ssem, rsem, grad_x, _scatter_plan, "late_scatter_wait")
            late_reduced = [_chip_sum(rc, p, to_core, N_CORES, f"chip_sum_{k}")
                            for k, (p, rc) in enumerate(zip(bufs[:len(srcs)], bufs[len(srcs):]))]
        grads[l] = dict(dsm=dsm, wr=_diag_blocks(d_wr, hd), wi=_diag_blocks(d_wi, hd), g=d_g)
        if l == depth - 1:
            lands = [lax.empty((a.shape[0],) + a.shape[2:], a.dtype) for a in srcs]
            ssem, rsem, *bufs, token = _copies_start(srcs + lands, _swap_plan, len(srcs), "early_swap_start")
            early = (ssem, rsem, bufs, token)
        else:
            early = None
    grad_x = grad_x[None]

    sharded = []
    sp = jnp.zeros((sm_rows, s, ds), F32)
    sp = sp.at[0:n_meta].set(d_meta.reshape(n_meta, s, ds))
    for l in range(depth):
        base = n_meta + l * SUBLANES
        dsm = grads[l]["dsm"]
        sp = sp.at[base:base + ka, :, 0:cs].set(dsm[ROW_DWA:ROW_DWA + ka].reshape(ka, s, cs))
        sp = sp.at[base + ka:base + ka + kb, :, 0:cs].set(dsm[ROW_DWB:ROW_DWB + kb].reshape(kb, s, cs))
    sharded.append(jnp.transpose(sp, (1, 0, 2)).reshape(s, 2, sm_rows // 2, ds))
    rep_parts = [jnp.concatenate([grads[l]["g"].reshape(-1) for l in range(depth)]), d_final_g.reshape(-1)]
    for row in (ROW_DBA, ROW_DBR, ROW_DBI, ROW_DLAM):
        rep_parts.append(jnp.concatenate([grads[l]["dsm"][row] for l in range(depth)]))
    rep_parts.append(jnp.concatenate([grads[l]["wr"].reshape(-1) for l in range(depth)]))
    rep_parts.append(jnp.concatenate([grads[l]["wi"].reshape(-1) for l in range(depth)]))
    rep_sizes = [p.shape[0] for p in rep_parts]
    piece = _round_up(-(-sum(rep_sizes) // (s * 2)), 2 * SUBLANES * LANES)
    flat = jnp.concatenate(rep_parts + [jnp.zeros((s * 2 * piece - sum(rep_sizes),), F32)])
    sharded.append(flat.reshape(s, 2, piece // LANES, LANES))

    from_sibling = _pair_swap(sharded, "small_pair_swap")
    pair_sums = [_pair_add(a, b, c_idx, f"small_pair_add_{k}") for k, (a, b) in enumerate(zip(sharded, from_sibling))]
    by_chip = _chip_scatter(pair_sums)
    to_device = jnp.stack([my_chip, 2 * my_chip + my_c])
    reduced_sp = _chip_sum(by_chip[0], pair_sums[0], to_core, N_CORES, "small_chip_sum")
    reduced_rep = _chip_sum(by_chip[1], pair_sums[1], to_device, N_CHIPS * N_CORES, "chip_sum_rep")
    *full, rep_all = _final_gather(late_reduced + [reduced_sp], reduced_rep)

    g_win = [full[0].reshape(d, ns), early_full[0].reshape(d, ns)]
    g_wout = [full[1].reshape(dms, d), early_full[1].reshape(dms, d)]
    g_sp = full[2].reshape(sm_rows, ds)
    rep_flat = rep_all.reshape(-1)
    rep_out, off = [], 0
    for n in rep_sizes:
        rep_out.append(rep_flat[off:off + n])
        off += n
    grad = dict(
        meta=g_sp[0:n_meta],
        norm_g=rep_out[0].reshape(depth, d),
        w_in=jnp.stack(g_win),
        conv_a_w=jnp.stack([g_sp[n_meta + l * SUBLANES:n_meta + l * SUBLANES + ka, 0:cs] for l in range(depth)]),
        conv_a_b=rep_out[2].reshape(depth, c),
        lru_wr=rep_out[6].reshape(depth, nh, hd, hd),
        lru_br=rep_out[3].reshape(depth, c),
        lru_wi=rep_out[7].reshape(depth, nh, hd, hd),
        lru_bi=rep_out[4].reshape(depth, c),
        lru_lambda=rep_out[5].reshape(depth, c),
        conv_b_w=jnp.stack([g_sp[n_meta + l * SUBLANES + ka:n_meta + l * SUBLANES + ka + kb, 0:cs]
                            for l in range(depth)]),
        w_out=jnp.stack(g_wout),
        final_g=rep_out[1].reshape(d),
    )

    delta, new_m, new_v = {}, {}, {}
    for n, g_first in (("w_in", g_win[0]), ("w_out", g_wout[0])):
        delta[n], new_m[n], new_v[n] = _adamw_layer(weights[n], g_first, mom1[n], mom2[n], 0, early_step[n],
                                                    f"adamw_{n}_0")
    for n in names:
        if n in delta:
            continue
        shape = weights[n].shape
        as_block = shape if len(shape) > 1 else (1,) + shape
        out = _adamw(weights[n].reshape(as_block), grad[n].reshape(as_block), mom1[n].reshape(as_block),
                     mom2[n].reshape(as_block), f"adamw_{n}")
        delta[n], new_m[n], new_v[n] = (o.reshape(shape) for o in out)

    return (loss, grad_x, *[grad[n] for n in names], *[delta[n] for n in names],
            *[new_m[n] for n in names], *[new_v[n] for n in names])
```

```python
import functools

import jax
import jax.numpy as jnp
from jax import lax
from jax.experimental import pallas as pl
from jax.experimental.pallas import tpu as pltpu

F32 = jnp.float32
BF16 = jnp.bfloat16

RMS_EPS = 1e-6
LRU_C = 8.0
ADAM_LR = 0.001
ADAM_B1 = 0.9
ADAM_B2 = 0.999
ADAM_EPS = 1e-08
ADAM_WD = 0.01
ADAM_STEP = 10

N_CHIPS = 4
N_CORES = 2
VMEM_LIMIT_BYTES = 56 * 1024 * 1024
SUBLANES = 8
LANES = 128
ROW_QUANTUM = 384
MIX_CHUNK = 192
SCAN_UNROLL = 4
GATE_BLOCK = 256
MESH = pl.DeviceIdType.MESH
ANY = pl.BlockSpec(memory_space=pl.ANY)

NT_DIMS = (((1,), (1,)), ((), ()))
TN_DIMS = (((0,), (0,)), ((), ()))


def _params(sem):
    return pltpu.CompilerParams(dimension_semantics=sem, vmem_limit_bytes=VMEM_LIMIT_BYTES)


def _sig(x):
    return 0.5 * jnp.tanh(0.5 * x) + 0.5


def _row_tile(t):
    return 704 if t % 704 == 0 else 192


def _col_tile(n, prefs):
    for p in prefs:
        if n % p == 0:
            return p
    return n


def _slab_rows(rows, cols):
    if rows * cols * 4 <= 1024 * 1024:
        return rows
    return _col_tile(rows, (256, 128, 64, 32, 16))


def _norm_in_own(h, g, wg, me_idx, name):
    t, d = h.shape
    s, _, ns = wg.shape
    tm = 1408 if t % 1408 == 0 else _row_tile(t)
    tn = _col_tile(ns, (768, 384, 128))
    nb = ns // tn

    def body(m_ref, h_ref, g_ref, w_ref, u_ref, hn_ref):
        @pl.when(pl.program_id(1) == 0)
        def _():
            x = h_ref[...]
            r = lax.rsqrt(jnp.mean(x * x, axis=-1, keepdims=True) + RMS_EPS)
            hn_ref[...] = ((x * r) * g_ref[...]).astype(BF16)

        u_ref[...] = jnp.dot(hn_ref[...], w_ref[...], preferred_element_type=F32)

    return pl.pallas_call(
        body, name=name,
        grid_spec=pltpu.PrefetchScalarGridSpec(
            num_scalar_prefetch=1, grid=(t // tm, nb),
            in_specs=[pl.BlockSpec((tm, d), lambda i, n, m: (i, 0)),
                      pl.BlockSpec((1, d), lambda i, n, m: (0, 0)),
                      pl.BlockSpec((None, d, tn), lambda i, n, m: (m[0], 0, n))],
            out_specs=[pl.BlockSpec((tm, tn), lambda i, n, m: (i, m[0] * nb + n)),
                       pl.BlockSpec((tm, d), lambda i, n, m: (i, 0))]),
        out_shape=[jax.ShapeDtypeStruct((t, s * ns), F32), jax.ShapeDtypeStruct((t, d), BF16)],
        compiler_params=_params(("arbitrary", "arbitrary")),
    )(me_idx, h, g, wg)


def _norm_in_rest(hn, wg, u, me_idx, name, after=None):
    t, d = hn.shape
    s, _, ns = wg.shape
    tm = 1408 if t % 1408 == 0 else _row_tile(t)
    tn = _col_tile(ns, (1536, 768, 384, 128))
    nb = ns // tn

    def body(m_ref, hn_ref, w_ref, u_in, u_ref):
        del u_in
        u_ref[...] = jnp.dot(hn_ref[...], w_ref[...], preferred_element_type=F32)

    def shard(n, m):
        return (m[0] + 1 + n // nb) % s

    body, more_specs, more = _behind(body, 4, after)
    return pl.pallas_call(
        body, name=name,
        grid_spec=pltpu.PrefetchScalarGridSpec(
            num_scalar_prefetch=1, grid=(t // tm, (s - 1) * nb),
            in_specs=[pl.BlockSpec((tm, d), lambda i, n, m: (i, 0)),
                      pl.BlockSpec((None, d, tn), lambda i, n, m: (shard(n, m), 0, n % nb)),
                      ANY] + more_specs,
            out_specs=pl.BlockSpec((tm, tn), lambda i, n, m: (i, shard(n, m) * nb + n % nb))),
        out_shape=jax.ShapeDtypeStruct(u.shape, u.dtype),
        input_output_aliases={3: 0},
        compiler_params=_params(("arbitrary", "arbitrary")),
    )(me_idx, hn, wg, u, *more)


def _decay_consts(lam):
    z = -lam
    e = jnp.exp(-jnp.abs(z))
    u = 1.0 + e
    log1p_e = jnp.where(u == 1.0, e, jnp.log(u) * (e / (u - 1.0)))
    sp = jnp.maximum(z, 0.0) + log1p_e
    return -LRU_C * sp, LRU_C * _sig(z)


def _gates(xc, wr_ref, br_ref, wi_ref, bi_ref, c8, j, gb):
    sl = slice(j * gb, (j + 1) * gb)
    x16 = xc.astype(BF16)
    r = _sig(jnp.dot(x16, wr_ref[j], preferred_element_type=F32) + br_ref[:, sl])
    ig = _sig(jnp.dot(x16, wi_ref[j], preferred_element_type=F32) + bi_ref[:, sl])
    la = c8[:, sl] * r
    a = jnp.exp(la)
    sq = jnp.sqrt(-jnp.tanh(la) * (a * a + 1.0))
    return r, ig, a, sq


def _mix_fwd(u, wa, ba, wr, br, wi, bi, lam, wb, name):
    t = u.shape[0]
    c = u.shape[1] // 6
    tc = MIX_CHUNK
    gb = wr.shape[1]
    nblk = c // gb
    ka, kb = wa.shape[0], wb.shape[0]

    def body(u_ref, wa_ref, ba_ref, wr_ref, br_ref, wi_ref, bi_ref, lam_ref, wb_ref,
             y_ref, hs_ref, xa_ext, v_ext, xc_s, a_s, b_s, carry_s):
        @pl.when(pl.program_id(0) == 0)
        def _():
            xa_ext[0:SUBLANES, :] = jnp.zeros((SUBLANES, c), F32)
            v_ext[0:SUBLANES, :] = jnp.zeros((SUBLANES, c), F32)
            carry_s[...] = jnp.zeros_like(carry_s)

        xa_ext[SUBLANES:SUBLANES + tc, :] = u_ref[:, 0:c]
        xc = ba_ref[...]
        for k in range(ka):
            xc = xc + wa_ref[pl.ds(k, 1), :] * xa_ext[pl.ds(SUBLANES - (ka - 1) + k, tc), :]
        xc_s[...] = xc
        c8, _ = _decay_consts(lam_ref[...])
        for j in range(nblk):
            sl = slice(j * gb, (j + 1) * gb)
            xcj = xc_s[:, sl]
            _, ig, a, sq = _gates(xcj, wr_ref, br_ref, wi_ref, bi_ref, c8, j, gb)
            a_s[:, sl] = a
            b_s[:, sl] = sq * (ig * xcj)

        row = lax.broadcasted_iota(jnp.int32, (SUBLANES, c), 0)

        def scan_step(j, _):
            off = pl.multiple_of(j * SUBLANES, SUBLANES)
            av = a_s[pl.ds(off, SUBLANES), :]
            bv = b_s[pl.ds(off, SUBLANES), :]
            for d in (1, 2, 4):
                keep = row >= d
                bsh = jnp.where(keep, pltpu.roll(bv, d, axis=0), 0.0)
                ash = jnp.where(keep, pltpu.roll(av, d, axis=0), 1.0)
                bv = av * bsh + bv
                av = av * ash
            hv = av * carry_s[...] + bv
            hs_ref[pl.ds(off, SUBLANES), :] = hv
            carry_s[...] = hs_ref[pl.ds(off + SUBLANES - 1, 1), :]
            return 0

        lax.fori_loop(0, tc // SUBLANES, scan_step, 0, unroll=SCAN_UNROLL)

        ga = u_ref[:, c:2 * c]
        y_ref[:, 0:c] = (hs_ref[...] * (ga * _sig(ga))).astype(BF16)

        v_ext[SUBLANES:SUBLANES + tc, :] = u_ref[:, 3 * c:4 * c] * u_ref[:, 4 * c:5 * c]
        cv = wb_ref[pl.ds(0, 1), :] * v_ext[pl.ds(SUBLANES - (kb - 1), tc), :]
        for k in range(1, kb):
            cv = cv + wb_ref[pl.ds(k, 1), :] * v_ext[pl.ds(SUBLANES - (kb - 1) + k, tc), :]
        gbv = u_ref[:, 5 * c:6 * c]
        y_ref[:, c:2 * c] = (u_ref[:, 2 * c:3 * c] * cv * (gbv * _sig(gbv))).astype(BF16)

        xa_ext[0:SUBLANES, :] = xa_ext[tc:tc + SUBLANES, :]
        v_ext[0:SUBLANES, :] = v_ext[tc:tc + SUBLANES, :]

    full = lambda shape: pl.BlockSpec(shape, lambda i: (0,) * len(shape))
    return pl.pallas_call(
        body, name=name, grid=(t // tc,),
        in_specs=[pl.BlockSpec((tc, 6 * c), lambda i: (i, 0)),
                  full(wa.shape), full(ba.shape), full(wr.shape), full(br.shape),
                  full(wi.shape), full(bi.shape), full(lam.shape), full(wb.shape)],
        out_specs=[pl.BlockSpec((tc, 2 * c), lambda i: (i, 0)),
                   pl.BlockSpec((tc, c), lambda i: (i, 0))],
        out_shape=[jax.ShapeDtypeStruct((t, 2 * c), BF16), jax.ShapeDtypeStruct((t, c), F32)],
        scratch_shapes=[pltpu.VMEM((tc + SUBLANES, c), F32), pltpu.VMEM((tc + SUBLANES, c), F32),
                        pltpu.VMEM((tc, c), F32), pltpu.VMEM((tc, c), F32), pltpu.VMEM((tc, c), F32),
                        pltpu.VMEM((1, c), F32)],
        compiler_params=_params(("arbitrary",)),
    )(u, wa, ba, wr, br, wi, bi, lam, wb)


ROW_DWA = 0
ROW_DBA = 4
ROW_DBR = 5
ROW_DBI = 6
ROW_DLAM = 7
ROW_DWB = 8
SMALL_ROWS = 16


def _mix_bwd(u, hs, dy, wa, ba, wr, br, wi, bi, lam, wb, name, after=None):
    t = u.shape[0]
    c = u.shape[1] // 6
    tc = MIX_CHUNK
    nt = t // tc
    gb = wr.shape[1]
    nblk = c // gb
    ka, kb = wa.shape[0], wb.shape[0]
    assert ka <= ROW_DBA and kb <= SMALL_ROWS - ROW_DWB
    hb = tc // SUBLANES

    def body(u_ref, uh_ref, hs_ref, hsh_ref, dy_ref, wa_ref, ba_ref, wr_ref, br_ref, wi_ref, bi_ref, lam_ref, wb_ref,
             du_ref, dsm_ref, dwr_ref, dwi_ref,
             xa_ext, v_ext, hs_ext, a_ext, ds_ext, dxc_ext, dcv_ext, xc_s, r_s, i_s, sq_s, g_s, an_s):
        i = pl.program_id(0)
        chunk = nt - 1 - i
        tail = slice(tc, tc + SUBLANES)
        head = slice(0, SUBLANES)

        @pl.when(i == 0)
        def _():
            zero = jnp.zeros((SUBLANES, c), F32)
            a_ext[tail, :] = zero
            ds_ext[tail, :] = zero
            dxc_ext[tail, :] = zero
            dcv_ext[tail, :] = zero
            dsm_ref[...] = jnp.zeros_like(dsm_ref)
            dwr_ref[...] = jnp.zeros_like(dwr_ref)
            dwi_ref[...] = jnp.zeros_like(dwi_ref)

        prev = jnp.where(chunk > 0, 1.0, 0.0)
        xa_ext[head, :] = uh_ref[:, 0:c] * prev
        xa_ext[SUBLANES:SUBLANES + tc, :] = u_ref[:, 0:c]
        v_ext[head, :] = uh_ref[:, 3 * c:4 * c] * uh_ref[:, 4 * c:5 * c] * prev
        v_ext[SUBLANES:SUBLANES + tc, :] = u_ref[:, 3 * c:4 * c] * u_ref[:, 4 * c:5 * c]
        hs_ext[head, :] = hsh_ref[...] * prev
        hs_ext[SUBLANES:SUBLANES + tc, :] = hs_ref[...]

        xc = ba_ref[...]
        for k in range(ka):
            xc = xc + wa_ref[pl.ds(k, 1), :] * xa_ext[pl.ds(SUBLANES - (ka - 1) + k, tc), :]
        xc_s[...] = xc
        c8, dc8 = _decay_consts(lam_ref[...])
        for j in range(nblk):
            sl = slice(j * gb, (j + 1) * gb)
            r, ig, a, sq = _gates(xc_s[:, sl], wr_ref, br_ref, wi_ref, bi_ref, c8, j, gb)
            r_s[:, sl] = r
            i_s[:, sl] = ig
            sq_s[:, sl] = sq
            a_ext[0:tc, sl] = a

        ga = u_ref[:, c:2 * c]
        sga = _sig(ga)
        g_s[...] = dy_ref[:, 0:c] * (ga * sga)
        an_s[...] = a_ext[pl.ds(1, tc), :]

        row = lax.broadcasted_iota(jnp.int32, (SUBLANES, c), 0)

        def scan_step(j, _):
            off = pl.multiple_of(tc - SUBLANES - j * SUBLANES, SUBLANES)
            av = an_s[pl.ds(off, SUBLANES), :]
            bv = g_s[pl.ds(off, SUBLANES), :]
            for d in (1, 2, 4):
                keep = row < SUBLANES - d
                bsh = jnp.where(keep, pltpu.roll(bv, SUBLANES - d, axis=0), 0.0)
                ash = jnp.where(keep, pltpu.roll(av, SUBLANES - d, axis=0), 1.0)
                bv = av * bsh + bv
                av = av * ash
            ds_ext[pl.ds(off, SUBLANES), :] = av * ds_ext[pl.ds(off + SUBLANES, 1), :] + bv
            return 0

        lax.fori_loop(0, tc // SUBLANES, scan_step, 0, unroll=SCAN_UNROLL)

        def acc(row_index, val):
            dsm_ref[pl.ds(row_index, 1), :] += jnp.sum(val, axis=0, keepdims=True)

        def acc_block(row_index, sl, val):
            dsm_ref[pl.ds(row_index, 1), sl] += jnp.sum(val, axis=0, keepdims=True)

        for j in range(nblk):
            sl = slice(j * gb, (j + 1) * gb)
            ds = ds_ext[0:tc, sl]
            hprev = hs_ext[pl.ds(SUBLANES - 1, tc), sl]
            a = a_ext[0:tc, sl]
            sq = sq_s[:, sl]
            ig = i_s[:, sl]
            r = r_s[:, sl]
            xcj = xc_s[:, sl]
            t1 = ds * xcj
            dla = (ds * hprev) * a - (t1 * ig) * ((a * a) / sq)
            acc_block(ROW_DLAM, sl, dla * r)
            dpr = (dla * c8[:, sl]) * (r * (1.0 - r))
            dpi = (t1 * sq) * (ig * (1.0 - ig))
            acc_block(ROW_DBR, sl, dpr)
            acc_block(ROW_DBI, sl, dpi)
            p16 = dpr.astype(BF16)
            q16 = dpi.astype(BF16)
            x16 = xcj.astype(BF16)
            dwr_ref[j] += lax.dot_general(x16, p16, TN_DIMS, preferred_element_type=F32)
            dwi_ref[j] += lax.dot_general(x16, q16, TN_DIMS, preferred_element_type=F32)
            dxc = (ds * (sq * ig)
                   + lax.dot_general(p16, wr_ref[j], NT_DIMS, preferred_element_type=F32)
                   + lax.dot_general(q16, wi_ref[j], NT_DIMS, preferred_element_type=F32))
            dxc_ext[0:tc, sl] = dxc
            acc_block(ROW_DBA, sl, dxc)

        dsilu_a = sga * (1.0 + ga * (1.0 - sga))
        du_ref[:, c:2 * c] = (dy_ref[:, 0:c] * hs_ref[...] * dsilu_a).astype(BF16)

        dxc = dxc_ext[0:tc, :]
        dxa = wa_ref[pl.ds(ka - 1, 1), :] * dxc
        acc(ROW_DWA + ka - 1, dxc * xa_ext[SUBLANES:SUBLANES + tc, :])
        for k in range(ka - 1):
            acc(ROW_DWA + k, dxc * xa_ext[pl.ds(SUBLANES - (ka - 1) + k, tc), :])
            dxa = dxa + wa_ref[pl.ds(k, 1), :] * dxc_ext[pl.ds(ka - 1 - k, tc), :]
        du_ref[:, 0:c] = dxa.astype(BF16)

        cv = wb_ref[pl.ds(0, 1), :] * v_ext[pl.ds(SUBLANES - (kb - 1), tc), :]
        for k in range(1, kb):
            cv = cv + wb_ref[pl.ds(k, 1), :] * v_ext[pl.ds(SUBLANES - (kb - 1) + k, tc), :]
        gbv = u_ref[:, 5 * c:6 * c]
        sgb = _sig(gbv)
        silu_b = gbv * sgb
        dyb = dy_ref[:, c:2 * c]
        gB = u_ref[:, 2 * c:3 * c]
        du_ref[:, 2 * c:3 * c] = (dyb * cv * silu_b).astype(BF16)
        du_ref[:, 5 * c:6 * c] = (dyb * gB * cv * (sgb * (1.0 + gbv * (1.0 - sgb)))).astype(BF16)
        dcv = dyb * gB * silu_b
        dcv_ext[0:tc, :] = dcv
        dv = wb_ref[pl.ds(kb - 1, 1), :] * dcv
        acc(ROW_DWB + kb - 1, dcv * v_ext[SUBLANES:SUBLANES + tc, :])
        for k in range(kb - 1):
            acc(ROW_DWB + k, dcv * v_ext[pl.ds(SUBLANES - (kb - 1) + k, tc), :])
            dv = dv + wb_ref[pl.ds(k, 1), :] * dcv_ext[pl.ds(kb - 1 - k, tc), :]
        du_ref[:, 3 * c:4 * c] = (dv * u_ref[:, 4 * c:5 * c]).astype(BF16)
        du_ref[:, 4 * c:5 * c] = (dv * u_ref[:, 3 * c:4 * c]).astype(BF16)

        a_ext[tail, :] = a_ext[head, :]
        ds_ext[tail, :] = ds_ext[head, :]
        dxc_ext[tail, :] = dxc_ext[head, :]
        dcv_ext[tail, :] = dcv_ext[head, :]

        @pl.when(i == nt - 1)
        def _():
            dsm_ref[pl.ds(ROW_DLAM, 1), :] = dsm_ref[pl.ds(ROW_DLAM, 1), :] * dc8

    full = lambda shape: pl.BlockSpec(shape, lambda i: (0,) * len(shape))
    rev = lambda i: (nt - 1 - i, 0)
    halo = lambda i: (jnp.maximum((nt - 1 - i) * hb - 1, 0), 0)
    ext = pltpu.VMEM((tc + SUBLANES, c), F32)
    blk = pltpu.VMEM((tc, c), F32)
    body, more_specs, more = _behind(body, 13, after)
    return pl.pallas_call(
        body, name=name, grid=(nt,),
        in_specs=[pl.BlockSpec((tc, 6 * c), rev), pl.BlockSpec((SUBLANES, 6 * c), halo),
                  pl.BlockSpec((tc, c), rev), pl.BlockSpec((SUBLANES, c), halo),
                  pl.BlockSpec((tc, 2 * c), rev),
                  full(wa.shape), full(ba.shape), full(wr.shape), full(br.shape),
                  full(wi.shape), full(bi.shape), full(lam.shape), full(wb.shape)] + more_specs,
        out_specs=[pl.BlockSpec((tc, 6 * c), rev), full((SMALL_ROWS, c)), full(wr.shape), full(wi.shape)],
        out_shape=[jax.ShapeDtypeStruct((t, 6 * c), BF16), jax.ShapeDtypeStruct((SMALL_ROWS, c), F32),
                   jax.ShapeDtypeStruct(wr.shape, F32), jax.ShapeDtypeStruct(wi.shape, F32)],
        scratch_shapes=[ext] * 7 + [blk] * 6,
        compiler_params=_params(("arbitrary",)),
    )(u, u, hs, hs, dy, wa, ba, wr, br, wi, bi, lam, wb, *more)


def _behind(body, n_in, after):
    if after is None:
        return body, [], []
    return (lambda *refs: body(*refs[:n_in], *refs[n_in + 1:])), [ANY], [after]


def _out_proj_norm(h, y, w, g_next, name, after=None):
    t, d = h.shape
    dm = y.shape[1]
    tm = _row_tile(t)

    def body(h_ref, y_ref, w_ref, g_ref, o_ref, hn_ref):
        x = h_ref[...] + jnp.dot(y_ref[...], w_ref[...], preferred_element_type=F32)
        o_ref[...] = x
        r = lax.rsqrt(jnp.mean(x * x, axis=-1, keepdims=True) + RMS_EPS)
        hn_ref[...] = ((x * r) * g_ref[...]).astype(BF16)

    body, more_specs, more = _behind(body, 4, after)
    rows = pl.BlockSpec((tm, d), lambda i: (i, 0))
    return pl.pallas_call(
        body, name=name, grid=(t // tm,),
        in_specs=[rows, pl.BlockSpec((tm, dm), lambda i: (i, 0)), pl.BlockSpec((dm, d), lambda i: (0, 0)),
                  pl.BlockSpec((1, d), lambda i: (0, 0))] + more_specs,
        out_specs=[rows, rows],
        out_shape=[jax.ShapeDtypeStruct((t, d), F32), jax.ShapeDtypeStruct((t, d), BF16)],
        compiler_params=_params(("arbitrary",)),
    )(h, y, w, g_next, *more)


def _in_proj(hn, wg, name, after=None):
    t, d = hn.shape
    s, _, ns = wg.shape
    tm = 1408 if t % 1408 == 0 else _row_tile(t)

    def body(hn_ref, w_ref, u_ref):
        u_ref[...] = jnp.dot(hn_ref[...], w_ref[...], preferred_element_type=F32)

    body, more_specs, more = _behind(body, 2, after)
    return pl.pallas_call(
        body, name=name, grid=(t // tm, s),
        in_specs=[pl.BlockSpec((tm, d), lambda i, n: (i, 0)),
                  pl.BlockSpec((None, d, ns), lambda i, n: (n, 0, 0))] + more_specs,
        out_specs=pl.BlockSpec((tm, ns), lambda i, n: (i, n)),
        out_shape=jax.ShapeDtypeStruct((t, s * ns), F32),
        compiler_params=_params(("arbitrary", "arbitrary")),
    )(hn, wg, *more)


def _out_proj_dy(dout, w, name, after=None):
    t, d = dout.shape
    dm = w.shape[0]
    tm = _row_tile(t)
    tn = _col_tile(dm, (2048, 1024, 512, 256))

    def body(g_ref, w_ref, o_ref):
        o_ref[...] = lax.dot_general(g_ref[...].astype(BF16), w_ref[...], NT_DIMS, preferred_element_type=F32)

    body, more_specs, more = _behind(body, 2, after)
    return pl.pallas_call(
        body, name=name, grid=(dm // tn, t // tm),
        in_specs=[pl.BlockSpec((tm, d), lambda n, i: (i, 0)),
                  pl.BlockSpec((tn, d), lambda n, i: (n, 0))] + more_specs,
        out_specs=pl.BlockSpec((tm, tn), lambda n, i: (i, n)),
        out_shape=jax.ShapeDtypeStruct((t, dm), F32),
        compiler_params=_params(("arbitrary", "arbitrary")),
    )(dout, w, *more)


def _out_proj_dw(y, dout, name):
    t, dm = y.shape
    d = dout.shape[1]
    tmm = _col_tile(dm, (1024, 512, 256))
    tn = _col_tile(d, (512, 256))

    def body(y_ref, g_ref, o_ref):
        o_ref[...] = lax.dot_general(y_ref[...], g_ref[...].astype(BF16), TN_DIMS, preferred_element_type=F32)

    return pl.pallas_call(
        body, name=name, grid=(d // tn, dm // tmm),
        in_specs=[pl.BlockSpec((t, tmm), lambda n, m: (0, m)),
                  pl.BlockSpec((t, tn), lambda n, m: (0, n))],
        out_specs=pl.BlockSpec((tmm, tn), lambda n, m: (m, n)),
        out_shape=jax.ShapeDtypeStruct((dm, d), F32),
        compiler_params=_params(("arbitrary", "arbitrary")),
    )(y, dout)


def _in_proj_bwd(du, wg, h, g, dout, name, after=None, split=None):
    t, d = h.shape
    s, _, ns = wg.shape
    tm = _row_tile(t)
    tn = _col_tile(d, (1024, 512, 256))

    def mm_body(du_ref, w_ref, o_ref):
        total = lax.dot_general(du_ref[:, 0:ns], w_ref[0], NT_DIMS, preferred_element_type=F32)
        for a in range(1, s):
            total = total + lax.dot_general(du_ref[:, a * ns:(a + 1) * ns], w_ref[a], NT_DIMS,
                                            preferred_element_type=F32)
        o_ref[...] = total

    mm_body, more_specs, more = _behind(mm_body, 2, after)
    dhn = pl.pallas_call(
        mm_body, name=name, grid=(t // tm, d // tn),
        in_specs=[pl.BlockSpec((tm, s * ns), lambda i, n: (i, 0)),
                  pl.BlockSpec((s, tn, ns), lambda i, n: (0, n, 0))] + more_specs,
        out_specs=pl.BlockSpec((tm, tn), lambda i, n: (i, n)),
        out_shape=jax.ShapeDtypeStruct((t, d), F32),
        compiler_params=_params(("arbitrary", "arbitrary")),
    )(du, wg, *more)

    tr = 352 if t % 352 == 0 else 192
    nt = t // tr

    def row_grad(dhn_ref, h_ref, g_ref, dout_ref, dg_ref):
        @pl.when(pl.program_id(0) == 0)
        def _():
            dg_ref[...] = jnp.zeros_like(dg_ref)

        x = h_ref[...]
        dn = dhn_ref[...]
        r = lax.rsqrt(jnp.mean(x * x, axis=-1, keepdims=True) + RMS_EPS)
        gd = dn * g_ref[...]
        dot = jnp.mean(gd * x, axis=-1, keepdims=True)
        dg_ref[...] += jnp.sum(dn * (x * r), axis=0, keepdims=True)
        return dout_ref[...] + (r * gd - x * ((r * r * r) * dot))

    rows = pl.BlockSpec((tr, d), lambda i: (i, 0))
    one = pl.BlockSpec((1, d), lambda i: (0, 0))
    if split is None:
        def norm_body(dhn_ref, h_ref, g_ref, dout_ref, dh_ref, dg_ref):
            dh_ref[...] = row_grad(dhn_ref, h_ref, g_ref, dout_ref, dg_ref)

        return pl.pallas_call(
            norm_body, name=name + "_norm", grid=(nt,),
            in_specs=[rows, rows, one, rows], out_specs=[rows, one],
            out_shape=[jax.ShapeDtypeStruct((t, d), F32), jax.ShapeDtypeStruct((1, d), F32)],
            compiler_params=_params(("arbitrary",)),
        )(dhn, h, g, dout)

    n_head, n_body = split
    n_first = tr - n_head
    n_last = n_head + n_body - (nt - 1) * tr
    assert nt >= 2 and 0 < n_head < tr and 0 < n_last <= tr and n_head % SUBLANES == 0 and n_last % SUBLANES == 0

    def split_body(dhn_ref, h_ref, g_ref, dout_ref, body_ref, head_ref, dg_ref, stage, sems):
        i = pl.program_id(0)
        slot = i % 2

        def first_copy(sl):
            return pltpu.make_async_copy(stage.at[sl, pl.ds(n_head, n_first)], body_ref.at[pl.ds(0, n_first)], sems.at[sl])

        def middle_copy(sl, step):
            start = pl.multiple_of(step * tr - n_head, SUBLANES)
            return pltpu.make_async_copy(stage.at[sl], body_ref.at[pl.ds(start, tr)], sems.at[sl])

        def last_copy(sl):
            return pltpu.make_async_copy(stage.at[sl, pl.ds(0, n_last)],
                                         body_ref.at[pl.ds((nt - 1) * tr - n_head, n_last)], sems.at[sl])

        dh = row_grad(dhn_ref, h_ref, g_ref, dout_ref, dg_ref)

        @pl.when(i == 2)
        def _():
            first_copy(0).wait()

        @pl.when(i > 2)
        def _():
            middle_copy(slot, i - 2).wait()

        stage[slot] = dh

        @pl.when(i == 0)
        def _():
            head_ref[...] = stage[0, 0:n_head, :]
            first_copy(0).start()

        @pl.when((i > 0) & (i < nt - 1))
        def _():
            middle_copy(slot, i).start()

        @pl.when(i == nt - 1)
        def _():
            last = last_copy((nt - 1) % 2)
            last.start()
            if nt == 2:
                first_copy(0).wait()
            else:
                middle_copy((nt - 2) % 2, nt - 2).wait()
            last.wait()

    return pl.pallas_call(
        split_body, name=name + "_norm", grid=(nt,),
        in_specs=[rows, rows, one, rows],
        out_specs=[ANY, pl.BlockSpec((n_head, d), lambda i: (0, 0)), one],
        out_shape=[jax.ShapeDtypeStruct((n_body, d), F32), jax.ShapeDtypeStruct((n_head, d), F32),
                   jax.ShapeDtypeStruct((1, d), F32)],
        scratch_shapes=[pltpu.VMEM((2, tr, d), F32), pltpu.SemaphoreType.DMA((2,))],
        compiler_params=_params(("arbitrary",)),
    )(dhn, h, g, dout)


def _in_proj_dw(hn, du, s, name, after=None):
    t, d = hn.shape
    ns = du.shape[1] // s
    tmm = _col_tile(d, (1024, 512, 256))
    tn = _col_tile(ns, (768, 384, 128))
    nb = ns // tn

    def body(hn_ref, du_ref, o_ref):
        o_ref[...] = lax.dot_general(hn_ref[...], du_ref[...], TN_DIMS, preferred_element_type=F32)

    body, more_specs, more = _behind(body, 2, after)
    return pl.pallas_call(
        body, name=name, grid=(s * nb, d // tmm),
        in_specs=[pl.BlockSpec((t, tmm), lambda n, m: (0, m)),
                  pl.BlockSpec((t, tn), lambda n, m: (0, n))] + more_specs,
        out_specs=pl.BlockSpec((None, tmm, tn), lambda n, m: (n // nb, m, n % nb)),
        out_shape=jax.ShapeDtypeStruct((s, d, ns), F32),
        compiler_params=_params(("arbitrary", "arbitrary")),
    )(hn, du, *more)


def _out_proj_loss(h, y, w, tgt, g, n_meta, t_real, name):
    t, d = h.shape
    dm = y.shape[1]
    tm = 352 if t % 352 == 0 else 192

    def body(h_ref, y_ref, w_ref, t_ref, g_ref, dh_ref, loss_ref, dg_ref):
        i = pl.program_id(0)

        @pl.when(i == 0)
        def _():
            loss_ref[...] = jnp.zeros_like(loss_ref)
            dg_ref[...] = jnp.zeros_like(dg_ref)

        x = h_ref[...] + jnp.dot(y_ref[...], w_ref[...], preferred_element_type=F32)
        gv = g_ref[...]
        r = lax.rsqrt(jnp.mean(x * x, axis=-1, keepdims=True) + RMS_EPS)
        xr = x * r
        rows = i * tm + lax.broadcasted_iota(jnp.int32, (tm, 1), 0)
        valid = (rows >= n_meta) & (rows < t_real)
        err = jnp.where(valid, xr * gv - t_ref[...], 0.0)
        loss_ref[...] += 0.5 * jnp.sum(jnp.mean(err * err, axis=-1, keepdims=True))
        dy = err * (1.0 / d)
        gd = dy * gv
        dot = jnp.mean(gd * x, axis=-1, keepdims=True)
        dh_ref[...] = r * gd - x * ((r * r * r) * dot)
        dg_ref[...] += jnp.sum(dy * xr, axis=0, keepdims=True)

    rows = pl.BlockSpec((tm, d), lambda i: (i, 0))
    return pl.pallas_call(
        body, name=name, grid=(t // tm,),
        in_specs=[rows, pl.BlockSpec((tm, dm), lambda i: (i, 0)), pl.BlockSpec((dm, d), lambda i: (0, 0)), rows,
                  pl.BlockSpec((1, d), lambda i: (0, 0))],
        out_specs=[rows, pl.BlockSpec((1, LANES), lambda i: (0, 0)), pl.BlockSpec((1, d), lambda i: (0, 0))],
        out_shape=[jax.ShapeDtypeStruct((t, d), F32), jax.ShapeDtypeStruct((1, LANES), F32),
                   jax.ShapeDtypeStruct((1, d), F32)],
        compiler_params=_params(("arbitrary",)),
    )(h, y, w, tgt, g)


def _adamw_rows(rows, cols):
    for cand in (512, 256, 128, 64, 32, 16, 8):
        if rows % cand == 0 and cand * cols * 4 <= 2 * 1024 * 1024:
            return cand
    return rows


def _adamw_math(w_ref, g_ref, m_ref, v_ref, d_ref, nm_ref, nv_ref):
    gv = g_ref[...]
    m2 = ADAM_B1 * m_ref[...] + (1.0 - ADAM_B1) * gv
    v2 = ADAM_B2 * v_ref[...] + (1.0 - ADAM_B2) * (gv * gv)
    m_hat = m2 / (1.0 - ADAM_B1 ** ADAM_STEP)
    v_hat = v2 / (1.0 - ADAM_B2 ** ADAM_STEP)
    d_ref[...] = -ADAM_LR * (m_hat / (jnp.sqrt(v_hat) + ADAM_EPS) + ADAM_WD * w_ref[...])
    nm_ref[...] = m2
    nv_ref[...] = v2


def _adamw(w, g, m, v, name):
    shape = w.shape
    assert len(shape) >= 2 and w.size * 4 <= 2 * 1024 * 1024

    def body(*refs):
        _adamw_math(*refs)

    spec = pl.BlockSpec(shape, lambda i: (0,) * len(shape))
    return pl.pallas_call(
        body, name=name, grid=(1,),
        in_specs=[spec] * 4, out_specs=[spec] * 3,
        out_shape=[jax.ShapeDtypeStruct(shape, F32)] * 3,
        compiler_params=_params(("arbitrary",)),
    )(w, g, m, v)


def _adamw_layer(w, g, m, v, layer, kept, name, after=None):
    nl, rows, cols = w.shape
    tr = _adamw_rows(rows, cols)
    n_kept = 0 if kept is None else 3

    def body(*refs):
        _adamw_math(*refs[:4], *refs[4 + n_kept:])

    body, more_specs, more = _behind(body, 4 + n_kept, after)
    lay = pl.BlockSpec((None, tr, cols), lambda i: (layer, i, 0))
    return pl.pallas_call(
        body, name=name, grid=(rows // tr,),
        in_specs=[lay, pl.BlockSpec((tr, cols), lambda i: (i, 0)), lay, lay] + [ANY] * n_kept + more_specs,
        out_specs=[lay] * 3,
        out_shape=[jax.ShapeDtypeStruct((nl, rows, cols), F32)] * 3,
        input_output_aliases={4 + k: k for k in range(n_kept)},
        compiler_params=_params(("arbitrary",)),
    )(w, g, m, v, *([] if kept is None else kept), *more)


def _pair_add(x, ra, c_idx, name):
    s, _, rows, cols = x.shape
    tr = _slab_rows(rows, cols)

    def body(c_ref, x_ref, r_ref, o_ref):
        o_ref[...] = (x_ref[...] + r_ref[...]).astype(BF16)

    return pl.pallas_call(
        body, name=name,
        grid_spec=pltpu.PrefetchScalarGridSpec(
            num_scalar_prefetch=1, grid=(s, rows // tr),
            in_specs=[pl.BlockSpec((None, None, tr, cols), lambda a, i, c_ref: (a, c_ref[0], i, 0)),
                      pl.BlockSpec((None, tr, cols), lambda a, i, c_ref: (a, i, 0))],
            out_specs=pl.BlockSpec((None, tr, cols), lambda a, i, c_ref: (a, i, 0))),
        out_shape=jax.ShapeDtypeStruct((s, rows, cols), BF16),
        compiler_params=_params(("arbitrary", "arbitrary")),
    )(c_idx, x, ra)


def _chip_sum(rc, p, where, n_slots, name):
    s, rows, cols = rc.shape
    tr = _slab_rows(rows, cols)

    def body(w_ref, x_ref, p_ref, o_ref):
        me = w_ref[0]
        total = jnp.where(me == 0, p_ref[...], x_ref[0]).astype(F32)
        for a in range(1, s):
            total = total + jnp.where(me == a, p_ref[...], x_ref[a]).astype(F32)
        o_ref[...] = total

    return pl.pallas_call(
        body, name=name,
        grid_spec=pltpu.PrefetchScalarGridSpec(
            num_scalar_prefetch=1, grid=(rows // tr,),
            in_specs=[pl.BlockSpec((s, tr, cols), lambda i, w_ref: (0, i, 0)),
                      pl.BlockSpec((None, tr, cols), lambda i, w_ref: (w_ref[0], i, 0))],
            out_specs=pl.BlockSpec((None, tr, cols), lambda i, w_ref: (w_ref[1], i, 0))),
        out_shape=jax.ShapeDtypeStruct((n_slots, rows, cols), F32),
        compiler_params=_params(("arbitrary",)),
    )(where, rc, p)


def _cast_place(w, layer, me_idx, name, after=None):
    _, rows, cols = w.shape
    tr = _slab_rows(rows, cols)

    def body(m_ref, w_ref, o_ref):
        o_ref[...] = w_ref[...].astype(BF16)

    body, more_specs, more = _behind(body, 2, after)
    return pl.pallas_call(
        body, name=name,
        grid_spec=pltpu.PrefetchScalarGridSpec(
            num_scalar_prefetch=1, grid=(rows // tr,),
            in_specs=[pl.BlockSpec((None, tr, cols), lambda i, m_ref: (layer, i, 0))] + more_specs,
            out_specs=pl.BlockSpec((None, tr, cols), lambda i, m_ref: (m_ref[0], i, 0))),
        out_shape=jax.ShapeDtypeStruct((N_CHIPS, rows, cols), BF16),
        compiler_params=_params(("arbitrary",)),
    )(me_idx, w, *more)


def _place():
    x, y, c = lax.axis_index("x"), lax.axis_index("y"), lax.axis_index("c")
    chips = [(1 - x, y), (x, 1 - y), (1 - x, 1 - y)]
    return x, y, c, chips


def _chip_index(cx, cy):
    return 2 * cx + cy


def _gather_copies(bufs, stage):
    x, y, c, chips = _place()
    me = _chip_index(x, y)
    copies = []
    for b in bufs:
        for chip in chips:
            src = _chip_index(*chip)
            if stage == 0:
                copies.append((b.at[me, c], (*chip, c), b.at[src, c]))
            else:
                copies.append((b.at[src, c], (x, y, 1 - c), b.at[src, 1 - c]))
    return copies


def _remote(ref, peer, ssem, rsem, k):
    return pltpu.make_async_remote_copy(src_ref=ref, dst_ref=ref, send_sem=ssem.at[k], recv_sem=rsem.at[k],
                                        device_id=peer, device_id_type=MESH)


def _gather_first(bufs, small):
    n = len(bufs)
    k = 3 * n

    def body(*refs):
        sm_ref = refs[n]
        b_refs, smg_ref = refs[n + 1:2 * n + 1], refs[2 * n + 1]
        lsem, ssem, rsem = refs[2 * n + 2:]
        x, y, c, chips = _place()
        me = _chip_index(x, y)
        local = pltpu.make_async_copy(sm_ref, smg_ref.at[me], lsem)
        local.start()
        first = _gather_copies(b_refs, 0)
        second = _gather_copies(b_refs, 1)
        started = []
        for i, (ref, peer, _) in enumerate(first):
            started.append(_remote(ref, peer, ssem, rsem, i))
        for j, chip in enumerate(chips):
            started.append(pltpu.make_async_remote_copy(
                src_ref=sm_ref, dst_ref=smg_ref.at[me], send_sem=ssem.at[2 * k + j], recv_sem=rsem.at[2 * k + j],
                device_id=(*chip, c), device_id_type=MESH))
        for cp in started:
            cp.start()
        for i, (_, peer, lands) in enumerate(first):
            _remote(lands, peer, ssem, rsem, i).wait_recv()
            ref, sib, _ = second[i]
            fwd = _remote(ref, sib, ssem, rsem, k + i)
            fwd.start()
            started.append(fwd)
        for i, (_, sib, lands) in enumerate(second):
            _remote(lands, sib, ssem, rsem, k + i).wait_recv()
        for j, chip in enumerate(chips):
            theirs = smg_ref.at[_chip_index(*chip)]
            pltpu.make_async_remote_copy(src_ref=theirs, dst_ref=theirs, send_sem=ssem.at[2 * k + j],
                                         recv_sem=rsem.at[2 * k + j], device_id=(*chip, c),
                                         device_id_type=MESH).wait_recv()
        for cp in started:
            cp.wait_send()
        local.wait()

    return pl.pallas_call(
        body, name="gather_first",
        in_specs=[ANY] * (n + 1), out_specs=[ANY] * (n + 1),
        out_shape=[jax.ShapeDtypeStruct(b.shape, b.dtype) for b in bufs]
        + [jax.ShapeDtypeStruct((N_CHIPS,) + small.shape, small.dtype)],
        input_output_aliases={i: i for i in range(n)},
        scratch_shapes=[pltpu.SemaphoreType.DMA, pltpu.SemaphoreType.DMA((2 * k + 3,)),
                        pltpu.SemaphoreType.DMA((2 * k + 3,))],
    )(*bufs, small)


HBM = pl.BlockSpec(memory_space=pltpu.HBM)
SEM = pl.BlockSpec(memory_space=pltpu.SEMAPHORE)
DATAFLOW = pltpu.SideEffectType.DATAFLOW_SIDE_EFFECTING


def _copies_start(bufs, plan, n_copies, name, after=None):
    n = len(bufs)
    extra = [] if after is None else [after]

    def body(*refs):
        refs = refs[:n] + refs[n + len(extra):]
        ssem, rsem = refs[n], refs[n + 1]
        b_refs, token = refs[n + 2:2 * n + 2], refs[2 * n + 2]
        copies = plan(b_refs)
        assert len(copies) == n_copies
        for i, (src, dst, peer, _) in enumerate(copies):
            pltpu.make_async_remote_copy(src_ref=src, dst_ref=dst, send_sem=ssem.at[i], recv_sem=rsem.at[i],
                                         device_id=peer, device_id_type=MESH).start()
        token[...] = jnp.zeros_like(token)

    return pl.pallas_call(
        body, name=name,
        out_shape=(pltpu.SemaphoreType.DMA((n_copies,)), pltpu.SemaphoreType.DMA((n_copies,)),
                   *[pltpu.HBM(b.shape, b.dtype) for b in bufs], jax.ShapeDtypeStruct((SUBLANES, LANES), F32)),
        in_specs=[HBM] * n + [ANY] * len(extra),
        out_specs=(SEM, SEM, *[HBM] * n, pl.BlockSpec(memory_space=pltpu.VMEM)),
        input_output_aliases={i: 2 + i for i in range(n)},
        compiler_params=pltpu.CompilerParams(has_side_effects=DATAFLOW),
    )(*[pltpu.with_memory_space_constraint(b, pltpu.HBM) for b in bufs], *extra)


def _copies_wait(bufs, ssem, rsem, after, plan, name):
    n = len(bufs)
    afters = list(after) if isinstance(after, (list, tuple)) else [after]

    def body(*refs):
        b_refs, ssem_ref, rsem_ref = refs[:n], refs[n], refs[n + 1]
        for i, (src, dst, peer, lands) in enumerate(plan(b_refs)):
            pltpu.make_async_remote_copy(src_ref=src, dst_ref=dst, send_sem=ssem_ref.at[i], recv_sem=rsem_ref.at[i],
                                         device_id=peer, device_id_type=MESH).wait_send()
            pltpu.make_async_remote_copy(src_ref=lands, dst_ref=lands, send_sem=ssem_ref.at[i],
                                         recv_sem=rsem_ref.at[i], device_id=peer, device_id_type=MESH).wait_recv()

    return pl.pallas_call(
        body, name=name,
        out_shape=tuple(pltpu.HBM(b.shape, b.dtype) for b in bufs),
        in_specs=[HBM] * n + [SEM, SEM] + [ANY] * len(afters), out_specs=tuple([HBM] * n),
        input_output_aliases={i: i for i in range(n)},
        compiler_params=pltpu.CompilerParams(has_side_effects=DATAFLOW),
    )(*bufs, ssem, rsem, *afters)


def _gather_plan(stage):
    return lambda refs: [(ref, ref, peer, lands) for ref, peer, lands in _gather_copies(refs, stage)]


def _swap_plan(refs):
    n = len(refs) // 2
    x, y, c, _ = _place()
    return [(refs[a].at[:, 1 - c], refs[n + a], (x, y, 1 - c), refs[n + a]) for a in range(n)]


def _scatter_plan(refs):
    n = len(refs) // 2
    x, y, c, chips = _place()
    me = _chip_index(x, y)
    return [(refs[a].at[_chip_index(*chip)], refs[n + a].at[me], (*chip, c), refs[n + a].at[_chip_index(*chip)])
            for a in range(n) for chip in chips]


def _pair_gather_plan(refs):
    x, y, c, _ = _place()
    return [(r.at[c], r.at[c], (x, y, 1 - c), r.at[1 - c]) for r in refs]


def _pair_swap(xs, name):
    n = len(xs)

    def body(*refs):
        x_refs, o_refs, ssem, rsem = refs[:n], refs[n:2 * n], refs[2 * n], refs[2 * n + 1]
        x, y, c, _ = _place()
        copies = [pltpu.make_async_remote_copy(src_ref=x_refs[a].at[:, 1 - c], dst_ref=o_refs[a],
                                               send_sem=ssem.at[a], recv_sem=rsem.at[a],
                                               device_id=(x, y, 1 - c), device_id_type=MESH) for a in range(n)]
        for cp in copies:
            cp.start()
        for cp in copies:
            cp.wait()

    return pl.pallas_call(
        body, name=name, in_specs=[ANY] * n, out_specs=[ANY] * n,
        out_shape=[jax.ShapeDtypeStruct((a.shape[0],) + a.shape[2:], a.dtype) for a in xs],
        scratch_shapes=[pltpu.SemaphoreType.DMA((n,)), pltpu.SemaphoreType.DMA((n,))],
    )(*xs)


def _chip_scatter(ps):
    n = len(ps)

    def body(*refs):
        p_refs, o_refs, ssem, rsem = refs[:n], refs[n:2 * n], refs[2 * n], refs[2 * n + 1]
        x, y, c, chips = _place()
        me = _chip_index(x, y)
        sends = []
        for a in range(n):
            for j, chip in enumerate(chips):
                sends.append(pltpu.make_async_remote_copy(
                    src_ref=p_refs[a].at[_chip_index(*chip)], dst_ref=o_refs[a].at[me],
                    send_sem=ssem.at[3 * a + j], recv_sem=rsem.at[3 * a + j],
                    device_id=(*chip, c), device_id_type=MESH))
        for cp in sends:
            cp.start()
        for a in range(n):
            for j, chip in enumerate(chips):
                src = _chip_index(*chip)
                pltpu.make_async_remote_copy(
                    src_ref=p_refs[a].at[src], dst_ref=o_refs[a].at[src],
                    send_sem=ssem.at[3 * a + j], recv_sem=rsem.at[3 * a + j],
                    device_id=(*chip, c), device_id_type=MESH).wait_recv()
        for cp in sends:
            cp.wait_send()

    return pl.pallas_call(
        body, name="chip_scatter", in_specs=[ANY] * n, out_specs=[ANY] * n,
        out_shape=[jax.ShapeDtypeStruct(a.shape, a.dtype) for a in ps],
        scratch_shapes=[pltpu.SemaphoreType.DMA((3 * n,)), pltpu.SemaphoreType.DMA((3 * n,))],
    )(*ps)


def _final_gather(fs, rep):
    n = len(fs)

    def body(*refs):
        o_refs, repo_ref = refs[n + 1:2 * n + 1], refs[2 * n + 1]
        ssem, rsem = refs[2 * n + 2:]
        x, y, c, chips = _place()
        slot = 4 * x + 2 * y + c
        copies = [pltpu.make_async_remote_copy(src_ref=o_refs[a].at[c], dst_ref=o_refs[a].at[c],
                                               send_sem=ssem.at[a], recv_sem=rsem.at[a],
                                               device_id=(x, y, 1 - c), device_id_type=MESH) for a in range(n)]
        peers = [(x, y, 1 - c)] + [(*chip, c) for chip in chips] + [(*chip, 1 - c) for chip in chips]
        for k, peer in enumerate(peers):
            copies.append(pltpu.make_async_remote_copy(src_ref=repo_ref.at[slot], dst_ref=repo_ref.at[slot],
                                                       send_sem=ssem.at[n + k], recv_sem=rsem.at[n + k],
                                                       device_id=peer, device_id_type=MESH))
        for cp in copies:
            cp.start()
        for a in range(n):
            pltpu.make_async_remote_copy(src_ref=o_refs[a].at[1 - c], dst_ref=o_refs[a].at[1 - c],
                                         send_sem=ssem.at[a], recv_sem=rsem.at[a],
                                         device_id=(x, y, 1 - c), device_id_type=MESH).wait_recv()
        for k, peer in enumerate(peers):
            px, py, pc = peer
            theirs = repo_ref.at[4 * px + 2 * py + pc]
            pltpu.make_async_remote_copy(src_ref=theirs, dst_ref=theirs, send_sem=ssem.at[n + k], recv_sem=rsem.at[n + k],
                                         device_id=peer, device_id_type=MESH).wait_recv()
        for cp in copies:
            cp.wait_send()

    return pl.pallas_call(
        body, name="final_gather", in_specs=[ANY] * (n + 1), out_specs=[ANY] * (n + 1),
        out_shape=[jax.ShapeDtypeStruct(a.shape, a.dtype) for a in fs] + [jax.ShapeDtypeStruct(rep.shape, rep.dtype)],
        input_output_aliases={k: k for k in range(n + 1)},
        scratch_shapes=[pltpu.SemaphoreType.DMA((n + 7,)), pltpu.SemaphoreType.DMA((n + 7,))],
    )(*fs, rep)


def _block_diag(w, gb):
    nh, hd, _ = w.shape
    per = gb // hd
    w4 = w.reshape(nh // per, per, hd, hd)
    eye = jnp.eye(per, dtype=w.dtype)
    return jnp.einsum("jaik,ab->jaibk", w4, eye).reshape(nh // per, gb, gb)


def _diag_blocks(dense, hd):
    nj, gb, _ = dense.shape
    per = gb // hd
    d5 = dense.reshape(nj, per, hd, per, hd)
    return jnp.stack([d5[:, a, :, a, :] for a in range(per)], axis=1).reshape(nj * per, hd, hd)


def _round_up(n, q):
    return (n + q - 1) // q * q


def kernel(x, meta, norm_g, w_in, conv_a_w, conv_a_b, lru_wr, lru_br, lru_wi, lru_bi, lru_lambda, conv_b_w, w_out, final_g, loss_target, m_meta, m_norm_g, m_w_in, m_conv_a_w, m_conv_a_b, m_lru_wr, m_lru_br, m_lru_wi, m_lru_bi, m_lru_lambda, m_conv_b_w, m_w_out, m_final_g, v_meta, v_norm_g, v_w_in, v_conv_a_w, v_conv_a_b, v_lru_wr, v_lru_br, v_lru_wi, v_lru_bi, v_lru_lambda, v_conv_b_w, v_w_out, v_final_g):
    weights = dict(meta=meta, norm_g=norm_g, w_in=w_in, conv_a_w=conv_a_w, conv_a_b=conv_a_b, lru_wr=lru_wr,
                   lru_br=lru_br, lru_wi=lru_wi, lru_bi=lru_bi, lru_lambda=lru_lambda, conv_b_w=conv_b_w,
                   w_out=w_out, final_g=final_g)
    mom1 = dict(meta=m_meta, norm_g=m_norm_g, w_in=m_w_in, conv_a_w=m_conv_a_w, conv_a_b=m_conv_a_b,
                lru_wr=m_lru_wr, lru_br=m_lru_br, lru_wi=m_lru_wi, lru_bi=m_lru_bi, lru_lambda=m_lru_lambda,
                conv_b_w=m_conv_b_w, w_out=m_w_out, final_g=m_final_g)
    mom2 = dict(meta=v_meta, norm_g=v_norm_g, w_in=v_w_in, conv_a_w=v_conv_a_w, conv_a_b=v_conv_a_b,
                lru_wr=v_lru_wr, lru_br=v_lru_br, lru_wi=v_lru_wi, lru_bi=v_lru_bi, lru_lambda=v_lru_lambda,
                conv_b_w=v_conv_b_w, w_out=v_w_out, final_g=v_final_g)
    names = list(weights)

    assert x.shape[0] == 1
    seq, d = x.shape[1], x.shape[2]
    n_meta, ds = meta.shape
    depth = norm_g.shape[0]
    c = lru_lambda.shape[1]
    nh, hd = lru_wr.shape[1], lru_wr.shape[2]
    ns = w_in.shape[2]
    dms = w_out.shape[1]
    cs = conv_a_w.shape[2]
    ka, kb = conv_a_w.shape[1], conv_b_w.shape[1]
    s = N_CHIPS
    assert depth == N_CORES and d == s * ds and c == s * cs and s * ns == 6 * c and s * dms == 2 * c
    gb = min(GATE_BLOCK, c)
    t_real = n_meta + seq
    t = _round_up(t_real, ROW_QUANTUM)
    my_c = lax.axis_index("c").astype(jnp.int32)
    my_chip = (2 * lax.axis_index("x") + lax.axis_index("y")).astype(jnp.int32)
    c_idx = my_c.reshape(1)
    chip_idx = my_chip.reshape(1)

    sm_rows = _round_up(n_meta + depth * SUBLANES, 2 * SUBLANES)
    small = jnp.zeros((sm_rows, ds), F32)
    small = small.at[0:n_meta, :].set(meta)
    for l in range(depth):
        base = n_meta + l * SUBLANES
        small = small.at[base:base + ka, 0:cs].set(conv_a_w[l])
        small = small.at[base + ka:base + ka + kb, 0:cs].set(conv_b_w[l])
    (small_g,) = _gather_first([], small)
    meta_full = jnp.transpose(small_g[:, 0:n_meta, :], (1, 0, 2)).reshape(n_meta, d)
    wa_full, wb_full = [], []
    for l in range(depth):
        base = n_meta + l * SUBLANES
        wa_full.append(jnp.transpose(small_g[:, base:base + ka, 0:cs], (1, 0, 2)).reshape(ka, c))
        wb_full.append(jnp.transpose(small_g[:, base + ka:base + ka + kb, 0:cs], (1, 0, 2)).reshape(kb, c))
    win0 = _cast_place(w_in, 0, chip_idx, "cast_w_in_0").reshape(s, 2, d // 2, ns)
    ssem_w, rsem_w, win0, token_w = _copies_start([win0], _gather_plan(0), 3, "gather_win0_ici_start", after=small_g)
    win_b = [None] + [_cast_place(w_in, l, chip_idx, f"cast_w_in_{l}", after=token_w).reshape(s, 2, d // 2, ns)
                      for l in range(1, depth)]
    wout_b = [_cast_place(w_out, l, chip_idx, f"cast_w_out_{l}", after=token_w).reshape(s, 2, dms // 2, d)
              for l in range(depth)]
    h = jnp.concatenate([meta_full, x[0], jnp.zeros((t - t_real, d), F32)], axis=0) + token_w[0, 0]
    tgt = jnp.concatenate([jnp.zeros((n_meta, d), F32), loss_target[0], jnp.zeros((t - t_real, d), F32)],
                          axis=0) + token_w[0, 0]
    u_own, hn_own = _norm_in_own(h, norm_g[0].reshape(1, d), win0.reshape(s, d, ns), chip_idx, "norm_in_0_own")
    (win0,) = _copies_wait([win0], ssem_w, rsem_w, [u_own, tgt] + win_b[1:] + wout_b, _gather_plan(0),
                           "gather_win0_ici_wait")
    ssem_w, rsem_w, win0, token_w = _copies_start([win0], _gather_plan(1), 3, "gather_win0_d2d_start")
    def travel(buf, stage, tag, after):
        return _copies_start([buf], _gather_plan(stage), 3, f"gather_{tag}_{'d2d' if stage else 'ici'}_start",
                             after=after)

    def arrived(state, stage, tag, after):
        (buf,) = _copies_wait([state[2]], state[0], state[1], after, _gather_plan(stage),
                              f"gather_{tag}_{'d2d' if stage else 'ici'}_wait")
        return buf

    on_wout0 = travel(wout_b[0], 0, "wout0", token_w)
    on_win1 = travel(win_b[1], 0, "win1", on_wout0[3])
    on_wout1 = travel(wout_b[1], 0, "wout1", on_win1[3])
    token = on_wout1[3]
    (win_b[0],) = _copies_wait([win0], ssem_w, rsem_w, token, _gather_plan(1), "gather_win0_d2d_wait")

    layer_w = []
    for l in range(depth):
        layer_w.append(dict(
            g=norm_g[l].reshape(1, d), wa=wa_full[l], ba=conv_a_b[l].reshape(1, c),
            wr=_block_diag(lru_wr[l], gb).astype(BF16), br=lru_br[l].reshape(1, c),
            wi=_block_diag(lru_wi[l], gb).astype(BF16), bi=lru_bi[l].reshape(1, c),
            lam=lru_lambda[l].reshape(1, c), wb=wb_full[l]))
    saved = []
    for l, lw in enumerate(layer_w):
        first = l == 0
        lw["win"] = win_b[l].reshape(s, d, ns)
        if first:
            u = _norm_in_rest(hn_own, lw["win"], u_own, chip_idx, "norm_in_0_rest", after=token)
            hn = hn_own
            on_wout0 = travel(arrived(on_wout0, 0, "wout0", u), 1, "wout0", None)
            token = on_wout0[3]
        else:
            hn = hn_next
            u = _in_proj(hn, lw["win"], f"norm_in_{l}", after=token)
            wout_b[1] = arrived(on_wout1, 1, "wout1", u)
        y, hs = _mix_fwd(u, lw["wa"], lw["ba"] + token[0, 0] if first else lw["ba"], lw["wr"], lw["br"], lw["wi"],
                         lw["bi"], lw["lam"], lw["wb"], f"mix_fwd_{l}")
        token = None
        if first:
            wout_b[0] = arrived(on_wout0, 1, "wout0", y)
            on_win1 = travel(arrived(on_win1, 0, "win1", y), 1, "win1", None)
            token = on_win1[3]
        lw["wout"] = wout_b[l].reshape(2 * c, d)
        saved.append((h, u, hn, y, hs))
        if first:
            h, hn_next = _out_proj_norm(h, y, lw["wout"], layer_w[1]["g"], f"out_proj_{l}", after=token)
        else:
            dh, loss_lanes, d_final_g = _out_proj_loss(h, y, lw["wout"], tgt, final_g.reshape(1, d), n_meta, t_real,
                                                       f"out_proj_{l}_loss")
        if first:
            win_b[1] = arrived(on_win1, 1, "win1", h)
            on_wout1 = travel(arrived(on_wout1, 0, "wout1", h), 1, "wout1", None)
            token = on_wout1[3]
    loss = lax.psum(loss_lanes[0, 0], ("x", "y", "c"))

    to_core = jnp.stack([my_chip, my_c])
    grads = [None] * depth
    early = None
    for l in reversed(range(depth)):
        lw = layer_w[l]
        h_in, u, hn, y, hs = saved[l]
        token = early[-1] if early else None
        dy = _out_proj_dy(dh, lw["wout"], f"out_proj_dy_{l}", after=token)
        d_wout = _out_proj_dw(y, dh, f"out_proj_dw_{l}")
        if early:
            ssem, rsem, bufs, _ = early
            bufs = _copies_wait(bufs, ssem, rsem, d_wout, _swap_plan, "early_swap_wait")
            half = len(bufs) // 2
            sums = [_pair_add(a, b, c_idx, f"early_pair_add_{k}") for k, (a, b) in enumerate(zip(bufs[:half], bufs[half:]))]
            lands = [lax.empty(p.shape, p.dtype) for p in sums]
            ssem, rsem, *bufs, token = _copies_start(sums + lands, _scatter_plan, 3 * half, "early_scatter_start")
        du, dsm, d_wr, d_wi = _mix_bwd(u, hs, dy, lw["wa"], lw["ba"], lw["wr"], lw["br"], lw["wi"], lw["bi"],
                                       lw["lam"], lw["wb"], f"mix_bwd_{l}", after=token)
        if early:
            bufs = _copies_wait(bufs, ssem, rsem, du, _scatter_plan, "early_scatter_wait")
            halves = [_chip_sum(rc, p, to_core, N_CORES, f"early_chip_sum_{k}")
                      for k, (p, rc) in enumerate(zip(bufs[:half], bufs[half:]))]
            ssem, rsem, *bufs, token = _copies_start(halves, _pair_gather_plan, half, "early_gather_start")
        d_win = _in_proj_dw(hn, du, s, f"in_proj_dw_{l}", after=token)
        srcs = [d_win.reshape(s, 2, d // 2, ns), d_wout.reshape(s, 2, dms // 2, d)]
        if early:
            early_full = _copies_wait(bufs, ssem, rsem, d_win, _pair_gather_plan, "early_gather_wait")
            lands = [lax.empty((a.shape[0],) + a.shape[2:], a.dtype) for a in srcs]
            ssem, rsem, *bufs, token = _copies_start(srcs + lands, _swap_plan, len(srcs), "late_swap_start")
            last = depth - 1
            early_grad = dict(w_in=early_full[0].reshape(d, ns), w_out=early_full[1].reshape(dms, d))
            early_step = {n: _adamw_layer(weights[n], early_grad[n], mom1[n], mom2[n], last, None,
                                          f"adamw_{n}_{last}", after=token) for n in ("w_in", "w_out")}
            bufs = _copies_wait(bufs, ssem, rsem, [o[0] for o in early_step.values()], _swap_plan, "late_swap_wait")
            late_sums = [_pair_add(a, b, c_idx, f"pair_add_{k}")
                         for k, (a, b) in enumerate(zip(bufs[:len(srcs)], bufs[len(srcs):]))]
            lands = [lax.empty(p.shape, p.dtype) for p in late_sums]
            ssem, rsem, *bufs, token = _copies_start(late_sums + lands, _scatter_plan, 3 * len(srcs), "late_scatter_start")
        if l > 0:
            dh, d_g = _in_proj_bwd(du, lw["win"], h_in, lw["g"], dh, f"in_proj_bwd_{l}", after=token)
        else:
            grad_x, d_meta, d_g = _in_proj_bwd(du, lw["win"], h_in, lw["g"], dh, f"in_proj_bwd_{l}", after=token,
                                               split=(n_meta, seq))
        if early:
            bufs = _copies_wait(bufs, ssem, rsem, grad_x, _scatter_plan, "late_scatter_wait")
            late_reduced = [_chip_sum(rc, p, to_core, N_CORES, f"chip_sum_{k}")
                            for k, (p, rc) in enumerate(zip(bufs[:len(srcs)], bufs[len(srcs):]))]
        grads[l] = dict(dsm=dsm, wr=_diag_blocks(d_wr, hd), wi=_diag_blocks(d_wi, hd), g=d_g)
        if l == depth - 1:
            lands = [lax.empty((a.shape[0],) + a.shape[2:], a.dtype) for a in srcs]
            ssem, rsem, *bufs, token = _copies_start(srcs + lands, _swap_plan, len(srcs), "early_swap_start")
            early = (ssem, rsem, bufs, token)
        else:
            early = None
    grad_x = grad_x[None]

    sharded = []
    sp = jnp.zeros((sm_rows, s, ds), F32)
    sp = sp.at[0:n_meta].set(d_meta.reshape(n_meta, s, ds))
    for l in range(depth):
        base = n_meta + l * SUBLANES
        dsm = grads[l]["dsm"]
        sp = sp.at[base:base + ka, :, 0:cs].set(dsm[ROW_DWA:ROW_DWA + ka].reshape(ka, s, cs))
        sp = sp.at[base + ka:base + ka + kb, :, 0:cs].set(dsm[ROW_DWB:ROW_DWB + kb].reshape(kb, s, cs))
    sharded.append(jnp.transpose(sp, (1, 0, 2)).reshape(s, 2, sm_rows // 2, ds))
    rep_parts = [jnp.concatenate([grads[l]["g"].reshape(-1) for l in range(depth)]), d_final_g.reshape(-1)]
    for row in (ROW_DBA, ROW_DBR, ROW_DBI, ROW_DLAM):
        rep_parts.append(jnp.concatenate([grads[l]["dsm"][row] for l in range(depth)]))
    rep_parts.append(jnp.concatenate([grads[l]["wr"].reshape(-1) for l in range(depth)]))
    rep_parts.append(jnp.concatenate([grads[l]["wi"].reshape(-1) for l in range(depth)]))
    rep_sizes = [p.shape[0] for p in rep_parts]
    piece = _round_up(-(-sum(rep_sizes) // (s * 2)), 2 * SUBLANES * LANES)
    flat = jnp.concatenate(rep_parts + [jnp.zeros((s * 2 * piece - sum(rep_sizes),), F32)])
    sharded.append(flat.reshape(s, 2, piece // LANES, LANES))

    from_sibling = _pair_swap(sharded, "small_pair_swap")
    pair_sums = [_pair_add(a, b, c_idx, f"small_pair_add_{k}") for k, (a, b) in enumerate(zip(sharded, from_sibling))]
    by_chip = _chip_scatter(pair_sums)
    to_device = jnp.stack([my_chip, 2 * my_chip + my_c])
    reduced_sp = _chip_sum(by_chip[0], pair_sums[0], to_core, N_CORES, "small_chip_sum")
    reduced_rep = _chip_sum(by_chip[1], pair_sums[1], to_device, N_CHIPS * N_CORES, "chip_sum_rep")
    *full, rep_all = _final_gather(late_reduced + [reduced_sp], reduced_rep)

    g_win = [full[0].reshape(d, ns), early_full[0].reshape(d, ns)]
    g_wout = [full[1].reshape(dms, d), early_full[1].reshape(dms, d)]
    g_sp = full[2].reshape(sm_rows, ds)
    rep_flat = rep_all.reshape(-1)
    rep_out, off = [], 0
    for n in rep_sizes:
        rep_out.append(rep_flat[off:off + n])
        off += n
    grad = dict(
        meta=g_sp[0:n_meta],
        norm_g=rep_out[0].reshape(depth, d),
        w_in=jnp.stack(g_win),
        conv_a_w=jnp.stack([g_sp[n_meta + l * SUBLANES:n_meta + l * SUBLANES + ka, 0:cs] for l in range(depth)]),
        conv_a_b=rep_out[2].reshape(depth, c),
        lru_wr=rep_out[6].reshape(depth, nh, hd, hd),
        lru_br=rep_out[3].reshape(depth, c),
        lru_wi=rep_out[7].reshape(depth, nh, hd, hd),
        lru_bi=rep_out[4].reshape(depth, c),
        lru_lambda=rep_out[5].reshape(depth, c),
        conv_b_w=jnp.stack([g_sp[n_meta + l * SUBLANES + ka:n_meta + l * SUBLANES + ka + kb, 0:cs]
                            for l in range(depth)]),
        w_out=jnp.stack(g_wout),
        final_g=rep_out[1].reshape(d),
    )

    delta, new_m, new_v = {}, {}, {}
    for n, g_first in (("w_in", g_win[0]), ("w_out", g_wout[0])):
        delta[n], new_m[n], new_v[n] = _adamw_layer(weights[n], g_first, mom1[n], mom2[n], 0, early_step[n],
                                                    f"adamw_{n}_0")
    for n in names:
        if n in delta:
            continue
        shape = weights[n].shape
        as_block = shape if len(shape) > 1 else (1,) + shape
        out = _adamw(weights[n].reshape(as_block), grad[n].reshape(as_block), mom1[n].reshape(as_block),
                     mom2[n].reshape(as_block), f"adamw_{n}")
        delta[n], new_m[n], new_v[n] = (o.reshape(shape) for o in out)

    return (loss, grad_x, *[grad[n] for n in names], *[delta[n] for n in names],
            *[new_m[n] for n in names], *[new_v[n] for n in names])
```

```python
import functools

import jax
import jax.numpy as jnp
from jax import lax
from jax.experimental import pallas as pl
from jax.experimental.pallas import tpu as pltpu

F32 = jnp.float32
BF16 = jnp.bfloat16

RMS_EPS = 1e-6
LRU_C = 8.0
ADAM_LR = 0.001
ADAM_B1 = 0.9
ADAM_B2 = 0.999
ADAM_EPS = 1e-08
ADAM_WD = 0.01
ADAM_STEP = 10

N_CHIPS = 4
N_CORES = 2
VMEM_LIMIT_BYTES = 56 * 1024 * 1024
SUBLANES = 8
LANES = 128
ROW_QUANTUM = 384
MIX_CHUNK = 192
SCAN_UNROLL = 4
GATE_BLOCK = 256
MESH = pl.DeviceIdType.MESH
ANY = pl.BlockSpec(memory_space=pl.ANY)

NT_DIMS = (((1,), (1,)), ((), ()))
TN_DIMS = (((0,), (0,)), ((), ()))


def _params(sem):
    return pltpu.CompilerParams(dimension_semantics=sem, vmem_limit_bytes=VMEM_LIMIT_BYTES)


def _sig(x):
    return 0.5 * jnp.tanh(0.5 * x) + 0.5


def _row_tile(t):
    return 704 if t % 704 == 0 else 192


def _col_tile(n, prefs):
    for p in prefs:
        if n % p == 0:
            return p
    return n


def _slab_rows(rows, cols):
    if rows * cols * 4 <= 1024 * 1024:
        return rows
    return _col_tile(rows, (256, 128, 64, 32, 16))


def _norm_in_own(h, g, wg, me_idx, name):
    t, d = h.shape
    s, _, ns = wg.shape
    tm = 1408 if t % 1408 == 0 else _row_tile(t)
    tn = _col_tile(ns, (768, 384, 128))
    nb = ns // tn

    def body(m_ref, h_ref, g_ref, w_ref, u_ref, hn_ref):
        @pl.when(pl.program_id(1) == 0)
        def _():
            x = h_ref[...]
            r = lax.rsqrt(jnp.mean(x * x, axis=-1, keepdims=True) + RMS_EPS)
            hn_ref[...] = ((x * r) * g_ref[...]).astype(BF16)

        u_ref[...] = jnp.dot(hn_ref[...], w_ref[...], preferred_element_type=F32)

    return pl.pallas_call(
        body, name=name,
        grid_spec=pltpu.PrefetchScalarGridSpec(
            num_scalar_prefetch=1, grid=(t // tm, nb),
            in_specs=[pl.BlockSpec((tm, d), lambda i, n, m: (i, 0)),
                      pl.BlockSpec((1, d), lambda i, n, m: (0, 0)),
                      pl.BlockSpec((None, d, tn), lambda i, n, m: (m[0], 0, n))],
            out_specs=[pl.BlockSpec((tm, tn), lambda i, n, m: (i, m[0] * nb + n)),
                       pl.BlockSpec((tm, d), lambda i, n, m: (i, 0))]),
        out_shape=[jax.ShapeDtypeStruct((t, s * ns), F32), jax.ShapeDtypeStruct((t, d), BF16)],
        compiler_params=_params(("arbitrary", "arbitrary")),
    )(me_idx, h, g, wg)


def _norm_in_rest(hn, wg, u, me_idx, name, after=None):
    t, d = hn.shape
    s, _, ns = wg.shape
    tm = 1408 if t % 1408 == 0 else _row_tile(t)
    tn = _col_tile(ns, (1536, 768, 384, 128))
    nb = ns // tn

    def body(m_ref, hn_ref, w_ref, u_in, u_ref):
        del u_in
        u_ref[...] = jnp.dot(hn_ref[...], w_ref[...], preferred_element_type=F32)

    def shard(n, m):
        return (m[0] + 1 + n // nb) % s

    body, more_specs, more = _behind(body, 4, after)
    return pl.pallas_call(
        body, name=name,
        grid_spec=pltpu.PrefetchScalarGridSpec(
            num_scalar_prefetch=1, grid=(t // tm, (s - 1) * nb),
            in_specs=[pl.BlockSpec((tm, d), lambda i, n, m: (i, 0)),
                      pl.BlockSpec((None, d, tn), lambda i, n, m: (shard(n, m), 0, n % nb)),
                      ANY] + more_specs,
            out_specs=pl.BlockSpec((tm, tn), lambda i, n, m: (i, shard(n, m) * nb + n % nb))),
        out_shape=jax.ShapeDtypeStruct(u.shape, u.dtype),
        input_output_aliases={3: 0},
        compiler_params=_params(("arbitrary", "arbitrary")),
    )(me_idx, hn, wg, u, *more)


def _decay_consts(lam):
    z = -lam
    e = jnp.exp(-jnp.abs(z))
    u = 1.0 + e
    log1p_e = jnp.where(u == 1.0, e, jnp.log(u) * (e / (u - 1.0)))
    sp = jnp.maximum(z, 0.0) + log1p_e
    return -LRU_C * sp, LRU_C * _sig(z)


def _gates(xc, wr_ref, br_ref, wi_ref, bi_ref, c8, j, gb):
    sl = slice(j * gb, (j + 1) * gb)
    x16 = xc.astype(BF16)
    r = _sig(jnp.dot(x16, wr_ref[j], preferred_element_type=F32) + br_ref[:, sl])
    ig = _sig(jnp.dot(x16, wi_ref[j], preferred_element_type=F32) + bi_ref[:, sl])
    la = c8[:, sl] * r
    a = jnp.exp(la)
    sq = jnp.sqrt(-jnp.tanh(la) * (a * a + 1.0))
    return r, ig, a, sq


def _mix_fwd(u, wa, ba, wr, br, wi, bi, lam, wb, name):
    t = u.shape[0]
    c = u.shape[1] // 6
    tc = MIX_CHUNK
    gb = wr.shape[1]
    nblk = c // gb
    ka, kb = wa.shape[0], wb.shape[0]

    def body(u_ref, wa_ref, ba_ref, wr_ref, br_ref, wi_ref, bi_ref, lam_ref, wb_ref,
             y_ref, hs_ref, xa_ext, v_ext, xc_s, a_s, b_s, carry_s):
        @pl.when(pl.program_id(0) == 0)
        def _():
            xa_ext[0:SUBLANES, :] = jnp.zeros((SUBLANES, c), F32)
            v_ext[0:SUBLANES, :] = jnp.zeros((SUBLANES, c), F32)
            carry_s[...] = jnp.zeros_like(carry_s)

        xa_ext[SUBLANES:SUBLANES + tc, :] = u_ref[:, 0:c]
        xc = ba_ref[...]
        for k in range(ka):
            xc = xc + wa_ref[pl.ds(k, 1), :] * xa_ext[pl.ds(SUBLANES - (ka - 1) + k, tc), :]
        xc_s[...] = xc
        c8, _ = _decay_consts(lam_ref[...])
        for j in range(nblk):
            sl = slice(j * gb, (j + 1) * gb)
            xcj = xc_s[:, sl]
            _, ig, a, sq = _gates(xcj, wr_ref, br_ref, wi_ref, bi_ref, c8, j, gb)
            a_s[:, sl] = a
            b_s[:, sl] = sq * (ig * xcj)

        row = lax.broadcasted_iota(jnp.int32, (SUBLANES, c), 0)

        def scan_step(j, _):
            off = pl.multiple_of(j * SUBLANES, SUBLANES)
            av = a_s[pl.ds(off, SUBLANES), :]
            bv = b_s[pl.ds(off, SUBLANES), :]
            for d in (1, 2, 4):
                keep = row >= d
                bsh = jnp.where(keep, pltpu.roll(bv, d, axis=0), 0.0)
                ash = jnp.where(keep, pltpu.roll(av, d, axis=0), 1.0)
                bv = av * bsh + bv
                av = av * ash
            hv = av * carry_s[...] + bv
            hs_ref[pl.ds(off, SUBLANES), :] = hv
            carry_s[...] = hs_ref[pl.ds(off + SUBLANES - 1, 1), :]
            return 0

        lax.fori_loop(0, tc // SUBLANES, scan_step, 0, unroll=SCAN_UNROLL)

        ga = u_ref[:, c:2 * c]
        y_ref[:, 0:c] = (hs_ref[...] * (ga * _sig(ga))).astype(BF16)

        v_ext[SUBLANES:SUBLANES + tc, :] = u_ref[:, 3 * c:4 * c] * u_ref[:, 4 * c:5 * c]
        cv = wb_ref[pl.ds(0, 1), :] * v_ext[pl.ds(SUBLANES - (kb - 1), tc), :]
        for k in range(1, kb):
            cv = cv + wb_ref[pl.ds(k, 1), :] * v_ext[pl.ds(SUBLANES - (kb - 1) + k, tc), :]
        gbv = u_ref[:, 5 * c:6 * c]
        y_ref[:, c:2 * c] = (u_ref[:, 2 * c:3 * c] * cv * (gbv * _sig(gbv))).astype(BF16)

        xa_ext[0:SUBLANES, :] = xa_ext[tc:tc + SUBLANES, :]
        v_ext[0:SUBLANES, :] = v_ext[tc:tc + SUBLANES, :]

    full = lambda shape: pl.BlockSpec(shape, lambda i: (0,) * len(shape))
    return pl.pallas_call(
        body, name=name, grid=(t // tc,),
        in_specs=[pl.BlockSpec((tc, 6 * c), lambda i: (i, 0)),
                  full(wa.shape), full(ba.shape), full(wr.shape), full(br.shape),
                  full(wi.shape), full(bi.shape), full(lam.shape), full(wb.shape)],
        out_specs=[pl.BlockSpec((tc, 2 * c), lambda i: (i, 0)),
                   pl.BlockSpec((tc, c), lambda i: (i, 0))],
        out_shape=[jax.ShapeDtypeStruct((t, 2 * c), BF16), jax.ShapeDtypeStruct((t, c), F32)],
        scratch_shapes=[pltpu.VMEM((tc + SUBLANES, c), F32), pltpu.VMEM((tc + SUBLANES, c), F32),
                        pltpu.VMEM((tc, c), F32), pltpu.VMEM((tc, c), F32), pltpu.VMEM((tc, c), F32),
                        pltpu.VMEM((1, c), F32)],
        compiler_params=_params(("arbitrary",)),
    )(u, wa, ba, wr, br, wi, bi, lam, wb)


ROW_DWA = 0
ROW_DBA = 4
ROW_DBR = 5
ROW_DBI = 6
ROW_DLAM = 7
ROW_DWB = 8
SMALL_ROWS = 16


def _mix_bwd(u, hs, dy, wa, ba, wr, br, wi, bi, lam, wb, name, after=None):
    t = u.shape[0]
    c = u.shape[1] // 6
    tc = MIX_CHUNK
    nt = t // tc
    gb = wr.shape[1]
    nblk = c // gb
    ka, kb = wa.shape[0], wb.shape[0]
    assert ka <= ROW_DBA and kb <= SMALL_ROWS - ROW_DWB
    hb = tc // SUBLANES

    def body(u_ref, uh_ref, hs_ref, hsh_ref, dy_ref, wa_ref, ba_ref, wr_ref, br_ref, wi_ref, bi_ref, lam_ref, wb_ref,
             du_ref, dsm_ref, dwr_ref, dwi_ref,
             xa_ext, v_ext, hs_ext, a_ext, ds_ext, dxc_ext, dcv_ext, xc_s, r_s, i_s, sq_s, g_s, an_s):
        i = pl.program_id(0)
        chunk = nt - 1 - i
        tail = slice(tc, tc + SUBLANES)
        head = slice(0, SUBLANES)

        @pl.when(i == 0)
        def _():
            zero = jnp.zeros((SUBLANES, c), F32)
            a_ext[tail, :] = zero
            ds_ext[tail, :] = zero
            dxc_ext[tail, :] = zero
            dcv_ext[tail, :] = zero
            dsm_ref[...] = jnp.zeros_like(dsm_ref)
            dwr_ref[...] = jnp.zeros_like(dwr_ref)
            dwi_ref[...] = jnp.zeros_like(dwi_ref)

        prev = jnp.where(chunk > 0, 1.0, 0.0)
        xa_ext[head, :] = uh_ref[:, 0:c] * prev
        xa_ext[SUBLANES:SUBLANES + tc, :] = u_ref[:, 0:c]
        v_ext[head, :] = uh_ref[:, 3 * c:4 * c] * uh_ref[:, 4 * c:5 * c] * prev
        v_ext[SUBLANES:SUBLANES + tc, :] = u_ref[:, 3 * c:4 * c] * u_ref[:, 4 * c:5 * c]
        hs_ext[head, :] = hsh_ref[...] * prev
        hs_ext[SUBLANES:SUBLANES + tc, :] = hs_ref[...]

        xc = ba_ref[...]
        for k in range(ka):
            xc = xc + wa_ref[pl.ds(k, 1), :] * xa_ext[pl.ds(SUBLANES - (ka - 1) + k, tc), :]
        xc_s[...] = xc
        c8, dc8 = _decay_consts(lam_ref[...])
        for j in range(nblk):
            sl = slice(j * gb, (j + 1) * gb)
            r, ig, a, sq = _gates(xc_s[:, sl], wr_ref, br_ref, wi_ref, bi_ref, c8, j, gb)
            r_s[:, sl] = r
            i_s[:, sl] = ig
            sq_s[:, sl] = sq
            a_ext[0:tc, sl] = a

        ga = u_ref[:, c:2 * c]
        sga = _sig(ga)
        g_s[...] = dy_ref[:, 0:c] * (ga * sga)
        an_s[...] = a_ext[pl.ds(1, tc), :]

        row = lax.broadcasted_iota(jnp.int32, (SUBLANES, c), 0)

        def scan_step(j, _):
            off = pl.multiple_of(tc - SUBLANES - j * SUBLANES, SUBLANES)
            av = an_s[pl.ds(off, SUBLANES), :]
            bv = g_s[pl.ds(off, SUBLANES), :]
            for d in (1, 2, 4):
                keep = row < SUBLANES - d
                bsh = jnp.where(keep, pltpu.roll(bv, SUBLANES - d, axis=0), 0.0)
                ash = jnp.where(keep, pltpu.roll(av, SUBLANES - d, axis=0), 1.0)
                bv = av * bsh + bv
                av = av * ash
            ds_ext[pl.ds(off, SUBLANES), :] = av * ds_ext[pl.ds(off + SUBLANES, 1), :] + bv
            return 0

        lax.fori_loop(0, tc // SUBLANES, scan_step, 0, unroll=SCAN_UNROLL)

        def acc(row_index, val):
            dsm_ref[pl.ds(row_index, 1), :] += jnp.sum(val, axis=0, keepdims=True)

        def acc_block(row_index, sl, val):
            dsm_ref[pl.ds(row_index, 1), sl] += jnp.sum(val, axis=0, keepdims=True)

        for j in range(nblk):
            sl = slice(j * gb, (j + 1) * gb)
            ds = ds_ext[0:tc, sl]
            hprev = hs_ext[pl.ds(SUBLANES - 1, tc), sl]
            a = a_ext[0:tc, sl]
            sq = sq_s[:, sl]
            ig = i_s[:, sl]
            r = r_s[:, sl]
            xcj = xc_s[:, sl]
            t1 = ds * xcj
            dla = (ds * hprev) * a - (t1 * ig) * ((a * a) / sq)
            acc_block(ROW_DLAM, sl, dla * r)
            dpr = (dla * c8[:, sl]) * (r * (1.0 - r))
            dpi = (t1 * sq) * (ig * (1.0 - ig))
            acc_block(ROW_DBR, sl, dpr)
            acc_block(ROW_DBI, sl, dpi)
            p16 = dpr.astype(BF16)
            q16 = dpi.astype(BF16)
            x16 = xcj.astype(BF16)
            dwr_ref[j] += lax.dot_general(x16, p16, TN_DIMS, preferred_element_type=F32)
            dwi_ref[j] += lax.dot_general(x16, q16, TN_DIMS, preferred_element_type=F32)
            dxc = (ds * (sq * ig)
                   + lax.dot_general(p16, wr_ref[j], NT_DIMS, preferred_element_type=F32)
                   + lax.dot_general(q16, wi_ref[j], NT_DIMS, preferred_element_type=F32))
            dxc_ext[0:tc, sl] = dxc
            acc_block(ROW_DBA, sl, dxc)

        dsilu_a = sga * (1.0 + ga * (1.0 - sga))
        du_ref[:, c:2 * c] = (dy_ref[:, 0:c] * hs_ref[...] * dsilu_a).astype(BF16)

        dxc = dxc_ext[0:tc, :]
        dxa = wa_ref[pl.ds(ka - 1, 1), :] * dxc
        acc(ROW_DWA + ka - 1, dxc * xa_ext[SUBLANES:SUBLANES + tc, :])
        for k in range(ka - 1):
            acc(ROW_DWA + k, dxc * xa_ext[pl.ds(SUBLANES - (ka - 1) + k, tc), :])
            dxa = dxa + wa_ref[pl.ds(k, 1), :] * dxc_ext[pl.ds(ka - 1 - k, tc), :]
        du_ref[:, 0:c] = dxa.astype(BF16)

        cv = wb_ref[pl.ds(0, 1), :] * v_ext[pl.ds(SUBLANES - (kb - 1), tc), :]
        for k in range(1, kb):
            cv = cv + wb_ref[pl.ds(k, 1), :] * v_ext[pl.ds(SUBLANES - (kb - 1) + k, tc), :]
        gbv = u_ref[:, 5 * c:6 * c]
        sgb = _sig(gbv)
        silu_b = gbv * sgb
        dyb = dy_ref[:, c:2 * c]
        gB = u_ref[:, 2 * c:3 * c]
        du_ref[:, 2 * c:3 * c] = (dyb * cv * silu_b).astype(BF16)
        du_ref[:, 5 * c:6 * c] = (dyb * gB * cv * (sgb * (1.0 + gbv * (1.0 - sgb)))).astype(BF16)
        dcv = dyb * gB * silu_b
        dcv_ext[0:tc, :] = dcv
        dv = wb_ref[pl.ds(kb - 1, 1), :] * dcv
        acc(ROW_DWB + kb - 1, dcv * v_ext[SUBLANES:SUBLANES + tc, :])
        for k in range(kb - 1):
            acc(ROW_DWB + k, dcv * v_ext[pl.ds(SUBLANES - (kb - 1) + k, tc), :])
            dv = dv + wb_ref[pl.ds(k, 1), :] * dcv_ext[pl.ds(kb - 1 - k, tc), :]
        du_ref[:, 3 * c:4 * c] = (dv * u_ref[:, 4 * c:5 * c]).astype(BF16)
        du_ref[:, 4 * c:5 * c] = (dv * u_ref[:, 3 * c:4 * c]).astype(BF16)

        a_ext[tail, :] = a_ext[head, :]
        ds_ext[tail, :] = ds_ext[head, :]
        dxc_ext[tail, :] = dxc_ext[head, :]
        dcv_ext[tail, :] = dcv_ext[head, :]

        @pl.when(i == nt - 1)
        def _():
            dsm_ref[pl.ds(ROW_DLAM, 1), :] = dsm_ref[pl.ds(ROW_DLAM, 1), :] * dc8

    full = lambda shape: pl.BlockSpec(shape, lambda i: (0,) * len(shape))
    rev = lambda i: (nt - 1 - i, 0)
    halo = lambda i: (jnp.maximum((nt - 1 - i) * hb - 1, 0), 0)
    ext = pltpu.VMEM((tc + SUBLANES, c), F32)
    blk = pltpu.VMEM((tc, c), F32)
    body, more_specs, more = _behind(body, 13, after)
    return pl.pallas_call(
        body, name=name, grid=(nt,),
        in_specs=[pl.BlockSpec((tc, 6 * c), rev), pl.BlockSpec((SUBLANES, 6 * c), halo),
                  pl.BlockSpec((tc, c), rev), pl.BlockSpec((SUBLANES, c), halo),
                  pl.BlockSpec((tc, 2 * c), rev),
                  full(wa.shape), full(ba.shape), full(wr.shape), full(br.shape),
                  full(wi.shape), full(bi.shape), full(lam.shape), full(wb.shape)] + more_specs,
        out_specs=[pl.BlockSpec((tc, 6 * c), rev), full((SMALL_ROWS, c)), full(wr.shape), full(wi.shape)],
        out_shape=[jax.ShapeDtypeStruct((t, 6 * c), BF16), jax.ShapeDtypeStruct((SMALL_ROWS, c), F32),
                   jax.ShapeDtypeStruct(wr.shape, F32), jax.ShapeDtypeStruct(wi.shape, F32)],
        scratch_shapes=[ext] * 7 + [blk] * 6,
        compiler_params=_params(("arbitrary",)),
    )(u, u, hs, hs, dy, wa, ba, wr, br, wi, bi, lam, wb, *more)


def _behind(body, n_in, after):
    if after is None:
        return body, [], []
    return (lambda *refs: body(*refs[:n_in], *refs[n_in + 1:])), [ANY], [after]


def _out_proj_norm(h, y, w, g_next, name, after=None):
    t, d = h.shape
    dm = y.shape[1]
    tm = _row_tile(t)

    def body(h_ref, y_ref, w_ref, g_ref, o_ref, hn_ref):
        x = h_ref[...] + jnp.dot(y_ref[...], w_ref[...], preferred_element_type=F32)
        o_ref[...] = x
        r = lax.rsqrt(jnp.mean(x * x, axis=-1, keepdims=True) + RMS_EPS)
        hn_ref[...] = ((x * r) * g_ref[...]).astype(BF16)

    body, more_specs, more = _behind(body, 4, after)
    rows = pl.BlockSpec((tm, d), lambda i: (i, 0))
    return pl.pallas_call(
        body, name=name, grid=(t // tm,),
        in_specs=[rows, pl.BlockSpec((tm, dm), lambda i: (i, 0)), pl.BlockSpec((dm, d), lambda i: (0, 0)),
                  pl.BlockSpec((1, d), lambda i: (0, 0))] + more_specs,
        out_specs=[rows, rows],
        out_shape=[jax.ShapeDtypeStruct((t, d), F32), jax.ShapeDtypeStruct((t, d), BF16)],
        compiler_params=_params(("arbitrary",)),
    )(h, y, w, g_next, *more)


def _in_proj(hn, wg, name, after=None):
    t, d = hn.shape
    s, _, ns = wg.shape
    tm = 1408 if t % 1408 == 0 else _row_tile(t)

    def body(hn_ref, w_ref, u_ref):
        u_ref[...] = jnp.dot(hn_ref[...], w_ref[...], preferred_element_type=F32)

    body, more_specs, more = _behind(body, 2, after)
    return pl.pallas_call(
        body, name=name, grid=(t // tm, s),
        in_specs=[pl.BlockSpec((tm, d), lambda i, n: (i, 0)),
                  pl.BlockSpec((None, d, ns), lambda i, n: (n, 0, 0))] + more_specs,
        out_specs=pl.BlockSpec((tm, ns), lambda i, n: (i, n)),
        out_shape=jax.ShapeDtypeStruct((t, s * ns), F32),
        compiler_params=_params(("arbitrary", "arbitrary")),
    )(hn, wg, *more)


def _out_proj_dw(y, dout, name, after=None):
    t, dm = y.shape
    d = dout.shape[1]
    tmm = _col_tile(dm, (1024, 512, 256))
    tn = _col_tile(d, (512, 256))

    def body(y_ref, g_ref, o_ref):
        o_ref[...] = lax.dot_general(y_ref[...], g_ref[...].astype(BF16), TN_DIMS, preferred_element_type=F32)

    body, more_specs, more = _behind(body, 2, after)
    return pl.pallas_call(
        body, name=name, grid=(d // tn, dm // tmm),
        in_specs=[pl.BlockSpec((t, tmm), lambda n, m: (0, m)),
                  pl.BlockSpec((t, tn), lambda n, m: (0, n))] + more_specs,
        out_specs=pl.BlockSpec((tmm, tn), lambda n, m: (m, n)),
        out_shape=jax.ShapeDtypeStruct((dm, d), F32),
        compiler_params=_params(("arbitrary", "arbitrary")),
    )(y, dout, *more)


def _in_proj_bwd(du, wg, h, g, dout, name, after=None, split=None, w_below=None):
    t, d = h.shape
    s, _, ns = wg.shape
    tm = _row_tile(t)
    tn = _col_tile(d, (1024, 512, 256))

    def mm_body(du_ref, w_ref, o_ref):
        total = lax.dot_general(du_ref[:, 0:ns], w_ref[0], NT_DIMS, preferred_element_type=F32)
        for a in range(1, s):
            total = total + lax.dot_general(du_ref[:, a * ns:(a + 1) * ns], w_ref[a], NT_DIMS,
                                            preferred_element_type=F32)
        o_ref[...] = total

    mm_body, more_specs, more = _behind(mm_body, 2, after)
    dhn = pl.pallas_call(
        mm_body, name=name, grid=(t // tm, d // tn),
        in_specs=[pl.BlockSpec((tm, s * ns), lambda i, n: (i, 0)),
                  pl.BlockSpec((s, tn, ns), lambda i, n: (0, n, 0))] + more_specs,
        out_specs=pl.BlockSpec((tm, tn), lambda i, n: (i, n)),
        out_shape=jax.ShapeDtypeStruct((t, d), F32),
        compiler_params=_params(("arbitrary", "arbitrary")),
    )(du, wg, *more)

    tr = 352 if t % 352 == 0 else 192
    nt = t // tr

    def row_grad(dhn_ref, h_ref, g_ref, dout_ref, dg_ref):
        @pl.when(pl.program_id(0) == 0)
        def _():
            dg_ref[...] = jnp.zeros_like(dg_ref)

        x = h_ref[...]
        dn = dhn_ref[...]
        r = lax.rsqrt(jnp.mean(x * x, axis=-1, keepdims=True) + RMS_EPS)
        gd = dn * g_ref[...]
        dot = jnp.mean(gd * x, axis=-1, keepdims=True)
        dg_ref[...] += jnp.sum(dn * (x * r), axis=0, keepdims=True)
        return dout_ref[...] + (r * gd - x * ((r * r * r) * dot))

    rows = pl.BlockSpec((tr, d), lambda i: (i, 0))
    one = pl.BlockSpec((1, d), lambda i: (0, 0))
    if split is None:
        dm = w_below.shape[0]

        def norm_body(dhn_ref, h_ref, g_ref, dout_ref, w_ref, dh_ref, dg_ref, dy_ref):
            dh = row_grad(dhn_ref, h_ref, g_ref, dout_ref, dg_ref)
            dh_ref[...] = dh
            dy_ref[...] = lax.dot_general(dh.astype(BF16), w_ref[...], NT_DIMS, preferred_element_type=F32)

        return pl.pallas_call(
            norm_body, name=name + "_norm", grid=(nt,),
            in_specs=[rows, rows, one, rows, pl.BlockSpec((dm, d), lambda i: (0, 0))],
            out_specs=[rows, one, pl.BlockSpec((tr, dm), lambda i: (i, 0))],
            out_shape=[jax.ShapeDtypeStruct((t, d), F32), jax.ShapeDtypeStruct((1, d), F32),
                       jax.ShapeDtypeStruct((t, dm), F32)],
            compiler_params=_params(("arbitrary",)),
        )(dhn, h, g, dout, w_below)

    n_head, n_body = split
    n_first = tr - n_head
    n_last = n_head + n_body - (nt - 1) * tr
    assert nt >= 2 and 0 < n_head < tr and 0 < n_last <= tr and n_head % SUBLANES == 0 and n_last % SUBLANES == 0

    def split_body(dhn_ref, h_ref, g_ref, dout_ref, body_ref, head_ref, dg_ref, stage, sems):
        i = pl.program_id(0)
        slot = i % 2

        def first_copy(sl):
            return pltpu.make_async_copy(stage.at[sl, pl.ds(n_head, n_first)], body_ref.at[pl.ds(0, n_first)], sems.at[sl])

        def middle_copy(sl, step):
            start = pl.multiple_of(step * tr - n_head, SUBLANES)
            return pltpu.make_async_copy(stage.at[sl], body_ref.at[pl.ds(start, tr)], sems.at[sl])

        def last_copy(sl):
            return pltpu.make_async_copy(stage.at[sl, pl.ds(0, n_last)],
                                         body_ref.at[pl.ds((nt - 1) * tr - n_head, n_last)], sems.at[sl])

        dh = row_grad(dhn_ref, h_ref, g_ref, dout_ref, dg_ref)

        @pl.when(i == 2)
        def _():
            first_copy(0).wait()

        @pl.when(i > 2)
        def _():
            middle_copy(slot, i - 2).wait()

        stage[slot] = dh

        @pl.when(i == 0)
        def _():
            head_ref[...] = stage[0, 0:n_head, :]
            first_copy(0).start()

        @pl.when((i > 0) & (i < nt - 1))
        def _():
            middle_copy(slot, i).start()

        @pl.when(i == nt - 1)
        def _():
            last = last_copy((nt - 1) % 2)
            last.start()
            if nt == 2:
                first_copy(0).wait()
            else:
                middle_copy((nt - 2) % 2, nt - 2).wait()
            last.wait()

    return pl.pallas_call(
        split_body, name=name + "_norm", grid=(nt,),
        in_specs=[rows, rows, one, rows],
        out_specs=[ANY, pl.BlockSpec((n_head, d), lambda i: (0, 0)), one],
        out_shape=[jax.ShapeDtypeStruct((n_body, d), F32), jax.ShapeDtypeStruct((n_head, d), F32),
                   jax.ShapeDtypeStruct((1, d), F32)],
        scratch_shapes=[pltpu.VMEM((2, tr, d), F32), pltpu.SemaphoreType.DMA((2,))],
        compiler_params=_params(("arbitrary",)),
    )(dhn, h, g, dout)


def _in_proj_dw(hn, du, s, name, after=None):
    t, d = hn.shape
    ns = du.shape[1] // s
    tmm = _col_tile(d, (1024, 512, 256))
    tn = _col_tile(ns, (768, 384, 128))
    nb = ns // tn

    def body(hn_ref, du_ref, o_ref):
        o_ref[...] = lax.dot_general(hn_ref[...], du_ref[...], TN_DIMS, preferred_element_type=F32)

    body, more_specs, more = _behind(body, 2, after)
    return pl.pallas_call(
        body, name=name, grid=(s * nb, d // tmm),
        in_specs=[pl.BlockSpec((t, tmm), lambda n, m: (0, m)),
                  pl.BlockSpec((t, tn), lambda n, m: (0, n))] + more_specs,
        out_specs=pl.BlockSpec((None, tmm, tn), lambda n, m: (n // nb, m, n % nb)),
        out_shape=jax.ShapeDtypeStruct((s, d, ns), F32),
        compiler_params=_params(("arbitrary", "arbitrary")),
    )(hn, du, *more)


def _out_proj_loss(h, y, w, tgt, g, n_meta, t_real, name):
    t, d = h.shape
    dm = y.shape[1]
    tm = 352 if t % 352 == 0 else 192

    def body(h_ref, y_ref, w_ref, t_ref, g_ref, dh_ref, loss_ref, dg_ref, dmix_ref):
        i = pl.program_id(0)

        @pl.when(i == 0)
        def _():
            loss_ref[...] = jnp.zeros_like(loss_ref)
            dg_ref[...] = jnp.zeros_like(dg_ref)

        x = h_ref[...] + jnp.dot(y_ref[...], w_ref[...], preferred_element_type=F32)
        gv = g_ref[...]
        r = lax.rsqrt(jnp.mean(x * x, axis=-1, keepdims=True) + RMS_EPS)
        xr = x * r
        rows = i * tm + lax.broadcasted_iota(jnp.int32, (tm, 1), 0)
        valid = (rows >= n_meta) & (rows < t_real)
        err = jnp.where(valid, xr * gv - t_ref[...], 0.0)
        loss_ref[...] += 0.5 * jnp.sum(jnp.mean(err * err, axis=-1, keepdims=True))
        dy = err * (1.0 / d)
        gd = dy * gv
        dot = jnp.mean(gd * x, axis=-1, keepdims=True)
        dh = r * gd - x * ((r * r * r) * dot)
        dh_ref[...] = dh
        dg_ref[...] += jnp.sum(dy * xr, axis=0, keepdims=True)
        dmix_ref[...] = lax.dot_general(dh.astype(BF16), w_ref[...], NT_DIMS, preferred_element_type=F32)

    rows = pl.BlockSpec((tm, d), lambda i: (i, 0))
    wide = pl.BlockSpec((tm, dm), lambda i: (i, 0))
    return pl.pallas_call(
        body, name=name, grid=(t // tm,),
        in_specs=[rows, wide, pl.BlockSpec((dm, d), lambda i: (0, 0)), rows, pl.BlockSpec((1, d), lambda i: (0, 0))],
        out_specs=[rows, pl.BlockSpec((1, LANES), lambda i: (0, 0)), pl.BlockSpec((1, d), lambda i: (0, 0)), wide],
        out_shape=[jax.ShapeDtypeStruct((t, d), F32), jax.ShapeDtypeStruct((1, LANES), F32),
                   jax.ShapeDtypeStruct((1, d), F32), jax.ShapeDtypeStruct((t, dm), F32)],
        compiler_params=_params(("arbitrary",)),
    )(h, y, w, tgt, g)


def _adamw_rows(rows, cols):
    for cand in (512, 256, 128, 64, 32, 16, 8):
        if rows % cand == 0 and cand * cols * 4 <= 2 * 1024 * 1024:
            return cand
    return rows


def _adamw_math(w_ref, g_ref, m_ref, v_ref, d_ref, nm_ref, nv_ref):
    gv = g_ref[...]
    m2 = ADAM_B1 * m_ref[...] + (1.0 - ADAM_B1) * gv
    v2 = ADAM_B2 * v_ref[...] + (1.0 - ADAM_B2) * (gv * gv)
    m_hat = m2 / (1.0 - ADAM_B1 ** ADAM_STEP)
    v_hat = v2 / (1.0 - ADAM_B2 ** ADAM_STEP)
    d_ref[...] = -ADAM_LR * (m_hat / (jnp.sqrt(v_hat) + ADAM_EPS) + ADAM_WD * w_ref[...])
    nm_ref[...] = m2
    nv_ref[...] = v2


def _adamw(w, g, m, v, name):
    shape = w.shape
    assert len(shape) >= 2 and w.size * 4 <= 2 * 1024 * 1024

    def body(*refs):
        _adamw_math(*refs)

    spec = pl.BlockSpec(shape, lambda i: (0,) * len(shape))
    return pl.pallas_call(
        body, name=name, grid=(1,),
        in_specs=[spec] * 4, out_specs=[spec] * 3,
        out_shape=[jax.ShapeDtypeStruct(shape, F32)] * 3,
        compiler_params=_params(("arbitrary",)),
    )(w, g, m, v)


def _adamw_layer(w, g, m, v, layer, kept, name, after=None):
    nl, rows, cols = w.shape
    tr = _adamw_rows(rows, cols)
    n_kept = 0 if kept is None else 3

    def body(*refs):
        _adamw_math(*refs[:4], *refs[4 + n_kept:])

    body, more_specs, more = _behind(body, 4 + n_kept, after)
    lay = pl.BlockSpec((None, tr, cols), lambda i: (layer, i, 0))
    return pl.pallas_call(
        body, name=name, grid=(rows // tr,),
        in_specs=[lay, pl.BlockSpec((tr, cols), lambda i: (i, 0)), lay, lay] + [ANY] * n_kept + more_specs,
        out_specs=[lay] * 3,
        out_shape=[jax.ShapeDtypeStruct((nl, rows, cols), F32)] * 3,
        input_output_aliases={4 + k: k for k in range(n_kept)},
        compiler_params=_params(("arbitrary",)),
    )(w, g, m, v, *([] if kept is None else kept), *more)


def _pair_add(x, ra, c_idx, name):
    s, _, rows, cols = x.shape
    tr = _slab_rows(rows, cols)

    def body(c_ref, x_ref, r_ref, o_ref):
        o_ref[...] = (x_ref[...] + r_ref[...]).astype(BF16)

    return pl.pallas_call(
        body, name=name,
        grid_spec=pltpu.PrefetchScalarGridSpec(
            num_scalar_prefetch=1, grid=(s, rows // tr),
            in_specs=[pl.BlockSpec((None, None, tr, cols), lambda a, i, c_ref: (a, c_ref[0], i, 0)),
                      pl.BlockSpec((None, tr, cols), lambda a, i, c_ref: (a, i, 0))],
            out_specs=pl.BlockSpec((None, tr, cols), lambda a, i, c_ref: (a, i, 0))),
        out_shape=jax.ShapeDtypeStruct((s, rows, cols), BF16),
        compiler_params=_params(("arbitrary", "arbitrary")),
    )(c_idx, x, ra)


def _chip_sum(rc, p, where, n_slots, name):
    s, rows, cols = rc.shape
    tr = _slab_rows(rows, cols)

    def body(w_ref, x_ref, p_ref, o_ref):
        me = w_ref[0]
        total = jnp.where(me == 0, p_ref[...], x_ref[0]).astype(F32)
        for a in range(1, s):
            total = total + jnp.where(me == a, p_ref[...], x_ref[a]).astype(F32)
        o_ref[...] = total

    return pl.pallas_call(
        body, name=name,
        grid_spec=pltpu.PrefetchScalarGridSpec(
            num_scalar_prefetch=1, grid=(rows // tr,),
            in_specs=[pl.BlockSpec((s, tr, cols), lambda i, w_ref: (0, i, 0)),
                      pl.BlockSpec((None, tr, cols), lambda i, w_ref: (w_ref[0], i, 0))],
            out_specs=pl.BlockSpec((None, tr, cols), lambda i, w_ref: (w_ref[1], i, 0))),
        out_shape=jax.ShapeDtypeStruct((n_slots, rows, cols), F32),
        compiler_params=_params(("arbitrary",)),
    )(where, rc, p)


def _cast_place(w, layer, me_idx, name, after=None):
    _, rows, cols = w.shape
    tr = _slab_rows(rows, cols)

    def body(m_ref, w_ref, o_ref):
        o_ref[...] = w_ref[...].astype(BF16)

    body, more_specs, more = _behind(body, 2, after)
    return pl.pallas_call(
        body, name=name,
        grid_spec=pltpu.PrefetchScalarGridSpec(
            num_scalar_prefetch=1, grid=(rows // tr,),
            in_specs=[pl.BlockSpec((None, tr, cols), lambda i, m_ref: (layer, i, 0))] + more_specs,
            out_specs=pl.BlockSpec((None, tr, cols), lambda i, m_ref: (m_ref[0], i, 0))),
        out_shape=jax.ShapeDtypeStruct((N_CHIPS, rows, cols), BF16),
        compiler_params=_params(("arbitrary",)),
    )(me_idx, w, *more)


def _place():
    x, y, c = lax.axis_index("x"), lax.axis_index("y"), lax.axis_index("c")
    chips = [(1 - x, y), (x, 1 - y), (1 - x, 1 - y)]
    return x, y, c, chips


def _chip_index(cx, cy):
    return 2 * cx + cy


def _gather_copies(bufs, stage):
    x, y, c, chips = _place()
    me = _chip_index(x, y)
    copies = []
    for b in bufs:
        for chip in chips:
            src = _chip_index(*chip)
            if stage == 0:
                copies.append((b.at[me, c], (*chip, c), b.at[src, c]))
            else:
                copies.append((b.at[src, c], (x, y, 1 - c), b.at[src, 1 - c]))
    return copies


def _remote(ref, peer, ssem, rsem, k):
    return pltpu.make_async_remote_copy(src_ref=ref, dst_ref=ref, send_sem=ssem.at[k], recv_sem=rsem.at[k],
                                        device_id=peer, device_id_type=MESH)


def _gather_first(bufs, small):
    n = len(bufs)
    k = 3 * n

    def body(*refs):
        sm_ref = refs[n]
        b_refs, smg_ref = refs[n + 1:2 * n + 1], refs[2 * n + 1]
        lsem, ssem, rsem = refs[2 * n + 2:]
        x, y, c, chips = _place()
        me = _chip_index(x, y)
        local = pltpu.make_async_copy(sm_ref, smg_ref.at[me], lsem)
        local.start()
        first = _gather_copies(b_refs, 0)
        second = _gather_copies(b_refs, 1)
        started = []
        for i, (ref, peer, _) in enumerate(first):
            started.append(_remote(ref, peer, ssem, rsem, i))
        for j, chip in enumerate(chips):
            started.append(pltpu.make_async_remote_copy(
                src_ref=sm_ref, dst_ref=smg_ref.at[me], send_sem=ssem.at[2 * k + j], recv_sem=rsem.at[2 * k + j],
                device_id=(*chip, c), device_id_type=MESH))
        for cp in started:
            cp.start()
        for i, (_, peer, lands) in enumerate(first):
            _remote(lands, peer, ssem, rsem, i).wait_recv()
            ref, sib, _ = second[i]
            fwd = _remote(ref, sib, ssem, rsem, k + i)
            fwd.start()
            started.append(fwd)
        for i, (_, sib, lands) in enumerate(second):
            _remote(lands, sib, ssem, rsem, k + i).wait_recv()
        for j, chip in enumerate(chips):
            theirs = smg_ref.at[_chip_index(*chip)]
            pltpu.make_async_remote_copy(src_ref=theirs, dst_ref=theirs, send_sem=ssem.at[2 * k + j],
                                         recv_sem=rsem.at[2 * k + j], device_id=(*chip, c),
                                         device_id_type=MESH).wait_recv()
        for cp in started:
            cp.wait_send()
        local.wait()

    return pl.pallas_call(
        body, name="gather_first",
        in_specs=[ANY] * (n + 1), out_specs=[ANY] * (n + 1),
        out_shape=[jax.ShapeDtypeStruct(b.shape, b.dtype) for b in bufs]
        + [jax.ShapeDtypeStruct((N_CHIPS,) + small.shape, small.dtype)],
        input_output_aliases={i: i for i in range(n)},
        scratch_shapes=[pltpu.SemaphoreType.DMA, pltpu.SemaphoreType.DMA((2 * k + 3,)),
                        pltpu.SemaphoreType.DMA((2 * k + 3,))],
    )(*bufs, small)


HBM = pl.BlockSpec(memory_space=pltpu.HBM)
SEM = pl.BlockSpec(memory_space=pltpu.SEMAPHORE)
DATAFLOW = pltpu.SideEffectType.DATAFLOW_SIDE_EFFECTING


def _copies_start(bufs, plan, n_copies, name, after=None):
    n = len(bufs)
    extra = [] if after is None else [after]

    def body(*refs):
        refs = refs[:n] + refs[n + len(extra):]
        ssem, rsem = refs[n], refs[n + 1]
        b_refs, token = refs[n + 2:2 * n + 2], refs[2 * n + 2]
        copies = plan(b_refs)
        assert len(copies) == n_copies
        for i, (src, dst, peer, _) in enumerate(copies):
            pltpu.make_async_remote_copy(src_ref=src, dst_ref=dst, send_sem=ssem.at[i], recv_sem=rsem.at[i],
                                         device_id=peer, device_id_type=MESH).start()
        token[...] = jnp.zeros_like(token)

    return pl.pallas_call(
        body, name=name,
        out_shape=(pltpu.SemaphoreType.DMA((n_copies,)), pltpu.SemaphoreType.DMA((n_copies,)),
                   *[pltpu.HBM(b.shape, b.dtype) for b in bufs], jax.ShapeDtypeStruct((SUBLANES, LANES), F32)),
        in_specs=[HBM] * n + [ANY] * len(extra),
        out_specs=(SEM, SEM, *[HBM] * n, pl.BlockSpec(memory_space=pltpu.VMEM)),
        input_output_aliases={i: 2 + i for i in range(n)},
        compiler_params=pltpu.CompilerParams(has_side_effects=DATAFLOW),
    )(*[pltpu.with_memory_space_constraint(b, pltpu.HBM) for b in bufs], *extra)


def _copies_wait(bufs, ssem, rsem, after, plan, name):
    n = len(bufs)
    afters = list(after) if isinstance(after, (list, tuple)) else [after]

    def body(*refs):
        b_refs, ssem_ref, rsem_ref = refs[:n], refs[n], refs[n + 1]
        for i, (src, dst, peer, lands) in enumerate(plan(b_refs)):
            pltpu.make_async_remote_copy(src_ref=src, dst_ref=dst, send_sem=ssem_ref.at[i], recv_sem=rsem_ref.at[i],
                                         device_id=peer, device_id_type=MESH).wait_send()
            pltpu.make_async_remote_copy(src_ref=lands, dst_ref=lands, send_sem=ssem_ref.at[i],
                                         recv_sem=rsem_ref.at[i], device_id=peer, device_id_type=MESH).wait_recv()

    return pl.pallas_call(
        body, name=name,
        out_shape=tuple(pltpu.HBM(b.shape, b.dtype) for b in bufs),
        in_specs=[HBM] * n + [SEM, SEM] + [ANY] * len(afters), out_specs=tuple([HBM] * n),
        input_output_aliases={i: i for i in range(n)},
        compiler_params=pltpu.CompilerParams(has_side_effects=DATAFLOW),
    )(*bufs, ssem, rsem, *afters)


def _gather_plan(stage):
    return lambda refs: [(ref, ref, peer, lands) for ref, peer, lands in _gather_copies(refs, stage)]


def _swap_plan(refs):
    n = len(refs) // 2
    x, y, c, _ = _place()
    return [(refs[a].at[:, 1 - c], refs[n + a], (x, y, 1 - c), refs[n + a]) for a in range(n)]


def _scatter_plan(refs):
    n = len(refs) // 2
    x, y, c, chips = _place()
    me = _chip_index(x, y)
    return [(refs[a].at[_chip_index(*chip)], refs[n + a].at[me], (*chip, c), refs[n + a].at[_chip_index(*chip)])
            for a in range(n) for chip in chips]


def _pair_gather_plan(refs):
    x, y, c, _ = _place()
    return [(r.at[c], r.at[c], (x, y, 1 - c), r.at[1 - c]) for r in refs]


def _pair_swap(xs, name):
    n = len(xs)

    def body(*refs):
        x_refs, o_refs, ssem, rsem = refs[:n], refs[n:2 * n], refs[2 * n], refs[2 * n + 1]
        x, y, c, _ = _place()
        copies = [pltpu.make_async_remote_copy(src_ref=x_refs[a].at[:, 1 - c], dst_ref=o_refs[a],
                                               send_sem=ssem.at[a], recv_sem=rsem.at[a],
                                               device_id=(x, y, 1 - c), device_id_type=MESH) for a in range(n)]
        for cp in copies:
            cp.start()
        for cp in copies:
            cp.wait()

    return pl.pallas_call(
        body, name=name, in_specs=[ANY] * n, out_specs=[ANY] * n,
        out_shape=[jax.ShapeDtypeStruct((a.shape[0],) + a.shape[2:], a.dtype) for a in xs],
        scratch_shapes=[pltpu.SemaphoreType.DMA((n,)), pltpu.SemaphoreType.DMA((n,))],
    )(*xs)


def _chip_scatter(ps):
    n = len(ps)

    def body(*refs):
        p_refs, o_refs, ssem, rsem = refs[:n], refs[n:2 * n], refs[2 * n], refs[2 * n + 1]
        x, y, c, chips = _place()
        me = _chip_index(x, y)
        sends = []
        for a in range(n):
            for j, chip in enumerate(chips):
                sends.append(pltpu.make_async_remote_copy(
                    src_ref=p_refs[a].at[_chip_index(*chip)], dst_ref=o_refs[a].at[me],
                    send_sem=ssem.at[3 * a + j], recv_sem=rsem.at[3 * a + j],
                    device_id=(*chip, c), device_id_type=MESH))
        for cp in sends:
            cp.start()
        for a in range(n):
            for j, chip in enumerate(chips):
                src = _chip_index(*chip)
                pltpu.make_async_remote_copy(
                    src_ref=p_refs[a].at[src], dst_ref=o_refs[a].at[src],
                    send_sem=ssem.at[3 * a + j], recv_sem=rsem.at[3 * a + j],
                    device_id=(*chip, c), device_id_type=MESH).wait_recv()
        for cp in sends:
            cp.wait_send()

    return pl.pallas_call(
        body, name="chip_scatter", in_specs=[ANY] * n, out_specs=[ANY] * n,
        out_shape=[jax.ShapeDtypeStruct(a.shape, a.dtype) for a in ps],
        scratch_shapes=[pltpu.SemaphoreType.DMA((3 * n,)), pltpu.SemaphoreType.DMA((3 * n,))],
    )(*ps)


def _final_gather(fs, rep):
    n = len(fs)

    def body(*refs):
        o_refs, repo_ref = refs[n + 1:2 * n + 1], refs[2 * n + 1]
        ssem, rsem = refs[2 * n + 2:]
        x, y, c, chips = _place()
        slot = 4 * x + 2 * y + c
        copies = [pltpu.make_async_remote_copy(src_ref=o_refs[a].at[c], dst_ref=o_refs[a].at[c],
                                               send_sem=ssem.at[a], recv_sem=rsem.at[a],
                                               device_id=(x, y, 1 - c), device_id_type=MESH) for a in range(n)]
        peers = [(x, y, 1 - c)] + [(*chip, c) for chip in chips] + [(*chip, 1 - c) for chip in chips]
        for k, peer in enumerate(peers):
            copies.append(pltpu.make_async_remote_copy(src_ref=repo_ref.at[slot], dst_ref=repo_ref.at[slot],
                                                       send_sem=ssem.at[n + k], recv_sem=rsem.at[n + k],
                                                       device_id=peer, device_id_type=MESH))
        for cp in copies:
            cp.start()
        for a in range(n):
            pltpu.make_async_remote_copy(src_ref=o_refs[a].at[1 - c], dst_ref=o_refs[a].at[1 - c],
                                         send_sem=ssem.at[a], recv_sem=rsem.at[a],
                                         device_id=(x, y, 1 - c), device_id_type=MESH).wait_recv()
        for k, peer in enumerate(peers):
            px, py, pc = peer
            theirs = repo_ref.at[4 * px + 2 * py + pc]
            pltpu.make_async_remote_copy(src_ref=theirs, dst_ref=theirs, send_sem=ssem.at[n + k], recv_sem=rsem.at[n + k],
                                         device_id=peer, device_id_type=MESH).wait_recv()
        for cp in copies:
            cp.wait_send()

    return pl.pallas_call(
        body, name="final_gather", in_specs=[ANY] * (n + 1), out_specs=[ANY] * (n + 1),
        out_shape=[jax.ShapeDtypeStruct(a.shape, a.dtype) for a in fs] + [jax.ShapeDtypeStruct(rep.shape, rep.dtype)],
        input_output_aliases={k: k for k in range(n + 1)},
        scratch_shapes=[pltpu.SemaphoreType.DMA((n + 7,)), pltpu.SemaphoreType.DMA((n + 7,))],
    )(*fs, rep)


def _block_diag(w, gb):
    nh, hd, _ = w.shape
    per = gb // hd
    w4 = w.reshape(nh // per, per, hd, hd)
    eye = jnp.eye(per, dtype=w.dtype)
    return jnp.einsum("jaik,ab->jaibk", w4, eye).reshape(nh // per, gb, gb)


def _diag_blocks(dense, hd):
    nj, gb, _ = dense.shape
    per = gb // hd
    d5 = dense.reshape(nj, per, hd, per, hd)
    return jnp.stack([d5[:, a, :, a, :] for a in range(per)], axis=1).reshape(nj * per, hd, hd)


def _round_up(n, q):
    return (n + q - 1) // q * q


def kernel(x, meta, norm_g, w_in, conv_a_w, conv_a_b, lru_wr, lru_br, lru_wi, lru_bi, lru_lambda, conv_b_w, w_out, final_g, loss_target, m_meta, m_norm_g, m_w_in, m_conv_a_w, m_conv_a_b, m_lru_wr, m_lru_br, m_lru_wi, m_lru_bi, m_lru_lambda, m_conv_b_w, m_w_out, m_final_g, v_meta, v_norm_g, v_w_in, v_conv_a_w, v_conv_a_b, v_lru_wr, v_lru_br, v_lru_wi, v_lru_bi, v_lru_lambda, v_conv_b_w, v_w_out, v_final_g):
    weights = dict(meta=meta, norm_g=norm_g, w_in=w_in, conv_a_w=conv_a_w, conv_a_b=conv_a_b, lru_wr=lru_wr,
                   lru_br=lru_br, lru_wi=lru_wi, lru_bi=lru_bi, lru_lambda=lru_lambda, conv_b_w=conv_b_w,
                   w_out=w_out, final_g=final_g)
    mom1 = dict(meta=m_meta, norm_g=m_norm_g, w_in=m_w_in, conv_a_w=m_conv_a_w, conv_a_b=m_conv_a_b,
                lru_wr=m_lru_wr, lru_br=m_lru_br, lru_wi=m_lru_wi, lru_bi=m_lru_bi, lru_lambda=m_lru_lambda,
                conv_b_w=m_conv_b_w, w_out=m_w_out, final_g=m_final_g)
    mom2 = dict(meta=v_meta, norm_g=v_norm_g, w_in=v_w_in, conv_a_w=v_conv_a_w, conv_a_b=v_conv_a_b,
                lru_wr=v_lru_wr, lru_br=v_lru_br, lru_wi=v_lru_wi, lru_bi=v_lru_bi, lru_lambda=v_lru_lambda,
                conv_b_w=v_conv_b_w, w_out=v_w_out, final_g=v_final_g)
    names = list(weights)

    assert x.shape[0] == 1
    seq, d = x.shape[1], x.shape[2]
    n_meta, ds = meta.shape
    depth = norm_g.shape[0]
    c = lru_lambda.shape[1]
    nh, hd = lru_wr.shape[1], lru_wr.shape[2]
    ns = w_in.shape[2]
    dms = w_out.shape[1]
    cs = conv_a_w.shape[2]
    ka, kb = conv_a_w.shape[1], conv_b_w.shape[1]
    s = N_CHIPS
    assert depth == N_CORES and d == s * ds and c == s * cs and s * ns == 6 * c and s * dms == 2 * c
    gb = min(GATE_BLOCK, c)
    t_real = n_meta + seq
    t = _round_up(t_real, ROW_QUANTUM)
    my_c = lax.axis_index("c").astype(jnp.int32)
    my_chip = (2 * lax.axis_index("x") + lax.axis_index("y")).astype(jnp.int32)
    c_idx = my_c.reshape(1)
    chip_idx = my_chip.reshape(1)

    sm_rows = _round_up(n_meta + depth * SUBLANES, 2 * SUBLANES)
    small = jnp.zeros((sm_rows, ds), F32)
    small = small.at[0:n_meta, :].set(meta)
    for l in range(depth):
        base = n_meta + l * SUBLANES
        small = small.at[base:base + ka, 0:cs].set(conv_a_w[l])
        small = small.at[base + ka:base + ka + kb, 0:cs].set(conv_b_w[l])
    (small_g,) = _gather_first([], small)
    meta_full = jnp.transpose(small_g[:, 0:n_meta, :], (1, 0, 2)).reshape(n_meta, d)
    wa_full, wb_full = [], []
    for l in range(depth):
        base = n_meta + l * SUBLANES
        wa_full.append(jnp.transpose(small_g[:, base:base + ka, 0:cs], (1, 0, 2)).reshape(ka, c))
        wb_full.append(jnp.transpose(small_g[:, base + ka:base + ka + kb, 0:cs], (1, 0, 2)).reshape(kb, c))
    win0 = _cast_place(w_in, 0, chip_idx, "cast_w_in_0").reshape(s, 2, d // 2, ns)
    ssem_w, rsem_w, win0, token_w = _copies_start([win0], _gather_plan(0), 3, "gather_win0_ici_start", after=small_g)
    win_b = [None] + [_cast_place(w_in, l, chip_idx, f"cast_w_in_{l}", after=token_w).reshape(s, 2, d // 2, ns)
                      for l in range(1, depth)]
    wout_b = [_cast_place(w_out, l, chip_idx, f"cast_w_out_{l}", after=token_w).reshape(s, 2, dms // 2, d)
              for l in range(depth)]
    h = jnp.concatenate([meta_full, x[0], jnp.zeros((t - t_real, d), F32)], axis=0) + token_w[0, 0]
    tgt = jnp.concatenate([jnp.zeros((n_meta, d), F32), loss_target[0], jnp.zeros((t - t_real, d), F32)],
                          axis=0) + token_w[0, 0]
    u_own, hn_own = _norm_in_own(h, norm_g[0].reshape(1, d), win0.reshape(s, d, ns), chip_idx, "norm_in_0_own")
    (win0,) = _copies_wait([win0], ssem_w, rsem_w, [u_own, tgt] + win_b[1:] + wout_b, _gather_plan(0),
                           "gather_win0_ici_wait")
    ssem_w, rsem_w, win0, token_w = _copies_start([win0], _gather_plan(1), 3, "gather_win0_d2d_start")
    def travel(buf, stage, tag, after):
        return _copies_start([buf], _gather_plan(stage), 3, f"gather_{tag}_{'d2d' if stage else 'ici'}_start",
                             after=after)

    def arrived(state, stage, tag, after):
        (buf,) = _copies_wait([state[2]], state[0], state[1], after, _gather_plan(stage),
                              f"gather_{tag}_{'d2d' if stage else 'ici'}_wait")
        return buf

    on_wout0 = travel(wout_b[0], 0, "wout0", token_w)
    on_win1 = travel(win_b[1], 0, "win1", on_wout0[3])
    on_wout1 = travel(wout_b[1], 0, "wout1", on_win1[3])
    token = on_wout1[3]
    (win_b[0],) = _copies_wait([win0], ssem_w, rsem_w, token, _gather_plan(1), "gather_win0_d2d_wait")

    layer_w = []
    for l in range(depth):
        layer_w.append(dict(
            g=norm_g[l].reshape(1, d), wa=wa_full[l], ba=conv_a_b[l].reshape(1, c),
            wr=_block_diag(lru_wr[l], gb).astype(BF16), br=lru_br[l].reshape(1, c),
            wi=_block_diag(lru_wi[l], gb).astype(BF16), bi=lru_bi[l].reshape(1, c),
            lam=lru_lambda[l].reshape(1, c), wb=wb_full[l]))
    saved = []
    for l, lw in enumerate(layer_w):
        first = l == 0
        lw["win"] = win_b[l].reshape(s, d, ns)
        if first:
            u = _norm_in_rest(hn_own, lw["win"], u_own, chip_idx, "norm_in_0_rest", after=token)
            hn = hn_own
            on_wout0 = travel(arrived(on_wout0, 0, "wout0", u), 1, "wout0", None)
            token = on_wout0[3]
        else:
            hn = hn_next
            u = _in_proj(hn, lw["win"], f"norm_in_{l}", after=token)
            wout_b[1] = arrived(on_wout1, 1, "wout1", u)
        y, hs = _mix_fwd(u, lw["wa"], lw["ba"] + token[0, 0] if first else lw["ba"], lw["wr"], lw["br"], lw["wi"],
                         lw["bi"], lw["lam"], lw["wb"], f"mix_fwd_{l}")
        token = None
        if first:
            wout_b[0] = arrived(on_wout0, 1, "wout0", y)
            on_win1 = travel(arrived(on_win1, 0, "win1", y), 1, "win1", None)
            token = on_win1[3]
        lw["wout"] = wout_b[l].reshape(2 * c, d)
        saved.append((h, u, hn, y, hs))
        if first:
            h, hn_next = _out_proj_norm(h, y, lw["wout"], layer_w[1]["g"], f"out_proj_{l}", after=token)
        else:
            dh, loss_lanes, d_final_g, dy = _out_proj_loss(h, y, lw["wout"], tgt, final_g.reshape(1, d), n_meta,
                                                           t_real, f"out_proj_{l}_loss")
        if first:
            win_b[1] = arrived(on_win1, 1, "win1", h)
            on_wout1 = travel(arrived(on_wout1, 0, "wout1", h), 1, "wout1", None)
            token = on_wout1[3]
    loss = lax.psum(loss_lanes[0, 0], ("x", "y", "c"))

    to_core = jnp.stack([my_chip, my_c])
    grads = [None] * depth
    early = None
    for l in reversed(range(depth)):
        lw = layer_w[l]
        h_in, u, hn, y, hs = saved[l]
        token = early[-1] if early else None
        d_wout = _out_proj_dw(y, dh, f"out_proj_dw_{l}", after=token)
        if early:
            ssem, rsem, bufs, _ = early
            bufs = _copies_wait(bufs, ssem, rsem, d_wout, _swap_plan, "early_swap_wait")
            half = len(bufs) // 2
            sums = [_pair_add(a, b, c_idx, f"early_pair_add_{k}") for k, (a, b) in enumerate(zip(bufs[:half], bufs[half:]))]
            lands = [lax.empty(p.shape, p.dtype) for p in sums]
            ssem, rsem, *bufs, token = _copies_start(sums + lands, _scatter_plan, 3 * half, "early_scatter_start")
        du, dsm, d_wr, d_wi = _mix_bwd(u, hs, dy, lw["wa"], lw["ba"], lw["wr"], lw["br"], lw["wi"], lw["bi"],
                                       lw["lam"], lw["wb"], f"mix_bwd_{l}", after=token)
        if early:
            bufs = _copies_wait(bufs, ssem, rsem, du, _scatter_plan, "early_scatter_wait")
            halves = [_chip_sum(rc, p, to_core, N_CORES, f"early_chip_sum_{k}")
                      for k, (p, rc) in enumerate(zip(bufs[:half], bufs[half:]))]
            ssem, rsem, *bufs, token = _copies_start(halves, _pair_gather_plan, half, "early_gather_start")
        d_win = _in_proj_dw(hn, du, s, f"in_proj_dw_{l}", after=token)
        srcs = [d_win.reshape(s, 2, d // 2, ns), d_wout.reshape(s, 2, dms // 2, d)]
        if early:
            early_full = _copies_wait(bufs, ssem, rsem, d_win, _pair_gather_plan, "early_gather_wait")
            lands = [lax.empty((a.shape[0],) + a.shape[2:], a.dtype) for a in srcs]
            ssem, rsem, *bufs, token = _copies_start(srcs + lands, _swap_plan, len(srcs), "late_swap_start")
            last = depth - 1
            early_grad = dict(w_in=early_full[0].reshape(d, ns), w_out=early_full[1].reshape(dms, d))
            early_step = {n: _adamw_layer(weights[n], early_grad[n], mom1[n], mom2[n], last, None,
                                          f"adamw_{n}_{last}", after=token) for n in ("w_in", "w_out")}
            bufs = _copies_wait(bufs, ssem, rsem, [o[0] for o in early_step.values()], _swap_plan, "late_swap_wait")
            late_sums = [_pair_add(a, b, c_idx, f"pair_add_{k}")
                         for k, (a, b) in enumerate(zip(bufs[:len(srcs)], bufs[len(srcs):]))]
            lands = [lax.empty(p.shape, p.dtype) for p in late_sums]
            ssem, rsem, *bufs, token = _copies_start(late_sums + lands, _scatter_plan, 3 * len(srcs), "late_scatter_start")
        if l > 0:
            dh, d_g, dy = _in_proj_bwd(du, lw["win"], h_in, lw["g"], dh, f"in_proj_bwd_{l}", after=token,
                                       w_below=layer_w[l - 1]["wout"])
        else:
            grad_x, d_meta, d_g = _in_proj_bwd(du, lw["win"], h_in, lw["g"], dh, f"in_proj_bwd_{l}", after=token,
                                               split=(n_meta, seq))
        if early:
            bufs = _copies_wait(bufs, ssem, rsem, grad_x, _scatter_plan, "late_scatter_wait")
            late_reduced = [_chip_sum(rc, p, to_core, N_CORES, f"chip_sum_{k}")
                            for k, (p, rc) in enumerate(zip(bufs[:len(srcs)], bufs[len(srcs):]))]
        grads[l] = dict(dsm=dsm, wr=_diag_blocks(d_wr, hd), wi=_diag_blocks(d_wi, hd), g=d_g)
        if l == depth - 1:
            lands = [lax.empty((a.shape[0],) + a.shape[2:], a.dtype) for a in srcs]
            ssem, rsem, *bufs, token = _copies_start(srcs + lands, _swap_plan, len(srcs), "early_swap_start")
            early = (ssem, rsem, bufs, token)
        else:
            early = None
    grad_x = grad_x[None]

    sharded = []
    sp = jnp.zeros((sm_rows, s, ds), F32)
    sp = sp.at[0:n_meta].set(d_meta.reshape(n_meta, s, ds))
    for l in range(depth):
        base = n_meta + l * SUBLANES
        dsm = grads[l]["dsm"]
        sp = sp.at[base:base + ka, :, 0:cs].set(dsm[ROW_DWA:ROW_DWA + ka].reshape(ka, s, cs))
        sp = sp.at[base + ka:base + ka + kb, :, 0:cs].set(dsm[ROW_DWB:ROW_DWB + kb].reshape(kb, s, cs))
    sharded.append(jnp.transpose(sp, (1, 0, 2)).reshape(s, 2, sm_rows // 2, ds))
    rep_parts = [jnp.concatenate([grads[l]["g"].reshape(-1) for l in range(depth)]), d_final_g.reshape(-1)]
    for row in (ROW_DBA, ROW_DBR, ROW_DBI, ROW_DLAM):
        rep_parts.append(jnp.concatenate([grads[l]["dsm"][row] for l in range(depth)]))
    rep_parts.append(jnp.concatenate([grads[l]["wr"].reshape(-1) for l in range(depth)]))
    rep_parts.append(jnp.concatenate([grads[l]["wi"].reshape(-1) for l in range(depth)]))
    rep_sizes = [p.shape[0] for p in rep_parts]
    piece = _round_up(-(-sum(rep_sizes) // (s * 2)), 2 * SUBLANES * LANES)
    flat = jnp.concatenate(rep_parts + [jnp.zeros((s * 2 * piece - sum(rep_sizes),), F32)])
    sharded.append(flat.reshape(s, 2, piece // LANES, LANES))

    from_sibling = _pair_swap(sharded, "small_pair_swap")
    pair_sums = [_pair_add(a, b, c_idx, f"small_pair_add_{k}") for k, (a, b) in enumerate(zip(sharded, from_sibling))]
    by_chip = _chip_scatter(pair_sums)
    to_device = jnp.stack([my_chip, 2 * my_chip + my_c])
    reduced_sp = _chip_sum(by_chip[0], pair_sums[0], to_core, N_CORES, "small_chip_sum")
    reduced_rep = _chip_sum(by_chip[1], pair_sums[1], to_device, N_CHIPS * N_CORES, "chip_sum_rep")
    *full, rep_all = _final_gather(late_reduced + [reduced_sp], reduced_rep)

    g_win = [full[0].reshape(d, ns), early_full[0].reshape(d, ns)]
    g_wout = [full[1].reshape(dms, d), early_full[1].reshape(dms, d)]
    g_sp = full[2].reshape(sm_rows, ds)
    rep_flat = rep_all.reshape(-1)
    rep_out, off = [], 0
    for n in rep_sizes:
        rep_out.append(rep_flat[off:off + n])
        off += n
    grad = dict(
        meta=g_sp[0:n_meta],
        norm_g=rep_out[0].reshape(depth, d),
        w_in=jnp.stack(g_win),
        conv_a_w=jnp.stack([g_sp[n_meta + l * SUBLANES:n_meta + l * SUBLANES + ka, 0:cs] for l in range(depth)]),
        conv_a_b=rep_out[2].reshape(depth, c),
        lru_wr=rep_out[6].reshape(depth, nh, hd, hd),
        lru_br=rep_out[3].reshape(depth, c),
        lru_wi=rep_out[7].reshape(depth, nh, hd, hd),
        lru_bi=rep_out[4].reshape(depth, c),
        lru_lambda=rep_out[5].reshape(depth, c),
        conv_b_w=jnp.stack([g_sp[n_meta + l * SUBLANES + ka:n_meta + l * SUBLANES + ka + kb, 0:cs]
                            for l in range(depth)]),
        w_out=jnp.stack(g_wout),
        final_g=rep_out[1].reshape(d),
    )

    delta, new_m, new_v = {}, {}, {}
    for n, g_first in (("w_in", g_win[0]), ("w_out", g_wout[0])):
        delta[n], new_m[n], new_v[n] = _adamw_layer(weights[n], g_first, mom1[n], mom2[n], 0, early_step[n],
                                                    f"adamw_{n}_0")
    for n in names:
        if n in delta:
            continue
        shape = weights[n].shape
        as_block = shape if len(shape) > 1 else (1,) + shape
        out = _adamw(weights[n].reshape(as_block), grad[n].reshape(as_block), mom1[n].reshape(as_block),
                     mom2[n].reshape(as_block), f"adamw_{n}")
        delta[n], new_m[n], new_v[n] = (o.reshape(shape) for o in out)

    return (loss, grad_x, *[grad[n] for n in names], *[delta[n] for n in names],
            *[new_m[n] for n in names], *[new_v[n] for n in names])
```

```python
import functools

import jax
import jax.numpy as jnp
from jax import lax
from jax.experimental import pallas as pl
from jax.experimental.pallas import tpu as pltpu

F32 = jnp.float32
BF16 = jnp.bfloat16

RMS_EPS = 1e-6
LRU_C = 8.0
ADAM_LR = 0.001
ADAM_B1 = 0.9
ADAM_B2 = 0.999
ADAM_EPS = 1e-08
ADAM_WD = 0.01
ADAM_STEP = 10

N_CHIPS = 4
N_CORES = 2
VMEM_LIMIT_BYTES = 56 * 1024 * 1024
SUBLANES = 8
LANES = 128
ROW_QUANTUM = 384
MIX_CHUNK = 192
SCAN_UNROLL = 4
GATE_BLOCK = 256
MESH = pl.DeviceIdType.MESH
ANY = pl.BlockSpec(memory_space=pl.ANY)

NT_DIMS = (((1,), (1,)), ((), ()))
TN_DIMS = (((0,), (0,)), ((), ()))


def _params(sem):
    return pltpu.CompilerParams(dimension_semantics=sem, vmem_limit_bytes=VMEM_LIMIT_BYTES)


def _sig(x):
    return 0.5 * jnp.tanh(0.5 * x) + 0.5


def _row_tile(t):
    return 704 if t % 704 == 0 else 192


def _col_tile(n, prefs):
    for p in prefs:
        if n % p == 0:
            return p
    return n


def _slab_rows(rows, cols):
    if rows * cols * 4 <= 1024 * 1024:
        return rows
    return _col_tile(rows, (256, 128, 64, 32, 16))


def _norm_in_own(h, g, wg, me_idx, name):
    t, d = h.shape
    s, _, ns = wg.shape
    tm = 1408 if t % 1408 == 0 else _row_tile(t)
    tn = _col_tile(ns, (768, 384, 128))
    nb = ns // tn

    def body(m_ref, h_ref, g_ref, w_ref, u_ref, hn_ref):
        @pl.when(pl.program_id(1) == 0)
        def _():
            x = h_ref[...]
            r = lax.rsqrt(jnp.mean(x * x, axis=-1, keepdims=True) + RMS_EPS)
            hn_ref[...] = ((x * r) * g_ref[...]).astype(BF16)

        u_ref[...] = jnp.dot(hn_ref[...], w_ref[...], preferred_element_type=F32)

    return pl.pallas_call(
        body, name=name,
        grid_spec=pltpu.PrefetchScalarGridSpec(
            num_scalar_prefetch=1, grid=(t // tm, nb),
            in_specs=[pl.BlockSpec((tm, d), lambda i, n, m: (i, 0)),
                      pl.BlockSpec((1, d), lambda i, n, m: (0, 0)),
                      pl.BlockSpec((None, d, tn), lambda i, n, m: (m[0], 0, n))],
            out_specs=[pl.BlockSpec((tm, tn), lambda i, n, m: (i, m[0] * nb + n)),
                       pl.BlockSpec((tm, d), lambda i, n, m: (i, 0))]),
        out_shape=[jax.ShapeDtypeStruct((t, s * ns), F32), jax.ShapeDtypeStruct((t, d), BF16)],
        compiler_params=_params(("arbitrary", "arbitrary")),
    )(me_idx, h, g, wg)


def _norm_in_rest(hn, wg, u, me_idx, name, after=None):
    t, d = hn.shape
    s, _, ns = wg.shape
    tm = 1408 if t % 1408 == 0 else _row_tile(t)
    tn = _col_tile(ns, (1536, 768, 384, 128))
    nb = ns // tn

    def body(m_ref, hn_ref, w_ref, u_in, u_ref):
        del u_in
        u_ref[...] = jnp.dot(hn_ref[...], w_ref[...], preferred_element_type=F32)

    def shard(n, m):
        return (m[0] + 1 + n // nb) % s

    body, more_specs, more = _behind(body, 4, after)
    return pl.pallas_call(
        body, name=name,
        grid_spec=pltpu.PrefetchScalarGridSpec(
            num_scalar_prefetch=1, grid=(t // tm, (s - 1) * nb),
            in_specs=[pl.BlockSpec((tm, d), lambda i, n, m: (i, 0)),
                      pl.BlockSpec((None, d, tn), lambda i, n, m: (shard(n, m), 0, n % nb)),
                      ANY] + more_specs,
            out_specs=pl.BlockSpec((tm, tn), lambda i, n, m: (i, shard(n, m) * nb + n % nb))),
        out_shape=jax.ShapeDtypeStruct(u.shape, u.dtype),
        input_output_aliases={3: 0},
        compiler_params=_params(("arbitrary", "arbitrary")),
    )(me_idx, hn, wg, u, *more)


def _decay_consts(lam):
    z = -lam
    e = jnp.exp(-jnp.abs(z))
    u = 1.0 + e
    log1p_e = jnp.where(u == 1.0, e, jnp.log(u) * (e / (u - 1.0)))
    sp = jnp.maximum(z, 0.0) + log1p_e
    return -LRU_C * sp, LRU_C * _sig(z)


def _gates(xc, wr_ref, br_ref, wi_ref, bi_ref, c8, j, gb):
    sl = slice(j * gb, (j + 1) * gb)
    x16 = xc.astype(BF16)
    r = _sig(jnp.dot(x16, wr_ref[j], preferred_element_type=F32) + br_ref[:, sl])
    ig = _sig(jnp.dot(x16, wi_ref[j], preferred_element_type=F32) + bi_ref[:, sl])
    la = c8[:, sl] * r
    a = jnp.exp(la)
    sq = jnp.sqrt(-jnp.tanh(la) * (a * a + 1.0))
    return r, ig, a, sq


def _mix_fwd(u, wa, ba, wr, br, wi, bi, lam, wb, name, proj=None):
    t = u.shape[0]
    c = u.shape[1] // 6
    tc = MIX_CHUNK
    gb = wr.shape[1]
    nblk = c // gb
    ka, kb = wa.shape[0], wb.shape[0]
    n_proj = 0 if proj is None else 3

    def body(*refs):
        u_ref, wa_ref, ba_ref, wr_ref, br_ref, wi_ref, bi_ref, lam_ref, wb_ref = refs[:9]
        outs = refs[9 + n_proj:]
        y_ref, hs_ref = outs[:2]
        xa_ext, v_ext, xc_s, a_s, b_s, carry_s = outs[-6:]

        @pl.when(pl.program_id(0) == 0)
        def _():
            xa_ext[0:SUBLANES, :] = jnp.zeros((SUBLANES, c), F32)
            v_ext[0:SUBLANES, :] = jnp.zeros((SUBLANES, c), F32)
            carry_s[...] = jnp.zeros_like(carry_s)

        xa_ext[SUBLANES:SUBLANES + tc, :] = u_ref[:, 0:c]
        xc = ba_ref[...]
        for k in range(ka):
            xc = xc + wa_ref[pl.ds(k, 1), :] * xa_ext[pl.ds(SUBLANES - (ka - 1) + k, tc), :]
        xc_s[...] = xc
        c8, _ = _decay_consts(lam_ref[...])
        for j in range(nblk):
            sl = slice(j * gb, (j + 1) * gb)
            xcj = xc_s[:, sl]
            _, ig, a, sq = _gates(xcj, wr_ref, br_ref, wi_ref, bi_ref, c8, j, gb)
            a_s[:, sl] = a
            b_s[:, sl] = sq * (ig * xcj)

        row = lax.broadcasted_iota(jnp.int32, (SUBLANES, c), 0)

        def scan_step(j, _):
            off = pl.multiple_of(j * SUBLANES, SUBLANES)
            av = a_s[pl.ds(off, SUBLANES), :]
            bv = b_s[pl.ds(off, SUBLANES), :]
            for d in (1, 2, 4):
                keep = row >= d
                bsh = jnp.where(keep, pltpu.roll(bv, d, axis=0), 0.0)
                ash = jnp.where(keep, pltpu.roll(av, d, axis=0), 1.0)
                bv = av * bsh + bv
                av = av * ash
            hv = av * carry_s[...] + bv
            hs_ref[pl.ds(off, SUBLANES), :] = hv
            carry_s[...] = hs_ref[pl.ds(off + SUBLANES - 1, 1), :]
            return 0

        lax.fori_loop(0, tc // SUBLANES, scan_step, 0, unroll=SCAN_UNROLL)

        ga = u_ref[:, c:2 * c]
        y_ref[:, 0:c] = (hs_ref[...] * (ga * _sig(ga))).astype(BF16)

        v_ext[SUBLANES:SUBLANES + tc, :] = u_ref[:, 3 * c:4 * c] * u_ref[:, 4 * c:5 * c]
        cv = wb_ref[pl.ds(0, 1), :] * v_ext[pl.ds(SUBLANES - (kb - 1), tc), :]
        for k in range(1, kb):
            cv = cv + wb_ref[pl.ds(k, 1), :] * v_ext[pl.ds(SUBLANES - (kb - 1) + k, tc), :]
        gbv = u_ref[:, 5 * c:6 * c]
        y_ref[:, c:2 * c] = (u_ref[:, 2 * c:3 * c] * cv * (gbv * _sig(gbv))).astype(BF16)

        xa_ext[0:SUBLANES, :] = xa_ext[tc:tc + SUBLANES, :]
        v_ext[0:SUBLANES, :] = v_ext[tc:tc + SUBLANES, :]

        if proj is not None:
            h_ref, wout_ref, g_ref = refs[9:12]
            ho_ref, hn_ref = outs[2:4]
            x = h_ref[...] + jnp.dot(y_ref[...], wout_ref[...], preferred_element_type=F32)
            ho_ref[...] = x
            r = lax.rsqrt(jnp.mean(x * x, axis=-1, keepdims=True) + RMS_EPS)
            hn_ref[...] = ((x * r) * g_ref[...]).astype(BF16)

    full = lambda shape: pl.BlockSpec(shape, lambda i: (0,) * len(shape))
    rows = lambda width: pl.BlockSpec((tc, width), lambda i: (i, 0))
    more_in, more_specs, more_out_specs, more_out = [], [], [], []
    if proj is not None:
        h, wout, g_next = proj
        d = h.shape[1]
        more_in = [h, wout, g_next]
        more_specs = [rows(d), full(wout.shape), full(g_next.shape)]
        more_out_specs = [rows(d), rows(d)]
        more_out = [jax.ShapeDtypeStruct((t, d), F32), jax.ShapeDtypeStruct((t, d), BF16)]
    return pl.pallas_call(
        body, name=name, grid=(t // tc,),
        in_specs=[rows(6 * c), full(wa.shape), full(ba.shape), full(wr.shape), full(br.shape),
                  full(wi.shape), full(bi.shape), full(lam.shape), full(wb.shape)] + more_specs,
        out_specs=[rows(2 * c), rows(c)] + more_out_specs,
        out_shape=[jax.ShapeDtypeStruct((t, 2 * c), BF16), jax.ShapeDtypeStruct((t, c), F32)] + more_out,
        scratch_shapes=[pltpu.VMEM((tc + SUBLANES, c), F32), pltpu.VMEM((tc + SUBLANES, c), F32),
                        pltpu.VMEM((tc, c), F32), pltpu.VMEM((tc, c), F32), pltpu.VMEM((tc, c), F32),
                        pltpu.VMEM((1, c), F32)],
        compiler_params=_params(("arbitrary",)),
    )(u, wa, ba, wr, br, wi, bi, lam, wb, *more_in)


ROW_DWA = 0
ROW_DBA = 4
ROW_DBR = 5
ROW_DBI = 6
ROW_DLAM = 7
ROW_DWB = 8
SMALL_ROWS = 16


def _mix_bwd(u, hs, dy, wa, ba, wr, br, wi, bi, lam, wb, name, after=None):
    t = u.shape[0]
    c = u.shape[1] // 6
    tc = MIX_CHUNK
    nt = t // tc
    gb = wr.shape[1]
    nblk = c // gb
    ka, kb = wa.shape[0], wb.shape[0]
    assert ka <= ROW_DBA and kb <= SMALL_ROWS - ROW_DWB
    hb = tc // SUBLANES

    def body(u_ref, uh_ref, hs_ref, hsh_ref, dy_ref, wa_ref, ba_ref, wr_ref, br_ref, wi_ref, bi_ref, lam_ref, wb_ref,
             du_ref, dsm_ref, dwr_ref, dwi_ref,
             xa_ext, v_ext, hs_ext, a_ext, ds_ext, dxc_ext, dcv_ext, xc_s, r_s, i_s, sq_s, g_s, an_s):
        i = pl.program_id(0)
        chunk = nt - 1 - i
        tail = slice(tc, tc + SUBLANES)
        head = slice(0, SUBLANES)

        @pl.when(i == 0)
        def _():
            zero = jnp.zeros((SUBLANES, c), F32)
            a_ext[tail, :] = zero
            ds_ext[tail, :] = zero
            dxc_ext[tail, :] = zero
            dcv_ext[tail, :] = zero
            dsm_ref[...] = jnp.zeros_like(dsm_ref)
            dwr_ref[...] = jnp.zeros_like(dwr_ref)
            dwi_ref[...] = jnp.zeros_like(dwi_ref)

        prev = jnp.where(chunk > 0, 1.0, 0.0)
        xa_ext[head, :] = uh_ref[:, 0:c] * prev
        xa_ext[SUBLANES:SUBLANES + tc, :] = u_ref[:, 0:c]
        v_ext[head, :] = uh_ref[:, 3 * c:4 * c] * uh_ref[:, 4 * c:5 * c] * prev
        v_ext[SUBLANES:SUBLANES + tc, :] = u_ref[:, 3 * c:4 * c] * u_ref[:, 4 * c:5 * c]
        hs_ext[head, :] = hsh_ref[...] * prev
        hs_ext[SUBLANES:SUBLANES + tc, :] = hs_ref[...]

        xc = ba_ref[...]
        for k in range(ka):
            xc = xc + wa_ref[pl.ds(k, 1), :] * xa_ext[pl.ds(SUBLANES - (ka - 1) + k, tc), :]
        xc_s[...] = xc
        c8, dc8 = _decay_consts(lam_ref[...])
        for j in range(nblk):
            sl = slice(j * gb, (j + 1) * gb)
            r, ig, a, sq = _gates(xc_s[:, sl], wr_ref, br_ref, wi_ref, bi_ref, c8, j, gb)
            r_s[:, sl] = r
            i_s[:, sl] = ig
            sq_s[:, sl] = sq
            a_ext[0:tc, sl] = a

        ga = u_ref[:, c:2 * c]
        sga = _sig(ga)
        g_s[...] = dy_ref[:, 0:c] * (ga * sga)
        an_s[...] = a_ext[pl.ds(1, tc), :]

        row = lax.broadcasted_iota(jnp.int32, (SUBLANES, c), 0)

        def scan_step(j, _):
            off = pl.multiple_of(tc - SUBLANES - j * SUBLANES, SUBLANES)
            av = an_s[pl.ds(off, SUBLANES), :]
            bv = g_s[pl.ds(off, SUBLANES), :]
            for d in (1, 2, 4):
                keep = row < SUBLANES - d
                bsh = jnp.where(keep, pltpu.roll(bv, SUBLANES - d, axis=0), 0.0)
                ash = jnp.where(keep, pltpu.roll(av, SUBLANES - d, axis=0), 1.0)
                bv = av * bsh + bv
                av = av * ash
            ds_ext[pl.ds(off, SUBLANES), :] = av * ds_ext[pl.ds(off + SUBLANES, 1), :] + bv
            return 0

        lax.fori_loop(0, tc // SUBLANES, scan_step, 0, unroll=SCAN_UNROLL)

        def acc(row_index, val):
            dsm_ref[pl.ds(row_index, 1), :] += jnp.sum(val, axis=0, keepdims=True)

        def acc_block(row_index, sl, val):
            dsm_ref[pl.ds(row_index, 1), sl] += jnp.sum(val, axis=0, keepdims=True)

        for j in range(nblk):
            sl = slice(j * gb, (j + 1) * gb)
            ds = ds_ext[0:tc, sl]
            hprev = hs_ext[pl.ds(SUBLANES - 1, tc), sl]
            a = a_ext[0:tc, sl]
            sq = sq_s[:, sl]
            ig = i_s[:, sl]
            r = r_s[:, sl]
            xcj = xc_s[:, sl]
            t1 = ds * xcj
            dla = (ds * hprev) * a - (t1 * ig) * ((a * a) / sq)
            acc_block(ROW_DLAM, sl, dla * r)
            dpr = (dla * c8[:, sl]) * (r * (1.0 - r))
            dpi = (t1 * sq) * (ig * (1.0 - ig))
            acc_block(ROW_DBR, sl, dpr)
            acc_block(ROW_DBI, sl, dpi)
            p16 = dpr.astype(BF16)
            q16 = dpi.astype(BF16)
            x16 = xcj.astype(BF16)
            dwr_ref[j] += lax.dot_general(x16, p16, TN_DIMS, preferred_element_type=F32)
            dwi_ref[j] += lax.dot_general(x16, q16, TN_DIMS, preferred_element_type=F32)
            dxc = (ds * (sq * ig)
                   + lax.dot_general(p16, wr_ref[j], NT_DIMS, preferred_element_type=F32)
                   + lax.dot_general(q16, wi_ref[j], NT_DIMS, preferred_element_type=F32))
            dxc_ext[0:tc, sl] = dxc
            acc_block(ROW_DBA, sl, dxc)

        dsilu_a = sga * (1.0 + ga * (1.0 - sga))
        du_ref[:, c:2 * c] = (dy_ref[:, 0:c] * hs_ref[...] * dsilu_a).astype(BF16)

        dxc = dxc_ext[0:tc, :]
        dxa = wa_ref[pl.ds(ka - 1, 1), :] * dxc
        acc(ROW_DWA + ka - 1, dxc * xa_ext[SUBLANES:SUBLANES + tc, :])
        for k in range(ka - 1):
            acc(ROW_DWA + k, dxc * xa_ext[pl.ds(SUBLANES - (ka - 1) + k, tc), :])
            dxa = dxa + wa_ref[pl.ds(k, 1), :] * dxc_ext[pl.ds(ka - 1 - k, tc), :]
        du_ref[:, 0:c] = dxa.astype(BF16)

        cv = wb_ref[pl.ds(0, 1), :] * v_ext[pl.ds(SUBLANES - (kb - 1), tc), :]
        for k in range(1, kb):
            cv = cv + wb_ref[pl.ds(k, 1), :] * v_ext[pl.ds(SUBLANES - (kb - 1) + k, tc), :]
        gbv = u_ref[:, 5 * c:6 * c]
        sgb = _sig(gbv)
        silu_b = gbv * sgb
        dyb = dy_ref[:, c:2 * c]
        gB = u_ref[:, 2 * c:3 * c]
        du_ref[:, 2 * c:3 * c] = (dyb * cv * silu_b).astype(BF16)
        du_ref[:, 5 * c:6 * c] = (dyb * gB * cv * (sgb * (1.0 + gbv * (1.0 - sgb)))).astype(BF16)
        dcv = dyb * gB * silu_b
        dcv_ext[0:tc, :] = dcv
        dv = wb_ref[pl.ds(kb - 1, 1), :] * dcv
        acc(ROW_DWB + kb - 1, dcv * v_ext[SUBLANES:SUBLANES + tc, :])
        for k in range(kb - 1):
            acc(ROW_DWB + k, dcv * v_ext[pl.ds(SUBLANES - (kb - 1) + k, tc), :])
            dv = dv + wb_ref[pl.ds(k, 1), :] * dcv_ext[pl.ds(kb - 1 - k, tc), :]
        du_ref[:, 3 * c:4 * c] = (dv * u_ref[:, 4 * c:5 * c]).astype(BF16)
        du_ref[:, 4 * c:5 * c] = (dv * u_ref[:, 3 * c:4 * c]).astype(BF16)

        a_ext[tail, :] = a_ext[head, :]
        ds_ext[tail, :] = ds_ext[head, :]
        dxc_ext[tail, :] = dxc_ext[head, :]
        dcv_ext[tail, :] = dcv_ext[head, :]

        @pl.when(i == nt - 1)
        def _():
            dsm_ref[pl.ds(ROW_DLAM, 1), :] = dsm_ref[pl.ds(ROW_DLAM, 1), :] * dc8

    full = lambda shape: pl.BlockSpec(shape, lambda i: (0,) * len(shape))
    rev = lambda i: (nt - 1 - i, 0)
    halo = lambda i: (jnp.maximum((nt - 1 - i) * hb - 1, 0), 0)
    ext = pltpu.VMEM((tc + SUBLANES, c), F32)
    blk = pltpu.VMEM((tc, c), F32)
    body, more_specs, more = _behind(body, 13, after)
    return pl.pallas_call(
        body, name=name, grid=(nt,),
        in_specs=[pl.BlockSpec((tc, 6 * c), rev), pl.BlockSpec((SUBLANES, 6 * c), halo),
                  pl.BlockSpec((tc, c), rev), pl.BlockSpec((SUBLANES, c), halo),
                  pl.BlockSpec((tc, 2 * c), rev),
                  full(wa.shape), full(ba.shape), full(wr.shape), full(br.shape),
                  full(wi.shape), full(bi.shape), full(lam.shape), full(wb.shape)] + more_specs,
        out_specs=[pl.BlockSpec((tc, 6 * c), rev), full((SMALL_ROWS, c)), full(wr.shape), full(wi.shape)],
        out_shape=[jax.ShapeDtypeStruct((t, 6 * c), BF16), jax.ShapeDtypeStruct((SMALL_ROWS, c), F32),
                   jax.ShapeDtypeStruct(wr.shape, F32), jax.ShapeDtypeStruct(wi.shape, F32)],
        scratch_shapes=[ext] * 7 + [blk] * 6,
        compiler_params=_params(("arbitrary",)),
    )(u, u, hs, hs, dy, wa, ba, wr, br, wi, bi, lam, wb, *more)


def _behind(body, n_in, after):
    if after is None:
        return body, [], []
    return (lambda *refs: body(*refs[:n_in], *refs[n_in + 1:])), [ANY], [after]


def _out_proj_norm(h, y, w, g_next, name, after=None):
    t, d = h.shape
    dm = y.shape[1]
    tm = _row_tile(t)

    def body(h_ref, y_ref, w_ref, g_ref, o_ref, hn_ref):
        x = h_ref[...] + jnp.dot(y_ref[...], w_ref[...], preferred_element_type=F32)
        o_ref[...] = x
        r = lax.rsqrt(jnp.mean(x * x, axis=-1, keepdims=True) + RMS_EPS)
        hn_ref[...] = ((x * r) * g_ref[...]).astype(BF16)

    body, more_specs, more = _behind(body, 4, after)
    rows = pl.BlockSpec((tm, d), lambda i: (i, 0))
    return pl.pallas_call(
        body, name=name, grid=(t // tm,),
        in_specs=[rows, pl.BlockSpec((tm, dm), lambda i: (i, 0)), pl.BlockSpec((dm, d), lambda i: (0, 0)),
                  pl.BlockSpec((1, d), lambda i: (0, 0))] + more_specs,
        out_specs=[rows, rows],
        out_shape=[jax.ShapeDtypeStruct((t, d), F32), jax.ShapeDtypeStruct((t, d), BF16)],
        compiler_params=_params(("arbitrary",)),
    )(h, y, w, g_next, *more)


def _in_proj(hn, wg, name, after=None):
    t, d = hn.shape
    s, _, ns = wg.shape
    tm = 1408 if t % 1408 == 0 else _row_tile(t)

    def body(hn_ref, w_ref, u_ref):
        u_ref[...] = jnp.dot(hn_ref[...], w_ref[...], preferred_element_type=F32)

    body, more_specs, more = _behind(body, 2, after)
    return pl.pallas_call(
        body, name=name, grid=(t // tm, s),
        in_specs=[pl.BlockSpec((tm, d), lambda i, n: (i, 0)),
                  pl.BlockSpec((None, d, ns), lambda i, n: (n, 0, 0))] + more_specs,
        out_specs=pl.BlockSpec((tm, ns), lambda i, n: (i, n)),
        out_shape=jax.ShapeDtypeStruct((t, s * ns), F32),
        compiler_params=_params(("arbitrary", "arbitrary")),
    )(hn, wg, *more)


def _out_proj_dw(y, dout, name, after=None):
    t, dm = y.shape
    d = dout.shape[1]
    tmm = _col_tile(dm, (1024, 512, 256))
    tn = _col_tile(d, (512, 256))

    def body(y_ref, g_ref, o_ref):
        o_ref[...] = lax.dot_general(y_ref[...], g_ref[...].astype(BF16), TN_DIMS, preferred_element_type=F32)

    body, more_specs, more = _behind(body, 2, after)
    return pl.pallas_call(
        body, name=name, grid=(d // tn, dm // tmm),
        in_specs=[pl.BlockSpec((t, tmm), lambda n, m: (0, m)),
                  pl.BlockSpec((t, tn), lambda n, m: (0, n))] + more_specs,
        out_specs=pl.BlockSpec((tmm, tn), lambda n, m: (m, n)),
        out_shape=jax.ShapeDtypeStruct((dm, d), F32),
        compiler_params=_params(("arbitrary", "arbitrary")),
    )(y, dout, *more)


def _in_proj_bwd(du, wg, h, g, dout, name, after=None, split=None, w_below=None):
    t, d = h.shape
    s, _, ns = wg.shape
    tm = _row_tile(t)
    tn = _col_tile(d, (1024, 512, 256))

    def mm_body(du_ref, w_ref, o_ref):
        total = lax.dot_general(du_ref[:, 0:ns], w_ref[0], NT_DIMS, preferred_element_type=F32)
        for a in range(1, s):
            total = total + lax.dot_general(du_ref[:, a * ns:(a + 1) * ns], w_ref[a], NT_DIMS,
                                            preferred_element_type=F32)
        o_ref[...] = total

    mm_body, more_specs, more = _behind(mm_body, 2, after)
    dhn = pl.pallas_call(
        mm_body, name=name, grid=(t // tm, d // tn),
        in_specs=[pl.BlockSpec((tm, s * ns), lambda i, n: (i, 0)),
                  pl.BlockSpec((s, tn, ns), lambda i, n: (0, n, 0))] + more_specs,
        out_specs=pl.BlockSpec((tm, tn), lambda i, n: (i, n)),
        out_shape=jax.ShapeDtypeStruct((t, d), F32),
        compiler_params=_params(("arbitrary", "arbitrary")),
    )(du, wg, *more)

    tr = 352 if t % 352 == 0 else 192
    nt = t // tr

    def row_grad(dhn_ref, h_ref, g_ref, dout_ref, dg_ref):
        @pl.when(pl.program_id(0) == 0)
        def _():
            dg_ref[...] = jnp.zeros_like(dg_ref)

        x = h_ref[...]
        dn = dhn_ref[...]
        r = lax.rsqrt(jnp.mean(x * x, axis=-1, keepdims=True) + RMS_EPS)
        gd = dn * g_ref[...]
        dot = jnp.mean(gd * x, axis=-1, keepdims=True)
        dg_ref[...] += jnp.sum(dn * (x * r), axis=0, keepdims=True)
        return dout_ref[...] + (r * gd - x * ((r * r * r) * dot))

    rows = pl.BlockSpec((tr, d), lambda i: (i, 0))
    one = pl.BlockSpec((1, d), lambda i: (0, 0))
    if split is None:
        dm = w_below.shape[0]

        def norm_body(dhn_ref, h_ref, g_ref, dout_ref, w_ref, dh_ref, dg_ref, dy_ref):
            dh = row_grad(dhn_ref, h_ref, g_ref, dout_ref, dg_ref)
            dh_ref[...] = dh
            dy_ref[...] = lax.dot_general(dh.astype(BF16), w_ref[...], NT_DIMS, preferred_element_type=F32)

        return pl.pallas_call(
            norm_body, name=name + "_norm", grid=(nt,),
            in_specs=[rows, rows, one, rows, pl.BlockSpec((dm, d), lambda i: (0, 0))],
            out_specs=[rows, one, pl.BlockSpec((tr, dm), lambda i: (i, 0))],
            out_shape=[jax.ShapeDtypeStruct((t, d), F32), jax.ShapeDtypeStruct((1, d), F32),
                       jax.ShapeDtypeStruct((t, dm), F32)],
            compiler_params=_params(("arbitrary",)),
        )(dhn, h, g, dout, w_below)

    n_head, n_body = split
    n_first = tr - n_head
    n_last = n_head + n_body - (nt - 1) * tr
    assert nt >= 2 and 0 < n_head < tr and 0 < n_last <= tr and n_head % SUBLANES == 0 and n_last % SUBLANES == 0

    def split_body(dhn_ref, h_ref, g_ref, dout_ref, body_ref, head_ref, dg_ref, stage, sems):
        i = pl.program_id(0)
        slot = i % 2

        def first_copy(sl):
            return pltpu.make_async_copy(stage.at[sl, pl.ds(n_head, n_first)], body_ref.at[pl.ds(0, n_first)], sems.at[sl])

        def middle_copy(sl, step):
            start = pl.multiple_of(step * tr - n_head, SUBLANES)
            return pltpu.make_async_copy(stage.at[sl], body_ref.at[pl.ds(start, tr)], sems.at[sl])

        def last_copy(sl):
            return pltpu.make_async_copy(stage.at[sl, pl.ds(0, n_last)],
                                         body_ref.at[pl.ds((nt - 1) * tr - n_head, n_last)], sems.at[sl])

        dh = row_grad(dhn_ref, h_ref, g_ref, dout_ref, dg_ref)

        @pl.when(i == 2)
        def _():
            first_copy(0).wait()

        @pl.when(i > 2)
        def _():
            middle_copy(slot, i - 2).wait()

        stage[slot] = dh

        @pl.when(i == 0)
        def _():
            head_ref[...] = stage[0, 0:n_head, :]
            first_copy(0).start()

        @pl.when((i > 0) & (i < nt - 1))
        def _():
            middle_copy(slot, i).start()

        @pl.when(i == nt - 1)
        def _():
            last = last_copy((nt - 1) % 2)
            last.start()
            if nt == 2:
                first_copy(0).wait()
            else:
                middle_copy((nt - 2) % 2, nt - 2).wait()
            last.wait()

    return pl.pallas_call(
        split_body, name=name + "_norm", grid=(nt,),
        in_specs=[rows, rows, one, rows],
        out_specs=[ANY, pl.BlockSpec((n_head, d), lambda i: (0, 0)), one],
        out_shape=[jax.ShapeDtypeStruct((n_body, d), F32), jax.ShapeDtypeStruct((n_head, d), F32),
                   jax.ShapeDtypeStruct((1, d), F32)],
        scratch_shapes=[pltpu.VMEM((2, tr, d), F32), pltpu.SemaphoreType.DMA((2,))],
        compiler_params=_params(("arbitrary",)),
    )(dhn, h, g, dout)


def _in_proj_dw(hn, du, s, name, after=None):
    t, d = hn.shape
    ns = du.shape[1] // s
    tmm = _col_tile(d, (1024, 512, 256))
    tn = _col_tile(ns, (768, 384, 128))
    nb = ns // tn

    def body(hn_ref, du_ref, o_ref):
        o_ref[...] = lax.dot_general(hn_ref[...], du_ref[...], TN_DIMS, preferred_element_type=F32)

    body, more_specs, more = _behind(body, 2, after)
    return pl.pallas_call(
        body, name=name, grid=(s * nb, d // tmm),
        in_specs=[pl.BlockSpec((t, tmm), lambda n, m: (0, m)),
                  pl.BlockSpec((t, tn), lambda n, m: (0, n))] + more_specs,
        out_specs=pl.BlockSpec((None, tmm, tn), lambda n, m: (n // nb, m, n % nb)),
        out_shape=jax.ShapeDtypeStruct((s, d, ns), F32),
        compiler_params=_params(("arbitrary", "arbitrary")),
    )(hn, du, *more)


def _out_proj_loss(h, y, w, tgt, g, n_meta, t_real, name):
    t, d = h.shape
    dm = y.shape[1]
    tm = 352 if t % 352 == 0 else 192

    def body(h_ref, y_ref, w_ref, t_ref, g_ref, dh_ref, loss_ref, dg_ref, dmix_ref):
        i = pl.program_id(0)

        @pl.when(i == 0)
        def _():
            loss_ref[...] = jnp.zeros_like(loss_ref)
            dg_ref[...] = jnp.zeros_like(dg_ref)

        x = h_ref[...] + jnp.dot(y_ref[...], w_ref[...], preferred_element_type=F32)
        gv = g_ref[...]
        r = lax.rsqrt(jnp.mean(x * x, axis=-1, keepdims=True) + RMS_EPS)
        xr = x * r
        rows = i * tm + lax.broadcasted_iota(jnp.int32, (tm, 1), 0)
        valid = (rows >= n_meta) & (rows < t_real)
        err = jnp.where(valid, xr * gv - t_ref[...], 0.0)
        loss_ref[...] += 0.5 * jnp.sum(jnp.mean(err * err, axis=-1, keepdims=True))
        dy = err * (1.0 / d)
        gd = dy * gv
        dot = jnp.mean(gd * x, axis=-1, keepdims=True)
        dh = r * gd - x * ((r * r * r) * dot)
        dh_ref[...] = dh
        dg_ref[...] += jnp.sum(dy * xr, axis=0, keepdims=True)
        dmix_ref[...] = lax.dot_general(dh.astype(BF16), w_ref[...], NT_DIMS, preferred_element_type=F32)

    rows = pl.BlockSpec((tm, d), lambda i: (i, 0))
    wide = pl.BlockSpec((tm, dm), lambda i: (i, 0))
    return pl.pallas_call(
        body, name=name, grid=(t // tm,),
        in_specs=[rows, wide, pl.BlockSpec((dm, d), lambda i: (0, 0)), rows, pl.BlockSpec((1, d), lambda i: (0, 0))],
        out_specs=[rows, pl.BlockSpec((1, LANES), lambda i: (0, 0)), pl.BlockSpec((1, d), lambda i: (0, 0)), wide],
        out_shape=[jax.ShapeDtypeStruct((t, d), F32), jax.ShapeDtypeStruct((1, LANES), F32),
                   jax.ShapeDtypeStruct((1, d), F32), jax.ShapeDtypeStruct((t, dm), F32)],
        compiler_params=_params(("arbitrary",)),
    )(h, y, w, tgt, g)


def _adamw_rows(rows, cols):
    for cand in (512, 256, 128, 64, 32, 16, 8):
        if rows % cand == 0 and cand * cols * 4 <= 2 * 1024 * 1024:
            return cand
    return rows


def _adamw_math(w_ref, g_ref, m_ref, v_ref, d_ref, nm_ref, nv_ref):
    gv = g_ref[...]
    m2 = ADAM_B1 * m_ref[...] + (1.0 - ADAM_B1) * gv
    v2 = ADAM_B2 * v_ref[...] + (1.0 - ADAM_B2) * (gv * gv)
    m_hat = m2 / (1.0 - ADAM_B1 ** ADAM_STEP)
    v_hat = v2 / (1.0 - ADAM_B2 ** ADAM_STEP)
    d_ref[...] = -ADAM_LR * (m_hat / (jnp.sqrt(v_hat) + ADAM_EPS) + ADAM_WD * w_ref[...])
    nm_ref[...] = m2
    nv_ref[...] = v2


def _adamw(w, g, m, v, name):
    shape = w.shape
    assert len(shape) >= 2 and w.size * 4 <= 2 * 1024 * 1024

    def body(*refs):
        _adamw_math(*refs)

    spec = pl.BlockSpec(shape, lambda i: (0,) * len(shape))
    return pl.pallas_call(
        body, name=name, grid=(1,),
        in_specs=[spec] * 4, out_specs=[spec] * 3,
        out_shape=[jax.ShapeDtypeStruct(shape, F32)] * 3,
        compiler_params=_params(("arbitrary",)),
    )(w, g, m, v)


def _adamw_layer(w, g, m, v, layer, kept, name, after=None):
    nl, rows, cols = w.shape
    tr = _adamw_rows(rows, cols)
    n_kept = 0 if kept is None else 3

    def body(*refs):
        _adamw_math(*refs[:4], *refs[4 + n_kept:])

    body, more_specs, more = _behind(body, 4 + n_kept, after)
    lay = pl.BlockSpec((None, tr, cols), lambda i: (layer, i, 0))
    return pl.pallas_call(
        body, name=name, grid=(rows // tr,),
        in_specs=[lay, pl.BlockSpec((tr, cols), lambda i: (i, 0)), lay, lay] + [ANY] * n_kept + more_specs,
        out_specs=[lay] * 3,
        out_shape=[jax.ShapeDtypeStruct((nl, rows, cols), F32)] * 3,
        input_output_aliases={4 + k: k for k in range(n_kept)},
        compiler_params=_params(("arbitrary",)),
    )(w, g, m, v, *([] if kept is None else kept), *more)


def _pair_add(x, ra, c_idx, name):
    s, _, rows, cols = x.shape
    tr = _slab_rows(rows, cols)

    def body(c_ref, x_ref, r_ref, o_ref):
        o_ref[...] = (x_ref[...] + r_ref[...]).astype(BF16)

    return pl.pallas_call(
        body, name=name,
        grid_spec=pltpu.PrefetchScalarGridSpec(
            num_scalar_prefetch=1, grid=(s, rows // tr),
            in_specs=[pl.BlockSpec((None, None, tr, cols), lambda a, i, c_ref: (a, c_ref[0], i, 0)),
                      pl.BlockSpec((None, tr, cols), lambda a, i, c_ref: (a, i, 0))],
            out_specs=pl.BlockSpec((None, tr, cols), lambda a, i, c_ref: (a, i, 0))),
        out_shape=jax.ShapeDtypeStruct((s, rows, cols), BF16),
        compiler_params=_params(("arbitrary", "arbitrary")),
    )(c_idx, x, ra)


def _chip_sum(rc, p, where, n_slots, name):
    s, rows, cols = rc.shape
    tr = _slab_rows(rows, cols)

    def body(w_ref, x_ref, p_ref, o_ref):
        me = w_ref[0]
        total = jnp.where(me == 0, p_ref[...], x_ref[0]).astype(F32)
        for a in range(1, s):
            total = total + jnp.where(me == a, p_ref[...], x_ref[a]).astype(F32)
        o_ref[...] = total

    return pl.pallas_call(
        body, name=name,
        grid_spec=pltpu.PrefetchScalarGridSpec(
            num_scalar_prefetch=1, grid=(rows // tr,),
            in_specs=[pl.BlockSpec((s, tr, cols), lambda i, w_ref: (0, i, 0)),
                      pl.BlockSpec((None, tr, cols), lambda i, w_ref: (w_ref[0], i, 0))],
            out_specs=pl.BlockSpec((None, tr, cols), lambda i, w_ref: (w_ref[1], i, 0))),
        out_shape=jax.ShapeDtypeStruct((n_slots, rows, cols), F32),
        compiler_params=_params(("arbitrary",)),
    )(where, rc, p)


def _cast_place(w, layer, me_idx, name, after=None):
    _, rows, cols = w.shape
    tr = _slab_rows(rows, cols)

    def body(m_ref, w_ref, o_ref):
        o_ref[...] = w_ref[...].astype(BF16)

    body, more_specs, more = _behind(body, 2, after)
    return pl.pallas_call(
        body, name=name,
        grid_spec=pltpu.PrefetchScalarGridSpec(
            num_scalar_prefetch=1, grid=(rows // tr,),
            in_specs=[pl.BlockSpec((None, tr, cols), lambda i, m_ref: (layer, i, 0))] + more_specs,
            out_specs=pl.BlockSpec((None, tr, cols), lambda i, m_ref: (m_ref[0], i, 0))),
        out_shape=jax.ShapeDtypeStruct((N_CHIPS, rows, cols), BF16),
        compiler_params=_params(("arbitrary",)),
    )(me_idx, w, *more)


def _place():
    x, y, c = lax.axis_index("x"), lax.axis_index("y"), lax.axis_index("c")
    chips = [(1 - x, y), (x, 1 - y), (1 - x, 1 - y)]
    return x, y, c, chips


def _chip_index(cx, cy):
    return 2 * cx + cy


def _gather_copies(bufs, stage):
    x, y, c, chips = _place()
    me = _chip_index(x, y)
    copies = []
    for b in bufs:
        for chip in chips:
            src = _chip_index(*chip)
            if stage == 0:
                copies.append((b.at[me, c], (*chip, c), b.at[src, c]))
            else:
                copies.append((b.at[src, c], (x, y, 1 - c), b.at[src, 1 - c]))
    return copies


def _remote(ref, peer, ssem, rsem, k):
    return pltpu.make_async_remote_copy(src_ref=ref, dst_ref=ref, send_sem=ssem.at[k], recv_sem=rsem.at[k],
                                        device_id=peer, device_id_type=MESH)


def _gather_first(bufs, small):
    n = len(bufs)
    k = 3 * n

    def body(*refs):
        sm_ref = refs[n]
        b_refs, smg_ref = refs[n + 1:2 * n + 1], refs[2 * n + 1]
        lsem, ssem, rsem = refs[2 * n + 2:]
        x, y, c, chips = _place()
        me = _chip_index(x, y)
        local = pltpu.make_async_copy(sm_ref, smg_ref.at[me], lsem)
        local.start()
        first = _gather_copies(b_refs, 0)
        second = _gather_copies(b_refs, 1)
        started = []
        for i, (ref, peer, _) in enumerate(first):
            started.append(_remote(ref, peer, ssem, rsem, i))
        for j, chip in enumerate(chips):
            started.append(pltpu.make_async_remote_copy(
                src_ref=sm_ref, dst_ref=smg_ref.at[me], send_sem=ssem.at[2 * k + j], recv_sem=rsem.at[2 * k + j],
                device_id=(*chip, c), device_id_type=MESH))
        for cp in started:
            cp.start()
        for i, (_, peer, lands) in enumerate(first):
            _remote(lands, peer, ssem, rsem, i).wait_recv()
            ref, sib, _ = second[i]
            fwd = _remote(ref, sib, ssem, rsem, k + i)
            fwd.start()
            started.append(fwd)
        for i, (_, sib, lands) in enumerate(second):
            _remote(lands, sib, ssem, rsem, k + i).wait_recv()
        for j, chip in enumerate(chips):
            theirs = smg_ref.at[_chip_index(*chip)]
            pltpu.make_async_remote_copy(src_ref=theirs, dst_ref=theirs, send_sem=ssem.at[2 * k + j],
                                         recv_sem=rsem.at[2 * k + j], device_id=(*chip, c),
                                         device_id_type=MESH).wait_recv()
        for cp in started:
            cp.wait_send()
        local.wait()

    return pl.pallas_call(
        body, name="gather_first",
        in_specs=[ANY] * (n + 1), out_specs=[ANY] * (n + 1),
        out_shape=[jax.ShapeDtypeStruct(b.shape, b.dtype) for b in bufs]
        + [jax.ShapeDtypeStruct((N_CHIPS,) + small.shape, small.dtype)],
        input_output_aliases={i: i for i in range(n)},
        scratch_shapes=[pltpu.SemaphoreType.DMA, pltpu.SemaphoreType.DMA((2 * k + 3,)),
                        pltpu.SemaphoreType.DMA((2 * k + 3,))],
    )(*bufs, small)


HBM = pl.BlockSpec(memory_space=pltpu.HBM)
SEM = pl.BlockSpec(memory_space=pltpu.SEMAPHORE)
DATAFLOW = pltpu.SideEffectType.DATAFLOW_SIDE_EFFECTING


def _copies_start(bufs, plan, n_copies, name, after=None):
    n = len(bufs)
    extra = [] if after is None else [after]

    def body(*refs):
        refs = refs[:n] + refs[n + len(extra):]
        ssem, rsem = refs[n], refs[n + 1]
        b_refs, token = refs[n + 2:2 * n + 2], refs[2 * n + 2]
        copies = plan(b_refs)
        assert len(copies) == n_copies
        for i, (src, dst, peer, _) in enumerate(copies):
            pltpu.make_async_remote_copy(src_ref=src, dst_ref=dst, send_sem=ssem.at[i], recv_sem=rsem.at[i],
                                         device_id=peer, device_id_type=MESH).start()
        token[...] = jnp.zeros_like(token)

    return pl.pallas_call(
        body, name=name,
        out_shape=(pltpu.SemaphoreType.DMA((n_copies,)), pltpu.SemaphoreType.DMA((n_copies,)),
                   *[pltpu.HBM(b.shape, b.dtype) for b in bufs], jax.ShapeDtypeStruct((SUBLANES, LANES), F32)),
        in_specs=[HBM] * n + [ANY] * len(extra),
        out_specs=(SEM, SEM, *[HBM] * n, pl.BlockSpec(memory_space=pltpu.VMEM)),
        input_output_aliases={i: 2 + i for i in range(n)},
        compiler_params=pltpu.CompilerParams(has_side_effects=DATAFLOW),
    )(*[pltpu.with_memory_space_constraint(b, pltpu.HBM) for b in bufs], *extra)


def _copies_wait(bufs, ssem, rsem, after, plan, name):
    n = len(bufs)
    afters = list(after) if isinstance(after, (list, tuple)) else [after]

    def body(*refs):
        b_refs, ssem_ref, rsem_ref = refs[:n], refs[n], refs[n + 1]
        for i, (src, dst, peer, lands) in enumerate(plan(b_refs)):
            pltpu.make_async_remote_copy(src_ref=src, dst_ref=dst, send_sem=ssem_ref.at[i], recv_sem=rsem_ref.at[i],
                                         device_id=peer, device_id_type=MESH).wait_send()
            pltpu.make_async_remote_copy(src_ref=lands, dst_ref=lands, send_sem=ssem_ref.at[i],
                                         recv_sem=rsem_ref.at[i], device_id=peer, device_id_type=MESH).wait_recv()

    return pl.pallas_call(
        body, name=name,
        out_shape=tuple(pltpu.HBM(b.shape, b.dtype) for b in bufs),
        in_specs=[HBM] * n + [SEM, SEM] + [ANY] * len(afters), out_specs=tuple([HBM] * n),
        input_output_aliases={i: i for i in range(n)},
        compiler_params=pltpu.CompilerParams(has_side_effects=DATAFLOW),
    )(*bufs, ssem, rsem, *afters)


def _gather_plan(stage):
    return lambda refs: [(ref, ref, peer, lands) for ref, peer, lands in _gather_copies(refs, stage)]


def _swap_plan(refs):
    n = len(refs) // 2
    x, y, c, _ = _place()
    return [(refs[a].at[:, 1 - c], refs[n + a], (x, y, 1 - c), refs[n + a]) for a in range(n)]


def _scatter_plan(refs):
    n = len(refs) // 2
    x, y, c, chips = _place()
    me = _chip_index(x, y)
    return [(refs[a].at[_chip_index(*chip)], refs[n + a].at[me], (*chip, c), refs[n + a].at[_chip_index(*chip)])
            for a in range(n) for chip in chips]


def _pair_gather_plan(refs):
    x, y, c, _ = _place()
    return [(r.at[c], r.at[c], (x, y, 1 - c), r.at[1 - c]) for r in refs]


def _pair_swap(xs, name):
    n = len(xs)

    def body(*refs):
        x_refs, o_refs, ssem, rsem = refs[:n], refs[n:2 * n], refs[2 * n], refs[2 * n + 1]
        x, y, c, _ = _place()
        copies = [pltpu.make_async_remote_copy(src_ref=x_refs[a].at[:, 1 - c], dst_ref=o_refs[a],
                                               send_sem=ssem.at[a], recv_sem=rsem.at[a],
                                               device_id=(x, y, 1 - c), device_id_type=MESH) for a in range(n)]
        for cp in copies:
            cp.start()
        for cp in copies:
            cp.wait()

    return pl.pallas_call(
        body, name=name, in_specs=[ANY] * n, out_specs=[ANY] * n,
        out_shape=[jax.ShapeDtypeStruct((a.shape[0],) + a.shape[2:], a.dtype) for a in xs],
        scratch_shapes=[pltpu.SemaphoreType.DMA((n,)), pltpu.SemaphoreType.DMA((n,))],
    )(*xs)


def _chip_scatter(ps):
    n = len(ps)

    def body(*refs):
        p_refs, o_refs, ssem, rsem = refs[:n], refs[n:2 * n], refs[2 * n], refs[2 * n + 1]
        x, y, c, chips = _place()
        me = _chip_index(x, y)
        sends = []
        for a in range(n):
            for j, chip in enumerate(chips):
                sends.append(pltpu.make_async_remote_copy(
                    src_ref=p_refs[a].at[_chip_index(*chip)], dst_ref=o_refs[a].at[me],
                    send_sem=ssem.at[3 * a + j], recv_sem=rsem.at[3 * a + j],
                    device_id=(*chip, c), device_id_type=MESH))
        for cp in sends:
            cp.start()
        for a in range(n):
            for j, chip in enumerate(chips):
                src = _chip_index(*chip)
                pltpu.make_async_remote_copy(
                    src_ref=p_refs[a].at[src], dst_ref=o_refs[a].at[src],
                    send_sem=ssem.at[3 * a + j], recv_sem=rsem.at[3 * a + j],
                    device_id=(*chip, c), device_id_type=MESH).wait_recv()
        for cp in sends:
            cp.wait_send()

    return pl.pallas_call(
        body, name="chip_scatter", in_specs=[ANY] * n, out_specs=[ANY] * n,
        out_shape=[jax.ShapeDtypeStruct(a.shape, a.dtype) for a in ps],
        scratch_shapes=[pltpu.SemaphoreType.DMA((3 * n,)), pltpu.SemaphoreType.DMA((3 * n,))],
    )(*ps)


def _final_gather(fs, rep):
    n = len(fs)

    def body(*refs):
        o_refs, repo_ref = refs[n + 1:2 * n + 1], refs[2 * n + 1]
        ssem, rsem = refs[2 * n + 2:]
        x, y, c, chips = _place()
        slot = 4 * x + 2 * y + c
        copies = [pltpu.make_async_remote_copy(src_ref=o_refs[a].at[c], dst_ref=o_refs[a].at[c],
                                               send_sem=ssem.at[a], recv_sem=rsem.at[a],
                                               device_id=(x, y, 1 - c), device_id_type=MESH) for a in range(n)]
        peers = [(x, y, 1 - c)] + [(*chip, c) for chip in chips] + [(*chip, 1 - c) for chip in chips]
        for k, peer in enumerate(peers):
            copies.append(pltpu.make_async_remote_copy(src_ref=repo_ref.at[slot], dst_ref=repo_ref.at[slot],
                                                       send_sem=ssem.at[n + k], recv_sem=rsem.at[n + k],
                                                       device_id=peer, device_id_type=MESH))
        for cp in copies:
            cp.start()
        for a in range(n):
            pltpu.make_async_remote_copy(src_ref=o_refs[a].at[1 - c], dst_ref=o_refs[a].at[1 - c],
                                         send_sem=ssem.at[a], recv_sem=rsem.at[a],
                                         device_id=(x, y, 1 - c), device_id_type=MESH).wait_recv()
        for k, peer in enumerate(peers):
            px, py, pc = peer
            theirs = repo_ref.at[4 * px + 2 * py + pc]
            pltpu.make_async_remote_copy(src_ref=theirs, dst_ref=theirs, send_sem=ssem.at[n + k], recv_sem=rsem.at[n + k],
                                         device_id=peer, device_id_type=MESH).wait_recv()
        for cp in copies:
            cp.wait_send()

    return pl.pallas_call(
        body, name="final_gather", in_specs=[ANY] * (n + 1), out_specs=[ANY] * (n + 1),
        out_shape=[jax.ShapeDtypeStruct(a.shape, a.dtype) for a in fs] + [jax.ShapeDtypeStruct(rep.shape, rep.dtype)],
        input_output_aliases={k: k for k in range(n + 1)},
        scratch_shapes=[pltpu.SemaphoreType.DMA((n + 7,)), pltpu.SemaphoreType.DMA((n + 7,))],
    )(*fs, rep)


def _block_diag(w, gb):
    nh, hd, _ = w.shape
    per = gb // hd
    w4 = w.reshape(nh // per, per, hd, hd)
    eye = jnp.eye(per, dtype=w.dtype)
    return jnp.einsum("jaik,ab->jaibk", w4, eye).reshape(nh // per, gb, gb)


def _diag_blocks(dense, hd):
    nj, gb, _ = dense.shape
    per = gb // hd
    d5 = dense.reshape(nj, per, hd, per, hd)
    return jnp.stack([d5[:, a, :, a, :] for a in range(per)], axis=1).reshape(nj * per, hd, hd)


def _round_up(n, q):
    return (n + q - 1) // q * q


def kernel(x, meta, norm_g, w_in, conv_a_w, conv_a_b, lru_wr, lru_br, lru_wi, lru_bi, lru_lambda, conv_b_w, w_out, final_g, loss_target, m_meta, m_norm_g, m_w_in, m_conv_a_w, m_conv_a_b, m_lru_wr, m_lru_br, m_lru_wi, m_lru_bi, m_lru_lambda, m_conv_b_w, m_w_out, m_final_g, v_meta, v_norm_g, v_w_in, v_conv_a_w, v_conv_a_b, v_lru_wr, v_lru_br, v_lru_wi, v_lru_bi, v_lru_lambda, v_conv_b_w, v_w_out, v_final_g):
    weights = dict(meta=meta, norm_g=norm_g, w_in=w_in, conv_a_w=conv_a_w, conv_a_b=conv_a_b, lru_wr=lru_wr,
                   lru_br=lru_br, lru_wi=lru_wi, lru_bi=lru_bi, lru_lambda=lru_lambda, conv_b_w=conv_b_w,
                   w_out=w_out, final_g=final_g)
    mom1 = dict(meta=m_meta, norm_g=m_norm_g, w_in=m_w_in, conv_a_w=m_conv_a_w, conv_a_b=m_conv_a_b,
                lru_wr=m_lru_wr, lru_br=m_lru_br, lru_wi=m_lru_wi, lru_bi=m_lru_bi, lru_lambda=m_lru_lambda,
                conv_b_w=m_conv_b_w, w_out=m_w_out, final_g=m_final_g)
    mom2 = dict(meta=v_meta, norm_g=v_norm_g, w_in=v_w_in, conv_a_w=v_conv_a_w, conv_a_b=v_conv_a_b,
                lru_wr=v_lru_wr, lru_br=v_lru_br, lru_wi=v_lru_wi, lru_bi=v_lru_bi, lru_lambda=v_lru_lambda,
                conv_b_w=v_conv_b_w, w_out=v_w_out, final_g=v_final_g)
    names = list(weights)

    assert x.shape[0] == 1
    seq, d = x.shape[1], x.shape[2]
    n_meta, ds = meta.shape
    depth = norm_g.shape[0]
    c = lru_lambda.shape[1]
    nh, hd = lru_wr.shape[1], lru_wr.shape[2]
    ns = w_in.shape[2]
    dms = w_out.shape[1]
    cs = conv_a_w.shape[2]
    ka, kb = conv_a_w.shape[1], conv_b_w.shape[1]
    s = N_CHIPS
    assert depth == N_CORES and d == s * ds and c == s * cs and s * ns == 6 * c and s * dms == 2 * c
    gb = min(GATE_BLOCK, c)
    t_real = n_meta + seq
    t = _round_up(t_real, ROW_QUANTUM)
    my_c = lax.axis_index("c").astype(jnp.int32)
    my_chip = (2 * lax.axis_index("x") + lax.axis_index("y")).astype(jnp.int32)
    c_idx = my_c.reshape(1)
    chip_idx = my_chip.reshape(1)

    sm_rows = _round_up(n_meta + depth * SUBLANES, 2 * SUBLANES)
    small = jnp.zeros((sm_rows, ds), F32)
    small = small.at[0:n_meta, :].set(meta)
    for l in range(depth):
        base = n_meta + l * SUBLANES
        small = small.at[base:base + ka, 0:cs].set(conv_a_w[l])
        small = small.at[base + ka:base + ka + kb, 0:cs].set(conv_b_w[l])
    (small_g,) = _gather_first([], small)
    meta_full = jnp.transpose(small_g[:, 0:n_meta, :], (1, 0, 2)).reshape(n_meta, d)
    wa_full, wb_full = [], []
    for l in range(depth):
        base = n_meta + l * SUBLANES
        wa_full.append(jnp.transpose(small_g[:, base:base + ka, 0:cs], (1, 0, 2)).reshape(ka, c))
        wb_full.append(jnp.transpose(small_g[:, base + ka:base + ka + kb, 0:cs], (1, 0, 2)).reshape(kb, c))
    win0 = _cast_place(w_in, 0, chip_idx, "cast_w_in_0").reshape(s, 2, d // 2, ns)
    ssem_w, rsem_w, win0, token_w = _copies_start([win0], _gather_plan(0), 3, "gather_win0_ici_start", after=small_g)
    win_b = [None] + [_cast_place(w_in, l, chip_idx, f"cast_w_in_{l}", after=token_w).reshape(s, 2, d // 2, ns)
                      for l in range(1, depth)]
    wout_b = [_cast_place(w_out, l, chip_idx, f"cast_w_out_{l}", after=token_w).reshape(s, 2, dms // 2, d)
              for l in range(depth)]
    h = jnp.concatenate([meta_full, x[0], jnp.zeros((t - t_real, d), F32)], axis=0) + token_w[0, 0]
    tgt = jnp.concatenate([jnp.zeros((n_meta, d), F32), loss_target[0], jnp.zeros((t - t_real, d), F32)],
                          axis=0) + token_w[0, 0]
    u_own, hn_own = _norm_in_own(h, norm_g[0].reshape(1, d), win0.reshape(s, d, ns), chip_idx, "norm_in_0_own")
    (win0,) = _copies_wait([win0], ssem_w, rsem_w, [u_own, tgt] + win_b[1:] + wout_b, _gather_plan(0),
                           "gather_win0_ici_wait")
    ssem_w, rsem_w, win0, token_w = _copies_start([win0], _gather_plan(1), 3, "gather_win0_d2d_start")
    def travel(buf, stage, tag, after):
        return _copies_start([buf], _gather_plan(stage), 3, f"gather_{tag}_{'d2d' if stage else 'ici'}_start",
                             after=after)

    def arrived(state, stage, tag, after):
        (buf,) = _copies_wait([state[2]], state[0], state[1], after, _gather_plan(stage),
                              f"gather_{tag}_{'d2d' if stage else 'ici'}_wait")
        return buf

    on_wout0 = travel(wout_b[0], 0, "wout0", token_w)
    on_win1 = travel(win_b[1], 0, "win1", on_wout0[3])
    on_wout1 = travel(wout_b[1], 0, "wout1", on_win1[3])
    token = on_wout1[3]
    (win_b[0],) = _copies_wait([win0], ssem_w, rsem_w, token, _gather_plan(1), "gather_win0_d2d_wait")

    layer_w = []
    for l in range(depth):
        layer_w.append(dict(
            g=norm_g[l].reshape(1, d), wa=wa_full[l], ba=conv_a_b[l].reshape(1, c),
            wr=_block_diag(lru_wr[l], gb).astype(BF16), br=lru_br[l].reshape(1, c),
            wi=_block_diag(lru_wi[l], gb).astype(BF16), bi=lru_bi[l].reshape(1, c),
            lam=lru_lambda[l].reshape(1, c), wb=wb_full[l]))
    saved = []
    for l, lw in enumerate(layer_w):
        first = l == 0
        lw["win"] = win_b[l].reshape(s, d, ns)
        mixer_w = (lw["wa"], lw["ba"], lw["wr"], lw["br"], lw["wi"], lw["bi"], lw["lam"], lw["wb"])
        if first:
            u = _norm_in_rest(hn_own, lw["win"], u_own, chip_idx, "norm_in_0_rest", after=token)
            hn = hn_own
            on_wout0 = travel(arrived(on_wout0, 0, "wout0", u), 1, "wout0", None)
            wout_b[0] = arrived(on_wout0, 1, "wout0", on_wout0[3])
            lw["wout"] = wout_b[0].reshape(2 * c, d)
            y, hs, h_next, hn_next = _mix_fwd(u, *mixer_w, f"mix_fwd_{l}", proj=(h, lw["wout"], layer_w[1]["g"]))
            saved.append((h, u, hn, y, hs))
            h = h_next
            on_win1 = travel(arrived(on_win1, 0, "win1", y), 1, "win1", None)
            win_b[1] = arrived(on_win1, 1, "win1", on_win1[3])
            on_wout1 = travel(arrived(on_wout1, 0, "wout1", y), 1, "wout1", on_win1[3])
            token = on_wout1[3]
        else:
            hn = hn_next
            u = _in_proj(hn, lw["win"], f"norm_in_{l}", after=token)
            wout_b[1] = arrived(on_wout1, 1, "wout1", u)
            lw["wout"] = wout_b[1].reshape(2 * c, d)
            y, hs = _mix_fwd(u, *mixer_w, f"mix_fwd_{l}")
            saved.append((h, u, hn, y, hs))
            dh, loss_lanes, d_final_g, dy = _out_proj_loss(h, y, lw["wout"], tgt, final_g.reshape(1, d), n_meta,
                                                           t_real, f"out_proj_{l}_loss")
    loss = lax.psum(loss_lanes[0, 0], ("x", "y", "c"))

    to_core = jnp.stack([my_chip, my_c])
    grads = [None] * depth
    early = None
    for l in reversed(range(depth)):
        lw = layer_w[l]
        h_in, u, hn, y, hs = saved[l]
        token = early[-1] if early else None
        d_wout = _out_proj_dw(y, dh, f"out_proj_dw_{l}", after=token)
        if early:
            ssem, rsem, bufs, _ = early
            bufs = _copies_wait(bufs, ssem, rsem, d_wout, _swap_plan, "early_swap_wait")
            half = len(bufs) // 2
            sums = [_pair_add(a, b, c_idx, f"early_pair_add_{k}") for k, (a, b) in enumerate(zip(bufs[:half], bufs[half:]))]
            lands = [lax.empty(p.shape, p.dtype) for p in sums]
            ssem, rsem, *bufs, token = _copies_start(sums + lands, _scatter_plan, 3 * half, "early_scatter_start")
        du, dsm, d_wr, d_wi = _mix_bwd(u, hs, dy, lw["wa"], lw["ba"], lw["wr"], lw["br"], lw["wi"], lw["bi"],
                                       lw["lam"], lw["wb"], f"mix_bwd_{l}", after=token)
        if early:
            bufs = _copies_wait(bufs, ssem, rsem, du, _scatter_plan, "early_scatter_wait")
            halves = [_chip_sum(rc, p, to_core, N_CORES, f"early_chip_sum_{k}")
                      for k, (p, rc) in enumerate(zip(bufs[:half], bufs[half:]))]
            ssem, rsem, *bufs, token = _copies_start(halves, _pair_gather_plan, half, "early_gather_start")
        d_win = _in_proj_dw(hn, du, s, f"in_proj_dw_{l}", after=token)
        srcs = [d_win.reshape(s, 2, d // 2, ns), d_wout.reshape(s, 2, dms // 2, d)]
        if early:
            early_full = _copies_wait(bufs, ssem, rsem, d_win, _pair_gather_plan, "early_gather_wait")
            lands = [lax.empty((a.shape[0],) + a.shape[2:], a.dtype) for a in srcs]
            ssem, rsem, *bufs, token = _copies_start(srcs + lands, _swap_plan, len(srcs), "late_swap_start")
            last = depth - 1
            early_grad = dict(w_in=early_full[0].reshape(d, ns), w_out=early_full[1].reshape(dms, d))
            early_step = {n: _adamw_layer(weights[n], early_grad[n], mom1[n], mom2[n], last, None,
                                          f"adamw_{n}_{last}", after=token) for n in ("w_in", "w_out")}
            bufs = _copies_wait(bufs, ssem, rsem, [o[0] for o in early_step.values()], _swap_plan, "late_swap_wait")
            late_sums = [_pair_add(a, b, c_idx, f"pair_add_{k}")
                         for k, (a, b) in enumerate(zip(bufs[:len(srcs)], bufs[len(srcs):]))]
            lands = [lax.empty(p.shape, p.dtype) for p in late_sums]
            ssem, rsem, *bufs, token = _copies_start(late_sums + lands, _scatter_plan, 3 * len(srcs), "late_scatter_start")
        if l > 0:
            dh, d_g, dy = _in_proj_bwd(du, lw["win"], h_in, lw["g"], dh, f"in_proj_bwd_{l}", after=token,
                                       w_below=layer_w[l - 1]["wout"])
        else:
            grad_x, d_meta, d_g = _in_proj_bwd(du, lw["win"], h_in, lw["g"], dh, f"in_proj_bwd_{l}", after=token,
                                               split=(n_meta, seq))
        if early:
            bufs = _copies_wait(bufs, ssem, rsem, grad_x, _scatter_plan, "late_scatter_wait")
            late_reduced = [_chip_sum(rc, p, to_core, N_CORES, f"chip_sum_{k}")
                            for k, (p, rc) in enumerate(zip(bufs[:len(srcs)], bufs[len(srcs):]))]
        grads[l] = dict(dsm=dsm, wr=_diag_blocks(d_wr, hd), wi=_diag_blocks(d_wi, hd), g=d_g)
        if l == depth - 1:
            lands = [lax.empty((a.shape[0],) + a.shape[2:], a.dtype) for a in srcs]
            ssem, rsem, *bufs, token = _copies_start(srcs + lands, _swap_plan, len(srcs), "early_swap_start")
            early = (ssem, rsem, bufs, token)
        else:
            early = None
    grad_x = grad_x[None]

    sharded = []
    sp = jnp.zeros((sm_rows, s, ds), F32)
    sp = sp.at[0:n_meta].set(d_meta.reshape(n_meta, s, ds))
    for l in range(depth):
        base = n_meta + l * SUBLANES
        dsm = grads[l]["dsm"]
        sp = sp.at[base:base + ka, :, 0:cs].set(dsm[ROW_DWA:ROW_DWA + ka].reshape(ka, s, cs))
        sp = sp.at[base + ka:base + ka + kb, :, 0:cs].set(dsm[ROW_DWB:ROW_DWB + kb].reshape(kb, s, cs))
    sharded.append(jnp.transpose(sp, (1, 0, 2)).reshape(s, 2, sm_rows // 2, ds))
    rep_parts = [jnp.concatenate([grads[l]["g"].reshape(-1) for l in range(depth)]), d_final_g.reshape(-1)]
    for row in (ROW_DBA, ROW_DBR, ROW_DBI, ROW_DLAM):
        rep_parts.append(jnp.concatenate([grads[l]["dsm"][row] for l in range(depth)]))
    rep_parts.append(jnp.concatenate([grads[l]["wr"].reshape(-1) for l in range(depth)]))
    rep_parts.append(jnp.concatenate([grads[l]["wi"].reshape(-1) for l in range(depth)]))
    rep_sizes = [p.shape[0] for p in rep_parts]
    piece = _round_up(-(-sum(rep_sizes) // (s * 2)), 2 * SUBLANES * LANES)
    flat = jnp.concatenate(rep_parts + [jnp.zeros((s * 2 * piece - sum(rep_sizes),), F32)])
    sharded.append(flat.reshape(s, 2, piece // LANES, LANES))

    from_sibling = _pair_swap(sharded, "small_pair_swap")
    pair_sums = [_pair_add(a, b, c_idx, f"small_pair_add_{k}") for k, (a, b) in enumerate(zip(sharded, from_sibling))]
    by_chip = _chip_scatter(pair_sums)
    to_device = jnp.stack([my_chip, 2 * my_chip + my_c])
    reduced_sp = _chip_sum(by_chip[0], pair_sums[0], to_core, N_CORES, "small_chip_sum")
    reduced_rep = _chip_sum(by_chip[1], pair_sums[1], to_device, N_CHIPS * N_CORES, "chip_sum_rep")
    *full, rep_all = _final_gather(late_reduced + [reduced_sp], reduced_rep)

    g_win = [full[0].reshape(d, ns), early_full[0].reshape(d, ns)]
    g_wout = [full[1].reshape(dms, d), early_full[1].reshape(dms, d)]
    g_sp = full[2].reshape(sm_rows, ds)
    rep_flat = rep_all.reshape(-1)
    rep_out, off = [], 0
    for n in rep_sizes:
        rep_out.append(rep_flat[off:off + n])
        off += n
    grad = dict(
        meta=g_sp[0:n_meta],
        norm_g=rep_out[0].reshape(depth, d),
        w_in=jnp.stack(g_win),
        conv_a_w=jnp.stack([g_sp[n_meta + l * SUBLANES:n_meta + l * SUBLANES + ka, 0:cs] for l in range(depth)]),
        conv_a_b=rep_out[2].reshape(depth, c),
        lru_wr=rep_out[6].reshape(depth, nh, hd, hd),
        lru_br=rep_out[3].reshape(depth, c),
        lru_wi=rep_out[7].reshape(depth, nh, hd, hd),
        lru_bi=rep_out[4].reshape(depth, c),
        lru_lambda=rep_out[5].reshape(depth, c),
        conv_b_w=jnp.stack([g_sp[n_meta + l * SUBLANES + ka:n_meta + l * SUBLANES + ka + kb, 0:cs]
                            for l in range(depth)]),
        w_out=jnp.stack(g_wout),
        final_g=rep_out[1].reshape(d),
    )

    delta, new_m, new_v = {}, {}, {}
    for n, g_first in (("w_in", g_win[0]), ("w_out", g_wout[0])):
        delta[n], new_m[n], new_v[n] = _adamw_layer(weights[n], g_first, mom1[n], mom2[n], 0, early_step[n],
                                                    f"adamw_{n}_0")
    for n in names:
        if n in delta:
            continue
        shape = weights[n].shape
        as_block = shape if len(shape) > 1 else (1,) + shape
        out = _adamw(weights[n].reshape(as_block), grad[n].reshape(as_block), mom1[n].reshape(as_block),
                     mom2[n].reshape(as_block), f"adamw_{n}")
        delta[n], new_m[n], new_v[n] = (o.reshape(shape) for o in out)

    return (loss, grad_x, *[grad[n] for n in names], *[delta[n] for n in names],
            *[new_m[n] for n in names], *[new_v[n] for n in names])
```

```python
import jax
import jax.numpy as jnp
from jax import lax
from jax.experimental import pallas as pl
from jax.experimental.pallas import tpu as pltpu

F32 = jnp.float32
BF16 = jnp.bfloat16

RMS_EPS = 1e-6
LRU_C = 8.0
ADAM_LR = 0.001
ADAM_B1 = 0.9
ADAM_B2 = 0.999
ADAM_EPS = 1e-08
ADAM_WD = 0.01
ADAM_STEP = 10

N_CHIPS = 4
N_CORES = 2
VMEM_LIMIT_BYTES = 56 * 1024 * 1024
SUBLANES = 8
LANES = 128
ROW_QUANTUM = 384
MIX_CHUNK = 192
SCAN_UNROLL = 4
GATE_BLOCK = 256
MESH = pl.DeviceIdType.MESH
ANY = pl.BlockSpec(memory_space=pl.ANY)

NT_DIMS = (((1,), (1,)), ((), ()))
TN_DIMS = (((0,), (0,)), ((), ()))


def _params(sem):
    return pltpu.CompilerParams(dimension_semantics=sem, vmem_limit_bytes=VMEM_LIMIT_BYTES)


def _sig(x):
    return 0.5 * jnp.tanh(0.5 * x) + 0.5


def _row_tile(t):
    return 704 if t % 704 == 0 else 192


def _col_tile(n, prefs):
    for p in prefs:
        if n % p == 0:
            return p
    return n


def _slab_rows(rows, cols):
    if rows * cols * 4 <= 1024 * 1024:
        return rows
    return _col_tile(rows, (256, 128, 64, 32, 16))


def _norm_in_own(h, g, wg, me_idx, name):
    t, d = h.shape
    s, _, ns = wg.shape
    tm = 1408 if t % 1408 == 0 else _row_tile(t)
    tn = _col_tile(ns, (768, 384, 128))
    nb = ns // tn

    def body(m_ref, h_ref, g_ref, w_ref, u_ref, hn_ref):
        @pl.when(pl.program_id(1) == 0)
        def _():
            x = h_ref[...]
            r = lax.rsqrt(jnp.mean(x * x, axis=-1, keepdims=True) + RMS_EPS)
            hn_ref[...] = ((x * r) * g_ref[...]).astype(BF16)

        u_ref[...] = jnp.dot(hn_ref[...], w_ref[...], preferred_element_type=F32)

    return pl.pallas_call(
        body, name=name,
        grid_spec=pltpu.PrefetchScalarGridSpec(
            num_scalar_prefetch=1, grid=(t // tm, nb),
            in_specs=[pl.BlockSpec((tm, d), lambda i, n, m: (i, 0)),
                      pl.BlockSpec((1, d), lambda i, n, m: (0, 0)),
                      pl.BlockSpec((None, d, tn), lambda i, n, m: (m[0], 0, n))],
            out_specs=[pl.BlockSpec((tm, tn), lambda i, n, m: (i, m[0] * nb + n)),
                       pl.BlockSpec((tm, d), lambda i, n, m: (i, 0))]),
        out_shape=[jax.ShapeDtypeStruct((t, s * ns), F32), jax.ShapeDtypeStruct((t, d), BF16)],
        compiler_params=_params(("arbitrary", "arbitrary")),
    )(me_idx, h, g, wg)


def _norm_in_rest(hn, wg, u, me_idx, name, after=None):
    t, d = hn.shape
    s, _, ns = wg.shape
    tm = 1408 if t % 1408 == 0 else _row_tile(t)
    tn = _col_tile(ns, (1536, 768, 384, 128))
    nb = ns // tn

    def body(m_ref, hn_ref, w_ref, u_in, u_ref):
        del u_in
        u_ref[...] = jnp.dot(hn_ref[...], w_ref[...], preferred_element_type=F32)

    def shard(n, m):
        return (m[0] + 1 + n // nb) % s

    body, more_specs, more = _behind(body, 4, after)
    return pl.pallas_call(
        body, name=name,
        grid_spec=pltpu.PrefetchScalarGridSpec(
            num_scalar_prefetch=1, grid=(t // tm, (s - 1) * nb),
            in_specs=[pl.BlockSpec((tm, d), lambda i, n, m: (i, 0)),
                      pl.BlockSpec((None, d, tn), lambda i, n, m: (shard(n, m), 0, n % nb)),
                      ANY] + more_specs,
            out_specs=pl.BlockSpec((tm, tn), lambda i, n, m: (i, shard(n, m) * nb + n % nb))),
        out_shape=jax.ShapeDtypeStruct(u.shape, u.dtype),
        input_output_aliases={3: 0},
        compiler_params=_params(("arbitrary", "arbitrary")),
    )(me_idx, hn, wg, u, *more)


def _decay_consts(lam):
    z = -lam
    e = jnp.exp(-jnp.abs(z))
    u = 1.0 + e
    log1p_e = jnp.where(u == 1.0, e, jnp.log(u) * (e / (u - 1.0)))
    sp = jnp.maximum(z, 0.0) + log1p_e
    return -LRU_C * sp, LRU_C * _sig(z)


def _gates(xc, wr_ref, br_ref, wi_ref, bi_ref, c8, j, gb):
    sl = slice(j * gb, (j + 1) * gb)
    x16 = xc.astype(BF16)
    r = _sig(jnp.dot(x16, wr_ref[j], preferred_element_type=F32) + br_ref[:, sl])
    ig = _sig(jnp.dot(x16, wi_ref[j], preferred_element_type=F32) + bi_ref[:, sl])
    la = c8[:, sl] * r
    a = jnp.exp(la)
    sq = jnp.sqrt(-jnp.tanh(la) * (a * a + 1.0))
    return r, ig, a, sq


def _mix_fwd(u, wa, ba, wr, br, wi, bi, lam, wb, name, proj=None):
    t = u.shape[0]
    c = u.shape[1] // 6
    tc = MIX_CHUNK
    gb = wr.shape[1]
    nblk = c // gb
    ka, kb = wa.shape[0], wb.shape[0]
    n_proj = 0 if proj is None else 3

    def body(*refs):
        u_ref, wa_ref, ba_ref, wr_ref, br_ref, wi_ref, bi_ref, lam_ref, wb_ref = refs[:9]
        outs = refs[9 + n_proj:]
        y_ref, hs_ref = outs[:2]
        xa_ext, v_ext, xc_s, a_s, b_s, carry_s = outs[-6:]

        @pl.when(pl.program_id(0) == 0)
        def _():
            xa_ext[0:SUBLANES, :] = jnp.zeros((SUBLANES, c), F32)
            v_ext[0:SUBLANES, :] = jnp.zeros((SUBLANES, c), F32)
            carry_s[...] = jnp.zeros_like(carry_s)

        xa_ext[SUBLANES:SUBLANES + tc, :] = u_ref[:, 0:c]
        xc = ba_ref[...]
        for k in range(ka):
            xc = xc + wa_ref[pl.ds(k, 1), :] * xa_ext[pl.ds(SUBLANES - (ka - 1) + k, tc), :]
        xc_s[...] = xc
        c8, _ = _decay_consts(lam_ref[...])
        for j in range(nblk):
            sl = slice(j * gb, (j + 1) * gb)
            xcj = xc_s[:, sl]
            _, ig, a, sq = _gates(xcj, wr_ref, br_ref, wi_ref, bi_ref, c8, j, gb)
            a_s[:, sl] = a
            b_s[:, sl] = sq * (ig * xcj)

        row = lax.broadcasted_iota(jnp.int32, (SUBLANES, c), 0)

        def scan_step(j, _):
            off = pl.multiple_of(j * SUBLANES, SUBLANES)
            av = a_s[pl.ds(off, SUBLANES), :]
            bv = b_s[pl.ds(off, SUBLANES), :]
            for d in (1, 2, 4):
                keep = row >= d
                bsh = jnp.where(keep, pltpu.roll(bv, d, axis=0), 0.0)
                ash = jnp.where(keep, pltpu.roll(av, d, axis=0), 1.0)
                bv = av * bsh + bv
                av = av * ash
            hv = av * carry_s[...] + bv
            hs_ref[pl.ds(off, SUBLANES), :] = hv
            carry_s[...] = hs_ref[pl.ds(off + SUBLANES - 1, 1), :]
            return 0

        lax.fori_loop(0, tc // SUBLANES, scan_step, 0, unroll=SCAN_UNROLL)

        ga = u_ref[:, c:2 * c]
        y_ref[:, 0:c] = (hs_ref[...] * (ga * _sig(ga))).astype(BF16)

        v_ext[SUBLANES:SUBLANES + tc, :] = u_ref[:, 3 * c:4 * c] * u_ref[:, 4 * c:5 * c]
        cv = wb_ref[pl.ds(0, 1), :] * v_ext[pl.ds(SUBLANES - (kb - 1), tc), :]
        for k in range(1, kb):
            cv = cv + wb_ref[pl.ds(k, 1), :] * v_ext[pl.ds(SUBLANES - (kb - 1) + k, tc), :]
        gbv = u_ref[:, 5 * c:6 * c]
        y_ref[:, c:2 * c] = (u_ref[:, 2 * c:3 * c] * cv * (gbv * _sig(gbv))).astype(BF16)

        xa_ext[0:SUBLANES, :] = xa_ext[tc:tc + SUBLANES, :]
        v_ext[0:SUBLANES, :] = v_ext[tc:tc + SUBLANES, :]

        if proj is not None:
            h_ref, wout_ref, g_ref = refs[9:12]
            ho_ref, hn_ref = outs[2:4]
            x = h_ref[...] + jnp.dot(y_ref[...], wout_ref[...], preferred_element_type=F32)
            ho_ref[...] = x
            r = lax.rsqrt(jnp.mean(x * x, axis=-1, keepdims=True) + RMS_EPS)
            hn_ref[...] = ((x * r) * g_ref[...]).astype(BF16)

    full = lambda shape: pl.BlockSpec(shape, lambda i: (0,) * len(shape))
    rows = lambda width: pl.BlockSpec((tc, width), lambda i: (i, 0))
    more_in, more_specs, more_out_specs, more_out = [], [], [], []
    if proj is not None:
        h, wout, g_next = proj
        d = h.shape[1]
        more_in = [h, wout, g_next]
        more_specs = [rows(d), full(wout.shape), full(g_next.shape)]
        more_out_specs = [rows(d), rows(d)]
        more_out = [jax.ShapeDtypeStruct((t, d), F32), jax.ShapeDtypeStruct((t, d), BF16)]
    return pl.pallas_call(
        body, name=name, grid=(t // tc,),
        in_specs=[rows(6 * c), full(wa.shape), full(ba.shape), full(wr.shape), full(br.shape),
                  full(wi.shape), full(bi.shape), full(lam.shape), full(wb.shape)] + more_specs,
        out_specs=[rows(2 * c), rows(c)] + more_out_specs,
        out_shape=[jax.ShapeDtypeStruct((t, 2 * c), BF16), jax.ShapeDtypeStruct((t, c), F32)] + more_out,
        scratch_shapes=[pltpu.VMEM((tc + SUBLANES, c), F32), pltpu.VMEM((tc + SUBLANES, c), F32),
                        pltpu.VMEM((tc, c), F32), pltpu.VMEM((tc, c), F32), pltpu.VMEM((tc, c), F32),
                        pltpu.VMEM((1, c), F32)],
        compiler_params=_params(("arbitrary",)),
    )(u, wa, ba, wr, br, wi, bi, lam, wb, *more_in)


ROW_DWA = 0
ROW_DBA = 4
ROW_DBR = 5
ROW_DBI = 6
ROW_DLAM = 7
ROW_DWB = 8
SMALL_ROWS = 16


def _mix_bwd(u, hs, dy, wa, ba, wr, br, wi, bi, lam, wb, name, after=None):
    t = u.shape[0]
    c = u.shape[1] // 6
    tc = MIX_CHUNK
    nt = t // tc
    gb = wr.shape[1]
    nblk = c // gb
    ka, kb = wa.shape[0], wb.shape[0]
    assert ka <= ROW_DBA and kb <= SMALL_ROWS - ROW_DWB
    hb = tc // SUBLANES

    def body(u_ref, uh_ref, hs_ref, hsh_ref, dy_ref, wa_ref, ba_ref, wr_ref, br_ref, wi_ref, bi_ref, lam_ref, wb_ref,
             du_ref, dsm_ref, dwr_ref, dwi_ref,
             xa_ext, v_ext, hs_ext, a_ext, ds_ext, dxc_ext, dcv_ext, xc_s, r_s, i_s, sq_s, g_s, an_s):
        i = pl.program_id(0)
        chunk = nt - 1 - i
        tail = slice(tc, tc + SUBLANES)
        head = slice(0, SUBLANES)

        @pl.when(i == 0)
        def _():
            zero = jnp.zeros((SUBLANES, c), F32)
            a_ext[tail, :] = zero
            ds_ext[tail, :] = zero
            dxc_ext[tail, :] = zero
            dcv_ext[tail, :] = zero
            dsm_ref[...] = jnp.zeros_like(dsm_ref)
            dwr_ref[...] = jnp.zeros_like(dwr_ref)
            dwi_ref[...] = jnp.zeros_like(dwi_ref)

        prev = jnp.where(chunk > 0, 1.0, 0.0)
        xa_ext[head, :] = uh_ref[:, 0:c] * prev
        xa_ext[SUBLANES:SUBLANES + tc, :] = u_ref[:, 0:c]
        v_ext[head, :] = uh_ref[:, 3 * c:4 * c] * uh_ref[:, 4 * c:5 * c] * prev
        v_ext[SUBLANES:SUBLANES + tc, :] = u_ref[:, 3 * c:4 * c] * u_ref[:, 4 * c:5 * c]
        hs_ext[head, :] = hsh_ref[...] * prev
        hs_ext[SUBLANES:SUBLANES + tc, :] = hs_ref[...]

        xc = ba_ref[...]
        for k in range(ka):
            xc = xc + wa_ref[pl.ds(k, 1), :] * xa_ext[pl.ds(SUBLANES - (ka - 1) + k, tc), :]
        xc_s[...] = xc
        c8, dc8 = _decay_consts(lam_ref[...])
        for j in range(nblk):
            sl = slice(j * gb, (j + 1) * gb)
            r, ig, a, sq = _gates(xc_s[:, sl], wr_ref, br_ref, wi_ref, bi_ref, c8, j, gb)
            r_s[:, sl] = r
            i_s[:, sl] = ig
            sq_s[:, sl] = sq
            a_ext[0:tc, sl] = a

        ga = u_ref[:, c:2 * c]
        sga = _sig(ga)
        g_s[...] = dy_ref[:, 0:c] * (ga * sga)
        an_s[...] = a_ext[pl.ds(1, tc), :]

        row = lax.broadcasted_iota(jnp.int32, (SUBLANES, c), 0)

        def scan_step(j, _):
            off = pl.multiple_of(tc - SUBLANES - j * SUBLANES, SUBLANES)
            av = an_s[pl.ds(off, SUBLANES), :]
            bv = g_s[pl.ds(off, SUBLANES), :]
            for d in (1, 2, 4):
                keep = row < SUBLANES - d
                bsh = jnp.where(keep, pltpu.roll(bv, SUBLANES - d, axis=0), 0.0)
                ash = jnp.where(keep, pltpu.roll(av, SUBLANES - d, axis=0), 1.0)
                bv = av * bsh + bv
                av = av * ash
            ds_ext[pl.ds(off, SUBLANES), :] = av * ds_ext[pl.ds(off + SUBLANES, 1), :] + bv
            return 0

        lax.fori_loop(0, tc // SUBLANES, scan_step, 0, unroll=SCAN_UNROLL)

        def acc(row_index, val):
            dsm_ref[pl.ds(row_index, 1), :] += jnp.sum(val, axis=0, keepdims=True)

        def acc_block(row_index, sl, val):
            dsm_ref[pl.ds(row_index, 1), sl] += jnp.sum(val, axis=0, keepdims=True)

        for j in range(nblk):
            sl = slice(j * gb, (j + 1) * gb)
            ds = ds_ext[0:tc, sl]
            hprev = hs_ext[pl.ds(SUBLANES - 1, tc), sl]
            a = a_ext[0:tc, sl]
            sq = sq_s[:, sl]
            ig = i_s[:, sl]
            r = r_s[:, sl]
            xcj = xc_s[:, sl]
            t1 = ds * xcj
            dla = (ds * hprev) * a - (t1 * ig) * ((a * a) / sq)
            acc_block(ROW_DLAM, sl, dla * r)
            dpr = (dla * c8[:, sl]) * (r * (1.0 - r))
            dpi = (t1 * sq) * (ig * (1.0 - ig))
            acc_block(ROW_DBR, sl, dpr)
            acc_block(ROW_DBI, sl, dpi)
            p16 = dpr.astype(BF16)
            q16 = dpi.astype(BF16)
            x16 = xcj.astype(BF16)
            dwr_ref[j] += lax.dot_general(x16, p16, TN_DIMS, preferred_element_type=F32)
            dwi_ref[j] += lax.dot_general(x16, q16, TN_DIMS, preferred_element_type=F32)
            dxc = (ds * (sq * ig)
                   + lax.dot_general(p16, wr_ref[j], NT_DIMS, preferred_element_type=F32)
                   + lax.dot_general(q16, wi_ref[j], NT_DIMS, preferred_element_type=F32))
            dxc_ext[0:tc, sl] = dxc
            acc_block(ROW_DBA, sl, dxc)

        dsilu_a = sga * (1.0 + ga * (1.0 - sga))
        du_ref[:, c:2 * c] = (dy_ref[:, 0:c] * hs_ref[...] * dsilu_a).astype(BF16)

        dxc = dxc_ext[0:tc, :]
        dxa = wa_ref[pl.ds(ka - 1, 1), :] * dxc
        acc(ROW_DWA + ka - 1, dxc * xa_ext[SUBLANES:SUBLANES + tc, :])
        for k in range(ka - 1):
            acc(ROW_DWA + k, dxc * xa_ext[pl.ds(SUBLANES - (ka - 1) + k, tc), :])
            dxa = dxa + wa_ref[pl.ds(k, 1), :] * dxc_ext[pl.ds(ka - 1 - k, tc), :]
        du_ref[:, 0:c] = dxa.astype(BF16)

        cv = wb_ref[pl.ds(0, 1), :] * v_ext[pl.ds(SUBLANES - (kb - 1), tc), :]
        for k in range(1, kb):
            cv = cv + wb_ref[pl.ds(k, 1), :] * v_ext[pl.ds(SUBLANES - (kb - 1) + k, tc), :]
        gbv = u_ref[:, 5 * c:6 * c]
        sgb = _sig(gbv)
        silu_b = gbv * sgb
        dyb = dy_ref[:, c:2 * c]
        gB = u_ref[:, 2 * c:3 * c]
        du_ref[:, 2 * c:3 * c] = (dyb * cv * silu_b).astype(BF16)
        du_ref[:, 5 * c:6 * c] = (dyb * gB * cv * (sgb * (1.0 + gbv * (1.0 - sgb)))).astype(BF16)
        dcv = dyb * gB * silu_b
        dcv_ext[0:tc, :] = dcv
        dv = wb_ref[pl.ds(kb - 1, 1), :] * dcv
        acc(ROW_DWB + kb - 1, dcv * v_ext[SUBLANES:SUBLANES + tc, :])
        for k in range(kb - 1):
            acc(ROW_DWB + k, dcv * v_ext[pl.ds(SUBLANES - (kb - 1) + k, tc), :])
            dv = dv + wb_ref[pl.ds(k, 1), :] * dcv_ext[pl.ds(kb - 1 - k, tc), :]
        du_ref[:, 3 * c:4 * c] = (dv * u_ref[:, 4 * c:5 * c]).astype(BF16)
        du_ref[:, 4 * c:5 * c] = (dv * u_ref[:, 3 * c:4 * c]).astype(BF16)

        a_ext[tail, :] = a_ext[head, :]
        ds_ext[tail, :] = ds_ext[head, :]
        dxc_ext[tail, :] = dxc_ext[head, :]
        dcv_ext[tail, :] = dcv_ext[head, :]

        @pl.when(i == nt - 1)
        def _():
            dsm_ref[pl.ds(ROW_DLAM, 1), :] = dsm_ref[pl.ds(ROW_DLAM, 1), :] * dc8

    full = lambda shape: pl.BlockSpec(shape, lambda i: (0,) * len(shape))
    rev = lambda i: (nt - 1 - i, 0)
    halo = lambda i: (jnp.maximum((nt - 1 - i) * hb - 1, 0), 0)
    ext = pltpu.VMEM((tc + SUBLANES, c), F32)
    blk = pltpu.VMEM((tc, c), F32)
    body, more_specs, more = _behind(body, 13, after)
    return pl.pallas_call(
        body, name=name, grid=(nt,),
        in_specs=[pl.BlockSpec((tc, 6 * c), rev), pl.BlockSpec((SUBLANES, 6 * c), halo),
                  pl.BlockSpec((tc, c), rev), pl.BlockSpec((SUBLANES, c), halo),
                  pl.BlockSpec((tc, 2 * c), rev),
                  full(wa.shape), full(ba.shape), full(wr.shape), full(br.shape),
                  full(wi.shape), full(bi.shape), full(lam.shape), full(wb.shape)] + more_specs,
        out_specs=[pl.BlockSpec((tc, 6 * c), rev), full((SMALL_ROWS, c)), full(wr.shape), full(wi.shape)],
        out_shape=[jax.ShapeDtypeStruct((t, 6 * c), BF16), jax.ShapeDtypeStruct((SMALL_ROWS, c), F32),
                   jax.ShapeDtypeStruct(wr.shape, F32), jax.ShapeDtypeStruct(wi.shape, F32)],
        scratch_shapes=[ext] * 7 + [blk] * 6,
        compiler_params=_params(("arbitrary",)),
    )(u, u, hs, hs, dy, wa, ba, wr, br, wi, bi, lam, wb, *more)


def _behind(body, n_in, after):
    if after is None:
        return body, [], []
    return (lambda *refs: body(*refs[:n_in], *refs[n_in + 1:])), [ANY], [after]


def _in_proj(hn, wg, name, after=None):
    t, d = hn.shape
    s, _, ns = wg.shape
    tm = 1408 if t % 1408 == 0 else _row_tile(t)

    def body(hn_ref, w_ref, u_ref):
        u_ref[...] = jnp.dot(hn_ref[...], w_ref[...], preferred_element_type=F32)

    body, more_specs, more = _behind(body, 2, after)
    return pl.pallas_call(
        body, name=name, grid=(t // tm, s),
        in_specs=[pl.BlockSpec((tm, d), lambda i, n: (i, 0)),
                  pl.BlockSpec((None, d, ns), lambda i, n: (n, 0, 0))] + more_specs,
        out_specs=pl.BlockSpec((tm, ns), lambda i, n: (i, n)),
        out_shape=jax.ShapeDtypeStruct((t, s * ns), F32),
        compiler_params=_params(("arbitrary", "arbitrary")),
    )(hn, wg, *more)


def _out_proj_dw(y, dout, name, after=None):
    t, dm = y.shape
    d = dout.shape[1]
    tmm = _col_tile(dm, (1024, 512, 256))
    tn = _col_tile(d, (512, 256))

    def body(y_ref, g_ref, o_ref):
        o_ref[...] = lax.dot_general(y_ref[...], g_ref[...].astype(BF16), TN_DIMS, preferred_element_type=F32)

    body, more_specs, more = _behind(body, 2, after)
    return pl.pallas_call(
        body, name=name, grid=(d // tn, dm // tmm),
        in_specs=[pl.BlockSpec((t, tmm), lambda n, m: (0, m)),
                  pl.BlockSpec((t, tn), lambda n, m: (0, n))] + more_specs,
        out_specs=pl.BlockSpec((tmm, tn), lambda n, m: (m, n)),
        out_shape=jax.ShapeDtypeStruct((dm, d), F32),
        compiler_params=_params(("arbitrary", "arbitrary")),
    )(y, dout, *more)


def _in_proj_bwd(du, wg, h, g, dout, name, after=None, split=None, w_below=None):
    t, d = h.shape
    s, _, ns = wg.shape
    tm = _row_tile(t)
    tn = _col_tile(d, (1024, 512, 256))

    def mm_body(du_ref, w_ref, o_ref):
        total = lax.dot_general(du_ref[:, 0:ns], w_ref[0], NT_DIMS, preferred_element_type=F32)
        for a in range(1, s):
            total = total + lax.dot_general(du_ref[:, a * ns:(a + 1) * ns], w_ref[a], NT_DIMS,
                                            preferred_element_type=F32)
        o_ref[...] = total

    mm_body, more_specs, more = _behind(mm_body, 2, after)
    dhn = pl.pallas_call(
        mm_body, name=name, grid=(t // tm, d // tn),
        in_specs=[pl.BlockSpec((tm, s * ns), lambda i, n: (i, 0)),
                  pl.BlockSpec((s, tn, ns), lambda i, n: (0, n, 0))] + more_specs,
        out_specs=pl.BlockSpec((tm, tn), lambda i, n: (i, n)),
        out_shape=jax.ShapeDtypeStruct((t, d), F32),
        compiler_params=_params(("arbitrary", "arbitrary")),
    )(du, wg, *more)

    tr = 352 if t % 352 == 0 else 192
    nt = t // tr

    def row_grad(dhn_ref, h_ref, g_ref, dout_ref, dg_ref):
        @pl.when(pl.program_id(0) == 0)
        def _():
            dg_ref[...] = jnp.zeros_like(dg_ref)

        x = h_ref[...]
        dn = dhn_ref[...]
        r = lax.rsqrt(jnp.mean(x * x, axis=-1, keepdims=True) + RMS_EPS)
        gd = dn * g_ref[...]
        dot = jnp.mean(gd * x, axis=-1, keepdims=True)
        dg_ref[...] += jnp.sum(dn * (x * r), axis=0, keepdims=True)
        return dout_ref[...] + (r * gd - x * ((r * r * r) * dot))

    rows = pl.BlockSpec((tr, d), lambda i: (i, 0))
    one = pl.BlockSpec((1, d), lambda i: (0, 0))
    if split is None:
        dm = w_below.shape[0]

        def norm_body(dhn_ref, h_ref, g_ref, dout_ref, w_ref, dh_ref, dg_ref, dy_ref):
            dh = row_grad(dhn_ref, h_ref, g_ref, dout_ref, dg_ref)
            dh_ref[...] = dh
            dy_ref[...] = lax.dot_general(dh.astype(BF16), w_ref[...], NT_DIMS, preferred_element_type=F32)

        return pl.pallas_call(
            norm_body, name=name + "_norm", grid=(nt,),
            in_specs=[rows, rows, one, rows, pl.BlockSpec((dm, d), lambda i: (0, 0))],
            out_specs=[rows, one, pl.BlockSpec((tr, dm), lambda i: (i, 0))],
            out_shape=[jax.ShapeDtypeStruct((t, d), F32), jax.ShapeDtypeStruct((1, d), F32),
                       jax.ShapeDtypeStruct((t, dm), F32)],
            compiler_params=_params(("arbitrary",)),
        )(dhn, h, g, dout, w_below)

    n_head, n_body = split
    n_first = tr - n_head
    n_last = n_head + n_body - (nt - 1) * tr
    assert nt >= 2 and 0 < n_head < tr and 0 < n_last <= tr and n_head % SUBLANES == 0 and n_last % SUBLANES == 0

    def split_body(dhn_ref, h_ref, g_ref, dout_ref, body_ref, head_ref, dg_ref, stage, sems):
        i = pl.program_id(0)
        slot = i % 2

        def first_copy(sl):
            return pltpu.make_async_copy(stage.at[sl, pl.ds(n_head, n_first)], body_ref.at[pl.ds(0, n_first)], sems.at[sl])

        def middle_copy(sl, step):
            start = pl.multiple_of(step * tr - n_head, SUBLANES)
            return pltpu.make_async_copy(stage.at[sl], body_ref.at[pl.ds(start, tr)], sems.at[sl])

        def last_copy(sl):
            return pltpu.make_async_copy(stage.at[sl, pl.ds(0, n_last)],
                                         body_ref.at[pl.ds((nt - 1) * tr - n_head, n_last)], sems.at[sl])

        dh = row_grad(dhn_ref, h_ref, g_ref, dout_ref, dg_ref)

        @pl.when(i == 2)
        def _():
            first_copy(0).wait()

        @pl.when(i > 2)
        def _():
            middle_copy(slot, i - 2).wait()

        stage[slot] = dh

        @pl.when(i == 0)
        def _():
            head_ref[...] = stage[0, 0:n_head, :]
            first_copy(0).start()

        @pl.when((i > 0) & (i < nt - 1))
        def _():
            middle_copy(slot, i).start()

        @pl.when(i == nt - 1)
        def _():
            last = last_copy((nt - 1) % 2)
            last.start()
            if nt == 2:
                first_copy(0).wait()
            else:
                middle_copy((nt - 2) % 2, nt - 2).wait()
            last.wait()

    return pl.pallas_call(
        split_body, name=name + "_norm", grid=(nt,),
        in_specs=[rows, rows, one, rows],
        out_specs=[ANY, pl.BlockSpec((n_head, d), lambda i: (0, 0)), one],
        out_shape=[jax.ShapeDtypeStruct((n_body, d), F32), jax.ShapeDtypeStruct((n_head, d), F32),
                   jax.ShapeDtypeStruct((1, d), F32)],
        scratch_shapes=[pltpu.VMEM((2, tr, d), F32), pltpu.SemaphoreType.DMA((2,))],
        compiler_params=_params(("arbitrary",)),
    )(dhn, h, g, dout)


def _in_proj_dw(hn, du, s, name, after=None):
    t, d = hn.shape
    ns = du.shape[1] // s
    tmm = _col_tile(d, (1024, 512, 256))
    tn = _col_tile(ns, (768, 384, 128))
    nb = ns // tn

    def body(hn_ref, du_ref, o_ref):
        o_ref[...] = lax.dot_general(hn_ref[...], du_ref[...], TN_DIMS, preferred_element_type=F32)

    body, more_specs, more = _behind(body, 2, after)
    return pl.pallas_call(
        body, name=name, grid=(d // tmm, s * nb),
        in_specs=[pl.BlockSpec((t, tmm), lambda m, n: (0, m)),
                  pl.BlockSpec((t, tn), lambda m, n: (0, n))] + more_specs,
        out_specs=pl.BlockSpec((None, tmm, tn), lambda m, n: (n // nb, m, n % nb)),
        out_shape=jax.ShapeDtypeStruct((s, d, ns), F32),
        compiler_params=_params(("arbitrary", "arbitrary")),
    )(hn, du, *more)


def _out_proj_loss(h, y, w, tgt, g, n_meta, t_real, name):
    t, d = h.shape
    dm = y.shape[1]
    tm = 352 if t % 352 == 0 else 192

    def body(h_ref, y_ref, w_ref, t_ref, g_ref, dh_ref, loss_ref, dg_ref, dmix_ref):
        i = pl.program_id(0)

        @pl.when(i == 0)
        def _():
            loss_ref[...] = jnp.zeros_like(loss_ref)
            dg_ref[...] = jnp.zeros_like(dg_ref)

        x = h_ref[...] + jnp.dot(y_ref[...], w_ref[...], preferred_element_type=F32)
        gv = g_ref[...]
        r = lax.rsqrt(jnp.mean(x * x, axis=-1, keepdims=True) + RMS_EPS)
        xr = x * r
        rows = i * tm + lax.broadcasted_iota(jnp.int32, (tm, 1), 0)
        valid = (rows >= n_meta) & (rows < t_real)
        err = jnp.where(valid, xr * gv - t_ref[...], 0.0)
        loss_ref[...] += 0.5 * jnp.sum(jnp.mean(err * err, axis=-1, keepdims=True))
        dy = err * (1.0 / d)
        gd = dy * gv
        dot = jnp.mean(gd * x, axis=-1, keepdims=True)
        dh = r * gd - x * ((r * r * r) * dot)
        dh_ref[...] = dh
        dg_ref[...] += jnp.sum(dy * xr, axis=0, keepdims=True)
        dmix_ref[...] = lax.dot_general(dh.astype(BF16), w_ref[...], NT_DIMS, preferred_element_type=F32)

    rows = pl.BlockSpec((tm, d), lambda i: (i, 0))
    wide = pl.BlockSpec((tm, dm), lambda i: (i, 0))
    return pl.pallas_call(
        body, name=name, grid=(t // tm,),
        in_specs=[rows, wide, pl.BlockSpec((dm, d), lambda i: (0, 0)), rows, pl.BlockSpec((1, d), lambda i: (0, 0))],
        out_specs=[rows, pl.BlockSpec((1, LANES), lambda i: (0, 0)), pl.BlockSpec((1, d), lambda i: (0, 0)), wide],
        out_shape=[jax.ShapeDtypeStruct((t, d), F32), jax.ShapeDtypeStruct((1, LANES), F32),
                   jax.ShapeDtypeStruct((1, d), F32), jax.ShapeDtypeStruct((t, dm), F32)],
        compiler_params=_params(("arbitrary",)),
    )(h, y, w, tgt, g)


def _adamw_rows(rows, cols):
    for cand in (512, 256, 128, 64, 32, 16, 8):
        if rows % cand == 0 and cand * cols * 4 <= 2 * 1024 * 1024:
            return cand
    return rows


def _adamw_math(w_ref, g_ref, m_ref, v_ref, d_ref, nm_ref, nv_ref):
    gv = g_ref[...]
    m2 = ADAM_B1 * m_ref[...] + (1.0 - ADAM_B1) * gv
    v2 = ADAM_B2 * v_ref[...] + (1.0 - ADAM_B2) * (gv * gv)
    m_hat = m2 / (1.0 - ADAM_B1 ** ADAM_STEP)
    v_hat = v2 / (1.0 - ADAM_B2 ** ADAM_STEP)
    d_ref[...] = -ADAM_LR * (m_hat / (jnp.sqrt(v_hat) + ADAM_EPS) + ADAM_WD * w_ref[...])
    nm_ref[...] = m2
    nv_ref[...] = v2


def _adamw(w, g, m, v, name):
    shape = w.shape
    assert len(shape) >= 2 and w.size * 4 <= 2 * 1024 * 1024

    def body(*refs):
        _adamw_math(*refs)

    spec = pl.BlockSpec(shape, lambda i: (0,) * len(shape))
    return pl.pallas_call(
        body, name=name, grid=(1,),
        in_specs=[spec] * 4, out_specs=[spec] * 3,
        out_shape=[jax.ShapeDtypeStruct(shape, F32)] * 3,
        compiler_params=_params(("arbitrary",)),
    )(w, g, m, v)


def _adamw_layer(w, g, m, v, layer, kept, name, after=None):
    nl, rows, cols = w.shape
    tr = _adamw_rows(rows, cols)
    n_kept = 0 if kept is None else 3

    def body(*refs):
        _adamw_math(*refs[:4], *refs[4 + n_kept:])

    body, more_specs, more = _behind(body, 4 + n_kept, after)
    lay = pl.BlockSpec((None, tr, cols), lambda i: (layer, i, 0))
    return pl.pallas_call(
        body, name=name, grid=(rows // tr,),
        in_specs=[lay, pl.BlockSpec((tr, cols), lambda i: (i, 0)), lay, lay] + [ANY] * n_kept + more_specs,
        out_specs=[lay] * 3,
        out_shape=[jax.ShapeDtypeStruct((nl, rows, cols), F32)] * 3,
        input_output_aliases={4 + k: k for k in range(n_kept)},
        compiler_params=_params(("arbitrary",)),
    )(w, g, m, v, *([] if kept is None else kept), *more)


def _pair_add(x, ra, c_idx, name):
    s, _, rows, cols = x.shape
    tr = _slab_rows(rows, cols)

    def body(c_ref, x_ref, r_ref, o_ref):
        o_ref[...] = (x_ref[...] + r_ref[...]).astype(BF16)

    return pl.pallas_call(
        body, name=name,
        grid_spec=pltpu.PrefetchScalarGridSpec(
            num_scalar_prefetch=1, grid=(s, rows // tr),
            in_specs=[pl.BlockSpec((None, None, tr, cols), lambda a, i, c_ref: (a, c_ref[0], i, 0)),
                      pl.BlockSpec((None, tr, cols), lambda a, i, c_ref: (a, i, 0))],
            out_specs=pl.BlockSpec((None, tr, cols), lambda a, i, c_ref: (a, i, 0))),
        out_shape=jax.ShapeDtypeStruct((s, rows, cols), BF16),
        compiler_params=_params(("arbitrary", "arbitrary")),
    )(c_idx, x, ra)


def _chip_sum(rc, p, where, n_slots, name):
    s, rows, cols = rc.shape
    tr = _slab_rows(rows, cols)

    def body(w_ref, x_ref, p_ref, o_ref):
        me = w_ref[0]
        total = jnp.where(me == 0, p_ref[...], x_ref[0]).astype(F32)
        for a in range(1, s):
            total = total + jnp.where(me == a, p_ref[...], x_ref[a]).astype(F32)
        o_ref[...] = total

    return pl.pallas_call(
        body, name=name,
        grid_spec=pltpu.PrefetchScalarGridSpec(
            num_scalar_prefetch=1, grid=(rows // tr,),
            in_specs=[pl.BlockSpec((s, tr, cols), lambda i, w_ref: (0, i, 0)),
                      pl.BlockSpec((None, tr, cols), lambda i, w_ref: (w_ref[0], i, 0))],
            out_specs=pl.BlockSpec((None, tr, cols), lambda i, w_ref: (w_ref[1], i, 0))),
        out_shape=jax.ShapeDtypeStruct((n_slots, rows, cols), F32),
        compiler_params=_params(("arbitrary",)),
    )(where, rc, p)


def _cast_place(w, layer, me_idx, name, after=None):
    _, rows, cols = w.shape
    tr = _slab_rows(rows, cols)

    def body(m_ref, w_ref, o_ref):
        o_ref[...] = w_ref[...].astype(BF16)

    body, more_specs, more = _behind(body, 2, after)
    return pl.pallas_call(
        body, name=name,
        grid_spec=pltpu.PrefetchScalarGridSpec(
            num_scalar_prefetch=1, grid=(rows // tr,),
            in_specs=[pl.BlockSpec((None, tr, cols), lambda i, m_ref: (layer, i, 0))] + more_specs,
            out_specs=pl.BlockSpec((None, tr, cols), lambda i, m_ref: (m_ref[0], i, 0))),
        out_shape=jax.ShapeDtypeStruct((N_CHIPS, rows, cols), BF16),
        compiler_params=_params(("arbitrary",)),
    )(me_idx, w, *more)


def _place():
    x, y, c = lax.axis_index("x"), lax.axis_index("y"), lax.axis_index("c")
    chips = [(1 - x, y), (x, 1 - y), (1 - x, 1 - y)]
    return x, y, c, chips


def _chip_index(cx, cy):
    return 2 * cx + cy


def _gather_copies(bufs, stage):
    x, y, c, chips = _place()
    me = _chip_index(x, y)
    copies = []
    for b in bufs:
        for chip in chips:
            src = _chip_index(*chip)
            if stage == 0:
                copies.append((b.at[me, c], (*chip, c), b.at[src, c]))
            else:
                copies.append((b.at[src, c], (x, y, 1 - c), b.at[src, 1 - c]))
    return copies


def _remote(ref, peer, ssem, rsem, k):
    return pltpu.make_async_remote_copy(src_ref=ref, dst_ref=ref, send_sem=ssem.at[k], recv_sem=rsem.at[k],
                                        device_id=peer, device_id_type=MESH)


def _gather_first(bufs, small):
    n = len(bufs)
    k = 3 * n

    def body(*refs):
        sm_ref = refs[n]
        b_refs, smg_ref = refs[n + 1:2 * n + 1], refs[2 * n + 1]
        lsem, ssem, rsem = refs[2 * n + 2:]
        x, y, c, chips = _place()
        me = _chip_index(x, y)
        local = pltpu.make_async_copy(sm_ref, smg_ref.at[me], lsem)
        local.start()
        first = _gather_copies(b_refs, 0)
        second = _gather_copies(b_refs, 1)
        started = []
        for i, (ref, peer, _) in enumerate(first):
            started.append(_remote(ref, peer, ssem, rsem, i))
        for j, chip in enumerate(chips):
            started.append(pltpu.make_async_remote_copy(
                src_ref=sm_ref, dst_ref=smg_ref.at[me], send_sem=ssem.at[2 * k + j], recv_sem=rsem.at[2 * k + j],
                device_id=(*chip, c), device_id_type=MESH))
        for cp in started:
            cp.start()
        for i, (_, peer, lands) in enumerate(first):
            _remote(lands, peer, ssem, rsem, i).wait_recv()
            ref, sib, _ = second[i]
            fwd = _remote(ref, sib, ssem, rsem, k + i)
            fwd.start()
            started.append(fwd)
        for i, (_, sib, lands) in enumerate(second):
            _remote(lands, sib, ssem, rsem, k + i).wait_recv()
        for j, chip in enumerate(chips):
            theirs = smg_ref.at[_chip_index(*chip)]
            pltpu.make_async_remote_copy(src_ref=theirs, dst_ref=theirs, send_sem=ssem.at[2 * k + j],
                                         recv_sem=rsem.at[2 * k + j], device_id=(*chip, c),
                                         device_id_type=MESH).wait_recv()
        for cp in started:
            cp.wait_send()
        local.wait()

    return pl.pallas_call(
        body, name="gather_first",
        in_specs=[ANY] * (n + 1), out_specs=[ANY] * (n + 1),
        out_shape=[jax.ShapeDtypeStruct(b.shape, b.dtype) for b in bufs]
        + [jax.ShapeDtypeStruct((N_CHIPS,) + small.shape, small.dtype)],
        input_output_aliases={i: i for i in range(n)},
        scratch_shapes=[pltpu.SemaphoreType.DMA, pltpu.SemaphoreType.DMA((2 * k + 3,)),
                        pltpu.SemaphoreType.DMA((2 * k + 3,))],
    )(*bufs, small)


HBM = pl.BlockSpec(memory_space=pltpu.HBM)
SEM = pl.BlockSpec(memory_space=pltpu.SEMAPHORE)
DATAFLOW = pltpu.SideEffectType.DATAFLOW_SIDE_EFFECTING


def _copies_start(bufs, plan, n_copies, name, after=None):
    n = len(bufs)
    extra = [] if after is None else [after]

    def body(*refs):
        refs = refs[:n] + refs[n + len(extra):]
        ssem, rsem = refs[n], refs[n + 1]
        b_refs, token = refs[n + 2:2 * n + 2], refs[2 * n + 2]
        copies = plan(b_refs)
        assert len(copies) == n_copies
        for i, (src, dst, peer, _) in enumerate(copies):
            pltpu.make_async_remote_copy(src_ref=src, dst_ref=dst, send_sem=ssem.at[i], recv_sem=rsem.at[i],
                                         device_id=peer, device_id_type=MESH).start()
        token[...] = jnp.zeros_like(token)

    return pl.pallas_call(
        body, name=name,
        out_shape=(pltpu.SemaphoreType.DMA((n_copies,)), pltpu.SemaphoreType.DMA((n_copies,)),
                   *[pltpu.HBM(b.shape, b.dtype) for b in bufs], jax.ShapeDtypeStruct((SUBLANES, LANES), F32)),
        in_specs=[HBM] * n + [ANY] * len(extra),
        out_specs=(SEM, SEM, *[HBM] * n, pl.BlockSpec(memory_space=pltpu.VMEM)),
        input_output_aliases={i: 2 + i for i in range(n)},
        compiler_params=pltpu.CompilerParams(has_side_effects=DATAFLOW),
    )(*[pltpu.with_memory_space_constraint(b, pltpu.HBM) for b in bufs], *extra)


def _copies_wait(bufs, ssem, rsem, after, plan, name):
    n = len(bufs)
    afters = list(after) if isinstance(after, (list, tuple)) else [after]

    def body(*refs):
        b_refs, ssem_ref, rsem_ref = refs[:n], refs[n], refs[n + 1]
        for i, (src, dst, peer, lands) in enumerate(plan(b_refs)):
            pltpu.make_async_remote_copy(src_ref=src, dst_ref=dst, send_sem=ssem_ref.at[i], recv_sem=rsem_ref.at[i],
                                         device_id=peer, device_id_type=MESH).wait_send()
            pltpu.make_async_remote_copy(src_ref=lands, dst_ref=lands, send_sem=ssem_ref.at[i],
                                         recv_sem=rsem_ref.at[i], device_id=peer, device_id_type=MESH).wait_recv()

    return pl.pallas_call(
        body, name=name,
        out_shape=tuple(pltpu.HBM(b.shape, b.dtype) for b in bufs),
        in_specs=[HBM] * n + [SEM, SEM] + [ANY] * len(afters), out_specs=tuple([HBM] * n),
        input_output_aliases={i: i for i in range(n)},
        compiler_params=pltpu.CompilerParams(has_side_effects=DATAFLOW),
    )(*bufs, ssem, rsem, *afters)


def _gather_plan(stage):
    return lambda refs: [(ref, ref, peer, lands) for ref, peer, lands in _gather_copies(refs, stage)]


def _swap_plan(refs):
    n = len(refs) // 2
    x, y, c, _ = _place()
    return [(refs[a].at[:, 1 - c], refs[n + a], (x, y, 1 - c), refs[n + a]) for a in range(n)]


def _scatter_plan(refs):
    n = len(refs) // 2
    x, y, c, chips = _place()
    me = _chip_index(x, y)
    return [(refs[a].at[_chip_index(*chip)], refs[n + a].at[me], (*chip, c), refs[n + a].at[_chip_index(*chip)])
            for a in range(n) for chip in chips]


def _pair_gather_plan(refs):
    x, y, c, _ = _place()
    return [(r.at[c], r.at[c], (x, y, 1 - c), r.at[1 - c]) for r in refs]


def _pair_swap(xs, name):
    n = len(xs)

    def body(*refs):
        x_refs, o_refs, ssem, rsem = refs[:n], refs[n:2 * n], refs[2 * n], refs[2 * n + 1]
        x, y, c, _ = _place()
        copies = [pltpu.make_async_remote_copy(src_ref=x_refs[a].at[:, 1 - c], dst_ref=o_refs[a],
                                               send_sem=ssem.at[a], recv_sem=rsem.at[a],
                                               device_id=(x, y, 1 - c), device_id_type=MESH) for a in range(n)]
        for cp in copies:
            cp.start()
        for cp in copies:
            cp.wait()

    return pl.pallas_call(
        body, name=name, in_specs=[ANY] * n, out_specs=[ANY] * n,
        out_shape=[jax.ShapeDtypeStruct((a.shape[0],) + a.shape[2:], a.dtype) for a in xs],
        scratch_shapes=[pltpu.SemaphoreType.DMA((n,)), pltpu.SemaphoreType.DMA((n,))],
    )(*xs)


def _chip_scatter(ps):
    n = len(ps)

    def body(*refs):
        p_refs, o_refs, ssem, rsem = refs[:n], refs[n:2 * n], refs[2 * n], refs[2 * n + 1]
        x, y, c, chips = _place()
        me = _chip_index(x, y)
        sends = []
        for a in range(n):
            for j, chip in enumerate(chips):
                sends.append(pltpu.make_async_remote_copy(
                    src_ref=p_refs[a].at[_chip_index(*chip)], dst_ref=o_refs[a].at[me],
                    send_sem=ssem.at[3 * a + j], recv_sem=rsem.at[3 * a + j],
                    device_id=(*chip, c), device_id_type=MESH))
        for cp in sends:
            cp.start()
        for a in range(n):
            for j, chip in enumerate(chips):
                src = _chip_index(*chip)
                pltpu.make_async_remote_copy(
                    src_ref=p_refs[a].at[src], dst_ref=o_refs[a].at[src],
                    send_sem=ssem.at[3 * a + j], recv_sem=rsem.at[3 * a + j],
                    device_id=(*chip, c), device_id_type=MESH).wait_recv()
        for cp in sends:
            cp.wait_send()

    return pl.pallas_call(
        body, name="chip_scatter", in_specs=[ANY] * n, out_specs=[ANY] * n,
        out_shape=[jax.ShapeDtypeStruct(a.shape, a.dtype) for a in ps],
        scratch_shapes=[pltpu.SemaphoreType.DMA((3 * n,)), pltpu.SemaphoreType.DMA((3 * n,))],
    )(*ps)


def _final_gather(fs, rep):
    n = len(fs)

    def body(*refs):
        o_refs, repo_ref = refs[n + 1:2 * n + 1], refs[2 * n + 1]
        ssem, rsem = refs[2 * n + 2:]
        x, y, c, chips = _place()
        slot = 4 * x + 2 * y + c
        copies = [pltpu.make_async_remote_copy(src_ref=o_refs[a].at[c], dst_ref=o_refs[a].at[c],
                                               send_sem=ssem.at[a], recv_sem=rsem.at[a],
                                               device_id=(x, y, 1 - c), device_id_type=MESH) for a in range(n)]
        peers = [(x, y, 1 - c)] + [(*chip, c) for chip in chips] + [(*chip, 1 - c) for chip in chips]
        for k, peer in enumerate(peers):
            copies.append(pltpu.make_async_remote_copy(src_ref=repo_ref.at[slot], dst_ref=repo_ref.at[slot],
                                                       send_sem=ssem.at[n + k], recv_sem=rsem.at[n + k],
                                                       device_id=peer, device_id_type=MESH))
        for cp in copies:
            cp.start()
        for a in range(n):
            pltpu.make_async_remote_copy(src_ref=o_refs[a].at[1 - c], dst_ref=o_refs[a].at[1 - c],
                                         send_sem=ssem.at[a], recv_sem=rsem.at[a],
                                         device_id=(x, y, 1 - c), device_id_type=MESH).wait_recv()
        for k, peer in enumerate(peers):
            px, py, pc = peer
            theirs = repo_ref.at[4 * px + 2 * py + pc]
            pltpu.make_async_remote_copy(src_ref=theirs, dst_ref=theirs, send_sem=ssem.at[n + k], recv_sem=rsem.at[n + k],
                                         device_id=peer, device_id_type=MESH).wait_recv()
        for cp in copies:
            cp.wait_send()

    return pl.pallas_call(
        body, name="final_gather", in_specs=[ANY] * (n + 1), out_specs=[ANY] * (n + 1),
        out_shape=[jax.ShapeDtypeStruct(a.shape, a.dtype) for a in fs] + [jax.ShapeDtypeStruct(rep.shape, rep.dtype)],
        input_output_aliases={k: k for k in range(n + 1)},
        scratch_shapes=[pltpu.SemaphoreType.DMA((n + 7,)), pltpu.SemaphoreType.DMA((n + 7,))],
    )(*fs, rep)


def _block_diag(w, gb):
    nh, hd, _ = w.shape
    per = gb // hd
    w4 = w.reshape(nh // per, per, hd, hd)
    eye = jnp.eye(per, dtype=w.dtype)
    return jnp.einsum("jaik,ab->jaibk", w4, eye).reshape(nh // per, gb, gb)


def _diag_blocks(dense, hd):
    nj, gb, _ = dense.shape
    per = gb // hd
    d5 = dense.reshape(nj, per, hd, per, hd)
    return jnp.stack([d5[:, a, :, a, :] for a in range(per)], axis=1).reshape(nj * per, hd, hd)


def _round_up(n, q):
    return (n + q - 1) // q * q


def kernel(x, meta, norm_g, w_in, conv_a_w, conv_a_b, lru_wr, lru_br, lru_wi, lru_bi, lru_lambda, conv_b_w, w_out, final_g, loss_target, m_meta, m_norm_g, m_w_in, m_conv_a_w, m_conv_a_b, m_lru_wr, m_lru_br, m_lru_wi, m_lru_bi, m_lru_lambda, m_conv_b_w, m_w_out, m_final_g, v_meta, v_norm_g, v_w_in, v_conv_a_w, v_conv_a_b, v_lru_wr, v_lru_br, v_lru_wi, v_lru_bi, v_lru_lambda, v_conv_b_w, v_w_out, v_final_g):
    weights = dict(meta=meta, norm_g=norm_g, w_in=w_in, conv_a_w=conv_a_w, conv_a_b=conv_a_b, lru_wr=lru_wr,
                   lru_br=lru_br, lru_wi=lru_wi, lru_bi=lru_bi, lru_lambda=lru_lambda, conv_b_w=conv_b_w,
                   w_out=w_out, final_g=final_g)
    mom1 = dict(meta=m_meta, norm_g=m_norm_g, w_in=m_w_in, conv_a_w=m_conv_a_w, conv_a_b=m_conv_a_b,
                lru_wr=m_lru_wr, lru_br=m_lru_br, lru_wi=m_lru_wi, lru_bi=m_lru_bi, lru_lambda=m_lru_lambda,
                conv_b_w=m_conv_b_w, w_out=m_w_out, final_g=m_final_g)
    mom2 = dict(meta=v_meta, norm_g=v_norm_g, w_in=v_w_in, conv_a_w=v_conv_a_w, conv_a_b=v_conv_a_b,
                lru_wr=v_lru_wr, lru_br=v_lru_br, lru_wi=v_lru_wi, lru_bi=v_lru_bi, lru_lambda=v_lru_lambda,
                conv_b_w=v_conv_b_w, w_out=v_w_out, final_g=v_final_g)
    names = list(weights)

    assert x.shape[0] == 1
    seq, d = x.shape[1], x.shape[2]
    n_meta, ds = meta.shape
    depth = norm_g.shape[0]
    c = lru_lambda.shape[1]
    nh, hd = lru_wr.shape[1], lru_wr.shape[2]
    ns = w_in.shape[2]
    dms = w_out.shape[1]
    cs = conv_a_w.shape[2]
    ka, kb = conv_a_w.shape[1], conv_b_w.shape[1]
    s = N_CHIPS
    assert depth == N_CORES and d == s * ds and c == s * cs and s * ns == 6 * c and s * dms == 2 * c
    gb = min(GATE_BLOCK, c)
    t_real = n_meta + seq
    t = _round_up(t_real, ROW_QUANTUM)
    my_c = lax.axis_index("c").astype(jnp.int32)
    my_chip = (2 * lax.axis_index("x") + lax.axis_index("y")).astype(jnp.int32)
    c_idx = my_c.reshape(1)
    chip_idx = my_chip.reshape(1)

    sm_rows = _round_up(n_meta + depth * SUBLANES, 2 * SUBLANES)
    small = jnp.zeros((sm_rows, ds), F32)
    small = small.at[0:n_meta, :].set(meta)
    for l in range(depth):
        base = n_meta + l * SUBLANES
        small = small.at[base:base + ka, 0:cs].set(conv_a_w[l])
        small = small.at[base + ka:base + ka + kb, 0:cs].set(conv_b_w[l])
    (small_g,) = _gather_first([], small)
    meta_full = jnp.transpose(small_g[:, 0:n_meta, :], (1, 0, 2)).reshape(n_meta, d)
    wa_full, wb_full = [], []
    for l in range(depth):
        base = n_meta + l * SUBLANES
        wa_full.append(jnp.transpose(small_g[:, base:base + ka, 0:cs], (1, 0, 2)).reshape(ka, c))
        wb_full.append(jnp.transpose(small_g[:, base + ka:base + ka + kb, 0:cs], (1, 0, 2)).reshape(kb, c))
    win0 = _cast_place(w_in, 0, chip_idx, "cast_w_in_0").reshape(s, 2, d // 2, ns)
    ssem_w, rsem_w, win0, token_w = _copies_start([win0], _gather_plan(0), 3, "gather_win0_ici_start", after=small_g)
    win_b = [None] + [_cast_place(w_in, l, chip_idx, f"cast_w_in_{l}", after=token_w).reshape(s, 2, d // 2, ns)
                      for l in range(1, depth)]
    wout_b = [_cast_place(w_out, l, chip_idx, f"cast_w_out_{l}", after=token_w).reshape(s, 2, dms // 2, d)
              for l in range(depth)]
    h = jnp.concatenate([meta_full, x[0], jnp.zeros((t - t_real, d), F32)], axis=0) + token_w[0, 0]
    tgt = jnp.concatenate([jnp.zeros((n_meta, d), F32), loss_target[0], jnp.zeros((t - t_real, d), F32)],
                          axis=0) + token_w[0, 0]
    u_own, hn_own = _norm_in_own(h, norm_g[0].reshape(1, d), win0.reshape(s, d, ns), chip_idx, "norm_in_0_own")
    (win0,) = _copies_wait([win0], ssem_w, rsem_w, [u_own, tgt] + win_b[1:] + wout_b, _gather_plan(0),
                           "gather_win0_ici_wait")
    ssem_w, rsem_w, win0, token_w = _copies_start([win0], _gather_plan(1), 3, "gather_win0_d2d_start")
    def travel(buf, stage, tag, after):
        return _copies_start([buf], _gather_plan(stage), 3, f"gather_{tag}_{'d2d' if stage else 'ici'}_start",
                             after=after)

    def arrived(state, stage, tag, after):
        (buf,) = _copies_wait([state[2]], state[0], state[1], after, _gather_plan(stage),
                              f"gather_{tag}_{'d2d' if stage else 'ici'}_wait")
        return buf

    on_wout0 = travel(wout_b[0], 0, "wout0", token_w)
    on_win1 = travel(win_b[1], 0, "win1", on_wout0[3])
    on_wout1 = travel(wout_b[1], 0, "wout1", on_win1[3])
    token = on_wout1[3]
    (win_b[0],) = _copies_wait([win0], ssem_w, rsem_w, token, _gather_plan(1), "gather_win0_d2d_wait")

    layer_w = []
    for l in range(depth):
        layer_w.append(dict(
            g=norm_g[l].reshape(1, d), wa=wa_full[l], ba=conv_a_b[l].reshape(1, c),
            wr=_block_diag(lru_wr[l], gb).astype(BF16), br=lru_br[l].reshape(1, c),
            wi=_block_diag(lru_wi[l], gb).astype(BF16), bi=lru_bi[l].reshape(1, c),
            lam=lru_lambda[l].reshape(1, c), wb=wb_full[l]))
    saved = []
    for l, lw in enumerate(layer_w):
        first = l == 0
        lw["win"] = win_b[l].reshape(s, d, ns)
        mixer_w = (lw["wa"], lw["ba"], lw["wr"], lw["br"], lw["wi"], lw["bi"], lw["lam"], lw["wb"])
        if first:
            u = _norm_in_rest(hn_own, lw["win"], u_own, chip_idx, "norm_in_0_rest", after=token)
            hn = hn_own
            on_wout0 = travel(arrived(on_wout0, 0, "wout0", u), 1, "wout0", None)
            wout_b[0] = arrived(on_wout0, 1, "wout0", on_wout0[3])
            lw["wout"] = wout_b[0].reshape(2 * c, d)
            y, hs, h_next, hn_next = _mix_fwd(u, *mixer_w, f"mix_fwd_{l}", proj=(h, lw["wout"], layer_w[1]["g"]))
            saved.append((h, u, hn, y, hs))
            h = h_next
            on_win1 = travel(arrived(on_win1, 0, "win1", y), 1, "win1", None)
            win_b[1] = arrived(on_win1, 1, "win1", on_win1[3])
            on_wout1 = travel(arrived(on_wout1, 0, "wout1", y), 1, "wout1", on_win1[3])
            token = on_wout1[3]
        else:
            hn = hn_next
            u = _in_proj(hn, lw["win"], f"norm_in_{l}", after=token)
            wout_b[1] = arrived(on_wout1, 1, "wout1", u)
            lw["wout"] = wout_b[1].reshape(2 * c, d)
            y, hs = _mix_fwd(u, *mixer_w, f"mix_fwd_{l}")
            saved.append((h, u, hn, y, hs))
            dh, loss_lanes, d_final_g, dy = _out_proj_loss(h, y, lw["wout"], tgt, final_g.reshape(1, d), n_meta,
                                                           t_real, f"out_proj_{l}_loss")
    loss = lax.psum(loss_lanes[0, 0], ("x", "y", "c"))

    to_core = jnp.stack([my_chip, my_c])
    grads = [None] * depth
    early = None
    for l in reversed(range(depth)):
        lw = layer_w[l]
        h_in, u, hn, y, hs = saved[l]
        token = early[-1] if early else None
        d_wout = _out_proj_dw(y, dh, f"out_proj_dw_{l}", after=token)
        if early:
            ssem, rsem, bufs, _ = early
            bufs = _copies_wait(bufs, ssem, rsem, d_wout, _swap_plan, "early_swap_wait")
            half = len(bufs) // 2
            sums = [_pair_add(a, b, c_idx, f"early_pair_add_{k}") for k, (a, b) in enumerate(zip(bufs[:half], bufs[half:]))]
            lands = [lax.empty(p.shape, p.dtype) for p in sums]
            ssem, rsem, *bufs, token = _copies_start(sums + lands, _scatter_plan, 3 * half, "early_scatter_start")
        du, dsm, d_wr, d_wi = _mix_bwd(u, hs, dy, lw["wa"], lw["ba"], lw["wr"], lw["br"], lw["wi"], lw["bi"],
                                       lw["lam"], lw["wb"], f"mix_bwd_{l}", after=token)
        if early:
            bufs = _copies_wait(bufs, ssem, rsem, du, _scatter_plan, "early_scatter_wait")
            halves = [_chip_sum(rc, p, to_core, N_CORES, f"early_chip_sum_{k}")
                      for k, (p, rc) in enumerate(zip(bufs[:half], bufs[half:]))]
            ssem, rsem, *bufs, token = _copies_start(halves, _pair_gather_plan, half, "early_gather_start")
        d_win = _in_proj_dw(hn, du, s, f"in_proj_dw_{l}", after=token)
        srcs = [d_win.reshape(s, 2, d // 2, ns), d_wout.reshape(s, 2, dms // 2, d)]
        if early:
            early_full = _copies_wait(bufs, ssem, rsem, d_win, _pair_gather_plan, "early_gather_wait")
            lands = [lax.empty((a.shape[0],) + a.shape[2:], a.dtype) for a in srcs]
            ssem, rsem, *bufs, token = _copies_start(srcs + lands, _swap_plan, len(srcs), "late_swap_start")
            last = depth - 1
            early_grad = dict(w_in=early_full[0].reshape(d, ns), w_out=early_full[1].reshape(dms, d))
            early_step = {n: _adamw_layer(weights[n], early_grad[n], mom1[n], mom2[n], last, None,
                                          f"adamw_{n}_{last}", after=token) for n in ("w_in", "w_out")}
            bufs = _copies_wait(bufs, ssem, rsem, [o[0] for o in early_step.values()], _swap_plan, "late_swap_wait")
            late_sums = [_pair_add(a, b, c_idx, f"pair_add_{k}")
                         for k, (a, b) in enumerate(zip(bufs[:len(srcs)], bufs[len(srcs):]))]
            lands = [lax.empty(p.shape, p.dtype) for p in late_sums]
            ssem, rsem, *bufs, token = _copies_start(late_sums + lands, _scatter_plan, 3 * len(srcs), "late_scatter_start")
        if l > 0:
            dh, d_g, dy = _in_proj_bwd(du, lw["win"], h_in, lw["g"], dh, f"in_proj_bwd_{l}", after=token,
                                       w_below=layer_w[l - 1]["wout"])
        else:
            grad_x, d_meta, d_g = _in_proj_bwd(du, lw["win"], h_in, lw["g"], dh, f"in_proj_bwd_{l}", after=token,
                                               split=(n_meta, seq))
        if early:
            bufs = _copies_wait(bufs, ssem, rsem, grad_x, _scatter_plan, "late_scatter_wait")
            late_reduced = [_chip_sum(rc, p, to_core, N_CORES, f"chip_sum_{k}")
                            for k, (p, rc) in enumerate(zip(bufs[:len(srcs)], bufs[len(srcs):]))]
        grads[l] = dict(dsm=dsm, wr=_diag_blocks(d_wr, hd), wi=_diag_blocks(d_wi, hd), g=d_g)
        if l == depth - 1:
            lands = [lax.empty((a.shape[0],) + a.shape[2:], a.dtype) for a in srcs]
            ssem, rsem, *bufs, token = _copies_start(srcs + lands, _swap_plan, len(srcs), "early_swap_start")
            early = (ssem, rsem, bufs, token)
        else:
            early = None
    grad_x = grad_x[None]

    sharded = []
    sp = jnp.zeros((sm_rows, s, ds), F32)
    sp = sp.at[0:n_meta].set(d_meta.reshape(n_meta, s, ds))
    for l in range(depth):
        base = n_meta + l * SUBLANES
        dsm = grads[l]["dsm"]
        sp = sp.at[base:base + ka, :, 0:cs].set(dsm[ROW_DWA:ROW_DWA + ka].reshape(ka, s, cs))
        sp = sp.at[base + ka:base + ka + kb, :, 0:cs].set(dsm[ROW_DWB:ROW_DWB + kb].reshape(kb, s, cs))
    sharded.append(jnp.transpose(sp, (1, 0, 2)).reshape(s, 2, sm_rows // 2, ds))
    rep_parts = [jnp.concatenate([grads[l]["g"].reshape(-1) for l in range(depth)]), d_final_g.reshape(-1)]
    for row in (ROW_DBA, ROW_DBR, ROW_DBI, ROW_DLAM):
        rep_parts.append(jnp.concatenate([grads[l]["dsm"][row] for l in range(depth)]))
    rep_parts.append(jnp.concatenate([grads[l]["wr"].reshape(-1) for l in range(depth)]))
    rep_parts.append(jnp.concatenate([grads[l]["wi"].reshape(-1) for l in range(depth)]))
    rep_sizes = [p.shape[0] for p in rep_parts]
    piece = _round_up(-(-sum(rep_sizes) // (s * 2)), 2 * SUBLANES * LANES)
    flat = jnp.concatenate(rep_parts + [jnp.zeros((s * 2 * piece - sum(rep_sizes),), F32)])
    sharded.append(flat.reshape(s, 2, piece // LANES, LANES))

    from_sibling = _pair_swap(sharded, "small_pair_swap")
    pair_sums = [_pair_add(a, b, c_idx, f"small_pair_add_{k}") for k, (a, b) in enumerate(zip(sharded, from_sibling))]
    by_chip = _chip_scatter(pair_sums)
    to_device = jnp.stack([my_chip, 2 * my_chip + my_c])
    reduced_sp = _chip_sum(by_chip[0], pair_sums[0], to_core, N_CORES, "small_chip_sum")
    reduced_rep = _chip_sum(by_chip[1], pair_sums[1], to_device, N_CHIPS * N_CORES, "chip_sum_rep")
    *full, rep_all = _final_gather(late_reduced + [reduced_sp], reduced_rep)

    g_win = [full[0].reshape(d, ns), early_full[0].reshape(d, ns)]
    g_wout = [full[1].reshape(dms, d), early_full[1].reshape(dms, d)]
    g_sp = full[2].reshape(sm_rows, ds)
    rep_flat = rep_all.reshape(-1)
    rep_out, off = [], 0
    for n in rep_sizes:
        rep_out.append(rep_flat[off:off + n])
        off += n
    grad = dict(
        meta=g_sp[0:n_meta],
        norm_g=rep_out[0].reshape(depth, d),
        w_in=jnp.stack(g_win),
        conv_a_w=jnp.stack([g_sp[n_meta + l * SUBLANES:n_meta + l * SUBLANES + ka, 0:cs] for l in range(depth)]),
        conv_a_b=rep_out[2].reshape(depth, c),
        lru_wr=rep_out[6].reshape(depth, nh, hd, hd),
        lru_br=rep_out[3].reshape(depth, c),
        lru_wi=rep_out[7].reshape(depth, nh, hd, hd),
        lru_bi=rep_out[4].reshape(depth, c),
        lru_lambda=rep_out[5].reshape(depth, c),
        conv_b_w=jnp.stack([g_sp[n_meta + l * SUBLANES + ka:n_meta + l * SUBLANES + ka + kb, 0:cs]
                            for l in range(depth)]),
        w_out=jnp.stack(g_wout),
        final_g=rep_out[1].reshape(d),
    )

    delta, new_m, new_v = {}, {}, {}
    for n, g_first in (("w_in", g_win[0]), ("w_out", g_wout[0])):
        delta[n], new_m[n], new_v[n] = _adamw_layer(weights[n], g_first, mom1[n], mom2[n], 0, early_step[n],
                                                    f"adamw_{n}_0")
    for n in names:
        if n in delta:
            continue
        shape = weights[n].shape
        as_block = shape if len(shape) > 1 else (1,) + shape
        out = _adamw(weights[n].reshape(as_block), grad[n].reshape(as_block), mom1[n].reshape(as_block),
                     mom2[n].reshape(as_block), f"adamw_{n}")
        delta[n], new_m[n], new_v[n] = (o.reshape(shape) for o in out)

    return (loss, grad_x, *[grad[n] for n in names], *[delta[n] for n in names],
            *[new_m[n] for n in names], *[new_v[n] for n in names])
```

```python
import jax
import jax.numpy as jnp
from jax import lax
from jax.experimental import pallas as pl
from jax.experimental.pallas import tpu as pltpu

F32 = jnp.float32
BF16 = jnp.bfloat16

RMS_EPS = 1e-6
LRU_C = 8.0
ADAM_LR = 0.001
ADAM_B1 = 0.9
ADAM_B2 = 0.999
ADAM_EPS = 1e-08
ADAM_WD = 0.01
ADAM_STEP = 10

N_CHIPS = 4
N_CORES = 2
VMEM_LIMIT_BYTES = 56 * 1024 * 1024
SUBLANES = 8
LANES = 128
ROW_QUANTUM = 384
MIX_CHUNK = 192
SCAN_UNROLL = 4
GATE_BLOCK = 256
MESH = pl.DeviceIdType.MESH
ANY = pl.BlockSpec(memory_space=pl.ANY)

NT_DIMS = (((1,), (1,)), ((), ()))
TN_DIMS = (((0,), (0,)), ((), ()))


def _params(sem):
    return pltpu.CompilerParams(dimension_semantics=sem, vmem_limit_bytes=VMEM_LIMIT_BYTES)


def _sig(x):
    return 0.5 * jnp.tanh(0.5 * x) + 0.5


def _row_tile(t):
    return 704 if t % 704 == 0 else 192


def _col_tile(n, prefs):
    for p in prefs:
        if n % p == 0:
            return p
    return n


def _slab_rows(rows, cols):
    if rows * cols * 4 <= 1024 * 1024:
        return rows
    return _col_tile(rows, (256, 128, 64, 32, 16))


def _norm_in_own(h, g, wg, me_idx, name):
    t, d = h.shape
    s, _, ns = wg.shape
    tm = 1408 if t % 1408 == 0 else _row_tile(t)
    tn = _col_tile(ns, (768, 384, 128))
    nb = ns // tn

    def body(m_ref, h_ref, g_ref, w_ref, u_ref, hn_ref):
        @pl.when(pl.program_id(1) == 0)
        def _():
            x = h_ref[...]
            r = lax.rsqrt(jnp.mean(x * x, axis=-1, keepdims=True) + RMS_EPS)
            hn_ref[...] = ((x * r) * g_ref[...]).astype(BF16)

        u_ref[...] = jnp.dot(hn_ref[...], w_ref[...], preferred_element_type=F32)

    return pl.pallas_call(
        body, name=name,
        grid_spec=pltpu.PrefetchScalarGridSpec(
            num_scalar_prefetch=1, grid=(t // tm, nb),
            in_specs=[pl.BlockSpec((tm, d), lambda i, n, m: (i, 0)),
                      pl.BlockSpec((1, d), lambda i, n, m: (0, 0)),
                      pl.BlockSpec((None, d, tn), lambda i, n, m: (m[0], 0, n))],
            out_specs=[pl.BlockSpec((tm, tn), lambda i, n, m: (i, m[0] * nb + n)),
                       pl.BlockSpec((tm, d), lambda i, n, m: (i, 0))]),
        out_shape=[jax.ShapeDtypeStruct((t, s * ns), F32), jax.ShapeDtypeStruct((t, d), BF16)],
        compiler_params=_params(("arbitrary", "arbitrary")),
    )(me_idx, h, g, wg)


def _norm_in_rest(hn, wg, u, me_idx, name, after=None):
    t, d = hn.shape
    s, _, ns = wg.shape
    tm = 1408 if t % 1408 == 0 else _row_tile(t)
    tn = _col_tile(ns, (1536, 768, 384, 128))
    nb = ns // tn

    def body(m_ref, hn_ref, w_ref, u_in, u_ref):
        del u_in
        u_ref[...] = jnp.dot(hn_ref[...], w_ref[...], preferred_element_type=F32)

    def shard(n, m):
        return (m[0] + 1 + n // nb) % s

    body, more_specs, more = _behind(body, 4, after)
    return pl.pallas_call(
        body, name=name,
        grid_spec=pltpu.PrefetchScalarGridSpec(
            num_scalar_prefetch=1, grid=(t // tm, (s - 1) * nb),
            in_specs=[pl.BlockSpec((tm, d), lambda i, n, m: (i, 0)),
                      pl.BlockSpec((None, d, tn), lambda i, n, m: (shard(n, m), 0, n % nb)),
                      ANY] + more_specs,
            out_specs=pl.BlockSpec((tm, tn), lambda i, n, m: (i, shard(n, m) * nb + n % nb))),
        out_shape=jax.ShapeDtypeStruct(u.shape, u.dtype),
        input_output_aliases={3: 0},
        compiler_params=_params(("arbitrary", "arbitrary")),
    )(me_idx, hn, wg, u, *more)


def _decay_consts(lam):
    z = -lam
    e = jnp.exp(-jnp.abs(z))
    u = 1.0 + e
    log1p_e = jnp.where(u == 1.0, e, jnp.log(u) * (e / (u - 1.0)))
    sp = jnp.maximum(z, 0.0) + log1p_e
    return -LRU_C * sp, LRU_C * _sig(z)


def _gates(xc, wr_ref, br_ref, wi_ref, bi_ref, c8, j, gb):
    sl = slice(j * gb, (j + 1) * gb)
    x16 = xc.astype(BF16)
    r = _sig(jnp.dot(x16, wr_ref[j], preferred_element_type=F32) + br_ref[:, sl])
    ig = _sig(jnp.dot(x16, wi_ref[j], preferred_element_type=F32) + bi_ref[:, sl])
    la = c8[:, sl] * r
    a = jnp.exp(la)
    sq = jnp.sqrt(-jnp.tanh(la) * (a * a + 1.0))
    return r, ig, a, sq


def _mix_fwd(u, wa, ba, wr, br, wi, bi, lam, wb, name, proj=None):
    t = u.shape[0]
    c = u.shape[1] // 6
    tc = MIX_CHUNK
    gb = wr.shape[1]
    nblk = c // gb
    ka, kb = wa.shape[0], wb.shape[0]
    n_proj = 0 if proj is None else 3

    def body(*refs):
        u_ref, wa_ref, ba_ref, wr_ref, br_ref, wi_ref, bi_ref, lam_ref, wb_ref = refs[:9]
        outs = refs[9 + n_proj:]
        y_ref, hs_ref = outs[:2]
        xa_ext, v_ext, xc_s, a_s, b_s, carry_s = outs[-6:]

        @pl.when(pl.program_id(0) == 0)
        def _():
            xa_ext[0:SUBLANES, :] = jnp.zeros((SUBLANES, c), F32)
            v_ext[0:SUBLANES, :] = jnp.zeros((SUBLANES, c), F32)
            carry_s[...] = jnp.zeros_like(carry_s)

        xa_ext[SUBLANES:SUBLANES + tc, :] = u_ref[:, 0:c]
        xc = ba_ref[...]
        for k in range(ka):
            xc = xc + wa_ref[pl.ds(k, 1), :] * xa_ext[pl.ds(SUBLANES - (ka - 1) + k, tc), :]
        xc_s[...] = xc
        c8, _ = _decay_consts(lam_ref[...])
        for j in range(nblk):
            sl = slice(j * gb, (j + 1) * gb)
            xcj = xc_s[:, sl]
            _, ig, a, sq = _gates(xcj, wr_ref, br_ref, wi_ref, bi_ref, c8, j, gb)
            a_s[:, sl] = a
            b_s[:, sl] = sq * (ig * xcj)

        row = lax.broadcasted_iota(jnp.int32, (SUBLANES, c), 0)

        def scan_step(j, _):
            off = pl.multiple_of(j * SUBLANES, SUBLANES)
            av = a_s[pl.ds(off, SUBLANES), :]
            bv = b_s[pl.ds(off, SUBLANES), :]
            for d in (1, 2, 4):
                keep = row >= d
                bsh = jnp.where(keep, pltpu.roll(bv, d, axis=0), 0.0)
                ash = jnp.where(keep, pltpu.roll(av, d, axis=0), 1.0)
                bv = av * bsh + bv
                av = av * ash
            hv = av * carry_s[...] + bv
            hs_ref[pl.ds(off, SUBLANES), :] = hv
            carry_s[...] = hs_ref[pl.ds(off + SUBLANES - 1, 1), :]
            return 0

        lax.fori_loop(0, tc // SUBLANES, scan_step, 0, unroll=SCAN_UNROLL)

        ga = u_ref[:, c:2 * c]
        y_ref[:, 0:c] = (hs_ref[...] * (ga * _sig(ga))).astype(BF16)

        v_ext[SUBLANES:SUBLANES + tc, :] = u_ref[:, 3 * c:4 * c] * u_ref[:, 4 * c:5 * c]
        cv = wb_ref[pl.ds(0, 1), :] * v_ext[pl.ds(SUBLANES - (kb - 1), tc), :]
        for k in range(1, kb):
            cv = cv + wb_ref[pl.ds(k, 1), :] * v_ext[pl.ds(SUBLANES - (kb - 1) + k, tc), :]
        gbv = u_ref[:, 5 * c:6 * c]
        y_ref[:, c:2 * c] = (u_ref[:, 2 * c:3 * c] * cv * (gbv * _sig(gbv))).astype(BF16)

        xa_ext[0:SUBLANES, :] = xa_ext[tc:tc + SUBLANES, :]
        v_ext[0:SUBLANES, :] = v_ext[tc:tc + SUBLANES, :]

        if proj is not None:
            h_ref, wout_ref, g_ref = refs[9:12]
            ho_ref, hn_ref = outs[2:4]
            x = h_ref[...] + jnp.dot(y_ref[...], wout_ref[...], preferred_element_type=F32)
            ho_ref[...] = x
            r = lax.rsqrt(jnp.mean(x * x, axis=-1, keepdims=True) + RMS_EPS)
            hn_ref[...] = ((x * r) * g_ref[...]).astype(BF16)

    full = lambda shape: pl.BlockSpec(shape, lambda i: (0,) * len(shape))
    rows = lambda width: pl.BlockSpec((tc, width), lambda i: (i, 0))
    more_in, more_specs, more_out_specs, more_out = [], [], [], []
    if proj is not None:
        h, wout, g_next = proj
        d = h.shape[1]
        more_in = [h, wout, g_next]
        more_specs = [rows(d), full(wout.shape), full(g_next.shape)]
        more_out_specs = [rows(d), rows(d)]
        more_out = [jax.ShapeDtypeStruct((t, d), F32), jax.ShapeDtypeStruct((t, d), BF16)]
    return pl.pallas_call(
        body, name=name, grid=(t // tc,),
        in_specs=[rows(6 * c), full(wa.shape), full(ba.shape), full(wr.shape), full(br.shape),
                  full(wi.shape), full(bi.shape), full(lam.shape), full(wb.shape)] + more_specs,
        out_specs=[rows(2 * c), rows(c)] + more_out_specs,
        out_shape=[jax.ShapeDtypeStruct((t, 2 * c), BF16), jax.ShapeDtypeStruct((t, c), F32)] + more_out,
        scratch_shapes=[pltpu.VMEM((tc + SUBLANES, c), F32), pltpu.VMEM((tc + SUBLANES, c), F32),
                        pltpu.VMEM((tc, c), F32), pltpu.VMEM((tc, c), F32), pltpu.VMEM((tc, c), F32),
                        pltpu.VMEM((1, c), F32)],
        compiler_params=_params(("arbitrary",)),
    )(u, wa, ba, wr, br, wi, bi, lam, wb, *more_in)


ROW_DWA = 0
ROW_DBA = 4
ROW_DBR = 5
ROW_DBI = 6
ROW_DLAM = 7
ROW_DWB = 8
SMALL_ROWS = 16


def _mix_bwd(u, hs, dy, wa, ba, wr, br, wi, bi, lam, wb, name, after=None):
    t = u.shape[0]
    c = u.shape[1] // 6
    tc = MIX_CHUNK
    nt = t // tc
    gb = wr.shape[1]
    nblk = c // gb
    ka, kb = wa.shape[0], wb.shape[0]
    assert ka <= ROW_DBA and kb <= SMALL_ROWS - ROW_DWB
    hb = tc // SUBLANES

    def body(u_ref, uh_ref, hs_ref, hsh_ref, dy_ref, wa_ref, ba_ref, wr_ref, br_ref, wi_ref, bi_ref, lam_ref, wb_ref,
             du_ref, dsm_ref, dwr_ref, dwi_ref,
             xa_ext, v_ext, hs_ext, a_ext, ds_ext, dxc_ext, dcv_ext, xc_s, r_s, i_s, sq_s, g_s, an_s):
        i = pl.program_id(0)
        chunk = nt - 1 - i
        tail = slice(tc, tc + SUBLANES)
        head = slice(0, SUBLANES)

        @pl.when(i == 0)
        def _():
            zero = jnp.zeros((SUBLANES, c), F32)
            a_ext[tail, :] = zero
            ds_ext[tail, :] = zero
            dxc_ext[tail, :] = zero
            dcv_ext[tail, :] = zero
            dsm_ref[...] = jnp.zeros_like(dsm_ref)
            dwr_ref[...] = jnp.zeros_like(dwr_ref)
            dwi_ref[...] = jnp.zeros_like(dwi_ref)

        prev = jnp.where(chunk > 0, 1.0, 0.0)
        xa_ext[head, :] = uh_ref[:, 0:c] * prev
        xa_ext[SUBLANES:SUBLANES + tc, :] = u_ref[:, 0:c]
        v_ext[head, :] = uh_ref[:, 3 * c:4 * c] * uh_ref[:, 4 * c:5 * c] * prev
        v_ext[SUBLANES:SUBLANES + tc, :] = u_ref[:, 3 * c:4 * c] * u_ref[:, 4 * c:5 * c]
        hs_ext[head, :] = hsh_ref[...] * prev
        hs_ext[SUBLANES:SUBLANES + tc, :] = hs_ref[...]

        xc = ba_ref[...]
        for k in range(ka):
            xc = xc + wa_ref[pl.ds(k, 1), :] * xa_ext[pl.ds(SUBLANES - (ka - 1) + k, tc), :]
        xc_s[...] = xc
        c8, dc8 = _decay_consts(lam_ref[...])
        for j in range(nblk):
            sl = slice(j * gb, (j + 1) * gb)
            r, ig, a, sq = _gates(xc_s[:, sl], wr_ref, br_ref, wi_ref, bi_ref, c8, j, gb)
            r_s[:, sl] = r
            i_s[:, sl] = ig
            sq_s[:, sl] = sq
            a_ext[0:tc, sl] = a

        ga = u_ref[:, c:2 * c]
        sga = _sig(ga)
        g_s[...] = dy_ref[:, 0:c] * (ga * sga)
        an_s[...] = a_ext[pl.ds(1, tc), :]

        row = lax.broadcasted_iota(jnp.int32, (SUBLANES, c), 0)

        def scan_step(j, _):
            off = pl.multiple_of(tc - SUBLANES - j * SUBLANES, SUBLANES)
            av = an_s[pl.ds(off, SUBLANES), :]
            bv = g_s[pl.ds(off, SUBLANES), :]
            for d in (1, 2, 4):
                keep = row < SUBLANES - d
                bsh = jnp.where(keep, pltpu.roll(bv, SUBLANES - d, axis=0), 0.0)
                ash = jnp.where(keep, pltpu.roll(av, SUBLANES - d, axis=0), 1.0)
                bv = av * bsh + bv
                av = av * ash
            ds_ext[pl.ds(off, SUBLANES), :] = av * ds_ext[pl.ds(off + SUBLANES, 1), :] + bv
            return 0

        lax.fori_loop(0, tc // SUBLANES, scan_step, 0, unroll=SCAN_UNROLL)

        def acc(row_index, val):
            dsm_ref[pl.ds(row_index, 1), :] += jnp.sum(val, axis=0, keepdims=True)

        def acc_block(row_index, sl, val):
            dsm_ref[pl.ds(row_index, 1), sl] += jnp.sum(val, axis=0, keepdims=True)

        for j in range(nblk):
            sl = slice(j * gb, (j + 1) * gb)
            ds = ds_ext[0:tc, sl]
            hprev = hs_ext[pl.ds(SUBLANES - 1, tc), sl]
            a = a_ext[0:tc, sl]
            sq = sq_s[:, sl]
            ig = i_s[:, sl]
            r = r_s[:, sl]
            xcj = xc_s[:, sl]
            t1 = ds * xcj
            dla = (ds * hprev) * a - (t1 * ig) * ((a * a) * lax.rsqrt(sq * sq))
            acc_block(ROW_DLAM, sl, dla * r)
            dpr = (dla * c8[:, sl]) * (r * (1.0 - r))
            dpi = (t1 * sq) * (ig * (1.0 - ig))
            acc_block(ROW_DBR, sl, dpr)
            acc_block(ROW_DBI, sl, dpi)
            p16 = dpr.astype(BF16)
            q16 = dpi.astype(BF16)
            x16 = xcj.astype(BF16)
            dwr_ref[j] += lax.dot_general(x16, p16, TN_DIMS, preferred_element_type=F32)
            dwi_ref[j] += lax.dot_general(x16, q16, TN_DIMS, preferred_element_type=F32)
            dxc = (ds * (sq * ig)
                   + lax.dot_general(p16, wr_ref[j], NT_DIMS, preferred_element_type=F32)
                   + lax.dot_general(q16, wi_ref[j], NT_DIMS, preferred_element_type=F32))
            dxc_ext[0:tc, sl] = dxc
            acc_block(ROW_DBA, sl, dxc)

        dsilu_a = sga * (1.0 + ga * (1.0 - sga))
        du_ref[:, c:2 * c] = (dy_ref[:, 0:c] * hs_ref[...] * dsilu_a).astype(BF16)

        dxc = dxc_ext[0:tc, :]
        dxa = wa_ref[pl.ds(ka - 1, 1), :] * dxc
        acc(ROW_DWA + ka - 1, dxc * xa_ext[SUBLANES:SUBLANES + tc, :])
        for k in range(ka - 1):
            acc(ROW_DWA + k, dxc * xa_ext[pl.ds(SUBLANES - (ka - 1) + k, tc), :])
            dxa = dxa + wa_ref[pl.ds(k, 1), :] * dxc_ext[pl.ds(ka - 1 - k, tc), :]
        du_ref[:, 0:c] = dxa.astype(BF16)

        cv = wb_ref[pl.ds(0, 1), :] * v_ext[pl.ds(SUBLANES - (kb - 1), tc), :]
        for k in range(1, kb):
            cv = cv + wb_ref[pl.ds(k, 1), :] * v_ext[pl.ds(SUBLANES - (kb - 1) + k, tc), :]
        gbv = u_ref[:, 5 * c:6 * c]
        sgb = _sig(gbv)
        silu_b = gbv * sgb
        dyb = dy_ref[:, c:2 * c]
        gB = u_ref[:, 2 * c:3 * c]
        du_ref[:, 2 * c:3 * c] = (dyb * cv * silu_b).astype(BF16)
        du_ref[:, 5 * c:6 * c] = (dyb * gB * cv * (sgb * (1.0 + gbv * (1.0 - sgb)))).astype(BF16)
        dcv = dyb * gB * silu_b
        dcv_ext[0:tc, :] = dcv
        dv = wb_ref[pl.ds(kb - 1, 1), :] * dcv
        acc(ROW_DWB + kb - 1, dcv * v_ext[SUBLANES:SUBLANES + tc, :])
        for k in range(kb - 1):
            acc(ROW_DWB + k, dcv * v_ext[pl.ds(SUBLANES - (kb - 1) + k, tc), :])
            dv = dv + wb_ref[pl.ds(k, 1), :] * dcv_ext[pl.ds(kb - 1 - k, tc), :]
        du_ref[:, 3 * c:4 * c] = (dv * u_ref[:, 4 * c:5 * c]).astype(BF16)
        du_ref[:, 4 * c:5 * c] = (dv * u_ref[:, 3 * c:4 * c]).astype(BF16)

        a_ext[tail, :] = a_ext[head, :]
        ds_ext[tail, :] = ds_ext[head, :]
        dxc_ext[tail, :] = dxc_ext[head, :]
        dcv_ext[tail, :] = dcv_ext[head, :]

        @pl.when(i == nt - 1)
        def _():
            dsm_ref[pl.ds(ROW_DLAM, 1), :] = dsm_ref[pl.ds(ROW_DLAM, 1), :] * dc8

    full = lambda shape: pl.BlockSpec(shape, lambda i: (0,) * len(shape))
    rev = lambda i: (nt - 1 - i, 0)
    halo = lambda i: (jnp.maximum((nt - 1 - i) * hb - 1, 0), 0)
    ext = pltpu.VMEM((tc + SUBLANES, c), F32)
    blk = pltpu.VMEM((tc, c), F32)
    body, more_specs, more = _behind(body, 13, after)
    return pl.pallas_call(
        body, name=name, grid=(nt,),
        in_specs=[pl.BlockSpec((tc, 6 * c), rev), pl.BlockSpec((SUBLANES, 6 * c), halo),
                  pl.BlockSpec((tc, c), rev), pl.BlockSpec((SUBLANES, c), halo),
                  pl.BlockSpec((tc, 2 * c), rev),
                  full(wa.shape), full(ba.shape), full(wr.shape), full(br.shape),
                  full(wi.shape), full(bi.shape), full(lam.shape), full(wb.shape)] + more_specs,
        out_specs=[pl.BlockSpec((tc, 6 * c), rev), full((SMALL_ROWS, c)), full(wr.shape), full(wi.shape)],
        out_shape=[jax.ShapeDtypeStruct((t, 6 * c), BF16), jax.ShapeDtypeStruct((SMALL_ROWS, c), F32),
                   jax.ShapeDtypeStruct(wr.shape, F32), jax.ShapeDtypeStruct(wi.shape, F32)],
        scratch_shapes=[ext] * 7 + [blk] * 6,
        compiler_params=_params(("arbitrary",)),
    )(u, u, hs, hs, dy, wa, ba, wr, br, wi, bi, lam, wb, *more)


def _behind(body, n_in, after):
    if after is None:
        return body, [], []
    return (lambda *refs: body(*refs[:n_in], *refs[n_in + 1:])), [ANY], [after]


def _in_proj(hn, wg, name, after=None):
    t, d = hn.shape
    s, _, ns = wg.shape
    tm = 1408 if t % 1408 == 0 else _row_tile(t)

    def body(hn_ref, w_ref, u_ref):
        u_ref[...] = jnp.dot(hn_ref[...], w_ref[...], preferred_element_type=F32)

    body, more_specs, more = _behind(body, 2, after)
    return pl.pallas_call(
        body, name=name, grid=(t // tm, s),
        in_specs=[pl.BlockSpec((tm, d), lambda i, n: (i, 0)),
                  pl.BlockSpec((None, d, ns), lambda i, n: (n, 0, 0))] + more_specs,
        out_specs=pl.BlockSpec((tm, ns), lambda i, n: (i, n)),
        out_shape=jax.ShapeDtypeStruct((t, s * ns), F32),
        compiler_params=_params(("arbitrary", "arbitrary")),
    )(hn, wg, *more)


def _out_proj_dw(y, dout, name, after=None):
    t, dm = y.shape
    d = dout.shape[1]
    tmm = _col_tile(dm, (1024, 512, 256))
    tn = _col_tile(d, (512, 256))

    def body(y_ref, g_ref, o_ref):
        o_ref[...] = lax.dot_general(y_ref[...], g_ref[...].astype(BF16), TN_DIMS, preferred_element_type=F32)

    body, more_specs, more = _behind(body, 2, after)
    return pl.pallas_call(
        body, name=name, grid=(d // tn, dm // tmm),
        in_specs=[pl.BlockSpec((t, tmm), lambda n, m: (0, m)),
                  pl.BlockSpec((t, tn), lambda n, m: (0, n))] + more_specs,
        out_specs=pl.BlockSpec((tmm, tn), lambda n, m: (m, n)),
        out_shape=jax.ShapeDtypeStruct((dm, d), F32),
        compiler_params=_params(("arbitrary", "arbitrary")),
    )(y, dout, *more)


def _in_proj_bwd(du, wg, h, g, dout, name, after=None, split=None, w_below=None):
    t, d = h.shape
    s, _, ns = wg.shape
    tm = _row_tile(t)
    tn = _col_tile(d, (1024, 512, 256))

    def mm_body(du_ref, w_ref, o_ref):
        total = lax.dot_general(du_ref[:, 0:ns], w_ref[0], NT_DIMS, preferred_element_type=F32)
        for a in range(1, s):
            total = total + lax.dot_general(du_ref[:, a * ns:(a + 1) * ns], w_ref[a], NT_DIMS,
                                            preferred_element_type=F32)
        o_ref[...] = total

    mm_body, more_specs, more = _behind(mm_body, 2, after)
    dhn = pl.pallas_call(
        mm_body, name=name, grid=(t // tm, d // tn),
        in_specs=[pl.BlockSpec((tm, s * ns), lambda i, n: (i, 0)),
                  pl.BlockSpec((s, tn, ns), lambda i, n: (0, n, 0))] + more_specs,
        out_specs=pl.BlockSpec((tm, tn), lambda i, n: (i, n)),
        out_shape=jax.ShapeDtypeStruct((t, d), F32),
        compiler_params=_params(("arbitrary", "arbitrary")),
    )(du, wg, *more)

    tr = 352 if t % 352 == 0 else 192
    nt = t // tr

    def row_grad(dhn_ref, h_ref, g_ref, dout_ref, dg_ref):
        @pl.when(pl.program_id(0) == 0)
        def _():
            dg_ref[...] = jnp.zeros_like(dg_ref)

        x = h_ref[...]
        dn = dhn_ref[...]
        r = lax.rsqrt(jnp.mean(x * x, axis=-1, keepdims=True) + RMS_EPS)
        gd = dn * g_ref[...]
        dot = jnp.mean(gd * x, axis=-1, keepdims=True)
        dg_ref[...] += jnp.sum(dn * (x * r), axis=0, keepdims=True)
        return dout_ref[...] + (r * gd - x * ((r * r * r) * dot))

    rows = pl.BlockSpec((tr, d), lambda i: (i, 0))
    one = pl.BlockSpec((1, d), lambda i: (0, 0))
    if split is None:
        dm = w_below.shape[0]

        def norm_body(dhn_ref, h_ref, g_ref, dout_ref, w_ref, dh_ref, dg_ref, dy_ref):
            dh = row_grad(dhn_ref, h_ref, g_ref, dout_ref, dg_ref)
            dh_ref[...] = dh
            dy_ref[...] = lax.dot_general(dh.astype(BF16), w_ref[...], NT_DIMS, preferred_element_type=F32)

        return pl.pallas_call(
            norm_body, name=name + "_norm", grid=(nt,),
            in_specs=[rows, rows, one, rows, pl.BlockSpec((dm, d), lambda i: (0, 0))],
            out_specs=[rows, one, pl.BlockSpec((tr, dm), lambda i: (i, 0))],
            out_shape=[jax.ShapeDtypeStruct((t, d), F32), jax.ShapeDtypeStruct((1, d), F32),
                       jax.ShapeDtypeStruct((t, dm), F32)],
            compiler_params=_params(("arbitrary",)),
        )(dhn, h, g, dout, w_below)

    n_head, n_body = split
    n_first = tr - n_head
    n_last = n_head + n_body - (nt - 1) * tr
    assert nt >= 2 and 0 < n_head < tr and 0 < n_last <= tr and n_head % SUBLANES == 0 and n_last % SUBLANES == 0

    def split_body(dhn_ref, h_ref, g_ref, dout_ref, body_ref, head_ref, dg_ref, stage, sems):
        i = pl.program_id(0)
        slot = i % 2

        def first_copy(sl):
            return pltpu.make_async_copy(stage.at[sl, pl.ds(n_head, n_first)], body_ref.at[pl.ds(0, n_first)], sems.at[sl])

        def middle_copy(sl, step):
            start = pl.multiple_of(step * tr - n_head, SUBLANES)
            return pltpu.make_async_copy(stage.at[sl], body_ref.at[pl.ds(start, tr)], sems.at[sl])

        def last_copy(sl):
            return pltpu.make_async_copy(stage.at[sl, pl.ds(0, n_last)],
                                         body_ref.at[pl.ds((nt - 1) * tr - n_head, n_last)], sems.at[sl])

        dh = row_grad(dhn_ref, h_ref, g_ref, dout_ref, dg_ref)

        @pl.when(i == 2)
        def _():
            first_copy(0).wait()

        @pl.when(i > 2)
        def _():
            middle_copy(slot, i - 2).wait()

        stage[slot] = dh

        @pl.when(i == 0)
        def _():
            head_ref[...] = stage[0, 0:n_head, :]
            first_copy(0).start()

        @pl.when((i > 0) & (i < nt - 1))
        def _():
            middle_copy(slot, i).start()

        @pl.when(i == nt - 1)
        def _():
            last = last_copy((nt - 1) % 2)
            last.start()
            if nt == 2:
                first_copy(0).wait()
            else:
                middle_copy((nt - 2) % 2, nt - 2).wait()
            last.wait()

    return pl.pallas_call(
        split_body, name=name + "_norm", grid=(nt,),
        in_specs=[rows, rows, one, rows],
        out_specs=[ANY, pl.BlockSpec((n_head, d), lambda i: (0, 0)), one],
        out_shape=[jax.ShapeDtypeStruct((n_body, d), F32), jax.ShapeDtypeStruct((n_head, d), F32),
                   jax.ShapeDtypeStruct((1, d), F32)],
        scratch_shapes=[pltpu.VMEM((2, tr, d), F32), pltpu.SemaphoreType.DMA((2,))],
        compiler_params=_params(("arbitrary",)),
    )(dhn, h, g, dout)


def _in_proj_dw(hn, du, s, name, after=None):
    t, d = hn.shape
    ns = du.shape[1] // s
    tmm = _col_tile(d, (1024, 512, 256))
    tn = _col_tile(ns, (768, 384, 128))
    nb = ns // tn

    def body(hn_ref, du_ref, o_ref):
        o_ref[...] = lax.dot_general(hn_ref[...], du_ref[...], TN_DIMS, preferred_element_type=F32)

    body, more_specs, more = _behind(body, 2, after)
    return pl.pallas_call(
        body, name=name, grid=(d // tmm, s * nb),
        in_specs=[pl.BlockSpec((t, tmm), lambda m, n: (0, m)),
                  pl.BlockSpec((t, tn), lambda m, n: (0, n))] + more_specs,
        out_specs=pl.BlockSpec((None, tmm, tn), lambda m, n: (n // nb, m, n % nb)),
        out_shape=jax.ShapeDtypeStruct((s, d, ns), F32),
        compiler_params=_params(("arbitrary", "arbitrary")),
    )(hn, du, *more)


def _out_proj_loss(h, y, w, tgt, g, n_meta, t_real, name):
    t, d = h.shape
    dm = y.shape[1]
    tm = 352 if t % 352 == 0 else 192

    def body(h_ref, y_ref, w_ref, t_ref, g_ref, dh_ref, loss_ref, dg_ref, dmix_ref):
        i = pl.program_id(0)

        @pl.when(i == 0)
        def _():
            loss_ref[...] = jnp.zeros_like(loss_ref)
            dg_ref[...] = jnp.zeros_like(dg_ref)

        x = h_ref[...] + jnp.dot(y_ref[...], w_ref[...], preferred_element_type=F32)
        gv = g_ref[...]
        r = lax.rsqrt(jnp.mean(x * x, axis=-1, keepdims=True) + RMS_EPS)
        xr = x * r
        rows = i * tm + lax.broadcasted_iota(jnp.int32, (tm, 1), 0)
        valid = (rows >= n_meta) & (rows < t_real)
        err = jnp.where(valid, xr * gv - t_ref[...], 0.0)
        loss_ref[...] += 0.5 * jnp.sum(jnp.mean(err * err, axis=-1, keepdims=True))
        dy = err * (1.0 / d)
        gd = dy * gv
        dot = jnp.mean(gd * x, axis=-1, keepdims=True)
        dh = r * gd - x * ((r * r * r) * dot)
        dh_ref[...] = dh
        dg_ref[...] += jnp.sum(dy * xr, axis=0, keepdims=True)
        dmix_ref[...] = lax.dot_general(dh.astype(BF16), w_ref[...], NT_DIMS, preferred_element_type=F32)

    rows = pl.BlockSpec((tm, d), lambda i: (i, 0))
    wide = pl.BlockSpec((tm, dm), lambda i: (i, 0))
    return pl.pallas_call(
        body, name=name, grid=(t // tm,),
        in_specs=[rows, wide, pl.BlockSpec((dm, d), lambda i: (0, 0)), rows, pl.BlockSpec((1, d), lambda i: (0, 0))],
        out_specs=[rows, pl.BlockSpec((1, LANES), lambda i: (0, 0)), pl.BlockSpec((1, d), lambda i: (0, 0)), wide],
        out_shape=[jax.ShapeDtypeStruct((t, d), F32), jax.ShapeDtypeStruct((1, LANES), F32),
                   jax.ShapeDtypeStruct((1, d), F32), jax.ShapeDtypeStruct((t, dm), F32)],
        compiler_params=_params(("arbitrary",)),
    )(h, y, w, tgt, g)


def _adamw_rows(rows, cols):
    for cand in (512, 256, 128, 64, 32, 16, 8):
        if rows % cand == 0 and cand * cols * 4 <= 2 * 1024 * 1024:
            return cand
    return rows


def _adamw_math(w_ref, g_ref, m_ref, v_ref, d_ref, nm_ref, nv_ref):
    gv = g_ref[...]
    m2 = ADAM_B1 * m_ref[...] + (1.0 - ADAM_B1) * gv
    v2 = ADAM_B2 * v_ref[...] + (1.0 - ADAM_B2) * (gv * gv)
    m_hat = m2 / (1.0 - ADAM_B1 ** ADAM_STEP)
    v_hat = v2 / (1.0 - ADAM_B2 ** ADAM_STEP)
    d_ref[...] = -ADAM_LR * (m_hat / (jnp.sqrt(v_hat) + ADAM_EPS) + ADAM_WD * w_ref[...])
    nm_ref[...] = m2
    nv_ref[...] = v2


def _adamw(w, g, m, v, name, after=None):
    shape = w.shape
    assert len(shape) >= 2 and w.size * 4 <= 2 * 1024 * 1024

    def body(*refs):
        _adamw_math(*refs)

    body, more_specs, more = _behind(body, 4, after)
    spec = pl.BlockSpec(shape, lambda i: (0,) * len(shape))
    return pl.pallas_call(
        body, name=name, grid=(1,),
        in_specs=[spec] * 4 + more_specs, out_specs=[spec] * 3,
        out_shape=[jax.ShapeDtypeStruct(shape, F32)] * 3,
        compiler_params=_params(("arbitrary",)),
    )(w, g, m, v, *more)


def _adamw_layer(w, g, m, v, layer, kept, name, after=None):
    nl, rows, cols = w.shape
    tr = _adamw_rows(rows, cols)
    n_kept = 0 if kept is None else 3

    def body(*refs):
        _adamw_math(*refs[:4], *refs[4 + n_kept:])

    body, more_specs, more = _behind(body, 4 + n_kept, after)
    lay = pl.BlockSpec((None, tr, cols), lambda i: (layer, i, 0))
    return pl.pallas_call(
        body, name=name, grid=(rows // tr,),
        in_specs=[lay, pl.BlockSpec((tr, cols), lambda i: (i, 0)), lay, lay] + [ANY] * n_kept + more_specs,
        out_specs=[lay] * 3,
        out_shape=[jax.ShapeDtypeStruct((nl, rows, cols), F32)] * 3,
        input_output_aliases={4 + k: k for k in range(n_kept)},
        compiler_params=_params(("arbitrary",)),
    )(w, g, m, v, *([] if kept is None else kept), *more)


def _pair_add(x, ra, c_idx, name):
    s, _, rows, cols = x.shape
    tr = _slab_rows(rows, cols)

    def body(c_ref, x_ref, r_ref, o_ref):
        o_ref[...] = (x_ref[...] + r_ref[...]).astype(BF16)

    return pl.pallas_call(
        body, name=name,
        grid_spec=pltpu.PrefetchScalarGridSpec(
            num_scalar_prefetch=1, grid=(s, rows // tr),
            in_specs=[pl.BlockSpec((None, None, tr, cols), lambda a, i, c_ref: (a, c_ref[0], i, 0)),
                      pl.BlockSpec((None, tr, cols), lambda a, i, c_ref: (a, i, 0))],
            out_specs=pl.BlockSpec((None, tr, cols), lambda a, i, c_ref: (a, i, 0))),
        out_shape=jax.ShapeDtypeStruct((s, rows, cols), BF16),
        compiler_params=_params(("arbitrary", "arbitrary")),
    )(c_idx, x, ra)


def _chip_sum(rc, p, where, n_slots, name):
    s, rows, cols = rc.shape
    tr = _slab_rows(rows, cols)

    def body(w_ref, x_ref, p_ref, o_ref):
        me = w_ref[0]
        total = jnp.where(me == 0, p_ref[...], x_ref[0]).astype(F32)
        for a in range(1, s):
            total = total + jnp.where(me == a, p_ref[...], x_ref[a]).astype(F32)
        o_ref[...] = total

    return pl.pallas_call(
        body, name=name,
        grid_spec=pltpu.PrefetchScalarGridSpec(
            num_scalar_prefetch=1, grid=(rows // tr,),
            in_specs=[pl.BlockSpec((s, tr, cols), lambda i, w_ref: (0, i, 0)),
                      pl.BlockSpec((None, tr, cols), lambda i, w_ref: (w_ref[0], i, 0))],
            out_specs=pl.BlockSpec((None, tr, cols), lambda i, w_ref: (w_ref[1], i, 0))),
        out_shape=jax.ShapeDtypeStruct((n_slots, rows, cols), F32),
        compiler_params=_params(("arbitrary",)),
    )(where, rc, p)


def _cast_place(w, layer, me_idx, name, after=None):
    _, rows, cols = w.shape
    tr = _slab_rows(rows, cols)

    def body(m_ref, w_ref, o_ref):
        o_ref[...] = w_ref[...].astype(BF16)

    body, more_specs, more = _behind(body, 2, after)
    return pl.pallas_call(
        body, name=name,
        grid_spec=pltpu.PrefetchScalarGridSpec(
            num_scalar_prefetch=1, grid=(rows // tr,),
            in_specs=[pl.BlockSpec((None, tr, cols), lambda i, m_ref: (layer, i, 0))] + more_specs,
            out_specs=pl.BlockSpec((None, tr, cols), lambda i, m_ref: (m_ref[0], i, 0))),
        out_shape=jax.ShapeDtypeStruct((N_CHIPS, rows, cols), BF16),
        compiler_params=_params(("arbitrary",)),
    )(me_idx, w, *more)


def _place():
    x, y, c = lax.axis_index("x"), lax.axis_index("y"), lax.axis_index("c")
    chips = [(1 - x, y), (x, 1 - y), (1 - x, 1 - y)]
    return x, y, c, chips


def _chip_index(cx, cy):
    return 2 * cx + cy


def _gather_copies(bufs, stage):
    x, y, c, chips = _place()
    me = _chip_index(x, y)
    copies = []
    for b in bufs:
        for chip in chips:
            src = _chip_index(*chip)
            if stage == 0:
                copies.append((b.at[me, c], (*chip, c), b.at[src, c]))
            else:
                copies.append((b.at[src, c], (x, y, 1 - c), b.at[src, 1 - c]))
    return copies


def _remote(ref, peer, ssem, rsem, k):
    return pltpu.make_async_remote_copy(src_ref=ref, dst_ref=ref, send_sem=ssem.at[k], recv_sem=rsem.at[k],
                                        device_id=peer, device_id_type=MESH)


def _gather_first(bufs, small):
    n = len(bufs)
    k = 3 * n

    def body(*refs):
        sm_ref = refs[n]
        b_refs, smg_ref = refs[n + 1:2 * n + 1], refs[2 * n + 1]
        lsem, ssem, rsem = refs[2 * n + 2:]
        x, y, c, chips = _place()
        me = _chip_index(x, y)
        local = pltpu.make_async_copy(sm_ref, smg_ref.at[me], lsem)
        local.start()
        first = _gather_copies(b_refs, 0)
        second = _gather_copies(b_refs, 1)
        started = []
        for i, (ref, peer, _) in enumerate(first):
            started.append(_remote(ref, peer, ssem, rsem, i))
        for j, chip in enumerate(chips):
            started.append(pltpu.make_async_remote_copy(
                src_ref=sm_ref, dst_ref=smg_ref.at[me], send_sem=ssem.at[2 * k + j], recv_sem=rsem.at[2 * k + j],
                device_id=(*chip, c), device_id_type=MESH))
        for cp in started:
            cp.start()
        for i, (_, peer, lands) in enumerate(first):
            _remote(lands, peer, ssem, rsem, i).wait_recv()
            ref, sib, _ = second[i]
            fwd = _remote(ref, sib, ssem, rsem, k + i)
            fwd.start()
            started.append(fwd)
        for i, (_, sib, lands) in enumerate(second):
            _remote(lands, sib, ssem, rsem, k + i).wait_recv()
        for j, chip in enumerate(chips):
            theirs = smg_ref.at[_chip_index(*chip)]
            pltpu.make_async_remote_copy(src_ref=theirs, dst_ref=theirs, send_sem=ssem.at[2 * k + j],
                                         recv_sem=rsem.at[2 * k + j], device_id=(*chip, c),
                                         device_id_type=MESH).wait_recv()
        for cp in started:
            cp.wait_send()
        local.wait()

    return pl.pallas_call(
        body, name="gather_first",
        in_specs=[ANY] * (n + 1), out_specs=[ANY] * (n + 1),
        out_shape=[jax.ShapeDtypeStruct(b.shape, b.dtype) for b in bufs]
        + [jax.ShapeDtypeStruct((N_CHIPS,) + small.shape, small.dtype)],
        input_output_aliases={i: i for i in range(n)},
        scratch_shapes=[pltpu.SemaphoreType.DMA, pltpu.SemaphoreType.DMA((2 * k + 3,)),
                        pltpu.SemaphoreType.DMA((2 * k + 3,))],
    )(*bufs, small)


HBM = pl.BlockSpec(memory_space=pltpu.HBM)
SEM = pl.BlockSpec(memory_space=pltpu.SEMAPHORE)
DATAFLOW = pltpu.SideEffectType.DATAFLOW_SIDE_EFFECTING


def _copies_start(bufs, plan, n_copies, name, after=None):
    n = len(bufs)
    extra = [] if after is None else [after]

    def body(*refs):
        refs = refs[:n] + refs[n + len(extra):]
        ssem, rsem = refs[n], refs[n + 1]
        b_refs, token = refs[n + 2:2 * n + 2], refs[2 * n + 2]
        copies = plan(b_refs)
        assert len(copies) == n_copies
        for i, (src, dst, peer, _) in enumerate(copies):
            pltpu.make_async_remote_copy(src_ref=src, dst_ref=dst, send_sem=ssem.at[i], recv_sem=rsem.at[i],
                                         device_id=peer, device_id_type=MESH).start()
        token[...] = jnp.zeros_like(token)

    return pl.pallas_call(
        body, name=name,
        out_shape=(pltpu.SemaphoreType.DMA((n_copies,)), pltpu.SemaphoreType.DMA((n_copies,)),
                   *[pltpu.HBM(b.shape, b.dtype) for b in bufs], jax.ShapeDtypeStruct((SUBLANES, LANES), F32)),
        in_specs=[HBM] * n + [ANY] * len(extra),
        out_specs=(SEM, SEM, *[HBM] * n, pl.BlockSpec(memory_space=pltpu.VMEM)),
        input_output_aliases={i: 2 + i for i in range(n)},
        compiler_params=pltpu.CompilerParams(has_side_effects=DATAFLOW),
    )(*[pltpu.with_memory_space_constraint(b, pltpu.HBM) for b in bufs], *extra)


def _copies_wait(bufs, ssem, rsem, after, plan, name):
    n = len(bufs)
    afters = list(after) if isinstance(after, (list, tuple)) else [after]

    def body(*refs):
        b_refs, ssem_ref, rsem_ref = refs[:n], refs[n], refs[n + 1]
        for i, (src, dst, peer, lands) in enumerate(plan(b_refs)):
            pltpu.make_async_remote_copy(src_ref=src, dst_ref=dst, send_sem=ssem_ref.at[i], recv_sem=rsem_ref.at[i],
                                         device_id=peer, device_id_type=MESH).wait_send()
            pltpu.make_async_remote_copy(src_ref=lands, dst_ref=lands, send_sem=ssem_ref.at[i],
                                         recv_sem=rsem_ref.at[i], device_id=peer, device_id_type=MESH).wait_recv()

    return pl.pallas_call(
        body, name=name,
        out_shape=tuple(pltpu.HBM(b.shape, b.dtype) for b in bufs),
        in_specs=[HBM] * n + [SEM, SEM] + [ANY] * len(afters), out_specs=tuple([HBM] * n),
        input_output_aliases={i: i for i in range(n)},
        compiler_params=pltpu.CompilerParams(has_side_effects=DATAFLOW),
    )(*bufs, ssem, rsem, *afters)


def _gather_plan(stage):
    return lambda refs: [(ref, ref, peer, lands) for ref, peer, lands in _gather_copies(refs, stage)]


def _swap_plan(refs):
    n = len(refs) // 2
    x, y, c, _ = _place()
    return [(refs[a].at[:, 1 - c], refs[n + a], (x, y, 1 - c), refs[n + a]) for a in range(n)]


def _scatter_plan(refs):
    n = len(refs) // 2
    x, y, c, chips = _place()
    me = _chip_index(x, y)
    return [(refs[a].at[_chip_index(*chip)], refs[n + a].at[me], (*chip, c), refs[n + a].at[_chip_index(*chip)])
            for a in range(n) for chip in chips]


def _pair_gather_plan(refs):
    x, y, c, _ = _place()
    return [(r.at[c], r.at[c], (x, y, 1 - c), r.at[1 - c]) for r in refs]


def _pair_swap(xs, name):
    n = len(xs)

    def body(*refs):
        x_refs, o_refs, ssem, rsem = refs[:n], refs[n:2 * n], refs[2 * n], refs[2 * n + 1]
        x, y, c, _ = _place()
        copies = [pltpu.make_async_remote_copy(src_ref=x_refs[a].at[:, 1 - c], dst_ref=o_refs[a],
                                               send_sem=ssem.at[a], recv_sem=rsem.at[a],
                                               device_id=(x, y, 1 - c), device_id_type=MESH) for a in range(n)]
        for cp in copies:
            cp.start()
        for cp in copies:
            cp.wait()

    return pl.pallas_call(
        body, name=name, in_specs=[ANY] * n, out_specs=[ANY] * n,
        out_shape=[jax.ShapeDtypeStruct((a.shape[0],) + a.shape[2:], a.dtype) for a in xs],
        scratch_shapes=[pltpu.SemaphoreType.DMA((n,)), pltpu.SemaphoreType.DMA((n,))],
    )(*xs)


def _chip_scatter(ps):
    n = len(ps)

    def body(*refs):
        p_refs, o_refs, ssem, rsem = refs[:n], refs[n:2 * n], refs[2 * n], refs[2 * n + 1]
        x, y, c, chips = _place()
        me = _chip_index(x, y)
        sends = []
        for a in range(n):
            for j, chip in enumerate(chips):
                sends.append(pltpu.make_async_remote_copy(
                    src_ref=p_refs[a].at[_chip_index(*chip)], dst_ref=o_refs[a].at[me],
                    send_sem=ssem.at[3 * a + j], recv_sem=rsem.at[3 * a + j],
                    device_id=(*chip, c), device_id_type=MESH))
        for cp in sends:
            cp.start()
        for a in range(n):
            for j, chip in enumerate(chips):
                src = _chip_index(*chip)
                pltpu.make_async_remote_copy(
                    src_ref=p_refs[a].at[src], dst_ref=o_refs[a].at[src],
                    send_sem=ssem.at[3 * a + j], recv_sem=rsem.at[3 * a + j],
                    device_id=(*chip, c), device_id_type=MESH).wait_recv()
        for cp in sends:
            cp.wait_send()

    return pl.pallas_call(
        body, name="chip_scatter", in_specs=[ANY] * n, out_specs=[ANY] * n,
        out_shape=[jax.ShapeDtypeStruct(a.shape, a.dtype) for a in ps],
        scratch_shapes=[pltpu.SemaphoreType.DMA((3 * n,)), pltpu.SemaphoreType.DMA((3 * n,))],
    )(*ps)


def _final_gather(fs, rep):
    n = len(fs)

    def body(*refs):
        o_refs, repo_ref = refs[n + 1:2 * n + 1], refs[2 * n + 1]
        ssem, rsem = refs[2 * n + 2:]
        x, y, c, chips = _place()
        slot = 4 * x + 2 * y + c
        copies = [pltpu.make_async_remote_copy(src_ref=o_refs[a].at[c], dst_ref=o_refs[a].at[c],
                                               send_sem=ssem.at[a], recv_sem=rsem.at[a],
                                               device_id=(x, y, 1 - c), device_id_type=MESH) for a in range(n)]
        peers = [(x, y, 1 - c)] + [(*chip, c) for chip in chips] + [(*chip, 1 - c) for chip in chips]
        for k, peer in enumerate(peers):
            copies.append(pltpu.make_async_remote_copy(src_ref=repo_ref.at[slot], dst_ref=repo_ref.at[slot],
                                                       send_sem=ssem.at[n + k], recv_sem=rsem.at[n + k],
                                                       device_id=peer, device_id_type=MESH))
        for cp in copies:
            cp.start()
        for a in range(n):
            pltpu.make_async_remote_copy(src_ref=o_refs[a].at[1 - c], dst_ref=o_refs[a].at[1 - c],
                                         send_sem=ssem.at[a], recv_sem=rsem.at[a],
                                         device_id=(x, y, 1 - c), device_id_type=MESH).wait_recv()
        for k, peer in enumerate(peers):
            px, py, pc = peer
            theirs = repo_ref.at[4 * px + 2 * py + pc]
            pltpu.make_async_remote_copy(src_ref=theirs, dst_ref=theirs, send_sem=ssem.at[n + k], recv_sem=rsem.at[n + k],
                                         device_id=peer, device_id_type=MESH).wait_recv()
        for cp in copies:
            cp.wait_send()

    return pl.pallas_call(
        body, name="final_gather", in_specs=[ANY] * (n + 1), out_specs=[ANY] * (n + 1),
        out_shape=[jax.ShapeDtypeStruct(a.shape, a.dtype) for a in fs] + [jax.ShapeDtypeStruct(rep.shape, rep.dtype)],
        input_output_aliases={k: k for k in range(n + 1)},
        scratch_shapes=[pltpu.SemaphoreType.DMA((n + 7,)), pltpu.SemaphoreType.DMA((n + 7,))],
    )(*fs, rep)


def _block_diag(w, gb):
    nh, hd, _ = w.shape
    per = gb // hd
    w4 = w.reshape(nh // per, per, hd, hd)
    eye = jnp.eye(per, dtype=w.dtype)
    return jnp.einsum("jaik,ab->jaibk", w4, eye).reshape(nh // per, gb, gb)


def _diag_blocks(dense, hd):
    nj, gb, _ = dense.shape
    per = gb // hd
    d5 = dense.reshape(nj, per, hd, per, hd)
    return jnp.stack([d5[:, a, :, a, :] for a in range(per)], axis=1).reshape(nj * per, hd, hd)


def _round_up(n, q):
    return (n + q - 1) // q * q


def kernel(x, meta, norm_g, w_in, conv_a_w, conv_a_b, lru_wr, lru_br, lru_wi, lru_bi, lru_lambda, conv_b_w, w_out, final_g, loss_target, m_meta, m_norm_g, m_w_in, m_conv_a_w, m_conv_a_b, m_lru_wr, m_lru_br, m_lru_wi, m_lru_bi, m_lru_lambda, m_conv_b_w, m_w_out, m_final_g, v_meta, v_norm_g, v_w_in, v_conv_a_w, v_conv_a_b, v_lru_wr, v_lru_br, v_lru_wi, v_lru_bi, v_lru_lambda, v_conv_b_w, v_w_out, v_final_g):
    weights = dict(meta=meta, norm_g=norm_g, w_in=w_in, conv_a_w=conv_a_w, conv_a_b=conv_a_b, lru_wr=lru_wr,
                   lru_br=lru_br, lru_wi=lru_wi, lru_bi=lru_bi, lru_lambda=lru_lambda, conv_b_w=conv_b_w,
                   w_out=w_out, final_g=final_g)
    mom1 = dict(meta=m_meta, norm_g=m_norm_g, w_in=m_w_in, conv_a_w=m_conv_a_w, conv_a_b=m_conv_a_b,
                lru_wr=m_lru_wr, lru_br=m_lru_br, lru_wi=m_lru_wi, lru_bi=m_lru_bi, lru_lambda=m_lru_lambda,
                conv_b_w=m_conv_b_w, w_out=m_w_out, final_g=m_final_g)
    mom2 = dict(meta=v_meta, norm_g=v_norm_g, w_in=v_w_in, conv_a_w=v_conv_a_w, conv_a_b=v_conv_a_b,
                lru_wr=v_lru_wr, lru_br=v_lru_br, lru_wi=v_lru_wi, lru_bi=v_lru_bi, lru_lambda=v_lru_lambda,
                conv_b_w=v_conv_b_w, w_out=v_w_out, final_g=v_final_g)
    names = list(weights)

    assert x.shape[0] == 1
    seq, d = x.shape[1], x.shape[2]
    n_meta, ds = meta.shape
    depth = norm_g.shape[0]
    c = lru_lambda.shape[1]
    nh, hd = lru_wr.shape[1], lru_wr.shape[2]
    ns = w_in.shape[2]
    dms = w_out.shape[1]
    cs = conv_a_w.shape[2]
    ka, kb = conv_a_w.shape[1], conv_b_w.shape[1]
    s = N_CHIPS
    assert depth == N_CORES and d == s * ds and c == s * cs and s * ns == 6 * c and s * dms == 2 * c
    gb = min(GATE_BLOCK, c)
    t_real = n_meta + seq
    t = _round_up(t_real, ROW_QUANTUM)
    my_c = lax.axis_index("c").astype(jnp.int32)
    my_chip = (2 * lax.axis_index("x") + lax.axis_index("y")).astype(jnp.int32)
    c_idx = my_c.reshape(1)
    chip_idx = my_chip.reshape(1)

    sm_rows = _round_up(n_meta + depth * SUBLANES, 2 * SUBLANES)
    small = jnp.zeros((sm_rows, ds), F32)
    small = small.at[0:n_meta, :].set(meta)
    for l in range(depth):
        base = n_meta + l * SUBLANES
        small = small.at[base:base + ka, 0:cs].set(conv_a_w[l])
        small = small.at[base + ka:base + ka + kb, 0:cs].set(conv_b_w[l])
    (small_g,) = _gather_first([], small)
    meta_full = jnp.transpose(small_g[:, 0:n_meta, :], (1, 0, 2)).reshape(n_meta, d)
    wa_full, wb_full = [], []
    for l in range(depth):
        base = n_meta + l * SUBLANES
        wa_full.append(jnp.transpose(small_g[:, base:base + ka, 0:cs], (1, 0, 2)).reshape(ka, c))
        wb_full.append(jnp.transpose(small_g[:, base + ka:base + ka + kb, 0:cs], (1, 0, 2)).reshape(kb, c))
    win0 = _cast_place(w_in, 0, chip_idx, "cast_w_in_0").reshape(s, 2, d // 2, ns)
    ssem_w, rsem_w, win0, token_w = _copies_start([win0], _gather_plan(0), 3, "gather_win0_ici_start", after=small_g)
    win_b = [None] + [_cast_place(w_in, l, chip_idx, f"cast_w_in_{l}", after=token_w).reshape(s, 2, d // 2, ns)
                      for l in range(1, depth)]
    wout_b = [_cast_place(w_out, l, chip_idx, f"cast_w_out_{l}", after=token_w).reshape(s, 2, dms // 2, d)
              for l in range(depth)]
    h = jnp.concatenate([meta_full, x[0], jnp.zeros((t - t_real, d), F32)], axis=0) + token_w[0, 0]
    tgt = jnp.concatenate([jnp.zeros((n_meta, d), F32), loss_target[0], jnp.zeros((t - t_real, d), F32)],
                          axis=0) + token_w[0, 0]
    u_own, hn_own = _norm_in_own(h, norm_g[0].reshape(1, d), win0.reshape(s, d, ns), chip_idx, "norm_in_0_own")
    (win0,) = _copies_wait([win0], ssem_w, rsem_w, [u_own, tgt] + win_b[1:] + wout_b, _gather_plan(0),
                           "gather_win0_ici_wait")
    ssem_w, rsem_w, win0, token_w = _copies_start([win0], _gather_plan(1), 3, "gather_win0_d2d_start")
    def travel(buf, stage, tag, after):
        return _copies_start([buf], _gather_plan(stage), 3, f"gather_{tag}_{'d2d' if stage else 'ici'}_start",
                             after=after)

    def arrived(state, stage, tag, after):
        (buf,) = _copies_wait([state[2]], state[0], state[1], after, _gather_plan(stage),
                              f"gather_{tag}_{'d2d' if stage else 'ici'}_wait")
        return buf

    on_wout0 = travel(wout_b[0], 0, "wout0", token_w)
    on_win1 = travel(win_b[1], 0, "win1", on_wout0[3])
    on_wout1 = travel(wout_b[1], 0, "wout1", on_win1[3])
    token = on_wout1[3]
    (win_b[0],) = _copies_wait([win0], ssem_w, rsem_w, token, _gather_plan(1), "gather_win0_d2d_wait")

    layer_w = []
    for l in range(depth):
        layer_w.append(dict(
            g=norm_g[l].reshape(1, d), wa=wa_full[l], ba=conv_a_b[l].reshape(1, c),
            wr=_block_diag(lru_wr[l], gb).astype(BF16), br=lru_br[l].reshape(1, c),
            wi=_block_diag(lru_wi[l], gb).astype(BF16), bi=lru_bi[l].reshape(1, c),
            lam=lru_lambda[l].reshape(1, c), wb=wb_full[l]))
    saved = []
    for l, lw in enumerate(layer_w):
        first = l == 0
        lw["win"] = win_b[l].reshape(s, d, ns)
        mixer_w = (lw["wa"], lw["ba"], lw["wr"], lw["br"], lw["wi"], lw["bi"], lw["lam"], lw["wb"])
        if first:
            u = _norm_in_rest(hn_own, lw["win"], u_own, chip_idx, "norm_in_0_rest", after=token)
            hn = hn_own
            on_wout0 = travel(arrived(on_wout0, 0, "wout0", u), 1, "wout0", None)
            wout_b[0] = arrived(on_wout0, 1, "wout0", on_wout0[3])
            lw["wout"] = wout_b[0].reshape(2 * c, d)
            y, hs, h_next, hn_next = _mix_fwd(u, *mixer_w, f"mix_fwd_{l}", proj=(h, lw["wout"], layer_w[1]["g"]))
            saved.append((h, u, hn, y, hs))
            h = h_next
            on_win1 = travel(arrived(on_win1, 0, "win1", y), 1, "win1", None)
            win_b[1] = arrived(on_win1, 1, "win1", on_win1[3])
            on_wout1 = travel(arrived(on_wout1, 0, "wout1", y), 1, "wout1", on_win1[3])
            token = on_wout1[3]
        else:
            hn = hn_next
            u = _in_proj(hn, lw["win"], f"norm_in_{l}", after=token)
            wout_b[1] = arrived(on_wout1, 1, "wout1", u)
            lw["wout"] = wout_b[1].reshape(2 * c, d)
            y, hs = _mix_fwd(u, *mixer_w, f"mix_fwd_{l}")
            saved.append((h, u, hn, y, hs))
            dh, loss_lanes, d_final_g, dy = _out_proj_loss(h, y, lw["wout"], tgt, final_g.reshape(1, d), n_meta,
                                                           t_real, f"out_proj_{l}_loss")
    loss = lax.psum(loss_lanes[0, 0], ("x", "y", "c"))

    to_core = jnp.stack([my_chip, my_c])
    grads = [None] * depth
    early = None
    for l in reversed(range(depth)):
        lw = layer_w[l]
        h_in, u, hn, y, hs = saved[l]
        token = early[-1] if early else None
        d_wout = _out_proj_dw(y, dh, f"out_proj_dw_{l}", after=token)
        if early:
            ssem, rsem, bufs, _ = early
            bufs = _copies_wait(bufs, ssem, rsem, d_wout, _swap_plan, "early_swap_wait")
            half = len(bufs) // 2
            sums = [_pair_add(a, b, c_idx, f"early_pair_add_{k}") for k, (a, b) in enumerate(zip(bufs[:half], bufs[half:]))]
            lands = [lax.empty(p.shape, p.dtype) for p in sums]
            ssem, rsem, *bufs, token = _copies_start(sums + lands, _scatter_plan, 3 * half, "early_scatter_start")
        du, dsm, d_wr, d_wi = _mix_bwd(u, hs, dy, lw["wa"], lw["ba"], lw["wr"], lw["br"], lw["wi"], lw["bi"],
                                       lw["lam"], lw["wb"], f"mix_bwd_{l}", after=token)
        if early:
            bufs = _copies_wait(bufs, ssem, rsem, du, _scatter_plan, "early_scatter_wait")
            halves = [_chip_sum(rc, p, to_core, N_CORES, f"early_chip_sum_{k}")
                      for k, (p, rc) in enumerate(zip(bufs[:half], bufs[half:]))]
            ssem, rsem, *bufs, token = _copies_start(halves, _pair_gather_plan, half, "early_gather_start")
        d_win = _in_proj_dw(hn, du, s, f"in_proj_dw_{l}", after=token)
        srcs = [d_win.reshape(s, 2, d // 2, ns), d_wout.reshape(s, 2, dms // 2, d)]
        if early:
            early_full = _copies_wait(bufs, ssem, rsem, d_win, _pair_gather_plan, "early_gather_wait")
            lands = [lax.empty((a.shape[0],) + a.shape[2:], a.dtype) for a in srcs]
            ssem, rsem, *bufs, token = _copies_start(srcs + lands, _swap_plan, len(srcs), "late_swap_start")
            last = depth - 1
            early_grad = dict(w_in=early_full[0].reshape(d, ns), w_out=early_full[1].reshape(dms, d))
            early_step = {n: _adamw_layer(weights[n], early_grad[n], mom1[n], mom2[n], last, None,
                                          f"adamw_{n}_{last}", after=token) for n in ("w_in", "w_out")}
            bufs = _copies_wait(bufs, ssem, rsem, [o[0] for o in early_step.values()], _swap_plan, "late_swap_wait")
            late_sums = [_pair_add(a, b, c_idx, f"pair_add_{k}")
                         for k, (a, b) in enumerate(zip(bufs[:len(srcs)], bufs[len(srcs):]))]
            lands = [lax.empty(p.shape, p.dtype) for p in late_sums]
            ssem, rsem, *bufs, token = _copies_start(late_sums + lands, _scatter_plan, 3 * len(srcs), "late_scatter_start")
        if l > 0:
            dh, d_g, dy = _in_proj_bwd(du, lw["win"], h_in, lw["g"], dh, f"in_proj_bwd_{l}", after=token,
                                       w_below=layer_w[l - 1]["wout"])
        else:
            grad_x, d_meta, d_g = _in_proj_bwd(du, lw["win"], h_in, lw["g"], dh, f"in_proj_bwd_{l}", after=token,
                                               split=(n_meta, seq))
        if early:
            bufs = _copies_wait(bufs, ssem, rsem, grad_x, _scatter_plan, "late_scatter_wait")
            late_reduced = [_chip_sum(rc, p, to_core, N_CORES, f"chip_sum_{k}")
                            for k, (p, rc) in enumerate(zip(bufs[:len(srcs)], bufs[len(srcs):]))]
        grads[l] = dict(dsm=dsm, wr=_diag_blocks(d_wr, hd), wi=_diag_blocks(d_wi, hd), g=d_g)
        if l == depth - 1:
            lands = [lax.empty((a.shape[0],) + a.shape[2:], a.dtype) for a in srcs]
            ssem, rsem, *bufs, token = _copies_start(srcs + lands, _swap_plan, len(srcs), "early_swap_start")
            early = (ssem, rsem, bufs, token)
        else:
            early = None
    grad_x = grad_x[None]

    sharded = []
    sp = jnp.zeros((sm_rows, s, ds), F32)
    sp = sp.at[0:n_meta].set(d_meta.reshape(n_meta, s, ds))
    for l in range(depth):
        base = n_meta + l * SUBLANES
        dsm = grads[l]["dsm"]
        sp = sp.at[base:base + ka, :, 0:cs].set(dsm[ROW_DWA:ROW_DWA + ka].reshape(ka, s, cs))
        sp = sp.at[base + ka:base + ka + kb, :, 0:cs].set(dsm[ROW_DWB:ROW_DWB + kb].reshape(kb, s, cs))
    sharded.append(jnp.transpose(sp, (1, 0, 2)).reshape(s, 2, sm_rows // 2, ds))
    rep_parts = [jnp.concatenate([grads[l]["g"].reshape(-1) for l in range(depth)]), d_final_g.reshape(-1)]
    for row in (ROW_DBA, ROW_DBR, ROW_DBI, ROW_DLAM):
        rep_parts.append(jnp.concatenate([grads[l]["dsm"][row] for l in range(depth)]))
    rep_parts.append(jnp.concatenate([grads[l]["wr"].reshape(-1) for l in range(depth)]))
    rep_parts.append(jnp.concatenate([grads[l]["wi"].reshape(-1) for l in range(depth)]))
    rep_sizes = [p.shape[0] for p in rep_parts]
    piece = _round_up(-(-sum(rep_sizes) // (s * 2)), 2 * SUBLANES * LANES)
    flat = jnp.concatenate(rep_parts + [jnp.zeros((s * 2 * piece - sum(rep_sizes),), F32)])
    sharded.append(flat.reshape(s, 2, piece // LANES, LANES))

    from_sibling = _pair_swap(sharded, "small_pair_swap")
    pair_sums = [_pair_add(a, b, c_idx, f"small_pair_add_{k}") for k, (a, b) in enumerate(zip(sharded, from_sibling))]
    by_chip = _chip_scatter(pair_sums)
    to_device = jnp.stack([my_chip, 2 * my_chip + my_c])
    reduced_sp = _chip_sum(by_chip[0], pair_sums[0], to_core, N_CORES, "small_chip_sum")
    reduced_rep = _chip_sum(by_chip[1], pair_sums[1], to_device, N_CHIPS * N_CORES, "chip_sum_rep")
    sp_full, rep_all = _final_gather([reduced_sp], reduced_rep)
    ssem, rsem, *bufs, token = _copies_start(late_reduced, _pair_gather_plan, len(late_reduced), "late_gather_start",
                                             after=rep_all)
    g_sp = sp_full.reshape(sm_rows, ds)
    rep_flat = rep_all.reshape(-1)
    rep_out, off = [], 0
    for n in rep_sizes:
        rep_out.append(rep_flat[off:off + n])
        off += n
    grad = dict(
        meta=g_sp[0:n_meta],
        norm_g=rep_out[0].reshape(depth, d),
        conv_a_w=jnp.stack([g_sp[n_meta + l * SUBLANES:n_meta + l * SUBLANES + ka, 0:cs] for l in range(depth)]),
        conv_a_b=rep_out[2].reshape(depth, c),
        lru_wr=rep_out[6].reshape(depth, nh, hd, hd),
        lru_br=rep_out[3].reshape(depth, c),
        lru_wi=rep_out[7].reshape(depth, nh, hd, hd),
        lru_bi=rep_out[4].reshape(depth, c),
        lru_lambda=rep_out[5].reshape(depth, c),
        conv_b_w=jnp.stack([g_sp[n_meta + l * SUBLANES + ka:n_meta + l * SUBLANES + ka + kb, 0:cs]
                            for l in range(depth)]),
        final_g=rep_out[1].reshape(d),
    )

    delta, new_m, new_v = {}, {}, {}
    for n in grad:
        shape = weights[n].shape
        as_block = shape if len(shape) > 1 else (1,) + shape
        out = _adamw(weights[n].reshape(as_block), grad[n].reshape(as_block), mom1[n].reshape(as_block),
                     mom2[n].reshape(as_block), f"adamw_{n}", after=token)
        delta[n], new_m[n], new_v[n] = (o.reshape(shape) for o in out)
    full = _copies_wait(bufs, ssem, rsem, [delta[n] for n in grad], _pair_gather_plan, "late_gather_wait")
    g_win = [full[0].reshape(d, ns), early_full[0].reshape(d, ns)]
    g_wout = [full[1].reshape(dms, d), early_full[1].reshape(dms, d)]
    grad["w_in"] = jnp.stack(g_win)
    grad["w_out"] = jnp.stack(g_wout)
    for n, g_first in (("w_in", g_win[0]), ("w_out", g_wout[0])):
        delta[n], new_m[n], new_v[n] = _adamw_layer(weights[n], g_first, mom1[n], mom2[n], 0, early_step[n],
                                                    f"adamw_{n}_0")

    return (loss, grad_x, *[grad[n] for n in names], *[delta[n] for n in names],
            *[new_m[n] for n in names], *[new_v[n] for n in names])
```

```python
import jax
import jax.numpy as jnp
from jax import lax
from jax.experimental import pallas as pl
from jax.experimental.pallas import tpu as pltpu

F32 = jnp.float32
BF16 = jnp.bfloat16

RMS_EPS = 1e-6
LRU_C = 8.0
ADAM_LR = 0.001
ADAM_B1 = 0.9
ADAM_B2 = 0.999
ADAM_EPS = 1e-08
ADAM_WD = 0.01
ADAM_STEP = 10

N_CHIPS = 4
N_CORES = 2
VMEM_LIMIT_BYTES = 56 * 1024 * 1024
SUBLANES = 8
LANES = 128
ROW_QUANTUM = 384
MIX_CHUNK = 192
SCAN_UNROLL = 4
GATE_BLOCK = 256
MESH = pl.DeviceIdType.MESH
ANY = pl.BlockSpec(memory_space=pl.ANY)

NT_DIMS = (((1,), (1,)), ((), ()))
TN_DIMS = (((0,), (0,)), ((), ()))


def _params(sem):
    return pltpu.CompilerParams(dimension_semantics=sem, vmem_limit_bytes=VMEM_LIMIT_BYTES)


def _sig(x):
    return 0.5 * jnp.tanh(0.5 * x) + 0.5


def _row_tile(t):
    return 704 if t % 704 == 0 else 192


def _col_tile(n, prefs):
    for p in prefs:
        if n % p == 0:
            return p
    return n


def _slab_rows(rows, cols):
    if rows * cols * 4 <= 1024 * 1024:
        return rows
    return _col_tile(rows, (256, 128, 64, 32, 16))


def _norm_in_own(h, g, wg, me_idx, name):
    t, d = h.shape
    s, _, ns = wg.shape
    tm = 1408 if t % 1408 == 0 else _row_tile(t)
    tn = _col_tile(ns, (768, 384, 128))
    nb = ns // tn

    def body(m_ref, h_ref, g_ref, w_ref, u_ref, hn_ref):
        @pl.when(pl.program_id(1) == 0)
        def _():
            x = h_ref[...]
            r = lax.rsqrt(jnp.mean(x * x, axis=-1, keepdims=True) + RMS_EPS)
            hn_ref[...] = ((x * r) * g_ref[...]).astype(BF16)

        u_ref[...] = jnp.dot(hn_ref[...], w_ref[...], preferred_element_type=F32)

    return pl.pallas_call(
        body, name=name,
        grid_spec=pltpu.PrefetchScalarGridSpec(
            num_scalar_prefetch=1, grid=(t // tm, nb),
            in_specs=[pl.BlockSpec((tm, d), lambda i, n, m: (i, 0)),
                      pl.BlockSpec((1, d), lambda i, n, m: (0, 0)),
                      pl.BlockSpec((None, d, tn), lambda i, n, m: (m[0], 0, n))],
            out_specs=[pl.BlockSpec((tm, tn), lambda i, n, m: (i, m[0] * nb + n)),
                       pl.BlockSpec((tm, d), lambda i, n, m: (i, 0))]),
        out_shape=[jax.ShapeDtypeStruct((t, s * ns), F32), jax.ShapeDtypeStruct((t, d), BF16)],
        compiler_params=_params(("arbitrary", "arbitrary")),
    )(me_idx, h, g, wg)


def _norm_in_rest(hn, wg, u, me_idx, name, after=None):
    t, d = hn.shape
    s, _, ns = wg.shape
    tm = 1408 if t % 1408 == 0 else _row_tile(t)
    tn = _col_tile(ns, (1536, 768, 384, 128))
    nb = ns // tn

    def body(m_ref, hn_ref, w_ref, u_in, u_ref):
        del u_in
        u_ref[...] = jnp.dot(hn_ref[...], w_ref[...], preferred_element_type=F32)

    def shard(n, m):
        return (m[0] + 1 + n // nb) % s

    body, more_specs, more = _behind(body, 4, after)
    return pl.pallas_call(
        body, name=name,
        grid_spec=pltpu.PrefetchScalarGridSpec(
            num_scalar_prefetch=1, grid=(t // tm, (s - 1) * nb),
            in_specs=[pl.BlockSpec((tm, d), lambda i, n, m: (i, 0)),
                      pl.BlockSpec((None, d, tn), lambda i, n, m: (shard(n, m), 0, n % nb)),
                      ANY] + more_specs,
            out_specs=pl.BlockSpec((tm, tn), lambda i, n, m: (i, shard(n, m) * nb + n % nb))),
        out_shape=jax.ShapeDtypeStruct(u.shape, u.dtype),
        input_output_aliases={3: 0},
        compiler_params=_params(("arbitrary", "arbitrary")),
    )(me_idx, hn, wg, u, *more)


def _decay_consts(lam):
    z = -lam
    e = jnp.exp(-jnp.abs(z))
    u = 1.0 + e
    log1p_e = jnp.where(u == 1.0, e, jnp.log(u) * (e / (u - 1.0)))
    sp = jnp.maximum(z, 0.0) + log1p_e
    return -LRU_C * sp, LRU_C * _sig(z)


def _gates(xc, wr_ref, br_ref, wi_ref, bi_ref, c8, j, gb):
    sl = slice(j * gb, (j + 1) * gb)
    x16 = xc.astype(BF16)
    r = _sig(jnp.dot(x16, wr_ref[j], preferred_element_type=F32) + br_ref[:, sl])
    ig = _sig(jnp.dot(x16, wi_ref[j], preferred_element_type=F32) + bi_ref[:, sl])
    la = c8[:, sl] * r
    a = jnp.exp(la)
    sq = jnp.sqrt(-jnp.tanh(la) * (a * a + 1.0))
    return r, ig, a, sq


def _mix_fwd(u, wa, ba, wr, br, wi, bi, lam, wb, name, proj=None):
    t = u.shape[0]
    c = u.shape[1] // 6
    tc = MIX_CHUNK
    gb = wr.shape[1]
    nblk = c // gb
    ka, kb = wa.shape[0], wb.shape[0]
    n_proj = 0 if proj is None else 3

    def body(*refs):
        u_ref, wa_ref, ba_ref, wr_ref, br_ref, wi_ref, bi_ref, lam_ref, wb_ref = refs[:9]
        outs = refs[9 + n_proj:]
        y_ref, hs_ref, gt_ref = outs[:3]
        xa_ext, v_ext, xc_s, a_s, b_s, carry_s = outs[-6:]

        @pl.when(pl.program_id(0) == 0)
        def _():
            xa_ext[0:SUBLANES, :] = jnp.zeros((SUBLANES, c), F32)
            v_ext[0:SUBLANES, :] = jnp.zeros((SUBLANES, c), F32)
            carry_s[...] = jnp.zeros_like(carry_s)

        xa_ext[SUBLANES:SUBLANES + tc, :] = u_ref[:, 0:c]
        xc = ba_ref[...]
        for k in range(ka):
            xc = xc + wa_ref[pl.ds(k, 1), :] * xa_ext[pl.ds(SUBLANES - (ka - 1) + k, tc), :]
        xc_s[...] = xc
        c8, _ = _decay_consts(lam_ref[...])
        for j in range(nblk):
            sl = slice(j * gb, (j + 1) * gb)
            xcj = xc_s[:, sl]
            r, ig, a, sq = _gates(xcj, wr_ref, br_ref, wi_ref, bi_ref, c8, j, gb)
            a_s[:, sl] = a
            b_s[:, sl] = sq * (ig * xcj)
            for k, val in enumerate((r, ig, a, sq)):
                gt_ref[:, k * c + j * gb:k * c + (j + 1) * gb] = val

        row = lax.broadcasted_iota(jnp.int32, (SUBLANES, c), 0)

        def scan_step(j, _):
            off = pl.multiple_of(j * SUBLANES, SUBLANES)
            av = a_s[pl.ds(off, SUBLANES), :]
            bv = b_s[pl.ds(off, SUBLANES), :]
            for d in (1, 2, 4):
                keep = row >= d
                bsh = jnp.where(keep, pltpu.roll(bv, d, axis=0), 0.0)
                ash = jnp.where(keep, pltpu.roll(av, d, axis=0), 1.0)
                bv = av * bsh + bv
                av = av * ash
            hv = av * carry_s[...] + bv
            hs_ref[pl.ds(off, SUBLANES), :] = hv
            carry_s[...] = hs_ref[pl.ds(off + SUBLANES - 1, 1), :]
            return 0

        lax.fori_loop(0, tc // SUBLANES, scan_step, 0, unroll=SCAN_UNROLL)

        ga = u_ref[:, c:2 * c]
        y_ref[:, 0:c] = (hs_ref[...] * (ga * _sig(ga))).astype(BF16)

        v_ext[SUBLANES:SUBLANES + tc, :] = u_ref[:, 3 * c:4 * c] * u_ref[:, 4 * c:5 * c]
        cv = wb_ref[pl.ds(0, 1), :] * v_ext[pl.ds(SUBLANES - (kb - 1), tc), :]
        for k in range(1, kb):
            cv = cv + wb_ref[pl.ds(k, 1), :] * v_ext[pl.ds(SUBLANES - (kb - 1) + k, tc), :]
        gbv = u_ref[:, 5 * c:6 * c]
        y_ref[:, c:2 * c] = (u_ref[:, 2 * c:3 * c] * cv * (gbv * _sig(gbv))).astype(BF16)

        xa_ext[0:SUBLANES, :] = xa_ext[tc:tc + SUBLANES, :]
        v_ext[0:SUBLANES, :] = v_ext[tc:tc + SUBLANES, :]

        if proj is not None:
            h_ref, wout_ref, g_ref = refs[9:12]
            ho_ref, hn_ref = outs[3:5]
            x = h_ref[...] + jnp.dot(y_ref[...], wout_ref[...], preferred_element_type=F32)
            ho_ref[...] = x
            r = lax.rsqrt(jnp.mean(x * x, axis=-1, keepdims=True) + RMS_EPS)
            hn_ref[...] = ((x * r) * g_ref[...]).astype(BF16)

    full = lambda shape: pl.BlockSpec(shape, lambda i: (0,) * len(shape))
    rows = lambda width: pl.BlockSpec((tc, width), lambda i: (i, 0))
    more_in, more_specs, more_out_specs, more_out = [], [], [], []
    if proj is not None:
        h, wout, g_next = proj
        d = h.shape[1]
        more_in = [h, wout, g_next]
        more_specs = [rows(d), full(wout.shape), full(g_next.shape)]
        more_out_specs = [rows(d), rows(d)]
        more_out = [jax.ShapeDtypeStruct((t, d), F32), jax.ShapeDtypeStruct((t, d), BF16)]
    return pl.pallas_call(
        body, name=name, grid=(t // tc,),
        in_specs=[rows(6 * c), full(wa.shape), full(ba.shape), full(wr.shape), full(br.shape),
                  full(wi.shape), full(bi.shape), full(lam.shape), full(wb.shape)] + more_specs,
        out_specs=[rows(2 * c), rows(c), rows(4 * c)] + more_out_specs,
        out_shape=[jax.ShapeDtypeStruct((t, 2 * c), BF16), jax.ShapeDtypeStruct((t, c), F32),
                   jax.ShapeDtypeStruct((t, 4 * c), F32)] + more_out,
        scratch_shapes=[pltpu.VMEM((tc + SUBLANES, c), F32), pltpu.VMEM((tc + SUBLANES, c), F32),
                        pltpu.VMEM((tc, c), F32), pltpu.VMEM((tc, c), F32), pltpu.VMEM((tc, c), F32),
                        pltpu.VMEM((1, c), F32)],
        compiler_params=_params(("arbitrary",)),
    )(u, wa, ba, wr, br, wi, bi, lam, wb, *more_in)


ROW_DWA = 0
ROW_DBA = 4
ROW_DBR = 5
ROW_DBI = 6
ROW_DLAM = 7
ROW_DWB = 8
SMALL_ROWS = 16


def _mix_bwd(u, hs, dy, gates, wa, ba, wr, wi, lam, wb, name, after=None):
    t = u.shape[0]
    c = u.shape[1] // 6
    tc = MIX_CHUNK
    nt = t // tc
    gb = wr.shape[1]
    nblk = c // gb
    ka, kb = wa.shape[0], wb.shape[0]
    assert ka <= ROW_DBA and kb <= SMALL_ROWS - ROW_DWB
    hb = tc // SUBLANES

    def body(u_ref, uh_ref, hs_ref, hsh_ref, dy_ref, gt_ref, wa_ref, ba_ref, wr_ref, wi_ref, lam_ref, wb_ref,
             du_ref, dsm_ref, dwr_ref, dwi_ref,
             xa_ext, v_ext, hs_ext, a_ext, ds_ext, dxc_ext, dcv_ext, xc_s, g_s, an_s):
        i = pl.program_id(0)
        chunk = nt - 1 - i
        tail = slice(tc, tc + SUBLANES)
        head = slice(0, SUBLANES)

        @pl.when(i == 0)
        def _():
            zero = jnp.zeros((SUBLANES, c), F32)
            a_ext[tail, :] = zero
            ds_ext[tail, :] = zero
            dxc_ext[tail, :] = zero
            dcv_ext[tail, :] = zero
            dsm_ref[...] = jnp.zeros_like(dsm_ref)
            dwr_ref[...] = jnp.zeros_like(dwr_ref)
            dwi_ref[...] = jnp.zeros_like(dwi_ref)

        prev = jnp.where(chunk > 0, 1.0, 0.0)
        xa_ext[head, :] = uh_ref[:, 0:c] * prev
        xa_ext[SUBLANES:SUBLANES + tc, :] = u_ref[:, 0:c]
        v_ext[head, :] = uh_ref[:, 3 * c:4 * c] * uh_ref[:, 4 * c:5 * c] * prev
        v_ext[SUBLANES:SUBLANES + tc, :] = u_ref[:, 3 * c:4 * c] * u_ref[:, 4 * c:5 * c]
        hs_ext[head, :] = hsh_ref[...] * prev
        hs_ext[SUBLANES:SUBLANES + tc, :] = hs_ref[...]

        xc = ba_ref[...]
        for k in range(ka):
            xc = xc + wa_ref[pl.ds(k, 1), :] * xa_ext[pl.ds(SUBLANES - (ka - 1) + k, tc), :]
        xc_s[...] = xc
        c8, dc8 = _decay_consts(lam_ref[...])
        a_ext[0:tc, :] = gt_ref[:, 2 * c:3 * c]

        ga = u_ref[:, c:2 * c]
        sga = _sig(ga)
        g_s[...] = dy_ref[:, 0:c] * (ga * sga)
        an_s[...] = a_ext[pl.ds(1, tc), :]

        row = lax.broadcasted_iota(jnp.int32, (SUBLANES, c), 0)

        def scan_step(j, _):
            off = pl.multiple_of(tc - SUBLANES - j * SUBLANES, SUBLANES)
            av = an_s[pl.ds(off, SUBLANES), :]
            bv = g_s[pl.ds(off, SUBLANES), :]
            for d in (1, 2, 4):
                keep = row < SUBLANES - d
                bsh = jnp.where(keep, pltpu.roll(bv, SUBLANES - d, axis=0), 0.0)
                ash = jnp.where(keep, pltpu.roll(av, SUBLANES - d, axis=0), 1.0)
                bv = av * bsh + bv
                av = av * ash
            ds_ext[pl.ds(off, SUBLANES), :] = av * ds_ext[pl.ds(off + SUBLANES, 1), :] + bv
            return 0

        lax.fori_loop(0, tc // SUBLANES, scan_step, 0, unroll=SCAN_UNROLL)

        def acc(row_index, val):
            dsm_ref[pl.ds(row_index, 1), :] += jnp.sum(val, axis=0, keepdims=True)

        def acc_block(row_index, sl, val):
            dsm_ref[pl.ds(row_index, 1), sl] += jnp.sum(val, axis=0, keepdims=True)

        for j in range(nblk):
            sl = slice(j * gb, (j + 1) * gb)
            ds = ds_ext[0:tc, sl]
            hprev = hs_ext[pl.ds(SUBLANES - 1, tc), sl]
            a = a_ext[0:tc, sl]
            r = gt_ref[:, j * gb:(j + 1) * gb]
            ig = gt_ref[:, c + j * gb:c + (j + 1) * gb]
            sq = gt_ref[:, 3 * c + j * gb:3 * c + (j + 1) * gb]
            xcj = xc_s[:, sl]
            t1 = ds * xcj
            dla = (ds * hprev) * a - (t1 * ig) * ((a * a) * lax.rsqrt(sq * sq))
            acc_block(ROW_DLAM, sl, dla * r)
            dpr = (dla * c8[:, sl]) * (r * (1.0 - r))
            dpi = (t1 * sq) * (ig * (1.0 - ig))
            acc_block(ROW_DBR, sl, dpr)
            acc_block(ROW_DBI, sl, dpi)
            p16 = dpr.astype(BF16)
            q16 = dpi.astype(BF16)
            x16 = xcj.astype(BF16)
            dwr_ref[j] += lax.dot_general(x16, p16, TN_DIMS, preferred_element_type=F32)
            dwi_ref[j] += lax.dot_general(x16, q16, TN_DIMS, preferred_element_type=F32)
            dxc = (ds * (sq * ig)
                   + lax.dot_general(p16, wr_ref[j], NT_DIMS, preferred_element_type=F32)
                   + lax.dot_general(q16, wi_ref[j], NT_DIMS, preferred_element_type=F32))
            dxc_ext[0:tc, sl] = dxc
            acc_block(ROW_DBA, sl, dxc)

        dsilu_a = sga * (1.0 + ga * (1.0 - sga))
        du_ref[:, c:2 * c] = (dy_ref[:, 0:c] * hs_ref[...] * dsilu_a).astype(BF16)

        dxc = dxc_ext[0:tc, :]
        dxa = wa_ref[pl.ds(ka - 1, 1), :] * dxc
        acc(ROW_DWA + ka - 1, dxc * xa_ext[SUBLANES:SUBLANES + tc, :])
        for k in range(ka - 1):
            acc(ROW_DWA + k, dxc * xa_ext[pl.ds(SUBLANES - (ka - 1) + k, tc), :])
            dxa = dxa + wa_ref[pl.ds(k, 1), :] * dxc_ext[pl.ds(ka - 1 - k, tc), :]
        du_ref[:, 0:c] = dxa.astype(BF16)

        cv = wb_ref[pl.ds(0, 1), :] * v_ext[pl.ds(SUBLANES - (kb - 1), tc), :]
        for k in range(1, kb):
            cv = cv + wb_ref[pl.ds(k, 1), :] * v_ext[pl.ds(SUBLANES - (kb - 1) + k, tc), :]
        gbv = u_ref[:, 5 * c:6 * c]
        sgb = _sig(gbv)
        silu_b = gbv * sgb
        dyb = dy_ref[:, c:2 * c]
        gB = u_ref[:, 2 * c:3 * c]
        du_ref[:, 2 * c:3 * c] = (dyb * cv * silu_b).astype(BF16)
        du_ref[:, 5 * c:6 * c] = (dyb * gB * cv * (sgb * (1.0 + gbv * (1.0 - sgb)))).astype(BF16)
        dcv = dyb * gB * silu_b
        dcv_ext[0:tc, :] = dcv
        dv = wb_ref[pl.ds(kb - 1, 1), :] * dcv
        acc(ROW_DWB + kb - 1, dcv * v_ext[SUBLANES:SUBLANES + tc, :])
        for k in range(kb - 1):
            acc(ROW_DWB + k, dcv * v_ext[pl.ds(SUBLANES - (kb - 1) + k, tc), :])
            dv = dv + wb_ref[pl.ds(k, 1), :] * dcv_ext[pl.ds(kb - 1 - k, tc), :]
        du_ref[:, 3 * c:4 * c] = (dv * u_ref[:, 4 * c:5 * c]).astype(BF16)
        du_ref[:, 4 * c:5 * c] = (dv * u_ref[:, 3 * c:4 * c]).astype(BF16)

        a_ext[tail, :] = a_ext[head, :]
        ds_ext[tail, :] = ds_ext[head, :]
        dxc_ext[tail, :] = dxc_ext[head, :]
        dcv_ext[tail, :] = dcv_ext[head, :]

        @pl.when(i == nt - 1)
        def _():
            dsm_ref[pl.ds(ROW_DLAM, 1), :] = dsm_ref[pl.ds(ROW_DLAM, 1), :] * dc8

    full = lambda shape: pl.BlockSpec(shape, lambda i: (0,) * len(shape))
    rev = lambda i: (nt - 1 - i, 0)
    halo = lambda i: (jnp.maximum((nt - 1 - i) * hb - 1, 0), 0)
    ext = pltpu.VMEM((tc + SUBLANES, c), F32)
    blk = pltpu.VMEM((tc, c), F32)
    body, more_specs, more = _behind(body, 12, after)
    return pl.pallas_call(
        body, name=name, grid=(nt,),
        in_specs=[pl.BlockSpec((tc, 6 * c), rev), pl.BlockSpec((SUBLANES, 6 * c), halo),
                  pl.BlockSpec((tc, c), rev), pl.BlockSpec((SUBLANES, c), halo),
                  pl.BlockSpec((tc, 2 * c), rev), pl.BlockSpec((tc, 4 * c), rev),
                  full(wa.shape), full(ba.shape), full(wr.shape), full(wi.shape), full(lam.shape),
                  full(wb.shape)] + more_specs,
        out_specs=[pl.BlockSpec((tc, 6 * c), rev), full((SMALL_ROWS, c)), full(wr.shape), full(wi.shape)],
        out_shape=[jax.ShapeDtypeStruct((t, 6 * c), BF16), jax.ShapeDtypeStruct((SMALL_ROWS, c), F32),
                   jax.ShapeDtypeStruct(wr.shape, F32), jax.ShapeDtypeStruct(wi.shape, F32)],
        scratch_shapes=[ext] * 7 + [blk] * 3,
        compiler_params=_params(("arbitrary",)),
    )(u, u, hs, hs, dy, gates, wa, ba, wr, wi, lam, wb, *more)


def _behind(body, n_in, after):
    if after is None:
        return body, [], []
    return (lambda *refs: body(*refs[:n_in], *refs[n_in + 1:])), [ANY], [after]


def _in_proj(hn, wg, name, after=None):
    t, d = hn.shape
    s, _, ns = wg.shape
    tm = 1408 if t % 1408 == 0 else _row_tile(t)

    def body(hn_ref, w_ref, u_ref):
        u_ref[...] = jnp.dot(hn_ref[...], w_ref[...], preferred_element_type=F32)

    body, more_specs, more = _behind(body, 2, after)
    return pl.pallas_call(
        body, name=name, grid=(t // tm, s),
        in_specs=[pl.BlockSpec((tm, d), lambda i, n: (i, 0)),
                  pl.BlockSpec((None, d, ns), lambda i, n: (n, 0, 0))] + more_specs,
        out_specs=pl.BlockSpec((tm, ns), lambda i, n: (i, n)),
        out_shape=jax.ShapeDtypeStruct((t, s * ns), F32),
        compiler_params=_params(("arbitrary", "arbitrary")),
    )(hn, wg, *more)


def _out_proj_dw(y, dout, name, after=None):
    t, dm = y.shape
    d = dout.shape[1]
    tmm = _col_tile(dm, (1024, 512, 256))
    tn = _col_tile(d, (512, 256))

    def body(y_ref, g_ref, o_ref):
        o_ref[...] = lax.dot_general(y_ref[...], g_ref[...].astype(BF16), TN_DIMS, preferred_element_type=F32)

    body, more_specs, more = _behind(body, 2, after)
    return pl.pallas_call(
        body, name=name, grid=(d // tn, dm // tmm),
        in_specs=[pl.BlockSpec((t, tmm), lambda n, m: (0, m)),
                  pl.BlockSpec((t, tn), lambda n, m: (0, n))] + more_specs,
        out_specs=pl.BlockSpec((tmm, tn), lambda n, m: (m, n)),
        out_shape=jax.ShapeDtypeStruct((dm, d), F32),
        compiler_params=_params(("arbitrary", "arbitrary")),
    )(y, dout, *more)


def _in_proj_bwd(du, wg, h, g, dout, name, after=None, split=None, w_below=None):
    t, d = h.shape
    s, _, ns = wg.shape
    tm = _row_tile(t)
    tn = _col_tile(d, (1024, 512, 256))

    def mm_body(du_ref, w_ref, o_ref):
        total = lax.dot_general(du_ref[:, 0:ns], w_ref[0], NT_DIMS, preferred_element_type=F32)
        for a in range(1, s):
            total = total + lax.dot_general(du_ref[:, a * ns:(a + 1) * ns], w_ref[a], NT_DIMS,
                                            preferred_element_type=F32)
        o_ref[...] = total

    mm_body, more_specs, more = _behind(mm_body, 2, after)
    dhn = pl.pallas_call(
        mm_body, name=name, grid=(t // tm, d // tn),
        in_specs=[pl.BlockSpec((tm, s * ns), lambda i, n: (i, 0)),
                  pl.BlockSpec((s, tn, ns), lambda i, n: (0, n, 0))] + more_specs,
        out_specs=pl.BlockSpec((tm, tn), lambda i, n: (i, n)),
        out_shape=jax.ShapeDtypeStruct((t, d), F32),
        compiler_params=_params(("arbitrary", "arbitrary")),
    )(du, wg, *more)

    tr = 352 if t % 352 == 0 else 192
    nt = t // tr

    def row_grad(dhn_ref, h_ref, g_ref, dout_ref, dg_ref):
        @pl.when(pl.program_id(0) == 0)
        def _():
            dg_ref[...] = jnp.zeros_like(dg_ref)

        x = h_ref[...]
        dn = dhn_ref[...]
        r = lax.rsqrt(jnp.mean(x * x, axis=-1, keepdims=True) + RMS_EPS)
        gd = dn * g_ref[...]
        dot = jnp.mean(gd * x, axis=-1, keepdims=True)
        dg_ref[...] += jnp.sum(dn * (x * r), axis=0, keepdims=True)
        return dout_ref[...] + (r * gd - x * ((r * r * r) * dot))

    rows = pl.BlockSpec((tr, d), lambda i: (i, 0))
    one = pl.BlockSpec((1, d), lambda i: (0, 0))
    if split is None:
        dm = w_below.shape[0]

        def norm_body(dhn_ref, h_ref, g_ref, dout_ref, w_ref, dh_ref, dg_ref, dy_ref):
            dh = row_grad(dhn_ref, h_ref, g_ref, dout_ref, dg_ref)
            dh_ref[...] = dh
            dy_ref[...] = lax.dot_general(dh.astype(BF16), w_ref[...], NT_DIMS, preferred_element_type=F32)

        return pl.pallas_call(
            norm_body, name=name + "_norm", grid=(nt,),
            in_specs=[rows, rows, one, rows, pl.BlockSpec((dm, d), lambda i: (0, 0))],
            out_specs=[rows, one, pl.BlockSpec((tr, dm), lambda i: (i, 0))],
            out_shape=[jax.ShapeDtypeStruct((t, d), F32), jax.ShapeDtypeStruct((1, d), F32),
                       jax.ShapeDtypeStruct((t, dm), F32)],
            compiler_params=_params(("arbitrary",)),
        )(dhn, h, g, dout, w_below)

    n_head, n_body = split
    n_first = tr - n_head
    n_last = n_head + n_body - (nt - 1) * tr
    assert nt >= 2 and 0 < n_head < tr and 0 < n_last <= tr and n_head % SUBLANES == 0 and n_last % SUBLANES == 0

    def split_body(dhn_ref, h_ref, g_ref, dout_ref, body_ref, head_ref, dg_ref, stage, sems):
        i = pl.program_id(0)
        slot = i % 2

        def first_copy(sl):
            return pltpu.make_async_copy(stage.at[sl, pl.ds(n_head, n_first)], body_ref.at[pl.ds(0, n_first)], sems.at[sl])

        def middle_copy(sl, step):
            start = pl.multiple_of(step * tr - n_head, SUBLANES)
            return pltpu.make_async_copy(stage.at[sl], body_ref.at[pl.ds(start, tr)], sems.at[sl])

        def last_copy(sl):
            return pltpu.make_async_copy(stage.at[sl, pl.ds(0, n_last)],
                                         body_ref.at[pl.ds((nt - 1) * tr - n_head, n_last)], sems.at[sl])

        dh = row_grad(dhn_ref, h_ref, g_ref, dout_ref, dg_ref)

        @pl.when(i == 2)
        def _():
            first_copy(0).wait()

        @pl.when(i > 2)
        def _():
            middle_copy(slot, i - 2).wait()

        stage[slot] = dh

        @pl.when(i == 0)
        def _():
            head_ref[...] = stage[0, 0:n_head, :]
            first_copy(0).start()

        @pl.when((i > 0) & (i < nt - 1))
        def _():
            middle_copy(slot, i).start()

        @pl.when(i == nt - 1)
        def _():
            last = last_copy((nt - 1) % 2)
            last.start()
            if nt == 2:
                first_copy(0).wait()
            else:
                middle_copy((nt - 2) % 2, nt - 2).wait()
            last.wait()

    return pl.pallas_call(
        split_body, name=name + "_norm", grid=(nt,),
        in_specs=[rows, rows, one, rows],
        out_specs=[ANY, pl.BlockSpec((n_head, d), lambda i: (0, 0)), one],
        out_shape=[jax.ShapeDtypeStruct((n_body, d), F32), jax.ShapeDtypeStruct((n_head, d), F32),
                   jax.ShapeDtypeStruct((1, d), F32)],
        scratch_shapes=[pltpu.VMEM((2, tr, d), F32), pltpu.SemaphoreType.DMA((2,))],
        compiler_params=_params(("arbitrary",)),
    )(dhn, h, g, dout)


def _in_proj_dw(hn, du, s, name, after=None):
    t, d = hn.shape
    ns = du.shape[1] // s
    tmm = _col_tile(d, (1024, 512, 256))
    tn = _col_tile(ns, (768, 384, 128))
    nb = ns // tn

    def body(hn_ref, du_ref, o_ref):
        o_ref[...] = lax.dot_general(hn_ref[...], du_ref[...], TN_DIMS, preferred_element_type=F32)

    body, more_specs, more = _behind(body, 2, after)
    return pl.pallas_call(
        body, name=name, grid=(d // tmm, s * nb),
        in_specs=[pl.BlockSpec((t, tmm), lambda m, n: (0, m)),
                  pl.BlockSpec((t, tn), lambda m, n: (0, n))] + more_specs,
        out_specs=pl.BlockSpec((None, tmm, tn), lambda m, n: (n // nb, m, n % nb)),
        out_shape=jax.ShapeDtypeStruct((s, d, ns), F32),
        compiler_params=_params(("arbitrary", "arbitrary")),
    )(hn, du, *more)


def _out_proj_loss(h, y, w, tgt, g, n_meta, t_real, name):
    t, d = h.shape
    dm = y.shape[1]
    tm = 352 if t % 352 == 0 else 192

    def body(h_ref, y_ref, w_ref, t_ref, g_ref, dh_ref, loss_ref, dg_ref, dmix_ref):
        i = pl.program_id(0)

        @pl.when(i == 0)
        def _():
            loss_ref[...] = jnp.zeros_like(loss_ref)
            dg_ref[...] = jnp.zeros_like(dg_ref)

        x = h_ref[...] + jnp.dot(y_ref[...], w_ref[...], preferred_element_type=F32)
        gv = g_ref[...]
        r = lax.rsqrt(jnp.mean(x * x, axis=-1, keepdims=True) + RMS_EPS)
        xr = x * r
        rows = i * tm + lax.broadcasted_iota(jnp.int32, (tm, 1), 0)
        valid = (rows >= n_meta) & (rows < t_real)
        err = jnp.where(valid, xr * gv - t_ref[...], 0.0)
        loss_ref[...] += 0.5 * jnp.sum(jnp.mean(err * err, axis=-1, keepdims=True))
        dy = err * (1.0 / d)
        gd = dy * gv
        dot = jnp.mean(gd * x, axis=-1, keepdims=True)
        dh = r * gd - x * ((r * r * r) * dot)
        dh_ref[...] = dh
        dg_ref[...] += jnp.sum(dy * xr, axis=0, keepdims=True)
        dmix_ref[...] = lax.dot_general(dh.astype(BF16), w_ref[...], NT_DIMS, preferred_element_type=F32)

    rows = pl.BlockSpec((tm, d), lambda i: (i, 0))
    wide = pl.BlockSpec((tm, dm), lambda i: (i, 0))
    return pl.pallas_call(
        body, name=name, grid=(t // tm,),
        in_specs=[rows, wide, pl.BlockSpec((dm, d), lambda i: (0, 0)), rows, pl.BlockSpec((1, d), lambda i: (0, 0))],
        out_specs=[rows, pl.BlockSpec((1, LANES), lambda i: (0, 0)), pl.BlockSpec((1, d), lambda i: (0, 0)), wide],
        out_shape=[jax.ShapeDtypeStruct((t, d), F32), jax.ShapeDtypeStruct((1, LANES), F32),
                   jax.ShapeDtypeStruct((1, d), F32), jax.ShapeDtypeStruct((t, dm), F32)],
        compiler_params=_params(("arbitrary",)),
    )(h, y, w, tgt, g)


def _adamw_rows(rows, cols):
    for cand in (512, 256, 128, 64, 32, 16, 8):
        if rows % cand == 0 and cand * cols * 4 <= 2 * 1024 * 1024:
            return cand
    return rows


def _adamw_math(w_ref, g_ref, m_ref, v_ref, d_ref, nm_ref, nv_ref):
    gv = g_ref[...]
    m2 = ADAM_B1 * m_ref[...] + (1.0 - ADAM_B1) * gv
    v2 = ADAM_B2 * v_ref[...] + (1.0 - ADAM_B2) * (gv * gv)
    m_hat = m2 / (1.0 - ADAM_B1 ** ADAM_STEP)
    v_hat = v2 / (1.0 - ADAM_B2 ** ADAM_STEP)
    d_ref[...] = -ADAM_LR * (m_hat / (jnp.sqrt(v_hat) + ADAM_EPS) + ADAM_WD * w_ref[...])
    nm_ref[...] = m2
    nv_ref[...] = v2


def _adamw(w, g, m, v, name, after=None):
    shape = w.shape
    assert len(shape) >= 2 and w.size * 4 <= 2 * 1024 * 1024

    def body(*refs):
        _adamw_math(*refs)

    body, more_specs, more = _behind(body, 4, after)
    spec = pl.BlockSpec(shape, lambda i: (0,) * len(shape))
    return pl.pallas_call(
        body, name=name, grid=(1,),
        in_specs=[spec] * 4 + more_specs, out_specs=[spec] * 3,
        out_shape=[jax.ShapeDtypeStruct(shape, F32)] * 3,
        compiler_params=_params(("arbitrary",)),
    )(w, g, m, v, *more)


def _adamw_layer(w, g, m, v, layer, kept, name, after=None):
    nl, rows, cols = w.shape
    tr = _adamw_rows(rows, cols)
    n_kept = 0 if kept is None else 3

    def body(*refs):
        _adamw_math(*refs[:4], *refs[4 + n_kept:])

    body, more_specs, more = _behind(body, 4 + n_kept, after)
    lay = pl.BlockSpec((None, tr, cols), lambda i: (layer, i, 0))
    return pl.pallas_call(
        body, name=name, grid=(rows // tr,),
        in_specs=[lay, pl.BlockSpec((tr, cols), lambda i: (i, 0)), lay, lay] + [ANY] * n_kept + more_specs,
        out_specs=[lay] * 3,
        out_shape=[jax.ShapeDtypeStruct((nl, rows, cols), F32)] * 3,
        input_output_aliases={4 + k: k for k in range(n_kept)},
        compiler_params=_params(("arbitrary",)),
    )(w, g, m, v, *([] if kept is None else kept), *more)


def _pair_add(x, ra, c_idx, name):
    s, _, rows, cols = x.shape
    tr = _slab_rows(rows, cols)

    def body(c_ref, x_ref, r_ref, o_ref):
        o_ref[...] = (x_ref[...] + r_ref[...]).astype(BF16)

    return pl.pallas_call(
        body, name=name,
        grid_spec=pltpu.PrefetchScalarGridSpec(
            num_scalar_prefetch=1, grid=(s, rows // tr),
            in_specs=[pl.BlockSpec((None, None, tr, cols), lambda a, i, c_ref: (a, c_ref[0], i, 0)),
                      pl.BlockSpec((None, tr, cols), lambda a, i, c_ref: (a, i, 0))],
            out_specs=pl.BlockSpec((None, tr, cols), lambda a, i, c_ref: (a, i, 0))),
        out_shape=jax.ShapeDtypeStruct((s, rows, cols), BF16),
        compiler_params=_params(("arbitrary", "arbitrary")),
    )(c_idx, x, ra)


def _chip_sum(rc, p, where, n_slots, name):
    s, rows, cols = rc.shape
    tr = _slab_rows(rows, cols)

    def body(w_ref, x_ref, p_ref, o_ref):
        me = w_ref[0]
        total = jnp.where(me == 0, p_ref[...], x_ref[0]).astype(F32)
        for a in range(1, s):
            total = total + jnp.where(me == a, p_ref[...], x_ref[a]).astype(F32)
        o_ref[...] = total

    return pl.pallas_call(
        body, name=name,
        grid_spec=pltpu.PrefetchScalarGridSpec(
            num_scalar_prefetch=1, grid=(rows // tr,),
            in_specs=[pl.BlockSpec((s, tr, cols), lambda i, w_ref: (0, i, 0)),
                      pl.BlockSpec((None, tr, cols), lambda i, w_ref: (w_ref[0], i, 0))],
            out_specs=pl.BlockSpec((None, tr, cols), lambda i, w_ref: (w_ref[1], i, 0))),
        out_shape=jax.ShapeDtypeStruct((n_slots, rows, cols), F32),
        compiler_params=_params(("arbitrary",)),
    )(where, rc, p)


def _cast_place(w, layer, me_idx, name, after=None):
    _, rows, cols = w.shape
    tr = _slab_rows(rows, cols)

    def body(m_ref, w_ref, o_ref):
        o_ref[...] = w_ref[...].astype(BF16)

    body, more_specs, more = _behind(body, 2, after)
    return pl.pallas_call(
        body, name=name,
        grid_spec=pltpu.PrefetchScalarGridSpec(
            num_scalar_prefetch=1, grid=(rows // tr,),
            in_specs=[pl.BlockSpec((None, tr, cols), lambda i, m_ref: (layer, i, 0))] + more_specs,
            out_specs=pl.BlockSpec((None, tr, cols), lambda i, m_ref: (m_ref[0], i, 0))),
        out_shape=jax.ShapeDtypeStruct((N_CHIPS, rows, cols), BF16),
        compiler_params=_params(("arbitrary",)),
    )(me_idx, w, *more)


def _place():
    x, y, c = lax.axis_index("x"), lax.axis_index("y"), lax.axis_index("c")
    chips = [(1 - x, y), (x, 1 - y), (1 - x, 1 - y)]
    return x, y, c, chips


def _chip_index(cx, cy):
    return 2 * cx + cy


def _gather_copies(bufs, stage):
    x, y, c, chips = _place()
    me = _chip_index(x, y)
    copies = []
    for b in bufs:
        for chip in chips:
            src = _chip_index(*chip)
            if stage == 0:
                copies.append((b.at[me, c], (*chip, c), b.at[src, c]))
            else:
                copies.append((b.at[src, c], (x, y, 1 - c), b.at[src, 1 - c]))
    return copies


def _remote(ref, peer, ssem, rsem, k):
    return pltpu.make_async_remote_copy(src_ref=ref, dst_ref=ref, send_sem=ssem.at[k], recv_sem=rsem.at[k],
                                        device_id=peer, device_id_type=MESH)


def _gather_first(bufs, small):
    n = len(bufs)
    k = 3 * n

    def body(*refs):
        sm_ref = refs[n]
        b_refs, smg_ref = refs[n + 1:2 * n + 1], refs[2 * n + 1]
        lsem, ssem, rsem = refs[2 * n + 2:]
        x, y, c, chips = _place()
        me = _chip_index(x, y)
        local = pltpu.make_async_copy(sm_ref, smg_ref.at[me], lsem)
        local.start()
        first = _gather_copies(b_refs, 0)
        second = _gather_copies(b_refs, 1)
        started = []
        for i, (ref, peer, _) in enumerate(first):
            started.append(_remote(ref, peer, ssem, rsem, i))
        for j, chip in enumerate(chips):
            started.append(pltpu.make_async_remote_copy(
                src_ref=sm_ref, dst_ref=smg_ref.at[me], send_sem=ssem.at[2 * k + j], recv_sem=rsem.at[2 * k + j],
                device_id=(*chip, c), device_id_type=MESH))
        for cp in started:
            cp.start()
        for i, (_, peer, lands) in enumerate(first):
            _remote(lands, peer, ssem, rsem, i).wait_recv()
            ref, sib, _ = second[i]
            fwd = _remote(ref, sib, ssem, rsem, k + i)
            fwd.start()
            started.append(fwd)
        for i, (_, sib, lands) in enumerate(second):
            _remote(lands, sib, ssem, rsem, k + i).wait_recv()
        for j, chip in enumerate(chips):
            theirs = smg_ref.at[_chip_index(*chip)]
            pltpu.make_async_remote_copy(src_ref=theirs, dst_ref=theirs, send_sem=ssem.at[2 * k + j],
                                         recv_sem=rsem.at[2 * k + j], device_id=(*chip, c),
                                         device_id_type=MESH).wait_recv()
        for cp in started:
            cp.wait_send()
        local.wait()

    return pl.pallas_call(
        body, name="gather_first",
        in_specs=[ANY] * (n + 1), out_specs=[ANY] * (n + 1),
        out_shape=[jax.ShapeDtypeStruct(b.shape, b.dtype) for b in bufs]
        + [jax.ShapeDtypeStruct((N_CHIPS,) + small.shape, small.dtype)],
        input_output_aliases={i: i for i in range(n)},
        scratch_shapes=[pltpu.SemaphoreType.DMA, pltpu.SemaphoreType.DMA((2 * k + 3,)),
                        pltpu.SemaphoreType.DMA((2 * k + 3,))],
    )(*bufs, small)


HBM = pl.BlockSpec(memory_space=pltpu.HBM)
SEM = pl.BlockSpec(memory_space=pltpu.SEMAPHORE)
DATAFLOW = pltpu.SideEffectType.DATAFLOW_SIDE_EFFECTING


def _copies_start(bufs, plan, n_copies, name, after=None):
    n = len(bufs)
    extra = [] if after is None else [after]

    def body(*refs):
        refs = refs[:n] + refs[n + len(extra):]
        ssem, rsem = refs[n], refs[n + 1]
        b_refs, token = refs[n + 2:2 * n + 2], refs[2 * n + 2]
        copies = plan(b_refs)
        assert len(copies) == n_copies
        for i, (src, dst, peer, _) in enumerate(copies):
            pltpu.make_async_remote_copy(src_ref=src, dst_ref=dst, send_sem=ssem.at[i], recv_sem=rsem.at[i],
                                         device_id=peer, device_id_type=MESH).start()
        token[...] = jnp.zeros_like(token)

    return pl.pallas_call(
        body, name=name,
        out_shape=(pltpu.SemaphoreType.DMA((n_copies,)), pltpu.SemaphoreType.DMA((n_copies,)),
                   *[pltpu.HBM(b.shape, b.dtype) for b in bufs], jax.ShapeDtypeStruct((SUBLANES, LANES), F32)),
        in_specs=[HBM] * n + [ANY] * len(extra),
        out_specs=(SEM, SEM, *[HBM] * n, pl.BlockSpec(memory_space=pltpu.VMEM)),
        input_output_aliases={i: 2 + i for i in range(n)},
        compiler_params=pltpu.CompilerParams(has_side_effects=DATAFLOW),
    )(*[pltpu.with_memory_space_constraint(b, pltpu.HBM) for b in bufs], *extra)


def _copies_wait(bufs, ssem, rsem, after, plan, name):
    n = len(bufs)
    afters = list(after) if isinstance(after, (list, tuple)) else [after]

    def body(*refs):
        b_refs, ssem_ref, rsem_ref = refs[:n], refs[n], refs[n + 1]
        for i, (src, dst, peer, lands) in enumerate(plan(b_refs)):
            pltpu.make_async_remote_copy(src_ref=src, dst_ref=dst, send_sem=ssem_ref.at[i], recv_sem=rsem_ref.at[i],
                                         device_id=peer, device_id_type=MESH).wait_send()
            pltpu.make_async_remote_copy(src_ref=lands, dst_ref=lands, send_sem=ssem_ref.at[i],
                                         recv_sem=rsem_ref.at[i], device_id=peer, device_id_type=MESH).wait_recv()

    return pl.pallas_call(
        body, name=name,
        out_shape=tuple(pltpu.HBM(b.shape, b.dtype) for b in bufs),
        in_specs=[HBM] * n + [SEM, SEM] + [ANY] * len(afters), out_specs=tuple([HBM] * n),
        input_output_aliases={i: i for i in range(n)},
        compiler_params=pltpu.CompilerParams(has_side_effects=DATAFLOW),
    )(*bufs, ssem, rsem, *afters)


def _gather_plan(stage):
    return lambda refs: [(ref, ref, peer, lands) for ref, peer, lands in _gather_copies(refs, stage)]


def _swap_plan(refs):
    n = len(refs) // 2
    x, y, c, _ = _place()
    return [(refs[a].at[:, 1 - c], refs[n + a], (x, y, 1 - c), refs[n + a]) for a in range(n)]


def _scatter_plan(refs):
    n = len(refs) // 2
    x, y, c, chips = _place()
    me = _chip_index(x, y)
    return [(refs[a].at[_chip_index(*chip)], refs[n + a].at[me], (*chip, c), refs[n + a].at[_chip_index(*chip)])
            for a in range(n) for chip in chips]


def _pair_gather_plan(refs):
    x, y, c, _ = _place()
    return [(r.at[c], r.at[c], (x, y, 1 - c), r.at[1 - c]) for r in refs]


def _pair_swap(xs, name):
    n = len(xs)

    def body(*refs):
        x_refs, o_refs, ssem, rsem = refs[:n], refs[n:2 * n], refs[2 * n], refs[2 * n + 1]
        x, y, c, _ = _place()
        copies = [pltpu.make_async_remote_copy(src_ref=x_refs[a].at[:, 1 - c], dst_ref=o_refs[a],
                                               send_sem=ssem.at[a], recv_sem=rsem.at[a],
                                               device_id=(x, y, 1 - c), device_id_type=MESH) for a in range(n)]
        for cp in copies:
            cp.start()
        for cp in copies:
            cp.wait()

    return pl.pallas_call(
        body, name=name, in_specs=[ANY] * n, out_specs=[ANY] * n,
        out_shape=[jax.ShapeDtypeStruct((a.shape[0],) + a.shape[2:], a.dtype) for a in xs],
        scratch_shapes=[pltpu.SemaphoreType.DMA((n,)), pltpu.SemaphoreType.DMA((n,))],
    )(*xs)


def _chip_scatter(ps):
    n = len(ps)

    def body(*refs):
        p_refs, o_refs, ssem, rsem = refs[:n], refs[n:2 * n], refs[2 * n], refs[2 * n + 1]
        x, y, c, chips = _place()
        me = _chip_index(x, y)
        sends = []
        for a in range(n):
            for j, chip in enumerate(chips):
                sends.append(pltpu.make_async_remote_copy(
                    src_ref=p_refs[a].at[_chip_index(*chip)], dst_ref=o_refs[a].at[me],
                    send_sem=ssem.at[3 * a + j], recv_sem=rsem.at[3 * a + j],
                    device_id=(*chip, c), device_id_type=MESH))
        for cp in sends:
            cp.start()
        for a in range(n):
            for j, chip in enumerate(chips):
                src = _chip_index(*chip)
                pltpu.make_async_remote_copy(
                    src_ref=p_refs[a].at[src], dst_ref=o_refs[a].at[src],
                    send_sem=ssem.at[3 * a + j], recv_sem=rsem.at[3 * a + j],
                    device_id=(*chip, c), device_id_type=MESH).wait_recv()
        for cp in sends:
            cp.wait_send()

    return pl.pallas_call(
        body, name="chip_scatter", in_specs=[ANY] * n, out_specs=[ANY] * n,
        out_shape=[jax.ShapeDtypeStruct(a.shape, a.dtype) for a in ps],
        scratch_shapes=[pltpu.SemaphoreType.DMA((3 * n,)), pltpu.SemaphoreType.DMA((3 * n,))],
    )(*ps)


def _final_gather(fs, rep):
    n = len(fs)

    def body(*refs):
        o_refs, repo_ref = refs[n + 1:2 * n + 1], refs[2 * n + 1]
        ssem, rsem = refs[2 * n + 2:]
        x, y, c, chips = _place()
        slot = 4 * x + 2 * y + c
        copies = [pltpu.make_async_remote_copy(src_ref=o_refs[a].at[c], dst_ref=o_refs[a].at[c],
                                               send_sem=ssem.at[a], recv_sem=rsem.at[a],
                                               device_id=(x, y, 1 - c), device_id_type=MESH) for a in range(n)]
        peers = [(x, y, 1 - c)] + [(*chip, c) for chip in chips] + [(*chip, 1 - c) for chip in chips]
        for k, peer in enumerate(peers):
            copies.append(pltpu.make_async_remote_copy(src_ref=repo_ref.at[slot], dst_ref=repo_ref.at[slot],
                                                       send_sem=ssem.at[n + k], recv_sem=rsem.at[n + k],
                                                       device_id=peer, device_id_type=MESH))
        for cp in copies:
            cp.start()
        for a in range(n):
            pltpu.make_async_remote_copy(src_ref=o_refs[a].at[1 - c], dst_ref=o_refs[a].at[1 - c],
                                         send_sem=ssem.at[a], recv_sem=rsem.at[a],
                                         device_id=(x, y, 1 - c), device_id_type=MESH).wait_recv()
        for k, peer in enumerate(peers):
            px, py, pc = peer
            theirs = repo_ref.at[4 * px + 2 * py + pc]
            pltpu.make_async_remote_copy(src_ref=theirs, dst_ref=theirs, send_sem=ssem.at[n + k], recv_sem=rsem.at[n + k],
                                         device_id=peer, device_id_type=MESH).wait_recv()
        for cp in copies:
            cp.wait_send()

    return pl.pallas_call(
        body, name="final_gather", in_specs=[ANY] * (n + 1), out_specs=[ANY] * (n + 1),
        out_shape=[jax.ShapeDtypeStruct(a.shape, a.dtype) for a in fs] + [jax.ShapeDtypeStruct(rep.shape, rep.dtype)],
        input_output_aliases={k: k for k in range(n + 1)},
        scratch_shapes=[pltpu.SemaphoreType.DMA((n + 7,)), pltpu.SemaphoreType.DMA((n + 7,))],
    )(*fs, rep)


def _block_diag(w, gb):
    nh, hd, _ = w.shape
    per = gb // hd
    w4 = w.reshape(nh // per, per, hd, hd)
    eye = jnp.eye(per, dtype=w.dtype)
    return jnp.einsum("jaik,ab->jaibk", w4, eye).reshape(nh // per, gb, gb)


def _diag_blocks(dense, hd):
    nj, gb, _ = dense.shape
    per = gb // hd
    d5 = dense.reshape(nj, per, hd, per, hd)
    return jnp.stack([d5[:, a, :, a, :] for a in range(per)], axis=1).reshape(nj * per, hd, hd)


def _round_up(n, q):
    return (n + q - 1) // q * q


def kernel(x, meta, norm_g, w_in, conv_a_w, conv_a_b, lru_wr, lru_br, lru_wi, lru_bi, lru_lambda, conv_b_w, w_out, final_g, loss_target, m_meta, m_norm_g, m_w_in, m_conv_a_w, m_conv_a_b, m_lru_wr, m_lru_br, m_lru_wi, m_lru_bi, m_lru_lambda, m_conv_b_w, m_w_out, m_final_g, v_meta, v_norm_g, v_w_in, v_conv_a_w, v_conv_a_b, v_lru_wr, v_lru_br, v_lru_wi, v_lru_bi, v_lru_lambda, v_conv_b_w, v_w_out, v_final_g):
    weights = dict(meta=meta, norm_g=norm_g, w_in=w_in, conv_a_w=conv_a_w, conv_a_b=conv_a_b, lru_wr=lru_wr,
                   lru_br=lru_br, lru_wi=lru_wi, lru_bi=lru_bi, lru_lambda=lru_lambda, conv_b_w=conv_b_w,
                   w_out=w_out, final_g=final_g)
    mom1 = dict(meta=m_meta, norm_g=m_norm_g, w_in=m_w_in, conv_a_w=m_conv_a_w, conv_a_b=m_conv_a_b,
                lru_wr=m_lru_wr, lru_br=m_lru_br, lru_wi=m_lru_wi, lru_bi=m_lru_bi, lru_lambda=m_lru_lambda,
                conv_b_w=m_conv_b_w, w_out=m_w_out, final_g=m_final_g)
    mom2 = dict(meta=v_meta, norm_g=v_norm_g, w_in=v_w_in, conv_a_w=v_conv_a_w, conv_a_b=v_conv_a_b,
                lru_wr=v_lru_wr, lru_br=v_lru_br, lru_wi=v_lru_wi, lru_bi=v_lru_bi, lru_lambda=v_lru_lambda,
                conv_b_w=v_conv_b_w, w_out=v_w_out, final_g=v_final_g)
    names = list(weights)

    assert x.shape[0] == 1
    seq, d = x.shape[1], x.shape[2]
    n_meta, ds = meta.shape
    depth = norm_g.shape[0]
    c = lru_lambda.shape[1]
    nh, hd = lru_wr.shape[1], lru_wr.shape[2]
    ns = w_in.shape[2]
    dms = w_out.shape[1]
    cs = conv_a_w.shape[2]
    ka, kb = conv_a_w.shape[1], conv_b_w.shape[1]
    s = N_CHIPS
    assert depth == N_CORES and d == s * ds and c == s * cs and s * ns == 6 * c and s * dms == 2 * c
    gb = min(GATE_BLOCK, c)
    t_real = n_meta + seq
    t = _round_up(t_real, ROW_QUANTUM)
    my_c = lax.axis_index("c").astype(jnp.int32)
    my_chip = (2 * lax.axis_index("x") + lax.axis_index("y")).astype(jnp.int32)
    c_idx = my_c.reshape(1)
    chip_idx = my_chip.reshape(1)

    sm_rows = _round_up(n_meta + depth * SUBLANES, 2 * SUBLANES)
    small = jnp.zeros((sm_rows, ds), F32)
    small = small.at[0:n_meta, :].set(meta)
    for l in range(depth):
        base = n_meta + l * SUBLANES
        small = small.at[base:base + ka, 0:cs].set(conv_a_w[l])
        small = small.at[base + ka:base + ka + kb, 0:cs].set(conv_b_w[l])
    (small_g,) = _gather_first([], small)
    meta_full = jnp.transpose(small_g[:, 0:n_meta, :], (1, 0, 2)).reshape(n_meta, d)
    wa_full, wb_full = [], []
    for l in range(depth):
        base = n_meta + l * SUBLANES
        wa_full.append(jnp.transpose(small_g[:, base:base + ka, 0:cs], (1, 0, 2)).reshape(ka, c))
        wb_full.append(jnp.transpose(small_g[:, base + ka:base + ka + kb, 0:cs], (1, 0, 2)).reshape(kb, c))
    win0 = _cast_place(w_in, 0, chip_idx, "cast_w_in_0").reshape(s, 2, d // 2, ns)
    ssem_w, rsem_w, win0, token_w = _copies_start([win0], _gather_plan(0), 3, "gather_win0_ici_start", after=small_g)
    win_b = [None] + [_cast_place(w_in, l, chip_idx, f"cast_w_in_{l}", after=token_w).reshape(s, 2, d // 2, ns)
                      for l in range(1, depth)]
    wout_b = [_cast_place(w_out, l, chip_idx, f"cast_w_out_{l}", after=token_w).reshape(s, 2, dms // 2, d)
              for l in range(depth)]
    h = jnp.concatenate([meta_full, x[0], jnp.zeros((t - t_real, d), F32)], axis=0) + token_w[0, 0]
    tgt = jnp.concatenate([jnp.zeros((n_meta, d), F32), loss_target[0], jnp.zeros((t - t_real, d), F32)],
                          axis=0) + token_w[0, 0]
    u_own, hn_own = _norm_in_own(h, norm_g[0].reshape(1, d), win0.reshape(s, d, ns), chip_idx, "norm_in_0_own")
    (win0,) = _copies_wait([win0], ssem_w, rsem_w, [u_own, tgt] + win_b[1:] + wout_b, _gather_plan(0),
                           "gather_win0_ici_wait")
    ssem_w, rsem_w, win0, token_w = _copies_start([win0], _gather_plan(1), 3, "gather_win0_d2d_start")
    def travel(buf, stage, tag, after):
        return _copies_start([buf], _gather_plan(stage), 3, f"gather_{tag}_{'d2d' if stage else 'ici'}_start",
                             after=after)

    def arrived(state, stage, tag, after):
        (buf,) = _copies_wait([state[2]], state[0], state[1], after, _gather_plan(stage),
                              f"gather_{tag}_{'d2d' if stage else 'ici'}_wait")
        return buf

    on_wout0 = travel(wout_b[0], 0, "wout0", token_w)
    on_win1 = travel(win_b[1], 0, "win1", on_wout0[3])
    on_wout1 = travel(wout_b[1], 0, "wout1", on_win1[3])
    token = on_wout1[3]
    (win_b[0],) = _copies_wait([win0], ssem_w, rsem_w, token, _gather_plan(1), "gather_win0_d2d_wait")

    layer_w = []
    for l in range(depth):
        layer_w.append(dict(
            g=norm_g[l].reshape(1, d), wa=wa_full[l], ba=conv_a_b[l].reshape(1, c),
            wr=_block_diag(lru_wr[l], gb).astype(BF16), br=lru_br[l].reshape(1, c),
            wi=_block_diag(lru_wi[l], gb).astype(BF16), bi=lru_bi[l].reshape(1, c),
            lam=lru_lambda[l].reshape(1, c), wb=wb_full[l]))
    saved = []
    for l, lw in enumerate(layer_w):
        first = l == 0
        lw["win"] = win_b[l].reshape(s, d, ns)
        mixer_w = (lw["wa"], lw["ba"], lw["wr"], lw["br"], lw["wi"], lw["bi"], lw["lam"], lw["wb"])
        if first:
            u = _norm_in_rest(hn_own, lw["win"], u_own, chip_idx, "norm_in_0_rest", after=token)
            hn = hn_own
            on_wout0 = travel(arrived(on_wout0, 0, "wout0", u), 1, "wout0", None)
            wout_b[0] = arrived(on_wout0, 1, "wout0", on_wout0[3])
            lw["wout"] = wout_b[0].reshape(2 * c, d)
            y, hs, gates, h_next, hn_next = _mix_fwd(u, *mixer_w, f"mix_fwd_{l}",
                                                     proj=(h, lw["wout"], layer_w[1]["g"]))
            saved.append((h, u, hn, y, hs, gates))
            h = h_next
            on_win1 = travel(arrived(on_win1, 0, "win1", y), 1, "win1", None)
            win_b[1] = arrived(on_win1, 1, "win1", on_win1[3])
            on_wout1 = travel(arrived(on_wout1, 0, "wout1", y), 1, "wout1", on_win1[3])
            token = on_wout1[3]
        else:
            hn = hn_next
            u = _in_proj(hn, lw["win"], f"norm_in_{l}", after=token)
            wout_b[1] = arrived(on_wout1, 1, "wout1", u)
            lw["wout"] = wout_b[1].reshape(2 * c, d)
            y, hs, gates = _mix_fwd(u, *mixer_w, f"mix_fwd_{l}")
            saved.append((h, u, hn, y, hs, gates))
            dh, loss_lanes, d_final_g, dy = _out_proj_loss(h, y, lw["wout"], tgt, final_g.reshape(1, d), n_meta,
                                                           t_real, f"out_proj_{l}_loss")
    loss = lax.psum(loss_lanes[0, 0], ("x", "y", "c"))

    to_core = jnp.stack([my_chip, my_c])
    grads = [None] * depth
    early = None
    for l in reversed(range(depth)):
        lw = layer_w[l]
        h_in, u, hn, y, hs, gates = saved[l]
        token = early[-1] if early else None
        d_wout = _out_proj_dw(y, dh, f"out_proj_dw_{l}", after=token)
        if early:
            ssem, rsem, bufs, _ = early
            bufs = _copies_wait(bufs, ssem, rsem, d_wout, _swap_plan, "early_swap_wait")
            half = len(bufs) // 2
            sums = [_pair_add(a, b, c_idx, f"early_pair_add_{k}") for k, (a, b) in enumerate(zip(bufs[:half], bufs[half:]))]
            lands = [lax.empty(p.shape, p.dtype) for p in sums]
            ssem, rsem, *bufs, token = _copies_start(sums + lands, _scatter_plan, 3 * half, "early_scatter_start")
        du, dsm, d_wr, d_wi = _mix_bwd(u, hs, dy, gates, lw["wa"], lw["ba"], lw["wr"], lw["wi"], lw["lam"], lw["wb"],
                                       f"mix_bwd_{l}", after=token)
        if early:
            bufs = _copies_wait(bufs, ssem, rsem, du, _scatter_plan, "early_scatter_wait")
            halves = [_chip_sum(rc, p, to_core, N_CORES, f"early_chip_sum_{k}")
                      for k, (p, rc) in enumerate(zip(bufs[:half], bufs[half:]))]
            ssem, rsem, *bufs, token = _copies_start(halves, _pair_gather_plan, half, "early_gather_start")
        d_win = _in_proj_dw(hn, du, s, f"in_proj_dw_{l}", after=token)
        srcs = [d_win.reshape(s, 2, d // 2, ns), d_wout.reshape(s, 2, dms // 2, d)]
        if early:
            early_full = _copies_wait(bufs, ssem, rsem, d_win, _pair_gather_plan, "early_gather_wait")
            lands = [lax.empty((a.shape[0],) + a.shape[2:], a.dtype) for a in srcs]
            ssem, rsem, *bufs, token = _copies_start(srcs + lands, _swap_plan, len(srcs), "late_swap_start")
            last = depth - 1
            early_grad = dict(w_in=early_full[0].reshape(d, ns), w_out=early_full[1].reshape(dms, d))
            early_step = {n: _adamw_layer(weights[n], early_grad[n], mom1[n], mom2[n], last, None,
                                          f"adamw_{n}_{last}", after=token) for n in ("w_in", "w_out")}
            bufs = _copies_wait(bufs, ssem, rsem, [o[0] for o in early_step.values()], _swap_plan, "late_swap_wait")
            late_sums = [_pair_add(a, b, c_idx, f"pair_add_{k}")
                         for k, (a, b) in enumerate(zip(bufs[:len(srcs)], bufs[len(srcs):]))]
            lands = [lax.empty(p.shape, p.dtype) for p in late_sums]
            ssem, rsem, *bufs, token = _copies_start(late_sums + lands, _scatter_plan, 3 * len(srcs), "late_scatter_start")
        if l > 0:
            dh, d_g, dy = _in_proj_bwd(du, lw["win"], h_in, lw["g"], dh, f"in_proj_bwd_{l}", after=token,
                                       w_below=layer_w[l - 1]["wout"])
        else:
            grad_x, d_meta, d_g = _in_proj_bwd(du, lw["win"], h_in, lw["g"], dh, f"in_proj_bwd_{l}", after=token,
                                               split=(n_meta, seq))
        if early:
            bufs = _copies_wait(bufs, ssem, rsem, grad_x, _scatter_plan, "late_scatter_wait")
            late_reduced = [_chip_sum(rc, p, to_core, N_CORES, f"chip_sum_{k}")
                            for k, (p, rc) in enumerate(zip(bufs[:len(srcs)], bufs[len(srcs):]))]
        grads[l] = dict(dsm=dsm, wr=_diag_blocks(d_wr, hd), wi=_diag_blocks(d_wi, hd), g=d_g)
        if l == depth - 1:
            lands = [lax.empty((a.shape[0],) + a.shape[2:], a.dtype) for a in srcs]
            ssem, rsem, *bufs, token = _copies_start(srcs + lands, _swap_plan, len(srcs), "early_swap_start")
            early = (ssem, rsem, bufs, token)
        else:
            early = None
    grad_x = grad_x[None]

    sharded = []
    sp = jnp.zeros((sm_rows, s, ds), F32)
    sp = sp.at[0:n_meta].set(d_meta.reshape(n_meta, s, ds))
    for l in range(depth):
        base = n_meta + l * SUBLANES
        dsm = grads[l]["dsm"]
        sp = sp.at[base:base + ka, :, 0:cs].set(dsm[ROW_DWA:ROW_DWA + ka].reshape(ka, s, cs))
        sp = sp.at[base + ka:base + ka + kb, :, 0:cs].set(dsm[ROW_DWB:ROW_DWB + kb].reshape(kb, s, cs))
    sharded.append(jnp.transpose(sp, (1, 0, 2)).reshape(s, 2, sm_rows // 2, ds))
    rep_parts = [jnp.concatenate([grads[l]["g"].reshape(-1) for l in range(depth)]), d_final_g.reshape(-1)]
    for row in (ROW_DBA, ROW_DBR, ROW_DBI, ROW_DLAM):
        rep_parts.append(jnp.concatenate([grads[l]["dsm"][row] for l in range(depth)]))
    rep_parts.append(jnp.concatenate([grads[l]["wr"].reshape(-1) for l in range(depth)]))
    rep_parts.append(jnp.concatenate([grads[l]["wi"].reshape(-1) for l in range(depth)]))
    rep_sizes = [p.shape[0] for p in rep_parts]
    piece = _round_up(-(-sum(rep_sizes) // (s * 2)), 2 * SUBLANES * LANES)
    flat = jnp.concatenate(rep_parts + [jnp.zeros((s * 2 * piece - sum(rep_sizes),), F32)])
    sharded.append(flat.reshape(s, 2, piece // LANES, LANES))

    from_sibling = _pair_swap(sharded, "small_pair_swap")
    pair_sums = [_pair_add(a, b, c_idx, f"small_pair_add_{k}") for k, (a, b) in enumerate(zip(sharded, from_sibling))]
    by_chip = _chip_scatter(pair_sums)
    to_device = jnp.stack([my_chip, 2 * my_chip + my_c])
    reduced_sp = _chip_sum(by_chip[0], pair_sums[0], to_core, N_CORES, "small_chip_sum")
    reduced_rep = _chip_sum(by_chip[1], pair_sums[1], to_device, N_CHIPS * N_CORES, "chip_sum_rep")
    sp_full, rep_all = _final_gather([reduced_sp], reduced_rep)
    ssem, rsem, *bufs, token = _copies_start(late_reduced, _pair_gather_plan, len(late_reduced), "late_gather_start",
                                             after=rep_all)
    g_sp = sp_full.reshape(sm_rows, ds)
    rep_flat = rep_all.reshape(-1)
    rep_out, off = [], 0
    for n in rep_sizes:
        rep_out.append(rep_flat[off:off + n])
        off += n
    grad = dict(
        meta=g_sp[0:n_meta],
        norm_g=rep_out[0].reshape(depth, d),
        conv_a_w=jnp.stack([g_sp[n_meta + l * SUBLANES:n_meta + l * SUBLANES + ka, 0:cs] for l in range(depth)]),
        conv_a_b=rep_out[2].reshape(depth, c),
        lru_wr=rep_out[6].reshape(depth, nh, hd, hd),
        lru_br=rep_out[3].reshape(depth, c),
        lru_wi=rep_out[7].reshape(depth, nh, hd, hd),
        lru_bi=rep_out[4].reshape(depth, c),
        lru_lambda=rep_out[5].reshape(depth, c),
        conv_b_w=jnp.stack([g_sp[n_meta + l * SUBLANES + ka:n_meta + l * SUBLANES + ka + kb, 0:cs]
                            for l in range(depth)]),
        final_g=rep_out[1].reshape(d),
    )

    delta, new_m, new_v = {}, {}, {}
    for n in grad:
        shape = weights[n].shape
        as_block = shape if len(shape) > 1 else (1,) + shape
        out = _adamw(weights[n].reshape(as_block), grad[n].reshape(as_block), mom1[n].reshape(as_block),
                     mom2[n].reshape(as_block), f"adamw_{n}", after=token)
        delta[n], new_m[n], new_v[n] = (o.reshape(shape) for o in out)
    full = _copies_wait(bufs, ssem, rsem, [delta[n] for n in grad], _pair_gather_plan, "late_gather_wait")
    g_win = [full[0].reshape(d, ns), early_full[0].reshape(d, ns)]
    g_wout = [full[1].reshape(dms, d), early_full[1].reshape(dms, d)]
    grad["w_in"] = jnp.stack(g_win)
    grad["w_out"] = jnp.stack(g_wout)
    for n, g_first in (("w_in", g_win[0]), ("w_out", g_wout[0])):
        delta[n], new_m[n], new_v[n] = _adamw_layer(weights[n], g_first, mom1[n], mom2[n], 0, early_step[n],
                                                    f"adamw_{n}_0")

    return (loss, grad_x, *[grad[n] for n in names], *[delta[n] for n in names],
            *[new_m[n] for n in names], *[new_v[n] for n in names])
```

```python
import jax
import jax.numpy as jnp
from jax import lax
from jax.experimental import pallas as pl
from jax.experimental.pallas import tpu as pltpu

F32 = jnp.float32
BF16 = jnp.bfloat16

RMS_EPS = 1e-6
LRU_C = 8.0
ADAM_LR = 0.001
ADAM_B1 = 0.9
ADAM_B2 = 0.999
ADAM_EPS = 1e-08
ADAM_WD = 0.01
ADAM_STEP = 10

N_CHIPS = 4
N_CORES = 2
VMEM_LIMIT_BYTES = 56 * 1024 * 1024
SUBLANES = 8
LANES = 128
ROW_QUANTUM = 384
MIX_CHUNK = 192
SCAN_UNROLL = 8
GATE_BLOCK = 256
MESH = pl.DeviceIdType.MESH
ANY = pl.BlockSpec(memory_space=pl.ANY)

NT_DIMS = (((1,), (1,)), ((), ()))
TN_DIMS = (((0,), (0,)), ((), ()))


def _params(sem):
    return pltpu.CompilerParams(dimension_semantics=sem, vmem_limit_bytes=VMEM_LIMIT_BYTES)


def _sig(x):
    return 0.5 * jnp.tanh(0.5 * x) + 0.5


def _row_tile(t):
    return 704 if t % 704 == 0 else 192


def _col_tile(n, prefs):
    for p in prefs:
        if n % p == 0:
            return p
    return n


def _slab_rows(rows, cols):
    if rows * cols * 4 <= 1024 * 1024:
        return rows
    return _col_tile(rows, (256, 128, 64, 32, 16))


def _norm_in_own(h, g, wg, me_idx, name):
    t, d = h.shape
    s, _, ns = wg.shape
    tm = 1408 if t % 1408 == 0 else _row_tile(t)
    tn = _col_tile(ns, (768, 384, 128))
    nb = ns // tn

    def body(m_ref, h_ref, g_ref, w_ref, u_ref, hn_ref):
        @pl.when(pl.program_id(1) == 0)
        def _():
            x = h_ref[...]
            r = lax.rsqrt(jnp.mean(x * x, axis=-1, keepdims=True) + RMS_EPS)
            hn_ref[...] = ((x * r) * g_ref[...]).astype(BF16)

        u_ref[...] = jnp.dot(hn_ref[...], w_ref[...], preferred_element_type=F32)

    return pl.pallas_call(
        body, name=name,
        grid_spec=pltpu.PrefetchScalarGridSpec(
            num_scalar_prefetch=1, grid=(t // tm, nb),
            in_specs=[pl.BlockSpec((tm, d), lambda i, n, m: (i, 0)),
                      pl.BlockSpec((1, d), lambda i, n, m: (0, 0)),
                      pl.BlockSpec((None, d, tn), lambda i, n, m: (m[0], 0, n))],
            out_specs=[pl.BlockSpec((tm, tn), lambda i, n, m: (i, m[0] * nb + n)),
                       pl.BlockSpec((tm, d), lambda i, n, m: (i, 0))]),
        out_shape=[jax.ShapeDtypeStruct((t, s * ns), F32), jax.ShapeDtypeStruct((t, d), BF16)],
        compiler_params=_params(("arbitrary", "arbitrary")),
    )(me_idx, h, g, wg)


def _norm_in_rest(hn, wg, u, me_idx, name, after=None):
    t, d = hn.shape
    s, _, ns = wg.shape
    tm = 1408 if t % 1408 == 0 else _row_tile(t)
    tn = _col_tile(ns, (1536, 768, 384, 128))
    nb = ns // tn

    def body(m_ref, hn_ref, w_ref, u_in, u_ref):
        del u_in
        u_ref[...] = jnp.dot(hn_ref[...], w_ref[...], preferred_element_type=F32)

    def shard(n, m):
        return (m[0] + 1 + n // nb) % s

    body, more_specs, more = _behind(body, 4, after)
    return pl.pallas_call(
        body, name=name,
        grid_spec=pltpu.PrefetchScalarGridSpec(
            num_scalar_prefetch=1, grid=(t // tm, (s - 1) * nb),
            in_specs=[pl.BlockSpec((tm, d), lambda i, n, m: (i, 0)),
                      pl.BlockSpec((None, d, tn), lambda i, n, m: (shard(n, m), 0, n % nb)),
                      ANY] + more_specs,
            out_specs=pl.BlockSpec((tm, tn), lambda i, n, m: (i, shard(n, m) * nb + n % nb))),
        out_shape=jax.ShapeDtypeStruct(u.shape, u.dtype),
        input_output_aliases={3: 0},
        compiler_params=_params(("arbitrary", "arbitrary")),
    )(me_idx, hn, wg, u, *more)


def _decay_consts(lam):
    z = -lam
    e = jnp.exp(-jnp.abs(z))
    u = 1.0 + e
    log1p_e = jnp.where(u == 1.0, e, jnp.log(u) * (e / (u - 1.0)))
    sp = jnp.maximum(z, 0.0) + log1p_e
    return -LRU_C * sp, LRU_C * _sig(z)


def _gates(xc, wr_ref, br_ref, wi_ref, bi_ref, c8, j, gb):
    sl = slice(j * gb, (j + 1) * gb)
    x16 = xc.astype(BF16)
    r = _sig(jnp.dot(x16, wr_ref[j], preferred_element_type=F32) + br_ref[:, sl])
    ig = _sig(jnp.dot(x16, wi_ref[j], preferred_element_type=F32) + bi_ref[:, sl])
    la = c8[:, sl] * r
    a = jnp.exp(la)
    sq = jnp.sqrt(-jnp.tanh(la) * (a * a + 1.0))
    return r, ig, a, sq


def _mix_fwd(u, wa, ba, wr, br, wi, bi, lam, wb, name, proj=None):
    t = u.shape[0]
    c = u.shape[1] // 6
    tc = MIX_CHUNK
    gb = wr.shape[1]
    nblk = c // gb
    ka, kb = wa.shape[0], wb.shape[0]
    n_proj = 0 if proj is None else 3

    def body(*refs):
        u_ref, wa_ref, ba_ref, wr_ref, br_ref, wi_ref, bi_ref, lam_ref, wb_ref = refs[:9]
        outs = refs[9 + n_proj:]
        y_ref, hs_ref = outs[:2]
        xa_ext, v_ext, xc_s, a_s, b_s, carry_s = outs[-6:]

        @pl.when(pl.program_id(0) == 0)
        def _():
            xa_ext[0:SUBLANES, :] = jnp.zeros((SUBLANES, c), F32)
            v_ext[0:SUBLANES, :] = jnp.zeros((SUBLANES, c), F32)
            carry_s[...] = jnp.zeros_like(carry_s)

        xa_ext[SUBLANES:SUBLANES + tc, :] = u_ref[:, 0:c]
        xc = ba_ref[...]
        for k in range(ka):
            xc = xc + wa_ref[pl.ds(k, 1), :] * xa_ext[pl.ds(SUBLANES - (ka - 1) + k, tc), :]
        xc_s[...] = xc
        c8, _ = _decay_consts(lam_ref[...])
        for j in range(nblk):
            sl = slice(j * gb, (j + 1) * gb)
            xcj = xc_s[:, sl]
            _, ig, a, sq = _gates(xcj, wr_ref, br_ref, wi_ref, bi_ref, c8, j, gb)
            a_s[:, sl] = a
            b_s[:, sl] = sq * (ig * xcj)

        row = lax.broadcasted_iota(jnp.int32, (SUBLANES, c), 0)

        def scan_step(j, _):
            off = pl.multiple_of(j * SUBLANES, SUBLANES)
            av = a_s[pl.ds(off, SUBLANES), :]
            bv = b_s[pl.ds(off, SUBLANES), :]
            for d in (1, 2, 4):
                keep = row >= d
                bsh = jnp.where(keep, pltpu.roll(bv, d, axis=0), 0.0)
                ash = jnp.where(keep, pltpu.roll(av, d, axis=0), 1.0)
                bv = av * bsh + bv
                av = av * ash
            hv = av * carry_s[...] + bv
            hs_ref[pl.ds(off, SUBLANES), :] = hv
            carry_s[...] = hs_ref[pl.ds(off + SUBLANES - 1, 1), :]
            return 0

        lax.fori_loop(0, tc // SUBLANES, scan_step, 0, unroll=SCAN_UNROLL)

        ga = u_ref[:, c:2 * c]
        y_ref[:, 0:c] = (hs_ref[...] * (ga * _sig(ga))).astype(BF16)

        v_ext[SUBLANES:SUBLANES + tc, :] = u_ref[:, 3 * c:4 * c] * u_ref[:, 4 * c:5 * c]
        cv = wb_ref[pl.ds(0, 1), :] * v_ext[pl.ds(SUBLANES - (kb - 1), tc), :]
        for k in range(1, kb):
            cv = cv + wb_ref[pl.ds(k, 1), :] * v_ext[pl.ds(SUBLANES - (kb - 1) + k, tc), :]
        gbv = u_ref[:, 5 * c:6 * c]
        y_ref[:, c:2 * c] = (u_ref[:, 2 * c:3 * c] * cv * (gbv * _sig(gbv))).astype(BF16)

        xa_ext[0:SUBLANES, :] = xa_ext[tc:tc + SUBLANES, :]
        v_ext[0:SUBLANES, :] = v_ext[tc:tc + SUBLANES, :]

        if proj is not None:
            h_ref, wout_ref, g_ref = refs[9:12]
            ho_ref, hn_ref = outs[2:4]
            x = h_ref[...] + jnp.dot(y_ref[...], wout_ref[...], preferred_element_type=F32)
            ho_ref[...] = x
            r = lax.rsqrt(jnp.mean(x * x, axis=-1, keepdims=True) + RMS_EPS)
            hn_ref[...] = ((x * r) * g_ref[...]).astype(BF16)

    full = lambda shape: pl.BlockSpec(shape, lambda i: (0,) * len(shape))
    rows = lambda width: pl.BlockSpec((tc, width), lambda i: (i, 0))
    more_in, more_specs, more_out_specs, more_out = [], [], [], []
    if proj is not None:
        h, wout, g_next = proj
        d = h.shape[1]
        more_in = [h, wout, g_next]
        more_specs = [rows(d), full(wout.shape), full(g_next.shape)]
        more_out_specs = [rows(d), rows(d)]
        more_out = [jax.ShapeDtypeStruct((t, d), F32), jax.ShapeDtypeStruct((t, d), BF16)]
    return pl.pallas_call(
        body, name=name, grid=(t // tc,),
        in_specs=[rows(6 * c), full(wa.shape), full(ba.shape), full(wr.shape), full(br.shape),
                  full(wi.shape), full(bi.shape), full(lam.shape), full(wb.shape)] + more_specs,
        out_specs=[rows(2 * c), rows(c)] + more_out_specs,
        out_shape=[jax.ShapeDtypeStruct((t, 2 * c), BF16), jax.ShapeDtypeStruct((t, c), F32)] + more_out,
        scratch_shapes=[pltpu.VMEM((tc + SUBLANES, c), F32), pltpu.VMEM((tc + SUBLANES, c), F32),
                        pltpu.VMEM((tc, c), F32), pltpu.VMEM((tc, c), F32), pltpu.VMEM((tc, c), F32),
                        pltpu.VMEM((1, c), F32)],
        compiler_params=_params(("arbitrary",)),
    )(u, wa, ba, wr, br, wi, bi, lam, wb, *more_in)


ROW_DWA = 0
ROW_DBA = 4
ROW_DBR = 5
ROW_DBI = 6
ROW_DLAM = 7
ROW_DWB = 8
SMALL_ROWS = 16


def _mix_bwd(u, hs, dy, wa, ba, wr, br, wi, bi, lam, wb, name, after=None):
    t = u.shape[0]
    c = u.shape[1] // 6
    tc = MIX_CHUNK
    nt = t // tc
    gb = wr.shape[1]
    nblk = c // gb
    ka, kb = wa.shape[0], wb.shape[0]
    assert ka <= ROW_DBA and kb <= SMALL_ROWS - ROW_DWB
    hb = tc // SUBLANES

    def body(u_ref, uh_ref, hs_ref, hsh_ref, dy_ref, wa_ref, ba_ref, wr_ref, br_ref, wi_ref, bi_ref, lam_ref, wb_ref,
             du_ref, dsm_ref, dwr_ref, dwi_ref,
             xa_ext, v_ext, hs_ext, a_ext, ds_ext, dxc_ext, dcv_ext, xc_s, r_s, i_s, sq_s, g_s, an_s):
        i = pl.program_id(0)
        chunk = nt - 1 - i
        tail = slice(tc, tc + SUBLANES)
        head = slice(0, SUBLANES)

        @pl.when(i == 0)
        def _():
            zero = jnp.zeros((SUBLANES, c), F32)
            a_ext[tail, :] = zero
            ds_ext[tail, :] = zero
            dxc_ext[tail, :] = zero
            dcv_ext[tail, :] = zero
            dsm_ref[...] = jnp.zeros_like(dsm_ref)
            dwr_ref[...] = jnp.zeros_like(dwr_ref)
            dwi_ref[...] = jnp.zeros_like(dwi_ref)

        prev = jnp.where(chunk > 0, 1.0, 0.0)
        xa_ext[head, :] = uh_ref[:, 0:c] * prev
        xa_ext[SUBLANES:SUBLANES + tc, :] = u_ref[:, 0:c]
        v_ext[head, :] = uh_ref[:, 3 * c:4 * c] * uh_ref[:, 4 * c:5 * c] * prev
        v_ext[SUBLANES:SUBLANES + tc, :] = u_ref[:, 3 * c:4 * c] * u_ref[:, 4 * c:5 * c]
        hs_ext[head, :] = hsh_ref[...] * prev
        hs_ext[SUBLANES:SUBLANES + tc, :] = hs_ref[...]

        xc = ba_ref[...]
        for k in range(ka):
            xc = xc + wa_ref[pl.ds(k, 1), :] * xa_ext[pl.ds(SUBLANES - (ka - 1) + k, tc), :]
        xc_s[...] = xc
        c8, dc8 = _decay_consts(lam_ref[...])
        for j in range(nblk):
            sl = slice(j * gb, (j + 1) * gb)
            r, ig, a, sq = _gates(xc_s[:, sl], wr_ref, br_ref, wi_ref, bi_ref, c8, j, gb)
            r_s[:, sl] = r
            i_s[:, sl] = ig
            sq_s[:, sl] = sq
            a_ext[0:tc, sl] = a

        ga = u_ref[:, c:2 * c]
        sga = _sig(ga)
        g_s[...] = dy_ref[:, 0:c] * (ga * sga)
        an_s[...] = a_ext[pl.ds(1, tc), :]

        row = lax.broadcasted_iota(jnp.int32, (SUBLANES, c), 0)

        def scan_step(j, _):
            off = pl.multiple_of(tc - SUBLANES - j * SUBLANES, SUBLANES)
            av = an_s[pl.ds(off, SUBLANES), :]
            bv = g_s[pl.ds(off, SUBLANES), :]
            for d in (1, 2, 4):
                keep = row < SUBLANES - d
                bsh = jnp.where(keep, pltpu.roll(bv, SUBLANES - d, axis=0), 0.0)
                ash = jnp.where(keep, pltpu.roll(av, SUBLANES - d, axis=0), 1.0)
                bv = av * bsh + bv
                av = av * ash
            ds_ext[pl.ds(off, SUBLANES), :] = av * ds_ext[pl.ds(off + SUBLANES, 1), :] + bv
            return 0

        lax.fori_loop(0, tc // SUBLANES, scan_step, 0, unroll=SCAN_UNROLL)

        def acc(row_index, val):
            dsm_ref[pl.ds(row_index, 1), :] += jnp.sum(val, axis=0, keepdims=True)

        def acc_block(row_index, sl, val):
            dsm_ref[pl.ds(row_index, 1), sl] += jnp.sum(val, axis=0, keepdims=True)

        for j in range(nblk):
            sl = slice(j * gb, (j + 1) * gb)
            ds = ds_ext[0:tc, sl]
            hprev = hs_ext[pl.ds(SUBLANES - 1, tc), sl]
            a = a_ext[0:tc, sl]
            sq = sq_s[:, sl]
            ig = i_s[:, sl]
            r = r_s[:, sl]
            xcj = xc_s[:, sl]
            t1 = ds * xcj
            dla = (ds * hprev) * a - (t1 * ig) * ((a * a) * lax.rsqrt(sq * sq))
            acc_block(ROW_DLAM, sl, dla * r)
            dpr = (dla * c8[:, sl]) * (r * (1.0 - r))
            dpi = (t1 * sq) * (ig * (1.0 - ig))
            acc_block(ROW_DBR, sl, dpr)
            acc_block(ROW_DBI, sl, dpi)
            p16 = dpr.astype(BF16)
            q16 = dpi.astype(BF16)
            x16 = xcj.astype(BF16)
            dwr_ref[j] += lax.dot_general(x16, p16, TN_DIMS, preferred_element_type=F32)
            dwi_ref[j] += lax.dot_general(x16, q16, TN_DIMS, preferred_element_type=F32)
            dxc = (ds * (sq * ig)
                   + lax.dot_general(p16, wr_ref[j], NT_DIMS, preferred_element_type=F32)
                   + lax.dot_general(q16, wi_ref[j], NT_DIMS, preferred_element_type=F32))
            dxc_ext[0:tc, sl] = dxc
            acc_block(ROW_DBA, sl, dxc)

        dsilu_a = sga * (1.0 + ga * (1.0 - sga))
        du_ref[:, c:2 * c] = (dy_ref[:, 0:c] * hs_ref[...] * dsilu_a).astype(BF16)

        dxc = dxc_ext[0:tc, :]
        dxa = wa_ref[pl.ds(ka - 1, 1), :] * dxc
        acc(ROW_DWA + ka - 1, dxc * xa_ext[SUBLANES:SUBLANES + tc, :])
        for k in range(ka - 1):
            acc(ROW_DWA + k, dxc * xa_ext[pl.ds(SUBLANES - (ka - 1) + k, tc), :])
            dxa = dxa + wa_ref[pl.ds(k, 1), :] * dxc_ext[pl.ds(ka - 1 - k, tc), :]
        du_ref[:, 0:c] = dxa.astype(BF16)

        cv = wb_ref[pl.ds(0, 1), :] * v_ext[pl.ds(SUBLANES - (kb - 1), tc), :]
        for k in range(1, kb):
            cv = cv + wb_ref[pl.ds(k, 1), :] * v_ext[pl.ds(SUBLANES - (kb - 1) + k, tc), :]
        gbv = u_ref[:, 5 * c:6 * c]
        sgb = _sig(gbv)
        silu_b = gbv * sgb
        dyb = dy_ref[:, c:2 * c]
        gB = u_ref[:, 2 * c:3 * c]
        du_ref[:, 2 * c:3 * c] = (dyb * cv * silu_b).astype(BF16)
        du_ref[:, 5 * c:6 * c] = (dyb * gB * cv * (sgb * (1.0 + gbv * (1.0 - sgb)))).astype(BF16)
        dcv = dyb * gB * silu_b
        dcv_ext[0:tc, :] = dcv
        dv = wb_ref[pl.ds(kb - 1, 1), :] * dcv
        acc(ROW_DWB + kb - 1, dcv * v_ext[SUBLANES:SUBLANES + tc, :])
        for k in range(kb - 1):
            acc(ROW_DWB + k, dcv * v_ext[pl.ds(SUBLANES - (kb - 1) + k, tc), :])
            dv = dv + wb_ref[pl.ds(k, 1), :] * dcv_ext[pl.ds(kb - 1 - k, tc), :]
        du_ref[:, 3 * c:4 * c] = (dv * u_ref[:, 4 * c:5 * c]).astype(BF16)
        du_ref[:, 4 * c:5 * c] = (dv * u_ref[:, 3 * c:4 * c]).astype(BF16)

        a_ext[tail, :] = a_ext[head, :]
        ds_ext[tail, :] = ds_ext[head, :]
        dxc_ext[tail, :] = dxc_ext[head, :]
        dcv_ext[tail, :] = dcv_ext[head, :]

        @pl.when(i == nt - 1)
        def _():
            dsm_ref[pl.ds(ROW_DLAM, 1), :] = dsm_ref[pl.ds(ROW_DLAM, 1), :] * dc8

    full = lambda shape: pl.BlockSpec(shape, lambda i: (0,) * len(shape))
    rev = lambda i: (nt - 1 - i, 0)
    halo = lambda i: (jnp.maximum((nt - 1 - i) * hb - 1, 0), 0)
    ext = pltpu.VMEM((tc + SUBLANES, c), F32)
    blk = pltpu.VMEM((tc, c), F32)
    body, more_specs, more = _behind(body, 13, after)
    return pl.pallas_call(
        body, name=name, grid=(nt,),
        in_specs=[pl.BlockSpec((tc, 6 * c), rev), pl.BlockSpec((SUBLANES, 6 * c), halo),
                  pl.BlockSpec((tc, c), rev), pl.BlockSpec((SUBLANES, c), halo),
                  pl.BlockSpec((tc, 2 * c), rev),
                  full(wa.shape), full(ba.shape), full(wr.shape), full(br.shape),
                  full(wi.shape), full(bi.shape), full(lam.shape), full(wb.shape)] + more_specs,
        out_specs=[pl.BlockSpec((tc, 6 * c), rev), full((SMALL_ROWS, c)), full(wr.shape), full(wi.shape)],
        out_shape=[jax.ShapeDtypeStruct((t, 6 * c), BF16), jax.ShapeDtypeStruct((SMALL_ROWS, c), F32),
                   jax.ShapeDtypeStruct(wr.shape, F32), jax.ShapeDtypeStruct(wi.shape, F32)],
        scratch_shapes=[ext] * 7 + [blk] * 6,
        compiler_params=_params(("arbitrary",)),
    )(u, u, hs, hs, dy, wa, ba, wr, br, wi, bi, lam, wb, *more)


def _behind(body, n_in, after):
    if after is None:
        return body, [], []
    return (lambda *refs: body(*refs[:n_in], *refs[n_in + 1:])), [ANY], [after]


def _in_proj(hn, wg, name, after=None):
    t, d = hn.shape
    s, _, ns = wg.shape
    tm = 1408 if t % 1408 == 0 else _row_tile(t)

    def body(hn_ref, w_ref, u_ref):
        u_ref[...] = jnp.dot(hn_ref[...], w_ref[...], preferred_element_type=F32)

    body, more_specs, more = _behind(body, 2, after)
    return pl.pallas_call(
        body, name=name, grid=(t // tm, s),
        in_specs=[pl.BlockSpec((tm, d), lambda i, n: (i, 0)),
                  pl.BlockSpec((None, d, ns), lambda i, n: (n, 0, 0))] + more_specs,
        out_specs=pl.BlockSpec((tm, ns), lambda i, n: (i, n)),
        out_shape=jax.ShapeDtypeStruct((t, s * ns), F32),
        compiler_params=_params(("arbitrary", "arbitrary")),
    )(hn, wg, *more)


def _out_proj_dw(y, dout, name, after=None):
    t, dm = y.shape
    d = dout.shape[1]
    tmm = _col_tile(dm, (1024, 512, 256))
    tn = _col_tile(d, (512, 256))

    def body(y_ref, g_ref, o_ref):
        o_ref[...] = lax.dot_general(y_ref[...], g_ref[...].astype(BF16), TN_DIMS, preferred_element_type=F32)

    body, more_specs, more = _behind(body, 2, after)
    return pl.pallas_call(
        body, name=name, grid=(d // tn, dm // tmm),
        in_specs=[pl.BlockSpec((t, tmm), lambda n, m: (0, m)),
                  pl.BlockSpec((t, tn), lambda n, m: (0, n))] + more_specs,
        out_specs=pl.BlockSpec((tmm, tn), lambda n, m: (m, n)),
        out_shape=jax.ShapeDtypeStruct((dm, d), F32),
        compiler_params=_params(("arbitrary", "arbitrary")),
    )(y, dout, *more)


def _in_proj_bwd(du, wg, h, g, dout, name, after=None, split=None, w_below=None):
    t, d = h.shape
    s, _, ns = wg.shape
    tm = 1408 if t % 1408 == 0 else _row_tile(t)
    tn = _col_tile(d, (512, 256))

    def mm_body(du_ref, w_ref, o_ref):
        total = lax.dot_general(du_ref[:, 0:ns], w_ref[0], NT_DIMS, preferred_element_type=F32)
        for a in range(1, s):
            total = total + lax.dot_general(du_ref[:, a * ns:(a + 1) * ns], w_ref[a], NT_DIMS,
                                            preferred_element_type=F32)
        o_ref[...] = total

    mm_body, more_specs, more = _behind(mm_body, 2, after)
    dhn = pl.pallas_call(
        mm_body, name=name, grid=(t // tm, d // tn),
        in_specs=[pl.BlockSpec((tm, s * ns), lambda i, n: (i, 0)),
                  pl.BlockSpec((s, tn, ns), lambda i, n: (0, n, 0))] + more_specs,
        out_specs=pl.BlockSpec((tm, tn), lambda i, n: (i, n)),
        out_shape=jax.ShapeDtypeStruct((t, d), F32),
        compiler_params=_params(("arbitrary", "arbitrary")),
    )(du, wg, *more)

    tr = 352 if t % 352 == 0 else 192
    nt = t // tr

    def row_grad(dhn_ref, h_ref, g_ref, dout_ref, dg_ref):
        @pl.when(pl.program_id(0) == 0)
        def _():
            dg_ref[...] = jnp.zeros_like(dg_ref)

        x = h_ref[...]
        dn = dhn_ref[...]
        r = lax.rsqrt(jnp.mean(x * x, axis=-1, keepdims=True) + RMS_EPS)
        gd = dn * g_ref[...]
        dot = jnp.mean(gd * x, axis=-1, keepdims=True)
        dg_ref[...] += jnp.sum(dn * (x * r), axis=0, keepdims=True)
        return dout_ref[...] + (r * gd - x * ((r * r * r) * dot))

    rows = pl.BlockSpec((tr, d), lambda i: (i, 0))
    one = pl.BlockSpec((1, d), lambda i: (0, 0))
    if split is None:
        dm = w_below.shape[0]

        def norm_body(dhn_ref, h_ref, g_ref, dout_ref, w_ref, dh_ref, dg_ref, dy_ref):
            dh = row_grad(dhn_ref, h_ref, g_ref, dout_ref, dg_ref)
            dh_ref[...] = dh
            dy_ref[...] = lax.dot_general(dh.astype(BF16), w_ref[...], NT_DIMS, preferred_element_type=F32)

        return pl.pallas_call(
            norm_body, name=name + "_norm", grid=(nt,),
            in_specs=[rows, rows, one, rows, pl.BlockSpec((dm, d), lambda i: (0, 0))],
            out_specs=[rows, one, pl.BlockSpec((tr, dm), lambda i: (i, 0))],
            out_shape=[jax.ShapeDtypeStruct((t, d), F32), jax.ShapeDtypeStruct((1, d), F32),
                       jax.ShapeDtypeStruct((t, dm), F32)],
            compiler_params=_params(("arbitrary",)),
        )(dhn, h, g, dout, w_below)

    n_head, n_body = split
    n_first = tr - n_head
    n_last = n_head + n_body - (nt - 1) * tr
    assert nt >= 2 and 0 < n_head < tr and 0 < n_last <= tr and n_head % SUBLANES == 0 and n_last % SUBLANES == 0

    def split_body(dhn_ref, h_ref, g_ref, dout_ref, body_ref, head_ref, dg_ref, stage, sems):
        i = pl.program_id(0)
        slot = i % 2

        def first_copy(sl):
            return pltpu.make_async_copy(stage.at[sl, pl.ds(n_head, n_first)], body_ref.at[pl.ds(0, n_first)], sems.at[sl])

        def middle_copy(sl, step):
            start = pl.multiple_of(step * tr - n_head, SUBLANES)
            return pltpu.make_async_copy(stage.at[sl], body_ref.at[pl.ds(start, tr)], sems.at[sl])

        def last_copy(sl):
            return pltpu.make_async_copy(stage.at[sl, pl.ds(0, n_last)],
                                         body_ref.at[pl.ds((nt - 1) * tr - n_head, n_last)], sems.at[sl])

        dh = row_grad(dhn_ref, h_ref, g_ref, dout_ref, dg_ref)

        @pl.when(i == 2)
        def _():
            first_copy(0).wait()

        @pl.when(i > 2)
        def _():
            middle_copy(slot, i - 2).wait()

        stage[slot] = dh

        @pl.when(i == 0)
        def _():
            head_ref[...] = stage[0, 0:n_head, :]
            first_copy(0).start()

        @pl.when((i > 0) & (i < nt - 1))
        def _():
            middle_copy(slot, i).start()

        @pl.when(i == nt - 1)
        def _():
            last = last_copy((nt - 1) % 2)
            last.start()
            if nt == 2:
                first_copy(0).wait()
            else:
                middle_copy((nt - 2) % 2, nt - 2).wait()
            last.wait()

    return pl.pallas_call(
        split_body, name=name + "_norm", grid=(nt,),
        in_specs=[rows, rows, one, rows],
        out_specs=[ANY, pl.BlockSpec((n_head, d), lambda i: (0, 0)), one],
        out_shape=[jax.ShapeDtypeStruct((n_body, d), F32), jax.ShapeDtypeStruct((n_head, d), F32),
                   jax.ShapeDtypeStruct((1, d), F32)],
        scratch_shapes=[pltpu.VMEM((2, tr, d), F32), pltpu.SemaphoreType.DMA((2,))],
        compiler_params=_params(("arbitrary",)),
    )(dhn, h, g, dout)


def _in_proj_dw(hn, du, s, name, after=None):
    t, d = hn.shape
    ns = du.shape[1] // s
    tmm = _col_tile(d, (1024, 512, 256))
    tn = _col_tile(ns, (768, 384, 128))
    nb = ns // tn

    def body(hn_ref, du_ref, o_ref):
        o_ref[...] = lax.dot_general(hn_ref[...], du_ref[...], TN_DIMS, preferred_element_type=F32)

    body, more_specs, more = _behind(body, 2, after)
    return pl.pallas_call(
        body, name=name, grid=(d // tmm, s * nb),
        in_specs=[pl.BlockSpec((t, tmm), lambda m, n: (0, m)),
                  pl.BlockSpec((t, tn), lambda m, n: (0, n))] + more_specs,
        out_specs=pl.BlockSpec((None, tmm, tn), lambda m, n: (n // nb, m, n % nb)),
        out_shape=jax.ShapeDtypeStruct((s, d, ns), F32),
        compiler_params=_params(("arbitrary", "arbitrary")),
    )(hn, du, *more)


def _out_proj_loss(h, y, w, tgt, g, n_meta, t_real, name):
    t, d = h.shape
    dm = y.shape[1]
    tm = 352 if t % 352 == 0 else 192

    def body(h_ref, y_ref, w_ref, t_ref, g_ref, dh_ref, loss_ref, dg_ref, dmix_ref):
        i = pl.program_id(0)

        @pl.when(i == 0)
        def _():
            loss_ref[...] = jnp.zeros_like(loss_ref)
            dg_ref[...] = jnp.zeros_like(dg_ref)

        x = h_ref[...] + jnp.dot(y_ref[...], w_ref[...], preferred_element_type=F32)
        gv = g_ref[...]
        r = lax.rsqrt(jnp.mean(x * x, axis=-1, keepdims=True) + RMS_EPS)
        xr = x * r
        rows = i * tm + lax.broadcasted_iota(jnp.int32, (tm, 1), 0)
        valid = (rows >= n_meta) & (rows < t_real)
        err = jnp.where(valid, xr * gv - t_ref[...], 0.0)
        loss_ref[...] += 0.5 * jnp.sum(jnp.mean(err * err, axis=-1, keepdims=True))
        dy = err * (1.0 / d)
        gd = dy * gv
        dot = jnp.mean(gd * x, axis=-1, keepdims=True)
        dh = r * gd - x * ((r * r * r) * dot)
        dh_ref[...] = dh
        dg_ref[...] += jnp.sum(dy * xr, axis=0, keepdims=True)
        dmix_ref[...] = lax.dot_general(dh.astype(BF16), w_ref[...], NT_DIMS, preferred_element_type=F32)

    rows = pl.BlockSpec((tm, d), lambda i: (i, 0))
    wide = pl.BlockSpec((tm, dm), lambda i: (i, 0))
    return pl.pallas_call(
        body, name=name, grid=(t // tm,),
        in_specs=[rows, wide, pl.BlockSpec((dm, d), lambda i: (0, 0)), rows, pl.BlockSpec((1, d), lambda i: (0, 0))],
        out_specs=[rows, pl.BlockSpec((1, LANES), lambda i: (0, 0)), pl.BlockSpec((1, d), lambda i: (0, 0)), wide],
        out_shape=[jax.ShapeDtypeStruct((t, d), F32), jax.ShapeDtypeStruct((1, LANES), F32),
                   jax.ShapeDtypeStruct((1, d), F32), jax.ShapeDtypeStruct((t, dm), F32)],
        compiler_params=_params(("arbitrary",)),
    )(h, y, w, tgt, g)


def _adamw_rows(rows, cols):
    for cand in (512, 256, 128, 64, 32, 16, 8):
        if rows % cand == 0 and cand * cols * 4 <= 2 * 1024 * 1024:
            return cand
    return rows


def _adamw_math(w_ref, g_ref, m_ref, v_ref, d_ref, nm_ref, nv_ref):
    gv = g_ref[...]
    m2 = ADAM_B1 * m_ref[...] + (1.0 - ADAM_B1) * gv
    v2 = ADAM_B2 * v_ref[...] + (1.0 - ADAM_B2) * (gv * gv)
    m_hat = m2 / (1.0 - ADAM_B1 ** ADAM_STEP)
    v_hat = v2 / (1.0 - ADAM_B2 ** ADAM_STEP)
    d_ref[...] = -ADAM_LR * (m_hat / (jnp.sqrt(v_hat) + ADAM_EPS) + ADAM_WD * w_ref[...])
    nm_ref[...] = m2
    nv_ref[...] = v2


def _adamw(w, g, m, v, name, after=None):
    shape = w.shape
    assert len(shape) >= 2 and w.size * 4 <= 2 * 1024 * 1024

    def body(*refs):
        _adamw_math(*refs)

    body, more_specs, more = _behind(body, 4, after)
    spec = pl.BlockSpec(shape, lambda i: (0,) * len(shape))
    return pl.pallas_call(
        body, name=name, grid=(1,),
        in_specs=[spec] * 4 + more_specs, out_specs=[spec] * 3,
        out_shape=[jax.ShapeDtypeStruct(shape, F32)] * 3,
        compiler_params=_params(("arbitrary",)),
    )(w, g, m, v, *more)


def _adamw_layer(w, g, m, v, layer, kept, name, after=None):
    nl, rows, cols = w.shape
    tr = _adamw_rows(rows, cols)
    n_kept = 0 if kept is None else 3

    def body(*refs):
        _adamw_math(*refs[:4], *refs[4 + n_kept:])

    body, more_specs, more = _behind(body, 4 + n_kept, after)
    lay = pl.BlockSpec((None, tr, cols), lambda i: (layer, i, 0))
    return pl.pallas_call(
        body, name=name, grid=(rows // tr,),
        in_specs=[lay, pl.BlockSpec((tr, cols), lambda i: (i, 0)), lay, lay] + [ANY] * n_kept + more_specs,
        out_specs=[lay] * 3,
        out_shape=[jax.ShapeDtypeStruct((nl, rows, cols), F32)] * 3,
        input_output_aliases={4 + k: k for k in range(n_kept)},
        compiler_params=_params(("arbitrary",)),
    )(w, g, m, v, *([] if kept is None else kept), *more)


def _pair_add(x, ra, c_idx, name):
    s, _, rows, cols = x.shape
    tr = _slab_rows(rows, cols)

    def body(c_ref, x_ref, r_ref, o_ref):
        o_ref[...] = (x_ref[...] + r_ref[...]).astype(BF16)

    return pl.pallas_call(
        body, name=name,
        grid_spec=pltpu.PrefetchScalarGridSpec(
            num_scalar_prefetch=1, grid=(s, rows // tr),
            in_specs=[pl.BlockSpec((None, None, tr, cols), lambda a, i, c_ref: (a, c_ref[0], i, 0)),
                      pl.BlockSpec((None, tr, cols), lambda a, i, c_ref: (a, i, 0))],
            out_specs=pl.BlockSpec((None, tr, cols), lambda a, i, c_ref: (a, i, 0))),
        out_shape=jax.ShapeDtypeStruct((s, rows, cols), BF16),
        compiler_params=_params(("arbitrary", "arbitrary")),
    )(c_idx, x, ra)


def _chip_sum(rc, p, where, n_slots, name):
    s, rows, cols = rc.shape
    tr = _slab_rows(rows, cols)

    def body(w_ref, x_ref, p_ref, o_ref):
        me = w_ref[0]
        total = jnp.where(me == 0, p_ref[...], x_ref[0]).astype(F32)
        for a in range(1, s):
            total = total + jnp.where(me == a, p_ref[...], x_ref[a]).astype(F32)
        o_ref[...] = total

    return pl.pallas_call(
        body, name=name,
        grid_spec=pltpu.PrefetchScalarGridSpec(
            num_scalar_prefetch=1, grid=(rows // tr,),
            in_specs=[pl.BlockSpec((s, tr, cols), lambda i, w_ref: (0, i, 0)),
                      pl.BlockSpec((None, tr, cols), lambda i, w_ref: (w_ref[0], i, 0))],
            out_specs=pl.BlockSpec((None, tr, cols), lambda i, w_ref: (w_ref[1], i, 0))),
        out_shape=jax.ShapeDtypeStruct((n_slots, rows, cols), F32),
        compiler_params=_params(("arbitrary",)),
    )(where, rc, p)


def _cast_place(w, layer, me_idx, name, after=None):
    _, rows, cols = w.shape
    tr = _slab_rows(rows, cols)

    def body(m_ref, w_ref, o_ref):
        o_ref[...] = w_ref[...].astype(BF16)

    body, more_specs, more = _behind(body, 2, after)
    return pl.pallas_call(
        body, name=name,
        grid_spec=pltpu.PrefetchScalarGridSpec(
            num_scalar_prefetch=1, grid=(rows // tr,),
            in_specs=[pl.BlockSpec((None, tr, cols), lambda i, m_ref: (layer, i, 0))] + more_specs,
            out_specs=pl.BlockSpec((None, tr, cols), lambda i, m_ref: (m_ref[0], i, 0))),
        out_shape=jax.ShapeDtypeStruct((N_CHIPS, rows, cols), BF16),
        compiler_params=_params(("arbitrary",)),
    )(me_idx, w, *more)


def _place():
    x, y, c = lax.axis_index("x"), lax.axis_index("y"), lax.axis_index("c")
    chips = [(1 - x, y), (x, 1 - y), (1 - x, 1 - y)]
    return x, y, c, chips


def _chip_index(cx, cy):
    return 2 * cx + cy


def _gather_copies(bufs, stage):
    x, y, c, chips = _place()
    me = _chip_index(x, y)
    copies = []
    for b in bufs:
        for chip in chips:
            src = _chip_index(*chip)
            if stage == 0:
                copies.append((b.at[me, c], (*chip, c), b.at[src, c]))
            else:
                copies.append((b.at[src, c], (x, y, 1 - c), b.at[src, 1 - c]))
    return copies


def _remote(ref, peer, ssem, rsem, k):
    return pltpu.make_async_remote_copy(src_ref=ref, dst_ref=ref, send_sem=ssem.at[k], recv_sem=rsem.at[k],
                                        device_id=peer, device_id_type=MESH)


def _gather_first(bufs, small):
    n = len(bufs)
    k = 3 * n

    def body(*refs):
        sm_ref = refs[n]
        b_refs, smg_ref = refs[n + 1:2 * n + 1], refs[2 * n + 1]
        lsem, ssem, rsem = refs[2 * n + 2:]
        x, y, c, chips = _place()
        me = _chip_index(x, y)
        local = pltpu.make_async_copy(sm_ref, smg_ref.at[me], lsem)
        local.start()
        first = _gather_copies(b_refs, 0)
        second = _gather_copies(b_refs, 1)
        started = []
        for i, (ref, peer, _) in enumerate(first):
            started.append(_remote(ref, peer, ssem, rsem, i))
        for j, chip in enumerate(chips):
            started.append(pltpu.make_async_remote_copy(
                src_ref=sm_ref, dst_ref=smg_ref.at[me], send_sem=ssem.at[2 * k + j], recv_sem=rsem.at[2 * k + j],
                device_id=(*chip, c), device_id_type=MESH))
        for cp in started:
            cp.start()
        for i, (_, peer, lands) in enumerate(first):
            _remote(lands, peer, ssem, rsem, i).wait_recv()
            ref, sib, _ = second[i]
            fwd = _remote(ref, sib, ssem, rsem, k + i)
            fwd.start()
            started.append(fwd)
        for i, (_, sib, lands) in enumerate(second):
            _remote(lands, sib, ssem, rsem, k + i).wait_recv()
        for j, chip in enumerate(chips):
            theirs = smg_ref.at[_chip_index(*chip)]
            pltpu.make_async_remote_copy(src_ref=theirs, dst_ref=theirs, send_sem=ssem.at[2 * k + j],
                                         recv_sem=rsem.at[2 * k + j], device_id=(*chip, c),
                                         device_id_type=MESH).wait_recv()
        for cp in started:
            cp.wait_send()
        local.wait()

    return pl.pallas_call(
        body, name="gather_first",
        in_specs=[ANY] * (n + 1), out_specs=[ANY] * (n + 1),
        out_shape=[jax.ShapeDtypeStruct(b.shape, b.dtype) for b in bufs]
        + [jax.ShapeDtypeStruct((N_CHIPS,) + small.shape, small.dtype)],
        input_output_aliases={i: i for i in range(n)},
        scratch_shapes=[pltpu.SemaphoreType.DMA, pltpu.SemaphoreType.DMA((2 * k + 3,)),
                        pltpu.SemaphoreType.DMA((2 * k + 3,))],
    )(*bufs, small)


HBM = pl.BlockSpec(memory_space=pltpu.HBM)
SEM = pl.BlockSpec(memory_space=pltpu.SEMAPHORE)
DATAFLOW = pltpu.SideEffectType.DATAFLOW_SIDE_EFFECTING


def _copies_start(bufs, plan, n_copies, name, after=None):
    n = len(bufs)
    extra = [] if after is None else [after]

    def body(*refs):
        refs = refs[:n] + refs[n + len(extra):]
        ssem, rsem = refs[n], refs[n + 1]
        b_refs, token = refs[n + 2:2 * n + 2], refs[2 * n + 2]
        copies = plan(b_refs)
        assert len(copies) == n_copies
        for i, (src, dst, peer, _) in enumerate(copies):
            pltpu.make_async_remote_copy(src_ref=src, dst_ref=dst, send_sem=ssem.at[i], recv_sem=rsem.at[i],
                                         device_id=peer, device_id_type=MESH).start()
        token[...] = jnp.zeros_like(token)

    return pl.pallas_call(
        body, name=name,
        out_shape=(pltpu.SemaphoreType.DMA((n_copies,)), pltpu.SemaphoreType.DMA((n_copies,)),
                   *[pltpu.HBM(b.shape, b.dtype) for b in bufs], jax.ShapeDtypeStruct((SUBLANES, LANES), F32)),
        in_specs=[HBM] * n + [ANY] * len(extra),
        out_specs=(SEM, SEM, *[HBM] * n, pl.BlockSpec(memory_space=pltpu.VMEM)),
        input_output_aliases={i: 2 + i for i in range(n)},
        compiler_params=pltpu.CompilerParams(has_side_effects=DATAFLOW),
    )(*[pltpu.with_memory_space_constraint(b, pltpu.HBM) for b in bufs], *extra)


def _copies_wait(bufs, ssem, rsem, after, plan, name):
    n = len(bufs)
    afters = list(after) if isinstance(after, (list, tuple)) else [after]

    def body(*refs):
        b_refs, ssem_ref, rsem_ref = refs[:n], refs[n], refs[n + 1]
        for i, (src, dst, peer, lands) in enumerate(plan(b_refs)):
            pltpu.make_async_remote_copy(src_ref=src, dst_ref=dst, send_sem=ssem_ref.at[i], recv_sem=rsem_ref.at[i],
                                         device_id=peer, device_id_type=MESH).wait_send()
            pltpu.make_async_remote_copy(src_ref=lands, dst_ref=lands, send_sem=ssem_ref.at[i],
                                         recv_sem=rsem_ref.at[i], device_id=peer, device_id_type=MESH).wait_recv()

    return pl.pallas_call(
        body, name=name,
        out_shape=tuple(pltpu.HBM(b.shape, b.dtype) for b in bufs),
        in_specs=[HBM] * n + [SEM, SEM] + [ANY] * len(afters), out_specs=tuple([HBM] * n),
        input_output_aliases={i: i for i in range(n)},
        compiler_params=pltpu.CompilerParams(has_side_effects=DATAFLOW),
    )(*bufs, ssem, rsem, *afters)


def _gather_plan(stage):
    return lambda refs: [(ref, ref, peer, lands) for ref, peer, lands in _gather_copies(refs, stage)]


def _swap_plan(refs):
    n = len(refs) // 2
    x, y, c, _ = _place()
    return [(refs[a].at[:, 1 - c], refs[n + a], (x, y, 1 - c), refs[n + a]) for a in range(n)]


def _scatter_plan(refs):
    n = len(refs) // 2
    x, y, c, chips = _place()
    me = _chip_index(x, y)
    return [(refs[a].at[_chip_index(*chip)], refs[n + a].at[me], (*chip, c), refs[n + a].at[_chip_index(*chip)])
            for a in range(n) for chip in chips]


def _pair_gather_plan(refs):
    x, y, c, _ = _place()
    return [(r.at[c], r.at[c], (x, y, 1 - c), r.at[1 - c]) for r in refs]


def _pair_swap(xs, name):
    n = len(xs)

    def body(*refs):
        x_refs, o_refs, ssem, rsem = refs[:n], refs[n:2 * n], refs[2 * n], refs[2 * n + 1]
        x, y, c, _ = _place()
        copies = [pltpu.make_async_remote_copy(src_ref=x_refs[a].at[:, 1 - c], dst_ref=o_refs[a],
                                               send_sem=ssem.at[a], recv_sem=rsem.at[a],
                                               device_id=(x, y, 1 - c), device_id_type=MESH) for a in range(n)]
        for cp in copies:
            cp.start()
        for cp in copies:
            cp.wait()

    return pl.pallas_call(
        body, name=name, in_specs=[ANY] * n, out_specs=[ANY] * n,
        out_shape=[jax.ShapeDtypeStruct((a.shape[0],) + a.shape[2:], a.dtype) for a in xs],
        scratch_shapes=[pltpu.SemaphoreType.DMA((n,)), pltpu.SemaphoreType.DMA((n,))],
    )(*xs)


def _chip_scatter(ps):
    n = len(ps)

    def body(*refs):
        p_refs, o_refs, ssem, rsem = refs[:n], refs[n:2 * n], refs[2 * n], refs[2 * n + 1]
        x, y, c, chips = _place()
        me = _chip_index(x, y)
        sends = []
        for a in range(n):
            for j, chip in enumerate(chips):
                sends.append(pltpu.make_async_remote_copy(
                    src_ref=p_refs[a].at[_chip_index(*chip)], dst_ref=o_refs[a].at[me],
                    send_sem=ssem.at[3 * a + j], recv_sem=rsem.at[3 * a + j],
                    device_id=(*chip, c), device_id_type=MESH))
        for cp in sends:
            cp.start()
        for a in range(n):
            for j, chip in enumerate(chips):
                src = _chip_index(*chip)
                pltpu.make_async_remote_copy(
                    src_ref=p_refs[a].at[src], dst_ref=o_refs[a].at[src],
                    send_sem=ssem.at[3 * a + j], recv_sem=rsem.at[3 * a + j],
                    device_id=(*chip, c), device_id_type=MESH).wait_recv()
        for cp in sends:
            cp.wait_send()

    return pl.pallas_call(
        body, name="chip_scatter", in_specs=[ANY] * n, out_specs=[ANY] * n,
        out_shape=[jax.ShapeDtypeStruct(a.shape, a.dtype) for a in ps],
        scratch_shapes=[pltpu.SemaphoreType.DMA((3 * n,)), pltpu.SemaphoreType.DMA((3 * n,))],
    )(*ps)


def _final_gather(fs, rep):
    n = len(fs)

    def body(*refs):
        o_refs, repo_ref = refs[n + 1:2 * n + 1], refs[2 * n + 1]
        ssem, rsem = refs[2 * n + 2:]
        x, y, c, chips = _place()
        slot = 4 * x + 2 * y + c
        copies = [pltpu.make_async_remote_copy(src_ref=o_refs[a].at[c], dst_ref=o_refs[a].at[c],
                                               send_sem=ssem.at[a], recv_sem=rsem.at[a],
                                               device_id=(x, y, 1 - c), device_id_type=MESH) for a in range(n)]
        peers = [(x, y, 1 - c)] + [(*chip, c) for chip in chips] + [(*chip, 1 - c) for chip in chips]
        for k, peer in enumerate(peers):
            copies.append(pltpu.make_async_remote_copy(src_ref=repo_ref.at[slot], dst_ref=repo_ref.at[slot],
                                                       send_sem=ssem.at[n + k], recv_sem=rsem.at[n + k],
                                                       device_id=peer, device_id_type=MESH))
        for cp in copies:
            cp.start()
        for a in range(n):
            pltpu.make_async_remote_copy(src_ref=o_refs[a].at[1 - c], dst_ref=o_refs[a].at[1 - c],
                                         send_sem=ssem.at[a], recv_sem=rsem.at[a],
                                         device_id=(x, y, 1 - c), device_id_type=MESH).wait_recv()
        for k, peer in enumerate(peers):
            px, py, pc = peer
            theirs = repo_ref.at[4 * px + 2 * py + pc]
            pltpu.make_async_remote_copy(src_ref=theirs, dst_ref=theirs, send_sem=ssem.at[n + k], recv_sem=rsem.at[n + k],
                                         device_id=peer, device_id_type=MESH).wait_recv()
        for cp in copies:
            cp.wait_send()

    return pl.pallas_call(
        body, name="final_gather", in_specs=[ANY] * (n + 1), out_specs=[ANY] * (n + 1),
        out_shape=[jax.ShapeDtypeStruct(a.shape, a.dtype) for a in fs] + [jax.ShapeDtypeStruct(rep.shape, rep.dtype)],
        input_output_aliases={k: k for k in range(n + 1)},
        scratch_shapes=[pltpu.SemaphoreType.DMA((n + 7,)), pltpu.SemaphoreType.DMA((n + 7,))],
    )(*fs, rep)


def _block_diag(w, gb):
    nh, hd, _ = w.shape
    per = gb // hd
    w4 = w.reshape(nh // per, per, hd, hd)
    eye = jnp.eye(per, dtype=w.dtype)
    return jnp.einsum("jaik,ab->jaibk", w4, eye).reshape(nh // per, gb, gb)


def _diag_blocks(dense, hd):
    nj, gb, _ = dense.shape
    per = gb // hd
    d5 = dense.reshape(nj, per, hd, per, hd)
    return jnp.stack([d5[:, a, :, a, :] for a in range(per)], axis=1).reshape(nj * per, hd, hd)


def _round_up(n, q):
    return (n + q - 1) // q * q


def kernel(x, meta, norm_g, w_in, conv_a_w, conv_a_b, lru_wr, lru_br, lru_wi, lru_bi, lru_lambda, conv_b_w, w_out, final_g, loss_target, m_meta, m_norm_g, m_w_in, m_conv_a_w, m_conv_a_b, m_lru_wr, m_lru_br, m_lru_wi, m_lru_bi, m_lru_lambda, m_conv_b_w, m_w_out, m_final_g, v_meta, v_norm_g, v_w_in, v_conv_a_w, v_conv_a_b, v_lru_wr, v_lru_br, v_lru_wi, v_lru_bi, v_lru_lambda, v_conv_b_w, v_w_out, v_final_g):
    weights = dict(meta=meta, norm_g=norm_g, w_in=w_in, conv_a_w=conv_a_w, conv_a_b=conv_a_b, lru_wr=lru_wr,
                   lru_br=lru_br, lru_wi=lru_wi, lru_bi=lru_bi, lru_lambda=lru_lambda, conv_b_w=conv_b_w,
                   w_out=w_out, final_g=final_g)
    mom1 = dict(meta=m_meta, norm_g=m_norm_g, w_in=m_w_in, conv_a_w=m_conv_a_w, conv_a_b=m_conv_a_b,
                lru_wr=m_lru_wr, lru_br=m_lru_br, lru_wi=m_lru_wi, lru_bi=m_lru_bi, lru_lambda=m_lru_lambda,
                conv_b_w=m_conv_b_w, w_out=m_w_out, final_g=m_final_g)
    mom2 = dict(meta=v_meta, norm_g=v_norm_g, w_in=v_w_in, conv_a_w=v_conv_a_w, conv_a_b=v_conv_a_b,
                lru_wr=v_lru_wr, lru_br=v_lru_br, lru_wi=v_lru_wi, lru_bi=v_lru_bi, lru_lambda=v_lru_lambda,
                conv_b_w=v_conv_b_w, w_out=v_w_out, final_g=v_final_g)
    names = list(weights)

    assert x.shape[0] == 1
    seq, d = x.shape[1], x.shape[2]
    n_meta, ds = meta.shape
    depth = norm_g.shape[0]
    c = lru_lambda.shape[1]
    nh, hd = lru_wr.shape[1], lru_wr.shape[2]
    ns = w_in.shape[2]
    dms = w_out.shape[1]
    cs = conv_a_w.shape[2]
    ka, kb = conv_a_w.shape[1], conv_b_w.shape[1]
    s = N_CHIPS
    assert depth == N_CORES and d == s * ds and c == s * cs and s * ns == 6 * c and s * dms == 2 * c
    gb = min(GATE_BLOCK, c)
    t_real = n_meta + seq
    t = _round_up(t_real, ROW_QUANTUM)
    my_c = lax.axis_index("c").astype(jnp.int32)
    my_chip = (2 * lax.axis_index("x") + lax.axis_index("y")).astype(jnp.int32)
    c_idx = my_c.reshape(1)
    chip_idx = my_chip.reshape(1)

    sm_rows = _round_up(n_meta + depth * SUBLANES, 2 * SUBLANES)
    small = jnp.zeros((sm_rows, ds), F32)
    small = small.at[0:n_meta, :].set(meta)
    for l in range(depth):
        base = n_meta + l * SUBLANES
        small = small.at[base:base + ka, 0:cs].set(conv_a_w[l])
        small = small.at[base + ka:base + ka + kb, 0:cs].set(conv_b_w[l])
    (small_g,) = _gather_first([], small)
    meta_full = jnp.transpose(small_g[:, 0:n_meta, :], (1, 0, 2)).reshape(n_meta, d)
    wa_full, wb_full = [], []
    for l in range(depth):
        base = n_meta + l * SUBLANES
        wa_full.append(jnp.transpose(small_g[:, base:base + ka, 0:cs], (1, 0, 2)).reshape(ka, c))
        wb_full.append(jnp.transpose(small_g[:, base + ka:base + ka + kb, 0:cs], (1, 0, 2)).reshape(kb, c))
    win0 = _cast_place(w_in, 0, chip_idx, "cast_w_in_0").reshape(s, 2, d // 2, ns)
    ssem_w, rsem_w, win0, token_w = _copies_start([win0], _gather_plan(0), 3, "gather_win0_ici_start", after=small_g)
    win_b = [None] + [_cast_place(w_in, l, chip_idx, f"cast_w_in_{l}", after=token_w).reshape(s, 2, d // 2, ns)
                      for l in range(1, depth)]
    wout_b = [_cast_place(w_out, l, chip_idx, f"cast_w_out_{l}", after=token_w).reshape(s, 2, dms // 2, d)
              for l in range(depth)]
    h = jnp.concatenate([meta_full, x[0], jnp.zeros((t - t_real, d), F32)], axis=0) + token_w[0, 0]
    tgt = jnp.concatenate([jnp.zeros((n_meta, d), F32), loss_target[0], jnp.zeros((t - t_real, d), F32)],
                          axis=0) + token_w[0, 0]
    u_own, hn_own = _norm_in_own(h, norm_g[0].reshape(1, d), win0.reshape(s, d, ns), chip_idx, "norm_in_0_own")
    (win0,) = _copies_wait([win0], ssem_w, rsem_w, [u_own, tgt] + win_b[1:] + wout_b, _gather_plan(0),
                           "gather_win0_ici_wait")
    ssem_w, rsem_w, win0, token_w = _copies_start([win0], _gather_plan(1), 3, "gather_win0_d2d_start")
    def travel(buf, stage, tag, after):
        return _copies_start([buf], _gather_plan(stage), 3, f"gather_{tag}_{'d2d' if stage else 'ici'}_start",
                             after=after)

    def arrived(state, stage, tag, after):
        (buf,) = _copies_wait([state[2]], state[0], state[1], after, _gather_plan(stage),
                              f"gather_{tag}_{'d2d' if stage else 'ici'}_wait")
        return buf

    on_wout0 = travel(wout_b[0], 0, "wout0", token_w)
    on_win1 = travel(win_b[1], 0, "win1", on_wout0[3])
    on_wout1 = travel(wout_b[1], 0, "wout1", on_win1[3])
    token = on_wout1[3]
    (win_b[0],) = _copies_wait([win0], ssem_w, rsem_w, token, _gather_plan(1), "gather_win0_d2d_wait")

    layer_w = []
    for l in range(depth):
        layer_w.append(dict(
            g=norm_g[l].reshape(1, d), wa=wa_full[l], ba=conv_a_b[l].reshape(1, c),
            wr=_block_diag(lru_wr[l], gb).astype(BF16), br=lru_br[l].reshape(1, c),
            wi=_block_diag(lru_wi[l], gb).astype(BF16), bi=lru_bi[l].reshape(1, c),
            lam=lru_lambda[l].reshape(1, c), wb=wb_full[l]))
    saved = []
    for l, lw in enumerate(layer_w):
        first = l == 0
        lw["win"] = win_b[l].reshape(s, d, ns)
        mixer_w = (lw["wa"], lw["ba"], lw["wr"], lw["br"], lw["wi"], lw["bi"], lw["lam"], lw["wb"])
        if first:
            u = _norm_in_rest(hn_own, lw["win"], u_own, chip_idx, "norm_in_0_rest", after=token)
            hn = hn_own
            on_wout0 = travel(arrived(on_wout0, 0, "wout0", u), 1, "wout0", None)
            wout_b[0] = arrived(on_wout0, 1, "wout0", on_wout0[3])
            lw["wout"] = wout_b[0].reshape(2 * c, d)
            y, hs, h_next, hn_next = _mix_fwd(u, *mixer_w, f"mix_fwd_{l}", proj=(h, lw["wout"], layer_w[1]["g"]))
            saved.append((h, u, hn, y, hs))
            h = h_next
            on_win1 = travel(arrived(on_win1, 0, "win1", y), 1, "win1", None)
            win_b[1] = arrived(on_win1, 1, "win1", on_win1[3])
            on_wout1 = travel(arrived(on_wout1, 0, "wout1", y), 1, "wout1", on_win1[3])
            token = on_wout1[3]
        else:
            hn = hn_next
            u = _in_proj(hn, lw["win"], f"norm_in_{l}", after=token)
            wout_b[1] = arrived(on_wout1, 1, "wout1", u)
            lw["wout"] = wout_b[1].reshape(2 * c, d)
            y, hs = _mix_fwd(u, *mixer_w, f"mix_fwd_{l}")
            saved.append((h, u, hn, y, hs))
            dh, loss_lanes, d_final_g, dy = _out_proj_loss(h, y, lw["wout"], tgt, final_g.reshape(1, d), n_meta,
                                                           t_real, f"out_proj_{l}_loss")
    loss = lax.psum(loss_lanes[0, 0], ("x", "y", "c"))

    to_core = jnp.stack([my_chip, my_c])
    grads = [None] * depth
    early = None
    for l in reversed(range(depth)):
        lw = layer_w[l]
        h_in, u, hn, y, hs = saved[l]
        token = early[-1] if early else None
        d_wout = _out_proj_dw(y, dh, f"out_proj_dw_{l}", after=token)
        if early:
            ssem, rsem, bufs, _ = early
            bufs = _copies_wait(bufs, ssem, rsem, d_wout, _swap_plan, "early_swap_wait")
            half = len(bufs) // 2
            sums = [_pair_add(a, b, c_idx, f"early_pair_add_{k}") for k, (a, b) in enumerate(zip(bufs[:half], bufs[half:]))]
            lands = [lax.empty(p.shape, p.dtype) for p in sums]
            ssem, rsem, *bufs, token = _copies_start(sums + lands, _scatter_plan, 3 * half, "early_scatter_start")
        du, dsm, d_wr, d_wi = _mix_bwd(u, hs, dy, lw["wa"], lw["ba"], lw["wr"], lw["br"], lw["wi"], lw["bi"],
                                       lw["lam"], lw["wb"], f"mix_bwd_{l}", after=token)
        if early:
            bufs = _copies_wait(bufs, ssem, rsem, du, _scatter_plan, "early_scatter_wait")
            halves = [_chip_sum(rc, p, to_core, N_CORES, f"early_chip_sum_{k}")
                      for k, (p, rc) in enumerate(zip(bufs[:half], bufs[half:]))]
            ssem, rsem, *bufs, token = _copies_start(halves, _pair_gather_plan, half, "early_gather_start")
        d_win = _in_proj_dw(hn, du, s, f"in_proj_dw_{l}", after=token)
        srcs = [d_win.reshape(s, 2, d // 2, ns), d_wout.reshape(s, 2, dms // 2, d)]
        if early:
            early_full = _copies_wait(bufs, ssem, rsem, d_win, _pair_gather_plan, "early_gather_wait")
            lands = [lax.empty((a.shape[0],) + a.shape[2:], a.dtype) for a in srcs]
            ssem, rsem, *bufs, token = _copies_start(srcs + lands, _swap_plan, len(srcs), "late_swap_start")
            last = depth - 1
            early_grad = dict(w_in=early_full[0].reshape(d, ns), w_out=early_full[1].reshape(dms, d))
            early_step = {n: _adamw_layer(weights[n], early_grad[n], mom1[n], mom2[n], last, None,
                                          f"adamw_{n}_{last}", after=token) for n in ("w_in", "w_out")}
            bufs = _copies_wait(bufs, ssem, rsem, [o[0] for o in early_step.values()], _swap_plan, "late_swap_wait")
            late_sums = [_pair_add(a, b, c_idx, f"pair_add_{k}")
                         for k, (a, b) in enumerate(zip(bufs[:len(srcs)], bufs[len(srcs):]))]
            lands = [lax.empty(p.shape, p.dtype) for p in late_sums]
            ssem, rsem, *bufs, token = _copies_start(late_sums + lands, _scatter_plan, 3 * len(srcs), "late_scatter_start")
        if l > 0:
            dh, d_g, dy = _in_proj_bwd(du, lw["win"], h_in, lw["g"], dh, f"in_proj_bwd_{l}", after=token,
                                       w_below=layer_w[l - 1]["wout"])
        else:
            grad_x, d_meta, d_g = _in_proj_bwd(du, lw["win"], h_in, lw["g"], dh, f"in_proj_bwd_{l}", after=token,
                                               split=(n_meta, seq))
        if early:
            bufs = _copies_wait(bufs, ssem, rsem, grad_x, _scatter_plan, "late_scatter_wait")
            late_reduced = [_chip_sum(rc, p, to_core, N_CORES, f"chip_sum_{k}")
                            for k, (p, rc) in enumerate(zip(bufs[:len(srcs)], bufs[len(srcs):]))]
        grads[l] = dict(dsm=dsm, wr=_diag_blocks(d_wr, hd), wi=_diag_blocks(d_wi, hd), g=d_g)
        if l == depth - 1:
            lands = [lax.empty((a.shape[0],) + a.shape[2:], a.dtype) for a in srcs]
            ssem, rsem, *bufs, token = _copies_start(srcs + lands, _swap_plan, len(srcs), "early_swap_start")
            early = (ssem, rsem, bufs, token)
        else:
            early = None
    grad_x = grad_x[None]

    sharded = []
    sp = jnp.zeros((sm_rows, s, ds), F32)
    sp = sp.at[0:n_meta].set(d_meta.reshape(n_meta, s, ds))
    for l in range(depth):
        base = n_meta + l * SUBLANES
        dsm = grads[l]["dsm"]
        sp = sp.at[base:base + ka, :, 0:cs].set(dsm[ROW_DWA:ROW_DWA + ka].reshape(ka, s, cs))
        sp = sp.at[base + ka:base + ka + kb, :, 0:cs].set(dsm[ROW_DWB:ROW_DWB + kb].reshape(kb, s, cs))
    sharded.append(jnp.transpose(sp, (1, 0, 2)).reshape(s, 2, sm_rows // 2, ds))
    rep_parts = [jnp.concatenate([grads[l]["g"].reshape(-1) for l in range(depth)]), d_final_g.reshape(-1)]
    for row in (ROW_DBA, ROW_DBR, ROW_DBI, ROW_DLAM):
        rep_parts.append(jnp.concatenate([grads[l]["dsm"][row] for l in range(depth)]))
    rep_parts.append(jnp.concatenate([grads[l]["wr"].reshape(-1) for l in range(depth)]))
    rep_parts.append(jnp.concatenate([grads[l]["wi"].reshape(-1) for l in range(depth)]))
    rep_sizes = [p.shape[0] for p in rep_parts]
    piece = _round_up(-(-sum(rep_sizes) // (s * 2)), 2 * SUBLANES * LANES)
    flat = jnp.concatenate(rep_parts + [jnp.zeros((s * 2 * piece - sum(rep_sizes),), F32)])
    sharded.append(flat.reshape(s, 2, piece // LANES, LANES))

    from_sibling = _pair_swap(sharded, "small_pair_swap")
    pair_sums = [_pair_add(a, b, c_idx, f"small_pair_add_{k}") for k, (a, b) in enumerate(zip(sharded, from_sibling))]
    by_chip = _chip_scatter(pair_sums)
    to_device = jnp.stack([my_chip, 2 * my_chip + my_c])
    reduced_sp = _chip_sum(by_chip[0], pair_sums[0], to_core, N_CORES, "small_chip_sum")
    reduced_rep = _chip_sum(by_chip[1], pair_sums[1], to_device, N_CHIPS * N_CORES, "chip_sum_rep")
    sp_full, rep_all = _final_gather([reduced_sp], reduced_rep)
    ssem, rsem, *bufs, token = _copies_start(late_reduced, _pair_gather_plan, len(late_reduced), "late_gather_start",
                                             after=rep_all)
    g_sp = sp_full.reshape(sm_rows, ds)
    rep_flat = rep_all.reshape(-1)
    rep_out, off = [], 0
    for n in rep_sizes:
        rep_out.append(rep_flat[off:off + n])
        off += n
    grad = dict(
        meta=g_sp[0:n_meta],
        norm_g=rep_out[0].reshape(depth, d),
        conv_a_w=jnp.stack([g_sp[n_meta + l * SUBLANES:n_meta + l * SUBLANES + ka, 0:cs] for l in range(depth)]),
        conv_a_b=rep_out[2].reshape(depth, c),
        lru_wr=rep_out[6].reshape(depth, nh, hd, hd),
        lru_br=rep_out[3].reshape(depth, c),
        lru_wi=rep_out[7].reshape(depth, nh, hd, hd),
        lru_bi=rep_out[4].reshape(depth, c),
        lru_lambda=rep_out[5].reshape(depth, c),
        conv_b_w=jnp.stack([g_sp[n_meta + l * SUBLANES + ka:n_meta + l * SUBLANES + ka + kb, 0:cs]
                            for l in range(depth)]),
        final_g=rep_out[1].reshape(d),
    )

    delta, new_m, new_v = {}, {}, {}
    for n in grad:
        shape = weights[n].shape
        as_block = shape if len(shape) > 1 else (1,) + shape
        out = _adamw(weights[n].reshape(as_block), grad[n].reshape(as_block), mom1[n].reshape(as_block),
                     mom2[n].reshape(as_block), f"adamw_{n}", after=token)
        delta[n], new_m[n], new_v[n] = (o.reshape(shape) for o in out)
    full = _copies_wait(bufs, ssem, rsem, [delta[n] for n in grad], _pair_gather_plan, "late_gather_wait")
    g_win = [full[0].reshape(d, ns), early_full[0].reshape(d, ns)]
    g_wout = [full[1].reshape(dms, d), early_full[1].reshape(dms, d)]
    grad["w_in"] = jnp.stack(g_win)
    grad["w_out"] = jnp.stack(g_wout)
    for n, g_first in (("w_in", g_win[0]), ("w_out", g_wout[0])):
        delta[n], new_m[n], new_v[n] = _adamw_layer(weights[n], g_first, mom1[n], mom2[n], 0, early_step[n],
                                                    f"adamw_{n}_0")

    return (loss, grad_x, *[grad[n] for n in names], *[delta[n] for n in names],
            *[new_m[n] for n in names], *[new_v[n] for n in names])
```

```python
import jax
import jax.numpy as jnp
from jax import lax
from jax.experimental import pallas as pl
from jax.experimental.pallas import tpu as pltpu

F32 = jnp.float32
BF16 = jnp.bfloat16

RMS_EPS = 1e-6
LRU_C = 8.0
ADAM_LR = 0.001
ADAM_B1 = 0.9
ADAM_B2 = 0.999
ADAM_EPS = 1e-08
ADAM_WD = 0.01
ADAM_STEP = 10

N_CHIPS = 4
N_CORES = 2
VMEM_LIMIT_BYTES = 56 * 1024 * 1024
SUBLANES = 8
LANES = 128
ROW_QUANTUM = 384
MIX_CHUNK = 192
SCAN_UNROLL = 8
RING = 3
GATE_BLOCK = 256
MESH = pl.DeviceIdType.MESH
ANY = pl.BlockSpec(memory_space=pl.ANY)

NT_DIMS = (((1,), (1,)), ((), ()))
TN_DIMS = (((0,), (0,)), ((), ()))


def _params(sem):
    return pltpu.CompilerParams(dimension_semantics=sem, vmem_limit_bytes=VMEM_LIMIT_BYTES)


def _sig(x):
    return 0.5 * jnp.tanh(0.5 * x) + 0.5


def _row_tile(t):
    return 704 if t % 704 == 0 else 192


def _col_tile(n, prefs):
    for p in prefs:
        if n % p == 0:
            return p
    return n


def _slab_rows(rows, cols):
    if rows * cols * 4 <= 1024 * 1024:
        return rows
    return _col_tile(rows, (256, 128, 64, 32, 16))


def _norm_in_own(h, g, wg, me_idx, name):
    t, d = h.shape
    s, _, ns = wg.shape
    tm = 1408 if t % 1408 == 0 else _row_tile(t)
    tn = _col_tile(ns, (768, 384, 128))
    nb = ns // tn

    def body(m_ref, h_ref, g_ref, w_ref, u_ref, hn_ref):
        @pl.when(pl.program_id(1) == 0)
        def _():
            x = h_ref[...]
            r = lax.rsqrt(jnp.mean(x * x, axis=-1, keepdims=True) + RMS_EPS)
            hn_ref[...] = ((x * r) * g_ref[...]).astype(BF16)

        u_ref[...] = jnp.dot(hn_ref[...], w_ref[...], preferred_element_type=F32)

    return pl.pallas_call(
        body, name=name,
        grid_spec=pltpu.PrefetchScalarGridSpec(
            num_scalar_prefetch=1, grid=(t // tm, nb),
            in_specs=[pl.BlockSpec((tm, d), lambda i, n, m: (i, 0)),
                      pl.BlockSpec((1, d), lambda i, n, m: (0, 0)),
                      pl.BlockSpec((None, d, tn), lambda i, n, m: (m[0], 0, n))],
            out_specs=[pl.BlockSpec((tm, tn), lambda i, n, m: (i, m[0] * nb + n)),
                       pl.BlockSpec((tm, d), lambda i, n, m: (i, 0))]),
        out_shape=[jax.ShapeDtypeStruct((t, s * ns), F32), jax.ShapeDtypeStruct((t, d), BF16)],
        compiler_params=_params(("arbitrary", "arbitrary")),
    )(me_idx, h, g, wg)


def _norm_in_rest(hn, wg, u, me_idx, name, after=None):
    t, d = hn.shape
    s, _, ns = wg.shape
    tm = 1408 if t % 1408 == 0 else _row_tile(t)
    tn = _col_tile(ns, (1536, 768, 384, 128))
    nb = ns // tn

    def body(m_ref, hn_ref, w_ref, u_in, u_ref):
        del u_in
        u_ref[...] = jnp.dot(hn_ref[...], w_ref[...], preferred_element_type=F32)

    def shard(n, m):
        return (m[0] + 1 + n // nb) % s

    body, more_specs, more = _behind(body, 4, after)
    return pl.pallas_call(
        body, name=name,
        grid_spec=pltpu.PrefetchScalarGridSpec(
            num_scalar_prefetch=1, grid=(t // tm, (s - 1) * nb),
            in_specs=[pl.BlockSpec((tm, d), lambda i, n, m: (i, 0)),
                      pl.BlockSpec((None, d, tn), lambda i, n, m: (shard(n, m), 0, n % nb)),
                      ANY] + more_specs,
            out_specs=pl.BlockSpec((tm, tn), lambda i, n, m: (i, shard(n, m) * nb + n % nb))),
        out_shape=jax.ShapeDtypeStruct(u.shape, u.dtype),
        input_output_aliases={3: 0},
        compiler_params=_params(("arbitrary", "arbitrary")),
    )(me_idx, hn, wg, u, *more)


def _decay_consts(lam):
    z = -lam
    e = jnp.exp(-jnp.abs(z))
    u = 1.0 + e
    log1p_e = jnp.where(u == 1.0, e, jnp.log(u) * (e / (u - 1.0)))
    sp = jnp.maximum(z, 0.0) + log1p_e
    return -LRU_C * sp, LRU_C * _sig(z)


def _gates(xc, wr_ref, br_ref, wi_ref, bi_ref, c8, j, gb):
    sl = slice(j * gb, (j + 1) * gb)
    x16 = xc.astype(BF16)
    r = _sig(jnp.dot(x16, wr_ref[j], preferred_element_type=F32) + br_ref[:, sl])
    ig = _sig(jnp.dot(x16, wi_ref[j], preferred_element_type=F32) + bi_ref[:, sl])
    la = c8[:, sl] * r
    a = jnp.exp(la)
    sq = jnp.sqrt(-jnp.tanh(la) * (a * a + 1.0))
    return r, ig, a, sq


def _mix_fwd(u, wa, ba, wr, br, wi, bi, lam, wb, name, proj=None):
    t = u.shape[0]
    c = u.shape[1] // 6
    tc = MIX_CHUNK
    gb = wr.shape[1]
    nblk = c // gb
    ka, kb = wa.shape[0], wb.shape[0]
    n_proj = 0 if proj is None else 3

    def body(*refs):
        u_ref, wa_ref, ba_ref, wr_ref, br_ref, wi_ref, bi_ref, lam_ref, wb_ref = refs[:9]
        outs = refs[9 + n_proj:]
        y_ref, hs_ref = outs[:2]
        xa_ext, v_ext, xc_s, a_s, b_s, carry_s = outs[-6:]

        @pl.when(pl.program_id(0) == 0)
        def _():
            xa_ext[0:SUBLANES, :] = jnp.zeros((SUBLANES, c), F32)
            v_ext[0:SUBLANES, :] = jnp.zeros((SUBLANES, c), F32)
            carry_s[...] = jnp.zeros_like(carry_s)

        xa_ext[SUBLANES:SUBLANES + tc, :] = u_ref[:, 0:c]
        xc = ba_ref[...]
        for k in range(ka):
            xc = xc + wa_ref[pl.ds(k, 1), :] * xa_ext[pl.ds(SUBLANES - (ka - 1) + k, tc), :]
        xc_s[...] = xc
        c8, _ = _decay_consts(lam_ref[...])
        for j in range(nblk):
            sl = slice(j * gb, (j + 1) * gb)
            xcj = xc_s[:, sl]
            _, ig, a, sq = _gates(xcj, wr_ref, br_ref, wi_ref, bi_ref, c8, j, gb)
            a_s[:, sl] = a
            b_s[:, sl] = sq * (ig * xcj)

        row = lax.broadcasted_iota(jnp.int32, (SUBLANES, c), 0)

        def scan_step(j, _):
            off = pl.multiple_of(j * SUBLANES, SUBLANES)
            av = a_s[pl.ds(off, SUBLANES), :]
            bv = b_s[pl.ds(off, SUBLANES), :]
            for d in (1, 2, 4):
                keep = row >= d
                bsh = jnp.where(keep, pltpu.roll(bv, d, axis=0), 0.0)
                ash = jnp.where(keep, pltpu.roll(av, d, axis=0), 1.0)
                bv = av * bsh + bv
                av = av * ash
            hv = av * carry_s[...] + bv
            hs_ref[pl.ds(off, SUBLANES), :] = hv
            carry_s[...] = hs_ref[pl.ds(off + SUBLANES - 1, 1), :]
            return 0

        lax.fori_loop(0, tc // SUBLANES, scan_step, 0, unroll=SCAN_UNROLL)

        ga = u_ref[:, c:2 * c]
        y_ref[:, 0:c] = (hs_ref[...] * (ga * _sig(ga))).astype(BF16)

        v_ext[SUBLANES:SUBLANES + tc, :] = u_ref[:, 3 * c:4 * c] * u_ref[:, 4 * c:5 * c]
        cv = wb_ref[pl.ds(0, 1), :] * v_ext[pl.ds(SUBLANES - (kb - 1), tc), :]
        for k in range(1, kb):
            cv = cv + wb_ref[pl.ds(k, 1), :] * v_ext[pl.ds(SUBLANES - (kb - 1) + k, tc), :]
        gbv = u_ref[:, 5 * c:6 * c]
        y_ref[:, c:2 * c] = (u_ref[:, 2 * c:3 * c] * cv * (gbv * _sig(gbv))).astype(BF16)

        xa_ext[0:SUBLANES, :] = xa_ext[tc:tc + SUBLANES, :]
        v_ext[0:SUBLANES, :] = v_ext[tc:tc + SUBLANES, :]

        if proj is not None:
            h_ref, wout_ref, g_ref = refs[9:12]
            ho_ref, hn_ref = outs[2:4]
            x = h_ref[...] + jnp.dot(y_ref[...], wout_ref[...], preferred_element_type=F32)
            ho_ref[...] = x
            r = lax.rsqrt(jnp.mean(x * x, axis=-1, keepdims=True) + RMS_EPS)
            hn_ref[...] = ((x * r) * g_ref[...]).astype(BF16)

    full = lambda shape: pl.BlockSpec(shape, lambda i: (0,) * len(shape))
    rows = lambda width: pl.BlockSpec((tc, width), lambda i: (i, 0))
    more_in, more_specs, more_out_specs, more_out = [], [], [], []
    if proj is not None:
        h, wout, g_next = proj
        d = h.shape[1]
        more_in = [h, wout, g_next]
        more_specs = [rows(d), full(wout.shape), full(g_next.shape)]
        more_out_specs = [rows(d), rows(d)]
        more_out = [jax.ShapeDtypeStruct((t, d), F32), jax.ShapeDtypeStruct((t, d), BF16)]
    return pl.pallas_call(
        body, name=name, grid=(t // tc,),
        in_specs=[rows(6 * c), full(wa.shape), full(ba.shape), full(wr.shape), full(br.shape),
                  full(wi.shape), full(bi.shape), full(lam.shape), full(wb.shape)] + more_specs,
        out_specs=[rows(2 * c), rows(c)] + more_out_specs,
        out_shape=[jax.ShapeDtypeStruct((t, 2 * c), BF16), jax.ShapeDtypeStruct((t, c), F32)] + more_out,
        scratch_shapes=[pltpu.VMEM((tc + SUBLANES, c), F32), pltpu.VMEM((tc + SUBLANES, c), F32),
                        pltpu.VMEM((tc, c), F32), pltpu.VMEM((tc, c), F32), pltpu.VMEM((tc, c), F32),
                        pltpu.VMEM((1, c), F32)],
        compiler_params=_params(("arbitrary",)),
    )(u, wa, ba, wr, br, wi, bi, lam, wb, *more_in)


ROW_DWA = 0
ROW_DBA = 4
ROW_DBR = 5
ROW_DBI = 6
ROW_DLAM = 7
ROW_DWB = 8
SMALL_ROWS = 16


def _mix_bwd(u, hs, dy, wa, ba, wr, br, wi, bi, lam, wb, name, after=None):
    t = u.shape[0]
    c = u.shape[1] // 6
    tc = MIX_CHUNK
    nt = t // tc
    gb = wr.shape[1]
    nblk = c // gb
    ka, kb = wa.shape[0], wb.shape[0]
    assert ka <= ROW_DBA and kb <= SMALL_ROWS - ROW_DWB
    hb = tc // SUBLANES

    def body(u_ref, uh_ref, hs_ref, hsh_ref, dy_ref, wa_ref, ba_ref, wr_ref, br_ref, wi_ref, bi_ref, lam_ref, wb_ref,
             du_ref, dsm_ref, dwr_ref, dwi_ref,
             xa_ext, v_ext, hs_ext, a_ext, ds_ext, dxc_ext, dcv_ext, xc_s, r_s, i_s, sq_s, g_s, an_s):
        i = pl.program_id(0)
        chunk = nt - 1 - i
        tail = slice(tc, tc + SUBLANES)
        head = slice(0, SUBLANES)

        @pl.when(i == 0)
        def _():
            zero = jnp.zeros((SUBLANES, c), F32)
            a_ext[tail, :] = zero
            ds_ext[tail, :] = zero
            dxc_ext[tail, :] = zero
            dcv_ext[tail, :] = zero
            dsm_ref[...] = jnp.zeros_like(dsm_ref)
            dwr_ref[...] = jnp.zeros_like(dwr_ref)
            dwi_ref[...] = jnp.zeros_like(dwi_ref)

        prev = jnp.where(chunk > 0, 1.0, 0.0)
        xa_ext[head, :] = uh_ref[:, 0:c] * prev
        xa_ext[SUBLANES:SUBLANES + tc, :] = u_ref[:, 0:c]
        v_ext[head, :] = uh_ref[:, 3 * c:4 * c] * uh_ref[:, 4 * c:5 * c] * prev
        v_ext[SUBLANES:SUBLANES + tc, :] = u_ref[:, 3 * c:4 * c] * u_ref[:, 4 * c:5 * c]
        hs_ext[head, :] = hsh_ref[...] * prev
        hs_ext[SUBLANES:SUBLANES + tc, :] = hs_ref[...]

        xc = ba_ref[...]
        for k in range(ka):
            xc = xc + wa_ref[pl.ds(k, 1), :] * xa_ext[pl.ds(SUBLANES - (ka - 1) + k, tc), :]
        xc_s[...] = xc
        c8, dc8 = _decay_consts(lam_ref[...])
        for j in range(nblk):
            sl = slice(j * gb, (j + 1) * gb)
            r, ig, a, sq = _gates(xc_s[:, sl], wr_ref, br_ref, wi_ref, bi_ref, c8, j, gb)
            r_s[:, sl] = r
            i_s[:, sl] = ig
            sq_s[:, sl] = sq
            a_ext[0:tc, sl] = a

        ga = u_ref[:, c:2 * c]
        sga = _sig(ga)
        g_s[...] = dy_ref[:, 0:c] * (ga * sga)
        an_s[...] = a_ext[pl.ds(1, tc), :]

        row = lax.broadcasted_iota(jnp.int32, (SUBLANES, c), 0)

        def scan_step(j, _):
            off = pl.multiple_of(tc - SUBLANES - j * SUBLANES, SUBLANES)
            av = an_s[pl.ds(off, SUBLANES), :]
            bv = g_s[pl.ds(off, SUBLANES), :]
            for d in (1, 2, 4):
                keep = row < SUBLANES - d
                bsh = jnp.where(keep, pltpu.roll(bv, SUBLANES - d, axis=0), 0.0)
                ash = jnp.where(keep, pltpu.roll(av, SUBLANES - d, axis=0), 1.0)
                bv = av * bsh + bv
                av = av * ash
            ds_ext[pl.ds(off, SUBLANES), :] = av * ds_ext[pl.ds(off + SUBLANES, 1), :] + bv
            return 0

        lax.fori_loop(0, tc // SUBLANES, scan_step, 0, unroll=SCAN_UNROLL)

        def acc(row_index, val):
            dsm_ref[pl.ds(row_index, 1), :] += jnp.sum(val, axis=0, keepdims=True)

        def acc_block(row_index, sl, val):
            dsm_ref[pl.ds(row_index, 1), sl] += jnp.sum(val, axis=0, keepdims=True)

        for j in range(nblk):
            sl = slice(j * gb, (j + 1) * gb)
            ds = ds_ext[0:tc, sl]
            hprev = hs_ext[pl.ds(SUBLANES - 1, tc), sl]
            a = a_ext[0:tc, sl]
            sq = sq_s[:, sl]
            ig = i_s[:, sl]
            r = r_s[:, sl]
            xcj = xc_s[:, sl]
            t1 = ds * xcj
            dla = (ds * hprev) * a - (t1 * ig) * ((a * a) * lax.rsqrt(sq * sq))
            acc_block(ROW_DLAM, sl, dla * r)
            dpr = (dla * c8[:, sl]) * (r * (1.0 - r))
            dpi = (t1 * sq) * (ig * (1.0 - ig))
            acc_block(ROW_DBR, sl, dpr)
            acc_block(ROW_DBI, sl, dpi)
            p16 = dpr.astype(BF16)
            q16 = dpi.astype(BF16)
            x16 = xcj.astype(BF16)
            dwr_ref[j] += lax.dot_general(x16, p16, TN_DIMS, preferred_element_type=F32)
            dwi_ref[j] += lax.dot_general(x16, q16, TN_DIMS, preferred_element_type=F32)
            dxc = (ds * (sq * ig)
                   + lax.dot_general(p16, wr_ref[j], NT_DIMS, preferred_element_type=F32)
                   + lax.dot_general(q16, wi_ref[j], NT_DIMS, preferred_element_type=F32))
            dxc_ext[0:tc, sl] = dxc
            acc_block(ROW_DBA, sl, dxc)

        dsilu_a = sga * (1.0 + ga * (1.0 - sga))
        du_ref[:, c:2 * c] = (dy_ref[:, 0:c] * hs_ref[...] * dsilu_a).astype(BF16)

        dxc = dxc_ext[0:tc, :]
        dxa = wa_ref[pl.ds(ka - 1, 1), :] * dxc
        acc(ROW_DWA + ka - 1, dxc * xa_ext[SUBLANES:SUBLANES + tc, :])
        for k in range(ka - 1):
            acc(ROW_DWA + k, dxc * xa_ext[pl.ds(SUBLANES - (ka - 1) + k, tc), :])
            dxa = dxa + wa_ref[pl.ds(k, 1), :] * dxc_ext[pl.ds(ka - 1 - k, tc), :]
        du_ref[:, 0:c] = dxa.astype(BF16)

        cv = wb_ref[pl.ds(0, 1), :] * v_ext[pl.ds(SUBLANES - (kb - 1), tc), :]
        for k in range(1, kb):
            cv = cv + wb_ref[pl.ds(k, 1), :] * v_ext[pl.ds(SUBLANES - (kb - 1) + k, tc), :]
        gbv = u_ref[:, 5 * c:6 * c]
        sgb = _sig(gbv)
        silu_b = gbv * sgb
        dyb = dy_ref[:, c:2 * c]
        gB = u_ref[:, 2 * c:3 * c]
        du_ref[:, 2 * c:3 * c] = (dyb * cv * silu_b).astype(BF16)
        du_ref[:, 5 * c:6 * c] = (dyb * gB * cv * (sgb * (1.0 + gbv * (1.0 - sgb)))).astype(BF16)
        dcv = dyb * gB * silu_b
        dcv_ext[0:tc, :] = dcv
        dv = wb_ref[pl.ds(kb - 1, 1), :] * dcv
        acc(ROW_DWB + kb - 1, dcv * v_ext[SUBLANES:SUBLANES + tc, :])
        for k in range(kb - 1):
            acc(ROW_DWB + k, dcv * v_ext[pl.ds(SUBLANES - (kb - 1) + k, tc), :])
            dv = dv + wb_ref[pl.ds(k, 1), :] * dcv_ext[pl.ds(kb - 1 - k, tc), :]
        du_ref[:, 3 * c:4 * c] = (dv * u_ref[:, 4 * c:5 * c]).astype(BF16)
        du_ref[:, 4 * c:5 * c] = (dv * u_ref[:, 3 * c:4 * c]).astype(BF16)

        a_ext[tail, :] = a_ext[head, :]
        ds_ext[tail, :] = ds_ext[head, :]
        dxc_ext[tail, :] = dxc_ext[head, :]
        dcv_ext[tail, :] = dcv_ext[head, :]

        @pl.when(i == nt - 1)
        def _():
            dsm_ref[pl.ds(ROW_DLAM, 1), :] = dsm_ref[pl.ds(ROW_DLAM, 1), :] * dc8

    full = lambda shape: pl.BlockSpec(shape, lambda i: (0,) * len(shape))
    rev = lambda i: (nt - 1 - i, 0)
    halo = lambda i: (jnp.maximum((nt - 1 - i) * hb - 1, 0), 0)
    ext = pltpu.VMEM((tc + SUBLANES, c), F32)
    blk = pltpu.VMEM((tc, c), F32)
    body, more_specs, more = _behind(body, 13, after)
    return pl.pallas_call(
        body, name=name, grid=(nt,),
        in_specs=[pl.BlockSpec((tc, 6 * c), rev), pl.BlockSpec((SUBLANES, 6 * c), halo),
                  pl.BlockSpec((tc, c), rev), pl.BlockSpec((SUBLANES, c), halo),
                  pl.BlockSpec((tc, 2 * c), rev),
                  full(wa.shape), full(ba.shape), full(wr.shape), full(br.shape),
                  full(wi.shape), full(bi.shape), full(lam.shape), full(wb.shape)] + more_specs,
        out_specs=[pl.BlockSpec((tc, 6 * c), rev), full((SMALL_ROWS, c)), full(wr.shape), full(wi.shape)],
        out_shape=[jax.ShapeDtypeStruct((t, 6 * c), BF16), jax.ShapeDtypeStruct((SMALL_ROWS, c), F32),
                   jax.ShapeDtypeStruct(wr.shape, F32), jax.ShapeDtypeStruct(wi.shape, F32)],
        scratch_shapes=[ext] * 7 + [blk] * 6,
        compiler_params=_params(("arbitrary",)),
    )(u, u, hs, hs, dy, wa, ba, wr, br, wi, bi, lam, wb, *more)


def _behind(body, n_in, after):
    if after is None:
        return body, [], []
    return (lambda *refs: body(*refs[:n_in], *refs[n_in + 1:])), [ANY], [after]


def _in_proj(hn, wg, name, after=None):
    t, d = hn.shape
    s, _, ns = wg.shape
    tm = 1408 if t % 1408 == 0 else _row_tile(t)
    steps = (t // tm) * s

    def body(hn_ref, w_hbm, u_ref, ring, sems):
        q = pl.program_id(0) * s + pl.program_id(1)

        def fetch(step):
            slot = step % RING
            return pltpu.make_async_copy(w_hbm.at[step % s], ring.at[slot], sems.at[slot])

        @pl.when(q == 0)
        def _():
            for ahead in range(min(RING - 1, steps)):
                fetch(ahead).start()

        @pl.when(q + RING - 1 < steps)
        def _():
            fetch(q + RING - 1).start()

        fetch(q).wait()
        u_ref[...] = jnp.dot(hn_ref[...], ring[q % RING], preferred_element_type=F32)

    body, more_specs, more = _behind(body, 2, after)
    return pl.pallas_call(
        body, name=name, grid=(t // tm, s),
        in_specs=[pl.BlockSpec((tm, d), lambda i, n: (i, 0)), ANY] + more_specs,
        out_specs=pl.BlockSpec((tm, ns), lambda i, n: (i, n)),
        out_shape=jax.ShapeDtypeStruct((t, s * ns), F32),
        scratch_shapes=[pltpu.VMEM((RING, d, ns), BF16), pltpu.SemaphoreType.DMA((RING,))],
        compiler_params=_params(("arbitrary", "arbitrary")),
    )(hn, wg, *more)


def _out_proj_dw(y, dout, name, after=None):
    t, dm = y.shape
    d = dout.shape[1]
    tmm = _col_tile(dm, (1024, 512, 256))
    tn = _col_tile(d, (512, 256))

    def body(y_ref, g_ref, o_ref):
        o_ref[...] = lax.dot_general(y_ref[...], g_ref[...].astype(BF16), TN_DIMS, preferred_element_type=F32)

    body, more_specs, more = _behind(body, 2, after)
    return pl.pallas_call(
        body, name=name, grid=(d // tn, dm // tmm),
        in_specs=[pl.BlockSpec((t, tmm), lambda n, m: (0, m)),
                  pl.BlockSpec((t, tn), lambda n, m: (0, n))] + more_specs,
        out_specs=pl.BlockSpec((tmm, tn), lambda n, m: (m, n)),
        out_shape=jax.ShapeDtypeStruct((dm, d), F32),
        compiler_params=_params(("arbitrary", "arbitrary")),
    )(y, dout, *more)


def _in_proj_bwd(du, wg, h, g, dout, name, after=None, split=None, w_below=None):
    t, d = h.shape
    s, _, ns = wg.shape
    tm = 1408 if t % 1408 == 0 else _row_tile(t)
    tn = _col_tile(d, (512, 256))

    def mm_body(du_ref, w_ref, o_ref):
        total = lax.dot_general(du_ref[:, 0:ns], w_ref[0], NT_DIMS, preferred_element_type=F32)
        for a in range(1, s):
            total = total + lax.dot_general(du_ref[:, a * ns:(a + 1) * ns], w_ref[a], NT_DIMS,
                                            preferred_element_type=F32)
        o_ref[...] = total

    mm_body, more_specs, more = _behind(mm_body, 2, after)
    dhn = pl.pallas_call(
        mm_body, name=name, grid=(t // tm, d // tn),
        in_specs=[pl.BlockSpec((tm, s * ns), lambda i, n: (i, 0)),
                  pl.BlockSpec((s, tn, ns), lambda i, n: (0, n, 0))] + more_specs,
        out_specs=pl.BlockSpec((tm, tn), lambda i, n: (i, n)),
        out_shape=jax.ShapeDtypeStruct((t, d), F32),
        compiler_params=_params(("arbitrary", "arbitrary")),
    )(du, wg, *more)

    tr = 352 if t % 352 == 0 else 192
    nt = t // tr

    def row_grad(dhn_ref, h_ref, g_ref, dout_ref, dg_ref):
        @pl.when(pl.program_id(0) == 0)
        def _():
            dg_ref[...] = jnp.zeros_like(dg_ref)

        x = h_ref[...]
        dn = dhn_ref[...]
        r = lax.rsqrt(jnp.mean(x * x, axis=-1, keepdims=True) + RMS_EPS)
        gd = dn * g_ref[...]
        dot = jnp.mean(gd * x, axis=-1, keepdims=True)
        dg_ref[...] += jnp.sum(dn * (x * r), axis=0, keepdims=True)
        return dout_ref[...] + (r * gd - x * ((r * r * r) * dot))

    rows = pl.BlockSpec((tr, d), lambda i: (i, 0))
    one = pl.BlockSpec((1, d), lambda i: (0, 0))
    if split is None:
        dm = w_below.shape[0]

        def norm_body(dhn_ref, h_ref, g_ref, dout_ref, w_ref, dh_ref, dg_ref, dy_ref):
            dh = row_grad(dhn_ref, h_ref, g_ref, dout_ref, dg_ref)
            dh_ref[...] = dh
            dy_ref[...] = lax.dot_general(dh.astype(BF16), w_ref[...], NT_DIMS, preferred_element_type=F32)

        return pl.pallas_call(
            norm_body, name=name + "_norm", grid=(nt,),
            in_specs=[rows, rows, one, rows, pl.BlockSpec((dm, d), lambda i: (0, 0))],
            out_specs=[rows, one, pl.BlockSpec((tr, dm), lambda i: (i, 0))],
            out_shape=[jax.ShapeDtypeStruct((t, d), F32), jax.ShapeDtypeStruct((1, d), F32),
                       jax.ShapeDtypeStruct((t, dm), F32)],
            compiler_params=_params(("arbitrary",)),
        )(dhn, h, g, dout, w_below)

    n_head, n_body = split
    n_first = tr - n_head
    n_last = n_head + n_body - (nt - 1) * tr
    assert nt >= 2 and 0 < n_head < tr and 0 < n_last <= tr and n_head % SUBLANES == 0 and n_last % SUBLANES == 0

    def split_body(dhn_ref, h_ref, g_ref, dout_ref, body_ref, head_ref, dg_ref, stage, sems):
        i = pl.program_id(0)
        slot = i % 2

        def first_copy(sl):
            return pltpu.make_async_copy(stage.at[sl, pl.ds(n_head, n_first)], body_ref.at[pl.ds(0, n_first)], sems.at[sl])

        def middle_copy(sl, step):
            start = pl.multiple_of(step * tr - n_head, SUBLANES)
            return pltpu.make_async_copy(stage.at[sl], body_ref.at[pl.ds(start, tr)], sems.at[sl])

        def last_copy(sl):
            return pltpu.make_async_copy(stage.at[sl, pl.ds(0, n_last)],
                                         body_ref.at[pl.ds((nt - 1) * tr - n_head, n_last)], sems.at[sl])

        dh = row_grad(dhn_ref, h_ref, g_ref, dout_ref, dg_ref)

        @pl.when(i == 2)
        def _():
            first_copy(0).wait()

        @pl.when(i > 2)
        def _():
            middle_copy(slot, i - 2).wait()

        stage[slot] = dh

        @pl.when(i == 0)
        def _():
            head_ref[...] = stage[0, 0:n_head, :]
            first_copy(0).start()

        @pl.when((i > 0) & (i < nt - 1))
        def _():
            middle_copy(slot, i).start()

        @pl.when(i == nt - 1)
        def _():
            last = last_copy((nt - 1) % 2)
            last.start()
            if nt == 2:
                first_copy(0).wait()
            else:
                middle_copy((nt - 2) % 2, nt - 2).wait()
            last.wait()

    return pl.pallas_call(
        split_body, name=name + "_norm", grid=(nt,),
        in_specs=[rows, rows, one, rows],
        out_specs=[ANY, pl.BlockSpec((n_head, d), lambda i: (0, 0)), one],
        out_shape=[jax.ShapeDtypeStruct((n_body, d), F32), jax.ShapeDtypeStruct((n_head, d), F32),
                   jax.ShapeDtypeStruct((1, d), F32)],
        scratch_shapes=[pltpu.VMEM((2, tr, d), F32), pltpu.SemaphoreType.DMA((2,))],
        compiler_params=_params(("arbitrary",)),
    )(dhn, h, g, dout)


def _in_proj_dw(hn, du, s, name, after=None):
    t, d = hn.shape
    ns = du.shape[1] // s
    tmm = _col_tile(d, (1024, 512, 256))
    tn = _col_tile(ns, (768, 384, 128))
    nb = ns // tn

    def body(hn_ref, du_ref, o_ref):
        o_ref[...] = lax.dot_general(hn_ref[...], du_ref[...], TN_DIMS, preferred_element_type=F32)

    body, more_specs, more = _behind(body, 2, after)
    return pl.pallas_call(
        body, name=name, grid=(d // tmm, s * nb),
        in_specs=[pl.BlockSpec((t, tmm), lambda m, n: (0, m)),
                  pl.BlockSpec((t, tn), lambda m, n: (0, n))] + more_specs,
        out_specs=pl.BlockSpec((None, tmm, tn), lambda m, n: (n // nb, m, n % nb)),
        out_shape=jax.ShapeDtypeStruct((s, d, ns), F32),
        compiler_params=_params(("arbitrary", "arbitrary")),
    )(hn, du, *more)


def _out_proj_loss(h, y, w, tgt, g, n_meta, t_real, name):
    t, d = h.shape
    dm = y.shape[1]
    tm = 352 if t % 352 == 0 else 192

    def body(h_ref, y_ref, w_ref, t_ref, g_ref, dh_ref, loss_ref, dg_ref, dmix_ref):
        i = pl.program_id(0)

        @pl.when(i == 0)
        def _():
            loss_ref[...] = jnp.zeros_like(loss_ref)
            dg_ref[...] = jnp.zeros_like(dg_ref)

        x = h_ref[...] + jnp.dot(y_ref[...], w_ref[...], preferred_element_type=F32)
        gv = g_ref[...]
        r = lax.rsqrt(jnp.mean(x * x, axis=-1, keepdims=True) + RMS_EPS)
        xr = x * r
        rows = i * tm + lax.broadcasted_iota(jnp.int32, (tm, 1), 0)
        valid = (rows >= n_meta) & (rows < t_real)
        err = jnp.where(valid, xr * gv - t_ref[...], 0.0)
        loss_ref[...] += 0.5 * jnp.sum(jnp.mean(err * err, axis=-1, keepdims=True))
        dy = err * (1.0 / d)
        gd = dy * gv
        dot = jnp.mean(gd * x, axis=-1, keepdims=True)
        dh = r * gd - x * ((r * r * r) * dot)
        dh_ref[...] = dh
        dg_ref[...] += jnp.sum(dy * xr, axis=0, keepdims=True)
        dmix_ref[...] = lax.dot_general(dh.astype(BF16), w_ref[...], NT_DIMS, preferred_element_type=F32)

    rows = pl.BlockSpec((tm, d), lambda i: (i, 0))
    wide = pl.BlockSpec((tm, dm), lambda i: (i, 0))
    return pl.pallas_call(
        body, name=name, grid=(t // tm,),
        in_specs=[rows, wide, pl.BlockSpec((dm, d), lambda i: (0, 0)), rows, pl.BlockSpec((1, d), lambda i: (0, 0))],
        out_specs=[rows, pl.BlockSpec((1, LANES), lambda i: (0, 0)), pl.BlockSpec((1, d), lambda i: (0, 0)), wide],
        out_shape=[jax.ShapeDtypeStruct((t, d), F32), jax.ShapeDtypeStruct((1, LANES), F32),
                   jax.ShapeDtypeStruct((1, d), F32), jax.ShapeDtypeStruct((t, dm), F32)],
        compiler_params=_params(("arbitrary",)),
    )(h, y, w, tgt, g)


def _adamw_rows(rows, cols):
    for cand in (512, 256, 128, 64, 32, 16, 8):
        if rows % cand == 0 and cand * cols * 4 <= 2 * 1024 * 1024:
            return cand
    return rows


def _adamw_math(w_ref, g_ref, m_ref, v_ref, d_ref, nm_ref, nv_ref):
    gv = g_ref[...]
    m2 = ADAM_B1 * m_ref[...] + (1.0 - ADAM_B1) * gv
    v2 = ADAM_B2 * v_ref[...] + (1.0 - ADAM_B2) * (gv * gv)
    m_hat = m2 / (1.0 - ADAM_B1 ** ADAM_STEP)
    v_hat = v2 / (1.0 - ADAM_B2 ** ADAM_STEP)
    d_ref[...] = -ADAM_LR * (m_hat / (jnp.sqrt(v_hat) + ADAM_EPS) + ADAM_WD * w_ref[...])
    nm_ref[...] = m2
    nv_ref[...] = v2


def _adamw(w, g, m, v, name, after=None):
    shape = w.shape
    assert len(shape) >= 2 and w.size * 4 <= 2 * 1024 * 1024

    def body(*refs):
        _adamw_math(*refs)

    body, more_specs, more = _behind(body, 4, after)
    spec = pl.BlockSpec(shape, lambda i: (0,) * len(shape))
    return pl.pallas_call(
        body, name=name, grid=(1,),
        in_specs=[spec] * 4 + more_specs, out_specs=[spec] * 3,
        out_shape=[jax.ShapeDtypeStruct(shape, F32)] * 3,
        compiler_params=_params(("arbitrary",)),
    )(w, g, m, v, *more)


def _adamw_layer(w, g, m, v, layer, kept, name, after=None):
    nl, rows, cols = w.shape
    tr = _adamw_rows(rows, cols)
    n_kept = 0 if kept is None else 3

    def body(*refs):
        _adamw_math(*refs[:4], *refs[4 + n_kept:])

    body, more_specs, more = _behind(body, 4 + n_kept, after)
    lay = pl.BlockSpec((None, tr, cols), lambda i: (layer, i, 0))
    return pl.pallas_call(
        body, name=name, grid=(rows // tr,),
        in_specs=[lay, pl.BlockSpec((tr, cols), lambda i: (i, 0)), lay, lay] + [ANY] * n_kept + more_specs,
        out_specs=[lay] * 3,
        out_shape=[jax.ShapeDtypeStruct((nl, rows, cols), F32)] * 3,
        input_output_aliases={4 + k: k for k in range(n_kept)},
        compiler_params=_params(("arbitrary",)),
    )(w, g, m, v, *([] if kept is None else kept), *more)


def _pair_add(x, ra, c_idx, name):
    s, _, rows, cols = x.shape
    tr = _slab_rows(rows, cols)

    def body(c_ref, x_ref, r_ref, o_ref):
        o_ref[...] = (x_ref[...] + r_ref[...]).astype(BF16)

    return pl.pallas_call(
        body, name=name,
        grid_spec=pltpu.PrefetchScalarGridSpec(
            num_scalar_prefetch=1, grid=(s, rows // tr),
            in_specs=[pl.BlockSpec((None, None, tr, cols), lambda a, i, c_ref: (a, c_ref[0], i, 0)),
                      pl.BlockSpec((None, tr, cols), lambda a, i, c_ref: (a, i, 0))],
            out_specs=pl.BlockSpec((None, tr, cols), lambda a, i, c_ref: (a, i, 0))),
        out_shape=jax.ShapeDtypeStruct((s, rows, cols), BF16),
        compiler_params=_params(("arbitrary", "arbitrary")),
    )(c_idx, x, ra)


def _chip_sum(rc, p, where, n_slots, name):
    s, rows, cols = rc.shape
    tr = _slab_rows(rows, cols)

    def body(w_ref, x_ref, p_ref, o_ref):
        me = w_ref[0]
        total = jnp.where(me == 0, p_ref[...], x_ref[0]).astype(F32)
        for a in range(1, s):
            total = total + jnp.where(me == a, p_ref[...], x_ref[a]).astype(F32)
        o_ref[...] = total

    return pl.pallas_call(
        body, name=name,
        grid_spec=pltpu.PrefetchScalarGridSpec(
            num_scalar_prefetch=1, grid=(rows // tr,),
            in_specs=[pl.BlockSpec((s, tr, cols), lambda i, w_ref: (0, i, 0)),
                      pl.BlockSpec((None, tr, cols), lambda i, w_ref: (w_ref[0], i, 0))],
            out_specs=pl.BlockSpec((None, tr, cols), lambda i, w_ref: (w_ref[1], i, 0))),
        out_shape=jax.ShapeDtypeStruct((n_slots, rows, cols), F32),
        compiler_params=_params(("arbitrary",)),
    )(where, rc, p)


def _cast_place(w, layer, me_idx, name, after=None):
    _, rows, cols = w.shape
    tr = _slab_rows(rows, cols)

    def body(m_ref, w_ref, o_ref):
        o_ref[...] = w_ref[...].astype(BF16)

    body, more_specs, more = _behind(body, 2, after)
    return pl.pallas_call(
        body, name=name,
        grid_spec=pltpu.PrefetchScalarGridSpec(
            num_scalar_prefetch=1, grid=(rows // tr,),
            in_specs=[pl.BlockSpec((None, tr, cols), lambda i, m_ref: (layer, i, 0))] + more_specs,
            out_specs=pl.BlockSpec((None, tr, cols), lambda i, m_ref: (m_ref[0], i, 0))),
        out_shape=jax.ShapeDtypeStruct((N_CHIPS, rows, cols), BF16),
        compiler_params=_params(("arbitrary",)),
    )(me_idx, w, *more)


def _place():
    x, y, c = lax.axis_index("x"), lax.axis_index("y"), lax.axis_index("c")
    chips = [(1 - x, y), (x, 1 - y), (1 - x, 1 - y)]
    return x, y, c, chips


def _chip_index(cx, cy):
    return 2 * cx + cy


def _gather_copies(bufs, stage):
    x, y, c, chips = _place()
    me = _chip_index(x, y)
    copies = []
    for b in bufs:
        for chip in chips:
            src = _chip_index(*chip)
            if stage == 0:
                copies.append((b.at[me, c], (*chip, c), b.at[src, c]))
            else:
                copies.append((b.at[src, c], (x, y, 1 - c), b.at[src, 1 - c]))
    return copies


def _remote(ref, peer, ssem, rsem, k):
    return pltpu.make_async_remote_copy(src_ref=ref, dst_ref=ref, send_sem=ssem.at[k], recv_sem=rsem.at[k],
                                        device_id=peer, device_id_type=MESH)


def _gather_first(bufs, small):
    n = len(bufs)
    k = 3 * n

    def body(*refs):
        sm_ref = refs[n]
        b_refs, smg_ref = refs[n + 1:2 * n + 1], refs[2 * n + 1]
        lsem, ssem, rsem = refs[2 * n + 2:]
        x, y, c, chips = _place()
        me = _chip_index(x, y)
        local = pltpu.make_async_copy(sm_ref, smg_ref.at[me], lsem)
        local.start()
        first = _gather_copies(b_refs, 0)
        second = _gather_copies(b_refs, 1)
        started = []
        for i, (ref, peer, _) in enumerate(first):
            started.append(_remote(ref, peer, ssem, rsem, i))
        for j, chip in enumerate(chips):
            started.append(pltpu.make_async_remote_copy(
                src_ref=sm_ref, dst_ref=smg_ref.at[me], send_sem=ssem.at[2 * k + j], recv_sem=rsem.at[2 * k + j],
                device_id=(*chip, c), device_id_type=MESH))
        for cp in started:
            cp.start()
        for i, (_, peer, lands) in enumerate(first):
            _remote(lands, peer, ssem, rsem, i).wait_recv()
            ref, sib, _ = second[i]
            fwd = _remote(ref, sib, ssem, rsem, k + i)
            fwd.start()
            started.append(fwd)
        for i, (_, sib, lands) in enumerate(second):
            _remote(lands, sib, ssem, rsem, k + i).wait_recv()
        for j, chip in enumerate(chips):
            theirs = smg_ref.at[_chip_index(*chip)]
            pltpu.make_async_remote_copy(src_ref=theirs, dst_ref=theirs, send_sem=ssem.at[2 * k + j],
                                         recv_sem=rsem.at[2 * k + j], device_id=(*chip, c),
                                         device_id_type=MESH).wait_recv()
        for cp in started:
            cp.wait_send()
        local.wait()

    return pl.pallas_call(
        body, name="gather_first",
        in_specs=[ANY] * (n + 1), out_specs=[ANY] * (n + 1),
        out_shape=[jax.ShapeDtypeStruct(b.shape, b.dtype) for b in bufs]
        + [jax.ShapeDtypeStruct((N_CHIPS,) + small.shape, small.dtype)],
        input_output_aliases={i: i for i in range(n)},
        scratch_shapes=[pltpu.SemaphoreType.DMA, pltpu.SemaphoreType.DMA((2 * k + 3,)),
                        pltpu.SemaphoreType.DMA((2 * k + 3,))],
    )(*bufs, small)


HBM = pl.BlockSpec(memory_space=pltpu.HBM)
SEM = pl.BlockSpec(memory_space=pltpu.SEMAPHORE)
DATAFLOW = pltpu.SideEffectType.DATAFLOW_SIDE_EFFECTING


def _copies_start(bufs, plan, n_copies, name, after=None):
    n = len(bufs)
    extra = [] if after is None else [after]

    def body(*refs):
        refs = refs[:n] + refs[n + len(extra):]
        ssem, rsem = refs[n], refs[n + 1]
        b_refs, token = refs[n + 2:2 * n + 2], refs[2 * n + 2]
        copies = plan(b_refs)
        assert len(copies) == n_copies
        for i, (src, dst, peer, _) in enumerate(copies):
            pltpu.make_async_remote_copy(src_ref=src, dst_ref=dst, send_sem=ssem.at[i], recv_sem=rsem.at[i],
                                         device_id=peer, device_id_type=MESH).start()
        token[...] = jnp.zeros_like(token)

    return pl.pallas_call(
        body, name=name,
        out_shape=(pltpu.SemaphoreType.DMA((n_copies,)), pltpu.SemaphoreType.DMA((n_copies,)),
                   *[pltpu.HBM(b.shape, b.dtype) for b in bufs], jax.ShapeDtypeStruct((SUBLANES, LANES), F32)),
        in_specs=[HBM] * n + [ANY] * len(extra),
        out_specs=(SEM, SEM, *[HBM] * n, pl.BlockSpec(memory_space=pltpu.VMEM)),
        input_output_aliases={i: 2 + i for i in range(n)},
        compiler_params=pltpu.CompilerParams(has_side_effects=DATAFLOW),
    )(*[pltpu.with_memory_space_constraint(b, pltpu.HBM) for b in bufs], *extra)


def _copies_wait(bufs, ssem, rsem, after, plan, name):
    n = len(bufs)
    afters = list(after) if isinstance(after, (list, tuple)) else [after]

    def body(*refs):
        b_refs, ssem_ref, rsem_ref = refs[:n], refs[n], refs[n + 1]
        for i, (src, dst, peer, lands) in enumerate(plan(b_refs)):
            pltpu.make_async_remote_copy(src_ref=src, dst_ref=dst, send_sem=ssem_ref.at[i], recv_sem=rsem_ref.at[i],
                                         device_id=peer, device_id_type=MESH).wait_send()
            pltpu.make_async_remote_copy(src_ref=lands, dst_ref=lands, send_sem=ssem_ref.at[i],
                                         recv_sem=rsem_ref.at[i], device_id=peer, device_id_type=MESH).wait_recv()

    return pl.pallas_call(
        body, name=name,
        out_shape=tuple(pltpu.HBM(b.shape, b.dtype) for b in bufs),
        in_specs=[HBM] * n + [SEM, SEM] + [ANY] * len(afters), out_specs=tuple([HBM] * n),
        input_output_aliases={i: i for i in range(n)},
        compiler_params=pltpu.CompilerParams(has_side_effects=DATAFLOW),
    )(*bufs, ssem, rsem, *afters)


def _gather_plan(stage):
    return lambda refs: [(ref, ref, peer, lands) for ref, peer, lands in _gather_copies(refs, stage)]


def _swap_plan(refs):
    n = len(refs) // 2
    x, y, c, _ = _place()
    return [(refs[a].at[:, 1 - c], refs[n + a], (x, y, 1 - c), refs[n + a]) for a in range(n)]


def _scatter_plan(refs):
    n = len(refs) // 2
    x, y, c, chips = _place()
    me = _chip_index(x, y)
    return [(refs[a].at[_chip_index(*chip)], refs[n + a].at[me], (*chip, c), refs[n + a].at[_chip_index(*chip)])
            for a in range(n) for chip in chips]


def _pair_gather_plan(refs):
    x, y, c, _ = _place()
    return [(r.at[c], r.at[c], (x, y, 1 - c), r.at[1 - c]) for r in refs]


def _pair_swap(xs, name):
    n = len(xs)

    def body(*refs):
        x_refs, o_refs, ssem, rsem = refs[:n], refs[n:2 * n], refs[2 * n], refs[2 * n + 1]
        x, y, c, _ = _place()
        copies = [pltpu.make_async_remote_copy(src_ref=x_refs[a].at[:, 1 - c], dst_ref=o_refs[a],
                                               send_sem=ssem.at[a], recv_sem=rsem.at[a],
                                               device_id=(x, y, 1 - c), device_id_type=MESH) for a in range(n)]
        for cp in copies:
            cp.start()
        for cp in copies:
            cp.wait()

    return pl.pallas_call(
        body, name=name, in_specs=[ANY] * n, out_specs=[ANY] * n,
        out_shape=[jax.ShapeDtypeStruct((a.shape[0],) + a.shape[2:], a.dtype) for a in xs],
        scratch_shapes=[pltpu.SemaphoreType.DMA((n,)), pltpu.SemaphoreType.DMA((n,))],
    )(*xs)


def _chip_scatter(ps):
    n = len(ps)

    def body(*refs):
        p_refs, o_refs, ssem, rsem = refs[:n], refs[n:2 * n], refs[2 * n], refs[2 * n + 1]
        x, y, c, chips = _place()
        me = _chip_index(x, y)
        sends = []
        for a in range(n):
            for j, chip in enumerate(chips):
                sends.append(pltpu.make_async_remote_copy(
                    src_ref=p_refs[a].at[_chip_index(*chip)], dst_ref=o_refs[a].at[me],
                    send_sem=ssem.at[3 * a + j], recv_sem=rsem.at[3 * a + j],
                    device_id=(*chip, c), device_id_type=MESH))
        for cp in sends:
            cp.start()
        for a in range(n):
            for j, chip in enumerate(chips):
                src = _chip_index(*chip)
                pltpu.make_async_remote_copy(
                    src_ref=p_refs[a].at[src], dst_ref=o_refs[a].at[src],
                    send_sem=ssem.at[3 * a + j], recv_sem=rsem.at[3 * a + j],
                    device_id=(*chip, c), device_id_type=MESH).wait_recv()
        for cp in sends:
            cp.wait_send()

    return pl.pallas_call(
        body, name="chip_scatter", in_specs=[ANY] * n, out_specs=[ANY] * n,
        out_shape=[jax.ShapeDtypeStruct(a.shape, a.dtype) for a in ps],
        scratch_shapes=[pltpu.SemaphoreType.DMA((3 * n,)), pltpu.SemaphoreType.DMA((3 * n,))],
    )(*ps)


def _final_gather(fs, rep):
    n = len(fs)

    def body(*refs):
        o_refs, repo_ref = refs[n + 1:2 * n + 1], refs[2 * n + 1]
        ssem, rsem = refs[2 * n + 2:]
        x, y, c, chips = _place()
        slot = 4 * x + 2 * y + c
        copies = [pltpu.make_async_remote_copy(src_ref=o_refs[a].at[c], dst_ref=o_refs[a].at[c],
                                               send_sem=ssem.at[a], recv_sem=rsem.at[a],
                                               device_id=(x, y, 1 - c), device_id_type=MESH) for a in range(n)]
        peers = [(x, y, 1 - c)] + [(*chip, c) for chip in chips] + [(*chip, 1 - c) for chip in chips]
        for k, peer in enumerate(peers):
            copies.append(pltpu.make_async_remote_copy(src_ref=repo_ref.at[slot], dst_ref=repo_ref.at[slot],
                                                       send_sem=ssem.at[n + k], recv_sem=rsem.at[n + k],
                                                       device_id=peer, device_id_type=MESH))
        for cp in copies:
            cp.start()
        for a in range(n):
            pltpu.make_async_remote_copy(src_ref=o_refs[a].at[1 - c], dst_ref=o_refs[a].at[1 - c],
                                         send_sem=ssem.at[a], recv_sem=rsem.at[a],
                                         device_id=(x, y, 1 - c), device_id_type=MESH).wait_recv()
        for k, peer in enumerate(peers):
            px, py, pc = peer
            theirs = repo_ref.at[4 * px + 2 * py + pc]
            pltpu.make_async_remote_copy(src_ref=theirs, dst_ref=theirs, send_sem=ssem.at[n + k], recv_sem=rsem.at[n + k],
                                         device_id=peer, device_id_type=MESH).wait_recv()
        for cp in copies:
            cp.wait_send()

    return pl.pallas_call(
        body, name="final_gather", in_specs=[ANY] * (n + 1), out_specs=[ANY] * (n + 1),
        out_shape=[jax.ShapeDtypeStruct(a.shape, a.dtype) for a in fs] + [jax.ShapeDtypeStruct(rep.shape, rep.dtype)],
        input_output_aliases={k: k for k in range(n + 1)},
        scratch_shapes=[pltpu.SemaphoreType.DMA((n + 7,)), pltpu.SemaphoreType.DMA((n + 7,))],
    )(*fs, rep)


def _block_diag(w, gb):
    nh, hd, _ = w.shape
    per = gb // hd
    w4 = w.reshape(nh // per, per, hd, hd)
    eye = jnp.eye(per, dtype=w.dtype)
    return jnp.einsum("jaik,ab->jaibk", w4, eye).reshape(nh // per, gb, gb)


def _diag_blocks(dense, hd):
    nj, gb, _ = dense.shape
    per = gb // hd
    d5 = dense.reshape(nj, per, hd, per, hd)
    return jnp.stack([d5[:, a, :, a, :] for a in range(per)], axis=1).reshape(nj * per, hd, hd)


def _round_up(n, q):
    return (n + q - 1) // q * q


def kernel(x, meta, norm_g, w_in, conv_a_w, conv_a_b, lru_wr, lru_br, lru_wi, lru_bi, lru_lambda, conv_b_w, w_out, final_g, loss_target, m_meta, m_norm_g, m_w_in, m_conv_a_w, m_conv_a_b, m_lru_wr, m_lru_br, m_lru_wi, m_lru_bi, m_lru_lambda, m_conv_b_w, m_w_out, m_final_g, v_meta, v_norm_g, v_w_in, v_conv_a_w, v_conv_a_b, v_lru_wr, v_lru_br, v_lru_wi, v_lru_bi, v_lru_lambda, v_conv_b_w, v_w_out, v_final_g):
    weights = dict(meta=meta, norm_g=norm_g, w_in=w_in, conv_a_w=conv_a_w, conv_a_b=conv_a_b, lru_wr=lru_wr,
                   lru_br=lru_br, lru_wi=lru_wi, lru_bi=lru_bi, lru_lambda=lru_lambda, conv_b_w=conv_b_w,
                   w_out=w_out, final_g=final_g)
    mom1 = dict(meta=m_meta, norm_g=m_norm_g, w_in=m_w_in, conv_a_w=m_conv_a_w, conv_a_b=m_conv_a_b,
                lru_wr=m_lru_wr, lru_br=m_lru_br, lru_wi=m_lru_wi, lru_bi=m_lru_bi, lru_lambda=m_lru_lambda,
                conv_b_w=m_conv_b_w, w_out=m_w_out, final_g=m_final_g)
    mom2 = dict(meta=v_meta, norm_g=v_norm_g, w_in=v_w_in, conv_a_w=v_conv_a_w, conv_a_b=v_conv_a_b,
                lru_wr=v_lru_wr, lru_br=v_lru_br, lru_wi=v_lru_wi, lru_bi=v_lru_bi, lru_lambda=v_lru_lambda,
                conv_b_w=v_conv_b_w, w_out=v_w_out, final_g=v_final_g)
    names = list(weights)

    assert x.shape[0] == 1
    seq, d = x.shape[1], x.shape[2]
    n_meta, ds = meta.shape
    depth = norm_g.shape[0]
    c = lru_lambda.shape[1]
    nh, hd = lru_wr.shape[1], lru_wr.shape[2]
    ns = w_in.shape[2]
    dms = w_out.shape[1]
    cs = conv_a_w.shape[2]
    ka, kb = conv_a_w.shape[1], conv_b_w.shape[1]
    s = N_CHIPS
    assert depth == N_CORES and d == s * ds and c == s * cs and s * ns == 6 * c and s * dms == 2 * c
    gb = min(GATE_BLOCK, c)
    t_real = n_meta + seq
    t = _round_up(t_real, ROW_QUANTUM)
    my_c = lax.axis_index("c").astype(jnp.int32)
    my_chip = (2 * lax.axis_index("x") + lax.axis_index("y")).astype(jnp.int32)
    c_idx = my_c.reshape(1)
    chip_idx = my_chip.reshape(1)

    sm_rows = _round_up(n_meta + depth * SUBLANES, 2 * SUBLANES)
    small = jnp.zeros((sm_rows, ds), F32)
    small = small.at[0:n_meta, :].set(meta)
    for l in range(depth):
        base = n_meta + l * SUBLANES
        small = small.at[base:base + ka, 0:cs].set(conv_a_w[l])
        small = small.at[base + ka:base + ka + kb, 0:cs].set(conv_b_w[l])
    (small_g,) = _gather_first([], small)
    meta_full = jnp.transpose(small_g[:, 0:n_meta, :], (1, 0, 2)).reshape(n_meta, d)
    wa_full, wb_full = [], []
    for l in range(depth):
        base = n_meta + l * SUBLANES
        wa_full.append(jnp.transpose(small_g[:, base:base + ka, 0:cs], (1, 0, 2)).reshape(ka, c))
        wb_full.append(jnp.transpose(small_g[:, base + ka:base + ka + kb, 0:cs], (1, 0, 2)).reshape(kb, c))
    win0 = _cast_place(w_in, 0, chip_idx, "cast_w_in_0").reshape(s, 2, d // 2, ns)
    ssem_w, rsem_w, win0, token_w = _copies_start([win0], _gather_plan(0), 3, "gather_win0_ici_start", after=small_g)
    win_b = [None] + [_cast_place(w_in, l, chip_idx, f"cast_w_in_{l}", after=token_w).reshape(s, 2, d // 2, ns)
                      for l in range(1, depth)]
    wout_b = [_cast_place(w_out, l, chip_idx, f"cast_w_out_{l}", after=token_w).reshape(s, 2, dms // 2, d)
              for l in range(depth)]
    h = jnp.concatenate([meta_full, x[0], jnp.zeros((t - t_real, d), F32)], axis=0) + token_w[0, 0]
    tgt = jnp.concatenate([jnp.zeros((n_meta, d), F32), loss_target[0], jnp.zeros((t - t_real, d), F32)],
                          axis=0) + token_w[0, 0]
    u_own, hn_own = _norm_in_own(h, norm_g[0].reshape(1, d), win0.reshape(s, d, ns), chip_idx, "norm_in_0_own")
    (win0,) = _copies_wait([win0], ssem_w, rsem_w, [u_own, tgt] + win_b[1:] + wout_b, _gather_plan(0),
                           "gather_win0_ici_wait")
    ssem_w, rsem_w, win0, token_w = _copies_start([win0], _gather_plan(1), 3, "gather_win0_d2d_start")
    def travel(buf, stage, tag, after):
        return _copies_start([buf], _gather_plan(stage), 3, f"gather_{tag}_{'d2d' if stage else 'ici'}_start",
                             after=after)

    def arrived(state, stage, tag, after):
        (buf,) = _copies_wait([state[2]], state[0], state[1], after, _gather_plan(stage),
                              f"gather_{tag}_{'d2d' if stage else 'ici'}_wait")
        return buf

    on_wout0 = travel(wout_b[0], 0, "wout0", token_w)
    on_win1 = travel(win_b[1], 0, "win1", on_wout0[3])
    on_wout1 = travel(wout_b[1], 0, "wout1", on_win1[3])
    token = on_wout1[3]
    (win_b[0],) = _copies_wait([win0], ssem_w, rsem_w, token, _gather_plan(1), "gather_win0_d2d_wait")

    layer_w = []
    for l in range(depth):
        layer_w.append(dict(
            g=norm_g[l].reshape(1, d), wa=wa_full[l], ba=conv_a_b[l].reshape(1, c),
            wr=_block_diag(lru_wr[l], gb).astype(BF16), br=lru_br[l].reshape(1, c),
            wi=_block_diag(lru_wi[l], gb).astype(BF16), bi=lru_bi[l].reshape(1, c),
            lam=lru_lambda[l].reshape(1, c), wb=wb_full[l]))
    saved = []
    for l, lw in enumerate(layer_w):
        first = l == 0
        lw["win"] = win_b[l].reshape(s, d, ns)
        mixer_w = (lw["wa"], lw["ba"], lw["wr"], lw["br"], lw["wi"], lw["bi"], lw["lam"], lw["wb"])
        if first:
            u = _norm_in_rest(hn_own, lw["win"], u_own, chip_idx, "norm_in_0_rest", after=token)
            hn = hn_own
            on_wout0 = travel(arrived(on_wout0, 0, "wout0", u), 1, "wout0", None)
            wout_b[0] = arrived(on_wout0, 1, "wout0", on_wout0[3])
            lw["wout"] = wout_b[0].reshape(2 * c, d)
            y, hs, h_next, hn_next = _mix_fwd(u, *mixer_w, f"mix_fwd_{l}", proj=(h, lw["wout"], layer_w[1]["g"]))
            saved.append((h, u, hn, y, hs))
            h = h_next
            on_win1 = travel(arrived(on_win1, 0, "win1", y), 1, "win1", None)
            win_b[1] = arrived(on_win1, 1, "win1", on_win1[3])
            on_wout1 = travel(arrived(on_wout1, 0, "wout1", y), 1, "wout1", on_win1[3])
            token = on_wout1[3]
        else:
            hn = hn_next
            u = _in_proj(hn, lw["win"], f"norm_in_{l}", after=token)
            wout_b[1] = arrived(on_wout1, 1, "wout1", u)
            lw["wout"] = wout_b[1].reshape(2 * c, d)
            y, hs = _mix_fwd(u, *mixer_w, f"mix_fwd_{l}")
            saved.append((h, u, hn, y, hs))
            dh, loss_lanes, d_final_g, dy = _out_proj_loss(h, y, lw["wout"], tgt, final_g.reshape(1, d), n_meta,
                                                           t_real, f"out_proj_{l}_loss")
    loss = lax.psum(loss_lanes[0, 0], ("x", "y", "c"))

    to_core = jnp.stack([my_chip, my_c])
    grads = [None] * depth
    early = None
    for l in reversed(range(depth)):
        lw = layer_w[l]
        h_in, u, hn, y, hs = saved[l]
        token = early[-1] if early else None
        d_wout = _out_proj_dw(y, dh, f"out_proj_dw_{l}", after=token)
        if early:
            ssem, rsem, bufs, _ = early
            bufs = _copies_wait(bufs, ssem, rsem, d_wout, _swap_plan, "early_swap_wait")
            half = len(bufs) // 2
            sums = [_pair_add(a, b, c_idx, f"early_pair_add_{k}") for k, (a, b) in enumerate(zip(bufs[:half], bufs[half:]))]
            lands = [lax.empty(p.shape, p.dtype) for p in sums]
            ssem, rsem, *bufs, token = _copies_start(sums + lands, _scatter_plan, 3 * half, "early_scatter_start")
        du, dsm, d_wr, d_wi = _mix_bwd(u, hs, dy, lw["wa"], lw["ba"], lw["wr"], lw["br"], lw["wi"], lw["bi"],
                                       lw["lam"], lw["wb"], f"mix_bwd_{l}", after=token)
        if early:
            bufs = _copies_wait(bufs, ssem, rsem, du, _scatter_plan, "early_scatter_wait")
            halves = [_chip_sum(rc, p, to_core, N_CORES, f"early_chip_sum_{k}")
                      for k, (p, rc) in enumerate(zip(bufs[:half], bufs[half:]))]
            ssem, rsem, *bufs, token = _copies_start(halves, _pair_gather_plan, half, "early_gather_start")
        d_win = _in_proj_dw(hn, du, s, f"in_proj_dw_{l}", after=token)
        srcs = [d_win.reshape(s, 2, d // 2, ns), d_wout.reshape(s, 2, dms // 2, d)]
        if early:
            early_full = _copies_wait(bufs, ssem, rsem, d_win, _pair_gather_plan, "early_gather_wait")
            lands = [lax.empty((a.shape[0],) + a.shape[2:], a.dtype) for a in srcs]
            ssem, rsem, *bufs, token = _copies_start(srcs + lands, _swap_plan, len(srcs), "late_swap_start")
            last = depth - 1
            early_grad = dict(w_in=early_full[0].reshape(d, ns), w_out=early_full[1].reshape(dms, d))
            early_step = {n: _adamw_layer(weights[n], early_grad[n], mom1[n], mom2[n], last, None,
                                          f"adamw_{n}_{last}", after=token) for n in ("w_in", "w_out")}
            bufs = _copies_wait(bufs, ssem, rsem, [o[0] for o in early_step.values()], _swap_plan, "late_swap_wait")
            late_sums = [_pair_add(a, b, c_idx, f"pair_add_{k}")
                         for k, (a, b) in enumerate(zip(bufs[:len(srcs)], bufs[len(srcs):]))]
            lands = [lax.empty(p.shape, p.dtype) for p in late_sums]
            ssem, rsem, *bufs, token = _copies_start(late_sums + lands, _scatter_plan, 3 * len(srcs), "late_scatter_start")
        if l > 0:
            dh, d_g, dy = _in_proj_bwd(du, lw["win"], h_in, lw["g"], dh, f"in_proj_bwd_{l}", after=token,
                                       w_below=layer_w[l - 1]["wout"])
        else:
            grad_x, d_meta, d_g = _in_proj_bwd(du, lw["win"], h_in, lw["g"], dh, f"in_proj_bwd_{l}", after=token,
                                               split=(n_meta, seq))
        if early:
            bufs = _copies_wait(bufs, ssem, rsem, grad_x, _scatter_plan, "late_scatter_wait")
            late_reduced = [_chip_sum(rc, p, to_core, N_CORES, f"chip_sum_{k}")
                            for k, (p, rc) in enumerate(zip(bufs[:len(srcs)], bufs[len(srcs):]))]
        grads[l] = dict(dsm=dsm, wr=_diag_blocks(d_wr, hd), wi=_diag_blocks(d_wi, hd), g=d_g)
        if l == depth - 1:
            lands = [lax.empty((a.shape[0],) + a.shape[2:], a.dtype) for a in srcs]
            ssem, rsem, *bufs, token = _copies_start(srcs + lands, _swap_plan, len(srcs), "early_swap_start")
            early = (ssem, rsem, bufs, token)
        else:
            early = None
    grad_x = grad_x[None]

    sharded = []
    sp = jnp.zeros((sm_rows, s, ds), F32)
    sp = sp.at[0:n_meta].set(d_meta.reshape(n_meta, s, ds))
    for l in range(depth):
        base = n_meta + l * SUBLANES
        dsm = grads[l]["dsm"]
        sp = sp.at[base:base + ka, :, 0:cs].set(dsm[ROW_DWA:ROW_DWA + ka].reshape(ka, s, cs))
        sp = sp.at[base + ka:base + ka + kb, :, 0:cs].set(dsm[ROW_DWB:ROW_DWB + kb].reshape(kb, s, cs))
    sharded.append(jnp.transpose(sp, (1, 0, 2)).reshape(s, 2, sm_rows // 2, ds))
    rep_parts = [jnp.concatenate([grads[l]["g"].reshape(-1) for l in range(depth)]), d_final_g.reshape(-1)]
    for row in (ROW_DBA, ROW_DBR, ROW_DBI, ROW_DLAM):
        rep_parts.append(jnp.concatenate([grads[l]["dsm"][row] for l in range(depth)]))
    rep_parts.append(jnp.concatenate([grads[l]["wr"].reshape(-1) for l in range(depth)]))
    rep_parts.append(jnp.concatenate([grads[l]["wi"].reshape(-1) for l in range(depth)]))
    rep_sizes = [p.shape[0] for p in rep_parts]
    piece = _round_up(-(-sum(rep_sizes) // (s * 2)), 2 * SUBLANES * LANES)
    flat = jnp.concatenate(rep_parts + [jnp.zeros((s * 2 * piece - sum(rep_sizes),), F32)])
    sharded.append(flat.reshape(s, 2, piece // LANES, LANES))

    from_sibling = _pair_swap(sharded, "small_pair_swap")
    pair_sums = [_pair_add(a, b, c_idx, f"small_pair_add_{k}") for k, (a, b) in enumerate(zip(sharded, from_sibling))]
    by_chip = _chip_scatter(pair_sums)
    to_device = jnp.stack([my_chip, 2 * my_chip + my_c])
    reduced_sp = _chip_sum(by_chip[0], pair_sums[0], to_core, N_CORES, "small_chip_sum")
    reduced_rep = _chip_sum(by_chip[1], pair_sums[1], to_device, N_CHIPS * N_CORES, "chip_sum_rep")
    sp_full, rep_all = _final_gather([reduced_sp], reduced_rep)
    ssem, rsem, *bufs, token = _copies_start(late_reduced, _pair_gather_plan, len(late_reduced), "late_gather_start",
                                             after=rep_all)
    g_sp = sp_full.reshape(sm_rows, ds)
    rep_flat = rep_all.reshape(-1)
    rep_out, off = [], 0
    for n in rep_sizes:
        rep_out.append(rep_flat[off:off + n])
        off += n
    grad = dict(
        meta=g_sp[0:n_meta],
        norm_g=rep_out[0].reshape(depth, d),
        conv_a_w=jnp.stack([g_sp[n_meta + l * SUBLANES:n_meta + l * SUBLANES + ka, 0:cs] for l in range(depth)]),
        conv_a_b=rep_out[2].reshape(depth, c),
        lru_wr=rep_out[6].reshape(depth, nh, hd, hd),
        lru_br=rep_out[3].reshape(depth, c),
        lru_wi=rep_out[7].reshape(depth, nh, hd, hd),
        lru_bi=rep_out[4].reshape(depth, c),
        lru_lambda=rep_out[5].reshape(depth, c),
        conv_b_w=jnp.stack([g_sp[n_meta + l * SUBLANES + ka:n_meta + l * SUBLANES + ka + kb, 0:cs]
                            for l in range(depth)]),
        final_g=rep_out[1].reshape(d),
    )

    delta, new_m, new_v = {}, {}, {}
    for n in grad:
        shape = weights[n].shape
        as_block = shape if len(shape) > 1 else (1,) + shape
        out = _adamw(weights[n].reshape(as_block), grad[n].reshape(as_block), mom1[n].reshape(as_block),
                     mom2[n].reshape(as_block), f"adamw_{n}", after=token)
        delta[n], new_m[n], new_v[n] = (o.reshape(shape) for o in out)
    full = _copies_wait(bufs, ssem, rsem, [delta[n] for n in grad], _pair_gather_plan, "late_gather_wait")
    g_win = [full[0].reshape(d, ns), early_full[0].reshape(d, ns)]
    g_wout = [full[1].reshape(dms, d), early_full[1].reshape(dms, d)]
    grad["w_in"] = jnp.stack(g_win)
    grad["w_out"] = jnp.stack(g_wout)
    for n, g_first in (("w_in", g_win[0]), ("w_out", g_wout[0])):
        delta[n], new_m[n], new_v[n] = _adamw_layer(weights[n], g_first, mom1[n], mom2[n], 0, early_step[n],
                                                    f"adamw_{n}_0")

    return (loss, grad_x, *[grad[n] for n in names], *[delta[n] for n in names],
            *[new_m[n] for n in names], *[new_v[n] for n in names])
```

```python
import jax
import jax.numpy as jnp
from jax import lax
from jax.experimental import pallas as pl
from jax.experimental.pallas import tpu as pltpu

F32 = jnp.float32
BF16 = jnp.bfloat16

RMS_EPS = 1e-6
LRU_C = 8.0
ADAM_LR = 0.001
ADAM_B1 = 0.9
ADAM_B2 = 0.999
ADAM_EPS = 1e-08
ADAM_WD = 0.01
ADAM_STEP = 10

N_CHIPS = 4
N_CORES = 2
VMEM_LIMIT_BYTES = 56 * 1024 * 1024
SUBLANES = 8
LANES = 128
ROW_QUANTUM = 384
MIX_CHUNK = 192
SCAN_UNROLL = 8
GATE_BLOCK = 256
MESH = pl.DeviceIdType.MESH
ANY = pl.BlockSpec(memory_space=pl.ANY)

NT_DIMS = (((1,), (1,)), ((), ()))
TN_DIMS = (((0,), (0,)), ((), ()))


def _params(sem):
    return pltpu.CompilerParams(dimension_semantics=sem, vmem_limit_bytes=VMEM_LIMIT_BYTES)


def _sig(x):
    return 0.5 * jnp.tanh(0.5 * x) + 0.5


def _row_tile(t):
    return 704 if t % 704 == 0 else 192


def _col_tile(n, prefs):
    for p in prefs:
        if n % p == 0:
            return p
    return n


def _slab_rows(rows, cols):
    if rows * cols * 4 <= 1024 * 1024:
        return rows
    return _col_tile(rows, (256, 128, 64, 32, 16))


def _norm_in_own(h, g, wg, me_idx, name):
    t, d = h.shape
    s, _, ns = wg.shape
    tm = 1408 if t % 1408 == 0 else _row_tile(t)
    tn = _col_tile(ns, (768, 384, 128))
    nb = ns // tn

    def body(m_ref, h_ref, g_ref, w_ref, u_ref, hn_ref):
        @pl.when(pl.program_id(1) == 0)
        def _():
            x = h_ref[...]
            r = lax.rsqrt(jnp.mean(x * x, axis=-1, keepdims=True) + RMS_EPS)
            hn_ref[...] = ((x * r) * g_ref[...]).astype(BF16)

        u_ref[...] = jnp.dot(hn_ref[...], w_ref[...], preferred_element_type=F32)

    return pl.pallas_call(
        body, name=name,
        grid_spec=pltpu.PrefetchScalarGridSpec(
            num_scalar_prefetch=1, grid=(t // tm, nb),
            in_specs=[pl.BlockSpec((tm, d), lambda i, n, m: (i, 0)),
                      pl.BlockSpec((1, d), lambda i, n, m: (0, 0)),
                      pl.BlockSpec((None, d, tn), lambda i, n, m: (m[0], 0, n))],
            out_specs=[pl.BlockSpec((tm, tn), lambda i, n, m: (i, m[0] * nb + n)),
                       pl.BlockSpec((tm, d), lambda i, n, m: (i, 0))]),
        out_shape=[jax.ShapeDtypeStruct((t, s * ns), F32), jax.ShapeDtypeStruct((t, d), BF16)],
        compiler_params=_params(("arbitrary", "arbitrary")),
    )(me_idx, h, g, wg)


def _norm_in_rest(hn, wg, u, me_idx, name, after=None):
    t, d = hn.shape
    s, _, ns = wg.shape
    tm = 1408 if t % 1408 == 0 else _row_tile(t)
    tn = _col_tile(ns, (1536, 768, 384, 128))
    nb = ns // tn

    def body(m_ref, hn_ref, w_ref, u_in, u_ref):
        del u_in
        u_ref[...] = jnp.dot(hn_ref[...], w_ref[...], preferred_element_type=F32)

    def shard(n, m):
        return (m[0] + 1 + n // nb) % s

    body, more_specs, more = _behind(body, 4, after)
    return pl.pallas_call(
        body, name=name,
        grid_spec=pltpu.PrefetchScalarGridSpec(
            num_scalar_prefetch=1, grid=(t // tm, (s - 1) * nb),
            in_specs=[pl.BlockSpec((tm, d), lambda i, n, m: (i, 0)),
                      pl.BlockSpec((None, d, tn), lambda i, n, m: (shard(n, m), 0, n % nb)),
                      ANY] + more_specs,
            out_specs=pl.BlockSpec((tm, tn), lambda i, n, m: (i, shard(n, m) * nb + n % nb))),
        out_shape=jax.ShapeDtypeStruct(u.shape, u.dtype),
        input_output_aliases={3: 0},
        compiler_params=_params(("arbitrary", "arbitrary")),
    )(me_idx, hn, wg, u, *more)


def _decay_consts(lam):
    z = -lam
    e = jnp.exp(-jnp.abs(z))
    u = 1.0 + e
    log1p_e = jnp.where(u == 1.0, e, jnp.log(u) * (e / (u - 1.0)))
    sp = jnp.maximum(z, 0.0) + log1p_e
    return -LRU_C * sp, LRU_C * _sig(z)


def _gates(xc, wr_ref, br_ref, wi_ref, bi_ref, c8, j, gb):
    sl = slice(j * gb, (j + 1) * gb)
    x16 = xc.astype(BF16)
    r = _sig(jnp.dot(x16, wr_ref[j], preferred_element_type=F32) + br_ref[:, sl])
    ig = _sig(jnp.dot(x16, wi_ref[j], preferred_element_type=F32) + bi_ref[:, sl])
    la = c8[:, sl] * r
    a = jnp.exp(la)
    sq = jnp.sqrt(-jnp.tanh(la) * (a * a + 1.0))
    return r, ig, a, sq


def _mix_fwd(u, wa, ba, wr, br, wi, bi, lam, wb, name, proj=None):
    t = u.shape[0]
    c = u.shape[1] // 6
    tc = MIX_CHUNK
    gb = wr.shape[1]
    nblk = c // gb
    ka, kb = wa.shape[0], wb.shape[0]
    n_proj = 0 if proj is None else 3

    def body(*refs):
        u_ref, wa_ref, ba_ref, wr_ref, br_ref, wi_ref, bi_ref, lam_ref, wb_ref = refs[:9]
        outs = refs[9 + n_proj:]
        y_ref, hs_ref = outs[:2]
        xa_ext, v_ext, xc_s, a_s, b_s, carry_s = outs[-6:]

        @pl.when(pl.program_id(0) == 0)
        def _():
            xa_ext[0:SUBLANES, :] = jnp.zeros((SUBLANES, c), F32)
            v_ext[0:SUBLANES, :] = jnp.zeros((SUBLANES, c), F32)
            carry_s[...] = jnp.zeros_like(carry_s)

        xa_ext[SUBLANES:SUBLANES + tc, :] = u_ref[:, 0:c]
        xc = ba_ref[...]
        for k in range(ka):
            xc = xc + wa_ref[pl.ds(k, 1), :] * xa_ext[pl.ds(SUBLANES - (ka - 1) + k, tc), :]
        xc_s[...] = xc
        c8, _ = _decay_consts(lam_ref[...])
        for j in range(nblk):
            sl = slice(j * gb, (j + 1) * gb)
            xcj = xc_s[:, sl]
            _, ig, a, sq = _gates(xcj, wr_ref, br_ref, wi_ref, bi_ref, c8, j, gb)
            a_s[:, sl] = a
            b_s[:, sl] = sq * (ig * xcj)

        row = lax.broadcasted_iota(jnp.int32, (SUBLANES, c), 0)

        def scan_step(j, _):
            off = pl.multiple_of(j * SUBLANES, SUBLANES)
            av = a_s[pl.ds(off, SUBLANES), :]
            bv = b_s[pl.ds(off, SUBLANES), :]
            for d in (1, 2, 4):
                keep = row >= d
                bsh = jnp.where(keep, pltpu.roll(bv, d, axis=0), 0.0)
                ash = jnp.where(keep, pltpu.roll(av, d, axis=0), 1.0)
                bv = av * bsh + bv
                av = av * ash
            hv = av * carry_s[...] + bv
            hs_ref[pl.ds(off, SUBLANES), :] = hv
            carry_s[...] = hs_ref[pl.ds(off + SUBLANES - 1, 1), :]
            return 0

        lax.fori_loop(0, tc // SUBLANES, scan_step, 0, unroll=SCAN_UNROLL)

        for j in range(nblk):
            sl = slice(j * gb, (j + 1) * gb)
            lanes = lambda group: slice(group * c + j * gb, group * c + (j + 1) * gb)
            ga = u_ref[:, lanes(1)]
            y_ref[:, sl] = (hs_ref[:, sl] * (ga * _sig(ga))).astype(BF16)

            v_ext[SUBLANES:SUBLANES + tc, sl] = u_ref[:, lanes(3)] * u_ref[:, lanes(4)]
            cv = wb_ref[pl.ds(0, 1), sl] * v_ext[pl.ds(SUBLANES - (kb - 1), tc), sl]
            for k in range(1, kb):
                cv = cv + wb_ref[pl.ds(k, 1), sl] * v_ext[pl.ds(SUBLANES - (kb - 1) + k, tc), sl]
            gbv = u_ref[:, lanes(5)]
            y_ref[:, lanes(1)] = (u_ref[:, lanes(2)] * cv * (gbv * _sig(gbv))).astype(BF16)

        xa_ext[0:SUBLANES, :] = xa_ext[tc:tc + SUBLANES, :]
        v_ext[0:SUBLANES, :] = v_ext[tc:tc + SUBLANES, :]

        if proj is not None:
            h_ref, wout_ref, g_ref = refs[9:12]
            ho_ref, hn_ref = outs[2:4]
            x = h_ref[...] + jnp.dot(y_ref[...], wout_ref[...], preferred_element_type=F32)
            ho_ref[...] = x
            r = lax.rsqrt(jnp.mean(x * x, axis=-1, keepdims=True) + RMS_EPS)
            hn_ref[...] = ((x * r) * g_ref[...]).astype(BF16)

    full = lambda shape: pl.BlockSpec(shape, lambda i: (0,) * len(shape))
    rows = lambda width: pl.BlockSpec((tc, width), lambda i: (i, 0))
    more_in, more_specs, more_out_specs, more_out = [], [], [], []
    if proj is not None:
        h, wout, g_next = proj
        d = h.shape[1]
        more_in = [h, wout, g_next]
        more_specs = [rows(d), full(wout.shape), full(g_next.shape)]
        more_out_specs = [rows(d), rows(d)]
        more_out = [jax.ShapeDtypeStruct((t, d), F32), jax.ShapeDtypeStruct((t, d), BF16)]
    return pl.pallas_call(
        body, name=name, grid=(t // tc,),
        in_specs=[rows(6 * c), full(wa.shape), full(ba.shape), full(wr.shape), full(br.shape),
                  full(wi.shape), full(bi.shape), full(lam.shape), full(wb.shape)] + more_specs,
        out_specs=[rows(2 * c), rows(c)] + more_out_specs,
        out_shape=[jax.ShapeDtypeStruct((t, 2 * c), BF16), jax.ShapeDtypeStruct((t, c), F32)] + more_out,
        scratch_shapes=[pltpu.VMEM((tc + SUBLANES, c), F32), pltpu.VMEM((tc + SUBLANES, c), F32),
                        pltpu.VMEM((tc, c), F32), pltpu.VMEM((tc, c), F32), pltpu.VMEM((tc, c), F32),
                        pltpu.VMEM((1, c), F32)],
        compiler_params=_params(("arbitrary",)),
    )(u, wa, ba, wr, br, wi, bi, lam, wb, *more_in)


ROW_DWA = 0
ROW_DBA = 4
ROW_DBR = 5
ROW_DBI = 6
ROW_DLAM = 7
ROW_DWB = 8
SMALL_ROWS = 16


def _mix_bwd(u, hs, dy, wa, ba, wr, br, wi, bi, lam, wb, name, after=None):
    t = u.shape[0]
    c = u.shape[1] // 6
    tc = MIX_CHUNK
    nt = t // tc
    gb = wr.shape[1]
    nblk = c // gb
    ka, kb = wa.shape[0], wb.shape[0]
    assert ka <= ROW_DBA and kb <= SMALL_ROWS - ROW_DWB
    hb = tc // SUBLANES

    def body(u_ref, uh_ref, hs_ref, hsh_ref, dy_ref, wa_ref, ba_ref, wr_ref, br_ref, wi_ref, bi_ref, lam_ref, wb_ref,
             du_ref, dsm_ref, dwr_ref, dwi_ref,
             xa_ext, v_ext, hs_ext, a_ext, ds_ext, dxc_ext, dcv_ext, xc_s, r_s, i_s, sq_s, g_s, an_s):
        i = pl.program_id(0)
        chunk = nt - 1 - i
        tail = slice(tc, tc + SUBLANES)
        head = slice(0, SUBLANES)

        @pl.when(i == 0)
        def _():
            zero = jnp.zeros((SUBLANES, c), F32)
            a_ext[tail, :] = zero
            ds_ext[tail, :] = zero
            dxc_ext[tail, :] = zero
            dcv_ext[tail, :] = zero
            dsm_ref[...] = jnp.zeros_like(dsm_ref)
            dwr_ref[...] = jnp.zeros_like(dwr_ref)
            dwi_ref[...] = jnp.zeros_like(dwi_ref)

        prev = jnp.where(chunk > 0, 1.0, 0.0)
        xa_ext[head, :] = uh_ref[:, 0:c] * prev
        xa_ext[SUBLANES:SUBLANES + tc, :] = u_ref[:, 0:c]
        v_ext[head, :] = uh_ref[:, 3 * c:4 * c] * uh_ref[:, 4 * c:5 * c] * prev
        v_ext[SUBLANES:SUBLANES + tc, :] = u_ref[:, 3 * c:4 * c] * u_ref[:, 4 * c:5 * c]
        hs_ext[head, :] = hsh_ref[...] * prev
        hs_ext[SUBLANES:SUBLANES + tc, :] = hs_ref[...]

        xc = ba_ref[...]
        for k in range(ka):
            xc = xc + wa_ref[pl.ds(k, 1), :] * xa_ext[pl.ds(SUBLANES - (ka - 1) + k, tc), :]
        xc_s[...] = xc
        c8, dc8 = _decay_consts(lam_ref[...])
        for j in range(nblk):
            sl = slice(j * gb, (j + 1) * gb)
            r, ig, a, sq = _gates(xc_s[:, sl], wr_ref, br_ref, wi_ref, bi_ref, c8, j, gb)
            r_s[:, sl] = r
            i_s[:, sl] = ig
            sq_s[:, sl] = sq
            a_ext[0:tc, sl] = a

        ga = u_ref[:, c:2 * c]
        sga = _sig(ga)
        g_s[...] = dy_ref[:, 0:c] * (ga * sga)
        an_s[...] = a_ext[pl.ds(1, tc), :]

        row = lax.broadcasted_iota(jnp.int32, (SUBLANES, c), 0)

        def scan_step(j, _):
            off = pl.multiple_of(tc - SUBLANES - j * SUBLANES, SUBLANES)
            av = an_s[pl.ds(off, SUBLANES), :]
            bv = g_s[pl.ds(off, SUBLANES), :]
            for d in (1, 2, 4):
                keep = row < SUBLANES - d
                bsh = jnp.where(keep, pltpu.roll(bv, SUBLANES - d, axis=0), 0.0)
                ash = jnp.where(keep, pltpu.roll(av, SUBLANES - d, axis=0), 1.0)
                bv = av * bsh + bv
                av = av * ash
            ds_ext[pl.ds(off, SUBLANES), :] = av * ds_ext[pl.ds(off + SUBLANES, 1), :] + bv
            return 0

        lax.fori_loop(0, tc // SUBLANES, scan_step, 0, unroll=SCAN_UNROLL)

        def acc(row_index, val):
            dsm_ref[pl.ds(row_index, 1), :] += jnp.sum(val, axis=0, keepdims=True)

        def acc_block(row_index, sl, val):
            dsm_ref[pl.ds(row_index, 1), sl] += jnp.sum(val, axis=0, keepdims=True)

        for j in range(nblk):
            sl = slice(j * gb, (j + 1) * gb)
            ds = ds_ext[0:tc, sl]
            hprev = hs_ext[pl.ds(SUBLANES - 1, tc), sl]
            a = a_ext[0:tc, sl]
            sq = sq_s[:, sl]
            ig = i_s[:, sl]
            r = r_s[:, sl]
            xcj = xc_s[:, sl]
            t1 = ds * xcj
            dla = (ds * hprev) * a - (t1 * ig) * ((a * a) * lax.rsqrt(sq * sq))
            acc_block(ROW_DLAM, sl, dla * r)
            dpr = (dla * c8[:, sl]) * (r * (1.0 - r))
            dpi = (t1 * sq) * (ig * (1.0 - ig))
            acc_block(ROW_DBR, sl, dpr)
            acc_block(ROW_DBI, sl, dpi)
            p16 = dpr.astype(BF16)
            q16 = dpi.astype(BF16)
            x16 = xcj.astype(BF16)
            dwr_ref[j] += lax.dot_general(x16, p16, TN_DIMS, preferred_element_type=F32)
            dwi_ref[j] += lax.dot_general(x16, q16, TN_DIMS, preferred_element_type=F32)
            dxc = (ds * (sq * ig)
                   + lax.dot_general(p16, wr_ref[j], NT_DIMS, preferred_element_type=F32)
                   + lax.dot_general(q16, wi_ref[j], NT_DIMS, preferred_element_type=F32))
            dxc_ext[0:tc, sl] = dxc
            acc_block(ROW_DBA, sl, dxc)

        now = slice(SUBLANES, SUBLANES + tc)
        for j in range(nblk):
            sl = slice(j * gb, (j + 1) * gb)
            lanes = lambda group: slice(group * c + j * gb, group * c + (j + 1) * gb)
            gaj = u_ref[:, lanes(1)]
            sgaj = _sig(gaj)
            du_ref[:, lanes(1)] = (dy_ref[:, sl] * hs_ref[:, sl] * (sgaj * (1.0 + gaj * (1.0 - sgaj)))).astype(BF16)

            dxc = dxc_ext[0:tc, sl]
            dxa = wa_ref[pl.ds(ka - 1, 1), sl] * dxc
            acc_block(ROW_DWA + ka - 1, sl, dxc * xa_ext[now, sl])
            for k in range(ka - 1):
                acc_block(ROW_DWA + k, sl, dxc * xa_ext[pl.ds(SUBLANES - (ka - 1) + k, tc), sl])
                dxa = dxa + wa_ref[pl.ds(k, 1), sl] * dxc_ext[pl.ds(ka - 1 - k, tc), sl]
            du_ref[:, lanes(0)] = dxa.astype(BF16)

            cv = wb_ref[pl.ds(0, 1), sl] * v_ext[pl.ds(SUBLANES - (kb - 1), tc), sl]
            for k in range(1, kb):
                cv = cv + wb_ref[pl.ds(k, 1), sl] * v_ext[pl.ds(SUBLANES - (kb - 1) + k, tc), sl]
            gbv = u_ref[:, lanes(5)]
            sgb = _sig(gbv)
            silu_b = gbv * sgb
            dyb = dy_ref[:, c + j * gb:c + (j + 1) * gb]
            gB = u_ref[:, lanes(2)]
            du_ref[:, lanes(2)] = (dyb * cv * silu_b).astype(BF16)
            du_ref[:, lanes(5)] = (dyb * gB * cv * (sgb * (1.0 + gbv * (1.0 - sgb)))).astype(BF16)
            dcv = dyb * gB * silu_b
            dcv_ext[0:tc, sl] = dcv
            dv = wb_ref[pl.ds(kb - 1, 1), sl] * dcv
            acc_block(ROW_DWB + kb - 1, sl, dcv * v_ext[now, sl])
            for k in range(kb - 1):
                acc_block(ROW_DWB + k, sl, dcv * v_ext[pl.ds(SUBLANES - (kb - 1) + k, tc), sl])
                dv = dv + wb_ref[pl.ds(k, 1), sl] * dcv_ext[pl.ds(kb - 1 - k, tc), sl]
            du_ref[:, lanes(3)] = (dv * u_ref[:, lanes(4)]).astype(BF16)
            du_ref[:, lanes(4)] = (dv * u_ref[:, lanes(3)]).astype(BF16)

        a_ext[tail, :] = a_ext[head, :]
        ds_ext[tail, :] = ds_ext[head, :]
        dxc_ext[tail, :] = dxc_ext[head, :]
        dcv_ext[tail, :] = dcv_ext[head, :]

        @pl.when(i == nt - 1)
        def _():
            dsm_ref[pl.ds(ROW_DLAM, 1), :] = dsm_ref[pl.ds(ROW_DLAM, 1), :] * dc8

    full = lambda shape: pl.BlockSpec(shape, lambda i: (0,) * len(shape))
    rev = lambda i: (nt - 1 - i, 0)
    halo = lambda i: (jnp.maximum((nt - 1 - i) * hb - 1, 0), 0)
    ext = pltpu.VMEM((tc + SUBLANES, c), F32)
    blk = pltpu.VMEM((tc, c), F32)
    body, more_specs, more = _behind(body, 13, after)
    return pl.pallas_call(
        body, name=name, grid=(nt,),
        in_specs=[pl.BlockSpec((tc, 6 * c), rev), pl.BlockSpec((SUBLANES, 6 * c), halo),
                  pl.BlockSpec((tc, c), rev), pl.BlockSpec((SUBLANES, c), halo),
                  pl.BlockSpec((tc, 2 * c), rev),
                  full(wa.shape), full(ba.shape), full(wr.shape), full(br.shape),
                  full(wi.shape), full(bi.shape), full(lam.shape), full(wb.shape)] + more_specs,
        out_specs=[pl.BlockSpec((tc, 6 * c), rev), full((SMALL_ROWS, c)), full(wr.shape), full(wi.shape)],
        out_shape=[jax.ShapeDtypeStruct((t, 6 * c), BF16), jax.ShapeDtypeStruct((SMALL_ROWS, c), F32),
                   jax.ShapeDtypeStruct(wr.shape, F32), jax.ShapeDtypeStruct(wi.shape, F32)],
        scratch_shapes=[ext] * 7 + [blk] * 6,
        compiler_params=_params(("arbitrary",)),
    )(u, u, hs, hs, dy, wa, ba, wr, br, wi, bi, lam, wb, *more)


def _behind(body, n_in, after):
    if after is None:
        return body, [], []
    return (lambda *refs: body(*refs[:n_in], *refs[n_in + 1:])), [ANY], [after]


def _in_proj(hn, wg, name, after=None):
    t, d = hn.shape
    s, _, ns = wg.shape
    tm = 1408 if t % 1408 == 0 else _row_tile(t)

    def body(hn_ref, w_ref, u_ref):
        u_ref[...] = jnp.dot(hn_ref[...], w_ref[...], preferred_element_type=F32)

    body, more_specs, more = _behind(body, 2, after)
    return pl.pallas_call(
        body, name=name, grid=(t // tm, s),
        in_specs=[pl.BlockSpec((tm, d), lambda i, n: (i, 0)),
                  pl.BlockSpec((None, d, ns), lambda i, n: (n, 0, 0))] + more_specs,
        out_specs=pl.BlockSpec((tm, ns), lambda i, n: (i, n)),
        out_shape=jax.ShapeDtypeStruct((t, s * ns), F32),
        compiler_params=_params(("arbitrary", "arbitrary")),
    )(hn, wg, *more)


def _out_proj_dw(y, dout, name, after=None):
    t, dm = y.shape
    d = dout.shape[1]
    tmm = _col_tile(dm, (1024, 512, 256))
    tn = _col_tile(d, (512, 256))

    def body(y_ref, g_ref, o_ref):
        o_ref[...] = lax.dot_general(y_ref[...], g_ref[...].astype(BF16), TN_DIMS, preferred_element_type=F32)

    body, more_specs, more = _behind(body, 2, after)
    return pl.pallas_call(
        body, name=name, grid=(d // tn, dm // tmm),
        in_specs=[pl.BlockSpec((t, tmm), lambda n, m: (0, m)),
                  pl.BlockSpec((t, tn), lambda n, m: (0, n))] + more_specs,
        out_specs=pl.BlockSpec((tmm, tn), lambda n, m: (m, n)),
        out_shape=jax.ShapeDtypeStruct((dm, d), F32),
        compiler_params=_params(("arbitrary", "arbitrary")),
    )(y, dout, *more)


def _in_proj_bwd(du, wg, h, g, dout, name, after=None, split=None, w_below=None):
    t, d = h.shape
    s, _, ns = wg.shape
    tm = 1408 if t % 1408 == 0 else _row_tile(t)
    tn = _col_tile(d, (512, 256))

    def mm_body(du_ref, w_ref, o_ref):
        total = lax.dot_general(du_ref[:, 0:ns], w_ref[0], NT_DIMS, preferred_element_type=F32)
        for a in range(1, s):
            total = total + lax.dot_general(du_ref[:, a * ns:(a + 1) * ns], w_ref[a], NT_DIMS,
                                            preferred_element_type=F32)
        o_ref[...] = total

    mm_body, more_specs, more = _behind(mm_body, 2, after)
    dhn = pl.pallas_call(
        mm_body, name=name, grid=(t // tm, d // tn),
        in_specs=[pl.BlockSpec((tm, s * ns), lambda i, n: (i, 0)),
                  pl.BlockSpec((s, tn, ns), lambda i, n: (0, n, 0))] + more_specs,
        out_specs=pl.BlockSpec((tm, tn), lambda i, n: (i, n)),
        out_shape=jax.ShapeDtypeStruct((t, d), F32),
        compiler_params=_params(("arbitrary", "arbitrary")),
    )(du, wg, *more)

    tr = 352 if t % 352 == 0 else 192
    nt = t // tr

    def row_grad(dhn_ref, h_ref, g_ref, dout_ref, dg_ref):
        @pl.when(pl.program_id(0) == 0)
        def _():
            dg_ref[...] = jnp.zeros_like(dg_ref)

        x = h_ref[...]
        dn = dhn_ref[...]
        r = lax.rsqrt(jnp.mean(x * x, axis=-1, keepdims=True) + RMS_EPS)
        gd = dn * g_ref[...]
        dot = jnp.mean(gd * x, axis=-1, keepdims=True)
        dg_ref[...] += jnp.sum(dn * (x * r), axis=0, keepdims=True)
        return dout_ref[...] + (r * gd - x * ((r * r * r) * dot))

    rows = pl.BlockSpec((tr, d), lambda i: (i, 0))
    one = pl.BlockSpec((1, d), lambda i: (0, 0))
    if split is None:
        dm = w_below.shape[0]

        def norm_body(dhn_ref, h_ref, g_ref, dout_ref, w_ref, dh_ref, dg_ref, dy_ref):
            dh = row_grad(dhn_ref, h_ref, g_ref, dout_ref, dg_ref)
            dh_ref[...] = dh
            dy_ref[...] = lax.dot_general(dh.astype(BF16), w_ref[...], NT_DIMS, preferred_element_type=F32)

        return pl.pallas_call(
            norm_body, name=name + "_norm", grid=(nt,),
            in_specs=[rows, rows, one, rows, pl.BlockSpec((dm, d), lambda i: (0, 0))],
            out_specs=[rows, one, pl.BlockSpec((tr, dm), lambda i: (i, 0))],
            out_shape=[jax.ShapeDtypeStruct((t, d), F32), jax.ShapeDtypeStruct((1, d), F32),
                       jax.ShapeDtypeStruct((t, dm), F32)],
            compiler_params=_params(("arbitrary",)),
        )(dhn, h, g, dout, w_below)

    n_head, n_body = split
    n_first = tr - n_head
    n_last = n_head + n_body - (nt - 1) * tr
    assert nt >= 2 and 0 < n_head < tr and 0 < n_last <= tr and n_head % SUBLANES == 0 and n_last % SUBLANES == 0

    def split_body(dhn_ref, h_ref, g_ref, dout_ref, body_ref, head_ref, dg_ref, stage, sems):
        i = pl.program_id(0)
        slot = i % 2

        def first_copy(sl):
            return pltpu.make_async_copy(stage.at[sl, pl.ds(n_head, n_first)], body_ref.at[pl.ds(0, n_first)], sems.at[sl])

        def middle_copy(sl, step):
            start = pl.multiple_of(step * tr - n_head, SUBLANES)
            return pltpu.make_async_copy(stage.at[sl], body_ref.at[pl.ds(start, tr)], sems.at[sl])

        def last_copy(sl):
            return pltpu.make_async_copy(stage.at[sl, pl.ds(0, n_last)],
                                         body_ref.at[pl.ds((nt - 1) * tr - n_head, n_last)], sems.at[sl])

        dh = row_grad(dhn_ref, h_ref, g_ref, dout_ref, dg_ref)

        @pl.when(i == 2)
        def _():
            first_copy(0).wait()

        @pl.when(i > 2)
        def _():
            middle_copy(slot, i - 2).wait()

        stage[slot] = dh

        @pl.when(i == 0)
        def _():
            head_ref[...] = stage[0, 0:n_head, :]
            first_copy(0).start()

        @pl.when((i > 0) & (i < nt - 1))
        def _():
            middle_copy(slot, i).start()

        @pl.when(i == nt - 1)
        def _():
            last = last_copy((nt - 1) % 2)
            last.start()
            if nt == 2:
                first_copy(0).wait()
            else:
                middle_copy((nt - 2) % 2, nt - 2).wait()
            last.wait()

    return pl.pallas_call(
        split_body, name=name + "_norm", grid=(nt,),
        in_specs=[rows, rows, one, rows],
        out_specs=[ANY, pl.BlockSpec((n_head, d), lambda i: (0, 0)), one],
        out_shape=[jax.ShapeDtypeStruct((n_body, d), F32), jax.ShapeDtypeStruct((n_head, d), F32),
                   jax.ShapeDtypeStruct((1, d), F32)],
        scratch_shapes=[pltpu.VMEM((2, tr, d), F32), pltpu.SemaphoreType.DMA((2,))],
        compiler_params=_params(("arbitrary",)),
    )(dhn, h, g, dout)


def _in_proj_dw(hn, du, s, name, after=None):
    t, d = hn.shape
    ns = du.shape[1] // s
    tmm = _col_tile(d, (1024, 512, 256))
    tn = _col_tile(ns, (768, 384, 128))
    nb = ns // tn

    def body(hn_ref, du_ref, o_ref):
        o_ref[...] = lax.dot_general(hn_ref[...], du_ref[...], TN_DIMS, preferred_element_type=F32)

    body, more_specs, more = _behind(body, 2, after)
    return pl.pallas_call(
        body, name=name, grid=(d // tmm, s * nb),
        in_specs=[pl.BlockSpec((t, tmm), lambda m, n: (0, m)),
                  pl.BlockSpec((t, tn), lambda m, n: (0, n))] + more_specs,
        out_specs=pl.BlockSpec((None, tmm, tn), lambda m, n: (n // nb, m, n % nb)),
        out_shape=jax.ShapeDtypeStruct((s, d, ns), F32),
        compiler_params=_params(("arbitrary", "arbitrary")),
    )(hn, du, *more)


def _out_proj_loss(h, y, w, tgt, g, n_meta, t_real, name):
    t, d = h.shape
    dm = y.shape[1]
    tm = 352 if t % 352 == 0 else 192

    def body(h_ref, y_ref, w_ref, t_ref, g_ref, dh_ref, loss_ref, dg_ref, dmix_ref):
        i = pl.program_id(0)

        @pl.when(i == 0)
        def _():
            loss_ref[...] = jnp.zeros_like(loss_ref)
            dg_ref[...] = jnp.zeros_like(dg_ref)

        x = h_ref[...] + jnp.dot(y_ref[...], w_ref[...], preferred_element_type=F32)
        gv = g_ref[...]
        r = lax.rsqrt(jnp.mean(x * x, axis=-1, keepdims=True) + RMS_EPS)
        xr = x * r
        rows = i * tm + lax.broadcasted_iota(jnp.int32, (tm, 1), 0)
        valid = (rows >= n_meta) & (rows < t_real)
        err = jnp.where(valid, xr * gv - t_ref[...], 0.0)
        loss_ref[...] += 0.5 * jnp.sum(jnp.mean(err * err, axis=-1, keepdims=True))
        dy = err * (1.0 / d)
        gd = dy * gv
        dot = jnp.mean(gd * x, axis=-1, keepdims=True)
        dh = r * gd - x * ((r * r * r) * dot)
        dh_ref[...] = dh
        dg_ref[...] += jnp.sum(dy * xr, axis=0, keepdims=True)
        dmix_ref[...] = lax.dot_general(dh.astype(BF16), w_ref[...], NT_DIMS, preferred_element_type=F32)

    rows = pl.BlockSpec((tm, d), lambda i: (i, 0))
    wide = pl.BlockSpec((tm, dm), lambda i: (i, 0))
    return pl.pallas_call(
        body, name=name, grid=(t // tm,),
        in_specs=[rows, wide, pl.BlockSpec((dm, d), lambda i: (0, 0)), rows, pl.BlockSpec((1, d), lambda i: (0, 0))],
        out_specs=[rows, pl.BlockSpec((1, LANES), lambda i: (0, 0)), pl.BlockSpec((1, d), lambda i: (0, 0)), wide],
        out_shape=[jax.ShapeDtypeStruct((t, d), F32), jax.ShapeDtypeStruct((1, LANES), F32),
                   jax.ShapeDtypeStruct((1, d), F32), jax.ShapeDtypeStruct((t, dm), F32)],
        compiler_params=_params(("arbitrary",)),
    )(h, y, w, tgt, g)


def _adamw_rows(rows, cols):
    for cand in (512, 256, 128, 64, 32, 16, 8):
        if rows % cand == 0 and cand * cols * 4 <= 2 * 1024 * 1024:
            return cand
    return rows


def _adamw_math(w_ref, g_ref, m_ref, v_ref, d_ref, nm_ref, nv_ref):
    gv = g_ref[...]
    m2 = ADAM_B1 * m_ref[...] + (1.0 - ADAM_B1) * gv
    v2 = ADAM_B2 * v_ref[...] + (1.0 - ADAM_B2) * (gv * gv)
    m_hat = m2 / (1.0 - ADAM_B1 ** ADAM_STEP)
    v_hat = v2 / (1.0 - ADAM_B2 ** ADAM_STEP)
    d_ref[...] = -ADAM_LR * (m_hat / (jnp.sqrt(v_hat) + ADAM_EPS) + ADAM_WD * w_ref[...])
    nm_ref[...] = m2
    nv_ref[...] = v2


def _adamw(w, g, m, v, name, after=None):
    shape = w.shape
    assert len(shape) >= 2 and w.size * 4 <= 2 * 1024 * 1024

    def body(*refs):
        _adamw_math(*refs)

    body, more_specs, more = _behind(body, 4, after)
    spec = pl.BlockSpec(shape, lambda i: (0,) * len(shape))
    return pl.pallas_call(
        body, name=name, grid=(1,),
        in_specs=[spec] * 4 + more_specs, out_specs=[spec] * 3,
        out_shape=[jax.ShapeDtypeStruct(shape, F32)] * 3,
        compiler_params=_params(("arbitrary",)),
    )(w, g, m, v, *more)


def _adamw_layer(w, g, m, v, layer, kept, name, after=None):
    nl, rows, cols = w.shape
    tr = _adamw_rows(rows, cols)
    n_kept = 0 if kept is None else 3

    def body(*refs):
        _adamw_math(*refs[:4], *refs[4 + n_kept:])

    body, more_specs, more = _behind(body, 4 + n_kept, after)
    lay = pl.BlockSpec((None, tr, cols), lambda i: (layer, i, 0))
    return pl.pallas_call(
        body, name=name, grid=(rows // tr,),
        in_specs=[lay, pl.BlockSpec((tr, cols), lambda i: (i, 0)), lay, lay] + [ANY] * n_kept + more_specs,
        out_specs=[lay] * 3,
        out_shape=[jax.ShapeDtypeStruct((nl, rows, cols), F32)] * 3,
        input_output_aliases={4 + k: k for k in range(n_kept)},
        compiler_params=_params(("arbitrary",)),
    )(w, g, m, v, *([] if kept is None else kept), *more)


def _pair_add(x, ra, c_idx, name):
    s, _, rows, cols = x.shape
    tr = _slab_rows(rows, cols)

    def body(c_ref, x_ref, r_ref, o_ref):
        o_ref[...] = (x_ref[...] + r_ref[...]).astype(BF16)

    return pl.pallas_call(
        body, name=name,
        grid_spec=pltpu.PrefetchScalarGridSpec(
            num_scalar_prefetch=1, grid=(s, rows // tr),
            in_specs=[pl.BlockSpec((None, None, tr, cols), lambda a, i, c_ref: (a, c_ref[0], i, 0)),
                      pl.BlockSpec((None, tr, cols), lambda a, i, c_ref: (a, i, 0))],
            out_specs=pl.BlockSpec((None, tr, cols), lambda a, i, c_ref: (a, i, 0))),
        out_shape=jax.ShapeDtypeStruct((s, rows, cols), BF16),
        compiler_params=_params(("arbitrary", "arbitrary")),
    )(c_idx, x, ra)


def _chip_sum(rc, p, where, n_slots, name):
    s, rows, cols = rc.shape
    tr = _slab_rows(rows, cols)

    def body(w_ref, x_ref, p_ref, o_ref):
        me = w_ref[0]
        total = jnp.where(me == 0, p_ref[...], x_ref[0]).astype(F32)
        for a in range(1, s):
            total = total + jnp.where(me == a, p_ref[...], x_ref[a]).astype(F32)
        o_ref[...] = total

    return pl.pallas_call(
        body, name=name,
        grid_spec=pltpu.PrefetchScalarGridSpec(
            num_scalar_prefetch=1, grid=(rows // tr,),
            in_specs=[pl.BlockSpec((s, tr, cols), lambda i, w_ref: (0, i, 0)),
                      pl.BlockSpec((None, tr, cols), lambda i, w_ref: (w_ref[0], i, 0))],
            out_specs=pl.BlockSpec((None, tr, cols), lambda i, w_ref: (w_ref[1], i, 0))),
        out_shape=jax.ShapeDtypeStruct((n_slots, rows, cols), F32),
        compiler_params=_params(("arbitrary",)),
    )(where, rc, p)


def _cast_place(w, layer, me_idx, name, after=None):
    _, rows, cols = w.shape
    tr = _slab_rows(rows, cols)

    def body(m_ref, w_ref, o_ref):
        o_ref[...] = w_ref[...].astype(BF16)

    body, more_specs, more = _behind(body, 2, after)
    return pl.pallas_call(
        body, name=name,
        grid_spec=pltpu.PrefetchScalarGridSpec(
            num_scalar_prefetch=1, grid=(rows // tr,),
            in_specs=[pl.BlockSpec((None, tr, cols), lambda i, m_ref: (layer, i, 0))] + more_specs,
            out_specs=pl.BlockSpec((None, tr, cols), lambda i, m_ref: (m_ref[0], i, 0))),
        out_shape=jax.ShapeDtypeStruct((N_CHIPS, rows, cols), BF16),
        compiler_params=_params(("arbitrary",)),
    )(me_idx, w, *more)


def _place():
    x, y, c = lax.axis_index("x"), lax.axis_index("y"), lax.axis_index("c")
    chips = [(1 - x, y), (x, 1 - y), (1 - x, 1 - y)]
    return x, y, c, chips


def _chip_index(cx, cy):
    return 2 * cx + cy


def _gather_copies(bufs, stage):
    x, y, c, chips = _place()
    me = _chip_index(x, y)
    copies = []
    for b in bufs:
        for chip in chips:
            src = _chip_index(*chip)
            if stage == 0:
                copies.append((b.at[me, c], (*chip, c), b.at[src, c]))
            else:
                copies.append((b.at[src, c], (x, y, 1 - c), b.at[src, 1 - c]))
    return copies


def _remote(ref, peer, ssem, rsem, k):
    return pltpu.make_async_remote_copy(src_ref=ref, dst_ref=ref, send_sem=ssem.at[k], recv_sem=rsem.at[k],
                                        device_id=peer, device_id_type=MESH)


def _gather_first(bufs, small):
    n = len(bufs)
    k = 3 * n

    def body(*refs):
        sm_ref = refs[n]
        b_refs, smg_ref = refs[n + 1:2 * n + 1], refs[2 * n + 1]
        lsem, ssem, rsem = refs[2 * n + 2:]
        x, y, c, chips = _place()
        me = _chip_index(x, y)
        local = pltpu.make_async_copy(sm_ref, smg_ref.at[me], lsem)
        local.start()
        first = _gather_copies(b_refs, 0)
        second = _gather_copies(b_refs, 1)
        started = []
        for i, (ref, peer, _) in enumerate(first):
            started.append(_remote(ref, peer, ssem, rsem, i))
        for j, chip in enumerate(chips):
            started.append(pltpu.make_async_remote_copy(
                src_ref=sm_ref, dst_ref=smg_ref.at[me], send_sem=ssem.at[2 * k + j], recv_sem=rsem.at[2 * k + j],
                device_id=(*chip, c), device_id_type=MESH))
        for cp in started:
            cp.start()
        for i, (_, peer, lands) in enumerate(first):
            _remote(lands, peer, ssem, rsem, i).wait_recv()
            ref, sib, _ = second[i]
            fwd = _remote(ref, sib, ssem, rsem, k + i)
            fwd.start()
            started.append(fwd)
        for i, (_, sib, lands) in enumerate(second):
            _remote(lands, sib, ssem, rsem, k + i).wait_recv()
        for j, chip in enumerate(chips):
            theirs = smg_ref.at[_chip_index(*chip)]
            pltpu.make_async_remote_copy(src_ref=theirs, dst_ref=theirs, send_sem=ssem.at[2 * k + j],
                                         recv_sem=rsem.at[2 * k + j], device_id=(*chip, c),
                                         device_id_type=MESH).wait_recv()
        for cp in started:
            cp.wait_send()
        local.wait()

    return pl.pallas_call(
        body, name="gather_first",
        in_specs=[ANY] * (n + 1), out_specs=[ANY] * (n + 1),
        out_shape=[jax.ShapeDtypeStruct(b.shape, b.dtype) for b in bufs]
        + [jax.ShapeDtypeStruct((N_CHIPS,) + small.shape, small.dtype)],
        input_output_aliases={i: i for i in range(n)},
        scratch_shapes=[pltpu.SemaphoreType.DMA, pltpu.SemaphoreType.DMA((2 * k + 3,)),
                        pltpu.SemaphoreType.DMA((2 * k + 3,))],
    )(*bufs, small)


HBM = pl.BlockSpec(memory_space=pltpu.HBM)
SEM = pl.BlockSpec(memory_space=pltpu.SEMAPHORE)
DATAFLOW = pltpu.SideEffectType.DATAFLOW_SIDE_EFFECTING


def _copies_start(bufs, plan, n_copies, name, after=None):
    n = len(bufs)
    extra = [] if after is None else [after]

    def body(*refs):
        refs = refs[:n] + refs[n + len(extra):]
        ssem, rsem = refs[n], refs[n + 1]
        b_refs, token = refs[n + 2:2 * n + 2], refs[2 * n + 2]
        copies = plan(b_refs)
        assert len(copies) == n_copies
        for i, (src, dst, peer, _) in enumerate(copies):
            pltpu.make_async_remote_copy(src_ref=src, dst_ref=dst, send_sem=ssem.at[i], recv_sem=rsem.at[i],
                                         device_id=peer, device_id_type=MESH).start()
        token[...] = jnp.zeros_like(token)

    return pl.pallas_call(
        body, name=name,
        out_shape=(pltpu.SemaphoreType.DMA((n_copies,)), pltpu.SemaphoreType.DMA((n_copies,)),
                   *[pltpu.HBM(b.shape, b.dtype) for b in bufs], jax.ShapeDtypeStruct((SUBLANES, LANES), F32)),
        in_specs=[HBM] * n + [ANY] * len(extra),
        out_specs=(SEM, SEM, *[HBM] * n, pl.BlockSpec(memory_space=pltpu.VMEM)),
        input_output_aliases={i: 2 + i for i in range(n)},
        compiler_params=pltpu.CompilerParams(has_side_effects=DATAFLOW),
    )(*[pltpu.with_memory_space_constraint(b, pltpu.HBM) for b in bufs], *extra)


def _copies_wait(bufs, ssem, rsem, after, plan, name):
    n = len(bufs)
    afters = list(after) if isinstance(after, (list, tuple)) else [after]

    def body(*refs):
        b_refs, ssem_ref, rsem_ref = refs[:n], refs[n], refs[n + 1]
        for i, (src, dst, peer, lands) in enumerate(plan(b_refs)):
            pltpu.make_async_remote_copy(src_ref=src, dst_ref=dst, send_sem=ssem_ref.at[i], recv_sem=rsem_ref.at[i],
                                         device_id=peer, device_id_type=MESH).wait_send()
            pltpu.make_async_remote_copy(src_ref=lands, dst_ref=lands, send_sem=ssem_ref.at[i],
                                         recv_sem=rsem_ref.at[i], device_id=peer, device_id_type=MESH).wait_recv()

    return pl.pallas_call(
        body, name=name,
        out_shape=tuple(pltpu.HBM(b.shape, b.dtype) for b in bufs),
        in_specs=[HBM] * n + [SEM, SEM] + [ANY] * len(afters), out_specs=tuple([HBM] * n),
        input_output_aliases={i: i for i in range(n)},
        compiler_params=pltpu.CompilerParams(has_side_effects=DATAFLOW),
    )(*bufs, ssem, rsem, *afters)


def _gather_plan(stage):
    return lambda refs: [(ref, ref, peer, lands) for ref, peer, lands in _gather_copies(refs, stage)]


def _swap_plan(refs):
    n = len(refs) // 2
    x, y, c, _ = _place()
    return [(refs[a].at[:, 1 - c], refs[n + a], (x, y, 1 - c), refs[n + a]) for a in range(n)]


def _scatter_plan(refs):
    n = len(refs) // 2
    x, y, c, chips = _place()
    me = _chip_index(x, y)
    return [(refs[a].at[_chip_index(*chip)], refs[n + a].at[me], (*chip, c), refs[n + a].at[_chip_index(*chip)])
            for a in range(n) for chip in chips]


def _pair_gather_plan(refs):
    x, y, c, _ = _place()
    return [(r.at[c], r.at[c], (x, y, 1 - c), r.at[1 - c]) for r in refs]


def _pair_swap(xs, name):
    n = len(xs)

    def body(*refs):
        x_refs, o_refs, ssem, rsem = refs[:n], refs[n:2 * n], refs[2 * n], refs[2 * n + 1]
        x, y, c, _ = _place()
        copies = [pltpu.make_async_remote_copy(src_ref=x_refs[a].at[:, 1 - c], dst_ref=o_refs[a],
                                               send_sem=ssem.at[a], recv_sem=rsem.at[a],
                                               device_id=(x, y, 1 - c), device_id_type=MESH) for a in range(n)]
        for cp in copies:
            cp.start()
        for cp in copies:
            cp.wait()

    return pl.pallas_call(
        body, name=name, in_specs=[ANY] * n, out_specs=[ANY] * n,
        out_shape=[jax.ShapeDtypeStruct((a.shape[0],) + a.shape[2:], a.dtype) for a in xs],
        scratch_shapes=[pltpu.SemaphoreType.DMA((n,)), pltpu.SemaphoreType.DMA((n,))],
    )(*xs)


def _chip_scatter(ps):
    n = len(ps)

    def body(*refs):
        p_refs, o_refs, ssem, rsem = refs[:n], refs[n:2 * n], refs[2 * n], refs[2 * n + 1]
        x, y, c, chips = _place()
        me = _chip_index(x, y)
        sends = []
        for a in range(n):
            for j, chip in enumerate(chips):
                sends.append(pltpu.make_async_remote_copy(
                    src_ref=p_refs[a].at[_chip_index(*chip)], dst_ref=o_refs[a].at[me],
                    send_sem=ssem.at[3 * a + j], recv_sem=rsem.at[3 * a + j],
                    device_id=(*chip, c), device_id_type=MESH))
        for cp in sends:
            cp.start()
        for a in range(n):
            for j, chip in enumerate(chips):
                src = _chip_index(*chip)
                pltpu.make_async_remote_copy(
                    src_ref=p_refs[a].at[src], dst_ref=o_refs[a].at[src],
                    send_sem=ssem.at[3 * a + j], recv_sem=rsem.at[3 * a + j],
                    device_id=(*chip, c), device_id_type=MESH).wait_recv()
        for cp in sends:
            cp.wait_send()

    return pl.pallas_call(
        body, name="chip_scatter", in_specs=[ANY] * n, out_specs=[ANY] * n,
        out_shape=[jax.ShapeDtypeStruct(a.shape, a.dtype) for a in ps],
        scratch_shapes=[pltpu.SemaphoreType.DMA((3 * n,)), pltpu.SemaphoreType.DMA((3 * n,))],
    )(*ps)


def _final_gather(fs, rep):
    n = len(fs)

    def body(*refs):
        o_refs, repo_ref = refs[n + 1:2 * n + 1], refs[2 * n + 1]
        ssem, rsem = refs[2 * n + 2:]
        x, y, c, chips = _place()
        slot = 4 * x + 2 * y + c
        copies = [pltpu.make_async_remote_copy(src_ref=o_refs[a].at[c], dst_ref=o_refs[a].at[c],
                                               send_sem=ssem.at[a], recv_sem=rsem.at[a],
                                               device_id=(x, y, 1 - c), device_id_type=MESH) for a in range(n)]
        peers = [(x, y, 1 - c)] + [(*chip, c) for chip in chips] + [(*chip, 1 - c) for chip in chips]
        for k, peer in enumerate(peers):
            copies.append(pltpu.make_async_remote_copy(src_ref=repo_ref.at[slot], dst_ref=repo_ref.at[slot],
                                                       send_sem=ssem.at[n + k], recv_sem=rsem.at[n + k],
                                                       device_id=peer, device_id_type=MESH))
        for cp in copies:
            cp.start()
        for a in range(n):
            pltpu.make_async_remote_copy(src_ref=o_refs[a].at[1 - c], dst_ref=o_refs[a].at[1 - c],
                                         send_sem=ssem.at[a], recv_sem=rsem.at[a],
                                         device_id=(x, y, 1 - c), device_id_type=MESH).wait_recv()
        for k, peer in enumerate(peers):
            px, py, pc = peer
            theirs = repo_ref.at[4 * px + 2 * py + pc]
            pltpu.make_async_remote_copy(src_ref=theirs, dst_ref=theirs, send_sem=ssem.at[n + k], recv_sem=rsem.at[n + k],
                                         device_id=peer, device_id_type=MESH).wait_recv()
        for cp in copies:
            cp.wait_send()

    return pl.pallas_call(
        body, name="final_gather", in_specs=[ANY] * (n + 1), out_specs=[ANY] * (n + 1),
        out_shape=[jax.ShapeDtypeStruct(a.shape, a.dtype) for a in fs] + [jax.ShapeDtypeStruct(rep.shape, rep.dtype)],
        input_output_aliases={k: k for k in range(n + 1)},
        scratch_shapes=[pltpu.SemaphoreType.DMA((n + 7,)), pltpu.SemaphoreType.DMA((n + 7,))],
    )(*fs, rep)


def _block_diag(w, gb):
    nh, hd, _ = w.shape
    per = gb // hd
    w4 = w.reshape(nh // per, per, hd, hd)
    eye = jnp.eye(per, dtype=w.dtype)
    return jnp.einsum("jaik,ab->jaibk", w4, eye).reshape(nh // per, gb, gb)


def _diag_blocks(dense, hd):
    nj, gb, _ = dense.shape
    per = gb // hd
    d5 = dense.reshape(nj, per, hd, per, hd)
    return jnp.stack([d5[:, a, :, a, :] for a in range(per)], axis=1).reshape(nj * per, hd, hd)


def _round_up(n, q):
    return (n + q - 1) // q * q


def kernel(x, meta, norm_g, w_in, conv_a_w, conv_a_b, lru_wr, lru_br, lru_wi, lru_bi, lru_lambda, conv_b_w, w_out, final_g, loss_target, m_meta, m_norm_g, m_w_in, m_conv_a_w, m_conv_a_b, m_lru_wr, m_lru_br, m_lru_wi, m_lru_bi, m_lru_lambda, m_conv_b_w, m_w_out, m_final_g, v_meta, v_norm_g, v_w_in, v_conv_a_w, v_conv_a_b, v_lru_wr, v_lru_br, v_lru_wi, v_lru_bi, v_lru_lambda, v_conv_b_w, v_w_out, v_final_g):
    weights = dict(meta=meta, norm_g=norm_g, w_in=w_in, conv_a_w=conv_a_w, conv_a_b=conv_a_b, lru_wr=lru_wr,
                   lru_br=lru_br, lru_wi=lru_wi, lru_bi=lru_bi, lru_lambda=lru_lambda, conv_b_w=conv_b_w,
                   w_out=w_out, final_g=final_g)
    mom1 = dict(meta=m_meta, norm_g=m_norm_g, w_in=m_w_in, conv_a_w=m_conv_a_w, conv_a_b=m_conv_a_b,
                lru_wr=m_lru_wr, lru_br=m_lru_br, lru_wi=m_lru_wi, lru_bi=m_lru_bi, lru_lambda=m_lru_lambda,
                conv_b_w=m_conv_b_w, w_out=m_w_out, final_g=m_final_g)
    mom2 = dict(meta=v_meta, norm_g=v_norm_g, w_in=v_w_in, conv_a_w=v_conv_a_w, conv_a_b=v_conv_a_b,
                lru_wr=v_lru_wr, lru_br=v_lru_br, lru_wi=v_lru_wi, lru_bi=v_lru_bi, lru_lambda=v_lru_lambda,
                conv_b_w=v_conv_b_w, w_out=v_w_out, final_g=v_final_g)
    names = list(weights)

    assert x.shape[0] == 1
    seq, d = x.shape[1], x.shape[2]
    n_meta, ds = meta.shape
    depth = norm_g.shape[0]
    c = lru_lambda.shape[1]
    nh, hd = lru_wr.shape[1], lru_wr.shape[2]
    ns = w_in.shape[2]
    dms = w_out.shape[1]
    cs = conv_a_w.shape[2]
    ka, kb = conv_a_w.shape[1], conv_b_w.shape[1]
    s = N_CHIPS
    assert depth == N_CORES and d == s * ds and c == s * cs and s * ns == 6 * c and s * dms == 2 * c
    gb = min(GATE_BLOCK, c)
    t_real = n_meta + seq
    t = _round_up(t_real, ROW_QUANTUM)
    my_c = lax.axis_index("c").astype(jnp.int32)
    my_chip = (2 * lax.axis_index("x") + lax.axis_index("y")).astype(jnp.int32)
    c_idx = my_c.reshape(1)
    chip_idx = my_chip.reshape(1)

    sm_rows = _round_up(n_meta + depth * SUBLANES, 2 * SUBLANES)
    small = jnp.zeros((sm_rows, ds), F32)
    small = small.at[0:n_meta, :].set(meta)
    for l in range(depth):
        base = n_meta + l * SUBLANES
        small = small.at[base:base + ka, 0:cs].set(conv_a_w[l])
        small = small.at[base + ka:base + ka + kb, 0:cs].set(conv_b_w[l])
    (small_g,) = _gather_first([], small)
    meta_full = jnp.transpose(small_g[:, 0:n_meta, :], (1, 0, 2)).reshape(n_meta, d)
    wa_full, wb_full = [], []
    for l in range(depth):
        base = n_meta + l * SUBLANES
        wa_full.append(jnp.transpose(small_g[:, base:base + ka, 0:cs], (1, 0, 2)).reshape(ka, c))
        wb_full.append(jnp.transpose(small_g[:, base + ka:base + ka + kb, 0:cs], (1, 0, 2)).reshape(kb, c))
    win0 = _cast_place(w_in, 0, chip_idx, "cast_w_in_0").reshape(s, 2, d // 2, ns)
    ssem_w, rsem_w, win0, token_w = _copies_start([win0], _gather_plan(0), 3, "gather_win0_ici_start", after=small_g)
    win_b = [None] + [_cast_place(w_in, l, chip_idx, f"cast_w_in_{l}", after=token_w).reshape(s, 2, d // 2, ns)
                      for l in range(1, depth)]
    wout_b = [_cast_place(w_out, l, chip_idx, f"cast_w_out_{l}", after=token_w).reshape(s, 2, dms // 2, d)
              for l in range(depth)]
    h = jnp.concatenate([meta_full, x[0], jnp.zeros((t - t_real, d), F32)], axis=0) + token_w[0, 0]
    tgt = jnp.concatenate([jnp.zeros((n_meta, d), F32), loss_target[0], jnp.zeros((t - t_real, d), F32)],
                          axis=0) + token_w[0, 0]
    u_own, hn_own = _norm_in_own(h, norm_g[0].reshape(1, d), win0.reshape(s, d, ns), chip_idx, "norm_in_0_own")
    (win0,) = _copies_wait([win0], ssem_w, rsem_w, [u_own, tgt] + win_b[1:] + wout_b, _gather_plan(0),
                           "gather_win0_ici_wait")
    ssem_w, rsem_w, win0, token_w = _copies_start([win0], _gather_plan(1), 3, "gather_win0_d2d_start")
    def travel(buf, stage, tag, after):
        return _copies_start([buf], _gather_plan(stage), 3, f"gather_{tag}_{'d2d' if stage else 'ici'}_start",
                             after=after)

    def arrived(state, stage, tag, after):
        (buf,) = _copies_wait([state[2]], state[0], state[1], after, _gather_plan(stage),
                              f"gather_{tag}_{'d2d' if stage else 'ici'}_wait")
        return buf

    on_wout0 = travel(wout_b[0], 0, "wout0", token_w)
    on_win1 = travel(win_b[1], 0, "win1", on_wout0[3])
    on_wout1 = travel(wout_b[1], 0, "wout1", on_win1[3])
    token = on_wout1[3]
    (win_b[0],) = _copies_wait([win0], ssem_w, rsem_w, token, _gather_plan(1), "gather_win0_d2d_wait")

    layer_w = []
    for l in range(depth):
        layer_w.append(dict(
            g=norm_g[l].reshape(1, d), wa=wa_full[l], ba=conv_a_b[l].reshape(1, c),
            wr=_block_diag(lru_wr[l], gb).astype(BF16), br=lru_br[l].reshape(1, c),
            wi=_block_diag(lru_wi[l], gb).astype(BF16), bi=lru_bi[l].reshape(1, c),
            lam=lru_lambda[l].reshape(1, c), wb=wb_full[l]))
    saved = []
    for l, lw in enumerate(layer_w):
        first = l == 0
        lw["win"] = win_b[l].reshape(s, d, ns)
        mixer_w = (lw["wa"], lw["ba"], lw["wr"], lw["br"], lw["wi"], lw["bi"], lw["lam"], lw["wb"])
        if first:
            u = _norm_in_rest(hn_own, lw["win"], u_own, chip_idx, "norm_in_0_rest", after=token)
            hn = hn_own
            on_wout0 = travel(arrived(on_wout0, 0, "wout0", u), 1, "wout0", None)
            wout_b[0] = arrived(on_wout0, 1, "wout0", on_wout0[3])
            lw["wout"] = wout_b[0].reshape(2 * c, d)
            y, hs, h_next, hn_next = _mix_fwd(u, *mixer_w, f"mix_fwd_{l}", proj=(h, lw["wout"], layer_w[1]["g"]))
            saved.append((h, u, hn, y, hs))
            h = h_next
            on_win1 = travel(arrived(on_win1, 0, "win1", y), 1, "win1", None)
            win_b[1] = arrived(on_win1, 1, "win1", on_win1[3])
            on_wout1 = travel(arrived(on_wout1, 0, "wout1", y), 1, "wout1", on_win1[3])
            token = on_wout1[3]
        else:
            hn = hn_next
            u = _in_proj(hn, lw["win"], f"norm_in_{l}", after=token)
            wout_b[1] = arrived(on_wout1, 1, "wout1", u)
            lw["wout"] = wout_b[1].reshape(2 * c, d)
            y, hs = _mix_fwd(u, *mixer_w, f"mix_fwd_{l}")
            saved.append((h, u, hn, y, hs))
            dh, loss_lanes, d_final_g, dy = _out_proj_loss(h, y, lw["wout"], tgt, final_g.reshape(1, d), n_meta,
                                                           t_real, f"out_proj_{l}_loss")
    loss = lax.psum(loss_lanes[0, 0], ("x", "y", "c"))

    to_core = jnp.stack([my_chip, my_c])
    grads = [None] * depth
    early = None
    for l in reversed(range(depth)):
        lw = layer_w[l]
        h_in, u, hn, y, hs = saved[l]
        token = early[-1] if early else None
        d_wout = _out_proj_dw(y, dh, f"out_proj_dw_{l}", after=token)
        if early:
            ssem, rsem, bufs, _ = early
            bufs = _copies_wait(bufs, ssem, rsem, d_wout, _swap_plan, "early_swap_wait")
            half = len(bufs) // 2
            sums = [_pair_add(a, b, c_idx, f"early_pair_add_{k}") for k, (a, b) in enumerate(zip(bufs[:half], bufs[half:]))]
            lands = [lax.empty(p.shape, p.dtype) for p in sums]
            ssem, rsem, *bufs, token = _copies_start(sums + lands, _scatter_plan, 3 * half, "early_scatter_start")
        du, dsm, d_wr, d_wi = _mix_bwd(u, hs, dy, lw["wa"], lw["ba"], lw["wr"], lw["br"], lw["wi"], lw["bi"],
                                       lw["lam"], lw["wb"], f"mix_bwd_{l}", after=token)
        if early:
            bufs = _copies_wait(bufs, ssem, rsem, du, _scatter_plan, "early_scatter_wait")
            halves = [_chip_sum(rc, p, to_core, N_CORES, f"early_chip_sum_{k}")
                      for k, (p, rc) in enumerate(zip(bufs[:half], bufs[half:]))]
            ssem, rsem, *bufs, token = _copies_start(halves, _pair_gather_plan, half, "early_gather_start")
        d_win = _in_proj_dw(hn, du, s, f"in_proj_dw_{l}", after=token)
        srcs = [d_win.reshape(s, 2, d // 2, ns), d_wout.reshape(s, 2, dms // 2, d)]
        if early:
            early_full = _copies_wait(bufs, ssem, rsem, d_win, _pair_gather_plan, "early_gather_wait")
            lands = [lax.empty((a.shape[0],) + a.shape[2:], a.dtype) for a in srcs]
            ssem, rsem, *bufs, token = _copies_start(srcs + lands, _swap_plan, len(srcs), "late_swap_start")
            last = depth - 1
            early_grad = dict(w_in=early_full[0].reshape(d, ns), w_out=early_full[1].reshape(dms, d))
            early_step = {n: _adamw_layer(weights[n], early_grad[n], mom1[n], mom2[n], last, None,
                                          f"adamw_{n}_{last}", after=token) for n in ("w_in", "w_out")}
            bufs = _copies_wait(bufs, ssem, rsem, [o[0] for o in early_step.values()], _swap_plan, "late_swap_wait")
            late_sums = [_pair_add(a, b, c_idx, f"pair_add_{k}")
                         for k, (a, b) in enumerate(zip(bufs[:len(srcs)], bufs[len(srcs):]))]
            lands = [lax.empty(p.shape, p.dtype) for p in late_sums]
            ssem, rsem, *bufs, token = _copies_start(late_sums + lands, _scatter_plan, 3 * len(srcs), "late_scatter_start")
        if l > 0:
            dh, d_g, dy = _in_proj_bwd(du, lw["win"], h_in, lw["g"], dh, f"in_proj_bwd_{l}", after=token,
                                       w_below=layer_w[l - 1]["wout"])
        else:
            grad_x, d_meta, d_g = _in_proj_bwd(du, lw["win"], h_in, lw["g"], dh, f"in_proj_bwd_{l}", after=token,
                                               split=(n_meta, seq))
        if early:
            bufs = _copies_wait(bufs, ssem, rsem, grad_x, _scatter_plan, "late_scatter_wait")
            late_reduced = [_chip_sum(rc, p, to_core, N_CORES, f"chip_sum_{k}")
                            for k, (p, rc) in enumerate(zip(bufs[:len(srcs)], bufs[len(srcs):]))]
        grads[l] = dict(dsm=dsm, wr=_diag_blocks(d_wr, hd), wi=_diag_blocks(d_wi, hd), g=d_g)
        if l == depth - 1:
            lands = [lax.empty((a.shape[0],) + a.shape[2:], a.dtype) for a in srcs]
            ssem, rsem, *bufs, token = _copies_start(srcs + lands, _swap_plan, len(srcs), "early_swap_start")
            early = (ssem, rsem, bufs, token)
        else:
            early = None
    grad_x = grad_x[None]

    sharded = []
    sp = jnp.zeros((sm_rows, s, ds), F32)
    sp = sp.at[0:n_meta].set(d_meta.reshape(n_meta, s, ds))
    for l in range(depth):
        base = n_meta + l * SUBLANES
        dsm = grads[l]["dsm"]
        sp = sp.at[base:base + ka, :, 0:cs].set(dsm[ROW_DWA:ROW_DWA + ka].reshape(ka, s, cs))
        sp = sp.at[base + ka:base + ka + kb, :, 0:cs].set(dsm[ROW_DWB:ROW_DWB + kb].reshape(kb, s, cs))
    sharded.append(jnp.transpose(sp, (1, 0, 2)).reshape(s, 2, sm_rows // 2, ds))
    rep_parts = [jnp.concatenate([grads[l]["g"].reshape(-1) for l in range(depth)]), d_final_g.reshape(-1)]
    for row in (ROW_DBA, ROW_DBR, ROW_DBI, ROW_DLAM):
        rep_parts.append(jnp.concatenate([grads[l]["dsm"][row] for l in range(depth)]))
    rep_parts.append(jnp.concatenate([grads[l]["wr"].reshape(-1) for l in range(depth)]))
    rep_parts.append(jnp.concatenate([grads[l]["wi"].reshape(-1) for l in range(depth)]))
    rep_sizes = [p.shape[0] for p in rep_parts]
    piece = _round_up(-(-sum(rep_sizes) // (s * 2)), 2 * SUBLANES * LANES)
    flat = jnp.concatenate(rep_parts + [jnp.zeros((s * 2 * piece - sum(rep_sizes),), F32)])
    sharded.append(flat.reshape(s, 2, piece // LANES, LANES))

    from_sibling = _pair_swap(sharded, "small_pair_swap")
    pair_sums = [_pair_add(a, b, c_idx, f"small_pair_add_{k}") for k, (a, b) in enumerate(zip(sharded, from_sibling))]
    by_chip = _chip_scatter(pair_sums)
    to_device = jnp.stack([my_chip, 2 * my_chip + my_c])
    reduced_sp = _chip_sum(by_chip[0], pair_sums[0], to_core, N_CORES, "small_chip_sum")
    reduced_rep = _chip_sum(by_chip[1], pair_sums[1], to_device, N_CHIPS * N_CORES, "chip_sum_rep")
    sp_full, rep_all = _final_gather([reduced_sp], reduced_rep)
    ssem, rsem, *bufs, token = _copies_start(late_reduced, _pair_gather_plan, len(late_reduced), "late_gather_start",
                                             after=rep_all)
    g_sp = sp_full.reshape(sm_rows, ds)
    rep_flat = rep_all.reshape(-1)
    rep_out, off = [], 0
    for n in rep_sizes:
        rep_out.append(rep_flat[off:off + n])
        off += n
    grad = dict(
        meta=g_sp[0:n_meta],
        norm_g=rep_out[0].reshape(depth, d),
        conv_a_w=jnp.stack([g_sp[n_meta + l * SUBLANES:n_meta + l * SUBLANES + ka, 0:cs] for l in range(depth)]),
        conv_a_b=rep_out[2].reshape(depth, c),
        lru_wr=rep_out[6].reshape(depth, nh, hd, hd),
        lru_br=rep_out[3].reshape(depth, c),
        lru_wi=rep_out[7].reshape(depth, nh, hd, hd),
        lru_bi=rep_out[4].reshape(depth, c),
        lru_lambda=rep_out[5].reshape(depth, c),
        conv_b_w=jnp.stack([g_sp[n_meta + l * SUBLANES + ka:n_meta + l * SUBLANES + ka + kb, 0:cs]
                            for l in range(depth)]),
        final_g=rep_out[1].reshape(d),
    )

    delta, new_m, new_v = {}, {}, {}
    for n in grad:
        shape = weights[n].shape
        as_block = shape if len(shape) > 1 else (1,) + shape
        out = _adamw(weights[n].reshape(as_block), grad[n].reshape(as_block), mom1[n].reshape(as_block),
                     mom2[n].reshape(as_block), f"adamw_{n}", after=token)
        delta[n], new_m[n], new_v[n] = (o.reshape(shape) for o in out)
    full = _copies_wait(bufs, ssem, rsem, [delta[n] for n in grad], _pair_gather_plan, "late_gather_wait")
    g_win = [full[0].reshape(d, ns), early_full[0].reshape(d, ns)]
    g_wout = [full[1].reshape(dms, d), early_full[1].reshape(dms, d)]
    grad["w_in"] = jnp.stack(g_win)
    grad["w_out"] = jnp.stack(g_wout)
    for n, g_first in (("w_in", g_win[0]), ("w_out", g_wout[0])):
        delta[n], new_m[n], new_v[n] = _adamw_layer(weights[n], g_first, mom1[n], mom2[n], 0, early_step[n],
                                                    f"adamw_{n}_0")

    return (loss, grad_x, *[grad[n] for n in names], *[delta[n] for n in names],
            *[new_m[n] for n in names], *[new_v[n] for n in names])
```
